```python
import jax, jax.numpy as jnp
from jax import lax
import numpy as np

D_MODEL = 1024
BATCH = 8
SEQ = 8192
DEPTH = 2

D_FF = 2816
HG_DK = 128
HG_HEADS = D_MODEL // HG_DK
HG_DV = D_MODEL // HG_HEADS
HG_WK = HG_HEADS * HG_DK
HG_WV = HG_HEADS * HG_DV
HG_CHUNK = 64
ATT_PATTERNS = ((128, 1), (512, 4), (2048, 16))
ATT_GROUPS = 3
ATT_HEADS = 4
ATT_DH = 128
ATT_W = ATT_GROUPS * ATT_HEADS * ATT_DH
ATT_OUT = ATT_HEADS * ATT_DH
ROPE_THETA = 10000.0
EPS = 1e-6
SPLIT_SIZES = (HG_WK, HG_WK, HG_WV, HG_WV, ATT_W, ATT_W, ATT_W, D_MODEL, D_MODEL)
P_IN = sum(SPLIT_SIZES)

kernel_name = "hybrid_hgrn2_dilated_attn_macaron"


def rms(x):
    xf = x.astype(jnp.float32)
    return xf * lax.rsqrt(jnp.mean(xf * xf, axis=-1, keepdims=True) + EPS)


def rmsnorm(x, g):
    return (rms(x) * g.astype(jnp.float32)).astype(x.dtype)


def swiglu(h, w_in, w_out):
    a, b = jnp.split(h @ w_in, 2, axis=-1)
    return (jax.nn.silu(a) * b) @ w_out


def rope_tables(t):
    pos = jnp.arange(t, dtype=jnp.float32)
    inv = ROPE_THETA ** (-jnp.arange(0, ATT_DH, 2, dtype=jnp.float32) / ATT_DH)
    ang = pos[:, None] * inv[None, :]
    ang = jnp.concatenate([ang, ang], axis=-1)
    return jnp.cos(ang), jnp.sin(ang)


def apply_rope(x, cos, sin):
    x1, x2 = jnp.split(x, 2, axis=-1)
    return x * cos + jnp.concatenate([-x2, x1], axis=-1) * sin


def hgrn2_chunk_scan(q, k, v, log_f):
    b, t, h, dk = q.shape
    dv = v.shape[-1]
    n = t // HG_CHUNK

    def chunks(a):
        return a.reshape(b, n, HG_CHUNK, h, a.shape[-1]).transpose(1, 0, 3, 2, 4)

    causal = jnp.tril(jnp.ones((HG_CHUNK, HG_CHUNK), dtype=bool))[:, :, None]

    def step(state, inp):
        qc, kc, vc, gc = inp
        gcum = jnp.cumsum(gc, axis=2)
        diff = gcum[:, :, :, None, :] - gcum[:, :, None, :, :]
        decay = jnp.exp(jnp.where(causal, diff, -jnp.inf))
        attn = jnp.einsum('bhtk,bhsk,bhtsk->bhts', qc, kc, decay)
        o = jnp.einsum('bhts,bhsv->bhtv', attn, vc) + jnp.einsum(
            'bhtk,bhkv->bhtv', qc * jnp.exp(gcum), state)
        g_last = gcum[:, :, -1:, :]
        state = jnp.exp(g_last[:, :, 0, :, None]) * state + jnp.einsum(
            'bhsk,bhsv->bhkv', kc * jnp.exp(g_last - gcum), vc)
        return state, o

    s0 = jnp.zeros((b, h, dk, dv), jnp.float32)
    _, o = lax.scan(step, s0, (chunks(q), chunks(k), chunks(v), chunks(log_f)))
    return o.transpose(1, 0, 3, 2, 4).reshape(b, t, h * dv)


def dilated_window_attention(q, k, v, window, dilation):
    b, h, t, dh = q.shape
    back = window // dilation
    blk = back
    L = t // dilation
    nb = -(-L // blk)
    Lp = nb * blk

    def to_res(a):
        a = a.reshape(b, h, L, dilation, dh).transpose(0, 1, 3, 2, 4)
        return jnp.pad(a, ((0, 0), (0, 0), (0, 0), (0, Lp - L), (0, 0)))

    def kv_blocks(a):
        a = jnp.pad(a, ((0, 0), (0, 0), (0, 0), (blk, 0), (0, 0)))
        a = a.reshape(b, h, dilation, nb + 1, blk, dh)
        return jnp.concatenate([a[:, :, :, :-1], a[:, :, :, 1:]], axis=4)

    qb = to_res(q).reshape(b, h, dilation, nb, blk, dh)
    kb = kv_blocks(to_res(k))
    vb = kv_blocks(to_res(v))
    s = jnp.einsum('bhrnqd,bhrnkd->bhrnqk', qb, kb) * (dh ** -0.5)
    qi = jnp.arange(blk)[:, None]
    ki = jnp.arange(2 * blk)[None, :]
    band = (ki >= qi) & (ki <= qi + back)
    valid = (ki >= blk)[None] | (jnp.arange(nb)[:, None, None] > 0)
    mask = band[None] & valid
    s = jnp.where(mask, s, -jnp.inf)
    lse = jax.nn.logsumexp(s, axis=-1)
    p = jnp.exp(s - lse[..., None])
    o = jnp.einsum('bhrnqk,bhrnkd->bhrnqd', p, vb)
    o = o.reshape(b, h, dilation, Lp, dh)[:, :, :, :L].transpose(0, 1, 3, 2, 4)
    lse = lse.reshape(b, h, dilation, Lp)[:, :, :, :L].transpose(0, 1, 3, 2)
    return o.reshape(b, h, t, dh), lse.reshape(b, h, t)


def _fwd_setup_inputs(seed: int = 0) -> dict:
    key = jax.random.key(seed)
    ks = jax.random.split(key, 16)
    f32 = jnp.float32

    def nrm(k, shape, fan_in):
        return jax.random.normal(k, shape, f32) * (fan_in ** -0.5)

    def gain(k, shape):
        return 1.0 + 0.05 * jax.random.normal(k, shape, f32)

    return {
        "x": jax.random.normal(ks[0], (BATCH, SEQ, D_MODEL), f32),
        "ffn1_norm": gain(ks[1], (DEPTH, D_MODEL)),
        "ffn1_w_in": nrm(ks[2], (DEPTH, D_MODEL, 2 * D_FF), D_MODEL),
        "ffn1_w_out": nrm(ks[3], (DEPTH, D_FF, D_MODEL), D_FF),
        "mix_norm": gain(ks[4], (DEPTH, D_MODEL)),
        "w_in": nrm(ks[5], (DEPTH, D_MODEL, P_IN), D_MODEL),
        "hgrn_lb_logits": 0.5 * jax.random.normal(ks[6], (DEPTH, HG_WK), f32),
        "hgrn_out_norm": gain(ks[7], (DEPTH, HG_WV)),
        "attn_q_norm": gain(ks[8], (DEPTH, ATT_GROUPS, ATT_DH)),
        "attn_k_norm": gain(ks[9], (DEPTH, ATT_GROUPS, ATT_DH)),
        "w_branch_a": nrm(ks[10], (DEPTH, HG_WV, D_MODEL), HG_WV),
        "w_branch_b": nrm(ks[11], (DEPTH, ATT_OUT, D_MODEL), ATT_OUT),
        "w_out": nrm(ks[12], (DEPTH, D_MODEL, D_MODEL), D_MODEL),
        "ffn2_norm": gain(ks[13], (DEPTH, D_MODEL)),
        "ffn2_w_in": nrm(ks[14], (DEPTH, D_MODEL, 2 * D_FF), D_MODEL),
        "ffn2_w_out": nrm(ks[15], (DEPTH, D_FF, D_MODEL), D_FF),
    }


def _fwd_reference(x, ffn1_norm, ffn1_w_in, ffn1_w_out, mix_norm, w_in, hgrn_lb_logits,
              hgrn_out_norm, attn_q_norm, attn_k_norm, w_branch_a, w_branch_b, w_out,
              ffn2_norm, ffn2_w_in, ffn2_w_out):
    b, t, _ = x.shape
    f32 = jnp.float32
    cos, sin = rope_tables(t)
    lb_all = jnp.cumsum(jax.nn.softmax(hgrn_lb_logits.astype(f32), axis=0), axis=0)
    lb_all = lb_all - lb_all[0:1]
    split_idx = [int(s) for s in np.cumsum(SPLIT_SIZES)[:-1]]

    for l in range(DEPTH):
        x = x + 0.5 * swiglu(rmsnorm(x, ffn1_norm[l]), ffn1_w_in[l], ffn1_w_out[l])

        h = rmsnorm(x, mix_norm[l])
        hq, hf, hi, hg, aq, ak, av, ga, gb = jnp.split(h @ w_in[l], split_idx, axis=-1)

        lb = lb_all[l]
        f = lb + (1.0 - lb) * jax.nn.sigmoid(hf.astype(f32))
        q_a = jax.nn.silu(hq.astype(f32)).reshape(b, t, HG_HEADS, HG_DK)
        k_a = (1.0 - f).reshape(b, t, HG_HEADS, HG_DK)
        v_a = hi.astype(f32).reshape(b, t, HG_HEADS, HG_DV)
        log_f = jnp.log(f).reshape(b, t, HG_HEADS, HG_DK)
        o_a = hgrn2_chunk_scan(q_a, k_a, v_a, log_f)
        o_a = rms(o_a.reshape(b, t, HG_HEADS, HG_DV)).reshape(b, t, HG_WV)
        o_a = o_a * hgrn_out_norm[l].astype(f32) * jax.nn.silu(hg.astype(f32))
        y_a = o_a.astype(x.dtype) @ w_branch_a[l]

        def heads(a):
            return a.reshape(b, t, ATT_GROUPS, ATT_HEADS, ATT_DH).transpose(2, 0, 3, 1, 4).astype(f32)

        qn = attn_q_norm[l].astype(f32)[:, None, None, None, :]
        kn = attn_k_norm[l].astype(f32)[:, None, None, None, :]
        q_b = apply_rope(rms(heads(aq)) * qn, cos, sin)
        k_b = apply_rope(rms(heads(ak)) * kn, cos, sin)
        v_b = heads(av)
        outs, lses = [], []
        for g, (window, dilation) in enumerate(ATT_PATTERNS):
            o_g, lse_g = dilated_window_attention(q_b[g], k_b[g], v_b[g], window, dilation)
            outs.append(o_g)
            lses.append(lse_g)
        alpha = jax.nn.softmax(jnp.stack(lses, axis=0), axis=0)
        o_b = jnp.einsum('gbht,gbhtd->bthd', alpha, jnp.stack(outs, axis=0)).reshape(b, t, ATT_OUT)
        y_b = o_b.astype(x.dtype) @ w_branch_b[l]

        merged = jax.nn.sigmoid(ga) * y_a + jax.nn.sigmoid(gb) * y_b
        x = x + merged @ w_out[l]

        x = x + 0.5 * swiglu(rmsnorm(x, ffn2_norm[l]), ffn2_w_in[l], ffn2_w_out[l])
    return x


import jax as _jax
import jax.numpy as _jnp

TWIN_FORMAT = 'train_step'
FWD_PARAMS = ['x', 'ffn1_norm', 'ffn1_w_in', 'ffn1_w_out', 'mix_norm', 'w_in', 'hgrn_lb_logits', 'hgrn_out_norm', 'attn_q_norm', 'attn_k_norm', 'w_branch_a', 'w_branch_b', 'w_out', 'ffn2_norm', 'ffn2_w_in', 'ffn2_w_out']
TWIN_WEIGHTS = ['ffn1_norm', 'ffn1_w_in', 'ffn1_w_out', 'mix_norm', 'w_in', 'hgrn_lb_logits', 'hgrn_out_norm', 'attn_q_norm', 'attn_k_norm', 'w_branch_a', 'w_branch_b', 'w_out', 'ffn2_norm', 'ffn2_w_in', 'ffn2_w_out']
TWIN_DIFF_INPUT = 'x'
TWIN_INPUTS = ['x', 'ffn1_norm', 'ffn1_w_in', 'ffn1_w_out', 'mix_norm', 'w_in', 'hgrn_lb_logits', 'hgrn_out_norm', 'attn_q_norm', 'attn_k_norm', 'w_branch_a', 'w_branch_b', 'w_out', 'ffn2_norm', 'ffn2_w_in', 'ffn2_w_out', 'loss_target', 'm_ffn1_norm', 'm_ffn1_w_in', 'm_ffn1_w_out', 'm_mix_norm', 'm_w_in', 'm_hgrn_lb_logits', 'm_hgrn_out_norm', 'm_attn_q_norm', 'm_attn_k_norm', 'm_w_branch_a', 'm_w_branch_b', 'm_w_out', 'm_ffn2_norm', 'm_ffn2_w_in', 'm_ffn2_w_out', 'v_ffn1_norm', 'v_ffn1_w_in', 'v_ffn1_w_out', 'v_mix_norm', 'v_w_in', 'v_hgrn_lb_logits', 'v_hgrn_out_norm', 'v_attn_q_norm', 'v_attn_k_norm', 'v_w_branch_a', 'v_w_branch_b', 'v_w_out', 'v_ffn2_norm', 'v_ffn2_w_in', 'v_ffn2_w_out']
TWIN_OUTPUTS = ['loss', 'grad_x', 'grad_ffn1_norm', 'grad_ffn1_w_in', 'grad_ffn1_w_out', 'grad_mix_norm', 'grad_w_in', 'grad_hgrn_lb_logits', 'grad_hgrn_out_norm', 'grad_attn_q_norm', 'grad_attn_k_norm', 'grad_w_branch_a', 'grad_w_branch_b', 'grad_w_out', 'grad_ffn2_norm', 'grad_ffn2_w_in', 'grad_ffn2_w_out', 'delta_ffn1_norm', 'delta_ffn1_w_in', 'delta_ffn1_w_out', 'delta_mix_norm', 'delta_w_in', 'delta_hgrn_lb_logits', 'delta_hgrn_out_norm', 'delta_attn_q_norm', 'delta_attn_k_norm', 'delta_w_branch_a', 'delta_w_branch_b', 'delta_w_out', 'delta_ffn2_norm', 'delta_ffn2_w_in', 'delta_ffn2_w_out', 'new_m_ffn1_norm', 'new_m_ffn1_w_in', 'new_m_ffn1_w_out', 'new_m_mix_norm', 'new_m_w_in', 'new_m_hgrn_lb_logits', 'new_m_hgrn_out_norm', 'new_m_attn_q_norm', 'new_m_attn_k_norm', 'new_m_w_branch_a', 'new_m_w_branch_b', 'new_m_w_out', 'new_m_ffn2_norm', 'new_m_ffn2_w_in', 'new_m_ffn2_w_out', 'new_v_ffn1_norm', 'new_v_ffn1_w_in', 'new_v_ffn1_w_out', 'new_v_mix_norm', 'new_v_w_in', 'new_v_hgrn_lb_logits', 'new_v_hgrn_out_norm', 'new_v_attn_q_norm', 'new_v_attn_k_norm', 'new_v_w_branch_a', 'new_v_w_branch_b', 'new_v_w_out', 'new_v_ffn2_norm', 'new_v_ffn2_w_in', 'new_v_ffn2_w_out']
TWIN_LEAF_KINDS = {'loss': 'loss', 'grad_x': 'grad_x', 'grad_ffn1_norm': 'grad_w', 'grad_ffn1_w_in': 'grad_w', 'grad_ffn1_w_out': 'grad_w', 'grad_mix_norm': 'grad_w', 'grad_w_in': 'grad_w', 'grad_hgrn_lb_logits': 'grad_w', 'grad_hgrn_out_norm': 'grad_w', 'grad_attn_q_norm': 'grad_w', 'grad_attn_k_norm': 'grad_w', 'grad_w_branch_a': 'grad_w', 'grad_w_branch_b': 'grad_w', 'grad_w_out': 'grad_w', 'grad_ffn2_norm': 'grad_w', 'grad_ffn2_w_in': 'grad_w', 'grad_ffn2_w_out': 'grad_w', 'delta_ffn1_norm': 'delta_w', 'delta_ffn1_w_in': 'delta_w', 'delta_ffn1_w_out': 'delta_w', 'delta_mix_norm': 'delta_w', 'delta_w_in': 'delta_w', 'delta_hgrn_lb_logits': 'delta_w', 'delta_hgrn_out_norm': 'delta_w', 'delta_attn_q_norm': 'delta_w', 'delta_attn_k_norm': 'delta_w', 'delta_w_branch_a': 'delta_w', 'delta_w_branch_b': 'delta_w', 'delta_w_out': 'delta_w', 'delta_ffn2_norm': 'delta_w', 'delta_ffn2_w_in': 'delta_w', 'delta_ffn2_w_out': 'delta_w', 'new_m_ffn1_norm': 'new_m', 'new_m_ffn1_w_in': 'new_m', 'new_m_ffn1_w_out': 'new_m', 'new_m_mix_norm': 'new_m', 'new_m_w_in': 'new_m', 'new_m_hgrn_lb_logits': 'new_m', 'new_m_hgrn_out_norm': 'new_m', 'new_m_attn_q_norm': 'new_m', 'new_m_attn_k_norm': 'new_m', 'new_m_w_branch_a': 'new_m', 'new_m_w_branch_b': 'new_m', 'new_m_w_out': 'new_m', 'new_m_ffn2_norm': 'new_m', 'new_m_ffn2_w_in': 'new_m', 'new_m_ffn2_w_out': 'new_m', 'new_v_ffn1_norm': 'new_v', 'new_v_ffn1_w_in': 'new_v', 'new_v_ffn1_w_out': 'new_v', 'new_v_mix_norm': 'new_v', 'new_v_w_in': 'new_v', 'new_v_hgrn_lb_logits': 'new_v', 'new_v_hgrn_out_norm': 'new_v', 'new_v_attn_q_norm': 'new_v', 'new_v_attn_k_norm': 'new_v', 'new_v_w_branch_a': 'new_v', 'new_v_w_branch_b': 'new_v', 'new_v_w_out': 'new_v', 'new_v_ffn2_norm': 'new_v', 'new_v_ffn2_w_in': 'new_v', 'new_v_ffn2_w_out': 'new_v'}


def _forward(args):
    return _fwd_reference(*[args[k] for k in FWD_PARAMS])


def _output_shape():
    def fwd():
        inp = _fwd_setup_inputs(0)
        return _fwd_reference(*[inp[k] for k in FWD_PARAMS])
    out = _jax.eval_shape(fwd)
    return out.shape, out.dtype

N_MICROBATCH = 1
ADAM_LR = 0.001
ADAM_B1 = 0.9
ADAM_B2 = 0.999
ADAM_EPS = 1e-08
ADAM_WD = 0.01
ADAM_STEP = 10
PER_EXAMPLE_BATCH_AXIS = {'x': 0, 'loss_target': 0}
SHARED_INPUTS = []
_WEIGHT_DTYPES = {'ffn1_norm': _jnp.float32, 'ffn1_w_in': _jnp.float32, 'ffn1_w_out': _jnp.float32, 'mix_norm': _jnp.float32, 'w_in': _jnp.float32, 'hgrn_lb_logits': _jnp.float32, 'hgrn_out_norm': _jnp.float32, 'attn_q_norm': _jnp.float32, 'attn_k_norm': _jnp.float32, 'w_branch_a': _jnp.float32, 'w_branch_b': _jnp.float32, 'w_out': _jnp.float32, 'ffn2_norm': _jnp.float32, 'ffn2_w_in': _jnp.float32, 'ffn2_w_out': _jnp.float32}
MOMENT_SCALE = {'ffn1_norm': 1.232333e+01, 'ffn1_w_in': 1.237197e-01, 'ffn1_w_out': 2.191332e-01, 'mix_norm': 8.202666e+00, 'w_in': 9.542174e-02, 'hgrn_lb_logits': 1.387749e-02, 'hgrn_out_norm': 6.624222e+00, 'attn_q_norm': 2.760677e-01, 'attn_k_norm': 2.764450e-01, 'w_branch_a': 2.535229e-01, 'w_branch_b': 5.914418e-02, 'w_out': 2.465841e-01, 'ffn2_norm': 1.249094e+01, 'ffn2_w_in': 1.140470e-01, 'ffn2_w_out': 2.098778e-01}


def _to_microbatches(a, axis):
    t = _jnp.moveaxis(a, axis, 0)
    t = t.reshape((N_MICROBATCH, t.shape[0] // N_MICROBATCH) + t.shape[1:])
    return _jnp.moveaxis(t, 1, axis + 1)


def setup_inputs(seed: int = 0) -> dict:
    inp = _fwd_setup_inputs(seed)
    key = _jax.random.fold_in(_jax.random.key(seed), 7919)
    shape, _ = _output_shape()
    out = dict(inp)
    out["loss_target"] = _jax.random.normal(_jax.random.fold_in(key, 0), shape, _jnp.float32)
    for i, name in enumerate(TWIN_WEIGHTS):
        w = inp[name].astype(_jnp.float32)
        if MOMENT_SCALE is None:
            s = _jnp.sqrt(_jnp.mean(_jnp.square(w)) + 1e-30)
        else:
            s = MOMENT_SCALE[name]
        km, kv = _jax.random.split(_jax.random.fold_in(key, i + 1))
        out[name] = w
        out["m_" + name] = s * _jax.random.normal(km, w.shape, _jnp.float32)
        out["v_" + name] = (s * s) * _jax.random.uniform(kv, w.shape, _jnp.float32, 0.5, 1.5)
    if N_MICROBATCH > 1:
        for name, axis in PER_EXAMPLE_BATCH_AXIS.items():
            out[name] = _to_microbatches(out[name], axis)
    return {'x': out['x'], 'ffn1_norm': out['ffn1_norm'], 'ffn1_w_in': out['ffn1_w_in'], 'ffn1_w_out': out['ffn1_w_out'], 'mix_norm': out['mix_norm'], 'w_in': out['w_in'], 'hgrn_lb_logits': out['hgrn_lb_logits'], 'hgrn_out_norm': out['hgrn_out_norm'], 'attn_q_norm': out['attn_q_norm'], 'attn_k_norm': out['attn_k_norm'], 'w_branch_a': out['w_branch_a'], 'w_branch_b': out['w_branch_b'], 'w_out': out['w_out'], 'ffn2_norm': out['ffn2_norm'], 'ffn2_w_in': out['ffn2_w_in'], 'ffn2_w_out': out['ffn2_w_out'], 'loss_target': out['loss_target'], 'm_ffn1_norm': out['m_ffn1_norm'], 'm_ffn1_w_in': out['m_ffn1_w_in'], 'm_ffn1_w_out': out['m_ffn1_w_out'], 'm_mix_norm': out['m_mix_norm'], 'm_w_in': out['m_w_in'], 'm_hgrn_lb_logits': out['m_hgrn_lb_logits'], 'm_hgrn_out_norm': out['m_hgrn_out_norm'], 'm_attn_q_norm': out['m_attn_q_norm'], 'm_attn_k_norm': out['m_attn_k_norm'], 'm_w_branch_a': out['m_w_branch_a'], 'm_w_branch_b': out['m_w_branch_b'], 'm_w_out': out['m_w_out'], 'm_ffn2_norm': out['m_ffn2_norm'], 'm_ffn2_w_in': out['m_ffn2_w_in'], 'm_ffn2_w_out': out['m_ffn2_w_out'], 'v_ffn1_norm': out['v_ffn1_norm'], 'v_ffn1_w_in': out['v_ffn1_w_in'], 'v_ffn1_w_out': out['v_ffn1_w_out'], 'v_mix_norm': out['v_mix_norm'], 'v_w_in': out['v_w_in'], 'v_hgrn_lb_logits': out['v_hgrn_lb_logits'], 'v_hgrn_out_norm': out['v_hgrn_out_norm'], 'v_attn_q_norm': out['v_attn_q_norm'], 'v_attn_k_norm': out['v_attn_k_norm'], 'v_w_branch_a': out['v_w_branch_a'], 'v_w_branch_b': out['v_w_branch_b'], 'v_w_out': out['v_w_out'], 'v_ffn2_norm': out['v_ffn2_norm'], 'v_ffn2_w_in': out['v_ffn2_w_in'], 'v_ffn2_w_out': out['v_ffn2_w_out']}


def _loss(weights, diff, rest, loss_target):
    with _jax.named_scope("forward"):
        args = {**rest, TWIN_DIFF_INPUT: diff, **{k: w.astype(_WEIGHT_DTYPES[k]) for k, w in weights.items()}}
        y = _forward(args)
    with _jax.named_scope("loss_head"):
        err = _jnp.square(y.astype(_jnp.float32) - loss_target)
        return 0.5 * _jnp.sum(_jnp.mean(err, axis=-1)) if err.ndim else 0.5 * err


def _adamw(w, g, m, v):
    m = ADAM_B1 * m + (1.0 - ADAM_B1) * g
    v = ADAM_B2 * v + (1.0 - ADAM_B2) * _jnp.square(g)
    m_hat = m / (1.0 - ADAM_B1 ** ADAM_STEP)
    v_hat = v / (1.0 - ADAM_B2 ** ADAM_STEP)
    delta = -ADAM_LR * (m_hat / (_jnp.sqrt(v_hat) + ADAM_EPS) + ADAM_WD * w)
    return delta, m, v


def reference(x, ffn1_norm, ffn1_w_in, ffn1_w_out, mix_norm, w_in, hgrn_lb_logits, hgrn_out_norm, attn_q_norm, attn_k_norm, w_branch_a, w_branch_b, w_out, ffn2_norm, ffn2_w_in, ffn2_w_out, loss_target, m_ffn1_norm, m_ffn1_w_in, m_ffn1_w_out, m_mix_norm, m_w_in, m_hgrn_lb_logits, m_hgrn_out_norm, m_attn_q_norm, m_attn_k_norm, m_w_branch_a, m_w_branch_b, m_w_out, m_ffn2_norm, m_ffn2_w_in, m_ffn2_w_out, v_ffn1_norm, v_ffn1_w_in, v_ffn1_w_out, v_mix_norm, v_w_in, v_hgrn_lb_logits, v_hgrn_out_norm, v_attn_q_norm, v_attn_k_norm, v_w_branch_a, v_w_branch_b, v_w_out, v_ffn2_norm, v_ffn2_w_in, v_ffn2_w_out):
    given = dict(x=x, ffn1_norm=ffn1_norm, ffn1_w_in=ffn1_w_in, ffn1_w_out=ffn1_w_out, mix_norm=mix_norm, w_in=w_in, hgrn_lb_logits=hgrn_lb_logits, hgrn_out_norm=hgrn_out_norm, attn_q_norm=attn_q_norm, attn_k_norm=attn_k_norm, w_branch_a=w_branch_a, w_branch_b=w_branch_b, w_out=w_out, ffn2_norm=ffn2_norm, ffn2_w_in=ffn2_w_in, ffn2_w_out=ffn2_w_out, loss_target=loss_target, m_ffn1_norm=m_ffn1_norm, m_ffn1_w_in=m_ffn1_w_in, m_ffn1_w_out=m_ffn1_w_out, m_mix_norm=m_mix_norm, m_w_in=m_w_in, m_hgrn_lb_logits=m_hgrn_lb_logits, m_hgrn_out_norm=m_hgrn_out_norm, m_attn_q_norm=m_attn_q_norm, m_attn_k_norm=m_attn_k_norm, m_w_branch_a=m_w_branch_a, m_w_branch_b=m_w_branch_b, m_w_out=m_w_out, m_ffn2_norm=m_ffn2_norm, m_ffn2_w_in=m_ffn2_w_in, m_ffn2_w_out=m_ffn2_w_out, v_ffn1_norm=v_ffn1_norm, v_ffn1_w_in=v_ffn1_w_in, v_ffn1_w_out=v_ffn1_w_out, v_mix_norm=v_mix_norm, v_w_in=v_w_in, v_hgrn_lb_logits=v_hgrn_lb_logits, v_hgrn_out_norm=v_hgrn_out_norm, v_attn_q_norm=v_attn_q_norm, v_attn_k_norm=v_attn_k_norm, v_w_branch_a=v_w_branch_a, v_w_branch_b=v_w_branch_b, v_w_out=v_w_out, v_ffn2_norm=v_ffn2_norm, v_ffn2_w_in=v_ffn2_w_in, v_ffn2_w_out=v_ffn2_w_out)
    weights = {n: given[n] for n in TWIN_WEIGHTS}
    shared = {n: given[n] for n in SHARED_INPUTS}
    per_example = {n: given[n] for n in ['x']}
    grad_fn = _jax.value_and_grad(_loss, argnums=(0, 1))

    def one_microbatch(ex, loss_target):
        ex = dict(ex)
        diff = ex.pop(TWIN_DIFF_INPUT)
        return grad_fn(weights, diff, {**shared, **ex}, loss_target)

    if N_MICROBATCH == 1:
        loss, (grad_w, grad_x) = one_microbatch(per_example, given["loss_target"])
    else:
        def body(carry, xs):
            loss_sum, grad_sum = carry
            l_k, (gw_k, gx_k) = one_microbatch(xs[0], xs[1])
            with _jax.named_scope("update"):
                return (loss_sum + l_k, _jax.tree.map(_jnp.add, grad_sum, gw_k)), gx_k

        init = (_jnp.zeros((), _jnp.float32), _jax.tree.map(_jnp.zeros_like, weights))
        (loss, grad_w), grad_x = _jax.lax.scan(body, init, (per_example, given["loss_target"]))
    with _jax.named_scope("update"):
        delta_w, new_m, new_v = {}, {}, {}
        for n in TWIN_WEIGHTS:
            delta_w[n], new_m[n], new_v[n] = _adamw(weights[n], grad_w[n], given["m_" + n], given["v_" + n])
    return (loss, grad_x, *[grad_w[n] for n in TWIN_WEIGHTS], *[delta_w[n] for n in TWIN_WEIGHTS],
            *[new_m[n] for n in TWIN_WEIGHTS], *[new_v[n] for n in TWIN_WEIGHTS])
```

```python
import functools

import jax
import jax.numpy as jnp
from jax import lax
from jax.experimental import pallas as pl
from jax.experimental.pallas import tpu as pltpu

F32 = jnp.float32
BF16 = jnp.bfloat16
MESH = pl.DeviceIdType.MESH

D_MODEL = 1024
D_FF = 2816
N_CHIPS = 4
HEAD = 128
HG_HEADS = 8
HG_CHUNK = 64
ATT_GROUPS = 3
ATT_HEADS = 4
ATT_GW = ATT_HEADS * HEAD
DILATIONS = (1, 4, 16)
ATT_BLK = 128
ATT_STEP_BLOCKS = 4
P_IN = 10752
CB_AQ, CB_AK, CB_AV, CB_GA, CB_GB = 8, 11, 14, 17, 19
EPS = 1e-6
ROPE_THETA = 10000.0
ADAM_LR, ADAM_B1, ADAM_B2, ADAM_EPS, ADAM_WD, ADAM_STEP = 0.001, 0.9, 0.999, 1e-08, 0.01, 10
VMEM_LIMIT_V7X = 56 * 1024 * 1024
NEG = -1e30


def _params(sem):
    return pltpu.CompilerParams(dimension_semantics=sem, vmem_limit_bytes=VMEM_LIMIT_V7X)


def _sig(x):
    return 1.0 / (1.0 + jnp.exp(-x))


def _dot(a, b):
    return jnp.dot(a, b, preferred_element_type=F32)


def _dot_nt(a, b):
    return lax.dot_general(a, b, (((1,), (1,)), ((), ())), preferred_element_type=F32)


def _dot_tn(a, b):
    return lax.dot_general(a, b, (((0,), (0,)), ((), ())), preferred_element_type=F32)


def _bf(x):
    return x.astype(BF16)


def _mm_nn(a, b3, *, name, tm, tn, out_dtype, res=None, alpha=1.0):
    m, k = a.shape
    nb, _, nw = b3.shape
    per = nw // tn
    assert nw % tn == 0 and m % tm == 0
    has_res = res is not None

    def body(*refs):
        if has_res:
            a_ref, b_ref, r_ref, o_ref = refs
        else:
            a_ref, b_ref, o_ref = refs
        acc = _dot(_bf(a_ref[...]), b_ref[...])
        if alpha != 1.0:
            acc = alpha * acc
        if has_res:
            acc = r_ref[...] + acc
        o_ref[...] = acc.astype(o_ref.dtype)

    in_specs = [pl.BlockSpec((tm, k), lambda i, j: (i, 0)),
                pl.BlockSpec((None, k, tn), lambda i, j: (j // per, 0, j % per))]
    args = [a, b3]
    if has_res:
        in_specs.append(pl.BlockSpec((tm, tn), lambda i, j: (i, j)))
        args.append(res)
    return pl.pallas_call(
        body, grid=(m // tm, nb * per), in_specs=in_specs,
        out_specs=pl.BlockSpec((tm, tn), lambda i, j: (i, j)),
        out_shape=jax.ShapeDtypeStruct((m, nb * nw), out_dtype),
        name=name, compiler_params=_params(("parallel", "arbitrary")))(*args)


def _mm_nt(d, b3, *, name, tm, tp, tn, out_dtype, alpha=1.0):
    m, n = d.shape
    nb, p, nw = b3.shape
    per = nw // tn
    nk = n // tn
    assert nb * nw == n and nw % tn == 0 and p % tp == 0 and m % tm == 0

    def body(d_ref, b_ref, o_ref, acc_ref):
        kk = pl.program_id(2)

        @pl.when(kk == 0)
        def _():
            acc_ref[...] = jnp.zeros_like(acc_ref)

        acc_ref[...] += _dot_nt(_bf(d_ref[...]), b_ref[...])

        @pl.when(kk == nk - 1)
        def _():
            o_ref[...] = (alpha * acc_ref[...]).astype(o_ref.dtype)

    return pl.pallas_call(
        body, grid=(m // tm, p // tp, nk),
        in_specs=[pl.BlockSpec((tm, tn), lambda i, j, kk: (i, kk)),
                  pl.BlockSpec((None, tp, tn), lambda i, j, kk: (kk // per, j, kk % per))],
        out_specs=pl.BlockSpec((tm, tp), lambda i, j, kk: (i, j)),
        out_shape=jax.ShapeDtypeStruct((m, p), out_dtype),
        scratch_shapes=[pltpu.VMEM((tm, tp), F32)],
        name=name, compiler_params=_params(("parallel", "parallel", "arbitrary")))(d, b3)


def _mm_tn(a, d, *, nb, name, tm, tk, tn, alpha=1.0):
    m, k = a.shape
    _, n = d.shape
    nw = n // nb
    per = nw // tn
    nm = m // tm
    assert nw % tn == 0 and k % tk == 0 and m % tm == 0

    def body(a_ref, d_ref, o_ref, acc_ref):
        mm = pl.program_id(2)

        @pl.when(mm == 0)
        def _():
            acc_ref[...] = jnp.zeros_like(acc_ref)

        acc_ref[...] += _dot_tn(_bf(a_ref[...]), _bf(d_ref[...]))

        @pl.when(mm == nm - 1)
        def _():
            o_ref[...] = (alpha * acc_ref[...]).astype(o_ref.dtype)

    return pl.pallas_call(
        body, grid=(k // tk, nb * per, nm),
        in_specs=[pl.BlockSpec((tm, tk), lambda i, j, mm: (mm, i)),
                  pl.BlockSpec((tm, tn), lambda i, j, mm: (mm, j))],
        out_specs=pl.BlockSpec((None, tk, tn), lambda i, j, mm: (j // per, i, j % per)),
        out_shape=jax.ShapeDtypeStruct((nb, k, nw), BF16),
        scratch_shapes=[pltpu.VMEM((tk, tn), F32)],
        name=name, compiler_params=_params(("parallel", "parallel", "arbitrary")))(a, d)


def _ew(fn, ins, outs, *, rows, tm, name):
    in_specs, args = [], []
    for s in ins:
        if s[0] == 't':
            _, arr, w, cb = s
            in_specs.append(pl.BlockSpec((tm, w), lambda i, cb=cb: (i, cb)))
        else:
            arr = s[1]
            in_specs.append(pl.BlockSpec(arr.shape, lambda i, nd=arr.ndim: (0,) * nd))
        args.append(arr)
    out_specs, out_shape = [], []
    for s in outs:
        if s[0] == 't':
            _, w, dt = s
            out_specs.append(pl.BlockSpec((tm, w), lambda i: (i, 0)))
            out_shape.append(jax.ShapeDtypeStruct((rows, w), dt))
        else:
            out_specs.append(pl.BlockSpec(s[1], lambda i: (0, 0)))
            out_shape.append(jax.ShapeDtypeStruct(s[1], F32))
    n_in = len(ins)

    def body(*refs):
        res = fn(*[r[...] for r in refs[:n_in]])
        if not isinstance(res, (tuple, list)):
            res = (res,)
        for r, s, v in zip(refs[n_in:], outs, res):
            if s[0] == 't':
                r[...] = v.astype(r.dtype)
            else:
                @pl.when(pl.program_id(0) == 0)
                def _(r=r):
                    r[...] = jnp.zeros_like(r)

                r[...] += v

    res = pl.pallas_call(
        body, grid=(rows // tm,), in_specs=in_specs, out_specs=out_specs, out_shape=out_shape,
        name=name, compiler_params=_params(("arbitrary",)))(*args)
    return res


def _heads(x):
    return [x[:, h * HEAD:(h + 1) * HEAD] for h in range(x.shape[1] // HEAD)]


def _cat(xs):
    return jnp.concatenate(xs, axis=1)


def _head_mean(x):
    return _cat([jnp.broadcast_to(jnp.mean(h, axis=1, keepdims=True), h.shape) for h in _heads(x)])


def _rms_rows(x):
    return lax.rsqrt(jnp.mean(x * x, axis=1, keepdims=True) + EPS)


def _norm_fwd(x, g, name):
    return _ew(lambda xv, gv: xv * _rms_rows(xv) * gv,
               [('t', x, D_MODEL, 0), ('f', g)], [('t', D_MODEL, BF16)], rows=x.shape[0], tm=512, name=name)[0]


def _norm_bwd(dh, x, g, dx, name):
    def fn(dhv, xv, gv, dxv):
        r = _rms_rows(xv)
        xh = xv * r
        dxh = dhv * gv
        out = dxv + r * (dxh - xh * jnp.mean(dxh * xh, axis=1, keepdims=True))
        return out, jnp.sum(dhv * xh, axis=0, keepdims=True)

    return _ew(fn, [('t', dh, D_MODEL, 0), ('t', x, D_MODEL, 0), ('f', g), ('t', dx, D_MODEL, 0)],
               [('t', D_MODEL, F32), ('acc', (1, D_MODEL))], rows=x.shape[0], tm=512, name=name)


def _swiglu_fwd(ab, name):
    return _ew(lambda a, b: a * _sig(a) * b, [('t', ab, D_FF, 0), ('t', ab, D_FF, 1)], [('t', D_FF, BF16)],
               rows=ab.shape[0], tm=256, name=name)[0]


def _swiglu_bwd(ab, du, name):
    def fn(a, b, duv):
        s = _sig(a)
        return _cat([duv * b * (s * (1.0 + a * (1.0 - s))), duv * a * s])

    return _ew(fn, [('t', ab, D_FF, 0), ('t', ab, D_FF, 1), ('t', du, D_FF, 0)], [('t', 2 * D_FF, BF16)],
               rows=ab.shape[0], tm=256, name=name)[0]


def _loss_fwd_bwd(y, target, name):
    def fn(yv, tv):
        e = yv - tv
        return e * (1.0 / D_MODEL), jnp.sum(e * e, axis=0, keepdims=True)

    return _ew(fn, [('t', y, D_MODEL, 0), ('t', target, D_MODEL, 0)], [('t', D_MODEL, F32), ('acc', (1, D_MODEL))],
               rows=y.shape[0], tm=512, name=name)


def _gate_fwd(proj, ya, yb, name):
    def fn(ga0, ga1, gb0, gb1, yav, ybv):
        return _sig(_cat([ga0, ga1])) * yav + _sig(_cat([gb0, gb1])) * ybv

    ins = [('t', proj, 512, CB_GA), ('t', proj, 512, CB_GA + 1), ('t', proj, 512, CB_GB), ('t', proj, 512, CB_GB + 1),
           ('t', ya, D_MODEL, 0), ('t', yb, D_MODEL, 0)]
    return _ew(fn, ins, [('t', D_MODEL, BF16)], rows=ya.shape[0], tm=512, name=name)[0]


def _gate_bwd(dm, proj, ya, yb, name):
    def fn(dmv, ga0, ga1, gb0, gb1, yav, ybv):
        sa = _sig(_cat([ga0, ga1]))
        sb = _sig(_cat([gb0, gb1]))
        return dmv * sa, dmv * sb, _cat([dmv * yav * sa * (1.0 - sa), dmv * ybv * sb * (1.0 - sb)])

    ins = [('t', dm, D_MODEL, 0),
           ('t', proj, 512, CB_GA), ('t', proj, 512, CB_GA + 1), ('t', proj, 512, CB_GB), ('t', proj, 512, CB_GB + 1),
           ('t', ya, D_MODEL, 0), ('t', yb, D_MODEL, 0)]
    return _ew(fn, ins, [('t', D_MODEL, BF16), ('t', D_MODEL, BF16), ('t', 2 * D_MODEL, BF16)],
               rows=ya.shape[0], tm=512, name=name)


def _rot(x):
    sgn = jnp.where(lax.broadcasted_iota(jnp.int32, x.shape, 1) < HEAD // 2, -1.0, 1.0)
    return pltpu.roll(x, HEAD // 2, 1) * sgn


def _gain_rows(qn, kn):
    return [a[g:g + 1] for a in (qn, kn) for g in range(ATT_GROUPS)]


def _qk_fwd(proj, cos, sin, qn, kn, name):
    def fn(*v):
        xs, cosv, sinv, gains, vs = v[:6], v[6], v[7], v[8:14], v[14:17]
        outs = []
        for j, x in enumerate(xs):
            gain = gains[j]
            ys = []
            for xh in _heads(x):
                xn = xh * _rms_rows(xh) * gain
                ys.append(xn * cosv + _rot(xn) * sinv)
            outs.append(_cat(ys))
        return outs + list(vs)

    ins = ([('t', proj, 512, CB_AQ + j) for j in range(6)] + [('t', cos, HEAD, 0), ('t', sin, HEAD, 0)]
           + [('f', a) for a in _gain_rows(qn, kn)] + [('t', proj, 512, CB_AV + g) for g in range(ATT_GROUPS)])
    return _ew(fn, ins, [('t', ATT_GW, BF16)] * 9, rows=proj.shape[0], tm=512, name=name)


def _qk_bwd(dqk, proj, cos, sin, qn, kn, name):
    def fn(*v):
        ds, xs, cosv, sinv, gains = v[:6], v[6:12], v[12], v[13], v[14:20]
        rows8 = lax.broadcasted_iota(jnp.int32, (8, HEAD), 0)
        outs, dgs = [], [jnp.zeros((8, HEAD), F32)] * 2
        for j in range(6):
            gain = gains[j]
            dx, dg = [], jnp.zeros((1, HEAD), F32)
            for dyh, xh in zip(_heads(ds[j]), _heads(xs[j])):
                r = _rms_rows(xh)
                xhat = xh * r
                dxn = dyh * cosv - _rot(dyh * sinv)
                dg = dg + jnp.sum(dxn * xhat, axis=0, keepdims=True)
                dxh = dxn * gain
                dx.append(r * (dxh - xhat * jnp.mean(dxh * xhat, axis=1, keepdims=True)))
            outs.append(_cat(dx))
            dgs[j // 3] = dgs[j // 3] + jnp.where(rows8 == j % 3, dg, 0.0)
        return _cat(outs), dgs[0], dgs[1]

    ins = ([('t', a, ATT_GW, 0) for a in dqk] + [('t', proj, 512, CB_AQ + j) for j in range(6)]
           + [('t', cos, HEAD, 0), ('t', sin, HEAD, 0)] + [('f', a) for a in _gain_rows(qn, kn)])
    return _ew(fn, ins, [('t', 6 * ATT_GW, BF16), ('acc', (8, HEAD)), ('acc', (8, HEAD))],
               rows=proj.shape[0], tm=256, name=name)


def _merge_fwd(outs, lses, name):
    def fn(o0, o1, o2, l0, l1, l2):
        m = jnp.maximum(jnp.maximum(l0, l1), l2)
        e0, e1, e2 = jnp.exp(l0 - m), jnp.exp(l1 - m), jnp.exp(l2 - m)
        return (e0 * o0 + e1 * o1 + e2 * o2) / (e0 + e1 + e2)

    ins = [('t', a, ATT_GW, 0) for a in list(outs) + list(lses)]
    return _ew(fn, ins, [('t', ATT_GW, BF16)], rows=outs[0].shape[0], tm=512, name=name)[0]


def _merge_bwd(dob, outs, lses, name):
    def fn(dov, o0, o1, o2, l0, l1, l2):
        m = jnp.maximum(jnp.maximum(l0, l1), l2)
        e0, e1, e2 = jnp.exp(l0 - m), jnp.exp(l1 - m), jnp.exp(l2 - m)
        inv = 1.0 / (e0 + e1 + e2)
        a0, a1, a2 = e0 * inv, e1 * inv, e2 * inv
        ob = a0 * o0 + a1 * o1 + a2 * o2
        s = _head_mean(dov * ob) * float(HEAD)
        return a0 * dov, a1 * dov, a2 * dov, a0 * s, a1 * s, a2 * s

    ins = [('t', dob, ATT_GW, 0)] + [('t', a, ATT_GW, 0) for a in list(outs) + list(lses)]
    return _ew(fn, ins, [('t', ATT_GW, BF16)] * 3 + [('t', ATT_GW, F32)] * 3, rows=dob.shape[0], tm=512, name=name)


def _assemble_dproj(dh4, dqk, dvs, dgab, name):
    fn = lambda *v: _cat(list(v))
    ins = [('t', dh4, 4 * D_MODEL, 0), ('t', dqk, 6 * ATT_GW, 0)] + [('t', a, ATT_GW, 0) for a in dvs] + [('t', dgab, 2 * D_MODEL, 0)]
    return _ew(fn, ins, [('t', P_IN, BF16)], rows=dh4.shape[0], tm=256, name=name)[0]


HG_ROWS = 256


def _hg_gates(hq, hf, hi, lbv):
    sig = _sig(hf)
    f = lbv + (1.0 - lbv) * sig
    return hq * _sig(hq), 1.0 - f, hi, jnp.log(f), sig, f


def _split3(x):
    hi = _bf(x)
    r1 = x - hi.astype(F32)
    mid = _bf(r1)
    return hi, mid, _bf(r1 - mid.astype(F32))


def _tri_dot(tri, x):
    hi, mid, lo = _split3(x)
    return _dot(tri, hi) + _dot(tri, mid) + _dot(tri, lo)


def _row(x, i):
    rows = lax.broadcasted_iota(jnp.int32, x.shape, 0)
    return jnp.sum(jnp.where(rows == i, x, 0.0), axis=0, keepdims=True)


def _hg_decay(logf, q, k):
    c = HG_CHUNK
    row = lax.broadcasted_iota(jnp.int32, (c, c), 0)
    col = lax.broadcasted_iota(jnp.int32, (c, c), 1)
    g = _tri_dot((row >= col).astype(BF16), logf)
    gm = _row(g, c // 2 - 1)
    gl = _row(g, c - 1)
    return g, gm, gl, q * jnp.exp(g), q * jnp.exp(g - gm), k * jnp.exp(gm - g), k * jnp.exp(gl - g)


def _hg_out_fwd(o, hg, gain):
    r = lax.rsqrt(_head_mean(o * o) + EPS)
    return o * r * gain * (hg * _sig(hg))


def _hgrn_fwd(proj, lb, gain, name):
    t = proj.shape[0]
    nck = HG_ROWS // HG_CHUNK

    def body(hq_ref, hf_ref, hi_ref, hg_ref, lb_ref, gn_ref, o_ref, oa_ref, sall_ref, st_ref):
        @pl.when(pl.program_id(0) == 0)
        def _():
            st_ref[...] = jnp.zeros_like(st_ref)

        lbv = lb_ref[...]
        gnv = gn_ref[...]
        c = HG_CHUNK
        mask = lax.broadcasted_iota(jnp.int32, (c, c), 0) >= lax.broadcasted_iota(jnp.int32, (c, c), 1)

        def chunk(cc, carry):
            sl = pl.ds(pl.multiple_of(cc * c, c), c)
            q, k, v, logf, _, _ = _hg_gates(hq_ref[sl, :], hf_ref[sl, :], hi_ref[sl, :], lbv)
            _, _, gl, qg, qt, kt, kd = _hg_decay(logf, q, k)
            egl = jnp.exp(gl)
            os = []
            for h in range(HG_HEADS):
                hs = slice(h * HEAD, (h + 1) * HEAD)
                st = st_ref[h]
                sall_ref[cc, h] = st
                a = jnp.where(mask, _dot_nt(_bf(qt[:, hs]), _bf(kt[:, hs])), 0.0)
                os.append(_dot(_bf(a), _bf(v[:, hs])) + _dot_nt(_bf(qg[:, hs]), _bf(st)))
                st_ref[h] = egl[:, hs] * st + _dot_tn(_bf(v[:, hs]), _bf(kd[:, hs]))
            o = _cat(os)
            o_ref[sl, :] = o
            oa_ref[sl, :] = _hg_out_fwd(o, hg_ref[sl, :], gnv).astype(oa_ref.dtype)
            return carry

        lax.fori_loop(0, nck, chunk, 0)

    col = lambda j: pl.BlockSpec((HG_ROWS, D_MODEL), lambda i, j=j: (i, j))
    small = pl.BlockSpec((1, D_MODEL), lambda i: (0, 0))
    return pl.pallas_call(
        body, grid=(t // HG_ROWS,),
        in_specs=[col(0), col(1), col(2), col(3), small, small],
        out_specs=[col(0), col(0), pl.BlockSpec((nck, HG_HEADS, HEAD, HEAD), lambda i: (i, 0, 0, 0))],
        out_shape=[jax.ShapeDtypeStruct((t, D_MODEL), F32), jax.ShapeDtypeStruct((t, D_MODEL), BF16),
                   jax.ShapeDtypeStruct((t // HG_CHUNK, HG_HEADS, HEAD, HEAD), F32)],
        scratch_shapes=[pltpu.VMEM((HG_HEADS, HEAD, HEAD), F32)],
        name=name, compiler_params=_params(("arbitrary",)))(proj, proj, proj, proj, lb, gain)


def _terms(x, precise):
    hi = _bf(x)
    return (hi, _bf(x - hi.astype(F32))) if precise else (hi,)


def _mm(dot, a, b):
    out = dot(a[0], b[0])
    if len(a) > 1:
        out = out + dot(a[1], b[0])
    if len(b) > 1:
        out = out + dot(a[0], b[1])
    return out


def _hgrn_bwd(doa, oscan, proj, sall, lb, gain, name, precise):
    t = proj.shape[0]
    nck = HG_ROWS // HG_CHUNK
    nsteps = t // HG_ROWS
    terms = functools.partial(_terms, precise=precise)

    def body(doa_ref, os_ref, hq_ref, hf_ref, hi_ref, hg_ref, sall_ref, lb_ref, gn_ref,
             d4_ref, dgn_ref, dlb_ref, dst_ref):
        @pl.when(pl.program_id(0) == 0)
        def _():
            dst_ref[...] = jnp.zeros_like(dst_ref)
            dgn_ref[...] = jnp.zeros_like(dgn_ref)
            dlb_ref[...] = jnp.zeros_like(dlb_ref)

        lbv = lb_ref[...]
        gnv = gn_ref[...]
        c = HG_CHUNK
        row = lax.broadcasted_iota(jnp.int32, (c, c), 0)
        colm = lax.broadcasted_iota(jnp.int32, (c, c), 1)
        mask = row >= colm
        triu = (row <= colm).astype(BF16)
        last = lax.broadcasted_iota(jnp.int32, (c, HEAD), 0) == c - 1

        def chunk(ci, carry):
            cc = nck - 1 - ci
            sl = pl.ds(pl.multiple_of(cc * c, c), c)
            hq, hf, hg = hq_ref[sl, :], hf_ref[sl, :], hg_ref[sl, :]
            q, k, v, logf, sig, f = _hg_gates(hq, hf, hi_ref[sl, :], lbv)
            g, gm, gl, qg, qt, kt, kd = _hg_decay(logf, q, k)
            egl = jnp.exp(gl)
            o = os_ref[sl, :]
            dy = doa_ref[sl, :]
            r = lax.rsqrt(_head_mean(o * o) + EPS)
            oh = o * r
            sg = _sig(hg)
            silu_g = hg * sg
            dgn_ref[...] += jnp.sum(dy * oh * silu_g, axis=0, keepdims=True)
            dhg = dy * oh * gnv * (sg * (1.0 + hg * (1.0 - sg)))
            doh = dy * gnv * silu_g
            do = r * (doh - oh * _head_mean(doh * oh))
            dqs, dks, dvs, dgs = [], [], [], []
            for h in range(HG_HEADS):
                hs = slice(h * HEAD, (h + 1) * HEAD)
                st = sall_ref[cc, h]
                dst = dst_ref[h]
                qt_h, kt_h, qg_h, kd_h = qt[:, hs], kt[:, hs], qg[:, hs], kd[:, hs]
                do_p, v_p, qt_p, kt_p, qg_p = terms(do[:, hs]), terms(v[:, hs]), terms(qt_h), terms(kt_h), terms(qg_h)
                st_p, dst_p = terms(st), terms(dst)
                a = jnp.where(mask, _dot_nt(qt_p[0], kt_p[0]), 0.0)
                da = terms(jnp.where(mask, _mm(_dot_nt, do_p, v_p), 0.0))
                dqt = _mm(_dot, da, kt_p)
                dkt = _mm(_dot_tn, da, qt_p)
                dqg = _mm(_dot, do_p, st_p)
                dv = _dot_tn(_bf(a), do_p[0]) + _dot_nt(_bf(kd_h), dst_p[0])
                dkd = _mm(_dot, v_p, dst_p)
                dgl = egl[:, hs] * jnp.sum(st * dst, axis=0, keepdims=True) + jnp.sum(dkd * kd_h, axis=0, keepdims=True)
                dst_ref[h] = egl[:, hs] * dst + _mm(_dot_tn, do_p, qg_p)
                g_h = g[:, hs]
                gm_h = gm[:, hs]
                gl_h = gl[:, hs]
                dqs.append(dqt * jnp.exp(g_h - gm_h) + dqg * jnp.exp(g_h))
                dks.append(dkt * jnp.exp(gm_h - g_h) + dkd * jnp.exp(gl_h - g_h))
                dvs.append(dv)
                dgs.append(dqt * qt_h - dkt * kt_h + dqg * qg_h - dkd * kd_h + jnp.where(last, dgl, 0.0))
            dq, dk, dv, dg = _cat(dqs), _cat(dks), _cat(dvs), _cat(dgs)
            dlogf = _tri_dot(triu, dg)
            df = dlogf / f - dk
            dlb_ref[...] += jnp.sum(df * (1.0 - sig), axis=0, keepdims=True)
            dhf = df * (1.0 - lbv) * sig * (1.0 - sig)
            sq = _sig(hq)
            dhq = dq * (sq * (1.0 + hq * (1.0 - sq)))
            d4_ref[sl, :] = _cat([dhq, dhf, dv, dhg]).astype(d4_ref.dtype)
            return carry

        lax.fori_loop(0, nck, chunk, 0)

    rev = lambda j: pl.BlockSpec((HG_ROWS, D_MODEL), lambda i, j=j: (nsteps - 1 - i, j))
    small = pl.BlockSpec((1, D_MODEL), lambda i: (0, 0))
    return pl.pallas_call(
        body, grid=(nsteps,),
        in_specs=[rev(0), rev(0), rev(0), rev(1), rev(2), rev(3),
                  pl.BlockSpec((nck, HG_HEADS, HEAD, HEAD), lambda i: (nsteps - 1 - i, 0, 0, 0)), small, small],
        out_specs=[pl.BlockSpec((HG_ROWS, 4 * D_MODEL), lambda i: (nsteps - 1 - i, 0)), small, small],
        out_shape=[jax.ShapeDtypeStruct((t, 4 * D_MODEL), BF16), jax.ShapeDtypeStruct((1, D_MODEL), F32),
                   jax.ShapeDtypeStruct((1, D_MODEL), F32)],
        scratch_shapes=[pltpu.VMEM((HG_HEADS, HEAD, HEAD), F32)],
        name=name, compiler_params=_params(("arbitrary",)))(doa, oscan, proj, proj, proj, proj, sall, lb, gain)


def _band_masks():
    qi = lax.broadcasted_iota(jnp.int32, (ATT_BLK, ATT_BLK), 0)
    ki = lax.broadcasted_iota(jnp.int32, (ATT_BLK, ATT_BLK), 1)
    return ki >= qi, ki <= qi


def _attn_cfg(t, g):
    d = DILATIONS[g]
    length = t // d
    nb = length // ATT_BLK
    return d, length, nb, min(ATT_STEP_BLOCKS, nb)


def _attn_fwd(qg, kg, vg, g, name):
    t = qg.shape[0]
    d, length, nb, rb = _attn_cfg(t, g)
    scale = HEAD ** -0.5

    def body(q_ref, k_ref, v_ref, kp_ref, vp_ref, o_ref, l_ref):
        n = pl.program_id(1)
        prev_m, own_m = _band_masks()
        first_m = jnp.logical_and(prev_m, n > 0)
        for h in range(ATT_HEADS):
            hs = slice(h * HEAD, (h + 1) * HEAD)
            for j in range(rb):
                rows = slice(j * ATT_BLK, (j + 1) * ATT_BLK)
                before = slice((j - 1) * ATT_BLK, j * ATT_BLK)
                q = q_ref[rows, hs]
                k0, v0, m0 = (kp_ref[:, hs], vp_ref[:, hs], first_m) if j == 0 else (k_ref[before, hs], v_ref[before, hs], prev_m)
                s0 = jnp.where(m0, _dot_nt(q, k0) * scale, NEG)
                s1 = jnp.where(own_m, _dot_nt(q, k_ref[rows, hs]) * scale, NEG)
                m = jnp.maximum(jnp.max(s0, axis=1, keepdims=True), jnp.max(s1, axis=1, keepdims=True))
                p0, p1 = jnp.exp(s0 - m), jnp.exp(s1 - m)
                l = jnp.sum(p0, axis=1, keepdims=True) + jnp.sum(p1, axis=1, keepdims=True)
                o = _dot(_bf(p0), v0) + _dot(_bf(p1), v_ref[rows, hs])
                o_ref[rows, hs] = o / l
                l_ref[rows, hs] = jnp.broadcast_to(m + jnp.log(l), (ATT_BLK, HEAD))

    own = pl.BlockSpec((rb * ATT_BLK, ATT_GW), lambda r, n: (n, r))
    prev = pl.BlockSpec((ATT_BLK, ATT_GW), lambda r, n: (jnp.maximum(n * rb - 1, 0), r))
    view = lambda a: a.reshape(length, d * ATT_GW)
    o, lse = pl.pallas_call(
        body, grid=(d, nb // rb), in_specs=[own, own, own, prev, prev], out_specs=[own, own],
        out_shape=[jax.ShapeDtypeStruct((length, d * ATT_GW), F32)] * 2,
        name=name, compiler_params=_params(("parallel", "arbitrary")))(view(qg), view(kg), view(vg), view(kg), view(vg))
    return o.reshape(t, ATT_GW), lse.reshape(t, ATT_GW)


def _attn_bwd(qg, kg, vg, dog, lse, delta, g, name):
    t = qg.shape[0]
    d, length, nb, rb = _attn_cfg(t, g)
    nsteps = nb // rb
    scale = HEAD ** -0.5

    def body(q_ref, k_ref, v_ref, do_ref, l_ref, dl_ref, kp_ref, vp_ref, qn_ref, don_ref, ln_ref, dln_ref,
             dq_ref, dk_ref, dv_ref):
        n = pl.program_id(1)
        prev_m, own_m = _band_masks()
        first_m = jnp.logical_and(prev_m, n > 0)
        next_m = jnp.logical_and(prev_m, n < nsteps - 1)
        for h in range(ATT_HEADS):
            hs = slice(h * HEAD, (h + 1) * HEAD)
            dk, dv = [None] * rb, [None] * rb
            for j in range(rb + 1):
                rows = slice(j * ATT_BLK, (j + 1) * ATT_BLK)
                before = slice((j - 1) * ATT_BLK, j * ATT_BLK)
                if j < rb:
                    q, do, lse_q, dl_q = q_ref[rows, hs], do_ref[rows, hs], l_ref[rows, hs], dl_ref[rows, hs]
                else:
                    q, do, lse_q, dl_q = qn_ref[:, hs], don_ref[:, hs], ln_ref[:, hs], dln_ref[:, hs]
                if j == 0:
                    k0, v0, m0 = kp_ref[:, hs], vp_ref[:, hs], first_m
                else:
                    k0, v0, m0 = k_ref[before, hs], v_ref[before, hs], (prev_m if j < rb else next_m)
                p0 = jnp.where(m0, jnp.exp(_dot_nt(q, k0) * scale - lse_q), 0.0)
                ds0 = _bf(p0 * (_dot_nt(do, v0) - dl_q) * scale)
                if j >= 1:
                    dk[j - 1] = dk[j - 1] + _dot_tn(ds0, q)
                    dv[j - 1] = dv[j - 1] + _dot_tn(_bf(p0), do)
                if j < rb:
                    k1, v1 = k_ref[rows, hs], v_ref[rows, hs]
                    p1 = jnp.where(own_m, jnp.exp(_dot_nt(q, k1) * scale - lse_q), 0.0)
                    ds1 = _bf(p1 * (_dot_nt(do, v1) - dl_q) * scale)
                    dq_ref[rows, hs] = _dot(ds0, k0) + _dot(ds1, k1)
                    dk[j] = _dot_tn(ds1, q)
                    dv[j] = _dot_tn(_bf(p1), do)
            for j in range(rb):
                rows = slice(j * ATT_BLK, (j + 1) * ATT_BLK)
                dk_ref[rows, hs] = dk[j]
                dv_ref[rows, hs] = dv[j].astype(dv_ref.dtype)

    own = pl.BlockSpec((rb * ATT_BLK, ATT_GW), lambda r, n: (n, r))
    prev = pl.BlockSpec((ATT_BLK, ATT_GW), lambda r, n: (jnp.maximum(n * rb - 1, 0), r))
    nxt = pl.BlockSpec((ATT_BLK, ATT_GW), lambda r, n: (jnp.minimum((n + 1) * rb, nb - 1), r))
    view = lambda a: a.reshape(length, d * ATT_GW)
    dq, dk, dv = pl.pallas_call(
        body, grid=(d, nsteps), in_specs=[own] * 6 + [prev, prev] + [nxt] * 4, out_specs=[own, own, own],
        out_shape=[jax.ShapeDtypeStruct((length, d * ATT_GW), F32), jax.ShapeDtypeStruct((length, d * ATT_GW), F32),
                   jax.ShapeDtypeStruct((length, d * ATT_GW), BF16)],
        name=name, compiler_params=_params(("parallel", "arbitrary")))(
            view(qg), view(kg), view(vg), view(dog), view(lse), view(delta), view(kg), view(vg),
            view(qg), view(dog), view(lse), view(delta))
    return dq.reshape(t, ATT_GW), dk.reshape(t, ATT_GW), dv.reshape(t, ATT_GW)


def _rope_tables(t):
    pos = jnp.arange(t, dtype=F32)
    inv = ROPE_THETA ** (-jnp.arange(0, HEAD, 2, dtype=F32) / HEAD)
    ang = pos[:, None] * inv[None, :]
    ang = jnp.concatenate([ang, ang], axis=-1)
    return jnp.cos(ang), jnp.sin(ang)


def _lower_bounds(logits):
    lb = jnp.cumsum(jax.nn.softmax(logits.astype(F32), axis=0), axis=0)
    return lb - lb[0:1]


def _ffn_fwd(x, g, w_in, w_out, tag):
    h = _norm_fwd(x, g, name=tag + "_norm")
    ab = _mm_nn(h, w_in, name=tag + "_in", tm=512, tn=1408, out_dtype=F32)
    u = _swiglu_fwd(ab, name=tag + "_act")
    y = _mm_nn(u, w_out, name=tag + "_out", tm=512, tn=D_MODEL, out_dtype=F32, res=x, alpha=0.5)
    return y, (x, h, ab, u)


def _ffn_bwd(dx, saved, g, w_in, w_out, tag):
    x, h, ab, u = saved
    du = _mm_nt(dx, w_out, name=tag + "_bwd_du", tm=512, tp=1408, tn=D_MODEL, out_dtype=F32, alpha=0.5)
    g_out = _mm_tn(u, dx, nb=1, name=tag + "_bwd_wout", tm=512, tk=1408, tn=D_MODEL, alpha=0.5)
    dab = _swiglu_bwd(ab, du, name=tag + "_bwd_act")
    g_in = _mm_tn(h, dab, nb=N_CHIPS, name=tag + "_bwd_win", tm=512, tk=D_MODEL, tn=1408)
    dh = _mm_nt(dab, w_in, name=tag + "_bwd_dh", tm=512, tp=D_MODEL, tn=1408, out_dtype=F32)
    dx, dg = _norm_bwd(dh, x, g, dx, name=tag + "_bwd_norm")
    return dx, dg, g_in, g_out.reshape(N_CHIPS, D_FF // N_CHIPS, D_MODEL)


def _mix_fwd(x, w, lb, cos, sin, tag):
    h = _norm_fwd(x, w["mix_norm"], name=tag + "_norm")
    proj = _mm_nn(h, w["w_in"], name=tag + "_in", tm=512, tn=896, out_dtype=F32)
    oscan, oa, sall = _hgrn_fwd(proj, lb, w["hgrn_out_norm"], name=tag + "_hgrn")
    qk = _qk_fwd(proj, cos, sin, w["attn_q_norm"], w["attn_k_norm"], name=tag + "_qk")
    outs, lses = [], []
    for g in range(ATT_GROUPS):
        o, l = _attn_fwd(qk[g], qk[3 + g], qk[6 + g], g, name=f"{tag}_attn{g}")
        outs.append(o)
        lses.append(l)
    ob = _merge_fwd(outs, lses, name=tag + "_merge")
    ya = _mm_nn(oa, w["w_branch_a"], name=tag + "_wa", tm=512, tn=D_MODEL, out_dtype=F32)
    yb = _mm_nn(ob, w["w_branch_b"], name=tag + "_wb", tm=512, tn=256, out_dtype=F32)
    merged = _gate_fwd(proj, ya, yb, name=tag + "_gate")
    y = _mm_nn(merged, w["w_out"], name=tag + "_out", tm=512, tn=D_MODEL, out_dtype=F32, res=x)
    return y, (x, h, proj, oscan, oa, sall, qk, outs, lses, ob, ya, yb, merged)


def _mix_bwd(dx, saved, w, lb, cos, sin, tag, lb_live):
    x, h, proj, oscan, oa, sall, qk, outs, lses, ob, ya, yb, merged = saved
    dm = _mm_nt(dx, w["w_out"], name=tag + "_bwd_dm", tm=512, tp=D_MODEL, tn=D_MODEL, out_dtype=F32)
    g_wout = _mm_tn(merged, dx, nb=1, name=tag + "_bwd_wout", tm=512, tk=256, tn=D_MODEL)
    dya, dyb, dgab = _gate_bwd(dm, proj, ya, yb, name=tag + "_bwd_gate")
    doa = _mm_nt(dya, w["w_branch_a"], name=tag + "_bwd_doa", tm=512, tp=D_MODEL, tn=D_MODEL, out_dtype=F32)
    g_wa = _mm_tn(oa, dya, nb=1, name=tag + "_bwd_wa", tm=512, tk=256, tn=D_MODEL)
    dob = _mm_nt(dyb, w["w_branch_b"], name=tag + "_bwd_dob", tm=512, tp=ATT_GW, tn=256, out_dtype=F32)
    g_wb = _mm_tn(ob, dyb, nb=N_CHIPS, name=tag + "_bwd_wb", tm=512, tk=ATT_GW, tn=256)
    mb = _merge_bwd(dob, outs, lses, name=tag + "_bwd_merge")
    dqk, dvs = [None] * 6, []
    for g in range(ATT_GROUPS):
        dq, dk, dv = _attn_bwd(qk[g], qk[3 + g], qk[6 + g], mb[g], lses[g], mb[3 + g], g, name=f"{tag}_bwd_attn{g}")
        dqk[g], dqk[3 + g] = dq, dk
        dvs.append(dv)
    dqk_cols, dqn, dkn = _qk_bwd(dqk, proj, cos, sin, w["attn_q_norm"], w["attn_k_norm"], name=tag + "_bwd_qk")
    dh4, dgn, dlb = _hgrn_bwd(doa, oscan, proj, sall, lb, w["hgrn_out_norm"], name=tag + "_bwd_hgrn", precise=lb_live)
    dproj = _assemble_dproj(dh4, dqk_cols, dvs, dgab, name=tag + "_bwd_cat")
    g_win = _mm_tn(h, dproj, nb=N_CHIPS, name=tag + "_bwd_win", tm=512, tk=D_MODEL, tn=896)
    dh = _mm_nt(dproj, w["w_in"], name=tag + "_bwd_dh", tm=512, tp=D_MODEL, tn=896, out_dtype=F32)
    dx, dg = _norm_bwd(dh, x, w["mix_norm"], dx, name=tag + "_bwd_norm")
    big = dict(w_in=g_win, w_branch_a=g_wa.reshape(N_CHIPS, D_MODEL // N_CHIPS, D_MODEL), w_branch_b=g_wb,
               w_out=g_wout.reshape(N_CHIPS, D_MODEL // N_CHIPS, D_MODEL))
    small = dict(mix_norm=dg, hgrn_out_norm=dgn, lb=dlb, attn_q_norm=dqn, attn_k_norm=dkn)
    return dx, big, small


BIG = ("ffn1_w_in", "ffn1_w_out", "w_in", "w_branch_a", "w_branch_b", "w_out", "ffn2_w_in", "ffn2_w_out")
ROW_SHARDED = ("ffn1_w_out", "w_branch_a", "w_out", "ffn2_w_out")
SMALL = ("ffn1_norm", "mix_norm", "hgrn_lb_logits", "hgrn_out_norm", "attn_q_norm", "attn_k_norm", "ffn2_norm")
WEIGHTS = ("ffn1_norm", "ffn1_w_in", "ffn1_w_out", "mix_norm", "w_in", "hgrn_lb_logits", "hgrn_out_norm", "attn_q_norm",
           "attn_k_norm", "w_branch_a", "w_branch_b", "w_out", "ffn2_norm", "ffn2_w_in", "ffn2_w_out")
SMALL_ROWS = 8


def _layer_weights(full, small, l):
    w = {}
    for n in BIG:
        a = full[n][l]
        w[n] = a.reshape(1, a.shape[0] * a.shape[1], a.shape[2]) if n in ROW_SHARDED else a
    for n in ("ffn1_norm", "mix_norm", "hgrn_out_norm", "ffn2_norm"):
        w[n] = small[n][l].reshape(1, D_MODEL)
    for n in ("attn_q_norm", "attn_k_norm"):
        w[n] = small[n][l]
    return w


def _local_step(x, target, full, small):
    t = x.shape[0]
    cos, sin = _rope_tables(t)
    lbs = _lower_bounds(small["hgrn_lb_logits"])
    saved = []
    for l in range(2):
        w = _layer_weights(full, small, l)
        lb = lbs[l].reshape(1, D_MODEL)
        x, s1 = _ffn_fwd(x, w["ffn1_norm"], w["ffn1_w_in"], w["ffn1_w_out"], f"l{l}_ffn1")
        x, s2 = _mix_fwd(x, w, lb, cos, sin, f"l{l}_mix")
        x, s3 = _ffn_fwd(x, w["ffn2_norm"], w["ffn2_w_in"], w["ffn2_w_out"], f"l{l}_ffn2")
        saved.append((w, lb, s1, s2, s3))
    dx, sq = _loss_fwd_bwd(x, target, name="loss")
    big_grads, small_rows = [None, None], [None, None]
    for l in (1, 0):
        w, lb, s1, s2, s3 = saved[l]
        dx, dg2, g_in2, g_out2 = _ffn_bwd(dx, s3, w["ffn2_norm"], w["ffn2_w_in"], w["ffn2_w_out"], f"l{l}_ffn2")
        dx, big, sm = _mix_bwd(dx, s2, w, lb, cos, sin, f"l{l}_mix", lb_live=l > 0)
        dx, dg1, g_in1, g_out1 = _ffn_bwd(dx, s1, w["ffn1_norm"], w["ffn1_w_in"], w["ffn1_w_out"], f"l{l}_ffn1")
        big.update(ffn1_w_in=g_in1, ffn1_w_out=g_out1, ffn2_w_in=g_in2, ffn2_w_out=g_out2)
        big_grads[l] = big
        pad = lambda a: jnp.pad(a[:ATT_GROUPS].reshape(1, ATT_GROUPS * HEAD), ((0, 0), (0, D_MODEL - ATT_GROUPS * HEAD)))
        small_rows[l] = jnp.concatenate(
            [dg1, sm["mix_norm"], sm["lb"], sm["hgrn_out_norm"], pad(sm["attn_q_norm"]), pad(sm["attn_k_norm"]), dg2,
             jnp.zeros((SMALL_ROWS - 7, D_MODEL), F32)], axis=0)
    return jnp.sum(sq), dx, big_grads, jnp.concatenate(small_rows, axis=0)


ANY = pl.BlockSpec(memory_space=pl.ANY)


def _coords():
    return lax.axis_index("x"), lax.axis_index("y"), lax.axis_index("c")


def _other_chips(x, y):
    return [(1 - x, y), (x, 1 - y), (1 - x, 1 - y)]


def _all_gather_weights(shards):
    n = len(shards)

    def body(*refs):
        w, full = refs[:n], refs[n:2 * n]
        send, recv, fsend, frecv, local = refs[2 * n:]
        x, y, c = _coords()
        slot = 2 * x + y
        chips = _other_chips(x, y)
        mine = [pltpu.make_async_copy(w[i], full[i].at[:, slot], local.at[i]) for i in range(n)]
        for cp in mine:
            cp.start()

        def ici(i, j, to_chip, src_slot):
            src = w[i].at[c] if src_slot is None else full[i].at[c, src_slot]
            dst = full[i].at[c, slot if src_slot is None else src_slot]
            return pltpu.make_async_remote_copy(src_ref=src, dst_ref=dst, send_sem=send.at[i * 3 + j],
                                                recv_sem=recv.at[i * 3 + j], device_id=(*to_chip, c), device_id_type=MESH)

        def d2d(i, j, layer, chip):
            blk = full[i].at[layer, 2 * chip[0] + chip[1]]
            return pltpu.make_async_remote_copy(src_ref=blk, dst_ref=blk, send_sem=fsend.at[i * 3 + j],
                                                recv_sem=frecv.at[i * 3 + j], device_id=(x, y, 1 - c), device_id_type=MESH)

        first = [ici(i, j, chip, None) for i in range(n) for j, chip in enumerate(chips)]
        for cp in first:
            cp.start()
        passed = []
        for i in range(n):
            for j, chip in enumerate(chips):
                ici(i, j, chip, 2 * chip[0] + chip[1]).wait_recv()
                cp = d2d(i, j, c, chip)
                cp.start()
                passed.append(cp)
        for i in range(n):
            for j, chip in enumerate(chips):
                d2d(i, j, 1 - c, chip).wait_recv()
        for cp in first + passed:
            cp.wait_send()
        for cp in mine:
            cp.wait()

    out_shape = [jax.ShapeDtypeStruct((2, N_CHIPS) + s.shape[1:], s.dtype) for s in shards]
    sems = [pltpu.SemaphoreType.DMA((3 * n,))] * 4 + [pltpu.SemaphoreType.DMA((n,))]
    return pl.pallas_call(body, in_specs=[ANY] * n, out_specs=[ANY] * n, out_shape=out_shape, scratch_shapes=sems,
                          name="all_gather_weights")(*shards)


N_PART = 8


def _scatter_partials(g0, g1):
    n = len(g0)

    def body(*refs):
        p0, p1, out = refs[:n], refs[n:2 * n], refs[2 * n:3 * n]
        send, recv, local = refs[3 * n:]
        x, y, c = _coords()
        slot = 2 * x + y
        chips = _other_chips(x, y)
        for layer, parts in ((0, p0), (1, p1)):
            @pl.when(c == layer)
            def _(parts=parts):
                own = [pltpu.make_async_copy(parts[i].at[slot], out[i].at[7], local.at[i]) for i in range(n)]
                for cp in own:
                    cp.start()
                for cp in own:
                    cp.wait()
        sends = []
        for layer, parts in ((0, p0), (1, p1)):
            for i in range(n):
                for j, chip in enumerate(chips):
                    k = i * 7 + 2 * j
                    sends.append(pltpu.make_async_remote_copy(
                        src_ref=parts[i].at[2 * chip[0] + chip[1]], dst_ref=out[i].at[2 * j + c],
                        send_sem=send.at[layer * 7 * n + k], recv_sem=recv.at[k + c],
                        device_id=(*chip, layer), device_id_type=MESH))
        for cp in sends:
            cp.start()
        for layer, parts in ((0, p0), (1, p1)):
            @pl.when(c != layer)
            def _(parts=parts, layer=layer):
                sib = [pltpu.make_async_remote_copy(
                    src_ref=parts[i].at[slot], dst_ref=out[i].at[6], send_sem=send.at[layer * 7 * n + i * 7 + 6],
                    recv_sem=recv.at[i * 7 + 6], device_id=(x, y, layer), device_id_type=MESH) for i in range(n)]
                for cp in sib:
                    cp.start()
                for cp in sib:
                    cp.wait_send()
        for cp in sends:
            cp.wait_send()
        for i in range(n):
            for k in range(7):
                pltpu.make_async_remote_copy(src_ref=out[i].at[k], dst_ref=out[i].at[k], send_sem=send.at[0],
                                             recv_sem=recv.at[i * 7 + k], device_id=(x, y, c), device_id_type=MESH).wait_recv()

    out_shape = [jax.ShapeDtypeStruct((N_PART,) + a.shape[1:], a.dtype) for a in g0]
    sems = [pltpu.SemaphoreType.DMA((14 * n,)), pltpu.SemaphoreType.DMA((7 * n,)), pltpu.SemaphoreType.DMA((n,))]
    return pl.pallas_call(body, in_specs=[ANY] * (2 * n), out_specs=[ANY] * n, out_shape=out_shape, scratch_shapes=sems,
                          name="scatter_partials")(*g0, *g1)


def _sum_partials(parts, name):
    _, r, wd = parts.shape
    tm = 128 if r % 128 == 0 else 64

    def body(p_ref, o_ref):
        acc = p_ref[0].astype(F32)
        for k in range(1, N_PART):
            acc = acc + p_ref[k].astype(F32)
        o_ref[...] = acc

    return pl.pallas_call(
        body, grid=(r // tm,), in_specs=[pl.BlockSpec((N_PART, tm, wd), lambda i: (0, i, 0))],
        out_specs=pl.BlockSpec((tm, wd), lambda i: (i, 0)), out_shape=jax.ShapeDtypeStruct((r, wd), F32),
        name=name, compiler_params=_params(("parallel",)))(parts)


def _exchange_layers(reduced):
    n = len(reduced)

    def body(*refs):
        r, out = refs[:n], refs[n:2 * n]
        send, recv, local = refs[2 * n:]
        x, y, c = _coords()
        own = [pltpu.make_async_copy(r[i], out[i].at[c], local.at[i]) for i in range(n)]
        sib = [pltpu.make_async_remote_copy(src_ref=r[i], dst_ref=out[i].at[c], send_sem=send.at[i], recv_sem=recv.at[i],
                                            device_id=(x, y, 1 - c), device_id_type=MESH) for i in range(n)]
        for cp in own + sib:
            cp.start()
        for i in range(n):
            pltpu.make_async_remote_copy(src_ref=r[i], dst_ref=out[i].at[1 - c], send_sem=send.at[i], recv_sem=recv.at[i],
                                         device_id=(x, y, 1 - c), device_id_type=MESH).wait_recv()
        for cp in sib:
            cp.wait_send()
        for cp in own:
            cp.wait()

    out_shape = [jax.ShapeDtypeStruct((2,) + a.shape, a.dtype) for a in reduced]
    sems = [pltpu.SemaphoreType.DMA((n,))] * 3
    return pl.pallas_call(body, in_specs=[ANY] * n, out_specs=[ANY] * n, out_shape=out_shape, scratch_shapes=sems,
                          name="exchange_layers")(*reduced)


def _all_reduce_small(rows):
    r = rows.shape[0]

    def body(x_ref, o_ref, buf, send, recv):
        x, y, c = _coords()
        me = 4 * x + 2 * y + c
        buf[me] = x_ref[...]
        copies = []
        for k in range(1, 8):
            peer = (x ^ (k >> 2), y ^ ((k >> 1) & 1), c ^ (k & 1))
            cp = pltpu.make_async_remote_copy(src_ref=x_ref, dst_ref=buf.at[me], send_sem=send.at[k - 1], recv_sem=recv.at[me],
                                              device_id=peer, device_id_type=MESH)
            cp.start()
            copies.append(cp)
        for k in range(1, 8):
            src = 4 * (x ^ (k >> 2)) + 2 * (y ^ ((k >> 1) & 1)) + (c ^ (k & 1))
            pltpu.make_async_remote_copy(src_ref=x_ref, dst_ref=buf.at[src], send_sem=send.at[0], recv_sem=recv.at[src],
                                         device_id=(x, y, c), device_id_type=MESH).wait_recv()
        for cp in copies:
            cp.wait_send()
        acc = buf[0]
        for k in range(1, 8):
            acc = acc + buf[k]
        o_ref[...] = acc

    vm = pl.BlockSpec(memory_space=pltpu.VMEM)
    return pl.pallas_call(
        body, in_specs=[vm], out_specs=vm, out_shape=jax.ShapeDtypeStruct(rows.shape, F32),
        scratch_shapes=[pltpu.VMEM((8, r, D_MODEL), F32), pltpu.SemaphoreType.DMA((7,)), pltpu.SemaphoreType.DMA((8,))],
        name="all_reduce_small")(rows)


def _adamw_math(w, g, m, v):
    m = ADAM_B1 * m + (1.0 - ADAM_B1) * g
    v = ADAM_B2 * v + (1.0 - ADAM_B2) * (g * g)
    m_hat = m / (1.0 - ADAM_B1 ** ADAM_STEP)
    v_hat = v / (1.0 - ADAM_B2 ** ADAM_STEP)
    return -ADAM_LR * (m_hat / (jnp.sqrt(v_hat) + ADAM_EPS) + ADAM_WD * w), m, v


def _adamw(w, g, m, v, name):
    shape = w.shape
    cols = shape[-1]
    flat = lambda a: a.reshape(-1, cols)
    rows = flat(w).shape[0]
    tm = 128 if rows % 128 == 0 else rows
    ins = [('t', flat(a), cols, 0) for a in (w, g, m, v)]
    res = _ew(_adamw_math, ins, [('t', cols, F32)] * 3, rows=rows, tm=tm, name=name)
    return [a.reshape(shape) for a in res]


def _small_update(sums, logits, w, m, v):
    def body(s_ref, lg_ref, w_ref, m_ref, v_ref, g_ref, d_ref, nm_ref, nv_ref):
        s = s_ref[...]
        l0, l1 = lg_ref[0:1, :], lg_ref[1:2, :]
        mx = jnp.maximum(l0, l1)
        e0, e1 = jnp.exp(l0 - mx), jnp.exp(l1 - mx)
        sm0, sm1 = e0 / (e0 + e1), e1 / (e0 + e1)
        dl1 = s_ref[SMALL_ROWS + 2:SMALL_ROWS + 3, :] * sm0 * sm1
        row = lax.broadcasted_iota(jnp.int32, s.shape, 0)
        g = jnp.where(row == 2, -dl1, jnp.where(row == SMALL_ROWS + 2, dl1, s))
        d, nm, nv = _adamw_math(w_ref[...], g, m_ref[...], v_ref[...])
        g_ref[...] = g
        d_ref[...] = d
        nm_ref[...] = nm
        nv_ref[...] = nv

    vm = pl.BlockSpec(memory_space=pltpu.VMEM)
    return pl.pallas_call(body, in_specs=[vm] * 5, out_specs=[vm] * 4,
                          out_shape=[jax.ShapeDtypeStruct(sums.shape, F32)] * 4, name="small_update")(sums, logits, w, m, v)


def _pack_small(vals):
    rows = []
    for l in range(2):
        for n in ("ffn1_norm", "mix_norm", "hgrn_lb_logits", "hgrn_out_norm", "attn_q_norm", "attn_k_norm", "ffn2_norm"):
            a = vals[n][l].reshape(1, -1)
            rows.append(jnp.pad(a, ((0, 0), (0, D_MODEL - a.shape[1]))))
        rows.append(jnp.zeros((SMALL_ROWS - 7, D_MODEL), F32))
    return jnp.concatenate(rows, axis=0)


def _unpack_small(packed):
    out = {}
    for k, n in enumerate(("ffn1_norm", "mix_norm", "hgrn_lb_logits", "hgrn_out_norm", "attn_q_norm", "attn_k_norm", "ffn2_norm")):
        a = jnp.stack([packed[k], packed[SMALL_ROWS + k]], axis=0)
        out[n] = a[:, :ATT_GROUPS * HEAD].reshape(2, ATT_GROUPS, HEAD) if n.startswith("attn") else a
    return out


def kernel(x, ffn1_norm, ffn1_w_in, ffn1_w_out, mix_norm, w_in, hgrn_lb_logits, hgrn_out_norm, attn_q_norm, attn_k_norm, w_branch_a, w_branch_b, w_out, ffn2_norm, ffn2_w_in, ffn2_w_out, loss_target, m_ffn1_norm, m_ffn1_w_in, m_ffn1_w_out, m_mix_norm, m_w_in, m_hgrn_lb_logits, m_hgrn_out_norm, m_attn_q_norm, m_attn_k_norm, m_w_branch_a, m_w_branch_b, m_w_out, m_ffn2_norm, m_ffn2_w_in, m_ffn2_w_out, v_ffn1_norm, v_ffn1_w_in, v_ffn1_w_out, v_mix_norm, v_w_in, v_hgrn_lb_logits, v_hgrn_out_norm, v_attn_q_norm, v_attn_k_norm, v_w_branch_a, v_w_branch_b, v_w_out, v_ffn2_norm, v_ffn2_w_in, v_ffn2_w_out):
    a = locals()
    w = {n: a[n] for n in WEIGHTS}
    m = {n: a["m_" + n] for n in WEIGHTS}
    v = {n: a["v_" + n] for n in WEIGHTS}

    gathered = _all_gather_weights([w[n].astype(BF16) for n in BIG])
    full = dict(zip(BIG, gathered))
    small = {n: w[n] for n in SMALL}
    sq, grad_x, big_grads, small_rows = _local_step(x[0], loss_target[0], full, small)
    loss = lax.psum(sq, ("x", "y", "c")) * (0.5 / D_MODEL)

    parts = _scatter_partials([big_grads[0][n] for n in BIG], [big_grads[1][n] for n in BIG])
    reduced = [_sum_partials(p.reshape(N_PART, -1, p.shape[-1]), name="sum_" + n).reshape(p.shape[1:])
               for n, p in zip(BIG, parts)]
    grads = dict(zip(BIG, _exchange_layers(reduced)))

    sums = _all_reduce_small(small_rows)
    g_s, d_s, m_s, v_s = _small_update(sums, w["hgrn_lb_logits"], _pack_small(small), _pack_small({n: m[n] for n in SMALL}),
                                       _pack_small({n: v[n] for n in SMALL}))
    grads.update(_unpack_small(g_s))
    delta, new_m, new_v = _unpack_small(d_s), _unpack_small(m_s), _unpack_small(v_s)
    for n in BIG:
        delta[n], new_m[n], new_v[n] = _adamw(w[n], grads[n], m[n], v[n], name="adamw_" + n)

    return (loss, grad_x[None], *[grads[n] for n in WEIGHTS], *[delta[n] for n in WEIGHTS],
            *[new_m[n] for n in WEIGHTS], *[new_v[n] for n in WEIGHTS])
```

```python
import functools

import jax
import jax.numpy as jnp
from jax import lax
from jax.experimental import pallas as pl
from jax.experimental.pallas import tpu as pltpu

F32 = jnp.float32
BF16 = jnp.bfloat16
MESH = pl.DeviceIdType.MESH

D_MODEL = 1024
D_FF = 2816
N_CHIPS = 4
HEAD = 128
HG_HEADS = 8
HG_CHUNK = 64
ATT_GROUPS = 3
ATT_HEADS = 4
ATT_GW = ATT_HEADS * HEAD
DILATIONS = (1, 4, 16)
ATT_BLK = 128
ATT_STEP_BLOCKS = 4
P_IN = 10752
CB_AQ, CB_AK, CB_AV, CB_GA, CB_GB = 8, 11, 14, 17, 19
EPS = 1e-6
ROPE_THETA = 10000.0
ADAM_LR, ADAM_B1, ADAM_B2, ADAM_EPS, ADAM_WD, ADAM_STEP = 0.001, 0.9, 0.999, 1e-08, 0.01, 10
VMEM_LIMIT_V7X = 56 * 1024 * 1024
NEG = -1e30


def _params(sem):
    return pltpu.CompilerParams(dimension_semantics=sem, vmem_limit_bytes=VMEM_LIMIT_V7X)


def _sig(x):
    return 1.0 / (1.0 + jnp.exp(-x))


def _dot(a, b):
    return jnp.dot(a, b, preferred_element_type=F32)


def _dot_nt(a, b):
    return lax.dot_general(a, b, (((1,), (1,)), ((), ())), preferred_element_type=F32)


def _dot_tn(a, b):
    return lax.dot_general(a, b, (((0,), (0,)), ((), ())), preferred_element_type=F32)


def _bf(x):
    return x.astype(BF16)


def _mm_nn(a, b3, *, name, tm, tn, out_dtype, res=None, alpha=1.0):
    m, k = a.shape
    nb, _, nw = b3.shape
    per = nw // tn
    assert nw % tn == 0 and m % tm == 0
    has_res = res is not None

    def body(*refs):
        if has_res:
            a_ref, b_ref, r_ref, o_ref = refs
        else:
            a_ref, b_ref, o_ref = refs
        acc = _dot(_bf(a_ref[...]), b_ref[...])
        if alpha != 1.0:
            acc = alpha * acc
        if has_res:
            acc = r_ref[...] + acc
        o_ref[...] = acc.astype(o_ref.dtype)

    in_specs = [pl.BlockSpec((tm, k), lambda i, j: (i, 0)),
                pl.BlockSpec((None, k, tn), lambda i, j: (j // per, 0, j % per))]
    args = [a, b3]
    if has_res:
        in_specs.append(pl.BlockSpec((tm, tn), lambda i, j: (i, j)))
        args.append(res)
    return pl.pallas_call(
        body, grid=(m // tm, nb * per), in_specs=in_specs,
        out_specs=pl.BlockSpec((tm, tn), lambda i, j: (i, j)),
        out_shape=jax.ShapeDtypeStruct((m, nb * nw), out_dtype),
        name=name, compiler_params=_params(("parallel", "arbitrary")))(*args)


def _mm_nt(d, b3, *, name, tm, tp, tn, out_dtype, alpha=1.0):
    m, n = d.shape
    nb, p, nw = b3.shape
    per = nw // tn
    nk = n // tn
    assert nb * nw == n and nw % tn == 0 and p % tp == 0 and m % tm == 0

    def body(d_ref, b_ref, o_ref, acc_ref):
        kk = pl.program_id(2)

        @pl.when(kk == 0)
        def _():
            acc_ref[...] = jnp.zeros_like(acc_ref)

        acc_ref[...] += _dot_nt(_bf(d_ref[...]), b_ref[...])

        @pl.when(kk == nk - 1)
        def _():
            o_ref[...] = (alpha * acc_ref[...]).astype(o_ref.dtype)

    return pl.pallas_call(
        body, grid=(m // tm, p // tp, nk),
        in_specs=[pl.BlockSpec((tm, tn), lambda i, j, kk: (i, kk)),
                  pl.BlockSpec((None, tp, tn), lambda i, j, kk: (kk // per, j, kk % per))],
        out_specs=pl.BlockSpec((tm, tp), lambda i, j, kk: (i, j)),
        out_shape=jax.ShapeDtypeStruct((m, p), out_dtype),
        scratch_shapes=[pltpu.VMEM((tm, tp), F32)],
        name=name, compiler_params=_params(("parallel", "parallel", "arbitrary")))(d, b3)


def _mm_tn(a, d, *, nb, name, tm, tk, tn, alpha=1.0):
    m, k = a.shape
    _, n = d.shape
    nw = n // nb
    per = nw // tn
    nm = m // tm
    assert nw % tn == 0 and k % tk == 0 and m % tm == 0

    def body(a_ref, d_ref, o_ref, acc_ref):
        mm = pl.program_id(2)

        @pl.when(mm == 0)
        def _():
            acc_ref[...] = jnp.zeros_like(acc_ref)

        acc_ref[...] += _dot_tn(_bf(a_ref[...]), _bf(d_ref[...]))

        @pl.when(mm == nm - 1)
        def _():
            o_ref[...] = (alpha * acc_ref[...]).astype(o_ref.dtype)

    return pl.pallas_call(
        body, grid=(k // tk, nb * per, nm),
        in_specs=[pl.BlockSpec((tm, tk), lambda i, j, mm: (mm, i)),
                  pl.BlockSpec((tm, tn), lambda i, j, mm: (mm, j))],
        out_specs=pl.BlockSpec((None, tk, tn), lambda i, j, mm: (j // per, i, j % per)),
        out_shape=jax.ShapeDtypeStruct((nb, k, nw), BF16),
        scratch_shapes=[pltpu.VMEM((tk, tn), F32)],
        name=name, compiler_params=_params(("parallel", "parallel", "arbitrary")))(a, d)


def _ew(fn, ins, outs, *, rows, tm, name):
    in_specs, args = [], []
    for s in ins:
        if s[0] == 't':
            _, arr, w, cb = s
            in_specs.append(pl.BlockSpec((tm, w), lambda i, cb=cb: (i, cb)))
        else:
            arr = s[1]
            in_specs.append(pl.BlockSpec(arr.shape, lambda i, nd=arr.ndim: (0,) * nd))
        args.append(arr)
    out_specs, out_shape = [], []
    for s in outs:
        if s[0] == 't':
            _, w, dt = s
            out_specs.append(pl.BlockSpec((tm, w), lambda i: (i, 0)))
            out_shape.append(jax.ShapeDtypeStruct((rows, w), dt))
        else:
            out_specs.append(pl.BlockSpec(s[1], lambda i: (0, 0)))
            out_shape.append(jax.ShapeDtypeStruct(s[1], F32))
    n_in = len(ins)

    def body(*refs):
        res = fn(*[r[...] for r in refs[:n_in]])
        if not isinstance(res, (tuple, list)):
            res = (res,)
        for r, s, v in zip(refs[n_in:], outs, res):
            if s[0] == 't':
                r[...] = v.astype(r.dtype)
            else:
                @pl.when(pl.program_id(0) == 0)
                def _(r=r):
                    r[...] = jnp.zeros_like(r)

                r[...] += v

    res = pl.pallas_call(
        body, grid=(rows // tm,), in_specs=in_specs, out_specs=out_specs, out_shape=out_shape,
        name=name, compiler_params=_params(("arbitrary",)))(*args)
    return res


def _heads(x):
    return [x[:, h * HEAD:(h + 1) * HEAD] for h in range(x.shape[1] // HEAD)]


def _cat(xs):
    return jnp.concatenate(xs, axis=1)


def _head_mean(x):
    return _cat([jnp.broadcast_to(jnp.mean(h, axis=1, keepdims=True), h.shape) for h in _heads(x)])


def _rms_rows(x):
    return lax.rsqrt(jnp.mean(x * x, axis=1, keepdims=True) + EPS)


def _norm_fwd(x, g, name):
    return _ew(lambda xv, gv: xv * _rms_rows(xv) * gv,
               [('t', x, D_MODEL, 0), ('f', g)], [('t', D_MODEL, BF16)], rows=x.shape[0], tm=512, name=name)[0]


def _norm_bwd(dh, x, g, dx, name):
    def fn(dhv, xv, gv, dxv):
        r = _rms_rows(xv)
        xh = xv * r
        dxh = dhv * gv
        out = dxv + r * (dxh - xh * jnp.mean(dxh * xh, axis=1, keepdims=True))
        return out, jnp.sum(dhv * xh, axis=0, keepdims=True)

    return _ew(fn, [('t', dh, D_MODEL, 0), ('t', x, D_MODEL, 0), ('f', g), ('t', dx, D_MODEL, 0)],
               [('t', D_MODEL, F32), ('acc', (1, D_MODEL))], rows=x.shape[0], tm=512, name=name)


def _swiglu_fwd(ab, name):
    return _ew(lambda a, b: a * _sig(a) * b, [('t', ab, D_FF, 0), ('t', ab, D_FF, 1)], [('t', D_FF, BF16)],
               rows=ab.shape[0], tm=256, name=name)[0]


def _swiglu_bwd(ab, du, name):
    def fn(a, b, duv):
        s = _sig(a)
        return _cat([duv * b * (s * (1.0 + a * (1.0 - s))), duv * a * s])

    return _ew(fn, [('t', ab, D_FF, 0), ('t', ab, D_FF, 1), ('t', du, D_FF, 0)], [('t', 2 * D_FF, BF16)],
               rows=ab.shape[0], tm=256, name=name)[0]


def _loss_fwd_bwd(y, target, name):
    def fn(yv, tv):
        e = yv - tv
        return e * (1.0 / D_MODEL), jnp.sum(e * e, axis=0, keepdims=True)

    return _ew(fn, [('t', y, D_MODEL, 0), ('t', target, D_MODEL, 0)], [('t', D_MODEL, F32), ('acc', (1, D_MODEL))],
               rows=y.shape[0], tm=512, name=name)


def _gate_fwd(proj, ya, yb, name):
    def fn(ga0, ga1, gb0, gb1, yav, ybv):
        return _sig(_cat([ga0, ga1])) * yav + _sig(_cat([gb0, gb1])) * ybv

    ins = [('t', proj, 512, CB_GA), ('t', proj, 512, CB_GA + 1), ('t', proj, 512, CB_GB), ('t', proj, 512, CB_GB + 1),
           ('t', ya, D_MODEL, 0), ('t', yb, D_MODEL, 0)]
    return _ew(fn, ins, [('t', D_MODEL, BF16)], rows=ya.shape[0], tm=512, name=name)[0]


def _gate_bwd(dm, proj, ya, yb, name):
    def fn(dmv, ga0, ga1, gb0, gb1, yav, ybv):
        sa = _sig(_cat([ga0, ga1]))
        sb = _sig(_cat([gb0, gb1]))
        return dmv * sa, dmv * sb, _cat([dmv * yav * sa * (1.0 - sa), dmv * ybv * sb * (1.0 - sb)])

    ins = [('t', dm, D_MODEL, 0),
           ('t', proj, 512, CB_GA), ('t', proj, 512, CB_GA + 1), ('t', proj, 512, CB_GB), ('t', proj, 512, CB_GB + 1),
           ('t', ya, D_MODEL, 0), ('t', yb, D_MODEL, 0)]
    return _ew(fn, ins, [('t', D_MODEL, BF16), ('t', D_MODEL, BF16), ('t', 2 * D_MODEL, BF16)],
               rows=ya.shape[0], tm=512, name=name)


def _rot(x):
    sgn = jnp.where(lax.broadcasted_iota(jnp.int32, x.shape, 1) < HEAD // 2, -1.0, 1.0)
    return pltpu.roll(x, HEAD // 2, 1) * sgn


def _gain_rows(qn, kn):
    return [a[g:g + 1] for a in (qn, kn) for g in range(ATT_GROUPS)]


def _qk_fwd(proj, cos, sin, qn, kn, name):
    def fn(*v):
        xs, cosv, sinv, gains, vs = v[:6], v[6], v[7], v[8:14], v[14:17]
        outs = []
        for j, x in enumerate(xs):
            gain = gains[j]
            ys = []
            for xh in _heads(x):
                xn = xh * _rms_rows(xh) * gain
                ys.append(xn * cosv + _rot(xn) * sinv)
            outs.append(_cat(ys))
        return outs + list(vs)

    ins = ([('t', proj, 512, CB_AQ + j) for j in range(6)] + [('t', cos, HEAD, 0), ('t', sin, HEAD, 0)]
           + [('f', a) for a in _gain_rows(qn, kn)] + [('t', proj, 512, CB_AV + g) for g in range(ATT_GROUPS)])
    return _ew(fn, ins, [('t', ATT_GW, BF16)] * 9, rows=proj.shape[0], tm=512, name=name)


def _qk_bwd(dqk, proj, cos, sin, qn, kn, name):
    def fn(*v):
        ds, xs, cosv, sinv, gains = v[:6], v[6:12], v[12], v[13], v[14:20]
        rows8 = lax.broadcasted_iota(jnp.int32, (8, HEAD), 0)
        outs, dgs = [], [jnp.zeros((8, HEAD), F32)] * 2
        for j in range(6):
            gain = gains[j]
            dx, dg = [], jnp.zeros((1, HEAD), F32)
            for dyh, xh in zip(_heads(ds[j]), _heads(xs[j])):
                r = _rms_rows(xh)
                xhat = xh * r
                dxn = dyh * cosv - _rot(dyh * sinv)
                dg = dg + jnp.sum(dxn * xhat, axis=0, keepdims=True)
                dxh = dxn * gain
                dx.append(r * (dxh - xhat * jnp.mean(dxh * xhat, axis=1, keepdims=True)))
            outs.append(_cat(dx))
            dgs[j // 3] = dgs[j // 3] + jnp.where(rows8 == j % 3, dg, 0.0)
        return _cat(outs), dgs[0], dgs[1]

    ins = ([('t', a, ATT_GW, 0) for a in dqk] + [('t', proj, 512, CB_AQ + j) for j in range(6)]
           + [('t', cos, HEAD, 0), ('t', sin, HEAD, 0)] + [('f', a) for a in _gain_rows(qn, kn)])
    return _ew(fn, ins, [('t', 6 * ATT_GW, BF16), ('acc', (8, HEAD)), ('acc', (8, HEAD))],
               rows=proj.shape[0], tm=256, name=name)


def _merge_fwd(outs, lses, name):
    def fn(o0, o1, o2, l0, l1, l2):
        m = jnp.maximum(jnp.maximum(l0, l1), l2)
        e0, e1, e2 = jnp.exp(l0 - m), jnp.exp(l1 - m), jnp.exp(l2 - m)
        return (e0 * o0 + e1 * o1 + e2 * o2) / (e0 + e1 + e2)

    ins = [('t', a, ATT_GW, 0) for a in list(outs) + list(lses)]
    return _ew(fn, ins, [('t', ATT_GW, BF16)], rows=outs[0].shape[0], tm=512, name=name)[0]


def _merge_bwd(dob, outs, lses, name):
    def fn(dov, o0, o1, o2, l0, l1, l2):
        m = jnp.maximum(jnp.maximum(l0, l1), l2)
        e0, e1, e2 = jnp.exp(l0 - m), jnp.exp(l1 - m), jnp.exp(l2 - m)
        inv = 1.0 / (e0 + e1 + e2)
        a0, a1, a2 = e0 * inv, e1 * inv, e2 * inv
        ob = a0 * o0 + a1 * o1 + a2 * o2
        s = _head_mean(dov * ob) * float(HEAD)
        return a0 * dov, a1 * dov, a2 * dov, a0 * s, a1 * s, a2 * s

    ins = [('t', dob, ATT_GW, 0)] + [('t', a, ATT_GW, 0) for a in list(outs) + list(lses)]
    return _ew(fn, ins, [('t', ATT_GW, BF16)] * 3 + [('t', ATT_GW, F32)] * 3, rows=dob.shape[0], tm=512, name=name)


def _assemble_dproj(dh4, dqk, dvs, dgab, name):
    fn = lambda *v: _cat(list(v))
    ins = [('t', dh4, 4 * D_MODEL, 0), ('t', dqk, 6 * ATT_GW, 0)] + [('t', a, ATT_GW, 0) for a in dvs] + [('t', dgab, 2 * D_MODEL, 0)]
    return _ew(fn, ins, [('t', P_IN, BF16)], rows=dh4.shape[0], tm=256, name=name)[0]


HG_ROWS = 256


def _hg_gates(hq, hf, hi, lbv):
    sig = _sig(hf)
    f = lbv + (1.0 - lbv) * sig
    return hq * _sig(hq), 1.0 - f, hi, jnp.log(f), sig, f


def _split3(x):
    hi = _bf(x)
    r1 = x - hi.astype(F32)
    mid = _bf(r1)
    return hi, mid, _bf(r1 - mid.astype(F32))


def _tri_dot(tri, x):
    hi, mid, lo = _split3(x)
    return _dot(tri, hi) + _dot(tri, mid) + _dot(tri, lo)


def _row(x, i):
    rows = lax.broadcasted_iota(jnp.int32, x.shape, 0)
    return jnp.sum(jnp.where(rows == i, x, 0.0), axis=0, keepdims=True)


def _hg_decay(logf, q, k):
    c = HG_CHUNK
    row = lax.broadcasted_iota(jnp.int32, (c, c), 0)
    col = lax.broadcasted_iota(jnp.int32, (c, c), 1)
    g = _tri_dot((row >= col).astype(BF16), logf)
    gm = _row(g, c // 2 - 1)
    gl = _row(g, c - 1)
    return g, gm, gl, q * jnp.exp(g), q * jnp.exp(g - gm), k * jnp.exp(gm - g), k * jnp.exp(gl - g)


def _hg_out_fwd(o, hg, gain):
    r = lax.rsqrt(_head_mean(o * o) + EPS)
    return o * r * gain * (hg * _sig(hg))


def _hgrn_fwd(proj, lb, gain, name):
    t = proj.shape[0]
    nck = HG_ROWS // HG_CHUNK

    def body(hq_ref, hf_ref, hi_ref, hg_ref, lb_ref, gn_ref, o_ref, oa_ref, sall_ref, st_ref):
        @pl.when(pl.program_id(0) == 0)
        def _():
            st_ref[...] = jnp.zeros_like(st_ref)

        lbv = lb_ref[...]
        gnv = gn_ref[...]
        c = HG_CHUNK
        mask = lax.broadcasted_iota(jnp.int32, (c, c), 0) >= lax.broadcasted_iota(jnp.int32, (c, c), 1)

        def chunk(cc, carry):
            sl = pl.ds(pl.multiple_of(cc * c, c), c)
            q, k, v, logf, _, _ = _hg_gates(hq_ref[sl, :], hf_ref[sl, :], hi_ref[sl, :], lbv)
            _, _, gl, qg, qt, kt, kd = _hg_decay(logf, q, k)
            egl = jnp.exp(gl)
            os = []
            for h in range(HG_HEADS):
                hs = slice(h * HEAD, (h + 1) * HEAD)
                st = st_ref[h]
                sall_ref[cc, h] = st
                a = jnp.where(mask, _dot_nt(_bf(qt[:, hs]), _bf(kt[:, hs])), 0.0)
                os.append(_dot(_bf(a), _bf(v[:, hs])) + _dot_nt(_bf(qg[:, hs]), _bf(st)))
                st_ref[h] = egl[:, hs] * st + _dot_tn(_bf(v[:, hs]), _bf(kd[:, hs]))
            o = _cat(os)
            o_ref[sl, :] = o
            oa_ref[sl, :] = _hg_out_fwd(o, hg_ref[sl, :], gnv).astype(oa_ref.dtype)
            return carry

        lax.fori_loop(0, nck, chunk, 0)

    col = lambda j: pl.BlockSpec((HG_ROWS, D_MODEL), lambda i, j=j: (i, j))
    small = pl.BlockSpec((1, D_MODEL), lambda i: (0, 0))
    return pl.pallas_call(
        body, grid=(t // HG_ROWS,),
        in_specs=[col(0), col(1), col(2), col(3), small, small],
        out_specs=[col(0), col(0), pl.BlockSpec((nck, HG_HEADS, HEAD, HEAD), lambda i: (i, 0, 0, 0))],
        out_shape=[jax.ShapeDtypeStruct((t, D_MODEL), F32), jax.ShapeDtypeStruct((t, D_MODEL), BF16),
                   jax.ShapeDtypeStruct((t // HG_CHUNK, HG_HEADS, HEAD, HEAD), F32)],
        scratch_shapes=[pltpu.VMEM((HG_HEADS, HEAD, HEAD), F32)],
        name=name, compiler_params=_params(("arbitrary",)))(proj, proj, proj, proj, lb, gain)


def _terms(x, precise):
    hi = _bf(x)
    return (hi, _bf(x - hi.astype(F32))) if precise else (hi,)


def _mm(dot, a, b):
    out = dot(a[0], b[0])
    if len(a) > 1:
        out = out + dot(a[1], b[0])
    if len(b) > 1:
        out = out + dot(a[0], b[1])
    return out


def _hgrn_bwd(doa, oscan, proj, sall, lb, gain, name, precise):
    t = proj.shape[0]
    nck = HG_ROWS // HG_CHUNK
    nsteps = t // HG_ROWS
    terms = functools.partial(_terms, precise=precise)

    def body(doa_ref, os_ref, hq_ref, hf_ref, hi_ref, hg_ref, sall_ref, lb_ref, gn_ref,
             d4_ref, dgn_ref, dlb_ref, dst_ref):
        @pl.when(pl.program_id(0) == 0)
        def _():
            dst_ref[...] = jnp.zeros_like(dst_ref)
            dgn_ref[...] = jnp.zeros_like(dgn_ref)
            dlb_ref[...] = jnp.zeros_like(dlb_ref)

        lbv = lb_ref[...]
        gnv = gn_ref[...]
        c = HG_CHUNK
        row = lax.broadcasted_iota(jnp.int32, (c, c), 0)
        colm = lax.broadcasted_iota(jnp.int32, (c, c), 1)
        mask = row >= colm
        triu = (row <= colm).astype(BF16)
        last = lax.broadcasted_iota(jnp.int32, (c, HEAD), 0) == c - 1

        def chunk(ci, carry):
            cc = nck - 1 - ci
            sl = pl.ds(pl.multiple_of(cc * c, c), c)
            hq, hf, hg = hq_ref[sl, :], hf_ref[sl, :], hg_ref[sl, :]
            q, k, v, logf, sig, f = _hg_gates(hq, hf, hi_ref[sl, :], lbv)
            g, gm, gl, qg, qt, kt, kd = _hg_decay(logf, q, k)
            egl = jnp.exp(gl)
            o = os_ref[sl, :]
            dy = doa_ref[sl, :]
            r = lax.rsqrt(_head_mean(o * o) + EPS)
            oh = o * r
            sg = _sig(hg)
            silu_g = hg * sg
            dgn_ref[...] += jnp.sum(dy * oh * silu_g, axis=0, keepdims=True)
            dhg = dy * oh * gnv * (sg * (1.0 + hg * (1.0 - sg)))
            doh = dy * gnv * silu_g
            do = r * (doh - oh * _head_mean(doh * oh))
            dqs, dks, dvs, dgs = [], [], [], []
            for h in range(HG_HEADS):
                hs = slice(h * HEAD, (h + 1) * HEAD)
                st = sall_ref[cc, h]
                dst = dst_ref[h]
                qt_h, kt_h, qg_h, kd_h = qt[:, hs], kt[:, hs], qg[:, hs], kd[:, hs]
                do_p, v_p, qt_p, kt_p, qg_p = terms(do[:, hs]), terms(v[:, hs]), terms(qt_h), terms(kt_h), terms(qg_h)
                st_p, dst_p = terms(st), terms(dst)
                a = jnp.where(mask, _dot_nt(qt_p[0], kt_p[0]), 0.0)
                da = terms(jnp.where(mask, _mm(_dot_nt, do_p, v_p), 0.0))
                dqt = _mm(_dot, da, kt_p)
                dkt = _mm(_dot_tn, da, qt_p)
                dqg = _mm(_dot, do_p, st_p)
                dv = _dot_tn(_bf(a), do_p[0]) + _dot_nt(_bf(kd_h), dst_p[0])
                dkd = _mm(_dot, v_p, dst_p)
                dgl = egl[:, hs] * jnp.sum(st * dst, axis=0, keepdims=True) + jnp.sum(dkd * kd_h, axis=0, keepdims=True)
                dst_ref[h] = egl[:, hs] * dst + _mm(_dot_tn, do_p, qg_p)
                g_h = g[:, hs]
                gm_h = gm[:, hs]
                gl_h = gl[:, hs]
                dqs.append(dqt * jnp.exp(g_h - gm_h) + dqg * jnp.exp(g_h))
                dks.append(dkt * jnp.exp(gm_h - g_h) + dkd * jnp.exp(gl_h - g_h))
                dvs.append(dv)
                dgs.append(dqt * qt_h - dkt * kt_h + dqg * qg_h - dkd * kd_h + jnp.where(last, dgl, 0.0))
            dq, dk, dv, dg = _cat(dqs), _cat(dks), _cat(dvs), _cat(dgs)
            dlogf = _tri_dot(triu, dg)
            df = dlogf / f - dk
            dlb_ref[...] += jnp.sum(df * (1.0 - sig), axis=0, keepdims=True)
            dhf = df * (1.0 - lbv) * sig * (1.0 - sig)
            sq = _sig(hq)
            dhq = dq * (sq * (1.0 + hq * (1.0 - sq)))
            d4_ref[sl, :] = _cat([dhq, dhf, dv, dhg]).astype(d4_ref.dtype)
            return carry

        lax.fori_loop(0, nck, chunk, 0)

    rev = lambda j: pl.BlockSpec((HG_ROWS, D_MODEL), lambda i, j=j: (nsteps - 1 - i, j))
    small = pl.BlockSpec((1, D_MODEL), lambda i: (0, 0))
    return pl.pallas_call(
        body, grid=(nsteps,),
        in_specs=[rev(0), rev(0), rev(0), rev(1), rev(2), rev(3),
                  pl.BlockSpec((nck, HG_HEADS, HEAD, HEAD), lambda i: (nsteps - 1 - i, 0, 0, 0)), small, small],
        out_specs=[pl.BlockSpec((HG_ROWS, 4 * D_MODEL), lambda i: (nsteps - 1 - i, 0)), small, small],
        out_shape=[jax.ShapeDtypeStruct((t, 4 * D_MODEL), BF16), jax.ShapeDtypeStruct((1, D_MODEL), F32),
                   jax.ShapeDtypeStruct((1, D_MODEL), F32)],
        scratch_shapes=[pltpu.VMEM((HG_HEADS, HEAD, HEAD), F32)],
        name=name, compiler_params=_params(("arbitrary",)))(doa, oscan, proj, proj, proj, proj, sall, lb, gain)


def _band_masks():
    qi = lax.broadcasted_iota(jnp.int32, (ATT_BLK, ATT_BLK), 0)
    ki = lax.broadcasted_iota(jnp.int32, (ATT_BLK, ATT_BLK), 1)
    return ki >= qi, ki <= qi


def _attn_cfg(t, g):
    d = DILATIONS[g]
    length = t // d
    nb = length // ATT_BLK
    return d, length, nb, min(ATT_STEP_BLOCKS, nb)


def _attn_fwd(qg, kg, vg, g, name):
    t = qg.shape[0]
    d, length, nb, rb = _attn_cfg(t, g)
    scale = HEAD ** -0.5

    def body(q_ref, k_ref, v_ref, kp_ref, vp_ref, o_ref, l_ref):
        n = pl.program_id(1)
        prev_m, own_m = _band_masks()
        first_m = jnp.logical_and(prev_m, n > 0)
        for h in range(ATT_HEADS):
            hs = slice(h * HEAD, (h + 1) * HEAD)
            for j in range(rb):
                rows = slice(j * ATT_BLK, (j + 1) * ATT_BLK)
                before = slice((j - 1) * ATT_BLK, j * ATT_BLK)
                q = q_ref[rows, hs]
                k0, v0, m0 = (kp_ref[:, hs], vp_ref[:, hs], first_m) if j == 0 else (k_ref[before, hs], v_ref[before, hs], prev_m)
                s0 = jnp.where(m0, _dot_nt(q, k0) * scale, NEG)
                s1 = jnp.where(own_m, _dot_nt(q, k_ref[rows, hs]) * scale, NEG)
                m = jnp.maximum(jnp.max(s0, axis=1, keepdims=True), jnp.max(s1, axis=1, keepdims=True))
                p0, p1 = jnp.exp(s0 - m), jnp.exp(s1 - m)
                l = jnp.sum(p0, axis=1, keepdims=True) + jnp.sum(p1, axis=1, keepdims=True)
                o = _dot(_bf(p0), v0) + _dot(_bf(p1), v_ref[rows, hs])
                o_ref[rows, hs] = o / l
                l_ref[rows, hs] = jnp.broadcast_to(m + jnp.log(l), (ATT_BLK, HEAD))

    own = pl.BlockSpec((rb * ATT_BLK, ATT_GW), lambda r, n: (n, r))
    prev = pl.BlockSpec((ATT_BLK, ATT_GW), lambda r, n: (jnp.maximum(n * rb - 1, 0), r))
    view = lambda a: a.reshape(length, d * ATT_GW)
    o, lse = pl.pallas_call(
        body, grid=(d, nb // rb), in_specs=[own, own, own, prev, prev], out_specs=[own, own],
        out_shape=[jax.ShapeDtypeStruct((length, d * ATT_GW), F32)] * 2,
        name=name, compiler_params=_params(("parallel", "arbitrary")))(view(qg), view(kg), view(vg), view(kg), view(vg))
    return o.reshape(t, ATT_GW), lse.reshape(t, ATT_GW)


def _attn_bwd(qg, kg, vg, dog, lse, delta, g, name):
    t = qg.shape[0]
    d, length, nb, rb = _attn_cfg(t, g)
    nsteps = nb // rb
    scale = HEAD ** -0.5

    def body(q_ref, k_ref, v_ref, do_ref, l_ref, dl_ref, kp_ref, vp_ref, qn_ref, don_ref, ln_ref, dln_ref,
             dq_ref, dk_ref, dv_ref):
        n = pl.program_id(1)
        prev_m, own_m = _band_masks()
        first_m = jnp.logical_and(prev_m, n > 0)
        next_m = jnp.logical_and(prev_m, n < nsteps - 1)
        for h in range(ATT_HEADS):
            hs = slice(h * HEAD, (h + 1) * HEAD)
            dk, dv = [None] * rb, [None] * rb
            for j in range(rb + 1):
                rows = slice(j * ATT_BLK, (j + 1) * ATT_BLK)
                before = slice((j - 1) * ATT_BLK, j * ATT_BLK)
                if j < rb:
                    q, do, lse_q, dl_q = q_ref[rows, hs], do_ref[rows, hs], l_ref[rows, hs], dl_ref[rows, hs]
                else:
                    q, do, lse_q, dl_q = qn_ref[:, hs], don_ref[:, hs], ln_ref[:, hs], dln_ref[:, hs]
                if j == 0:
                    k0, v0, m0 = kp_ref[:, hs], vp_ref[:, hs], first_m
                else:
                    k0, v0, m0 = k_ref[before, hs], v_ref[before, hs], (prev_m if j < rb else next_m)
                p0 = jnp.where(m0, jnp.exp(_dot_nt(q, k0) * scale - lse_q), 0.0)
                ds0 = _bf(p0 * (_dot_nt(do, v0) - dl_q) * scale)
                if j >= 1:
                    dk[j - 1] = dk[j - 1] + _dot_tn(ds0, q)
                    dv[j - 1] = dv[j - 1] + _dot_tn(_bf(p0), do)
                if j < rb:
                    k1, v1 = k_ref[rows, hs], v_ref[rows, hs]
                    p1 = jnp.where(own_m, jnp.exp(_dot_nt(q, k1) * scale - lse_q), 0.0)
                    ds1 = _bf(p1 * (_dot_nt(do, v1) - dl_q) * scale)
                    dq_ref[rows, hs] = _dot(ds0, k0) + _dot(ds1, k1)
                    dk[j] = _dot_tn(ds1, q)
                    dv[j] = _dot_tn(_bf(p1), do)
            for j in range(rb):
                rows = slice(j * ATT_BLK, (j + 1) * ATT_BLK)
                dk_ref[rows, hs] = dk[j]
                dv_ref[rows, hs] = dv[j].astype(dv_ref.dtype)

    own = pl.BlockSpec((rb * ATT_BLK, ATT_GW), lambda r, n: (n, r))
    prev = pl.BlockSpec((ATT_BLK, ATT_GW), lambda r, n: (jnp.maximum(n * rb - 1, 0), r))
    nxt = pl.BlockSpec((ATT_BLK, ATT_GW), lambda r, n: (jnp.minimum((n + 1) * rb, nb - 1), r))
    view = lambda a: a.reshape(length, d * ATT_GW)
    dq, dk, dv = pl.pallas_call(
        body, grid=(d, nsteps), in_specs=[own] * 6 + [prev, prev] + [nxt] * 4, out_specs=[own, own, own],
        out_shape=[jax.ShapeDtypeStruct((length, d * ATT_GW), F32), jax.ShapeDtypeStruct((length, d * ATT_GW), F32),
                   jax.ShapeDtypeStruct((length, d * ATT_GW), BF16)],
        name=name, compiler_params=_params(("parallel", "arbitrary")))(
            view(qg), view(kg), view(vg), view(dog), view(lse), view(delta), view(kg), view(vg),
            view(qg), view(dog), view(lse), view(delta))
    return dq.reshape(t, ATT_GW), dk.reshape(t, ATT_GW), dv.reshape(t, ATT_GW)


def _rope_tables(t):
    pos = jnp.arange(t, dtype=F32)
    inv = ROPE_THETA ** (-jnp.arange(0, HEAD, 2, dtype=F32) / HEAD)
    ang = pos[:, None] * inv[None, :]
    ang = jnp.concatenate([ang, ang], axis=-1)
    return jnp.cos(ang), jnp.sin(ang)


def _lower_bounds(logits):
    lb = jnp.cumsum(jax.nn.softmax(logits.astype(F32), axis=0), axis=0)
    return lb - lb[0:1]


def _ffn_fwd(x, g, w_in, w_out, tag):
    h = _norm_fwd(x, g, name=tag + "_norm")
    ab = _mm_nn(h, w_in, name=tag + "_in", tm=1024, tn=1408, out_dtype=F32)
    u = _swiglu_fwd(ab, name=tag + "_act")
    y = _mm_nn(u, w_out, name=tag + "_out", tm=512, tn=D_MODEL, out_dtype=F32, res=x, alpha=0.5)
    return y, (x, h, ab, u)


def _ffn_bwd(dx, saved, g, w_in, w_out, tag):
    x, h, ab, u = saved
    du = _mm_nt(dx, w_out, name=tag + "_bwd_du", tm=512, tp=1408, tn=D_MODEL, out_dtype=F32, alpha=0.5)
    g_out = _mm_tn(u, dx, nb=1, name=tag + "_bwd_wout", tm=512, tk=1408, tn=D_MODEL, alpha=0.5)
    dab = _swiglu_bwd(ab, du, name=tag + "_bwd_act")
    g_in = _mm_tn(h, dab, nb=N_CHIPS, name=tag + "_bwd_win", tm=512, tk=D_MODEL, tn=1408)
    dh = _mm_nt(dab, w_in, name=tag + "_bwd_dh", tm=512, tp=D_MODEL, tn=1408, out_dtype=F32)
    dx, dg = _norm_bwd(dh, x, g, dx, name=tag + "_bwd_norm")
    return dx, dg, g_in, g_out.reshape(N_CHIPS, D_FF // N_CHIPS, D_MODEL)


def _mix_fwd(x, w, lb, cos, sin, tag):
    h = _norm_fwd(x, w["mix_norm"], name=tag + "_norm")
    proj = _mm_nn(h, w["w_in"], name=tag + "_in", tm=1024, tn=896, out_dtype=F32)
    oscan, oa, sall = _hgrn_fwd(proj, lb, w["hgrn_out_norm"], name=tag + "_hgrn")
    qk = _qk_fwd(proj, cos, sin, w["attn_q_norm"], w["attn_k_norm"], name=tag + "_qk")
    outs, lses = [], []
    for g in range(ATT_GROUPS):
        o, l = _attn_fwd(qk[g], qk[3 + g], qk[6 + g], g, name=f"{tag}_attn{g}")
        outs.append(o)
        lses.append(l)
    ob = _merge_fwd(outs, lses, name=tag + "_merge")
    ya = _mm_nn(oa, w["w_branch_a"], name=tag + "_wa", tm=512, tn=D_MODEL, out_dtype=F32)
    yb = _mm_nn(ob, w["w_branch_b"], name=tag + "_wb", tm=512, tn=256, out_dtype=F32)
    merged = _gate_fwd(proj, ya, yb, name=tag + "_gate")
    y = _mm_nn(merged, w["w_out"], name=tag + "_out", tm=512, tn=D_MODEL, out_dtype=F32, res=x)
    return y, (x, h, proj, oscan, oa, sall, qk, outs, lses, ob, ya, yb, merged)


def _mix_bwd(dx, saved, w, lb, cos, sin, tag, lb_live):
    x, h, proj, oscan, oa, sall, qk, outs, lses, ob, ya, yb, merged = saved
    dm = _mm_nt(dx, w["w_out"], name=tag + "_bwd_dm", tm=512, tp=D_MODEL, tn=D_MODEL, out_dtype=F32)
    g_wout = _mm_tn(merged, dx, nb=1, name=tag + "_bwd_wout", tm=512, tk=256, tn=D_MODEL)
    dya, dyb, dgab = _gate_bwd(dm, proj, ya, yb, name=tag + "_bwd_gate")
    doa = _mm_nt(dya, w["w_branch_a"], name=tag + "_bwd_doa", tm=512, tp=D_MODEL, tn=D_MODEL, out_dtype=F32)
    g_wa = _mm_tn(oa, dya, nb=1, name=tag + "_bwd_wa", tm=512, tk=256, tn=D_MODEL)
    dob = _mm_nt(dyb, w["w_branch_b"], name=tag + "_bwd_dob", tm=512, tp=ATT_GW, tn=256, out_dtype=F32)
    g_wb = _mm_tn(ob, dyb, nb=N_CHIPS, name=tag + "_bwd_wb", tm=512, tk=ATT_GW, tn=256)
    mb = _merge_bwd(dob, outs, lses, name=tag + "_bwd_merge")
    dqk, dvs = [None] * 6, []
    for g in range(ATT_GROUPS):
        dq, dk, dv = _attn_bwd(qk[g], qk[3 + g], qk[6 + g], mb[g], lses[g], mb[3 + g], g, name=f"{tag}_bwd_attn{g}")
        dqk[g], dqk[3 + g] = dq, dk
        dvs.append(dv)
    dqk_cols, dqn, dkn = _qk_bwd(dqk, proj, cos, sin, w["attn_q_norm"], w["attn_k_norm"], name=tag + "_bwd_qk")
    dh4, dgn, dlb = _hgrn_bwd(doa, oscan, proj, sall, lb, w["hgrn_out_norm"], name=tag + "_bwd_hgrn", precise=lb_live)
    dproj = _assemble_dproj(dh4, dqk_cols, dvs, dgab, name=tag + "_bwd_cat")
    g_win = _mm_tn(h, dproj, nb=N_CHIPS, name=tag + "_bwd_win", tm=512, tk=D_MODEL, tn=896)
    dh = _mm_nt(dproj, w["w_in"], name=tag + "_bwd_dh", tm=512, tp=D_MODEL, tn=896, out_dtype=F32)
    dx, dg = _norm_bwd(dh, x, w["mix_norm"], dx, name=tag + "_bwd_norm")
    big = dict(w_in=g_win, w_branch_a=g_wa.reshape(N_CHIPS, D_MODEL // N_CHIPS, D_MODEL), w_branch_b=g_wb,
               w_out=g_wout.reshape(N_CHIPS, D_MODEL // N_CHIPS, D_MODEL))
    small = dict(mix_norm=dg, hgrn_out_norm=dgn, lb=dlb, attn_q_norm=dqn, attn_k_norm=dkn)
    return dx, big, small


BIG = ("ffn1_w_in", "ffn1_w_out", "w_in", "w_branch_a", "w_branch_b", "w_out", "ffn2_w_in", "ffn2_w_out")
ROW_SHARDED = ("ffn1_w_out", "w_branch_a", "w_out", "ffn2_w_out")
SMALL = ("ffn1_norm", "mix_norm", "hgrn_lb_logits", "hgrn_out_norm", "attn_q_norm", "attn_k_norm", "ffn2_norm")
WEIGHTS = ("ffn1_norm", "ffn1_w_in", "ffn1_w_out", "mix_norm", "w_in", "hgrn_lb_logits", "hgrn_out_norm", "attn_q_norm",
           "attn_k_norm", "w_branch_a", "w_branch_b", "w_out", "ffn2_norm", "ffn2_w_in", "ffn2_w_out")
SMALL_ROWS = 8


def _layer_weights(full, small, l):
    w = {}
    for n in BIG:
        a = full[n]
        w[n] = a.reshape(1, a.shape[0] * a.shape[1], a.shape[2]) if n in ROW_SHARDED else a
    for n in ("ffn1_norm", "mix_norm", "hgrn_out_norm", "ffn2_norm"):
        w[n] = small[n][l].reshape(1, D_MODEL)
    for n in ("attn_q_norm", "attn_k_norm"):
        w[n] = small[n][l]
    return w


def _local_step(x, target, fulls, small):
    t = x.shape[0]
    cos, sin = _rope_tables(t)
    lbs = _lower_bounds(small["hgrn_lb_logits"])
    saved = []
    for l in range(2):
        w = _layer_weights(fulls[l], small, l)
        lb = lbs[l].reshape(1, D_MODEL)
        x, s1 = _ffn_fwd(x, w["ffn1_norm"], w["ffn1_w_in"], w["ffn1_w_out"], f"l{l}_ffn1")
        x, s2 = _mix_fwd(x, w, lb, cos, sin, f"l{l}_mix")
        x, s3 = _ffn_fwd(x, w["ffn2_norm"], w["ffn2_w_in"], w["ffn2_w_out"], f"l{l}_ffn2")
        saved.append((w, lb, s1, s2, s3))
    dx, sq = _loss_fwd_bwd(x, target, name="loss")
    big_grads, small_rows = [None, None], [None, None]
    for l in (1, 0):
        w, lb, s1, s2, s3 = saved[l]
        dx, dg2, g_in2, g_out2 = _ffn_bwd(dx, s3, w["ffn2_norm"], w["ffn2_w_in"], w["ffn2_w_out"], f"l{l}_ffn2")
        dx, big, sm = _mix_bwd(dx, s2, w, lb, cos, sin, f"l{l}_mix", lb_live=l > 0)
        dx, dg1, g_in1, g_out1 = _ffn_bwd(dx, s1, w["ffn1_norm"], w["ffn1_w_in"], w["ffn1_w_out"], f"l{l}_ffn1")
        big.update(ffn1_w_in=g_in1, ffn1_w_out=g_out1, ffn2_w_in=g_in2, ffn2_w_out=g_out2)
        big_grads[l] = big
        pad = lambda a: jnp.pad(a[:ATT_GROUPS].reshape(1, ATT_GROUPS * HEAD), ((0, 0), (0, D_MODEL - ATT_GROUPS * HEAD)))
        small_rows[l] = jnp.concatenate(
            [dg1, sm["mix_norm"], sm["lb"], sm["hgrn_out_norm"], pad(sm["attn_q_norm"]), pad(sm["attn_k_norm"]), dg2,
             jnp.zeros((SMALL_ROWS - 7, D_MODEL), F32)], axis=0)
    return jnp.sum(sq), dx, big_grads, jnp.concatenate(small_rows, axis=0)


ANY = pl.BlockSpec(memory_space=pl.ANY)


def _coords():
    return lax.axis_index("x"), lax.axis_index("y"), lax.axis_index("c")


def _other_chips(x, y):
    return [(1 - x, y), (x, 1 - y), (1 - x, 1 - y)]


def _half_rows(rows, which):
    return pl.ds(which * (rows // 2), rows // 2)


def _gather_layer(shards, name):
    n = len(shards)

    def body(*refs):
        w, full = refs[:n], refs[n:2 * n]
        send, recv, fsend, frecv = refs[2 * n:]
        x, y, c = _coords()
        slot = 2 * x + y
        chips = _other_chips(x, y)

        def copy(i, j, blk, src, sems, to):
            return pltpu.make_async_remote_copy(src_ref=src, dst_ref=blk, send_sem=sems[0].at[i * 3 + j],
                                                recv_sem=sems[1].at[i * 3 + j], device_id=to, device_id_type=MESH)

        def block(i, chip_slot, core):
            return full[i].at[chip_slot, _half_rows(shards[i].shape[0], core)]

        first = []
        for i in range(n):
            for j, chip in enumerate(chips):
                first.append(copy(i, j, block(i, slot, c), w[i].at[_half_rows(shards[i].shape[0], c)], (send, recv), (*chip, c)))
        for cp in first:
            cp.start()
        passed = []
        for i in range(n):
            for j, chip in enumerate(chips):
                blk = block(i, 2 * chip[0] + chip[1], c)
                copy(i, j, blk, blk, (send, recv), (*chip, c)).wait_recv()
                cp = copy(i, j, blk, blk, (fsend, frecv), (x, y, 1 - c))
                cp.start()
                passed.append(cp)
        for i in range(n):
            for j, chip in enumerate(chips):
                blk = block(i, 2 * chip[0] + chip[1], 1 - c)
                copy(i, j, blk, blk, (fsend, frecv), (x, y, 1 - c)).wait_recv()
        for cp in first + passed:
            cp.wait_send()

    out_shape = [jax.ShapeDtypeStruct((N_CHIPS,) + s.shape, s.dtype) for s in shards]
    return pl.pallas_call(body, in_specs=[ANY] * n, out_specs=[ANY] * n, out_shape=out_shape,
                          scratch_shapes=[pltpu.SemaphoreType.DMA((3 * n,))] * 4, name=name)(*shards)


N_RECV = 7


def _scatter_layer(parts, name):
    n = len(parts)

    def body(*refs):
        p, out = refs[:n], refs[n:2 * n]
        send, recv = refs[2 * n:]
        x, y, c = _coords()
        slot = 2 * x + y
        chips = _other_chips(x, y)
        sends = []
        for i in range(n):
            rows = parts[i].shape[1]
            for j, chip in enumerate(chips):
                for core in (0, 1):
                    sends.append(pltpu.make_async_remote_copy(
                        src_ref=p[i].at[2 * chip[0] + chip[1], _half_rows(rows, core)], dst_ref=out[i].at[2 * j + c],
                        send_sem=send.at[i * N_RECV + 2 * j + core], recv_sem=recv.at[i * N_RECV + 2 * j + c],
                        device_id=(*chip, core), device_id_type=MESH))
            sends.append(pltpu.make_async_remote_copy(
                src_ref=p[i].at[slot, _half_rows(rows, 1 - c)], dst_ref=out[i].at[6], send_sem=send.at[i * N_RECV + 6],
                recv_sem=recv.at[i * N_RECV + 6], device_id=(x, y, 1 - c), device_id_type=MESH))
        for cp in sends:
            cp.start()
        for i in range(n):
            for k in range(N_RECV):
                pltpu.make_async_remote_copy(src_ref=out[i].at[k], dst_ref=out[i].at[k], send_sem=send.at[0],
                                             recv_sem=recv.at[i * N_RECV + k], device_id=(x, y, c), device_id_type=MESH).wait_recv()
        for cp in sends:
            cp.wait_send()

    out_shape = [jax.ShapeDtypeStruct((N_RECV, a.shape[1] // 2, a.shape[2]), a.dtype) for a in parts]
    return pl.pallas_call(body, in_specs=[ANY] * n, out_specs=[ANY] * n, out_shape=out_shape,
                          scratch_shapes=[pltpu.SemaphoreType.DMA((N_RECV * n,))] * 2, name=name)(*parts)


def _sum_partials(own, parts, name):
    r, wd = own.shape
    tm = next(t for t in (256, 128, 64, 32, 16) if r % t == 0)

    def body(own_ref, p_ref, o_ref):
        acc = own_ref[...].astype(F32)
        for k in range(N_RECV):
            acc = acc + p_ref[k].astype(F32)
        o_ref[...] = acc

    return pl.pallas_call(
        body, grid=(r // tm,),
        in_specs=[pl.BlockSpec((tm, wd), lambda i: (i, 0)), pl.BlockSpec((N_RECV, tm, wd), lambda i: (0, i, 0))],
        out_specs=pl.BlockSpec((tm, wd), lambda i: (i, 0)), out_shape=jax.ShapeDtypeStruct((r, wd), F32),
        name=name, compiler_params=_params(("parallel",)))(own, parts)


def _exchange_halves(reduced, name):
    n = len(reduced)

    def body(*refs):
        r, out = refs[:n], refs[n:2 * n]
        send, recv = refs[2 * n:]
        x, y, c = _coords()
        sib = [pltpu.make_async_remote_copy(src_ref=r[i], dst_ref=out[i], send_sem=send.at[i], recv_sem=recv.at[i],
                                            device_id=(x, y, 1 - c), device_id_type=MESH) for i in range(n)]
        for cp in sib:
            cp.start()
        for cp in sib:
            cp.wait_recv()
        for cp in sib:
            cp.wait_send()

    out_shape = [jax.ShapeDtypeStruct(a.shape, a.dtype) for a in reduced]
    return pl.pallas_call(body, in_specs=[ANY] * n, out_specs=[ANY] * n, out_shape=out_shape,
                          scratch_shapes=[pltpu.SemaphoreType.DMA((n,))] * 2, name=name)(*reduced)


def _reduce_layer(parts, tag):
    x, y, c = _coords()
    slot = 2 * x + y
    recv = _scatter_layer(parts, name=tag + "_scatter")
    halves = []
    for i, (p, r) in enumerate(zip(parts, recv)):
        half = p.shape[1] // 2
        own = lax.dynamic_slice(p, (slot, c * half, 0), (1, half, p.shape[2]))[0]
        halves.append(_sum_partials(own, r, name=f"{tag}_sum{i}"))
    theirs = _exchange_halves(halves, name=tag + "_exchange")
    return [jnp.where(c == 0, jnp.concatenate([h, t], axis=0), jnp.concatenate([t, h], axis=0)) for h, t in zip(halves, theirs)]


def _all_reduce_small(rows):
    r = rows.shape[0]

    def body(x_ref, o_ref, buf, send, recv):
        x, y, c = _coords()
        me = 4 * x + 2 * y + c
        buf[me] = x_ref[...]
        copies = []
        for k in range(1, 8):
            peer = (x ^ (k >> 2), y ^ ((k >> 1) & 1), c ^ (k & 1))
            cp = pltpu.make_async_remote_copy(src_ref=x_ref, dst_ref=buf.at[me], send_sem=send.at[k - 1], recv_sem=recv.at[me],
                                              device_id=peer, device_id_type=MESH)
            cp.start()
            copies.append(cp)
        for k in range(1, 8):
            src = 4 * (x ^ (k >> 2)) + 2 * (y ^ ((k >> 1) & 1)) + (c ^ (k & 1))
            pltpu.make_async_remote_copy(src_ref=x_ref, dst_ref=buf.at[src], send_sem=send.at[0], recv_sem=recv.at[src],
                                         device_id=(x, y, c), device_id_type=MESH).wait_recv()
        for cp in copies:
            cp.wait_send()
        acc = buf[0]
        for k in range(1, 8):
            acc = acc + buf[k]
        o_ref[...] = acc

    vm = pl.BlockSpec(memory_space=pltpu.VMEM)
    return pl.pallas_call(
        body, in_specs=[vm], out_specs=vm, out_shape=jax.ShapeDtypeStruct(rows.shape, F32),
        scratch_shapes=[pltpu.VMEM((8, r, D_MODEL), F32), pltpu.SemaphoreType.DMA((7,)), pltpu.SemaphoreType.DMA((8,))],
        name="all_reduce_small")(rows)


def _adamw_math(w, g, m, v):
    m = ADAM_B1 * m + (1.0 - ADAM_B1) * g
    v = ADAM_B2 * v + (1.0 - ADAM_B2) * (g * g)
    m_hat = m / (1.0 - ADAM_B1 ** ADAM_STEP)
    v_hat = v / (1.0 - ADAM_B2 ** ADAM_STEP)
    return -ADAM_LR * (m_hat / (jnp.sqrt(v_hat) + ADAM_EPS) + ADAM_WD * w), m, v


def _adamw(w, g, m, v, name):
    shape = w.shape
    cols = shape[-1]
    flat = lambda a: a.reshape(-1, cols)
    rows = flat(w).shape[0]
    tm = 128 if rows % 128 == 0 else rows
    ins = [('t', flat(a), cols, 0) for a in (w, g, m, v)]
    res = _ew(_adamw_math, ins, [('t', cols, F32)] * 3, rows=rows, tm=tm, name=name)
    return [a.reshape(shape) for a in res]


def _small_update(sums, logits, w, m, v):
    def body(s_ref, lg_ref, w_ref, m_ref, v_ref, g_ref, d_ref, nm_ref, nv_ref):
        s = s_ref[...]
        l0, l1 = lg_ref[0:1, :], lg_ref[1:2, :]
        mx = jnp.maximum(l0, l1)
        e0, e1 = jnp.exp(l0 - mx), jnp.exp(l1 - mx)
        sm0, sm1 = e0 / (e0 + e1), e1 / (e0 + e1)
        dl1 = s_ref[SMALL_ROWS + 2:SMALL_ROWS + 3, :] * sm0 * sm1
        row = lax.broadcasted_iota(jnp.int32, s.shape, 0)
        g = jnp.where(row == 2, -dl1, jnp.where(row == SMALL_ROWS + 2, dl1, s))
        d, nm, nv = _adamw_math(w_ref[...], g, m_ref[...], v_ref[...])
        g_ref[...] = g
        d_ref[...] = d
        nm_ref[...] = nm
        nv_ref[...] = nv

    vm = pl.BlockSpec(memory_space=pltpu.VMEM)
    return pl.pallas_call(body, in_specs=[vm] * 5, out_specs=[vm] * 4,
                          out_shape=[jax.ShapeDtypeStruct(sums.shape, F32)] * 4, name="small_update")(sums, logits, w, m, v)


def _pack_small(vals):
    rows = []
    for l in range(2):
        for n in ("ffn1_norm", "mix_norm", "hgrn_lb_logits", "hgrn_out_norm", "attn_q_norm", "attn_k_norm", "ffn2_norm"):
            a = vals[n][l].reshape(1, -1)
            rows.append(jnp.pad(a, ((0, 0), (0, D_MODEL - a.shape[1]))))
        rows.append(jnp.zeros((SMALL_ROWS - 7, D_MODEL), F32))
    return jnp.concatenate(rows, axis=0)


def _unpack_small(packed):
    out = {}
    for k, n in enumerate(("ffn1_norm", "mix_norm", "hgrn_lb_logits", "hgrn_out_norm", "attn_q_norm", "attn_k_norm", "ffn2_norm")):
        a = jnp.stack([packed[k], packed[SMALL_ROWS + k]], axis=0)
        out[n] = a[:, :ATT_GROUPS * HEAD].reshape(2, ATT_GROUPS, HEAD) if n.startswith("attn") else a
    return out


def kernel(x, ffn1_norm, ffn1_w_in, ffn1_w_out, mix_norm, w_in, hgrn_lb_logits, hgrn_out_norm, attn_q_norm, attn_k_norm, w_branch_a, w_branch_b, w_out, ffn2_norm, ffn2_w_in, ffn2_w_out, loss_target, m_ffn1_norm, m_ffn1_w_in, m_ffn1_w_out, m_mix_norm, m_w_in, m_hgrn_lb_logits, m_hgrn_out_norm, m_attn_q_norm, m_attn_k_norm, m_w_branch_a, m_w_branch_b, m_w_out, m_ffn2_norm, m_ffn2_w_in, m_ffn2_w_out, v_ffn1_norm, v_ffn1_w_in, v_ffn1_w_out, v_mix_norm, v_w_in, v_hgrn_lb_logits, v_hgrn_out_norm, v_attn_q_norm, v_attn_k_norm, v_w_branch_a, v_w_branch_b, v_w_out, v_ffn2_norm, v_ffn2_w_in, v_ffn2_w_out):
    a = locals()
    w = {n: a[n] for n in WEIGHTS}
    m = {n: a["m_" + n] for n in WEIGHTS}
    v = {n: a["v_" + n] for n in WEIGHTS}

    slot = 2 * lax.axis_index("x") + lax.axis_index("y")
    shards = {n: w[n].astype(BF16) for n in BIG}
    fulls = []
    for l in range(2):
        got = _gather_layer([shards[n][l] for n in BIG], name=f"gather_l{l}")
        fulls.append({n: lax.dynamic_update_slice(g, shards[n][l][None], (slot, 0, 0)) for n, g in zip(BIG, got)})
    small = {n: w[n] for n in SMALL}
    sq, grad_x, big_grads, small_rows = _local_step(x[0], loss_target[0], fulls, small)
    loss = lax.psum(sq, ("x", "y", "c")) * (0.5 / D_MODEL)

    reduced = [_reduce_layer([big_grads[l][n] for n in BIG], f"reduce_l{l}") for l in range(2)]
    grads = {n: jnp.stack([reduced[0][i], reduced[1][i]], axis=0) for i, n in enumerate(BIG)}

    sums = _all_reduce_small(small_rows)
    g_s, d_s, m_s, v_s = _small_update(sums, w["hgrn_lb_logits"], _pack_small(small), _pack_small({n: m[n] for n in SMALL}),
                                       _pack_small({n: v[n] for n in SMALL}))
    grads.update(_unpack_small(g_s))
    delta, new_m, new_v = _unpack_small(d_s), _unpack_small(m_s), _unpack_small(v_s)
    for n in BIG:
        delta[n], new_m[n], new_v[n] = _adamw(w[n], grads[n], m[n], v[n], name="adamw_" + n)

    return (loss, grad_x[None], *[grads[n] for n in WEIGHTS], *[delta[n] for n in WEIGHTS],
            *[new_m[n] for n in WEIGHTS], *[new_v[n] for n in WEIGHTS])
```

```python
import functools

import jax
import jax.numpy as jnp
from jax import lax
from jax.experimental import pallas as pl
from jax.experimental.pallas import tpu as pltpu

F32 = jnp.float32
BF16 = jnp.bfloat16
MESH = pl.DeviceIdType.MESH

D_MODEL = 1024
D_FF = 2816
N_CHIPS = 4
HEAD = 128
HG_HEADS = 8
HG_CHUNK = 64
ATT_GROUPS = 3
ATT_HEADS = 4
ATT_GW = ATT_HEADS * HEAD
DILATIONS = (1, 4, 16)
ATT_BLK = 128
ATT_STEP_BLOCKS = 4
P_IN = 10752
CB_AQ, CB_AK, CB_AV, CB_GA, CB_GB = 8, 11, 14, 17, 19
EPS = 1e-6
ROPE_THETA = 10000.0
ADAM_LR, ADAM_B1, ADAM_B2, ADAM_EPS, ADAM_WD, ADAM_STEP = 0.001, 0.9, 0.999, 1e-08, 0.01, 10
VMEM_LIMIT_V7X = 56 * 1024 * 1024
NEG = -1e30


def _params(sem):
    return pltpu.CompilerParams(dimension_semantics=sem, vmem_limit_bytes=VMEM_LIMIT_V7X)


def _sig(x):
    return 1.0 / (1.0 + jnp.exp(-x))


def _dot(a, b):
    return jnp.dot(a, b, preferred_element_type=F32)


def _dot_nt(a, b):
    return lax.dot_general(a, b, (((1,), (1,)), ((), ())), preferred_element_type=F32)


def _dot_tn(a, b):
    return lax.dot_general(a, b, (((0,), (0,)), ((), ())), preferred_element_type=F32)


def _bf(x):
    return x.astype(BF16)


def _mm_nn(a, b3, *, name, tm, tn, out_dtype, res=None, alpha=1.0):
    m, k = a.shape
    nb, _, nw = b3.shape
    per = nw // tn
    assert nw % tn == 0 and m % tm == 0
    has_res = res is not None

    def body(*refs):
        if has_res:
            a_ref, b_ref, r_ref, o_ref = refs
        else:
            a_ref, b_ref, o_ref = refs
        acc = _dot(_bf(a_ref[...]), b_ref[...])
        if alpha != 1.0:
            acc = alpha * acc
        if has_res:
            acc = r_ref[...] + acc
        o_ref[...] = acc.astype(o_ref.dtype)

    in_specs = [pl.BlockSpec((tm, k), lambda i, j: (i, 0)),
                pl.BlockSpec((None, k, tn), lambda i, j: (j // per, 0, j % per))]
    args = [a, b3]
    if has_res:
        in_specs.append(pl.BlockSpec((tm, tn), lambda i, j: (i, j)))
        args.append(res)
    return pl.pallas_call(
        body, grid=(m // tm, nb * per), in_specs=in_specs,
        out_specs=pl.BlockSpec((tm, tn), lambda i, j: (i, j)),
        out_shape=jax.ShapeDtypeStruct((m, nb * nw), out_dtype),
        name=name, compiler_params=_params(("parallel", "arbitrary")))(*args)


def _mm_nt(d, b3, *, name, tm, tp, tn, out_dtype, alpha=1.0):
    m, n = d.shape
    nb, p, nw = b3.shape
    per = nw // tn
    nk = n // tn
    assert nb * nw == n and nw % tn == 0 and p % tp == 0 and m % tm == 0

    def body(d_ref, b_ref, o_ref, acc_ref):
        kk = pl.program_id(2)

        @pl.when(kk == 0)
        def _():
            acc_ref[...] = jnp.zeros_like(acc_ref)

        acc_ref[...] += _dot_nt(_bf(d_ref[...]), b_ref[...])

        @pl.when(kk == nk - 1)
        def _():
            o_ref[...] = (alpha * acc_ref[...]).astype(o_ref.dtype)

    return pl.pallas_call(
        body, grid=(m // tm, p // tp, nk),
        in_specs=[pl.BlockSpec((tm, tn), lambda i, j, kk: (i, kk)),
                  pl.BlockSpec((None, tp, tn), lambda i, j, kk: (kk // per, j, kk % per))],
        out_specs=pl.BlockSpec((tm, tp), lambda i, j, kk: (i, j)),
        out_shape=jax.ShapeDtypeStruct((m, p), out_dtype),
        scratch_shapes=[pltpu.VMEM((tm, tp), F32)],
        name=name, compiler_params=_params(("parallel", "parallel", "arbitrary")))(d, b3)


def _mm_tn(a, d, *, nb, name, tm, tk, tn, alpha=1.0):
    m, k = a.shape
    _, n = d.shape
    nw = n // nb
    per = nw // tn
    nm = m // tm
    assert nw % tn == 0 and k % tk == 0 and m % tm == 0

    def body(a_ref, d_ref, o_ref, acc_ref):
        mm = pl.program_id(2)

        @pl.when(mm == 0)
        def _():
            acc_ref[...] = jnp.zeros_like(acc_ref)

        acc_ref[...] += _dot_tn(_bf(a_ref[...]), _bf(d_ref[...]))

        @pl.when(mm == nm - 1)
        def _():
            o_ref[...] = (alpha * acc_ref[...]).astype(o_ref.dtype)

    return pl.pallas_call(
        body, grid=(k // tk, nb * per, nm),
        in_specs=[pl.BlockSpec((tm, tk), lambda i, j, mm: (mm, i)),
                  pl.BlockSpec((tm, tn), lambda i, j, mm: (mm, j))],
        out_specs=pl.BlockSpec((None, tk, tn), lambda i, j, mm: (j // per, i, j % per)),
        out_shape=jax.ShapeDtypeStruct((nb, k, nw), BF16),
        scratch_shapes=[pltpu.VMEM((tk, tn), F32)],
        name=name, compiler_params=_params(("parallel", "parallel", "arbitrary")))(a, d)


def _ew(fn, ins, outs, *, rows, tm, name):
    in_specs, args = [], []
    for s in ins:
        if s[0] == 't':
            _, arr, w, cb = s
            in_specs.append(pl.BlockSpec((tm, w), lambda i, cb=cb: (i, cb)))
        else:
            arr = s[1]
            in_specs.append(pl.BlockSpec(arr.shape, lambda i, nd=arr.ndim: (0,) * nd))
        args.append(arr)
    out_specs, out_shape = [], []
    for s in outs:
        if s[0] == 't':
            _, w, dt = s
            out_specs.append(pl.BlockSpec((tm, w), lambda i: (i, 0)))
            out_shape.append(jax.ShapeDtypeStruct((rows, w), dt))
        else:
            out_specs.append(pl.BlockSpec(s[1], lambda i: (0, 0)))
            out_shape.append(jax.ShapeDtypeStruct(s[1], F32))
    n_in = len(ins)

    def body(*refs):
        res = fn(*[r[...] for r in refs[:n_in]])
        if not isinstance(res, (tuple, list)):
            res = (res,)
        for r, s, v in zip(refs[n_in:], outs, res):
            if s[0] == 't':
                r[...] = v.astype(r.dtype)
            else:
                @pl.when(pl.program_id(0) == 0)
                def _(r=r):
                    r[...] = jnp.zeros_like(r)

                r[...] += v

    res = pl.pallas_call(
        body, grid=(rows // tm,), in_specs=in_specs, out_specs=out_specs, out_shape=out_shape,
        name=name, compiler_params=_params(("arbitrary",)))(*args)
    return res


def _heads(x):
    return [x[:, h * HEAD:(h + 1) * HEAD] for h in range(x.shape[1] // HEAD)]


def _cat(xs):
    return jnp.concatenate(xs, axis=1)


def _head_mean(x):
    return _cat([jnp.broadcast_to(jnp.mean(h, axis=1, keepdims=True), h.shape) for h in _heads(x)])


def _rms_rows(x):
    return lax.rsqrt(jnp.mean(x * x, axis=1, keepdims=True) + EPS)


def _norm_fwd(x, g, name):
    return _ew(lambda xv, gv: xv * _rms_rows(xv) * gv,
               [('t', x, D_MODEL, 0), ('f', g)], [('t', D_MODEL, BF16)], rows=x.shape[0], tm=512, name=name)[0]


def _norm_bwd(dh, x, g, dx, name):
    def fn(dhv, xv, gv, dxv):
        r = _rms_rows(xv)
        xh = xv * r
        dxh = dhv * gv
        out = dxv + r * (dxh - xh * jnp.mean(dxh * xh, axis=1, keepdims=True))
        return out, jnp.sum(dhv * xh, axis=0, keepdims=True)

    return _ew(fn, [('t', dh, D_MODEL, 0), ('t', x, D_MODEL, 0), ('f', g), ('t', dx, D_MODEL, 0)],
               [('t', D_MODEL, F32), ('acc', (1, D_MODEL))], rows=x.shape[0], tm=512, name=name)


def _loss_fwd_bwd(y, target, name):
    def fn(yv, tv):
        e = yv - tv
        return e * (1.0 / D_MODEL), jnp.sum(e * e, axis=0, keepdims=True)

    return _ew(fn, [('t', y, D_MODEL, 0), ('t', target, D_MODEL, 0)], [('t', D_MODEL, F32), ('acc', (1, D_MODEL))],
               rows=y.shape[0], tm=512, name=name)


def _gate_fwd(proj, ya, yb, name):
    def fn(ga0, ga1, gb0, gb1, yav, ybv):
        return _sig(_cat([ga0, ga1])) * yav + _sig(_cat([gb0, gb1])) * ybv

    ins = [('t', proj, 512, CB_GA), ('t', proj, 512, CB_GA + 1), ('t', proj, 512, CB_GB), ('t', proj, 512, CB_GB + 1),
           ('t', ya, D_MODEL, 0), ('t', yb, D_MODEL, 0)]
    return _ew(fn, ins, [('t', D_MODEL, BF16)], rows=ya.shape[0], tm=512, name=name)[0]


def _gate_bwd(dm, proj, ya, yb, name):
    def fn(dmv, ga0, ga1, gb0, gb1, yav, ybv):
        sa = _sig(_cat([ga0, ga1]))
        sb = _sig(_cat([gb0, gb1]))
        return dmv * sa, dmv * sb, _cat([dmv * yav * sa * (1.0 - sa), dmv * ybv * sb * (1.0 - sb)])

    ins = [('t', dm, D_MODEL, 0),
           ('t', proj, 512, CB_GA), ('t', proj, 512, CB_GA + 1), ('t', proj, 512, CB_GB), ('t', proj, 512, CB_GB + 1),
           ('t', ya, D_MODEL, 0), ('t', yb, D_MODEL, 0)]
    return _ew(fn, ins, [('t', D_MODEL, BF16), ('t', D_MODEL, BF16), ('t', 2 * D_MODEL, BF16)],
               rows=ya.shape[0], tm=512, name=name)


def _rot(x):
    sgn = jnp.where(lax.broadcasted_iota(jnp.int32, x.shape, 1) < HEAD // 2, -1.0, 1.0)
    return pltpu.roll(x, HEAD // 2, 1) * sgn


def _gain_rows(qn, kn):
    return [a[g:g + 1] for a in (qn, kn) for g in range(ATT_GROUPS)]


def _qk_fwd(proj, cos, sin, qn, kn, name):
    def fn(*v):
        xs, cosv, sinv, gains, vs = v[:6], v[6], v[7], v[8:14], v[14:17]
        outs = []
        for j, x in enumerate(xs):
            gain = gains[j]
            ys = []
            for xh in _heads(x):
                xn = xh * _rms_rows(xh) * gain
                ys.append(xn * cosv + _rot(xn) * sinv)
            outs.append(_cat(ys))
        return outs + list(vs)

    ins = ([('t', proj, 512, CB_AQ + j) for j in range(6)] + [('t', cos, HEAD, 0), ('t', sin, HEAD, 0)]
           + [('f', a) for a in _gain_rows(qn, kn)] + [('t', proj, 512, CB_AV + g) for g in range(ATT_GROUPS)])
    return _ew(fn, ins, [('t', ATT_GW, BF16)] * 9, rows=proj.shape[0], tm=512, name=name)


def _qk_bwd(dqk, proj, cos, sin, qn, kn, name):
    def fn(*v):
        ds, xs, cosv, sinv, gains = v[:6], v[6:12], v[12], v[13], v[14:20]
        rows8 = lax.broadcasted_iota(jnp.int32, (8, HEAD), 0)
        outs, dgs = [], [jnp.zeros((8, HEAD), F32)] * 2
        for j in range(6):
            gain = gains[j]
            dx, dg = [], jnp.zeros((1, HEAD), F32)
            for dyh, xh in zip(_heads(ds[j]), _heads(xs[j])):
                r = _rms_rows(xh)
                xhat = xh * r
                dxn = dyh * cosv - _rot(dyh * sinv)
                dg = dg + jnp.sum(dxn * xhat, axis=0, keepdims=True)
                dxh = dxn * gain
                dx.append(r * (dxh - xhat * jnp.mean(dxh * xhat, axis=1, keepdims=True)))
            outs.append(_cat(dx))
            dgs[j // 3] = dgs[j // 3] + jnp.where(rows8 == j % 3, dg, 0.0)
        return _cat(outs), dgs[0], dgs[1]

    ins = ([('t', a, ATT_GW, 0) for a in dqk] + [('t', proj, 512, CB_AQ + j) for j in range(6)]
           + [('t', cos, HEAD, 0), ('t', sin, HEAD, 0)] + [('f', a) for a in _gain_rows(qn, kn)])
    return _ew(fn, ins, [('t', 6 * ATT_GW, BF16), ('acc', (8, HEAD)), ('acc', (8, HEAD))],
               rows=proj.shape[0], tm=256, name=name)


def _merge_fwd(outs, lses, name):
    def fn(o0, o1, o2, l0, l1, l2):
        m = jnp.maximum(jnp.maximum(l0, l1), l2)
        e0, e1, e2 = jnp.exp(l0 - m), jnp.exp(l1 - m), jnp.exp(l2 - m)
        return (e0 * o0 + e1 * o1 + e2 * o2) / (e0 + e1 + e2)

    ins = [('t', a, ATT_GW, 0) for a in list(outs) + list(lses)]
    return _ew(fn, ins, [('t', ATT_GW, BF16)], rows=outs[0].shape[0], tm=512, name=name)[0]


def _merge_bwd(dob, outs, lses, name):
    def fn(dov, o0, o1, o2, l0, l1, l2):
        m = jnp.maximum(jnp.maximum(l0, l1), l2)
        e0, e1, e2 = jnp.exp(l0 - m), jnp.exp(l1 - m), jnp.exp(l2 - m)
        inv = 1.0 / (e0 + e1 + e2)
        a0, a1, a2 = e0 * inv, e1 * inv, e2 * inv
        ob = a0 * o0 + a1 * o1 + a2 * o2
        s = _head_mean(dov * ob) * float(HEAD)
        return a0 * dov, a1 * dov, a2 * dov, a0 * s, a1 * s, a2 * s

    ins = [('t', dob, ATT_GW, 0)] + [('t', a, ATT_GW, 0) for a in list(outs) + list(lses)]
    return _ew(fn, ins, [('t', ATT_GW, BF16)] * 3 + [('t', ATT_GW, F32)] * 3, rows=dob.shape[0], tm=512, name=name)


def _assemble_dproj(dh4, dqk, dvs, dgab, name):
    fn = lambda *v: _cat(list(v))
    ins = [('t', dh4, 4 * D_MODEL, 0), ('t', dqk, 6 * ATT_GW, 0)] + [('t', a, ATT_GW, 0) for a in dvs] + [('t', dgab, 2 * D_MODEL, 0)]
    return _ew(fn, ins, [('t', P_IN, BF16)], rows=dh4.shape[0], tm=256, name=name)[0]


HG_ROWS = 256


def _hg_gates(hq, hf, hi, lbv):
    sig = _sig(hf)
    f = lbv + (1.0 - lbv) * sig
    return hq * _sig(hq), 1.0 - f, hi, jnp.log(f), sig, f


def _split3(x):
    hi = _bf(x)
    r1 = x - hi.astype(F32)
    mid = _bf(r1)
    return hi, mid, _bf(r1 - mid.astype(F32))


def _tri_dot(tri, x):
    hi, mid, lo = _split3(x)
    return _dot(tri, hi) + _dot(tri, mid) + _dot(tri, lo)


def _row(x, i):
    rows = lax.broadcasted_iota(jnp.int32, x.shape, 0)
    return jnp.sum(jnp.where(rows == i, x, 0.0), axis=0, keepdims=True)


def _hg_decay(logf, q, k):
    c = HG_CHUNK
    row = lax.broadcasted_iota(jnp.int32, (c, c), 0)
    col = lax.broadcasted_iota(jnp.int32, (c, c), 1)
    g = _tri_dot((row >= col).astype(BF16), logf)
    gm = _row(g, c // 2 - 1)
    gl = _row(g, c - 1)
    return g, gm, gl, q * jnp.exp(g), q * jnp.exp(g - gm), k * jnp.exp(gm - g), k * jnp.exp(gl - g)


def _hg_out_fwd(o, hg, gain):
    r = lax.rsqrt(_head_mean(o * o) + EPS)
    return o * r * gain * (hg * _sig(hg))


def _hgrn_fwd(proj, lb, gain, name):
    t = proj.shape[0]
    nck = HG_ROWS // HG_CHUNK

    def body(hq_ref, hf_ref, hi_ref, hg_ref, lb_ref, gn_ref, o_ref, oa_ref, sall_ref, st_ref):
        @pl.when(pl.program_id(0) == 0)
        def _():
            st_ref[...] = jnp.zeros_like(st_ref)

        lbv = lb_ref[...]
        gnv = gn_ref[...]
        c = HG_CHUNK
        mask = lax.broadcasted_iota(jnp.int32, (c, c), 0) >= lax.broadcasted_iota(jnp.int32, (c, c), 1)

        def chunk(cc, carry):
            sl = pl.ds(pl.multiple_of(cc * c, c), c)
            q, k, v, logf, _, _ = _hg_gates(hq_ref[sl, :], hf_ref[sl, :], hi_ref[sl, :], lbv)
            _, _, gl, qg, qt, kt, kd = _hg_decay(logf, q, k)
            egl = jnp.exp(gl)
            os = []
            for h in range(HG_HEADS):
                hs = slice(h * HEAD, (h + 1) * HEAD)
                st = st_ref[h]
                sall_ref[cc, h] = st
                a = jnp.where(mask, _dot_nt(_bf(qt[:, hs]), _bf(kt[:, hs])), 0.0)
                os.append(_dot(_bf(a), _bf(v[:, hs])) + _dot_nt(_bf(qg[:, hs]), _bf(st)))
                st_ref[h] = egl[:, hs] * st + _dot_tn(_bf(v[:, hs]), _bf(kd[:, hs]))
            o = _cat(os)
            o_ref[sl, :] = o
            oa_ref[sl, :] = _hg_out_fwd(o, hg_ref[sl, :], gnv).astype(oa_ref.dtype)
            return carry

        lax.fori_loop(0, nck, chunk, 0)

    col = lambda j: pl.BlockSpec((HG_ROWS, D_MODEL), lambda i, j=j: (i, j))
    small = pl.BlockSpec((1, D_MODEL), lambda i: (0, 0))
    return pl.pallas_call(
        body, grid=(t // HG_ROWS,),
        in_specs=[col(0), col(1), col(2), col(3), small, small],
        out_specs=[col(0), col(0), pl.BlockSpec((nck, HG_HEADS, HEAD, HEAD), lambda i: (i, 0, 0, 0))],
        out_shape=[jax.ShapeDtypeStruct((t, D_MODEL), F32), jax.ShapeDtypeStruct((t, D_MODEL), BF16),
                   jax.ShapeDtypeStruct((t // HG_CHUNK, HG_HEADS, HEAD, HEAD), F32)],
        scratch_shapes=[pltpu.VMEM((HG_HEADS, HEAD, HEAD), F32)],
        name=name, compiler_params=_params(("arbitrary",)))(proj, proj, proj, proj, lb, gain)


def _terms(x, precise):
    hi = _bf(x)
    return (hi, _bf(x - hi.astype(F32))) if precise else (hi,)


def _mm(dot, a, b):
    out = dot(a[0], b[0])
    if len(a) > 1:
        out = out + dot(a[1], b[0])
    if len(b) > 1:
        out = out + dot(a[0], b[1])
    return out


def _hgrn_bwd(doa, oscan, proj, sall, lb, gain, name, precise):
    t = proj.shape[0]
    nck = HG_ROWS // HG_CHUNK
    nsteps = t // HG_ROWS
    terms = functools.partial(_terms, precise=precise)

    def body(doa_ref, os_ref, hq_ref, hf_ref, hi_ref, hg_ref, sall_ref, lb_ref, gn_ref,
             d4_ref, dgn_ref, dlb_ref, dst_ref):
        @pl.when(pl.program_id(0) == 0)
        def _():
            dst_ref[...] = jnp.zeros_like(dst_ref)
            dgn_ref[...] = jnp.zeros_like(dgn_ref)
            dlb_ref[...] = jnp.zeros_like(dlb_ref)

        lbv = lb_ref[...]
        gnv = gn_ref[...]
        c = HG_CHUNK
        row = lax.broadcasted_iota(jnp.int32, (c, c), 0)
        colm = lax.broadcasted_iota(jnp.int32, (c, c), 1)
        mask = row >= colm
        triu = (row <= colm).astype(BF16)
        last = lax.broadcasted_iota(jnp.int32, (c, HEAD), 0) == c - 1

        def chunk(ci, carry):
            cc = nck - 1 - ci
            sl = pl.ds(pl.multiple_of(cc * c, c), c)
            hq, hf, hg = hq_ref[sl, :], hf_ref[sl, :], hg_ref[sl, :]
            q, k, v, logf, sig, f = _hg_gates(hq, hf, hi_ref[sl, :], lbv)
            g, gm, gl, qg, qt, kt, kd = _hg_decay(logf, q, k)
            egl = jnp.exp(gl)
            o = os_ref[sl, :]
            dy = doa_ref[sl, :]
            r = lax.rsqrt(_head_mean(o * o) + EPS)
            oh = o * r
            sg = _sig(hg)
            silu_g = hg * sg
            dgn_ref[...] += jnp.sum(dy * oh * silu_g, axis=0, keepdims=True)
            dhg = dy * oh * gnv * (sg * (1.0 + hg * (1.0 - sg)))
            doh = dy * gnv * silu_g
            do = r * (doh - oh * _head_mean(doh * oh))
            dqs, dks, dvs, dgs = [], [], [], []
            for h in range(HG_HEADS):
                hs = slice(h * HEAD, (h + 1) * HEAD)
                st = sall_ref[cc, h]
                dst = dst_ref[h]
                qt_h, kt_h, qg_h, kd_h = qt[:, hs], kt[:, hs], qg[:, hs], kd[:, hs]
                do_p, v_p, qt_p, kt_p, qg_p = terms(do[:, hs]), terms(v[:, hs]), terms(qt_h), terms(kt_h), terms(qg_h)
                st_p, dst_p = terms(st), terms(dst)
                a = jnp.where(mask, _dot_nt(qt_p[0], kt_p[0]), 0.0)
                da = terms(jnp.where(mask, _mm(_dot_nt, do_p, v_p), 0.0))
                dqt = _mm(_dot, da, kt_p)
                dkt = _mm(_dot_tn, da, qt_p)
                dqg = _mm(_dot, do_p, st_p)
                dv = _dot_tn(_bf(a), do_p[0]) + _dot_nt(_bf(kd_h), dst_p[0])
                dkd = _mm(_dot, v_p, dst_p)
                dgl = egl[:, hs] * jnp.sum(st * dst, axis=0, keepdims=True) + jnp.sum(dkd * kd_h, axis=0, keepdims=True)
                dst_ref[h] = egl[:, hs] * dst + _mm(_dot_tn, do_p, qg_p)
                g_h = g[:, hs]
                gm_h = gm[:, hs]
                gl_h = gl[:, hs]
                dqs.append(dqt * jnp.exp(g_h - gm_h) + dqg * jnp.exp(g_h))
                dks.append(dkt * jnp.exp(gm_h - g_h) + dkd * jnp.exp(gl_h - g_h))
                dvs.append(dv)
                dgs.append(dqt * qt_h - dkt * kt_h + dqg * qg_h - dkd * kd_h + jnp.where(last, dgl, 0.0))
            dq, dk, dv, dg = _cat(dqs), _cat(dks), _cat(dvs), _cat(dgs)
            dlogf = _tri_dot(triu, dg)
            df = dlogf / f - dk
            dlb_ref[...] += jnp.sum(df * (1.0 - sig), axis=0, keepdims=True)
            dhf = df * (1.0 - lbv) * sig * (1.0 - sig)
            sq = _sig(hq)
            dhq = dq * (sq * (1.0 + hq * (1.0 - sq)))
            d4_ref[sl, :] = _cat([dhq, dhf, dv, dhg]).astype(d4_ref.dtype)
            return carry

        lax.fori_loop(0, nck, chunk, 0)

    rev = lambda j: pl.BlockSpec((HG_ROWS, D_MODEL), lambda i, j=j: (nsteps - 1 - i, j))
    small = pl.BlockSpec((1, D_MODEL), lambda i: (0, 0))
    return pl.pallas_call(
        body, grid=(nsteps,),
        in_specs=[rev(0), rev(0), rev(0), rev(1), rev(2), rev(3),
                  pl.BlockSpec((nck, HG_HEADS, HEAD, HEAD), lambda i: (nsteps - 1 - i, 0, 0, 0)), small, small],
        out_specs=[pl.BlockSpec((HG_ROWS, 4 * D_MODEL), lambda i: (nsteps - 1 - i, 0)), small, small],
        out_shape=[jax.ShapeDtypeStruct((t, 4 * D_MODEL), BF16), jax.ShapeDtypeStruct((1, D_MODEL), F32),
                   jax.ShapeDtypeStruct((1, D_MODEL), F32)],
        scratch_shapes=[pltpu.VMEM((HG_HEADS, HEAD, HEAD), F32)],
        name=name, compiler_params=_params(("arbitrary",)))(doa, oscan, proj, proj, proj, proj, sall, lb, gain)


def _band_masks():
    qi = lax.broadcasted_iota(jnp.int32, (ATT_BLK, ATT_BLK), 0)
    ki = lax.broadcasted_iota(jnp.int32, (ATT_BLK, ATT_BLK), 1)
    return ki >= qi, ki <= qi


def _attn_cfg(t, g):
    d = DILATIONS[g]
    length = t // d
    nb = length // ATT_BLK
    return d, length, nb, min(ATT_STEP_BLOCKS, nb)


def _attn_fwd(qg, kg, vg, g, name):
    t = qg.shape[0]
    d, length, nb, rb = _attn_cfg(t, g)
    scale = HEAD ** -0.5

    def body(q_ref, k_ref, v_ref, kp_ref, vp_ref, o_ref, l_ref):
        n = pl.program_id(1)
        prev_m, own_m = _band_masks()
        first_m = jnp.logical_and(prev_m, n > 0)
        for h in range(ATT_HEADS):
            hs = slice(h * HEAD, (h + 1) * HEAD)
            for j in range(rb):
                rows = slice(j * ATT_BLK, (j + 1) * ATT_BLK)
                before = slice((j - 1) * ATT_BLK, j * ATT_BLK)
                q = q_ref[rows, hs]
                k0, v0, m0 = (kp_ref[:, hs], vp_ref[:, hs], first_m) if j == 0 else (k_ref[before, hs], v_ref[before, hs], prev_m)
                s0 = jnp.where(m0, _dot_nt(q, k0) * scale, NEG)
                s1 = jnp.where(own_m, _dot_nt(q, k_ref[rows, hs]) * scale, NEG)
                m = jnp.maximum(jnp.max(s0, axis=1, keepdims=True), jnp.max(s1, axis=1, keepdims=True))
                p0, p1 = jnp.exp(s0 - m), jnp.exp(s1 - m)
                l = jnp.sum(p0, axis=1, keepdims=True) + jnp.sum(p1, axis=1, keepdims=True)
                o = _dot(_bf(p0), v0) + _dot(_bf(p1), v_ref[rows, hs])
                o_ref[rows, hs] = o / l
                l_ref[rows, hs] = jnp.broadcast_to(m + jnp.log(l), (ATT_BLK, HEAD))

    own = pl.BlockSpec((rb * ATT_BLK, ATT_GW), lambda r, n: (n, r))
    prev = pl.BlockSpec((ATT_BLK, ATT_GW), lambda r, n: (jnp.maximum(n * rb - 1, 0), r))
    view = lambda a: a.reshape(length, d * ATT_GW)
    o, lse = pl.pallas_call(
        body, grid=(d, nb // rb), in_specs=[own, own, own, prev, prev], out_specs=[own, own],
        out_shape=[jax.ShapeDtypeStruct((length, d * ATT_GW), F32)] * 2,
        name=name, compiler_params=_params(("parallel", "arbitrary")))(view(qg), view(kg), view(vg), view(kg), view(vg))
    return o.reshape(t, ATT_GW), lse.reshape(t, ATT_GW)


def _attn_bwd(qg, kg, vg, dog, lse, delta, g, name):
    t = qg.shape[0]
    d, length, nb, rb = _attn_cfg(t, g)
    nsteps = nb // rb
    scale = HEAD ** -0.5

    def body(q_ref, k_ref, v_ref, do_ref, l_ref, dl_ref, kp_ref, vp_ref, qn_ref, don_ref, ln_ref, dln_ref,
             dq_ref, dk_ref, dv_ref):
        n = pl.program_id(1)
        prev_m, own_m = _band_masks()
        first_m = jnp.logical_and(prev_m, n > 0)
        next_m = jnp.logical_and(prev_m, n < nsteps - 1)
        for h in range(ATT_HEADS):
            hs = slice(h * HEAD, (h + 1) * HEAD)
            dk, dv = [None] * rb, [None] * rb
            for j in range(rb + 1):
                rows = slice(j * ATT_BLK, (j + 1) * ATT_BLK)
                before = slice((j - 1) * ATT_BLK, j * ATT_BLK)
                if j < rb:
                    q, do, lse_q, dl_q = q_ref[rows, hs], do_ref[rows, hs], l_ref[rows, hs], dl_ref[rows, hs]
                else:
                    q, do, lse_q, dl_q = qn_ref[:, hs], don_ref[:, hs], ln_ref[:, hs], dln_ref[:, hs]
                if j == 0:
                    k0, v0, m0 = kp_ref[:, hs], vp_ref[:, hs], first_m
                else:
                    k0, v0, m0 = k_ref[before, hs], v_ref[before, hs], (prev_m if j < rb else next_m)
                p0 = jnp.where(m0, jnp.exp(_dot_nt(q, k0) * scale - lse_q), 0.0)
                ds0 = _bf(p0 * (_dot_nt(do, v0) - dl_q) * scale)
                if j >= 1:
                    dk[j - 1] = dk[j - 1] + _dot_tn(ds0, q)
                    dv[j - 1] = dv[j - 1] + _dot_tn(_bf(p0), do)
                if j < rb:
                    k1, v1 = k_ref[rows, hs], v_ref[rows, hs]
                    p1 = jnp.where(own_m, jnp.exp(_dot_nt(q, k1) * scale - lse_q), 0.0)
                    ds1 = _bf(p1 * (_dot_nt(do, v1) - dl_q) * scale)
                    dq_ref[rows, hs] = _dot(ds0, k0) + _dot(ds1, k1)
                    dk[j] = _dot_tn(ds1, q)
                    dv[j] = _dot_tn(_bf(p1), do)
            for j in range(rb):
                rows = slice(j * ATT_BLK, (j + 1) * ATT_BLK)
                dk_ref[rows, hs] = dk[j]
                dv_ref[rows, hs] = dv[j].astype(dv_ref.dtype)

    own = pl.BlockSpec((rb * ATT_BLK, ATT_GW), lambda r, n: (n, r))
    prev = pl.BlockSpec((ATT_BLK, ATT_GW), lambda r, n: (jnp.maximum(n * rb - 1, 0), r))
    nxt = pl.BlockSpec((ATT_BLK, ATT_GW), lambda r, n: (jnp.minimum((n + 1) * rb, nb - 1), r))
    view = lambda a: a.reshape(length, d * ATT_GW)
    dq, dk, dv = pl.pallas_call(
        body, grid=(d, nsteps), in_specs=[own] * 6 + [prev, prev] + [nxt] * 4, out_specs=[own, own, own],
        out_shape=[jax.ShapeDtypeStruct((length, d * ATT_GW), F32), jax.ShapeDtypeStruct((length, d * ATT_GW), F32),
                   jax.ShapeDtypeStruct((length, d * ATT_GW), BF16)],
        name=name, compiler_params=_params(("parallel", "arbitrary")))(
            view(qg), view(kg), view(vg), view(dog), view(lse), view(delta), view(kg), view(vg),
            view(qg), view(dog), view(lse), view(delta))
    return dq.reshape(t, ATT_GW), dk.reshape(t, ATT_GW), dv.reshape(t, ATT_GW)


def _rope_tables(t):
    pos = jnp.arange(t, dtype=F32)
    inv = ROPE_THETA ** (-jnp.arange(0, HEAD, 2, dtype=F32) / HEAD)
    ang = pos[:, None] * inv[None, :]
    ang = jnp.concatenate([ang, ang], axis=-1)
    return jnp.cos(ang), jnp.sin(ang)


def _lower_bounds(logits):
    lb = jnp.cumsum(jax.nn.softmax(logits.astype(F32), axis=0), axis=0)
    return lb - lb[0:1]


FFN_ROWS = 256
FF_SHARD = 2 * D_FF // N_CHIPS


def _ffn_in_act(x, g, w_in, name):
    t = x.shape[0]

    def body(x_ref, g_ref, w_ref, h_ref, ab_ref, u_ref):
        xv = x_ref[...]
        h = _bf(xv * _rms_rows(xv) * g_ref[...])
        h_ref[...] = h
        for s in range(N_CHIPS // 2):
            cols = slice(s * FF_SHARD, (s + 1) * FF_SHARD)
            a = _dot(h, w_ref[s])
            b = _dot(h, w_ref[s + N_CHIPS // 2])
            ab_ref[:, cols] = a.astype(ab_ref.dtype)
            ab_ref[:, D_FF + s * FF_SHARD:D_FF + (s + 1) * FF_SHARD] = b.astype(ab_ref.dtype)
            u_ref[:, cols] = (a * _sig(a) * b).astype(u_ref.dtype)

    row = lambda w: pl.BlockSpec((FFN_ROWS, w), lambda i: (i, 0))
    return pl.pallas_call(
        body, grid=(t // FFN_ROWS,),
        in_specs=[row(D_MODEL), pl.BlockSpec((1, D_MODEL), lambda i: (0, 0)),
                  pl.BlockSpec(w_in.shape, lambda i: (0, 0, 0))],
        out_specs=[row(D_MODEL), row(2 * D_FF), row(D_FF)],
        out_shape=[jax.ShapeDtypeStruct((t, D_MODEL), BF16), jax.ShapeDtypeStruct((t, 2 * D_FF), BF16),
                   jax.ShapeDtypeStruct((t, D_FF), BF16)],
        name=name, compiler_params=_params(("parallel",)))(x, g, w_in)


def _ffn_bwd_du_act(dx, w_out, ab, name):
    t = dx.shape[0]

    def body(dx_ref, w_ref, ab_ref, o_ref):
        du = 0.5 * _dot_nt(_bf(dx_ref[...]), w_ref[0])
        a = ab_ref[:, :D_FF].astype(F32)
        b = ab_ref[:, D_FF:].astype(F32)
        s = _sig(a)
        o_ref[:, :D_FF] = (du * b * (s * (1.0 + a * (1.0 - s)))).astype(o_ref.dtype)
        o_ref[:, D_FF:] = (du * a * s).astype(o_ref.dtype)

    row = lambda w: pl.BlockSpec((FFN_ROWS, w), lambda i: (i, 0))
    return pl.pallas_call(
        body, grid=(t // FFN_ROWS,),
        in_specs=[row(D_MODEL), pl.BlockSpec(w_out.shape, lambda i: (0, 0, 0)), row(2 * D_FF)],
        out_specs=row(2 * D_FF), out_shape=jax.ShapeDtypeStruct((t, 2 * D_FF), BF16),
        name=name, compiler_params=_params(("parallel",)))(dx, w_out, ab)


def _ffn_fwd(x, g, w_in, w_out, tag):
    h, ab, u = _ffn_in_act(x, g, w_in, name=tag + "_in_act")
    y = _mm_nn(u, w_out, name=tag + "_out", tm=512, tn=D_MODEL, out_dtype=F32, res=x, alpha=0.5)
    return y, (x, h, ab, u)


def _ffn_bwd(dx, saved, g, w_in, w_out, tag):
    x, h, ab, u = saved
    g_out = _mm_tn(u, dx, nb=1, name=tag + "_bwd_wout", tm=1024, tk=1408, tn=D_MODEL, alpha=0.5)
    dab = _ffn_bwd_du_act(dx, w_out, ab, name=tag + "_bwd_du_act")
    g_in = _mm_tn(h, dab, nb=N_CHIPS, name=tag + "_bwd_win", tm=2048, tk=D_MODEL, tn=FF_SHARD)
    dh = _mm_nt(dab, w_in, name=tag + "_bwd_dh", tm=1024, tp=D_MODEL, tn=FF_SHARD, out_dtype=F32)
    dx, dg = _norm_bwd(dh, x, g, dx, name=tag + "_bwd_norm")
    return dx, dg, g_in, g_out.reshape(N_CHIPS, D_FF // N_CHIPS, D_MODEL)


def _mix_fwd(x, w, lb, cos, sin, tag):
    h = _norm_fwd(x, w["mix_norm"], name=tag + "_norm")
    proj = _mm_nn(h, w["w_in"], name=tag + "_in", tm=1024, tn=896, out_dtype=F32)
    oscan, oa, sall = _hgrn_fwd(proj, lb, w["hgrn_out_norm"], name=tag + "_hgrn")
    qk = _qk_fwd(proj, cos, sin, w["attn_q_norm"], w["attn_k_norm"], name=tag + "_qk")
    outs, lses = [], []
    for g in range(ATT_GROUPS):
        o, l = _attn_fwd(qk[g], qk[3 + g], qk[6 + g], g, name=f"{tag}_attn{g}")
        outs.append(o)
        lses.append(l)
    ob = _merge_fwd(outs, lses, name=tag + "_merge")
    ya = _mm_nn(oa, w["w_branch_a"], name=tag + "_wa", tm=512, tn=D_MODEL, out_dtype=F32)
    yb = _mm_nn(ob, w["w_branch_b"], name=tag + "_wb", tm=512, tn=256, out_dtype=F32)
    merged = _gate_fwd(proj, ya, yb, name=tag + "_gate")
    y = _mm_nn(merged, w["w_out"], name=tag + "_out", tm=512, tn=D_MODEL, out_dtype=F32, res=x)
    return y, (x, h, proj, oscan, oa, sall, qk, outs, lses, ob, ya, yb, merged)


def _mix_bwd(dx, saved, w, lb, cos, sin, tag, lb_live):
    x, h, proj, oscan, oa, sall, qk, outs, lses, ob, ya, yb, merged = saved
    dm = _mm_nt(dx, w["w_out"], name=tag + "_bwd_dm", tm=1024, tp=D_MODEL, tn=D_MODEL, out_dtype=F32)
    g_wout = _mm_tn(merged, dx, nb=1, name=tag + "_bwd_wout", tm=1024, tk=D_MODEL, tn=D_MODEL)
    dya, dyb, dgab = _gate_bwd(dm, proj, ya, yb, name=tag + "_bwd_gate")
    doa = _mm_nt(dya, w["w_branch_a"], name=tag + "_bwd_doa", tm=1024, tp=D_MODEL, tn=D_MODEL, out_dtype=F32)
    g_wa = _mm_tn(oa, dya, nb=1, name=tag + "_bwd_wa", tm=1024, tk=D_MODEL, tn=D_MODEL)
    dob = _mm_nt(dyb, w["w_branch_b"], name=tag + "_bwd_dob", tm=1024, tp=ATT_GW, tn=256, out_dtype=F32)
    g_wb = _mm_tn(ob, dyb, nb=N_CHIPS, name=tag + "_bwd_wb", tm=2048, tk=ATT_GW, tn=256)
    mb = _merge_bwd(dob, outs, lses, name=tag + "_bwd_merge")
    dqk, dvs = [None] * 6, []
    for g in range(ATT_GROUPS):
        dq, dk, dv = _attn_bwd(qk[g], qk[3 + g], qk[6 + g], mb[g], lses[g], mb[3 + g], g, name=f"{tag}_bwd_attn{g}")
        dqk[g], dqk[3 + g] = dq, dk
        dvs.append(dv)
    dqk_cols, dqn, dkn = _qk_bwd(dqk, proj, cos, sin, w["attn_q_norm"], w["attn_k_norm"], name=tag + "_bwd_qk")
    dh4, dgn, dlb = _hgrn_bwd(doa, oscan, proj, sall, lb, w["hgrn_out_norm"], name=tag + "_bwd_hgrn", precise=lb_live)
    dproj = _assemble_dproj(dh4, dqk_cols, dvs, dgab, name=tag + "_bwd_cat")
    g_win = _mm_tn(h, dproj, nb=N_CHIPS, name=tag + "_bwd_win", tm=2048, tk=D_MODEL, tn=896)
    dh = _mm_nt(dproj, w["w_in"], name=tag + "_bwd_dh", tm=1024, tp=D_MODEL, tn=2688, out_dtype=F32)
    dx, dg = _norm_bwd(dh, x, w["mix_norm"], dx, name=tag + "_bwd_norm")
    big = dict(w_in=g_win, w_branch_a=g_wa.reshape(N_CHIPS, D_MODEL // N_CHIPS, D_MODEL), w_branch_b=g_wb,
               w_out=g_wout.reshape(N_CHIPS, D_MODEL // N_CHIPS, D_MODEL))
    small = dict(mix_norm=dg, hgrn_out_norm=dgn, lb=dlb, attn_q_norm=dqn, attn_k_norm=dkn)
    return dx, big, small


BIG = ("ffn1_w_in", "ffn1_w_out", "w_in", "w_branch_a", "w_branch_b", "w_out", "ffn2_w_in", "ffn2_w_out")
ROW_SHARDED = ("ffn1_w_out", "w_branch_a", "w_out", "ffn2_w_out")
SMALL = ("ffn1_norm", "mix_norm", "hgrn_lb_logits", "hgrn_out_norm", "attn_q_norm", "attn_k_norm", "ffn2_norm")
WEIGHTS = ("ffn1_norm", "ffn1_w_in", "ffn1_w_out", "mix_norm", "w_in", "hgrn_lb_logits", "hgrn_out_norm", "attn_q_norm",
           "attn_k_norm", "w_branch_a", "w_branch_b", "w_out", "ffn2_norm", "ffn2_w_in", "ffn2_w_out")
SMALL_ROWS = 8


def _layer_weights(full, small, l):
    w = {}
    for n in BIG:
        a = full[n]
        w[n] = a.reshape(1, a.shape[0] * a.shape[1], a.shape[2]) if n in ROW_SHARDED else a
    for n in ("ffn1_norm", "mix_norm", "hgrn_out_norm", "ffn2_norm"):
        w[n] = small[n][l].reshape(1, D_MODEL)
    for n in ("attn_q_norm", "attn_k_norm"):
        w[n] = small[n][l]
    return w


def _local_step(x, target, fulls, small):
    t = x.shape[0]
    cos, sin = _rope_tables(t)
    lbs = _lower_bounds(small["hgrn_lb_logits"])
    saved = []
    for l in range(2):
        w = _layer_weights(fulls[l], small, l)
        lb = lbs[l].reshape(1, D_MODEL)
        x, s1 = _ffn_fwd(x, w["ffn1_norm"], w["ffn1_w_in"], w["ffn1_w_out"], f"l{l}_ffn1")
        x, s2 = _mix_fwd(x, w, lb, cos, sin, f"l{l}_mix")
        x, s3 = _ffn_fwd(x, w["ffn2_norm"], w["ffn2_w_in"], w["ffn2_w_out"], f"l{l}_ffn2")
        saved.append((w, lb, s1, s2, s3))
    dx, sq = _loss_fwd_bwd(x, target, name="loss")
    big_grads, small_rows = [None, None], [None, None]
    for l in (1, 0):
        w, lb, s1, s2, s3 = saved[l]
        dx, dg2, g_in2, g_out2 = _ffn_bwd(dx, s3, w["ffn2_norm"], w["ffn2_w_in"], w["ffn2_w_out"], f"l{l}_ffn2")
        dx, big, sm = _mix_bwd(dx, s2, w, lb, cos, sin, f"l{l}_mix", lb_live=l > 0)
        dx, dg1, g_in1, g_out1 = _ffn_bwd(dx, s1, w["ffn1_norm"], w["ffn1_w_in"], w["ffn1_w_out"], f"l{l}_ffn1")
        big.update(ffn1_w_in=g_in1, ffn1_w_out=g_out1, ffn2_w_in=g_in2, ffn2_w_out=g_out2)
        big_grads[l] = big
        pad = lambda a: jnp.pad(a[:ATT_GROUPS].reshape(1, ATT_GROUPS * HEAD), ((0, 0), (0, D_MODEL - ATT_GROUPS * HEAD)))
        small_rows[l] = jnp.concatenate(
            [dg1, sm["mix_norm"], sm["lb"], sm["hgrn_out_norm"], pad(sm["attn_q_norm"]), pad(sm["attn_k_norm"]), dg2,
             jnp.zeros((SMALL_ROWS - 7, D_MODEL), F32)], axis=0)
    return jnp.sum(sq), dx, big_grads, jnp.concatenate(small_rows, axis=0)


ANY = pl.BlockSpec(memory_space=pl.ANY)


def _coords():
    return lax.axis_index("x"), lax.axis_index("y"), lax.axis_index("c")


def _other_chips(x, y):
    return [(1 - x, y), (x, 1 - y), (1 - x, 1 - y)]


def _half_rows(rows, which):
    return pl.ds(which * (rows // 2), rows // 2)


def _gather_layer(shards, name):
    n = len(shards)

    def body(*refs):
        w, full = refs[:n], refs[n:2 * n]
        send, recv, fsend, frecv = refs[2 * n:]
        x, y, c = _coords()
        slot = 2 * x + y
        chips = _other_chips(x, y)

        def copy(i, j, blk, src, sems, to):
            return pltpu.make_async_remote_copy(src_ref=src, dst_ref=blk, send_sem=sems[0].at[i * 3 + j],
                                                recv_sem=sems[1].at[i * 3 + j], device_id=to, device_id_type=MESH)

        def block(i, chip_slot, core):
            return full[i].at[chip_slot, _half_rows(shards[i].shape[0], core)]

        first = []
        for i in range(n):
            for j, chip in enumerate(chips):
                first.append(copy(i, j, block(i, slot, c), w[i].at[_half_rows(shards[i].shape[0], c)], (send, recv), (*chip, c)))
        for cp in first:
            cp.start()
        passed = []
        for i in range(n):
            for j, chip in enumerate(chips):
                blk = block(i, 2 * chip[0] + chip[1], c)
                copy(i, j, blk, blk, (send, recv), (*chip, c)).wait_recv()
                cp = copy(i, j, blk, blk, (fsend, frecv), (x, y, 1 - c))
                cp.start()
                passed.append(cp)
        for i in range(n):
            for j, chip in enumerate(chips):
                blk = block(i, 2 * chip[0] + chip[1], 1 - c)
                copy(i, j, blk, blk, (fsend, frecv), (x, y, 1 - c)).wait_recv()
        for cp in first + passed:
            cp.wait_send()

    out_shape = [jax.ShapeDtypeStruct((N_CHIPS,) + s.shape, s.dtype) for s in shards]
    return pl.pallas_call(body, in_specs=[ANY] * n, out_specs=[ANY] * n, out_shape=out_shape,
                          scratch_shapes=[pltpu.SemaphoreType.DMA((3 * n,))] * 4, name=name)(*shards)


N_RECV = 7


def _scatter_layer(parts, name):
    n = len(parts)

    def body(*refs):
        p, out = refs[:n], refs[n:2 * n]
        send, recv = refs[2 * n:]
        x, y, c = _coords()
        slot = 2 * x + y
        chips = _other_chips(x, y)
        sends = []
        for i in range(n):
            rows = parts[i].shape[1]
            for j, chip in enumerate(chips):
                for core in (0, 1):
                    sends.append(pltpu.make_async_remote_copy(
                        src_ref=p[i].at[2 * chip[0] + chip[1], _half_rows(rows, core)], dst_ref=out[i].at[2 * j + c],
                        send_sem=send.at[i * N_RECV + 2 * j + core], recv_sem=recv.at[i * N_RECV + 2 * j + c],
                        device_id=(*chip, core), device_id_type=MESH))
            sends.append(pltpu.make_async_remote_copy(
                src_ref=p[i].at[slot, _half_rows(rows, 1 - c)], dst_ref=out[i].at[6], send_sem=send.at[i * N_RECV + 6],
                recv_sem=recv.at[i * N_RECV + 6], device_id=(x, y, 1 - c), device_id_type=MESH))
        for cp in sends:
            cp.start()
        for i in range(n):
            for k in range(N_RECV):
                pltpu.make_async_remote_copy(src_ref=out[i].at[k], dst_ref=out[i].at[k], send_sem=send.at[0],
                                             recv_sem=recv.at[i * N_RECV + k], device_id=(x, y, c), device_id_type=MESH).wait_recv()
        for cp in sends:
            cp.wait_send()

    out_shape = [jax.ShapeDtypeStruct((N_RECV, a.shape[1] // 2, a.shape[2]), a.dtype) for a in parts]
    return pl.pallas_call(body, in_specs=[ANY] * n, out_specs=[ANY] * n, out_shape=out_shape,
                          scratch_shapes=[pltpu.SemaphoreType.DMA((N_RECV * n,))] * 2, name=name)(*parts)


def _sum_partials(own, parts, name):
    r, wd = own.shape
    tm = next(t for t in (256, 128, 64, 32, 16) if r % t == 0)

    def body(own_ref, p_ref, o_ref):
        acc = own_ref[...].astype(F32)
        for k in range(N_RECV):
            acc = acc + p_ref[k].astype(F32)
        o_ref[...] = acc

    return pl.pallas_call(
        body, grid=(r // tm,),
        in_specs=[pl.BlockSpec((tm, wd), lambda i: (i, 0)), pl.BlockSpec((N_RECV, tm, wd), lambda i: (0, i, 0))],
        out_specs=pl.BlockSpec((tm, wd), lambda i: (i, 0)), out_shape=jax.ShapeDtypeStruct((r, wd), F32),
        name=name, compiler_params=_params(("parallel",)))(own, parts)


def _exchange_halves(reduced, name):
    n = len(reduced)

    def body(*refs):
        r, out = refs[:n], refs[n:2 * n]
        send, recv = refs[2 * n:]
        x, y, c = _coords()
        sib = [pltpu.make_async_remote_copy(src_ref=r[i], dst_ref=out[i], send_sem=send.at[i], recv_sem=recv.at[i],
                                            device_id=(x, y, 1 - c), device_id_type=MESH) for i in range(n)]
        for cp in sib:
            cp.start()
        for cp in sib:
            cp.wait_recv()
        for cp in sib:
            cp.wait_send()

    out_shape = [jax.ShapeDtypeStruct(a.shape, a.dtype) for a in reduced]
    return pl.pallas_call(body, in_specs=[ANY] * n, out_specs=[ANY] * n, out_shape=out_shape,
                          scratch_shapes=[pltpu.SemaphoreType.DMA((n,))] * 2, name=name)(*reduced)


def _reduce_layer(parts, tag):
    x, y, c = _coords()
    slot = 2 * x + y
    recv = _scatter_layer(parts, name=tag + "_scatter")
    halves = []
    for i, (p, r) in enumerate(zip(parts, recv)):
        half = p.shape[1] // 2
        own = lax.dynamic_slice(p, (slot, c * half, 0), (1, half, p.shape[2]))[0]
        halves.append(_sum_partials(own, r, name=f"{tag}_sum{i}"))
    theirs = _exchange_halves(halves, name=tag + "_exchange")
    return [jnp.where(c == 0, jnp.concatenate([h, t], axis=0), jnp.concatenate([t, h], axis=0)) for h, t in zip(halves, theirs)]


def _all_reduce_small(rows):
    r = rows.shape[0]

    def body(x_ref, o_ref, buf, send, recv):
        x, y, c = _coords()
        me = 4 * x + 2 * y + c
        buf[me] = x_ref[...]
        copies = []
        for k in range(1, 8):
            peer = (x ^ (k >> 2), y ^ ((k >> 1) & 1), c ^ (k & 1))
            cp = pltpu.make_async_remote_copy(src_ref=x_ref, dst_ref=buf.at[me], send_sem=send.at[k - 1], recv_sem=recv.at[me],
                                              device_id=peer, device_id_type=MESH)
            cp.start()
            copies.append(cp)
        for k in range(1, 8):
            src = 4 * (x ^ (k >> 2)) + 2 * (y ^ ((k >> 1) & 1)) + (c ^ (k & 1))
            pltpu.make_async_remote_copy(src_ref=x_ref, dst_ref=buf.at[src], send_sem=send.at[0], recv_sem=recv.at[src],
                                         device_id=(x, y, c), device_id_type=MESH).wait_recv()
        for cp in copies:
            cp.wait_send()
        acc = buf[0]
        for k in range(1, 8):
            acc = acc + buf[k]
        o_ref[...] = acc

    vm = pl.BlockSpec(memory_space=pltpu.VMEM)
    return pl.pallas_call(
        body, in_specs=[vm], out_specs=vm, out_shape=jax.ShapeDtypeStruct(rows.shape, F32),
        scratch_shapes=[pltpu.VMEM((8, r, D_MODEL), F32), pltpu.SemaphoreType.DMA((7,)), pltpu.SemaphoreType.DMA((8,))],
        name="all_reduce_small")(rows)


def _adamw_math(w, g, m, v):
    m = ADAM_B1 * m + (1.0 - ADAM_B1) * g
    v = ADAM_B2 * v + (1.0 - ADAM_B2) * (g * g)
    m_hat = m / (1.0 - ADAM_B1 ** ADAM_STEP)
    v_hat = v / (1.0 - ADAM_B2 ** ADAM_STEP)
    return -ADAM_LR * (m_hat / (jnp.sqrt(v_hat) + ADAM_EPS) + ADAM_WD * w), m, v


def _adamw(w, g, m, v, name):
    shape = w.shape
    cols = shape[-1]
    flat = lambda a: a.reshape(-1, cols)
    rows = flat(w).shape[0]
    tm = 128 if rows % 128 == 0 else rows
    ins = [('t', flat(a), cols, 0) for a in (w, g, m, v)]
    res = _ew(_adamw_math, ins, [('t', cols, F32)] * 3, rows=rows, tm=tm, name=name)
    return [a.reshape(shape) for a in res]


def _small_update(sums, logits, w, m, v):
    def body(s_ref, lg_ref, w_ref, m_ref, v_ref, g_ref, d_ref, nm_ref, nv_ref):
        s = s_ref[...]
        l0, l1 = lg_ref[0:1, :], lg_ref[1:2, :]
        mx = jnp.maximum(l0, l1)
        e0, e1 = jnp.exp(l0 - mx), jnp.exp(l1 - mx)
        sm0, sm1 = e0 / (e0 + e1), e1 / (e0 + e1)
        dl1 = s_ref[SMALL_ROWS + 2:SMALL_ROWS + 3, :] * sm0 * sm1
        row = lax.broadcasted_iota(jnp.int32, s.shape, 0)
        g = jnp.where(row == 2, -dl1, jnp.where(row == SMALL_ROWS + 2, dl1, s))
        d, nm, nv = _adamw_math(w_ref[...], g, m_ref[...], v_ref[...])
        g_ref[...] = g
        d_ref[...] = d
        nm_ref[...] = nm
        nv_ref[...] = nv

    vm = pl.BlockSpec(memory_space=pltpu.VMEM)
    return pl.pallas_call(body, in_specs=[vm] * 5, out_specs=[vm] * 4,
                          out_shape=[jax.ShapeDtypeStruct(sums.shape, F32)] * 4, name="small_update")(sums, logits, w, m, v)


def _pack_small(vals):
    rows = []
    for l in range(2):
        for n in ("ffn1_norm", "mix_norm", "hgrn_lb_logits", "hgrn_out_norm", "attn_q_norm", "attn_k_norm", "ffn2_norm"):
            a = vals[n][l].reshape(1, -1)
            rows.append(jnp.pad(a, ((0, 0), (0, D_MODEL - a.shape[1]))))
        rows.append(jnp.zeros((SMALL_ROWS - 7, D_MODEL), F32))
    return jnp.concatenate(rows, axis=0)


def _unpack_small(packed):
    out = {}
    for k, n in enumerate(("ffn1_norm", "mix_norm", "hgrn_lb_logits", "hgrn_out_norm", "attn_q_norm", "attn_k_norm", "ffn2_norm")):
        a = jnp.stack([packed[k], packed[SMALL_ROWS + k]], axis=0)
        out[n] = a[:, :ATT_GROUPS * HEAD].reshape(2, ATT_GROUPS, HEAD) if n.startswith("attn") else a
    return out


def kernel(x, ffn1_norm, ffn1_w_in, ffn1_w_out, mix_norm, w_in, hgrn_lb_logits, hgrn_out_norm, attn_q_norm, attn_k_norm, w_branch_a, w_branch_b, w_out, ffn2_norm, ffn2_w_in, ffn2_w_out, loss_target, m_ffn1_norm, m_ffn1_w_in, m_ffn1_w_out, m_mix_norm, m_w_in, m_hgrn_lb_logits, m_hgrn_out_norm, m_attn_q_norm, m_attn_k_norm, m_w_branch_a, m_w_branch_b, m_w_out, m_ffn2_norm, m_ffn2_w_in, m_ffn2_w_out, v_ffn1_norm, v_ffn1_w_in, v_ffn1_w_out, v_mix_norm, v_w_in, v_hgrn_lb_logits, v_hgrn_out_norm, v_attn_q_norm, v_attn_k_norm, v_w_branch_a, v_w_branch_b, v_w_out, v_ffn2_norm, v_ffn2_w_in, v_ffn2_w_out):
    a = locals()
    w = {n: a[n] for n in WEIGHTS}
    m = {n: a["m_" + n] for n in WEIGHTS}
    v = {n: a["v_" + n] for n in WEIGHTS}

    slot = 2 * lax.axis_index("x") + lax.axis_index("y")
    shards = {n: w[n].astype(BF16) for n in BIG}
    fulls = []
    for l in range(2):
        got = _gather_layer([shards[n][l] for n in BIG], name=f"gather_l{l}")
        fulls.append({n: lax.dynamic_update_slice(g, shards[n][l][None], (slot, 0, 0)) for n, g in zip(BIG, got)})
    small = {n: w[n] for n in SMALL}
    sq, grad_x, big_grads, small_rows = _local_step(x[0], loss_target[0], fulls, small)
    loss = lax.psum(sq, ("x", "y", "c")) * (0.5 / D_MODEL)

    reduced = [_reduce_layer([big_grads[l][n] for n in BIG], f"reduce_l{l}") for l in range(2)]
    grads = {n: jnp.stack([reduced[0][i], reduced[1][i]], axis=0) for i, n in enumerate(BIG)}

    sums = _all_reduce_small(small_rows)
    g_s, d_s, m_s, v_s = _small_update(sums, w["hgrn_lb_logits"], _pack_small(small), _pack_small({n: m[n] for n in SMALL}),
                                       _pack_small({n: v[n] for n in SMALL}))
    grads.update(_unpack_small(g_s))
    delta, new_m, new_v = _unpack_small(d_s), _unpack_small(m_s), _unpack_small(v_s)
    for n in BIG:
        delta[n], new_m[n], new_v[n] = _adamw(w[n], grads[n], m[n], v[n], name="adamw_" + n)

    return (loss, grad_x[None], *[grads[n] for n in WEIGHTS], *[delta[n] for n in WEIGHTS],
            *[new_m[n] for n in WEIGHTS], *[new_v[n] for n in WEIGHTS])
```

```python
import functools

import jax
import jax.numpy as jnp
from jax import lax
from jax.experimental import pallas as pl
from jax.experimental.pallas import tpu as pltpu

F32 = jnp.float32
BF16 = jnp.bfloat16
MESH = pl.DeviceIdType.MESH

D_MODEL = 1024
D_FF = 2816
N_CHIPS = 4
HEAD = 128
HG_HEADS = 8
HG_CHUNK = 64
ATT_GROUPS = 3
ATT_HEADS = 4
ATT_GW = ATT_HEADS * HEAD
DILATIONS = (1, 4, 16)
ATT_BLK = 128
ATT_STEP_BLOCKS = 4
P_IN = 10752
CB_AQ, CB_AK, CB_AV, CB_GA, CB_GB = 8, 11, 14, 17, 19
EPS = 1e-6
ROPE_THETA = 10000.0
ADAM_LR, ADAM_B1, ADAM_B2, ADAM_EPS, ADAM_WD, ADAM_STEP = 0.001, 0.9, 0.999, 1e-08, 0.01, 10
VMEM_LIMIT_V7X = 56 * 1024 * 1024
NEG = -1e30


def _params(sem):
    return pltpu.CompilerParams(dimension_semantics=sem, vmem_limit_bytes=VMEM_LIMIT_V7X)


def _sig(x):
    return 1.0 / (1.0 + jnp.exp(-x))


def _dot(a, b):
    return jnp.dot(a, b, preferred_element_type=F32)


def _dot_nt(a, b):
    return lax.dot_general(a, b, (((1,), (1,)), ((), ())), preferred_element_type=F32)


def _dot_tn(a, b):
    return lax.dot_general(a, b, (((0,), (0,)), ((), ())), preferred_element_type=F32)


def _bf(x):
    return x.astype(BF16)


ANY = pl.BlockSpec(memory_space=pl.ANY)


class _Rider:
    def __init__(self, args, out_shape, sems, begin, end):
        self.args, self.out_shape, self.sems, self.begin, self.end = list(args), list(out_shape), list(sems), begin, end
        self.result = None


def _pcall(body, *, grid, in_specs, out_specs, out_shape, name, sem, args, scratch_shapes=(), rider=None):
    multi = isinstance(out_shape, (list, tuple))
    o_specs = list(out_specs) if multi else [out_specs]
    o_shape = list(out_shape) if multi else [out_shape]
    if rider is None:
        res = pl.pallas_call(body, grid=grid, in_specs=list(in_specs), out_specs=o_specs, out_shape=o_shape,
                             scratch_shapes=list(scratch_shapes), name=name, compiler_params=_params(sem))(*args)
        return list(res) if multi else res[0]
    counts = [len(in_specs), len(rider.args), len(o_specs), len(rider.out_shape), len(scratch_shapes)]

    def wrapped(*refs):
        groups, at = [], 0
        for c in counts:
            groups.append(refs[at:at + c])
            at += c
        h_in, r_in, h_out, r_out, h_scratch = groups
        r_sems = refs[at:]
        if grid:
            ids = [pl.program_id(a) for a in range(len(grid))]
            first = functools.reduce(jnp.logical_and, [i == 0 for i in ids])
            last = functools.reduce(jnp.logical_and, [i == g - 1 for i, g in zip(ids, grid)])
            pl.when(first)(lambda: rider.begin(r_in, r_out, r_sems))
            body(*h_in, *h_out, *h_scratch)
            pl.when(last)(lambda: rider.end(r_in, r_out, r_sems))
        else:
            rider.begin(r_in, r_out, r_sems)
            body(*h_in, *h_out, *h_scratch)
            rider.end(r_in, r_out, r_sems)

    res = pl.pallas_call(
        wrapped, grid=grid, in_specs=list(in_specs) + [ANY] * counts[1], out_specs=o_specs + [ANY] * counts[3],
        out_shape=o_shape + rider.out_shape, scratch_shapes=list(scratch_shapes) + rider.sems, name=name,
        compiler_params=_params(("arbitrary",) * len(grid)))(*args, *rider.args)
    rider.result = list(res[counts[2]:])
    return list(res[:counts[2]]) if multi else res[0]


def _mm_nn(a, b3, *, name, tm, tn, out_dtype, res=None, alpha=1.0, rider=None):
    m, k = a.shape
    nb, _, nw = b3.shape
    per = nw // tn
    assert nw % tn == 0 and m % tm == 0
    has_res = res is not None

    def body(*refs):
        if has_res:
            a_ref, b_ref, r_ref, o_ref = refs
        else:
            a_ref, b_ref, o_ref = refs
        acc = _dot(_bf(a_ref[...]), b_ref[...])
        if alpha != 1.0:
            acc = alpha * acc
        if has_res:
            acc = r_ref[...] + acc
        o_ref[...] = acc.astype(o_ref.dtype)

    in_specs = [pl.BlockSpec((tm, k), lambda i, j: (i, 0)),
                pl.BlockSpec((None, k, tn), lambda i, j: (j // per, 0, j % per))]
    args = [a, b3]
    if has_res:
        in_specs.append(pl.BlockSpec((tm, tn), lambda i, j: (i, j)))
        args.append(res)
    return _pcall(body, grid=(m // tm, nb * per), in_specs=in_specs, out_specs=pl.BlockSpec((tm, tn), lambda i, j: (i, j)),
                  out_shape=jax.ShapeDtypeStruct((m, nb * nw), out_dtype), name=name, sem=("parallel", "arbitrary"),
                  args=args, rider=rider)


def _mm_nt(d, b3, *, name, tm, tp, tn, out_dtype, alpha=1.0, rider=None):
    m, n = d.shape
    nb, p, nw = b3.shape
    per = nw // tn
    nk = n // tn
    assert nb * nw == n and nw % tn == 0 and p % tp == 0 and m % tm == 0

    def body(d_ref, b_ref, o_ref, acc_ref):
        kk = pl.program_id(2)

        @pl.when(kk == 0)
        def _():
            acc_ref[...] = jnp.zeros_like(acc_ref)

        acc_ref[...] += _dot_nt(_bf(d_ref[...]), b_ref[...])

        @pl.when(kk == nk - 1)
        def _():
            o_ref[...] = (alpha * acc_ref[...]).astype(o_ref.dtype)

    return _pcall(
        body, grid=(m // tm, p // tp, nk),
        in_specs=[pl.BlockSpec((tm, tn), lambda i, j, kk: (i, kk)),
                  pl.BlockSpec((None, tp, tn), lambda i, j, kk: (kk // per, j, kk % per))],
        out_specs=pl.BlockSpec((tm, tp), lambda i, j, kk: (i, j)),
        out_shape=jax.ShapeDtypeStruct((m, p), out_dtype),
        scratch_shapes=[pltpu.VMEM((tm, tp), F32)],
        name=name, sem=("parallel", "parallel", "arbitrary"), args=(d, b3), rider=rider)


def _mm_tn(a, d, *, nb, name, tm, tk, tn, alpha=1.0, rider=None):
    m, k = a.shape
    _, n = d.shape
    nw = n // nb
    per = nw // tn
    nm = m // tm
    assert nw % tn == 0 and k % tk == 0 and m % tm == 0

    def body(a_ref, d_ref, o_ref, acc_ref):
        mm = pl.program_id(2)

        @pl.when(mm == 0)
        def _():
            acc_ref[...] = jnp.zeros_like(acc_ref)

        acc_ref[...] += _dot_tn(_bf(a_ref[...]), _bf(d_ref[...]))

        @pl.when(mm == nm - 1)
        def _():
            o_ref[...] = (alpha * acc_ref[...]).astype(o_ref.dtype)

    return _pcall(
        body, grid=(k // tk, nb * per, nm),
        in_specs=[pl.BlockSpec((tm, tk), lambda i, j, mm: (mm, i)),
                  pl.BlockSpec((tm, tn), lambda i, j, mm: (mm, j))],
        out_specs=pl.BlockSpec((None, tk, tn), lambda i, j, mm: (j // per, i, j % per)),
        out_shape=jax.ShapeDtypeStruct((nb, k, nw), BF16),
        scratch_shapes=[pltpu.VMEM((tk, tn), F32)],
        name=name, sem=("parallel", "parallel", "arbitrary"), args=(a, d), rider=rider)


def _ew(fn, ins, outs, *, rows, tm, name):
    in_specs, args = [], []
    for s in ins:
        if s[0] == 't':
            _, arr, w, cb = s
            in_specs.append(pl.BlockSpec((tm, w), lambda i, cb=cb: (i, cb)))
        else:
            arr = s[1]
            in_specs.append(pl.BlockSpec(arr.shape, lambda i, nd=arr.ndim: (0,) * nd))
        args.append(arr)
    out_specs, out_shape = [], []
    for s in outs:
        if s[0] == 't':
            _, w, dt = s
            out_specs.append(pl.BlockSpec((tm, w), lambda i: (i, 0)))
            out_shape.append(jax.ShapeDtypeStruct((rows, w), dt))
        else:
            out_specs.append(pl.BlockSpec(s[1], lambda i: (0, 0)))
            out_shape.append(jax.ShapeDtypeStruct(s[1], F32))
    n_in = len(ins)

    def body(*refs):
        res = fn(*[r[...] for r in refs[:n_in]])
        if not isinstance(res, (tuple, list)):
            res = (res,)
        for r, s, v in zip(refs[n_in:], outs, res):
            if s[0] == 't':
                r[...] = v.astype(r.dtype)
            else:
                @pl.when(pl.program_id(0) == 0)
                def _(r=r):
                    r[...] = jnp.zeros_like(r)

                r[...] += v

    res = pl.pallas_call(
        body, grid=(rows // tm,), in_specs=in_specs, out_specs=out_specs, out_shape=out_shape,
        name=name, compiler_params=_params(("arbitrary",)))(*args)
    return res


def _heads(x):
    return [x[:, h * HEAD:(h + 1) * HEAD] for h in range(x.shape[1] // HEAD)]


def _cat(xs):
    return jnp.concatenate(xs, axis=1)


def _head_mean(x):
    return _cat([jnp.broadcast_to(jnp.mean(h, axis=1, keepdims=True), h.shape) for h in _heads(x)])


def _rms_rows(x):
    return lax.rsqrt(jnp.mean(x * x, axis=1, keepdims=True) + EPS)


def _norm_fwd(x, g, name):
    return _ew(lambda xv, gv: xv * _rms_rows(xv) * gv,
               [('t', x, D_MODEL, 0), ('f', g)], [('t', D_MODEL, BF16)], rows=x.shape[0], tm=512, name=name)[0]


def _norm_bwd(dh, x, g, dx, name):
    def fn(dhv, xv, gv, dxv):
        r = _rms_rows(xv)
        xh = xv * r
        dxh = dhv * gv
        out = dxv + r * (dxh - xh * jnp.mean(dxh * xh, axis=1, keepdims=True))
        return out, jnp.sum(dhv * xh, axis=0, keepdims=True)

    return _ew(fn, [('t', dh, D_MODEL, 0), ('t', x, D_MODEL, 0), ('f', g), ('t', dx, D_MODEL, 0)],
               [('t', D_MODEL, F32), ('acc', (1, D_MODEL))], rows=x.shape[0], tm=512, name=name)


def _loss_fwd_bwd(y, target, name):
    def fn(yv, tv):
        e = yv - tv
        return e * (1.0 / D_MODEL), jnp.sum(e * e, axis=0, keepdims=True)

    return _ew(fn, [('t', y, D_MODEL, 0), ('t', target, D_MODEL, 0)], [('t', D_MODEL, F32), ('acc', (1, D_MODEL))],
               rows=y.shape[0], tm=512, name=name)


def _gate_fwd(proj, ya, yb, name):
    def fn(ga0, ga1, gb0, gb1, yav, ybv):
        return _sig(_cat([ga0, ga1])) * yav + _sig(_cat([gb0, gb1])) * ybv

    ins = [('t', proj, 512, CB_GA), ('t', proj, 512, CB_GA + 1), ('t', proj, 512, CB_GB), ('t', proj, 512, CB_GB + 1),
           ('t', ya, D_MODEL, 0), ('t', yb, D_MODEL, 0)]
    return _ew(fn, ins, [('t', D_MODEL, BF16)], rows=ya.shape[0], tm=512, name=name)[0]


def _gate_bwd(dm, proj, ya, yb, name):
    def fn(dmv, ga0, ga1, gb0, gb1, yav, ybv):
        sa = _sig(_cat([ga0, ga1]))
        sb = _sig(_cat([gb0, gb1]))
        return dmv * sa, dmv * sb, _cat([dmv * yav * sa * (1.0 - sa), dmv * ybv * sb * (1.0 - sb)])

    ins = [('t', dm, D_MODEL, 0),
           ('t', proj, 512, CB_GA), ('t', proj, 512, CB_GA + 1), ('t', proj, 512, CB_GB), ('t', proj, 512, CB_GB + 1),
           ('t', ya, D_MODEL, 0), ('t', yb, D_MODEL, 0)]
    return _ew(fn, ins, [('t', D_MODEL, BF16), ('t', D_MODEL, BF16), ('t', 2 * D_MODEL, BF16)],
               rows=ya.shape[0], tm=512, name=name)


def _rot(x):
    sgn = jnp.where(lax.broadcasted_iota(jnp.int32, x.shape, 1) < HEAD // 2, -1.0, 1.0)
    return pltpu.roll(x, HEAD // 2, 1) * sgn


def _gain_rows(qn, kn):
    return [a[g:g + 1] for a in (qn, kn) for g in range(ATT_GROUPS)]


def _qk_fwd(proj, cos, sin, qn, kn, name):
    def fn(*v):
        xs, cosv, sinv, gains, vs = v[:6], v[6], v[7], v[8:14], v[14:17]
        outs = []
        for j, x in enumerate(xs):
            gain = gains[j]
            ys = []
            for xh in _heads(x):
                xn = xh * _rms_rows(xh) * gain
                ys.append(xn * cosv + _rot(xn) * sinv)
            outs.append(_cat(ys))
        return outs + list(vs)

    ins = ([('t', proj, 512, CB_AQ + j) for j in range(6)] + [('t', cos, HEAD, 0), ('t', sin, HEAD, 0)]
           + [('f', a) for a in _gain_rows(qn, kn)] + [('t', proj, 512, CB_AV + g) for g in range(ATT_GROUPS)])
    return _ew(fn, ins, [('t', ATT_GW, BF16)] * 9, rows=proj.shape[0], tm=512, name=name)


def _qk_bwd(dqk, proj, cos, sin, qn, kn, name):
    def fn(*v):
        ds, xs, cosv, sinv, gains = v[:6], v[6:12], v[12], v[13], v[14:20]
        rows8 = lax.broadcasted_iota(jnp.int32, (8, HEAD), 0)
        outs, dgs = [], [jnp.zeros((8, HEAD), F32)] * 2
        for j in range(6):
            gain = gains[j]
            dx, dg = [], jnp.zeros((1, HEAD), F32)
            for dyh, xh in zip(_heads(ds[j]), _heads(xs[j])):
                r = _rms_rows(xh)
                xhat = xh * r
                dxn = dyh * cosv - _rot(dyh * sinv)
                dg = dg + jnp.sum(dxn * xhat, axis=0, keepdims=True)
                dxh = dxn * gain
                dx.append(r * (dxh - xhat * jnp.mean(dxh * xhat, axis=1, keepdims=True)))
            outs.append(_cat(dx))
            dgs[j // 3] = dgs[j // 3] + jnp.where(rows8 == j % 3, dg, 0.0)
        return _cat(outs), dgs[0], dgs[1]

    ins = ([('t', a, ATT_GW, 0) for a in dqk] + [('t', proj, 512, CB_AQ + j) for j in range(6)]
           + [('t', cos, HEAD, 0), ('t', sin, HEAD, 0)] + [('f', a) for a in _gain_rows(qn, kn)])
    return _ew(fn, ins, [('t', 6 * ATT_GW, BF16), ('acc', (8, HEAD)), ('acc', (8, HEAD))],
               rows=proj.shape[0], tm=256, name=name)


def _merge_fwd(outs, lses, name):
    def fn(o0, o1, o2, l0, l1, l2):
        m = jnp.maximum(jnp.maximum(l0, l1), l2)
        e0, e1, e2 = jnp.exp(l0 - m), jnp.exp(l1 - m), jnp.exp(l2 - m)
        return (e0 * o0 + e1 * o1 + e2 * o2) / (e0 + e1 + e2)

    ins = [('t', a, ATT_GW, 0) for a in list(outs) + list(lses)]
    return _ew(fn, ins, [('t', ATT_GW, BF16)], rows=outs[0].shape[0], tm=512, name=name)[0]


def _merge_bwd(dob, outs, lses, name):
    def fn(dov, o0, o1, o2, l0, l1, l2):
        m = jnp.maximum(jnp.maximum(l0, l1), l2)
        e0, e1, e2 = jnp.exp(l0 - m), jnp.exp(l1 - m), jnp.exp(l2 - m)
        inv = 1.0 / (e0 + e1 + e2)
        a0, a1, a2 = e0 * inv, e1 * inv, e2 * inv
        ob = a0 * o0 + a1 * o1 + a2 * o2
        s = _head_mean(dov * ob) * float(HEAD)
        return a0 * dov, a1 * dov, a2 * dov, a0 * s, a1 * s, a2 * s

    ins = [('t', dob, ATT_GW, 0)] + [('t', a, ATT_GW, 0) for a in list(outs) + list(lses)]
    return _ew(fn, ins, [('t', ATT_GW, BF16)] * 3 + [('t', ATT_GW, F32)] * 3, rows=dob.shape[0], tm=512, name=name)


def _assemble_dproj(dh4, dqk, dvs, dgab, name):
    fn = lambda *v: _cat(list(v))
    ins = [('t', dh4, 4 * D_MODEL, 0), ('t', dqk, 6 * ATT_GW, 0)] + [('t', a, ATT_GW, 0) for a in dvs] + [('t', dgab, 2 * D_MODEL, 0)]
    return _ew(fn, ins, [('t', P_IN, BF16)], rows=dh4.shape[0], tm=256, name=name)[0]


HG_ROWS = 256


def _hg_gates(hq, hf, hi, lbv):
    sig = _sig(hf)
    f = lbv + (1.0 - lbv) * sig
    return hq * _sig(hq), 1.0 - f, hi, jnp.log(f), sig, f


def _split3(x):
    hi = _bf(x)
    r1 = x - hi.astype(F32)
    mid = _bf(r1)
    return hi, mid, _bf(r1 - mid.astype(F32))


def _tri_dot(tri, x):
    hi, mid, lo = _split3(x)
    return _dot(tri, hi) + _dot(tri, mid) + _dot(tri, lo)


def _row(x, i):
    rows = lax.broadcasted_iota(jnp.int32, x.shape, 0)
    return jnp.sum(jnp.where(rows == i, x, 0.0), axis=0, keepdims=True)


def _hg_decay(logf, q, k):
    c = HG_CHUNK
    row = lax.broadcasted_iota(jnp.int32, (c, c), 0)
    col = lax.broadcasted_iota(jnp.int32, (c, c), 1)
    g = _tri_dot((row >= col).astype(BF16), logf)
    gm = _row(g, c // 2 - 1)
    gl = _row(g, c - 1)
    return g, gm, gl, q * jnp.exp(g), q * jnp.exp(g - gm), k * jnp.exp(gm - g), k * jnp.exp(gl - g)


def _hg_out_fwd(o, hg, gain):
    r = lax.rsqrt(_head_mean(o * o) + EPS)
    return o * r * gain * (hg * _sig(hg))


def _hgrn_fwd(proj, lb, gain, name, rider=None):
    t = proj.shape[0]
    nck = HG_ROWS // HG_CHUNK

    def body(hq_ref, hf_ref, hi_ref, hg_ref, lb_ref, gn_ref, o_ref, oa_ref, sall_ref, st_ref):
        @pl.when(pl.program_id(0) == 0)
        def _():
            st_ref[...] = jnp.zeros_like(st_ref)

        lbv = lb_ref[...]
        gnv = gn_ref[...]
        c = HG_CHUNK
        mask = lax.broadcasted_iota(jnp.int32, (c, c), 0) >= lax.broadcasted_iota(jnp.int32, (c, c), 1)

        def chunk(cc, carry):
            sl = pl.ds(pl.multiple_of(cc * c, c), c)
            q, k, v, logf, _, _ = _hg_gates(hq_ref[sl, :], hf_ref[sl, :], hi_ref[sl, :], lbv)
            _, _, gl, qg, qt, kt, kd = _hg_decay(logf, q, k)
            egl = jnp.exp(gl)
            os = []
            for h in range(HG_HEADS):
                hs = slice(h * HEAD, (h + 1) * HEAD)
                st = st_ref[h]
                sall_ref[cc, h] = st
                a = jnp.where(mask, _dot_nt(_bf(qt[:, hs]), _bf(kt[:, hs])), 0.0)
                os.append(_dot(_bf(a), _bf(v[:, hs])) + _dot_nt(_bf(qg[:, hs]), _bf(st)))
                st_ref[h] = egl[:, hs] * st + _dot_tn(_bf(v[:, hs]), _bf(kd[:, hs]))
            o = _cat(os)
            o_ref[sl, :] = o
            oa_ref[sl, :] = _hg_out_fwd(o, hg_ref[sl, :], gnv).astype(oa_ref.dtype)
            return carry

        lax.fori_loop(0, nck, chunk, 0)

    col = lambda j: pl.BlockSpec((HG_ROWS, D_MODEL), lambda i, j=j: (i, j))
    small = pl.BlockSpec((1, D_MODEL), lambda i: (0, 0))
    return _pcall(
        body, grid=(t // HG_ROWS,),
        in_specs=[col(0), col(1), col(2), col(3), small, small],
        out_specs=[col(0), col(0), pl.BlockSpec((nck, HG_HEADS, HEAD, HEAD), lambda i: (i, 0, 0, 0))],
        out_shape=[jax.ShapeDtypeStruct((t, D_MODEL), F32), jax.ShapeDtypeStruct((t, D_MODEL), BF16),
                   jax.ShapeDtypeStruct((t // HG_CHUNK, HG_HEADS, HEAD, HEAD), F32)],
        scratch_shapes=[pltpu.VMEM((HG_HEADS, HEAD, HEAD), F32)],
        name=name, sem=("arbitrary",), args=(proj, proj, proj, proj, lb, gain), rider=rider)


def _terms(x, precise):
    hi = _bf(x)
    return (hi, _bf(x - hi.astype(F32))) if precise else (hi,)


def _mm(dot, a, b):
    out = dot(a[0], b[0])
    if len(a) > 1:
        out = out + dot(a[1], b[0])
    if len(b) > 1:
        out = out + dot(a[0], b[1])
    return out


def _hgrn_bwd(doa, oscan, proj, sall, lb, gain, name, precise, rider=None):
    t = proj.shape[0]
    nck = HG_ROWS // HG_CHUNK
    nsteps = t // HG_ROWS
    terms = functools.partial(_terms, precise=precise)

    def body(doa_ref, os_ref, hq_ref, hf_ref, hi_ref, hg_ref, sall_ref, lb_ref, gn_ref,
             d4_ref, dgn_ref, dlb_ref, dst_ref):
        @pl.when(pl.program_id(0) == 0)
        def _():
            dst_ref[...] = jnp.zeros_like(dst_ref)
            dgn_ref[...] = jnp.zeros_like(dgn_ref)
            dlb_ref[...] = jnp.zeros_like(dlb_ref)

        lbv = lb_ref[...]
        gnv = gn_ref[...]
        c = HG_CHUNK
        row = lax.broadcasted_iota(jnp.int32, (c, c), 0)
        colm = lax.broadcasted_iota(jnp.int32, (c, c), 1)
        mask = row >= colm
        triu = (row <= colm).astype(BF16)
        last = lax.broadcasted_iota(jnp.int32, (c, HEAD), 0) == c - 1

        def chunk(ci, carry):
            cc = nck - 1 - ci
            sl = pl.ds(pl.multiple_of(cc * c, c), c)
            hq, hf, hg = hq_ref[sl, :], hf_ref[sl, :], hg_ref[sl, :]
            q, k, v, logf, sig, f = _hg_gates(hq, hf, hi_ref[sl, :], lbv)
            g, gm, gl, qg, qt, kt, kd = _hg_decay(logf, q, k)
            egl = jnp.exp(gl)
            o = os_ref[sl, :]
            dy = doa_ref[sl, :]
            r = lax.rsqrt(_head_mean(o * o) + EPS)
            oh = o * r
            sg = _sig(hg)
            silu_g = hg * sg
            dgn_ref[...] += jnp.sum(dy * oh * silu_g, axis=0, keepdims=True)
            dhg = dy * oh * gnv * (sg * (1.0 + hg * (1.0 - sg)))
            doh = dy * gnv * silu_g
            do = r * (doh - oh * _head_mean(doh * oh))
            dqs, dks, dvs, dgs = [], [], [], []
            for h in range(HG_HEADS):
                hs = slice(h * HEAD, (h + 1) * HEAD)
                st = sall_ref[cc, h]
                dst = dst_ref[h]
                qt_h, kt_h, qg_h, kd_h = qt[:, hs], kt[:, hs], qg[:, hs], kd[:, hs]
                do_p, v_p, qt_p, kt_p, qg_p = terms(do[:, hs]), terms(v[:, hs]), terms(qt_h), terms(kt_h), terms(qg_h)
                st_p, dst_p = terms(st), terms(dst)
                a = jnp.where(mask, _dot_nt(qt_p[0], kt_p[0]), 0.0)
                da = terms(jnp.where(mask, _mm(_dot_nt, do_p, v_p), 0.0))
                dqt = _mm(_dot, da, kt_p)
                dkt = _mm(_dot_tn, da, qt_p)
                dqg = _mm(_dot, do_p, st_p)
                dv = _dot_tn(_bf(a), do_p[0]) + _dot_nt(_bf(kd_h), dst_p[0])
                dkd = _mm(_dot, v_p, dst_p)
                dgl = egl[:, hs] * jnp.sum(st * dst, axis=0, keepdims=True) + jnp.sum(dkd * kd_h, axis=0, keepdims=True)
                dst_ref[h] = egl[:, hs] * dst + _mm(_dot_tn, do_p, qg_p)
                g_h = g[:, hs]
                gm_h = gm[:, hs]
                gl_h = gl[:, hs]
                dqs.append(dqt * jnp.exp(g_h - gm_h) + dqg * jnp.exp(g_h))
                dks.append(dkt * jnp.exp(gm_h - g_h) + dkd * jnp.exp(gl_h - g_h))
                dvs.append(dv)
                dgs.append(dqt * qt_h - dkt * kt_h + dqg * qg_h - dkd * kd_h + jnp.where(last, dgl, 0.0))
            dq, dk, dv, dg = _cat(dqs), _cat(dks), _cat(dvs), _cat(dgs)
            dlogf = _tri_dot(triu, dg)
            df = dlogf / f - dk
            dlb_ref[...] += jnp.sum(df * (1.0 - sig), axis=0, keepdims=True)
            dhf = df * (1.0 - lbv) * sig * (1.0 - sig)
            sq = _sig(hq)
            dhq = dq * (sq * (1.0 + hq * (1.0 - sq)))
            d4_ref[sl, :] = _cat([dhq, dhf, dv, dhg]).astype(d4_ref.dtype)
            return carry

        lax.fori_loop(0, nck, chunk, 0)

    rev = lambda j: pl.BlockSpec((HG_ROWS, D_MODEL), lambda i, j=j: (nsteps - 1 - i, j))
    small = pl.BlockSpec((1, D_MODEL), lambda i: (0, 0))
    return _pcall(
        body, grid=(nsteps,),
        in_specs=[rev(0), rev(0), rev(0), rev(1), rev(2), rev(3),
                  pl.BlockSpec((nck, HG_HEADS, HEAD, HEAD), lambda i: (nsteps - 1 - i, 0, 0, 0)), small, small],
        out_specs=[pl.BlockSpec((HG_ROWS, 4 * D_MODEL), lambda i: (nsteps - 1 - i, 0)), small, small],
        out_shape=[jax.ShapeDtypeStruct((t, 4 * D_MODEL), BF16), jax.ShapeDtypeStruct((1, D_MODEL), F32),
                   jax.ShapeDtypeStruct((1, D_MODEL), F32)],
        scratch_shapes=[pltpu.VMEM((HG_HEADS, HEAD, HEAD), F32)],
        name=name, sem=("arbitrary",), args=(doa, oscan, proj, proj, proj, proj, sall, lb, gain), rider=rider)


def _band_masks():
    qi = lax.broadcasted_iota(jnp.int32, (ATT_BLK, ATT_BLK), 0)
    ki = lax.broadcasted_iota(jnp.int32, (ATT_BLK, ATT_BLK), 1)
    return ki >= qi, ki <= qi


def _attn_cfg(t, g):
    d = DILATIONS[g]
    length = t // d
    nb = length // ATT_BLK
    return d, length, nb, min(ATT_STEP_BLOCKS, nb)


def _attn_fwd(qg, kg, vg, g, name):
    t = qg.shape[0]
    d, length, nb, rb = _attn_cfg(t, g)
    scale = HEAD ** -0.5

    def body(q_ref, k_ref, v_ref, kp_ref, vp_ref, o_ref, l_ref):
        n = pl.program_id(1)
        prev_m, own_m = _band_masks()
        first_m = jnp.logical_and(prev_m, n > 0)
        for h in range(ATT_HEADS):
            hs = slice(h * HEAD, (h + 1) * HEAD)
            for j in range(rb):
                rows = slice(j * ATT_BLK, (j + 1) * ATT_BLK)
                before = slice((j - 1) * ATT_BLK, j * ATT_BLK)
                q = q_ref[rows, hs]
                k0, v0, m0 = (kp_ref[:, hs], vp_ref[:, hs], first_m) if j == 0 else (k_ref[before, hs], v_ref[before, hs], prev_m)
                s0 = jnp.where(m0, _dot_nt(q, k0) * scale, NEG)
                s1 = jnp.where(own_m, _dot_nt(q, k_ref[rows, hs]) * scale, NEG)
                m = jnp.maximum(jnp.max(s0, axis=1, keepdims=True), jnp.max(s1, axis=1, keepdims=True))
                p0, p1 = jnp.exp(s0 - m), jnp.exp(s1 - m)
                l = jnp.sum(p0, axis=1, keepdims=True) + jnp.sum(p1, axis=1, keepdims=True)
                o = _dot(_bf(p0), v0) + _dot(_bf(p1), v_ref[rows, hs])
                o_ref[rows, hs] = o / l
                l_ref[rows, hs] = jnp.broadcast_to(m + jnp.log(l), (ATT_BLK, HEAD))

    own = pl.BlockSpec((rb * ATT_BLK, ATT_GW), lambda r, n: (n, r))
    prev = pl.BlockSpec((ATT_BLK, ATT_GW), lambda r, n: (jnp.maximum(n * rb - 1, 0), r))
    view = lambda a: a.reshape(length, d * ATT_GW)
    o, lse = pl.pallas_call(
        body, grid=(d, nb // rb), in_specs=[own, own, own, prev, prev], out_specs=[own, own],
        out_shape=[jax.ShapeDtypeStruct((length, d * ATT_GW), F32)] * 2,
        name=name, compiler_params=_params(("parallel", "arbitrary")))(view(qg), view(kg), view(vg), view(kg), view(vg))
    return o.reshape(t, ATT_GW), lse.reshape(t, ATT_GW)


def _attn_bwd(qg, kg, vg, dog, lse, delta, g, name):
    t = qg.shape[0]
    d, length, nb, rb = _attn_cfg(t, g)
    nsteps = nb // rb
    scale = HEAD ** -0.5

    def body(q_ref, k_ref, v_ref, do_ref, l_ref, dl_ref, kp_ref, vp_ref, qn_ref, don_ref, ln_ref, dln_ref,
             dq_ref, dk_ref, dv_ref):
        n = pl.program_id(1)
        prev_m, own_m = _band_masks()
        first_m = jnp.logical_and(prev_m, n > 0)
        next_m = jnp.logical_and(prev_m, n < nsteps - 1)
        for h in range(ATT_HEADS):
            hs = slice(h * HEAD, (h + 1) * HEAD)
            dk, dv = [None] * rb, [None] * rb
            for j in range(rb + 1):
                rows = slice(j * ATT_BLK, (j + 1) * ATT_BLK)
                before = slice((j - 1) * ATT_BLK, j * ATT_BLK)
                if j < rb:
                    q, do, lse_q, dl_q = q_ref[rows, hs], do_ref[rows, hs], l_ref[rows, hs], dl_ref[rows, hs]
                else:
                    q, do, lse_q, dl_q = qn_ref[:, hs], don_ref[:, hs], ln_ref[:, hs], dln_ref[:, hs]
                if j == 0:
                    k0, v0, m0 = kp_ref[:, hs], vp_ref[:, hs], first_m
                else:
                    k0, v0, m0 = k_ref[before, hs], v_ref[before, hs], (prev_m if j < rb else next_m)
                p0 = jnp.where(m0, jnp.exp(_dot_nt(q, k0) * scale - lse_q), 0.0)
                ds0 = _bf(p0 * (_dot_nt(do, v0) - dl_q) * scale)
                if j >= 1:
                    dk[j - 1] = dk[j - 1] + _dot_tn(ds0, q)
                    dv[j - 1] = dv[j - 1] + _dot_tn(_bf(p0), do)
                if j < rb:
                    k1, v1 = k_ref[rows, hs], v_ref[rows, hs]
                    p1 = jnp.where(own_m, jnp.exp(_dot_nt(q, k1) * scale - lse_q), 0.0)
                    ds1 = _bf(p1 * (_dot_nt(do, v1) - dl_q) * scale)
                    dq_ref[rows, hs] = _dot(ds0, k0) + _dot(ds1, k1)
                    dk[j] = _dot_tn(ds1, q)
                    dv[j] = _dot_tn(_bf(p1), do)
            for j in range(rb):
                rows = slice(j * ATT_BLK, (j + 1) * ATT_BLK)
                dk_ref[rows, hs] = dk[j]
                dv_ref[rows, hs] = dv[j].astype(dv_ref.dtype)

    own = pl.BlockSpec((rb * ATT_BLK, ATT_GW), lambda r, n: (n, r))
    prev = pl.BlockSpec((ATT_BLK, ATT_GW), lambda r, n: (jnp.maximum(n * rb - 1, 0), r))
    nxt = pl.BlockSpec((ATT_BLK, ATT_GW), lambda r, n: (jnp.minimum((n + 1) * rb, nb - 1), r))
    view = lambda a: a.reshape(length, d * ATT_GW)
    dq, dk, dv = pl.pallas_call(
        body, grid=(d, nsteps), in_specs=[own] * 6 + [prev, prev] + [nxt] * 4, out_specs=[own, own, own],
        out_shape=[jax.ShapeDtypeStruct((length, d * ATT_GW), F32), jax.ShapeDtypeStruct((length, d * ATT_GW), F32),
                   jax.ShapeDtypeStruct((length, d * ATT_GW), BF16)],
        name=name, compiler_params=_params(("parallel", "arbitrary")))(
            view(qg), view(kg), view(vg), view(dog), view(lse), view(delta), view(kg), view(vg),
            view(qg), view(dog), view(lse), view(delta))
    return dq.reshape(t, ATT_GW), dk.reshape(t, ATT_GW), dv.reshape(t, ATT_GW)


def _rope_tables(t):
    pos = jnp.arange(t, dtype=F32)
    inv = ROPE_THETA ** (-jnp.arange(0, HEAD, 2, dtype=F32) / HEAD)
    ang = pos[:, None] * inv[None, :]
    ang = jnp.concatenate([ang, ang], axis=-1)
    return jnp.cos(ang), jnp.sin(ang)


def _lower_bounds(logits):
    lb = jnp.cumsum(jax.nn.softmax(logits.astype(F32), axis=0), axis=0)
    return lb - lb[0:1]


FFN_ROWS = 256
FF_SHARD = 2 * D_FF // N_CHIPS


def _ffn_in_act(x, g, w_in, name, rider=None):
    t = x.shape[0]

    def body(x_ref, g_ref, w_ref, h_ref, ab_ref, u_ref):
        xv = x_ref[...]
        h = _bf(xv * _rms_rows(xv) * g_ref[...])
        h_ref[...] = h
        for s in range(N_CHIPS // 2):
            cols = slice(s * FF_SHARD, (s + 1) * FF_SHARD)
            a = _dot(h, w_ref[s])
            b = _dot(h, w_ref[s + N_CHIPS // 2])
            ab_ref[:, cols] = a.astype(ab_ref.dtype)
            ab_ref[:, D_FF + s * FF_SHARD:D_FF + (s + 1) * FF_SHARD] = b.astype(ab_ref.dtype)
            u_ref[:, cols] = (a * _sig(a) * b).astype(u_ref.dtype)

    row = lambda w: pl.BlockSpec((FFN_ROWS, w), lambda i: (i, 0))
    return _pcall(
        body, grid=(t // FFN_ROWS,),
        in_specs=[row(D_MODEL), pl.BlockSpec((1, D_MODEL), lambda i: (0, 0)),
                  pl.BlockSpec(w_in.shape, lambda i: (0, 0, 0))],
        out_specs=[row(D_MODEL), row(2 * D_FF), row(D_FF)],
        out_shape=[jax.ShapeDtypeStruct((t, D_MODEL), BF16), jax.ShapeDtypeStruct((t, 2 * D_FF), BF16),
                   jax.ShapeDtypeStruct((t, D_FF), BF16)],
        name=name, sem=("parallel",), args=(x, g, w_in), rider=rider)


def _ffn_bwd_du_act(dx, w_out, ab, name):
    t = dx.shape[0]

    def body(dx_ref, w_ref, ab_ref, o_ref):
        du = 0.5 * _dot_nt(_bf(dx_ref[...]), w_ref[0])
        a = ab_ref[:, :D_FF].astype(F32)
        b = ab_ref[:, D_FF:].astype(F32)
        s = _sig(a)
        o_ref[:, :D_FF] = (du * b * (s * (1.0 + a * (1.0 - s)))).astype(o_ref.dtype)
        o_ref[:, D_FF:] = (du * a * s).astype(o_ref.dtype)

    row = lambda w: pl.BlockSpec((FFN_ROWS, w), lambda i: (i, 0))
    return pl.pallas_call(
        body, grid=(t // FFN_ROWS,),
        in_specs=[row(D_MODEL), pl.BlockSpec(w_out.shape, lambda i: (0, 0, 0)), row(2 * D_FF)],
        out_specs=row(2 * D_FF), out_shape=jax.ShapeDtypeStruct((t, 2 * D_FF), BF16),
        name=name, compiler_params=_params(("parallel",)))(dx, w_out, ab)


def _ffn_fwd(x, g, w, src, tag, pre):
    h, ab, u = _ffn_in_act(x, g, w[pre + "_w_in"], name=tag + "_in_act", rider=src.ride(tag + "_in_act"))
    y = _mm_nn(u, w[pre + "_w_out"], name=tag + "_out", tm=512, tn=D_MODEL, out_dtype=F32, res=x, alpha=0.5)
    return y, (x, h, ab, u)


def _ffn_bwd(dx, saved, g, w, src, tag, pre):
    x, h, ab, u = saved
    w_in, w_out = w[pre + "_w_in"], w[pre + "_w_out"]
    g_out = _mm_tn(u, dx, nb=1, name=tag + "_bwd_wout", tm=1024, tk=1408, tn=D_MODEL, alpha=0.5)
    dab = _ffn_bwd_du_act(dx, w_out, ab, name=tag + "_bwd_du_act")
    g_in = _mm_tn(h, dab, nb=N_CHIPS, name=tag + "_bwd_win", tm=2048, tk=D_MODEL, tn=FF_SHARD, rider=src.ride(tag + "_bwd_win"))
    dh = _mm_nt(dab, w_in, name=tag + "_bwd_dh", tm=1024, tp=D_MODEL, tn=FF_SHARD, out_dtype=F32, rider=src.ride(tag + "_bwd_dh"))
    dx, dg = _norm_bwd(dh, x, g, dx, name=tag + "_bwd_norm")
    return dx, dg, {pre + "_w_in": g_in, pre + "_w_out": g_out.reshape(N_CHIPS, D_FF // N_CHIPS, D_MODEL)}


def _mix_fwd(x, w, small, lb, cos, sin, src, tag):
    h = _norm_fwd(x, small["mix_norm"], name=tag + "_norm")
    proj = _mm_nn(h, w["w_in"], name=tag + "_in", tm=1024, tn=896, out_dtype=F32, rider=src.ride(tag + "_in"))
    oscan, oa, sall = _hgrn_fwd(proj, lb, small["hgrn_out_norm"], name=tag + "_hgrn", rider=src.ride(tag + "_hgrn"))
    qk = _qk_fwd(proj, cos, sin, small["attn_q_norm"], small["attn_k_norm"], name=tag + "_qk")
    outs, lses = [], []
    for g in range(ATT_GROUPS):
        o, l = _attn_fwd(qk[g], qk[3 + g], qk[6 + g], g, name=f"{tag}_attn{g}")
        outs.append(o)
        lses.append(l)
    ob = _merge_fwd(outs, lses, name=tag + "_merge")
    ya = _mm_nn(oa, w["w_branch_a"], name=tag + "_wa", tm=512, tn=D_MODEL, out_dtype=F32)
    yb = _mm_nn(ob, w["w_branch_b"], name=tag + "_wb", tm=512, tn=256, out_dtype=F32)
    merged = _gate_fwd(proj, ya, yb, name=tag + "_gate")
    y = _mm_nn(merged, w["w_out"], name=tag + "_out", tm=512, tn=D_MODEL, out_dtype=F32, res=x)
    return y, (x, h, proj, oscan, oa, sall, qk, outs, lses, ob, ya, yb, merged)


def _mix_bwd(dx, saved, w, small, lb, cos, sin, src, tag, lb_live):
    x, h, proj, oscan, oa, sall, qk, outs, lses, ob, ya, yb, merged = saved
    dm = _mm_nt(dx, w["w_out"], name=tag + "_bwd_dm", tm=1024, tp=D_MODEL, tn=D_MODEL, out_dtype=F32)
    g_wout = _mm_tn(merged, dx, nb=1, name=tag + "_bwd_wout", tm=1024, tk=D_MODEL, tn=D_MODEL)
    dya, dyb, dgab = _gate_bwd(dm, proj, ya, yb, name=tag + "_bwd_gate")
    doa = _mm_nt(dya, w["w_branch_a"], name=tag + "_bwd_doa", tm=1024, tp=D_MODEL, tn=D_MODEL, out_dtype=F32)
    g_wa = _mm_tn(oa, dya, nb=1, name=tag + "_bwd_wa", tm=1024, tk=D_MODEL, tn=D_MODEL)
    dob = _mm_nt(dyb, w["w_branch_b"], name=tag + "_bwd_dob", tm=1024, tp=ATT_GW, tn=256, out_dtype=F32)
    g_wb = _mm_tn(ob, dyb, nb=N_CHIPS, name=tag + "_bwd_wb", tm=2048, tk=ATT_GW, tn=256)
    mb = _merge_bwd(dob, outs, lses, name=tag + "_bwd_merge")
    dqk, dvs = [None] * 6, []
    for g in range(ATT_GROUPS):
        dq, dk, dv = _attn_bwd(qk[g], qk[3 + g], qk[6 + g], mb[g], lses[g], mb[3 + g], g, name=f"{tag}_bwd_attn{g}")
        dqk[g], dqk[3 + g] = dq, dk
        dvs.append(dv)
    dqk_cols, dqn, dkn = _qk_bwd(dqk, proj, cos, sin, small["attn_q_norm"], small["attn_k_norm"], name=tag + "_bwd_qk")
    dh4, dgn, dlb = _hgrn_bwd(doa, oscan, proj, sall, lb, small["hgrn_out_norm"], name=tag + "_bwd_hgrn", precise=lb_live,
                              rider=src.ride(tag + "_bwd_hgrn"))
    dproj = _assemble_dproj(dh4, dqk_cols, dvs, dgab, name=tag + "_bwd_cat")
    g_win = _mm_tn(h, dproj, nb=N_CHIPS, name=tag + "_bwd_win", tm=2048, tk=D_MODEL, tn=896, rider=src.ride(tag + "_bwd_win"))
    dh = _mm_nt(dproj, w["w_in"], name=tag + "_bwd_dh", tm=1024, tp=D_MODEL, tn=2688, out_dtype=F32, rider=src.ride(tag + "_bwd_dh"))
    dx, dg = _norm_bwd(dh, x, small["mix_norm"], dx, name=tag + "_bwd_norm")
    big_in = dict(w_in=g_win)
    big_small = dict(w_branch_a=g_wa.reshape(N_CHIPS, D_MODEL // N_CHIPS, D_MODEL), w_branch_b=g_wb,
                     w_out=g_wout.reshape(N_CHIPS, D_MODEL // N_CHIPS, D_MODEL))
    grads = dict(mix_norm=dg, hgrn_out_norm=dgn, lb=dlb, attn_q_norm=dqn, attn_k_norm=dkn)
    return dx, big_in, big_small, grads


BIG = ("ffn1_w_in", "ffn1_w_out", "w_in", "w_branch_a", "w_branch_b", "w_out", "ffn2_w_in", "ffn2_w_out")
ROW_SHARDED = ("ffn1_w_out", "w_branch_a", "w_out", "ffn2_w_out")
GROUPS = {"ffn1": ("ffn1_w_in", "ffn1_w_out"), "mix_in": ("w_in",), "mix_small": ("w_branch_a", "w_branch_b", "w_out"),
          "ffn2": ("ffn2_w_in", "ffn2_w_out")}
SMALL = ("ffn1_norm", "mix_norm", "hgrn_lb_logits", "hgrn_out_norm", "attn_q_norm", "attn_k_norm", "ffn2_norm")
WEIGHTS = ("ffn1_norm", "ffn1_w_in", "ffn1_w_out", "mix_norm", "w_in", "hgrn_lb_logits", "hgrn_out_norm", "attn_q_norm",
           "attn_k_norm", "w_branch_a", "w_branch_b", "w_out", "ffn2_norm", "ffn2_w_in", "ffn2_w_out")
SMALL_ROWS = 8


def _matmul_ready(name, a):
    return a.reshape(1, a.shape[0] * a.shape[1], a.shape[2]) if name in ROW_SHARDED else a


def _layer_small(small, l):
    s = {n: small[n][l].reshape(1, D_MODEL) for n in ("ffn1_norm", "mix_norm", "hgrn_out_norm", "ffn2_norm")}
    s.update({n: small[n][l] for n in ("attn_q_norm", "attn_k_norm")})
    return s


def _local_step(x, target, small, src):
    t = x.shape[0]
    cos, sin = _rope_tables(t)
    lbs = _lower_bounds(small["hgrn_lb_logits"])
    weights = lambda l, groups: {n: _matmul_ready(n, a) for g in groups for n, a in src.weights(l, g).items()}
    saved = []
    for l in range(2):
        sm = _layer_small(small, l)
        lb = lbs[l].reshape(1, D_MODEL)
        w1 = weights(l, ("ffn1",))
        x, s1 = _ffn_fwd(x, sm["ffn1_norm"], w1, src, f"l{l}_ffn1", "ffn1")
        wm = weights(l, ("mix_in", "mix_small"))
        x, s2 = _mix_fwd(x, wm, sm, lb, cos, sin, src, f"l{l}_mix")
        w2 = weights(l, ("ffn2",))
        x, s3 = _ffn_fwd(x, sm["ffn2_norm"], w2, src, f"l{l}_ffn2", "ffn2")
        saved.append((sm, lb, w1, wm, w2, s1, s2, s3))
    dx, sq = _loss_fwd_bwd(x, target, name="loss")
    small_rows = [None, None]
    for l in (1, 0):
        sm, lb, w1, wm, w2, s1, s2, s3 = saved[l]
        dx, dg2, big = _ffn_bwd(dx, s3, sm["ffn2_norm"], w2, src, f"l{l}_ffn2", "ffn2")
        src.grads(l, "ffn2", big)
        dx, big_in, big_small, g = _mix_bwd(dx, s2, wm, sm, lb, cos, sin, src, f"l{l}_mix", lb_live=l > 0)
        src.grads(l, "mix_in", big_in)
        src.grads(l, "mix_small", big_small)
        dx, dg1, big = _ffn_bwd(dx, s1, sm["ffn1_norm"], w1, src, f"l{l}_ffn1", "ffn1")
        src.grads(l, "ffn1", big)
        pad = lambda a: jnp.pad(a[:ATT_GROUPS].reshape(1, ATT_GROUPS * HEAD), ((0, 0), (0, D_MODEL - ATT_GROUPS * HEAD)))
        small_rows[l] = jnp.concatenate(
            [dg1, g["mix_norm"], g["lb"], g["hgrn_out_norm"], pad(g["attn_q_norm"]), pad(g["attn_k_norm"]), dg2,
             jnp.zeros((SMALL_ROWS - 7, D_MODEL), F32)], axis=0)
    return jnp.sum(sq), dx, jnp.concatenate(small_rows, axis=0)


def _coords():
    return lax.axis_index("x"), lax.axis_index("y"), lax.axis_index("c")


def _other_chips(x, y):
    return [(1 - x, y), (x, 1 - y), (1 - x, 1 - y)]


def _half_rows(rows, which):
    return pl.ds(which * (rows // 2), rows // 2)


def _gather_rider(shards):
    n = len(shards)

    def copies(w, full, sems):
        send, recv, fsend, frecv = sems
        x, y, c = _coords()
        slot = 2 * x + y
        chips = _other_chips(x, y)

        def copy(i, j, blk, src, pair, to):
            return pltpu.make_async_remote_copy(src_ref=src, dst_ref=blk, send_sem=pair[0].at[i * 3 + j],
                                                recv_sem=pair[1].at[i * 3 + j], device_id=to, device_id_type=MESH)

        def block(i, chip_slot, core):
            return full[i].at[chip_slot, _half_rows(shards[i].shape[0], core)]

        pairs = [(i, j, chip) for i in range(n) for j, chip in enumerate(chips)]

        def first():
            return [copy(i, j, block(i, slot, c), w[i].at[_half_rows(shards[i].shape[0], c)], (send, recv), (*chip, c))
                    for i, j, chip in pairs]

        def landed(core, pair):
            return [copy(i, j, block(i, 2 * chip[0] + chip[1], core), block(i, 2 * chip[0] + chip[1], core), pair, (x, y, 1 - c))
                    for i, j, chip in pairs]

        return first, landed

    def begin(w, full, sems):
        for cp in copies(w, full, sems)[0]():
            cp.start()

    def end(w, full, sems):
        first, landed = copies(w, full, sems)
        forwards = landed(lax.axis_index("c"), sems[2:])
        for arrival, forward in zip(landed(lax.axis_index("c"), sems[:2]), forwards):
            arrival.wait_recv()
            forward.start()
        for cp in landed(1 - lax.axis_index("c"), sems[2:]):
            cp.wait_recv()
        for cp in first() + forwards:
            cp.wait_send()

    out_shape = [jax.ShapeDtypeStruct((N_CHIPS,) + s.shape, s.dtype) for s in shards]
    return _Rider(shards, out_shape, [pltpu.SemaphoreType.DMA((3 * n,))] * 4, begin, end)


N_RECV = 7


def _scatter_rider(parts):
    n = len(parts)

    def copies(p, out, sems):
        send, recv = sems
        x, y, c = _coords()
        slot = 2 * x + y
        chips = _other_chips(x, y)

        def arrivals():
            return [pltpu.make_async_remote_copy(
                src_ref=out[i].at[k], dst_ref=out[i].at[k], send_sem=send.at[0], recv_sem=recv.at[i * N_RECV + k],
                device_id=(x, y, c), device_id_type=MESH) for i in range(n) for k in range(N_RECV)]

        sends = []
        for i in range(n):
            rows = parts[i].shape[1]
            for j, chip in enumerate(chips):
                for core in (0, 1):
                    sends.append(pltpu.make_async_remote_copy(
                        src_ref=p[i].at[2 * chip[0] + chip[1], _half_rows(rows, core)], dst_ref=out[i].at[2 * j + c],
                        send_sem=send.at[i * N_RECV + 2 * j + core], recv_sem=recv.at[i * N_RECV + 2 * j + c],
                        device_id=(*chip, core), device_id_type=MESH))
            sends.append(pltpu.make_async_remote_copy(
                src_ref=p[i].at[slot, _half_rows(rows, 1 - c)], dst_ref=out[i].at[6], send_sem=send.at[i * N_RECV + 6],
                recv_sem=recv.at[i * N_RECV + 6], device_id=(x, y, 1 - c), device_id_type=MESH))
        return sends, arrivals

    def begin(p, out, sems):
        for cp in copies(p, out, sems)[0]:
            cp.start()

    def end(p, out, sems):
        sends, arrivals = copies(p, out, sems)
        for cp in arrivals():
            cp.wait_recv()
        for cp in sends:
            cp.wait_send()

    out_shape = [jax.ShapeDtypeStruct((N_RECV, a.shape[1] // 2, a.shape[2]), a.dtype) for a in parts]
    return _Rider(parts, out_shape, [pltpu.SemaphoreType.DMA((N_RECV * n,))] * 2, begin, end)


def _run_alone(rider, name):
    _pcall(lambda: None, grid=(), in_specs=[], out_specs=[], out_shape=[], name=name, sem=(), args=(), rider=rider)
    return rider.result


def _sum_partials(own, parts, name):
    r, wd = own.shape
    tm = next(t for t in (256, 128, 64, 32, 16) if r % t == 0)

    def body(own_ref, p_ref, o_ref):
        acc = own_ref[...].astype(F32)
        for k in range(N_RECV):
            acc = acc + p_ref[k].astype(F32)
        o_ref[...] = acc

    return pl.pallas_call(
        body, grid=(r // tm,),
        in_specs=[pl.BlockSpec((tm, wd), lambda i: (i, 0)), pl.BlockSpec((N_RECV, tm, wd), lambda i: (0, i, 0))],
        out_specs=pl.BlockSpec((tm, wd), lambda i: (i, 0)), out_shape=jax.ShapeDtypeStruct((r, wd), F32),
        name=name, compiler_params=_params(("parallel",)))(own, parts)


def _exchange_halves(reduced, name):
    n = len(reduced)

    def body(*refs):
        r, out = refs[:n], refs[n:2 * n]
        send, recv = refs[2 * n:]
        x, y, c = _coords()
        sib = [pltpu.make_async_remote_copy(src_ref=r[i], dst_ref=out[i], send_sem=send.at[i], recv_sem=recv.at[i],
                                            device_id=(x, y, 1 - c), device_id_type=MESH) for i in range(n)]
        for cp in sib:
            cp.start()
        for cp in sib:
            cp.wait_recv()
        for cp in sib:
            cp.wait_send()

    out_shape = [jax.ShapeDtypeStruct(a.shape, a.dtype) for a in reduced]
    return pl.pallas_call(body, in_specs=[ANY] * n, out_specs=[ANY] * n, out_shape=out_shape,
                          scratch_shapes=[pltpu.SemaphoreType.DMA((n,))] * 2, name=name)(*reduced)


def _reduce_finish(parts, recv, tag):
    x, y, c = _coords()
    slot = 2 * x + y
    halves = []
    for i, (p, r) in enumerate(zip(parts, recv)):
        half = p.shape[1] // 2
        own = lax.dynamic_slice(p, (slot, c * half, 0), (1, half, p.shape[2]))[0]
        halves.append(_sum_partials(own, r, name=f"{tag}_sum{i}"))
    theirs = _exchange_halves(halves, name=tag + "_exchange")
    return [jnp.where(c == 0, jnp.concatenate([h, t], axis=0), jnp.concatenate([t, h], axis=0)) for h, t in zip(halves, theirs)]


GATHER_RIDES = {
    "l0_ffn1_in_act": ((0, "mix_in"), (0, "mix_small")),
    "l0_mix_in": ((0, "ffn2"), (1, "ffn1")),
    "l0_mix_hgrn": ((1, "mix_in"), (1, "mix_small")),
    "l0_ffn2_in_act": ((1, "ffn2"),),
}
SCATTER_RIDES = {
    "l1_mix_bwd_hgrn": ((1, "ffn2"),),
    "l0_mix_bwd_hgrn": ((1, "mix_in"), (1, "mix_small")),
    "l0_mix_bwd_win": ((1, "ffn1"),),
    "l0_mix_bwd_dh": ((0, "ffn2"),),
    "l0_ffn1_bwd_win": ((0, "mix_small"),),
    "l0_ffn1_bwd_dh": ((0, "mix_in"),),
}


class _Exchange:
    def __init__(self, shards):
        self.shards = shards
        self.pending = []
        self.full = {}
        self.parts = {}
        self.recv = {}

    def _keys(self, items):
        return [(l, n) for l, g in items for n in GROUPS[g]]

    def _gather(self, items):
        keys = self._keys(items)
        return _gather_rider([self.shards[n][l] for l, n in keys]), "gather", keys

    def _scatter(self, items):
        keys = self._keys(items)
        return _scatter_rider([self.parts[k] for k in keys]), "scatter", keys

    def _unpack(self):
        slot = 2 * lax.axis_index("x") + lax.axis_index("y")
        waiting = []
        for rider, kind, keys in self.pending:
            if rider.result is None:
                waiting.append((rider, kind, keys))
            elif kind == "gather":
                for (l, n), got in zip(keys, rider.result):
                    self.full[(l, n)] = lax.dynamic_update_slice(got, self.shards[n][l][None], (slot, 0, 0))
            else:
                self.recv.update(zip(keys, rider.result))
        self.pending = waiting

    def ride(self, host):
        if host in GATHER_RIDES:
            self.pending.append(self._gather(GATHER_RIDES[host]))
        elif host in SCATTER_RIDES:
            self.pending.append(self._scatter(SCATTER_RIDES[host]))
        else:
            return None
        return self.pending[-1][0]

    def weights(self, l, group):
        self._unpack()
        if (l, GROUPS[group][0]) not in self.full:
            job = self._gather(((l, group),))
            _run_alone(job[0], name=f"gather_l{l}_{group}")
            self.pending.append(job)
            self._unpack()
        return {n: self.full[(l, n)] for n in GROUPS[group]}

    def grads(self, l, group, partials):
        self.parts.update({(l, n): a for n, a in partials.items()})

    def reduce(self):
        self._unpack()
        left = [k for k in self.parts if k not in self.recv]
        rider = _scatter_rider([self.parts[k] for k in left])
        self.recv.update(zip(left, _run_alone(rider, name="scatter_last")))
        out = {}
        for l in range(2):
            done = _reduce_finish([self.parts[(l, n)] for n in BIG], [self.recv[(l, n)] for n in BIG], f"reduce_l{l}")
            out[l] = dict(zip(BIG, done))
        return {n: jnp.stack([out[0][n], out[1][n]], axis=0) for n in BIG}


def _all_reduce_small(rows):
    r = rows.shape[0]

    def body(x_ref, o_ref, buf, send, recv):
        x, y, c = _coords()
        me = 4 * x + 2 * y + c
        buf[me] = x_ref[...]
        copies = []
        for k in range(1, 8):
            peer = (x ^ (k >> 2), y ^ ((k >> 1) & 1), c ^ (k & 1))
            cp = pltpu.make_async_remote_copy(src_ref=x_ref, dst_ref=buf.at[me], send_sem=send.at[k - 1], recv_sem=recv.at[me],
                                              device_id=peer, device_id_type=MESH)
            cp.start()
            copies.append(cp)
        for k in range(1, 8):
            src = 4 * (x ^ (k >> 2)) + 2 * (y ^ ((k >> 1) & 1)) + (c ^ (k & 1))
            pltpu.make_async_remote_copy(src_ref=x_ref, dst_ref=buf.at[src], send_sem=send.at[0], recv_sem=recv.at[src],
                                         device_id=(x, y, c), device_id_type=MESH).wait_recv()
        for cp in copies:
            cp.wait_send()
        acc = buf[0]
        for k in range(1, 8):
            acc = acc + buf[k]
        o_ref[...] = acc

    vm = pl.BlockSpec(memory_space=pltpu.VMEM)
    return pl.pallas_call(
        body, in_specs=[vm], out_specs=vm, out_shape=jax.ShapeDtypeStruct(rows.shape, F32),
        scratch_shapes=[pltpu.VMEM((8, r, D_MODEL), F32), pltpu.SemaphoreType.DMA((7,)), pltpu.SemaphoreType.DMA((8,))],
        name="all_reduce_small")(rows)


def _adamw_math(w, g, m, v):
    m = ADAM_B1 * m + (1.0 - ADAM_B1) * g
    v = ADAM_B2 * v + (1.0 - ADAM_B2) * (g * g)
    m_hat = m / (1.0 - ADAM_B1 ** ADAM_STEP)
    v_hat = v / (1.0 - ADAM_B2 ** ADAM_STEP)
    return -ADAM_LR * (m_hat / (jnp.sqrt(v_hat) + ADAM_EPS) + ADAM_WD * w), m, v


def _adamw(w, g, m, v, name):
    shape = w.shape
    cols = shape[-1]
    flat = lambda a: a.reshape(-1, cols)
    rows = flat(w).shape[0]
    tm = 128 if rows % 128 == 0 else rows
    ins = [('t', flat(a), cols, 0) for a in (w, g, m, v)]
    res = _ew(_adamw_math, ins, [('t', cols, F32)] * 3, rows=rows, tm=tm, name=name)
    return [a.reshape(shape) for a in res]


def _small_update(sums, logits, w, m, v):
    def body(s_ref, lg_ref, w_ref, m_ref, v_ref, g_ref, d_ref, nm_ref, nv_ref):
        s = s_ref[...]
        l0, l1 = lg_ref[0:1, :], lg_ref[1:2, :]
        mx = jnp.maximum(l0, l1)
        e0, e1 = jnp.exp(l0 - mx), jnp.exp(l1 - mx)
        sm0, sm1 = e0 / (e0 + e1), e1 / (e0 + e1)
        dl1 = s_ref[SMALL_ROWS + 2:SMALL_ROWS + 3, :] * sm0 * sm1
        row = lax.broadcasted_iota(jnp.int32, s.shape, 0)
        g = jnp.where(row == 2, -dl1, jnp.where(row == SMALL_ROWS + 2, dl1, s))
        d, nm, nv = _adamw_math(w_ref[...], g, m_ref[...], v_ref[...])
        g_ref[...] = g
        d_ref[...] = d
        nm_ref[...] = nm
        nv_ref[...] = nv

    vm = pl.BlockSpec(memory_space=pltpu.VMEM)
    return pl.pallas_call(body, in_specs=[vm] * 5, out_specs=[vm] * 4,
                          out_shape=[jax.ShapeDtypeStruct(sums.shape, F32)] * 4, name="small_update")(sums, logits, w, m, v)


def _pack_small(vals):
    rows = []
    for l in range(2):
        for n in ("ffn1_norm", "mix_norm", "hgrn_lb_logits", "hgrn_out_norm", "attn_q_norm", "attn_k_norm", "ffn2_norm"):
            a = vals[n][l].reshape(1, -1)
            rows.append(jnp.pad(a, ((0, 0), (0, D_MODEL - a.shape[1]))))
        rows.append(jnp.zeros((SMALL_ROWS - 7, D_MODEL), F32))
    return jnp.concatenate(rows, axis=0)


def _unpack_small(packed):
    out = {}
    for k, n in enumerate(("ffn1_norm", "mix_norm", "hgrn_lb_logits", "hgrn_out_norm", "attn_q_norm", "attn_k_norm", "ffn2_norm")):
        a = jnp.stack([packed[k], packed[SMALL_ROWS + k]], axis=0)
        out[n] = a[:, :ATT_GROUPS * HEAD].reshape(2, ATT_GROUPS, HEAD) if n.startswith("attn") else a
    return out


def kernel(x, ffn1_norm, ffn1_w_in, ffn1_w_out, mix_norm, w_in, hgrn_lb_logits, hgrn_out_norm, attn_q_norm, attn_k_norm, w_branch_a, w_branch_b, w_out, ffn2_norm, ffn2_w_in, ffn2_w_out, loss_target, m_ffn1_norm, m_ffn1_w_in, m_ffn1_w_out, m_mix_norm, m_w_in, m_hgrn_lb_logits, m_hgrn_out_norm, m_attn_q_norm, m_attn_k_norm, m_w_branch_a, m_w_branch_b, m_w_out, m_ffn2_norm, m_ffn2_w_in, m_ffn2_w_out, v_ffn1_norm, v_ffn1_w_in, v_ffn1_w_out, v_mix_norm, v_w_in, v_hgrn_lb_logits, v_hgrn_out_norm, v_attn_q_norm, v_attn_k_norm, v_w_branch_a, v_w_branch_b, v_w_out, v_ffn2_norm, v_ffn2_w_in, v_ffn2_w_out):
    a = locals()
    w = {n: a[n] for n in WEIGHTS}
    m = {n: a["m_" + n] for n in WEIGHTS}
    v = {n: a["v_" + n] for n in WEIGHTS}

    exchange = _Exchange({n: w[n].astype(BF16) for n in BIG})
    small = {n: w[n] for n in SMALL}
    sq, grad_x, small_rows = _local_step(x[0], loss_target[0], small, exchange)
    loss = lax.psum(sq, ("x", "y", "c")) * (0.5 / D_MODEL)
    grads = exchange.reduce()

    sums = _all_reduce_small(small_rows)
    g_s, d_s, m_s, v_s = _small_update(sums, w["hgrn_lb_logits"], _pack_small(small), _pack_small({n: m[n] for n in SMALL}),
                                       _pack_small({n: v[n] for n in SMALL}))
    grads.update(_unpack_small(g_s))
    delta, new_m, new_v = _unpack_small(d_s), _unpack_small(m_s), _unpack_small(v_s)
    for n in BIG:
        delta[n], new_m[n], new_v[n] = _adamw(w[n], grads[n], m[n], v[n], name="adamw_" + n)

    return (loss, grad_x[None], *[grads[n] for n in WEIGHTS], *[delta[n] for n in WEIGHTS],
            *[new_m[n] for n in WEIGHTS], *[new_v[n] for n in WEIGHTS])
```

```python
import functools

import jax
import jax.numpy as jnp
from jax import lax
from jax.experimental import pallas as pl
from jax.experimental.pallas import tpu as pltpu

F32 = jnp.float32
BF16 = jnp.bfloat16
MESH = pl.DeviceIdType.MESH

D_MODEL = 1024
D_FF = 2816
N_CHIPS = 4
HEAD = 128
HG_HEADS = 8
HG_CHUNK = 64
ATT_GROUPS = 3
ATT_HEADS = 4
ATT_GW = ATT_HEADS * HEAD
DILATIONS = (1, 4, 16)
ATT_BLK = 128
ATT_STEP_BLOCKS = 4
P_IN = 10752
CB_AQ, CB_AK, CB_AV, CB_GA, CB_GB = 8, 11, 14, 17, 19
EPS = 1e-6
ROPE_THETA = 10000.0
ADAM_LR, ADAM_B1, ADAM_B2, ADAM_EPS, ADAM_WD, ADAM_STEP = 0.001, 0.9, 0.999, 1e-08, 0.01, 10
VMEM_LIMIT_V7X = 56 * 1024 * 1024
NEG = -1e30


def _params(sem):
    return pltpu.CompilerParams(dimension_semantics=sem, vmem_limit_bytes=VMEM_LIMIT_V7X)


def _sig(x):
    return 1.0 / (1.0 + jnp.exp(-x))


def _dot(a, b):
    return jnp.dot(a, b, preferred_element_type=F32)


def _dot_nt(a, b):
    return lax.dot_general(a, b, (((1,), (1,)), ((), ())), preferred_element_type=F32)


def _dot_tn(a, b):
    return lax.dot_general(a, b, (((0,), (0,)), ((), ())), preferred_element_type=F32)


def _bf(x):
    return x.astype(BF16)


ANY = pl.BlockSpec(memory_space=pl.ANY)


class _Rider:
    def __init__(self, args, out_shape, sems, begin, end):
        self.args, self.out_shape, self.sems, self.begin, self.end = list(args), list(out_shape), list(sems), begin, end
        self.result = None


def _pcall(body, *, grid, in_specs, out_specs, out_shape, name, sem, args, scratch_shapes=(), rider=None):
    multi = isinstance(out_shape, (list, tuple))
    o_specs = list(out_specs) if multi else [out_specs]
    o_shape = list(out_shape) if multi else [out_shape]
    if rider is None:
        res = pl.pallas_call(body, grid=grid, in_specs=list(in_specs), out_specs=o_specs, out_shape=o_shape,
                             scratch_shapes=list(scratch_shapes), name=name, compiler_params=_params(sem))(*args)
        return list(res) if multi else res[0]
    counts = [len(in_specs), len(rider.args), len(o_specs), len(rider.out_shape), len(scratch_shapes)]

    def wrapped(*refs):
        groups, at = [], 0
        for c in counts:
            groups.append(refs[at:at + c])
            at += c
        h_in, r_in, h_out, r_out, h_scratch = groups
        r_sems = refs[at:]
        if grid:
            ids = [pl.program_id(a) for a in range(len(grid))]
            first = functools.reduce(jnp.logical_and, [i == 0 for i in ids])
            last = functools.reduce(jnp.logical_and, [i == g - 1 for i, g in zip(ids, grid)])
            pl.when(first)(lambda: rider.begin(r_in, r_out, r_sems))
            body(*h_in, *h_out, *h_scratch)
            pl.when(last)(lambda: rider.end(r_in, r_out, r_sems))
        else:
            rider.begin(r_in, r_out, r_sems)
            body(*h_in, *h_out, *h_scratch)
            rider.end(r_in, r_out, r_sems)

    res = pl.pallas_call(
        wrapped, grid=grid, in_specs=list(in_specs) + [ANY] * counts[1], out_specs=o_specs + [ANY] * counts[3],
        out_shape=o_shape + rider.out_shape, scratch_shapes=list(scratch_shapes) + rider.sems, name=name,
        compiler_params=_params(("arbitrary",) * len(grid)))(*args, *rider.args)
    rider.result = list(res[counts[2]:])
    return list(res[:counts[2]]) if multi else res[0]


def _mm_nn(a, b3, *, name, tm, tn, out_dtype, res=None, alpha=1.0, rider=None):
    m, k = a.shape
    nb, _, nw = b3.shape
    per = nw // tn
    assert nw % tn == 0 and m % tm == 0
    has_res = res is not None

    def body(*refs):
        if has_res:
            a_ref, b_ref, r_ref, o_ref = refs
        else:
            a_ref, b_ref, o_ref = refs
        acc = _dot(_bf(a_ref[...]), b_ref[...])
        if alpha != 1.0:
            acc = alpha * acc
        if has_res:
            acc = r_ref[...] + acc
        o_ref[...] = acc.astype(o_ref.dtype)

    in_specs = [pl.BlockSpec((tm, k), lambda i, j: (i, 0)),
                pl.BlockSpec((None, k, tn), lambda i, j: (j // per, 0, j % per))]
    args = [a, b3]
    if has_res:
        in_specs.append(pl.BlockSpec((tm, tn), lambda i, j: (i, j)))
        args.append(res)
    return _pcall(body, grid=(m // tm, nb * per), in_specs=in_specs, out_specs=pl.BlockSpec((tm, tn), lambda i, j: (i, j)),
                  out_shape=jax.ShapeDtypeStruct((m, nb * nw), out_dtype), name=name, sem=("parallel", "arbitrary"),
                  args=args, rider=rider)


def _mm_nt(d, b3, *, name, tm, tp, tn, out_dtype, alpha=1.0, rider=None):
    m, n = d.shape
    nb, p, nw = b3.shape
    per = nw // tn
    nk = n // tn
    assert nb * nw == n and nw % tn == 0 and p % tp == 0 and m % tm == 0

    def body(d_ref, b_ref, o_ref, acc_ref):
        kk = pl.program_id(2)

        @pl.when(kk == 0)
        def _():
            acc_ref[...] = jnp.zeros_like(acc_ref)

        acc_ref[...] += _dot_nt(_bf(d_ref[...]), b_ref[...])

        @pl.when(kk == nk - 1)
        def _():
            o_ref[...] = (alpha * acc_ref[...]).astype(o_ref.dtype)

    return _pcall(
        body, grid=(m // tm, p // tp, nk),
        in_specs=[pl.BlockSpec((tm, tn), lambda i, j, kk: (i, kk)),
                  pl.BlockSpec((None, tp, tn), lambda i, j, kk: (kk // per, j, kk % per))],
        out_specs=pl.BlockSpec((tm, tp), lambda i, j, kk: (i, j)),
        out_shape=jax.ShapeDtypeStruct((m, p), out_dtype),
        scratch_shapes=[pltpu.VMEM((tm, tp), F32)],
        name=name, sem=("parallel", "parallel", "arbitrary"), args=(d, b3), rider=rider)


def _mm_tn(a, d, *, nb, name, tm, tk, tn, alpha=1.0, rider=None):
    m, k = a.shape
    _, n = d.shape
    nw = n // nb
    per = nw // tn
    nm = m // tm
    assert nw % tn == 0 and k % tk == 0 and m % tm == 0

    def body(a_ref, d_ref, o_ref, acc_ref):
        mm = pl.program_id(2)

        @pl.when(mm == 0)
        def _():
            acc_ref[...] = jnp.zeros_like(acc_ref)

        acc_ref[...] += _dot_tn(_bf(a_ref[...]), _bf(d_ref[...]))

        @pl.when(mm == nm - 1)
        def _():
            o_ref[...] = (alpha * acc_ref[...]).astype(o_ref.dtype)

    return _pcall(
        body, grid=(k // tk, nb * per, nm),
        in_specs=[pl.BlockSpec((tm, tk), lambda i, j, mm: (mm, i)),
                  pl.BlockSpec((tm, tn), lambda i, j, mm: (mm, j))],
        out_specs=pl.BlockSpec((None, tk, tn), lambda i, j, mm: (j // per, i, j % per)),
        out_shape=jax.ShapeDtypeStruct((nb, k, nw), BF16),
        scratch_shapes=[pltpu.VMEM((tk, tn), F32)],
        name=name, sem=("parallel", "parallel", "arbitrary"), args=(a, d), rider=rider)


def _ew(fn, ins, outs, *, rows, tm, name):
    in_specs, args = [], []
    for s in ins:
        if s[0] == 't':
            _, arr, w, cb = s
            in_specs.append(pl.BlockSpec((tm, w), lambda i, cb=cb: (i, cb)))
        else:
            arr = s[1]
            in_specs.append(pl.BlockSpec(arr.shape, lambda i, nd=arr.ndim: (0,) * nd))
        args.append(arr)
    out_specs, out_shape = [], []
    for s in outs:
        if s[0] == 't':
            _, w, dt = s
            out_specs.append(pl.BlockSpec((tm, w), lambda i: (i, 0)))
            out_shape.append(jax.ShapeDtypeStruct((rows, w), dt))
        else:
            out_specs.append(pl.BlockSpec(s[1], lambda i: (0, 0)))
            out_shape.append(jax.ShapeDtypeStruct(s[1], F32))
    n_in = len(ins)

    def body(*refs):
        res = fn(*[r[...] for r in refs[:n_in]])
        if not isinstance(res, (tuple, list)):
            res = (res,)
        for r, s, v in zip(refs[n_in:], outs, res):
            if s[0] == 't':
                r[...] = v.astype(r.dtype)
            else:
                @pl.when(pl.program_id(0) == 0)
                def _(r=r):
                    r[...] = jnp.zeros_like(r)

                r[...] += v

    res = pl.pallas_call(
        body, grid=(rows // tm,), in_specs=in_specs, out_specs=out_specs, out_shape=out_shape,
        name=name, compiler_params=_params(("arbitrary",)))(*args)
    return res


def _heads(x):
    return [x[:, h * HEAD:(h + 1) * HEAD] for h in range(x.shape[1] // HEAD)]


def _cat(xs):
    return jnp.concatenate(xs, axis=1)


def _head_mean(x):
    return _cat([jnp.broadcast_to(jnp.mean(h, axis=1, keepdims=True), h.shape) for h in _heads(x)])


def _rms_rows(x):
    return lax.rsqrt(jnp.mean(x * x, axis=1, keepdims=True) + EPS)


def _norm_fwd(x, g, name):
    return _ew(lambda xv, gv: xv * _rms_rows(xv) * gv,
               [('t', x, D_MODEL, 0), ('f', g)], [('t', D_MODEL, BF16)], rows=x.shape[0], tm=512, name=name)[0]


def _norm_bwd(dh, x, g, dx, name):
    def fn(dhv, xv, gv, dxv):
        r = _rms_rows(xv)
        xh = xv * r
        dxh = dhv * gv
        out = dxv + r * (dxh - xh * jnp.mean(dxh * xh, axis=1, keepdims=True))
        return out, jnp.sum(dhv * xh, axis=0, keepdims=True)

    return _ew(fn, [('t', dh, D_MODEL, 0), ('t', x, D_MODEL, 0), ('f', g), ('t', dx, D_MODEL, 0)],
               [('t', D_MODEL, F32), ('acc', (1, D_MODEL))], rows=x.shape[0], tm=512, name=name)


def _loss_fwd_bwd(y, target, name):
    def fn(yv, tv):
        e = yv - tv
        return e * (1.0 / D_MODEL), jnp.sum(e * e, axis=0, keepdims=True)

    return _ew(fn, [('t', y, D_MODEL, 0), ('t', target, D_MODEL, 0)], [('t', D_MODEL, F32), ('acc', (1, D_MODEL))],
               rows=y.shape[0], tm=512, name=name)


def _gate_fwd(proj, ya, yb, name):
    def fn(ga0, ga1, gb0, gb1, yav, ybv):
        return _sig(_cat([ga0, ga1])) * yav + _sig(_cat([gb0, gb1])) * ybv

    ins = [('t', proj, 512, CB_GA), ('t', proj, 512, CB_GA + 1), ('t', proj, 512, CB_GB), ('t', proj, 512, CB_GB + 1),
           ('t', ya, D_MODEL, 0), ('t', yb, D_MODEL, 0)]
    return _ew(fn, ins, [('t', D_MODEL, BF16)], rows=ya.shape[0], tm=512, name=name)[0]


def _gate_bwd(dm, proj, ya, yb, name):
    def fn(dmv, ga0, ga1, gb0, gb1, yav, ybv):
        sa = _sig(_cat([ga0, ga1]))
        sb = _sig(_cat([gb0, gb1]))
        return dmv * sa, dmv * sb, _cat([dmv * yav * sa * (1.0 - sa), dmv * ybv * sb * (1.0 - sb)])

    ins = [('t', dm, D_MODEL, 0),
           ('t', proj, 512, CB_GA), ('t', proj, 512, CB_GA + 1), ('t', proj, 512, CB_GB), ('t', proj, 512, CB_GB + 1),
           ('t', ya, D_MODEL, 0), ('t', yb, D_MODEL, 0)]
    return _ew(fn, ins, [('t', D_MODEL, BF16), ('t', D_MODEL, BF16), ('t', 2 * D_MODEL, BF16)],
               rows=ya.shape[0], tm=512, name=name)


def _rot(x):
    sgn = jnp.where(lax.broadcasted_iota(jnp.int32, x.shape, 1) < HEAD // 2, -1.0, 1.0)
    return pltpu.roll(x, HEAD // 2, 1) * sgn


def _gain_rows(qn, kn):
    return [a[g:g + 1] for a in (qn, kn) for g in range(ATT_GROUPS)]


def _qk_fwd(proj, cos, sin, qn, kn, name):
    def fn(*v):
        xs, cosv, sinv, gains, vs = v[:6], v[6], v[7], v[8:14], v[14:17]
        outs = []
        for j, x in enumerate(xs):
            gain = gains[j]
            ys = []
            for xh in _heads(x):
                xn = xh * _rms_rows(xh) * gain
                ys.append(xn * cosv + _rot(xn) * sinv)
            outs.append(_cat(ys))
        return outs + list(vs)

    ins = ([('t', proj, 512, CB_AQ + j) for j in range(6)] + [('t', cos, HEAD, 0), ('t', sin, HEAD, 0)]
           + [('f', a) for a in _gain_rows(qn, kn)] + [('t', proj, 512, CB_AV + g) for g in range(ATT_GROUPS)])
    return _ew(fn, ins, [('t', ATT_GW, BF16)] * 9, rows=proj.shape[0], tm=512, name=name)


def _qk_bwd(dqk, proj, cos, sin, qn, kn, name):
    def fn(*v):
        ds, xs, cosv, sinv, gains = v[:6], v[6:12], v[12], v[13], v[14:20]
        rows8 = lax.broadcasted_iota(jnp.int32, (8, HEAD), 0)
        outs, dgs = [], [jnp.zeros((8, HEAD), F32)] * 2
        for j in range(6):
            gain = gains[j]
            dx, dg = [], jnp.zeros((1, HEAD), F32)
            for dyh, xh in zip(_heads(ds[j]), _heads(xs[j])):
                r = _rms_rows(xh)
                xhat = xh * r
                dxn = dyh * cosv - _rot(dyh * sinv)
                dg = dg + jnp.sum(dxn * xhat, axis=0, keepdims=True)
                dxh = dxn * gain
                dx.append(r * (dxh - xhat * jnp.mean(dxh * xhat, axis=1, keepdims=True)))
            outs.append(_cat(dx))
            dgs[j // 3] = dgs[j // 3] + jnp.where(rows8 == j % 3, dg, 0.0)
        return _cat(outs), dgs[0], dgs[1]

    ins = ([('t', a, ATT_GW, 0) for a in dqk] + [('t', proj, 512, CB_AQ + j) for j in range(6)]
           + [('t', cos, HEAD, 0), ('t', sin, HEAD, 0)] + [('f', a) for a in _gain_rows(qn, kn)])
    return _ew(fn, ins, [('t', 6 * ATT_GW, BF16), ('acc', (8, HEAD)), ('acc', (8, HEAD))],
               rows=proj.shape[0], tm=256, name=name)


def _merge_fwd(outs, lses, name):
    def fn(o0, o1, o2, l0, l1, l2):
        m = jnp.maximum(jnp.maximum(l0, l1), l2)
        e0, e1, e2 = jnp.exp(l0 - m), jnp.exp(l1 - m), jnp.exp(l2 - m)
        return (e0 * o0 + e1 * o1 + e2 * o2) / (e0 + e1 + e2)

    ins = [('t', a, ATT_GW, 0) for a in list(outs) + list(lses)]
    return _ew(fn, ins, [('t', ATT_GW, BF16)], rows=outs[0].shape[0], tm=512, name=name)[0]


def _merge_bwd(dob, outs, lses, name):
    def fn(dov, o0, o1, o2, l0, l1, l2):
        m = jnp.maximum(jnp.maximum(l0, l1), l2)
        e0, e1, e2 = jnp.exp(l0 - m), jnp.exp(l1 - m), jnp.exp(l2 - m)
        inv = 1.0 / (e0 + e1 + e2)
        a0, a1, a2 = e0 * inv, e1 * inv, e2 * inv
        ob = a0 * o0 + a1 * o1 + a2 * o2
        s = _head_mean(dov * ob) * float(HEAD)
        return a0 * dov, a1 * dov, a2 * dov, a0 * s, a1 * s, a2 * s

    ins = [('t', dob, ATT_GW, 0)] + [('t', a, ATT_GW, 0) for a in list(outs) + list(lses)]
    return _ew(fn, ins, [('t', ATT_GW, BF16)] * 3 + [('t', ATT_GW, F32)] * 3, rows=dob.shape[0], tm=512, name=name)


def _assemble_dproj(dh4, dqk, dvs, dgab, name):
    fn = lambda *v: _cat(list(v))
    ins = [('t', dh4, 4 * D_MODEL, 0), ('t', dqk, 6 * ATT_GW, 0)] + [('t', a, ATT_GW, 0) for a in dvs] + [('t', dgab, 2 * D_MODEL, 0)]
    return _ew(fn, ins, [('t', P_IN, BF16)], rows=dh4.shape[0], tm=256, name=name)[0]


HG_ROWS = 256


def _hg_gates(hq, hf, hi, lbv):
    sig = _sig(hf)
    f = lbv + (1.0 - lbv) * sig
    return hq * _sig(hq), 1.0 - f, hi, jnp.log(f), sig, f


def _split3(x):
    hi = _bf(x)
    r1 = x - hi.astype(F32)
    mid = _bf(r1)
    return hi, mid, _bf(r1 - mid.astype(F32))


def _tri_dot(tri, x):
    hi, mid, lo = _split3(x)
    return _dot(tri, hi) + _dot(tri, mid) + _dot(tri, lo)


def _row(x, i):
    rows = lax.broadcasted_iota(jnp.int32, x.shape, 0)
    return jnp.sum(jnp.where(rows == i, x, 0.0), axis=0, keepdims=True)


def _hg_decay(logf, q, k):
    c = HG_CHUNK
    row = lax.broadcasted_iota(jnp.int32, (c, c), 0)
    col = lax.broadcasted_iota(jnp.int32, (c, c), 1)
    g = _tri_dot((row >= col).astype(BF16), logf)
    gm = _row(g, c // 2 - 1)
    gl = _row(g, c - 1)
    return g, gm, gl, q * jnp.exp(g), q * jnp.exp(g - gm), k * jnp.exp(gm - g), k * jnp.exp(gl - g)


def _hg_out_fwd(o, hg, gain):
    r = lax.rsqrt(_head_mean(o * o) + EPS)
    return o * r * gain * (hg * _sig(hg))


def _hgrn_fwd(proj, lb, gain, name, rider=None):
    t = proj.shape[0]
    nck = HG_ROWS // HG_CHUNK

    def body(hq_ref, hf_ref, hi_ref, hg_ref, lb_ref, gn_ref, o_ref, oa_ref, sall_ref, st_ref):
        @pl.when(pl.program_id(0) == 0)
        def _():
            st_ref[...] = jnp.zeros_like(st_ref)

        lbv = lb_ref[...]
        gnv = gn_ref[...]
        c = HG_CHUNK
        mask = lax.broadcasted_iota(jnp.int32, (c, c), 0) >= lax.broadcasted_iota(jnp.int32, (c, c), 1)

        def chunk(cc, carry):
            sl = pl.ds(pl.multiple_of(cc * c, c), c)
            q, k, v, logf, _, _ = _hg_gates(hq_ref[sl, :], hf_ref[sl, :], hi_ref[sl, :], lbv)
            _, _, gl, qg, qt, kt, kd = _hg_decay(logf, q, k)
            egl = jnp.exp(gl)
            os = []
            for h in range(HG_HEADS):
                hs = slice(h * HEAD, (h + 1) * HEAD)
                st = st_ref[h]
                sall_ref[cc, h] = st
                a = jnp.where(mask, _dot_nt(_bf(qt[:, hs]), _bf(kt[:, hs])), 0.0)
                os.append(_dot(_bf(a), _bf(v[:, hs])) + _dot_nt(_bf(qg[:, hs]), _bf(st)))
                st_ref[h] = egl[:, hs] * st + _dot_tn(_bf(v[:, hs]), _bf(kd[:, hs]))
            o = _cat(os)
            o_ref[sl, :] = o
            oa_ref[sl, :] = _hg_out_fwd(o, hg_ref[sl, :], gnv).astype(oa_ref.dtype)
            return carry

        lax.fori_loop(0, nck, chunk, 0)

    col = lambda j: pl.BlockSpec((HG_ROWS, D_MODEL), lambda i, j=j: (i, j))
    small = pl.BlockSpec((1, D_MODEL), lambda i: (0, 0))
    return _pcall(
        body, grid=(t // HG_ROWS,),
        in_specs=[col(0), col(1), col(2), col(3), small, small],
        out_specs=[col(0), col(0), pl.BlockSpec((nck, HG_HEADS, HEAD, HEAD), lambda i: (i, 0, 0, 0))],
        out_shape=[jax.ShapeDtypeStruct((t, D_MODEL), F32), jax.ShapeDtypeStruct((t, D_MODEL), BF16),
                   jax.ShapeDtypeStruct((t // HG_CHUNK, HG_HEADS, HEAD, HEAD), F32)],
        scratch_shapes=[pltpu.VMEM((HG_HEADS, HEAD, HEAD), F32)],
        name=name, sem=("arbitrary",), args=(proj, proj, proj, proj, lb, gain), rider=rider)


def _terms(x, precise):
    hi = _bf(x)
    return (hi, _bf(x - hi.astype(F32))) if precise else (hi,)


def _mm(dot, a, b):
    out = dot(a[0], b[0])
    if len(a) > 1:
        out = out + dot(a[1], b[0])
    if len(b) > 1:
        out = out + dot(a[0], b[1])
    return out


def _hgrn_bwd(doa, oscan, proj, sall, lb, gain, name, precise, rider=None):
    t = proj.shape[0]
    nck = HG_ROWS // HG_CHUNK
    nsteps = t // HG_ROWS
    terms = functools.partial(_terms, precise=precise)

    def body(doa_ref, os_ref, hq_ref, hf_ref, hi_ref, hg_ref, sall_ref, lb_ref, gn_ref,
             d4_ref, dgn_ref, dlb_ref, dst_ref):
        @pl.when(pl.program_id(0) == 0)
        def _():
            dst_ref[...] = jnp.zeros_like(dst_ref)
            dgn_ref[...] = jnp.zeros_like(dgn_ref)
            dlb_ref[...] = jnp.zeros_like(dlb_ref)

        lbv = lb_ref[...]
        gnv = gn_ref[...]
        c = HG_CHUNK
        row = lax.broadcasted_iota(jnp.int32, (c, c), 0)
        colm = lax.broadcasted_iota(jnp.int32, (c, c), 1)
        mask = row >= colm
        triu = (row <= colm).astype(BF16)
        last = lax.broadcasted_iota(jnp.int32, (c, HEAD), 0) == c - 1

        def chunk(ci, carry):
            cc = nck - 1 - ci
            sl = pl.ds(pl.multiple_of(cc * c, c), c)
            hq, hf, hg = hq_ref[sl, :], hf_ref[sl, :], hg_ref[sl, :]
            q, k, v, logf, sig, f = _hg_gates(hq, hf, hi_ref[sl, :], lbv)
            g, gm, gl, qg, qt, kt, kd = _hg_decay(logf, q, k)
            egl = jnp.exp(gl)
            o = os_ref[sl, :]
            dy = doa_ref[sl, :]
            r = lax.rsqrt(_head_mean(o * o) + EPS)
            oh = o * r
            sg = _sig(hg)
            silu_g = hg * sg
            dgn_ref[...] += jnp.sum(dy * oh * silu_g, axis=0, keepdims=True)
            dhg = dy * oh * gnv * (sg * (1.0 + hg * (1.0 - sg)))
            doh = dy * gnv * silu_g
            do = r * (doh - oh * _head_mean(doh * oh))
            dqs, dks, dvs, dgs = [], [], [], []
            for h in range(HG_HEADS):
                hs = slice(h * HEAD, (h + 1) * HEAD)
                st = sall_ref[cc, h]
                dst = dst_ref[h]
                qt_h, kt_h, qg_h, kd_h = qt[:, hs], kt[:, hs], qg[:, hs], kd[:, hs]
                do_p, v_p, qt_p, kt_p, qg_p = terms(do[:, hs]), terms(v[:, hs]), terms(qt_h), terms(kt_h), terms(qg_h)
                st_p, dst_p = terms(st), terms(dst)
                a = jnp.where(mask, _dot_nt(qt_p[0], kt_p[0]), 0.0)
                da = terms(jnp.where(mask, _mm(_dot_nt, do_p, v_p), 0.0))
                dqt = _mm(_dot, da, kt_p)
                dkt = _mm(_dot_tn, da, qt_p)
                dqg = _mm(_dot, do_p, st_p)
                dv = _dot_tn(_bf(a), do_p[0]) + _dot_nt(_bf(kd_h), dst_p[0])
                dkd = _mm(_dot, v_p, dst_p)
                dgl = egl[:, hs] * jnp.sum(st * dst, axis=0, keepdims=True) + jnp.sum(dkd * kd_h, axis=0, keepdims=True)
                dst_ref[h] = egl[:, hs] * dst + _mm(_dot_tn, do_p, qg_p)
                g_h = g[:, hs]
                gm_h = gm[:, hs]
                gl_h = gl[:, hs]
                dqs.append(dqt * jnp.exp(g_h - gm_h) + dqg * jnp.exp(g_h))
                dks.append(dkt * jnp.exp(gm_h - g_h) + dkd * jnp.exp(gl_h - g_h))
                dvs.append(dv)
                dgs.append(dqt * qt_h - dkt * kt_h + dqg * qg_h - dkd * kd_h + jnp.where(last, dgl, 0.0))
            dq, dk, dv, dg = _cat(dqs), _cat(dks), _cat(dvs), _cat(dgs)
            dlogf = _tri_dot(triu, dg)
            df = dlogf / f - dk
            dlb_ref[...] += jnp.sum(df * (1.0 - sig), axis=0, keepdims=True)
            dhf = df * (1.0 - lbv) * sig * (1.0 - sig)
            sq = _sig(hq)
            dhq = dq * (sq * (1.0 + hq * (1.0 - sq)))
            d4_ref[sl, :] = _cat([dhq, dhf, dv, dhg]).astype(d4_ref.dtype)
            return carry

        lax.fori_loop(0, nck, chunk, 0)

    rev = lambda j: pl.BlockSpec((HG_ROWS, D_MODEL), lambda i, j=j: (nsteps - 1 - i, j))
    small = pl.BlockSpec((1, D_MODEL), lambda i: (0, 0))
    return _pcall(
        body, grid=(nsteps,),
        in_specs=[rev(0), rev(0), rev(0), rev(1), rev(2), rev(3),
                  pl.BlockSpec((nck, HG_HEADS, HEAD, HEAD), lambda i: (nsteps - 1 - i, 0, 0, 0)), small, small],
        out_specs=[pl.BlockSpec((HG_ROWS, 4 * D_MODEL), lambda i: (nsteps - 1 - i, 0)), small, small],
        out_shape=[jax.ShapeDtypeStruct((t, 4 * D_MODEL), BF16), jax.ShapeDtypeStruct((1, D_MODEL), F32),
                   jax.ShapeDtypeStruct((1, D_MODEL), F32)],
        scratch_shapes=[pltpu.VMEM((HG_HEADS, HEAD, HEAD), F32)],
        name=name, sem=("arbitrary",), args=(doa, oscan, proj, proj, proj, proj, sall, lb, gain), rider=rider)


def _band_masks():
    qi = lax.broadcasted_iota(jnp.int32, (ATT_BLK, ATT_BLK), 0)
    ki = lax.broadcasted_iota(jnp.int32, (ATT_BLK, ATT_BLK), 1)
    return ki >= qi, ki <= qi


def _attn_cfg(t, g):
    d = DILATIONS[g]
    length = t // d
    nb = length // ATT_BLK
    return d, length, nb, min(ATT_STEP_BLOCKS, nb)


def _attn_fwd(qg, kg, vg, g, name):
    t = qg.shape[0]
    d, length, nb, rb = _attn_cfg(t, g)
    scale = HEAD ** -0.5

    def body(q_ref, k_ref, v_ref, kp_ref, vp_ref, o_ref, l_ref):
        n = pl.program_id(1)
        prev_m, own_m = _band_masks()
        first_m = jnp.logical_and(prev_m, n > 0)
        for h in range(ATT_HEADS):
            hs = slice(h * HEAD, (h + 1) * HEAD)
            for j in range(rb):
                rows = slice(j * ATT_BLK, (j + 1) * ATT_BLK)
                before = slice((j - 1) * ATT_BLK, j * ATT_BLK)
                q = q_ref[rows, hs]
                k0, v0, m0 = (kp_ref[:, hs], vp_ref[:, hs], first_m) if j == 0 else (k_ref[before, hs], v_ref[before, hs], prev_m)
                s0 = jnp.where(m0, _dot_nt(q, k0) * scale, NEG)
                s1 = jnp.where(own_m, _dot_nt(q, k_ref[rows, hs]) * scale, NEG)
                m = jnp.maximum(jnp.max(s0, axis=1, keepdims=True), jnp.max(s1, axis=1, keepdims=True))
                p0, p1 = jnp.exp(s0 - m), jnp.exp(s1 - m)
                l = jnp.sum(p0, axis=1, keepdims=True) + jnp.sum(p1, axis=1, keepdims=True)
                o = _dot(_bf(p0), v0) + _dot(_bf(p1), v_ref[rows, hs])
                o_ref[rows, hs] = o / l
                l_ref[rows, hs] = jnp.broadcast_to(m + jnp.log(l), (ATT_BLK, HEAD))

    own = pl.BlockSpec((rb * ATT_BLK, ATT_GW), lambda r, n: (n, r))
    prev = pl.BlockSpec((ATT_BLK, ATT_GW), lambda r, n: (jnp.maximum(n * rb - 1, 0), r))
    view = lambda a: a.reshape(length, d * ATT_GW)
    o, lse = pl.pallas_call(
        body, grid=(d, nb // rb), in_specs=[own, own, own, prev, prev], out_specs=[own, own],
        out_shape=[jax.ShapeDtypeStruct((length, d * ATT_GW), F32)] * 2,
        name=name, compiler_params=_params(("parallel", "arbitrary")))(view(qg), view(kg), view(vg), view(kg), view(vg))
    return o.reshape(t, ATT_GW), lse.reshape(t, ATT_GW)


def _attn_bwd(qg, kg, vg, dog, lse, delta, g, name):
    t = qg.shape[0]
    d, length, nb, rb = _attn_cfg(t, g)
    nsteps = nb // rb
    scale = HEAD ** -0.5

    def body(q_ref, k_ref, v_ref, do_ref, l_ref, dl_ref, kp_ref, vp_ref, qn_ref, don_ref, ln_ref, dln_ref,
             dq_ref, dk_ref, dv_ref):
        n = pl.program_id(1)
        prev_m, own_m = _band_masks()
        first_m = jnp.logical_and(prev_m, n > 0)
        next_m = jnp.logical_and(prev_m, n < nsteps - 1)
        for h in range(ATT_HEADS):
            hs = slice(h * HEAD, (h + 1) * HEAD)
            dk, dv = [None] * rb, [None] * rb
            for j in range(rb + 1):
                rows = slice(j * ATT_BLK, (j + 1) * ATT_BLK)
                before = slice((j - 1) * ATT_BLK, j * ATT_BLK)
                if j < rb:
                    q, do, lse_q, dl_q = q_ref[rows, hs], do_ref[rows, hs], l_ref[rows, hs], dl_ref[rows, hs]
                else:
                    q, do, lse_q, dl_q = qn_ref[:, hs], don_ref[:, hs], ln_ref[:, hs], dln_ref[:, hs]
                if j == 0:
                    k0, v0, m0 = kp_ref[:, hs], vp_ref[:, hs], first_m
                else:
                    k0, v0, m0 = k_ref[before, hs], v_ref[before, hs], (prev_m if j < rb else next_m)
                p0 = jnp.where(m0, jnp.exp(_dot_nt(q, k0) * scale - lse_q), 0.0)
                ds0 = _bf(p0 * (_dot_nt(do, v0) - dl_q) * scale)
                if j >= 1:
                    dk[j - 1] = dk[j - 1] + _dot_tn(ds0, q)
                    dv[j - 1] = dv[j - 1] + _dot_tn(_bf(p0), do)
                if j < rb:
                    k1, v1 = k_ref[rows, hs], v_ref[rows, hs]
                    p1 = jnp.where(own_m, jnp.exp(_dot_nt(q, k1) * scale - lse_q), 0.0)
                    ds1 = _bf(p1 * (_dot_nt(do, v1) - dl_q) * scale)
                    dq_ref[rows, hs] = _dot(ds0, k0) + _dot(ds1, k1)
                    dk[j] = _dot_tn(ds1, q)
                    dv[j] = _dot_tn(_bf(p1), do)
            for j in range(rb):
                rows = slice(j * ATT_BLK, (j + 1) * ATT_BLK)
                dk_ref[rows, hs] = dk[j]
                dv_ref[rows, hs] = dv[j].astype(dv_ref.dtype)

    own = pl.BlockSpec((rb * ATT_BLK, ATT_GW), lambda r, n: (n, r))
    prev = pl.BlockSpec((ATT_BLK, ATT_GW), lambda r, n: (jnp.maximum(n * rb - 1, 0), r))
    nxt = pl.BlockSpec((ATT_BLK, ATT_GW), lambda r, n: (jnp.minimum((n + 1) * rb, nb - 1), r))
    view = lambda a: a.reshape(length, d * ATT_GW)
    dq, dk, dv = pl.pallas_call(
        body, grid=(d, nsteps), in_specs=[own] * 6 + [prev, prev] + [nxt] * 4, out_specs=[own, own, own],
        out_shape=[jax.ShapeDtypeStruct((length, d * ATT_GW), F32), jax.ShapeDtypeStruct((length, d * ATT_GW), F32),
                   jax.ShapeDtypeStruct((length, d * ATT_GW), BF16)],
        name=name, compiler_params=_params(("parallel", "arbitrary")))(
            view(qg), view(kg), view(vg), view(dog), view(lse), view(delta), view(kg), view(vg),
            view(qg), view(dog), view(lse), view(delta))
    return dq.reshape(t, ATT_GW), dk.reshape(t, ATT_GW), dv.reshape(t, ATT_GW)


def _rope_tables(t):
    pos = jnp.arange(t, dtype=F32)
    inv = ROPE_THETA ** (-jnp.arange(0, HEAD, 2, dtype=F32) / HEAD)
    ang = pos[:, None] * inv[None, :]
    ang = jnp.concatenate([ang, ang], axis=-1)
    return jnp.cos(ang), jnp.sin(ang)


def _lower_bounds(logits):
    lb = jnp.cumsum(jax.nn.softmax(logits.astype(F32), axis=0), axis=0)
    return lb - lb[0:1]


FFN_ROWS = 256
FF_SHARD = 2 * D_FF // N_CHIPS


def _ffn_in_act(x, g, w_in, name, rider=None):
    t = x.shape[0]

    def body(x_ref, g_ref, w_ref, h_ref, ab_ref, u_ref):
        xv = x_ref[...]
        h = _bf(xv * _rms_rows(xv) * g_ref[...])
        h_ref[...] = h
        for s in range(N_CHIPS // 2):
            cols = slice(s * FF_SHARD, (s + 1) * FF_SHARD)
            a = _dot(h, w_ref[s])
            b = _dot(h, w_ref[s + N_CHIPS // 2])
            ab_ref[:, cols] = a.astype(ab_ref.dtype)
            ab_ref[:, D_FF + s * FF_SHARD:D_FF + (s + 1) * FF_SHARD] = b.astype(ab_ref.dtype)
            u_ref[:, cols] = (a * _sig(a) * b).astype(u_ref.dtype)

    row = lambda w: pl.BlockSpec((FFN_ROWS, w), lambda i: (i, 0))
    return _pcall(
        body, grid=(t // FFN_ROWS,),
        in_specs=[row(D_MODEL), pl.BlockSpec((1, D_MODEL), lambda i: (0, 0)),
                  pl.BlockSpec(w_in.shape, lambda i: (0, 0, 0))],
        out_specs=[row(D_MODEL), row(2 * D_FF), row(D_FF)],
        out_shape=[jax.ShapeDtypeStruct((t, D_MODEL), BF16), jax.ShapeDtypeStruct((t, 2 * D_FF), BF16),
                   jax.ShapeDtypeStruct((t, D_FF), BF16)],
        name=name, sem=("parallel",), args=(x, g, w_in), rider=rider)


def _ffn_bwd_du_act(dx, w_out, ab, name, rider=None):
    t = dx.shape[0]

    def body(dx_ref, w_ref, ab_ref, o_ref):
        du = 0.5 * _dot_nt(_bf(dx_ref[...]), w_ref[0])
        a = ab_ref[:, :D_FF].astype(F32)
        b = ab_ref[:, D_FF:].astype(F32)
        s = _sig(a)
        o_ref[:, :D_FF] = (du * b * (s * (1.0 + a * (1.0 - s)))).astype(o_ref.dtype)
        o_ref[:, D_FF:] = (du * a * s).astype(o_ref.dtype)

    row = lambda w: pl.BlockSpec((FFN_ROWS, w), lambda i: (i, 0))
    return _pcall(
        body, grid=(t // FFN_ROWS,),
        in_specs=[row(D_MODEL), pl.BlockSpec(w_out.shape, lambda i: (0, 0, 0)), row(2 * D_FF)],
        out_specs=row(2 * D_FF), out_shape=jax.ShapeDtypeStruct((t, 2 * D_FF), BF16),
        name=name, sem=("parallel",), args=(dx, w_out, ab), rider=rider)


def _ffn_fwd(x, g, src, l, pre):
    tag = f"l{l}_{pre}"
    w_in = src.weight(l, pre + "_w_in")
    h, ab, u = _ffn_in_act(x, g, w_in, name=tag + "_in_act", rider=src.ride(tag + "_in_act"))
    w_out = src.weight(l, pre + "_w_out")
    y = _mm_nn(u, w_out, name=tag + "_out", tm=512, tn=D_MODEL, out_dtype=F32, res=x, alpha=0.5, rider=src.ride(tag + "_out"))
    return y, (x, h, ab, u, w_in, w_out)


def _ffn_bwd(dx, saved, g, src, l, pre):
    tag = f"l{l}_{pre}"
    x, h, ab, u, w_in, w_out = saved
    g_out = _mm_tn(u, dx, nb=1, name=tag + "_bwd_wout", tm=1024, tk=1408, tn=D_MODEL, alpha=0.5, rider=src.ride(tag + "_bwd_wout"))
    src.grads(l, {pre + "_w_out": g_out.reshape(N_CHIPS, D_FF // N_CHIPS, D_MODEL)})
    dab = _ffn_bwd_du_act(dx, w_out, ab, name=tag + "_bwd_du_act", rider=src.ride(tag + "_bwd_du_act"))
    g_in = _mm_tn(h, dab, nb=N_CHIPS, name=tag + "_bwd_win", tm=2048, tk=D_MODEL, tn=FF_SHARD, rider=src.ride(tag + "_bwd_win"))
    src.grads(l, {pre + "_w_in": g_in})
    dh = _mm_nt(dab, w_in, name=tag + "_bwd_dh", tm=1024, tp=D_MODEL, tn=FF_SHARD, out_dtype=F32, rider=src.ride(tag + "_bwd_dh"))
    return _norm_bwd(dh, x, g, dx, name=tag + "_bwd_norm")


def _mix_fwd(x, small, lb, cos, sin, src, l):
    tag = f"l{l}_mix"
    w = {}
    h = _norm_fwd(x, small["mix_norm"], name=tag + "_norm")
    w["w_in"] = src.weight(l, "w_in")
    proj = _mm_nn(h, w["w_in"], name=tag + "_in", tm=1024, tn=896, out_dtype=F32, rider=src.ride(tag + "_in"))
    oscan, oa, sall = _hgrn_fwd(proj, lb, small["hgrn_out_norm"], name=tag + "_hgrn", rider=src.ride(tag + "_hgrn"))
    qk = _qk_fwd(proj, cos, sin, small["attn_q_norm"], small["attn_k_norm"], name=tag + "_qk")
    outs, lses = [], []
    for g in range(ATT_GROUPS):
        o, lse = _attn_fwd(qk[g], qk[3 + g], qk[6 + g], g, name=f"{tag}_attn{g}")
        outs.append(o)
        lses.append(lse)
    ob = _merge_fwd(outs, lses, name=tag + "_merge")
    w.update({n: src.weight(l, n) for n in ("w_branch_a", "w_branch_b", "w_out")})
    ya = _mm_nn(oa, w["w_branch_a"], name=tag + "_wa", tm=512, tn=D_MODEL, out_dtype=F32)
    yb = _mm_nn(ob, w["w_branch_b"], name=tag + "_wb", tm=512, tn=256, out_dtype=F32)
    merged = _gate_fwd(proj, ya, yb, name=tag + "_gate")
    y = _mm_nn(merged, w["w_out"], name=tag + "_out", tm=512, tn=D_MODEL, out_dtype=F32, res=x)
    return y, (x, h, proj, oscan, oa, sall, qk, outs, lses, ob, ya, yb, merged, w)


def _mix_bwd(dx, saved, small, lb, cos, sin, src, l, lb_live):
    tag = f"l{l}_mix"
    x, h, proj, oscan, oa, sall, qk, outs, lses, ob, ya, yb, merged, w = saved
    dm = _mm_nt(dx, w["w_out"], name=tag + "_bwd_dm", tm=1024, tp=D_MODEL, tn=D_MODEL, out_dtype=F32)
    g_wout = _mm_tn(merged, dx, nb=1, name=tag + "_bwd_wout", tm=1024, tk=D_MODEL, tn=D_MODEL)
    dya, dyb, dgab = _gate_bwd(dm, proj, ya, yb, name=tag + "_bwd_gate")
    doa = _mm_nt(dya, w["w_branch_a"], name=tag + "_bwd_doa", tm=1024, tp=D_MODEL, tn=D_MODEL, out_dtype=F32)
    g_wa = _mm_tn(oa, dya, nb=1, name=tag + "_bwd_wa", tm=1024, tk=D_MODEL, tn=D_MODEL)
    dob = _mm_nt(dyb, w["w_branch_b"], name=tag + "_bwd_dob", tm=1024, tp=ATT_GW, tn=256, out_dtype=F32)
    g_wb = _mm_tn(ob, dyb, nb=N_CHIPS, name=tag + "_bwd_wb", tm=2048, tk=ATT_GW, tn=256)
    mb = _merge_bwd(dob, outs, lses, name=tag + "_bwd_merge")
    dqk, dvs = [None] * 6, []
    for g in range(ATT_GROUPS):
        dq, dk, dv = _attn_bwd(qk[g], qk[3 + g], qk[6 + g], mb[g], lses[g], mb[3 + g], g, name=f"{tag}_bwd_attn{g}")
        dqk[g], dqk[3 + g] = dq, dk
        dvs.append(dv)
    dqk_cols, dqn, dkn = _qk_bwd(dqk, proj, cos, sin, small["attn_q_norm"], small["attn_k_norm"], name=tag + "_bwd_qk")
    dh4, dgn, dlb = _hgrn_bwd(doa, oscan, proj, sall, lb, small["hgrn_out_norm"], name=tag + "_bwd_hgrn", precise=lb_live,
                              rider=src.ride(tag + "_bwd_hgrn"))
    dproj = _assemble_dproj(dh4, dqk_cols, dvs, dgab, name=tag + "_bwd_cat")
    src.grads(l, dict(w_branch_a=g_wa.reshape(N_CHIPS, D_MODEL // N_CHIPS, D_MODEL), w_branch_b=g_wb,
                      w_out=g_wout.reshape(N_CHIPS, D_MODEL // N_CHIPS, D_MODEL)))
    g_win = _mm_tn(h, dproj, nb=N_CHIPS, name=tag + "_bwd_win", tm=2048, tk=D_MODEL, tn=896, rider=src.ride(tag + "_bwd_win"))
    src.grads(l, dict(w_in=g_win))
    dh = _mm_nt(dproj, w["w_in"], name=tag + "_bwd_dh", tm=1024, tp=D_MODEL, tn=2688, out_dtype=F32, rider=src.ride(tag + "_bwd_dh"))
    dx, dg = _norm_bwd(dh, x, small["mix_norm"], dx, name=tag + "_bwd_norm")
    return dx, dict(mix_norm=dg, hgrn_out_norm=dgn, lb=dlb, attn_q_norm=dqn, attn_k_norm=dkn)


BIG = ("ffn1_w_in", "ffn1_w_out", "w_in", "w_branch_a", "w_branch_b", "w_out", "ffn2_w_in", "ffn2_w_out")
ROW_SHARDED = ("ffn1_w_out", "w_branch_a", "w_out", "ffn2_w_out")
SMALL = ("ffn1_norm", "mix_norm", "hgrn_lb_logits", "hgrn_out_norm", "attn_q_norm", "attn_k_norm", "ffn2_norm")
WEIGHTS = ("ffn1_norm", "ffn1_w_in", "ffn1_w_out", "mix_norm", "w_in", "hgrn_lb_logits", "hgrn_out_norm", "attn_q_norm",
           "attn_k_norm", "w_branch_a", "w_branch_b", "w_out", "ffn2_norm", "ffn2_w_in", "ffn2_w_out")
SMALL_ROWS = 8


def _matmul_ready(name, a):
    return a.reshape(1, a.shape[0] * a.shape[1], a.shape[2]) if name in ROW_SHARDED else a


def _layer_small(small, l):
    s = {n: small[n][l].reshape(1, D_MODEL) for n in ("ffn1_norm", "mix_norm", "hgrn_out_norm", "ffn2_norm")}
    s.update({n: small[n][l] for n in ("attn_q_norm", "attn_k_norm")})
    return s


def _local_step(x, target, small, src):
    t = x.shape[0]
    cos, sin = _rope_tables(t)
    lbs = _lower_bounds(small["hgrn_lb_logits"])
    saved = []
    for l in range(2):
        sm = _layer_small(small, l)
        lb = lbs[l].reshape(1, D_MODEL)
        x, s1 = _ffn_fwd(x, sm["ffn1_norm"], src, l, "ffn1")
        x, s2 = _mix_fwd(x, sm, lb, cos, sin, src, l)
        x, s3 = _ffn_fwd(x, sm["ffn2_norm"], src, l, "ffn2")
        saved.append((sm, lb, s1, s2, s3))
    dx, sq = _loss_fwd_bwd(x, target, name="loss")
    small_rows = [None, None]
    for l in (1, 0):
        sm, lb, s1, s2, s3 = saved[l]
        dx, dg2 = _ffn_bwd(dx, s3, sm["ffn2_norm"], src, l, "ffn2")
        dx, g = _mix_bwd(dx, s2, sm, lb, cos, sin, src, l, lb_live=l > 0)
        dx, dg1 = _ffn_bwd(dx, s1, sm["ffn1_norm"], src, l, "ffn1")
        pad = lambda a: jnp.pad(a[:ATT_GROUPS].reshape(1, ATT_GROUPS * HEAD), ((0, 0), (0, D_MODEL - ATT_GROUPS * HEAD)))
        small_rows[l] = jnp.concatenate(
            [dg1, g["mix_norm"], g["lb"], g["hgrn_out_norm"], pad(g["attn_q_norm"]), pad(g["attn_k_norm"]), dg2,
             jnp.zeros((SMALL_ROWS - 7, D_MODEL), F32)], axis=0)
    return jnp.sum(sq), dx, jnp.concatenate(small_rows, axis=0)


def _coords():
    return lax.axis_index("x"), lax.axis_index("y"), lax.axis_index("c")


def _other_chips(x, y):
    return [(1 - x, y), (x, 1 - y), (1 - x, 1 - y)]


def _half_rows(rows, which):
    return pl.ds(which * (rows // 2), rows // 2)


def _gather_rider(shards):
    n = len(shards)

    def copies(w, full, sems):
        send, recv, fsend, frecv = sems
        x, y, c = _coords()
        slot = 2 * x + y
        chips = _other_chips(x, y)

        def copy(i, j, blk, src, pair, to):
            return pltpu.make_async_remote_copy(src_ref=src, dst_ref=blk, send_sem=pair[0].at[i * 3 + j],
                                                recv_sem=pair[1].at[i * 3 + j], device_id=to, device_id_type=MESH)

        def block(i, chip_slot, core):
            return full[i].at[chip_slot, _half_rows(shards[i].shape[0], core)]

        pairs = [(i, j, chip) for i in range(n) for j, chip in enumerate(chips)]

        def first():
            return [copy(i, j, block(i, slot, c), w[i].at[_half_rows(shards[i].shape[0], c)], (send, recv), (*chip, c))
                    for i, j, chip in pairs]

        def landed(core, pair):
            return [copy(i, j, block(i, 2 * chip[0] + chip[1], core), block(i, 2 * chip[0] + chip[1], core), pair, (x, y, 1 - c))
                    for i, j, chip in pairs]

        return first, landed

    def begin(w, full, sems):
        for cp in copies(w, full, sems)[0]():
            cp.start()

    def end(w, full, sems):
        first, landed = copies(w, full, sems)
        forwards = landed(lax.axis_index("c"), sems[2:])
        for arrival, forward in zip(landed(lax.axis_index("c"), sems[:2]), forwards):
            arrival.wait_recv()
            forward.start()
        for cp in landed(1 - lax.axis_index("c"), sems[2:]):
            cp.wait_recv()
        for cp in first() + forwards:
            cp.wait_send()

    out_shape = [jax.ShapeDtypeStruct((N_CHIPS,) + s.shape, s.dtype) for s in shards]
    return _Rider(shards, out_shape, [pltpu.SemaphoreType.DMA((3 * n,))] * 4, begin, end)


N_RECV = 7


def _scatter_rider(parts):
    n = len(parts)

    def copies(p, out, sems):
        send, recv = sems
        x, y, c = _coords()
        slot = 2 * x + y
        chips = _other_chips(x, y)

        def arrivals():
            return [pltpu.make_async_remote_copy(
                src_ref=out[i].at[k], dst_ref=out[i].at[k], send_sem=send.at[0], recv_sem=recv.at[i * N_RECV + k],
                device_id=(x, y, c), device_id_type=MESH) for i in range(n) for k in range(N_RECV)]

        sends = []
        for i in range(n):
            rows = parts[i].shape[1]
            for j, chip in enumerate(chips):
                for core in (0, 1):
                    sends.append(pltpu.make_async_remote_copy(
                        src_ref=p[i].at[2 * chip[0] + chip[1], _half_rows(rows, core)], dst_ref=out[i].at[2 * j + c],
                        send_sem=send.at[i * N_RECV + 2 * j + core], recv_sem=recv.at[i * N_RECV + 2 * j + c],
                        device_id=(*chip, core), device_id_type=MESH))
            sends.append(pltpu.make_async_remote_copy(
                src_ref=p[i].at[slot, _half_rows(rows, 1 - c)], dst_ref=out[i].at[6], send_sem=send.at[i * N_RECV + 6],
                recv_sem=recv.at[i * N_RECV + 6], device_id=(x, y, 1 - c), device_id_type=MESH))
        return sends, arrivals

    def begin(p, out, sems):
        for cp in copies(p, out, sems)[0]:
            cp.start()

    def end(p, out, sems):
        sends, arrivals = copies(p, out, sems)
        for cp in arrivals():
            cp.wait_recv()
        for cp in sends:
            cp.wait_send()

    out_shape = [jax.ShapeDtypeStruct((N_RECV, a.shape[1] // 2, a.shape[2]), a.dtype) for a in parts]
    return _Rider(parts, out_shape, [pltpu.SemaphoreType.DMA((N_RECV * n,))] * 2, begin, end)


def _run_alone(rider, name):
    _pcall(lambda: None, grid=(), in_specs=[], out_specs=[], out_shape=[], name=name, sem=(), args=(), rider=rider)
    return rider.result


def _sum_partials(own, parts, name):
    r, wd = own.shape
    tm = next(t for t in (256, 128, 64, 32, 16) if r % t == 0)

    def body(own_ref, p_ref, o_ref):
        acc = own_ref[...].astype(F32)
        for k in range(N_RECV):
            acc = acc + p_ref[k].astype(F32)
        o_ref[...] = acc

    return pl.pallas_call(
        body, grid=(r // tm,),
        in_specs=[pl.BlockSpec((tm, wd), lambda i: (i, 0)), pl.BlockSpec((N_RECV, tm, wd), lambda i: (0, i, 0))],
        out_specs=pl.BlockSpec((tm, wd), lambda i: (i, 0)), out_shape=jax.ShapeDtypeStruct((r, wd), F32),
        name=name, compiler_params=_params(("parallel",)))(own, parts)


def _exchange_halves(reduced, name):
    n = len(reduced)

    def body(*refs):
        r, out = refs[:n], refs[n:2 * n]
        send, recv = refs[2 * n:]
        x, y, c = _coords()
        sib = [pltpu.make_async_remote_copy(src_ref=r[i], dst_ref=out[i], send_sem=send.at[i], recv_sem=recv.at[i],
                                            device_id=(x, y, 1 - c), device_id_type=MESH) for i in range(n)]
        for cp in sib:
            cp.start()
        for cp in sib:
            cp.wait_recv()
        for cp in sib:
            cp.wait_send()

    out_shape = [jax.ShapeDtypeStruct(a.shape, a.dtype) for a in reduced]
    return pl.pallas_call(body, in_specs=[ANY] * n, out_specs=[ANY] * n, out_shape=out_shape,
                          scratch_shapes=[pltpu.SemaphoreType.DMA((n,))] * 2, name=name)(*reduced)


def _reduce_finish(parts, recv, tag):
    x, y, c = _coords()
    slot = 2 * x + y
    halves = []
    for i, (p, r) in enumerate(zip(parts, recv)):
        half = p.shape[1] // 2
        own = lax.dynamic_slice(p, (slot, c * half, 0), (1, half, p.shape[2]))[0]
        halves.append(_sum_partials(own, r, name=f"{tag}_sum{i}"))
    theirs = _exchange_halves(halves, name=tag + "_exchange")
    return [jnp.where(c == 0, jnp.concatenate([h, t], axis=0), jnp.concatenate([t, h], axis=0)) for h, t in zip(halves, theirs)]


GATHER_RIDES = {
    "l0_ffn1_in_act": ((0, "w_in"),),
    "l0_ffn1_out": ((0, "w_branch_a"), (0, "w_branch_b"), (0, "w_out")),
    "l0_mix_in": ((0, "ffn2_w_in"), (0, "ffn2_w_out"), (1, "ffn1_w_in"), (1, "ffn1_w_out")),
    "l0_mix_hgrn": ((1, "w_in"), (1, "w_branch_a"), (1, "w_branch_b"), (1, "w_out")),
    "l0_ffn2_in_act": ((1, "ffn2_w_in"), (1, "ffn2_w_out")),
}
ALONE_FIRST = ((0, "ffn1_w_in"), (0, "ffn1_w_out"))
SCATTER_RIDES = {
    "l1_mix_bwd_hgrn": ((1, "ffn2_w_in"), (1, "ffn2_w_out")),
    "l0_ffn2_bwd_win": ((1, "ffn1_w_in"),),
    "l0_ffn2_bwd_dh": ((1, "ffn1_w_out"), (1, "w_branch_a"), (1, "w_branch_b"), (1, "w_out")),
    "l0_mix_bwd_hgrn": ((1, "w_in"), (0, "ffn2_w_out")),
    "l0_mix_bwd_win": ((0, "ffn2_w_in"),),
    "l0_mix_bwd_dh": ((0, "w_in"),),
    "l0_ffn1_bwd_wout": ((0, "w_branch_a"), (0, "w_branch_b"), (0, "w_out")),
    "l0_ffn1_bwd_du_act": ((0, "ffn1_w_out"),),
    "l0_ffn1_bwd_dh": ((0, "ffn1_w_in"),),
}


class _Exchange:
    def __init__(self, shards):
        self.shards = shards
        self.pending = []
        self.full = {}
        self.parts = {}
        self.recv = {}

    def _gather(self, keys):
        return _gather_rider([self.shards[n][l] for l, n in keys]), "gather", list(keys)

    def _scatter(self, keys):
        return _scatter_rider([self.parts[k] for k in keys]), "scatter", list(keys)

    def _unpack(self):
        slot = 2 * lax.axis_index("x") + lax.axis_index("y")
        waiting = []
        for rider, kind, keys in self.pending:
            if rider.result is None:
                waiting.append((rider, kind, keys))
            elif kind == "gather":
                for (l, n), got in zip(keys, rider.result):
                    self.full[(l, n)] = lax.dynamic_update_slice(got, self.shards[n][l][None], (slot, 0, 0))
            else:
                self.recv.update(zip(keys, rider.result))
        self.pending = waiting

    def ride(self, host):
        if host in GATHER_RIDES:
            self.pending.append(self._gather(GATHER_RIDES[host]))
        elif host in SCATTER_RIDES:
            self.pending.append(self._scatter(SCATTER_RIDES[host]))
        else:
            return None
        return self.pending[-1][0]

    def weight(self, l, name):
        self._unpack()
        if (l, name) not in self.full:
            assert (l, name) in ALONE_FIRST, (l, name)
            job = self._gather(ALONE_FIRST)
            _run_alone(job[0], name="gather_first")
            self.pending.append(job)
            self._unpack()
        return _matmul_ready(name, self.full[(l, name)])

    def grads(self, l, partials):
        self.parts.update({(l, n): a for n, a in partials.items()})

    def reduce(self):
        self._unpack()
        assert not self.pending and set(self.recv) == set(self.parts)
        out = {}
        for l in range(2):
            done = _reduce_finish([self.parts[(l, n)] for n in BIG], [self.recv[(l, n)] for n in BIG], f"reduce_l{l}")
            out[l] = dict(zip(BIG, done))
        return {n: jnp.stack([out[0][n], out[1][n]], axis=0) for n in BIG}


def _all_reduce_small(rows):
    r = rows.shape[0]

    def body(x_ref, o_ref, buf, send, recv):
        x, y, c = _coords()
        me = 4 * x + 2 * y + c
        buf[me] = x_ref[...]
        copies = []
        for k in range(1, 8):
            peer = (x ^ (k >> 2), y ^ ((k >> 1) & 1), c ^ (k & 1))
            cp = pltpu.make_async_remote_copy(src_ref=x_ref, dst_ref=buf.at[me], send_sem=send.at[k - 1], recv_sem=recv.at[me],
                                              device_id=peer, device_id_type=MESH)
            cp.start()
            copies.append(cp)
        for k in range(1, 8):
            src = 4 * (x ^ (k >> 2)) + 2 * (y ^ ((k >> 1) & 1)) + (c ^ (k & 1))
            pltpu.make_async_remote_copy(src_ref=x_ref, dst_ref=buf.at[src], send_sem=send.at[0], recv_sem=recv.at[src],
                                         device_id=(x, y, c), device_id_type=MESH).wait_recv()
        for cp in copies:
            cp.wait_send()
        acc = buf[0]
        for k in range(1, 8):
            acc = acc + buf[k]
        o_ref[...] = acc

    vm = pl.BlockSpec(memory_space=pltpu.VMEM)
    return pl.pallas_call(
        body, in_specs=[vm], out_specs=vm, out_shape=jax.ShapeDtypeStruct(rows.shape, F32),
        scratch_shapes=[pltpu.VMEM((8, r, D_MODEL), F32), pltpu.SemaphoreType.DMA((7,)), pltpu.SemaphoreType.DMA((8,))],
        name="all_reduce_small")(rows)


def _adamw_math(w, g, m, v):
    m = ADAM_B1 * m + (1.0 - ADAM_B1) * g
    v = ADAM_B2 * v + (1.0 - ADAM_B2) * (g * g)
    m_hat = m / (1.0 - ADAM_B1 ** ADAM_STEP)
    v_hat = v / (1.0 - ADAM_B2 ** ADAM_STEP)
    return -ADAM_LR * (m_hat / (jnp.sqrt(v_hat) + ADAM_EPS) + ADAM_WD * w), m, v


def _adamw(w, g, m, v, name):
    shape = w.shape
    cols = shape[-1]
    flat = lambda a: a.reshape(-1, cols)
    rows = flat(w).shape[0]
    tm = 128 if rows % 128 == 0 else rows
    ins = [('t', flat(a), cols, 0) for a in (w, g, m, v)]
    res = _ew(_adamw_math, ins, [('t', cols, F32)] * 3, rows=rows, tm=tm, name=name)
    return [a.reshape(shape) for a in res]


def _small_update(sums, logits, w, m, v):
    def body(s_ref, lg_ref, w_ref, m_ref, v_ref, g_ref, d_ref, nm_ref, nv_ref):
        s = s_ref[...]
        l0, l1 = lg_ref[0:1, :], lg_ref[1:2, :]
        mx = jnp.maximum(l0, l1)
        e0, e1 = jnp.exp(l0 - mx), jnp.exp(l1 - mx)
        sm0, sm1 = e0 / (e0 + e1), e1 / (e0 + e1)
        dl1 = s_ref[SMALL_ROWS + 2:SMALL_ROWS + 3, :] * sm0 * sm1
        row = lax.broadcasted_iota(jnp.int32, s.shape, 0)
        g = jnp.where(row == 2, -dl1, jnp.where(row == SMALL_ROWS + 2, dl1, s))
        d, nm, nv = _adamw_math(w_ref[...], g, m_ref[...], v_ref[...])
        g_ref[...] = g
        d_ref[...] = d
        nm_ref[...] = nm
        nv_ref[...] = nv

    vm = pl.BlockSpec(memory_space=pltpu.VMEM)
    return pl.pallas_call(body, in_specs=[vm] * 5, out_specs=[vm] * 4,
                          out_shape=[jax.ShapeDtypeStruct(sums.shape, F32)] * 4, name="small_update")(sums, logits, w, m, v)


def _pack_small(vals):
    rows = []
    for l in range(2):
        for n in ("ffn1_norm", "mix_norm", "hgrn_lb_logits", "hgrn_out_norm", "attn_q_norm", "attn_k_norm", "ffn2_norm"):
            a = vals[n][l].reshape(1, -1)
            rows.append(jnp.pad(a, ((0, 0), (0, D_MODEL - a.shape[1]))))
        rows.append(jnp.zeros((SMALL_ROWS - 7, D_MODEL), F32))
    return jnp.concatenate(rows, axis=0)


def _unpack_small(packed):
    out = {}
    for k, n in enumerate(("ffn1_norm", "mix_norm", "hgrn_lb_logits", "hgrn_out_norm", "attn_q_norm", "attn_k_norm", "ffn2_norm")):
        a = jnp.stack([packed[k], packed[SMALL_ROWS + k]], axis=0)
        out[n] = a[:, :ATT_GROUPS * HEAD].reshape(2, ATT_GROUPS, HEAD) if n.startswith("attn") else a
    return out


def kernel(x, ffn1_norm, ffn1_w_in, ffn1_w_out, mix_norm, w_in, hgrn_lb_logits, hgrn_out_norm, attn_q_norm, attn_k_norm, w_branch_a, w_branch_b, w_out, ffn2_norm, ffn2_w_in, ffn2_w_out, loss_target, m_ffn1_norm, m_ffn1_w_in, m_ffn1_w_out, m_mix_norm, m_w_in, m_hgrn_lb_logits, m_hgrn_out_norm, m_attn_q_norm, m_attn_k_norm, m_w_branch_a, m_w_branch_b, m_w_out, m_ffn2_norm, m_ffn2_w_in, m_ffn2_w_out, v_ffn1_norm, v_ffn1_w_in, v_ffn1_w_out, v_mix_norm, v_w_in, v_hgrn_lb_logits, v_hgrn_out_norm, v_attn_q_norm, v_attn_k_norm, v_w_branch_a, v_w_branch_b, v_w_out, v_ffn2_norm, v_ffn2_w_in, v_ffn2_w_out):
    a = locals()
    w = {n: a[n] for n in WEIGHTS}
    m = {n: a["m_" + n] for n in WEIGHTS}
    v = {n: a["v_" + n] for n in WEIGHTS}

    exchange = _Exchange({n: w[n].astype(BF16) for n in BIG})
    small = {n: w[n] for n in SMALL}
    sq, grad_x, small_rows = _local_step(x[0], loss_target[0], small, exchange)
    loss = lax.psum(sq, ("x", "y", "c")) * (0.5 / D_MODEL)
    grads = exchange.reduce()

    sums = _all_reduce_small(small_rows)
    g_s, d_s, m_s, v_s = _small_update(sums, w["hgrn_lb_logits"], _pack_small(small), _pack_small({n: m[n] for n in SMALL}),
                                       _pack_small({n: v[n] for n in SMALL}))
    grads.update(_unpack_small(g_s))
    delta, new_m, new_v = _unpack_small(d_s), _unpack_small(m_s), _unpack_small(v_s)
    for n in BIG:
        delta[n], new_m[n], new_v[n] = _adamw(w[n], grads[n], m[n], v[n], name="adamw_" + n)

    return (loss, grad_x[None], *[grads[n] for n in WEIGHTS], *[delta[n] for n in WEIGHTS],
            *[new_m[n] for n in WEIGHTS], *[new_v[n] for n in WEIGHTS])
```

```python
import functools

import jax
import jax.numpy as jnp
from jax import lax
from jax.experimental import pallas as pl
from jax.experimental.pallas import tpu as pltpu

F32 = jnp.float32
BF16 = jnp.bfloat16
MESH = pl.DeviceIdType.MESH

D_MODEL = 1024
D_FF = 2816
N_CHIPS = 4
HEAD = 128
HG_HEADS = 8
HG_CHUNK = 64
ATT_GROUPS = 3
ATT_HEADS = 4
ATT_GW = ATT_HEADS * HEAD
DILATIONS = (1, 4, 16)
ATT_BLK = 128
ATT_STEP_BLOCKS = 4
P_IN = 10752
CB_AQ, CB_AK, CB_AV, CB_GA, CB_GB = 8, 11, 14, 17, 19
EPS = 1e-6
ROPE_THETA = 10000.0
ADAM_LR, ADAM_B1, ADAM_B2, ADAM_EPS, ADAM_WD, ADAM_STEP = 0.001, 0.9, 0.999, 1e-08, 0.01, 10
VMEM_LIMIT_V7X = 56 * 1024 * 1024
NEG = -1e30


def _params(sem):
    return pltpu.CompilerParams(dimension_semantics=sem, vmem_limit_bytes=VMEM_LIMIT_V7X)


def _sig(x):
    return 1.0 / (1.0 + jnp.exp(-x))


def _dot(a, b):
    return jnp.dot(a, b, preferred_element_type=F32)


def _dot_nt(a, b):
    return lax.dot_general(a, b, (((1,), (1,)), ((), ())), preferred_element_type=F32)


def _dot_tn(a, b):
    return lax.dot_general(a, b, (((0,), (0,)), ((), ())), preferred_element_type=F32)


def _bf(x):
    return x.astype(BF16)


ANY = pl.BlockSpec(memory_space=pl.ANY)


class _Rider:
    def __init__(self, args, out_shape, sems, begin, end):
        self.args, self.out_shape, self.sems, self.begin, self.end = list(args), list(out_shape), list(sems), begin, end
        self.result = None


def _pcall(body, *, grid, in_specs, out_specs, out_shape, name, sem, args, scratch_shapes=(), rider=None):
    multi = isinstance(out_shape, (list, tuple))
    o_specs = list(out_specs) if multi else [out_specs]
    o_shape = list(out_shape) if multi else [out_shape]
    if rider is None:
        res = pl.pallas_call(body, grid=grid, in_specs=list(in_specs), out_specs=o_specs, out_shape=o_shape,
                             scratch_shapes=list(scratch_shapes), name=name, compiler_params=_params(sem))(*args)
        return list(res) if multi else res[0]
    counts = [len(in_specs), len(rider.args), len(o_specs), len(rider.out_shape), len(scratch_shapes)]

    def wrapped(*refs):
        groups, at = [], 0
        for c in counts:
            groups.append(refs[at:at + c])
            at += c
        h_in, r_in, h_out, r_out, h_scratch = groups
        r_sems = refs[at:]
        if grid:
            ids = [pl.program_id(a) for a in range(len(grid))]
            first = functools.reduce(jnp.logical_and, [i == 0 for i in ids])
            last = functools.reduce(jnp.logical_and, [i == g - 1 for i, g in zip(ids, grid)])
            pl.when(first)(lambda: rider.begin(r_in, r_out, r_sems))
            body(*h_in, *h_out, *h_scratch)
            pl.when(last)(lambda: rider.end(r_in, r_out, r_sems))
        else:
            rider.begin(r_in, r_out, r_sems)
            body(*h_in, *h_out, *h_scratch)
            rider.end(r_in, r_out, r_sems)

    res = pl.pallas_call(
        wrapped, grid=grid, in_specs=list(in_specs) + [ANY] * counts[1], out_specs=o_specs + [ANY] * counts[3],
        out_shape=o_shape + rider.out_shape, scratch_shapes=list(scratch_shapes) + rider.sems, name=name,
        compiler_params=_params(("arbitrary",) * len(grid)))(*args, *rider.args)
    rider.result = list(res[counts[2]:])
    return list(res[:counts[2]]) if multi else res[0]


def _mm_nn(a, b3, *, name, tm, tn, out_dtype, res=None, alpha=1.0, rider=None):
    m, k = a.shape
    nb, _, nw = b3.shape
    per = nw // tn
    assert nw % tn == 0 and m % tm == 0
    has_res = res is not None

    def body(*refs):
        if has_res:
            a_ref, b_ref, r_ref, o_ref = refs
        else:
            a_ref, b_ref, o_ref = refs
        acc = _dot(_bf(a_ref[...]), b_ref[...])
        if alpha != 1.0:
            acc = alpha * acc
        if has_res:
            acc = r_ref[...] + acc
        o_ref[...] = acc.astype(o_ref.dtype)

    in_specs = [pl.BlockSpec((tm, k), lambda i, j: (i, 0)),
                pl.BlockSpec((None, k, tn), lambda i, j: (j // per, 0, j % per))]
    args = [a, b3]
    if has_res:
        in_specs.append(pl.BlockSpec((tm, tn), lambda i, j: (i, j)))
        args.append(res)
    return _pcall(body, grid=(m // tm, nb * per), in_specs=in_specs, out_specs=pl.BlockSpec((tm, tn), lambda i, j: (i, j)),
                  out_shape=jax.ShapeDtypeStruct((m, nb * nw), out_dtype), name=name, sem=("parallel", "arbitrary"),
                  args=args, rider=rider)


def _mm_nt(d, b3, *, name, tm, tp, tn, out_dtype, alpha=1.0, rider=None):
    m, n = d.shape
    nb, p, nw = b3.shape
    per = nw // tn
    nk = n // tn
    assert nb * nw == n and nw % tn == 0 and p % tp == 0 and m % tm == 0

    def body(d_ref, b_ref, o_ref, acc_ref):
        kk = pl.program_id(2)

        @pl.when(kk == 0)
        def _():
            acc_ref[...] = jnp.zeros_like(acc_ref)

        acc_ref[...] += _dot_nt(_bf(d_ref[...]), b_ref[...])

        @pl.when(kk == nk - 1)
        def _():
            o_ref[...] = (alpha * acc_ref[...]).astype(o_ref.dtype)

    return _pcall(
        body, grid=(m // tm, p // tp, nk),
        in_specs=[pl.BlockSpec((tm, tn), lambda i, j, kk: (i, kk)),
                  pl.BlockSpec((None, tp, tn), lambda i, j, kk: (kk // per, j, kk % per))],
        out_specs=pl.BlockSpec((tm, tp), lambda i, j, kk: (i, j)),
        out_shape=jax.ShapeDtypeStruct((m, p), out_dtype),
        scratch_shapes=[pltpu.VMEM((tm, tp), F32)],
        name=name, sem=("parallel", "parallel", "arbitrary"), args=(d, b3), rider=rider)


def _mm_tn(a, d, *, nb, name, tm, tk, tn, alpha=1.0, rider=None):
    m, k = a.shape
    _, n = d.shape
    nw = n // nb
    per = nw // tn
    nm = m // tm
    assert nw % tn == 0 and k % tk == 0 and m % tm == 0

    def body(a_ref, d_ref, o_ref, acc_ref):
        mm = pl.program_id(2)

        @pl.when(mm == 0)
        def _():
            acc_ref[...] = jnp.zeros_like(acc_ref)

        acc_ref[...] += _dot_tn(_bf(a_ref[...]), _bf(d_ref[...]))

        @pl.when(mm == nm - 1)
        def _():
            o_ref[...] = (alpha * acc_ref[...]).astype(o_ref.dtype)

    return _pcall(
        body, grid=(k // tk, nb * per, nm),
        in_specs=[pl.BlockSpec((tm, tk), lambda i, j, mm: (mm, i)),
                  pl.BlockSpec((tm, tn), lambda i, j, mm: (mm, j))],
        out_specs=pl.BlockSpec((None, tk, tn), lambda i, j, mm: (j // per, i, j % per)),
        out_shape=jax.ShapeDtypeStruct((nb, k, nw), BF16),
        scratch_shapes=[pltpu.VMEM((tk, tn), F32)],
        name=name, sem=("parallel", "parallel", "arbitrary"), args=(a, d), rider=rider)


def _ew(fn, ins, outs, *, rows, tm, name):
    in_specs, args = [], []
    for s in ins:
        if s[0] == 't':
            _, arr, w, cb = s
            in_specs.append(pl.BlockSpec((tm, w), lambda i, cb=cb: (i, cb)))
        else:
            arr = s[1]
            in_specs.append(pl.BlockSpec(arr.shape, lambda i, nd=arr.ndim: (0,) * nd))
        args.append(arr)
    out_specs, out_shape = [], []
    for s in outs:
        if s[0] == 't':
            _, w, dt = s
            out_specs.append(pl.BlockSpec((tm, w), lambda i: (i, 0)))
            out_shape.append(jax.ShapeDtypeStruct((rows, w), dt))
        else:
            out_specs.append(pl.BlockSpec(s[1], lambda i: (0, 0)))
            out_shape.append(jax.ShapeDtypeStruct(s[1], F32))
    n_in = len(ins)

    def body(*refs):
        res = fn(*[r[...] for r in refs[:n_in]])
        if not isinstance(res, (tuple, list)):
            res = (res,)
        for r, s, v in zip(refs[n_in:], outs, res):
            if s[0] == 't':
                r[...] = v.astype(r.dtype)
            else:
                @pl.when(pl.program_id(0) == 0)
                def _(r=r):
                    r[...] = jnp.zeros_like(r)

                r[...] += v

    res = pl.pallas_call(
        body, grid=(rows // tm,), in_specs=in_specs, out_specs=out_specs, out_shape=out_shape,
        name=name, compiler_params=_params(("arbitrary",)))(*args)
    return res


def _heads(x):
    return [x[:, h * HEAD:(h + 1) * HEAD] for h in range(x.shape[1] // HEAD)]


def _cat(xs):
    return jnp.concatenate(xs, axis=1)


def _head_mean(x):
    return _cat([jnp.broadcast_to(jnp.mean(h, axis=1, keepdims=True), h.shape) for h in _heads(x)])


def _rms_rows(x):
    return lax.rsqrt(jnp.mean(x * x, axis=1, keepdims=True) + EPS)


def _norm_fwd(x, g, name):
    return _ew(lambda xv, gv: xv * _rms_rows(xv) * gv,
               [('t', x, D_MODEL, 0), ('f', g)], [('t', D_MODEL, BF16)], rows=x.shape[0], tm=512, name=name)[0]


def _norm_bwd(dh, x, g, dx, name):
    def fn(dhv, xv, gv, dxv):
        r = _rms_rows(xv)
        xh = xv * r
        dxh = dhv * gv
        out = dxv + r * (dxh - xh * jnp.mean(dxh * xh, axis=1, keepdims=True))
        return out, jnp.sum(dhv * xh, axis=0, keepdims=True)

    return _ew(fn, [('t', dh, D_MODEL, 0), ('t', x, D_MODEL, 0), ('f', g), ('t', dx, D_MODEL, 0)],
               [('t', D_MODEL, F32), ('acc', (1, D_MODEL))], rows=x.shape[0], tm=512, name=name)


def _loss_fwd_bwd(y, target, name):
    def fn(yv, tv):
        e = yv - tv
        return e * (1.0 / D_MODEL), jnp.sum(e * e, axis=0, keepdims=True)

    return _ew(fn, [('t', y, D_MODEL, 0), ('t', target, D_MODEL, 0)], [('t', D_MODEL, F32), ('acc', (1, D_MODEL))],
               rows=y.shape[0], tm=512, name=name)


def _rot(x):
    sgn = jnp.where(lax.broadcasted_iota(jnp.int32, x.shape, 1) < HEAD // 2, -1.0, 1.0)
    return pltpu.roll(x, HEAD // 2, 1) * sgn


def _gain_rows(qn, kn):
    return [a[g:g + 1] for a in (qn, kn) for g in range(ATT_GROUPS)]


def _qk_fwd(proj, cos, sin, qn, kn, name):
    def fn(*v):
        xs, cosv, sinv, gains, vs = v[:6], v[6], v[7], v[8:14], v[14:17]
        outs = []
        for j, x in enumerate(xs):
            gain = gains[j]
            ys = []
            for xh in _heads(x):
                xn = xh * _rms_rows(xh) * gain
                ys.append(xn * cosv + _rot(xn) * sinv)
            outs.append(_cat(ys))
        return outs + list(vs)

    ins = ([('t', proj, 512, CB_AQ + j) for j in range(6)] + [('t', cos, HEAD, 0), ('t', sin, HEAD, 0)]
           + [('f', a) for a in _gain_rows(qn, kn)] + [('t', proj, 512, CB_AV + g) for g in range(ATT_GROUPS)])
    return _ew(fn, ins, [('t', ATT_GW, BF16)] * 9, rows=proj.shape[0], tm=512, name=name)


def _qk_bwd(dqk, proj, cos, sin, qn, kn, name):
    def fn(*v):
        ds, xs, cosv, sinv, gains = v[:6], v[6:12], v[12], v[13], v[14:20]
        rows8 = lax.broadcasted_iota(jnp.int32, (8, HEAD), 0)
        outs, dgs = [], [jnp.zeros((8, HEAD), F32)] * 2
        for j in range(6):
            gain = gains[j]
            dx, dg = [], jnp.zeros((1, HEAD), F32)
            for dyh, xh in zip(_heads(ds[j]), _heads(xs[j])):
                r = _rms_rows(xh)
                xhat = xh * r
                dxn = dyh * cosv - _rot(dyh * sinv)
                dg = dg + jnp.sum(dxn * xhat, axis=0, keepdims=True)
                dxh = dxn * gain
                dx.append(r * (dxh - xhat * jnp.mean(dxh * xhat, axis=1, keepdims=True)))
            outs.append(_cat(dx))
            dgs[j // 3] = dgs[j // 3] + jnp.where(rows8 == j % 3, dg, 0.0)
        return _cat(outs), dgs[0], dgs[1]

    ins = ([('t', a, ATT_GW, 0) for a in dqk] + [('t', proj, 512, CB_AQ + j) for j in range(6)]
           + [('t', cos, HEAD, 0), ('t', sin, HEAD, 0)] + [('f', a) for a in _gain_rows(qn, kn)])
    return _ew(fn, ins, [('t', 6 * ATT_GW, BF16), ('acc', (8, HEAD)), ('acc', (8, HEAD))],
               rows=proj.shape[0], tm=256, name=name)


def _merge_fwd(outs, lses, name):
    def fn(o0, o1, o2, l0, l1, l2):
        m = jnp.maximum(jnp.maximum(l0, l1), l2)
        e0, e1, e2 = jnp.exp(l0 - m), jnp.exp(l1 - m), jnp.exp(l2 - m)
        return (e0 * o0 + e1 * o1 + e2 * o2) / (e0 + e1 + e2)

    ins = [('t', a, ATT_GW, 0) for a in list(outs) + list(lses)]
    return _ew(fn, ins, [('t', ATT_GW, BF16)], rows=outs[0].shape[0], tm=512, name=name)[0]


def _merge_bwd(dob, outs, lses, name):
    def fn(dov, o0, o1, o2, l0, l1, l2):
        m = jnp.maximum(jnp.maximum(l0, l1), l2)
        e0, e1, e2 = jnp.exp(l0 - m), jnp.exp(l1 - m), jnp.exp(l2 - m)
        inv = 1.0 / (e0 + e1 + e2)
        a0, a1, a2 = e0 * inv, e1 * inv, e2 * inv
        ob = a0 * o0 + a1 * o1 + a2 * o2
        s = _head_mean(dov * ob) * float(HEAD)
        return a0 * dov, a1 * dov, a2 * dov, a0 * s, a1 * s, a2 * s

    ins = [('t', dob, ATT_GW, 0)] + [('t', a, ATT_GW, 0) for a in list(outs) + list(lses)]
    return _ew(fn, ins, [('t', ATT_GW, BF16)] * 3 + [('t', ATT_GW, F32)] * 3, rows=dob.shape[0], tm=512, name=name)


def _assemble_dproj(dh4, dqk, dvs, dgab, name):
    fn = lambda *v: _cat(list(v))
    ins = [('t', dh4, 4 * D_MODEL, 0), ('t', dqk, 6 * ATT_GW, 0)] + [('t', a, ATT_GW, 0) for a in dvs] + [('t', dgab, 2 * D_MODEL, 0)]
    return _ew(fn, ins, [('t', P_IN, BF16)], rows=dh4.shape[0], tm=256, name=name)[0]


HG_ROWS = 256


def _hg_gates(hq, hf, hi, lbv):
    sig = _sig(hf)
    f = lbv + (1.0 - lbv) * sig
    return hq * _sig(hq), 1.0 - f, hi, jnp.log(f), sig, f


def _split3(x):
    hi = _bf(x)
    r1 = x - hi.astype(F32)
    mid = _bf(r1)
    return hi, mid, _bf(r1 - mid.astype(F32))


def _tri_dot(tri, x):
    hi, mid, lo = _split3(x)
    return _dot(tri, hi) + _dot(tri, mid) + _dot(tri, lo)


def _row(x, i):
    rows = lax.broadcasted_iota(jnp.int32, x.shape, 0)
    return jnp.sum(jnp.where(rows == i, x, 0.0), axis=0, keepdims=True)


def _hg_decay(logf, q, k):
    c = HG_CHUNK
    row = lax.broadcasted_iota(jnp.int32, (c, c), 0)
    col = lax.broadcasted_iota(jnp.int32, (c, c), 1)
    g = _tri_dot((row >= col).astype(BF16), logf)
    gm = _row(g, c // 2 - 1)
    gl = _row(g, c - 1)
    return g, gm, gl, q * jnp.exp(g), q * jnp.exp(g - gm), k * jnp.exp(gm - g), k * jnp.exp(gl - g)


def _hg_out_fwd(o, hg, gain):
    r = lax.rsqrt(_head_mean(o * o) + EPS)
    return o * r * gain * (hg * _sig(hg))


def _hgrn_fwd(proj, lb, gain, name, rider=None):
    t = proj.shape[0]
    nck = HG_ROWS // HG_CHUNK

    def body(hq_ref, hf_ref, hi_ref, hg_ref, lb_ref, gn_ref, o_ref, oa_ref, sall_ref, st_ref):
        @pl.when(pl.program_id(0) == 0)
        def _():
            st_ref[...] = jnp.zeros_like(st_ref)

        lbv = lb_ref[...]
        gnv = gn_ref[...]
        c = HG_CHUNK
        mask = lax.broadcasted_iota(jnp.int32, (c, c), 0) >= lax.broadcasted_iota(jnp.int32, (c, c), 1)

        def chunk(cc, carry):
            sl = pl.ds(pl.multiple_of(cc * c, c), c)
            q, k, v, logf, _, _ = _hg_gates(hq_ref[sl, :], hf_ref[sl, :], hi_ref[sl, :], lbv)
            _, _, gl, qg, qt, kt, kd = _hg_decay(logf, q, k)
            egl = jnp.exp(gl)
            os = []
            for h in range(HG_HEADS):
                hs = slice(h * HEAD, (h + 1) * HEAD)
                st = st_ref[h]
                sall_ref[cc, h] = st
                a = jnp.where(mask, _dot_nt(_bf(qt[:, hs]), _bf(kt[:, hs])), 0.0)
                os.append(_dot(_bf(a), _bf(v[:, hs])) + _dot_nt(_bf(qg[:, hs]), _bf(st)))
                st_ref[h] = egl[:, hs] * st + _dot_tn(_bf(v[:, hs]), _bf(kd[:, hs]))
            o = _cat(os)
            o_ref[sl, :] = o
            oa_ref[sl, :] = _hg_out_fwd(o, hg_ref[sl, :], gnv).astype(oa_ref.dtype)
            return carry

        lax.fori_loop(0, nck, chunk, 0)

    col = lambda j: pl.BlockSpec((HG_ROWS, D_MODEL), lambda i, j=j: (i, j))
    small = pl.BlockSpec((1, D_MODEL), lambda i: (0, 0))
    return _pcall(
        body, grid=(t // HG_ROWS,),
        in_specs=[col(0), col(1), col(2), col(3), small, small],
        out_specs=[col(0), col(0), pl.BlockSpec((nck, HG_HEADS, HEAD, HEAD), lambda i: (i, 0, 0, 0))],
        out_shape=[jax.ShapeDtypeStruct((t, D_MODEL), F32), jax.ShapeDtypeStruct((t, D_MODEL), BF16),
                   jax.ShapeDtypeStruct((t // HG_CHUNK, HG_HEADS, HEAD, HEAD), F32)],
        scratch_shapes=[pltpu.VMEM((HG_HEADS, HEAD, HEAD), F32)],
        name=name, sem=("arbitrary",), args=(proj, proj, proj, proj, lb, gain), rider=rider)


def _terms(x, precise):
    hi = _bf(x)
    return (hi, _bf(x - hi.astype(F32))) if precise else (hi,)


def _mm(dot, a, b):
    out = dot(a[0], b[0])
    if len(a) > 1:
        out = out + dot(a[1], b[0])
    if len(b) > 1:
        out = out + dot(a[0], b[1])
    return out


def _hgrn_bwd(doa, oscan, proj, sall, lb, gain, name, precise, rider=None):
    t = proj.shape[0]
    nck = HG_ROWS // HG_CHUNK
    nsteps = t // HG_ROWS
    terms = functools.partial(_terms, precise=precise)

    def body(doa_ref, os_ref, hq_ref, hf_ref, hi_ref, hg_ref, sall_ref, lb_ref, gn_ref,
             d4_ref, dgn_ref, dlb_ref, dst_ref):
        @pl.when(pl.program_id(0) == 0)
        def _():
            dst_ref[...] = jnp.zeros_like(dst_ref)
            dgn_ref[...] = jnp.zeros_like(dgn_ref)
            dlb_ref[...] = jnp.zeros_like(dlb_ref)

        lbv = lb_ref[...]
        gnv = gn_ref[...]
        c = HG_CHUNK
        row = lax.broadcasted_iota(jnp.int32, (c, c), 0)
        colm = lax.broadcasted_iota(jnp.int32, (c, c), 1)
        mask = row >= colm
        triu = (row <= colm).astype(BF16)
        last = lax.broadcasted_iota(jnp.int32, (c, HEAD), 0) == c - 1

        def chunk(ci, carry):
            cc = nck - 1 - ci
            sl = pl.ds(pl.multiple_of(cc * c, c), c)
            hq, hf, hg = hq_ref[sl, :], hf_ref[sl, :], hg_ref[sl, :]
            q, k, v, logf, sig, f = _hg_gates(hq, hf, hi_ref[sl, :], lbv)
            g, gm, gl, qg, qt, kt, kd = _hg_decay(logf, q, k)
            egl = jnp.exp(gl)
            o = os_ref[sl, :]
            dy = doa_ref[sl, :]
            r = lax.rsqrt(_head_mean(o * o) + EPS)
            oh = o * r
            sg = _sig(hg)
            silu_g = hg * sg
            dgn_ref[...] += jnp.sum(dy * oh * silu_g, axis=0, keepdims=True)
            dhg = dy * oh * gnv * (sg * (1.0 + hg * (1.0 - sg)))
            doh = dy * gnv * silu_g
            do = r * (doh - oh * _head_mean(doh * oh))
            dqs, dks, dvs, dgs = [], [], [], []
            for h in range(HG_HEADS):
                hs = slice(h * HEAD, (h + 1) * HEAD)
                st = sall_ref[cc, h]
                dst = dst_ref[h]
                qt_h, kt_h, qg_h, kd_h = qt[:, hs], kt[:, hs], qg[:, hs], kd[:, hs]
                do_p, v_p, qt_p, kt_p, qg_p = terms(do[:, hs]), terms(v[:, hs]), terms(qt_h), terms(kt_h), terms(qg_h)
                st_p, dst_p = terms(st), terms(dst)
                a = jnp.where(mask, _dot_nt(qt_p[0], kt_p[0]), 0.0)
                da = terms(jnp.where(mask, _mm(_dot_nt, do_p, v_p), 0.0))
                dqt = _mm(_dot, da, kt_p)
                dkt = _mm(_dot_tn, da, qt_p)
                dqg = _mm(_dot, do_p, st_p)
                dv = _dot_tn(_bf(a), do_p[0]) + _dot_nt(_bf(kd_h), dst_p[0])
                dkd = _mm(_dot, v_p, dst_p)
                dgl = egl[:, hs] * jnp.sum(st * dst, axis=0, keepdims=True) + jnp.sum(dkd * kd_h, axis=0, keepdims=True)
                dst_ref[h] = egl[:, hs] * dst + _mm(_dot_tn, do_p, qg_p)
                g_h = g[:, hs]
                gm_h = gm[:, hs]
                gl_h = gl[:, hs]
                dqs.append(dqt * jnp.exp(g_h - gm_h) + dqg * jnp.exp(g_h))
                dks.append(dkt * jnp.exp(gm_h - g_h) + dkd * jnp.exp(gl_h - g_h))
                dvs.append(dv)
                dgs.append(dqt * qt_h - dkt * kt_h + dqg * qg_h - dkd * kd_h + jnp.where(last, dgl, 0.0))
            dq, dk, dv, dg = _cat(dqs), _cat(dks), _cat(dvs), _cat(dgs)
            dlogf = _tri_dot(triu, dg)
            df = dlogf / f - dk
            dlb_ref[...] += jnp.sum(df * (1.0 - sig), axis=0, keepdims=True)
            dhf = df * (1.0 - lbv) * sig * (1.0 - sig)
            sq = _sig(hq)
            dhq = dq * (sq * (1.0 + hq * (1.0 - sq)))
            d4_ref[sl, :] = _cat([dhq, dhf, dv, dhg]).astype(d4_ref.dtype)
            return carry

        lax.fori_loop(0, nck, chunk, 0)

    rev = lambda j: pl.BlockSpec((HG_ROWS, D_MODEL), lambda i, j=j: (nsteps - 1 - i, j))
    small = pl.BlockSpec((1, D_MODEL), lambda i: (0, 0))
    return _pcall(
        body, grid=(nsteps,),
        in_specs=[rev(0), rev(0), rev(0), rev(1), rev(2), rev(3),
                  pl.BlockSpec((nck, HG_HEADS, HEAD, HEAD), lambda i: (nsteps - 1 - i, 0, 0, 0)), small, small],
        out_specs=[pl.BlockSpec((HG_ROWS, 4 * D_MODEL), lambda i: (nsteps - 1 - i, 0)), small, small],
        out_shape=[jax.ShapeDtypeStruct((t, 4 * D_MODEL), BF16), jax.ShapeDtypeStruct((1, D_MODEL), F32),
                   jax.ShapeDtypeStruct((1, D_MODEL), F32)],
        scratch_shapes=[pltpu.VMEM((HG_HEADS, HEAD, HEAD), F32)],
        name=name, sem=("arbitrary",), args=(doa, oscan, proj, proj, proj, proj, sall, lb, gain), rider=rider)


def _band_masks():
    qi = lax.broadcasted_iota(jnp.int32, (ATT_BLK, ATT_BLK), 0)
    ki = lax.broadcasted_iota(jnp.int32, (ATT_BLK, ATT_BLK), 1)
    return ki >= qi, ki <= qi


def _attn_cfg(t, g):
    d = DILATIONS[g]
    length = t // d
    nb = length // ATT_BLK
    return d, length, nb, min(ATT_STEP_BLOCKS, nb)


def _attn_fwd(qg, kg, vg, g, name):
    t = qg.shape[0]
    d, length, nb, rb = _attn_cfg(t, g)
    scale = HEAD ** -0.5

    def body(q_ref, k_ref, v_ref, kp_ref, vp_ref, o_ref, l_ref):
        n = pl.program_id(1)
        prev_m, own_m = _band_masks()
        first_m = jnp.logical_and(prev_m, n > 0)
        for h in range(ATT_HEADS):
            hs = slice(h * HEAD, (h + 1) * HEAD)
            for j in range(rb):
                rows = slice(j * ATT_BLK, (j + 1) * ATT_BLK)
                before = slice((j - 1) * ATT_BLK, j * ATT_BLK)
                q = q_ref[rows, hs]
                k0, v0, m0 = (kp_ref[:, hs], vp_ref[:, hs], first_m) if j == 0 else (k_ref[before, hs], v_ref[before, hs], prev_m)
                s0 = jnp.where(m0, _dot_nt(q, k0) * scale, NEG)
                s1 = jnp.where(own_m, _dot_nt(q, k_ref[rows, hs]) * scale, NEG)
                m = jnp.maximum(jnp.max(s0, axis=1, keepdims=True), jnp.max(s1, axis=1, keepdims=True))
                p0, p1 = jnp.exp(s0 - m), jnp.exp(s1 - m)
                l = jnp.sum(p0, axis=1, keepdims=True) + jnp.sum(p1, axis=1, keepdims=True)
                o = _dot(_bf(p0), v0) + _dot(_bf(p1), v_ref[rows, hs])
                o_ref[rows, hs] = o / l
                l_ref[rows, hs] = jnp.broadcast_to(m + jnp.log(l), (ATT_BLK, HEAD))

    own = pl.BlockSpec((rb * ATT_BLK, ATT_GW), lambda r, n: (n, r))
    prev = pl.BlockSpec((ATT_BLK, ATT_GW), lambda r, n: (jnp.maximum(n * rb - 1, 0), r))
    view = lambda a: a.reshape(length, d * ATT_GW)
    o, lse = pl.pallas_call(
        body, grid=(d, nb // rb), in_specs=[own, own, own, prev, prev], out_specs=[own, own],
        out_shape=[jax.ShapeDtypeStruct((length, d * ATT_GW), F32)] * 2,
        name=name, compiler_params=_params(("parallel", "arbitrary")))(view(qg), view(kg), view(vg), view(kg), view(vg))
    return o.reshape(t, ATT_GW), lse.reshape(t, ATT_GW)


def _attn_bwd(qg, kg, vg, dog, lse, delta, g, name):
    t = qg.shape[0]
    d, length, nb, rb = _attn_cfg(t, g)
    nsteps = nb // rb
    scale = HEAD ** -0.5

    def body(q_ref, k_ref, v_ref, do_ref, l_ref, dl_ref, kp_ref, vp_ref, qn_ref, don_ref, ln_ref, dln_ref,
             dq_ref, dk_ref, dv_ref):
        n = pl.program_id(1)
        prev_m, own_m = _band_masks()
        first_m = jnp.logical_and(prev_m, n > 0)
        next_m = jnp.logical_and(prev_m, n < nsteps - 1)
        for h in range(ATT_HEADS):
            hs = slice(h * HEAD, (h + 1) * HEAD)
            dk, dv = [None] * rb, [None] * rb
            for j in range(rb + 1):
                rows = slice(j * ATT_BLK, (j + 1) * ATT_BLK)
                before = slice((j - 1) * ATT_BLK, j * ATT_BLK)
                if j < rb:
                    q, do, lse_q, dl_q = q_ref[rows, hs], do_ref[rows, hs], l_ref[rows, hs], dl_ref[rows, hs]
                else:
                    q, do, lse_q, dl_q = qn_ref[:, hs], don_ref[:, hs], ln_ref[:, hs], dln_ref[:, hs]
                if j == 0:
                    k0, v0, m0 = kp_ref[:, hs], vp_ref[:, hs], first_m
                else:
                    k0, v0, m0 = k_ref[before, hs], v_ref[before, hs], (prev_m if j < rb else next_m)
                p0 = jnp.where(m0, jnp.exp(_dot_nt(q, k0) * scale - lse_q), 0.0)
                ds0 = _bf(p0 * (_dot_nt(do, v0) - dl_q) * scale)
                if j >= 1:
                    dk[j - 1] = dk[j - 1] + _dot_tn(ds0, q)
                    dv[j - 1] = dv[j - 1] + _dot_tn(_bf(p0), do)
                if j < rb:
                    k1, v1 = k_ref[rows, hs], v_ref[rows, hs]
                    p1 = jnp.where(own_m, jnp.exp(_dot_nt(q, k1) * scale - lse_q), 0.0)
                    ds1 = _bf(p1 * (_dot_nt(do, v1) - dl_q) * scale)
                    dq_ref[rows, hs] = _dot(ds0, k0) + _dot(ds1, k1)
                    dk[j] = _dot_tn(ds1, q)
                    dv[j] = _dot_tn(_bf(p1), do)
            for j in range(rb):
                rows = slice(j * ATT_BLK, (j + 1) * ATT_BLK)
                dk_ref[rows, hs] = dk[j]
                dv_ref[rows, hs] = dv[j].astype(dv_ref.dtype)

    own = pl.BlockSpec((rb * ATT_BLK, ATT_GW), lambda r, n: (n, r))
    prev = pl.BlockSpec((ATT_BLK, ATT_GW), lambda r, n: (jnp.maximum(n * rb - 1, 0), r))
    nxt = pl.BlockSpec((ATT_BLK, ATT_GW), lambda r, n: (jnp.minimum((n + 1) * rb, nb - 1), r))
    view = lambda a: a.reshape(length, d * ATT_GW)
    dq, dk, dv = pl.pallas_call(
        body, grid=(d, nsteps), in_specs=[own] * 6 + [prev, prev] + [nxt] * 4, out_specs=[own, own, own],
        out_shape=[jax.ShapeDtypeStruct((length, d * ATT_GW), F32), jax.ShapeDtypeStruct((length, d * ATT_GW), F32),
                   jax.ShapeDtypeStruct((length, d * ATT_GW), BF16)],
        name=name, compiler_params=_params(("parallel", "arbitrary")))(
            view(qg), view(kg), view(vg), view(dog), view(lse), view(delta), view(kg), view(vg),
            view(qg), view(dog), view(lse), view(delta))
    return dq.reshape(t, ATT_GW), dk.reshape(t, ATT_GW), dv.reshape(t, ATT_GW)


def _rope_tables(t):
    pos = jnp.arange(t, dtype=F32)
    inv = ROPE_THETA ** (-jnp.arange(0, HEAD, 2, dtype=F32) / HEAD)
    ang = pos[:, None] * inv[None, :]
    ang = jnp.concatenate([ang, ang], axis=-1)
    return jnp.cos(ang), jnp.sin(ang)


def _lower_bounds(logits):
    lb = jnp.cumsum(jax.nn.softmax(logits.astype(F32), axis=0), axis=0)
    return lb - lb[0:1]


FFN_ROWS = 256
FF_SHARD = 2 * D_FF // N_CHIPS


def _ffn_in_act(x, g, w_in, name, rider=None):
    t = x.shape[0]

    def body(x_ref, g_ref, w_ref, h_ref, ab_ref, u_ref):
        xv = x_ref[...]
        h = _bf(xv * _rms_rows(xv) * g_ref[...])
        h_ref[...] = h
        for s in range(N_CHIPS // 2):
            cols = slice(s * FF_SHARD, (s + 1) * FF_SHARD)
            a = _dot(h, w_ref[s])
            b = _dot(h, w_ref[s + N_CHIPS // 2])
            ab_ref[:, cols] = a.astype(ab_ref.dtype)
            ab_ref[:, D_FF + s * FF_SHARD:D_FF + (s + 1) * FF_SHARD] = b.astype(ab_ref.dtype)
            u_ref[:, cols] = (a * _sig(a) * b).astype(u_ref.dtype)

    row = lambda w: pl.BlockSpec((FFN_ROWS, w), lambda i: (i, 0))
    return _pcall(
        body, grid=(t // FFN_ROWS,),
        in_specs=[row(D_MODEL), pl.BlockSpec((1, D_MODEL), lambda i: (0, 0)),
                  pl.BlockSpec(w_in.shape, lambda i: (0, 0, 0))],
        out_specs=[row(D_MODEL), row(2 * D_FF), row(D_FF)],
        out_shape=[jax.ShapeDtypeStruct((t, D_MODEL), BF16), jax.ShapeDtypeStruct((t, 2 * D_FF), BF16),
                   jax.ShapeDtypeStruct((t, D_FF), BF16)],
        name=name, sem=("parallel",), args=(x, g, w_in), rider=rider)


def _ffn_bwd_du_act(dx, w_out, ab, name, rider=None):
    t = dx.shape[0]

    def body(dx_ref, w_ref, ab_ref, o_ref):
        du = 0.5 * _dot_nt(_bf(dx_ref[...]), w_ref[0])
        a = ab_ref[:, :D_FF].astype(F32)
        b = ab_ref[:, D_FF:].astype(F32)
        s = _sig(a)
        o_ref[:, :D_FF] = (du * b * (s * (1.0 + a * (1.0 - s)))).astype(o_ref.dtype)
        o_ref[:, D_FF:] = (du * a * s).astype(o_ref.dtype)

    row = lambda w: pl.BlockSpec((FFN_ROWS, w), lambda i: (i, 0))
    return _pcall(
        body, grid=(t // FFN_ROWS,),
        in_specs=[row(D_MODEL), pl.BlockSpec(w_out.shape, lambda i: (0, 0, 0)), row(2 * D_FF)],
        out_specs=row(2 * D_FF), out_shape=jax.ShapeDtypeStruct((t, 2 * D_FF), BF16),
        name=name, sem=("parallel",), args=(dx, w_out, ab), rider=rider)


MIX_ROWS = 512


def _gate_specs():
    return [pl.BlockSpec((MIX_ROWS, 512), lambda i, cb=cb: (i, cb)) for cb in (CB_GA, CB_GA + 1, CB_GB, CB_GB + 1)]


def _whole(a):
    return pl.BlockSpec(a.shape, lambda i: (0,) * a.ndim)


def _mix_tail_fwd(oa, ob, proj, x, w_a, w_b, w_o, name):
    t = x.shape[0]

    def body(oa_ref, ob_ref, ga0, ga1, gb0, gb1, x_ref, wa_ref, wb_ref, wo_ref, y_ref, m_ref, ya_ref, yb_ref):
        ya = _dot(oa_ref[...], wa_ref[0])
        yb = _cat([_dot(ob_ref[...], wb_ref[s]) for s in range(N_CHIPS)])
        merged = _bf(_sig(_cat([ga0[...], ga1[...]])) * ya + _sig(_cat([gb0[...], gb1[...]])) * yb)
        m_ref[...] = merged
        ya_ref[...] = ya.astype(ya_ref.dtype)
        yb_ref[...] = yb.astype(yb_ref.dtype)
        y_ref[...] = x_ref[...] + _dot(merged, wo_ref[0])

    row = lambda w: pl.BlockSpec((MIX_ROWS, w), lambda i: (i, 0))
    return pl.pallas_call(
        body, grid=(t // MIX_ROWS,),
        in_specs=[row(D_MODEL), row(ATT_GW)] + _gate_specs() + [row(D_MODEL), _whole(w_a), _whole(w_b), _whole(w_o)],
        out_specs=[row(D_MODEL)] * 4,
        out_shape=[jax.ShapeDtypeStruct((t, D_MODEL), F32)] + [jax.ShapeDtypeStruct((t, D_MODEL), BF16)] * 3,
        name=name, compiler_params=_params(("parallel",)))(oa, ob, proj, proj, proj, proj, x, w_a, w_b, w_o)


def _mix_tail_bwd(dx, proj, ya, yb, w_a, w_b, w_o, name):
    t = dx.shape[0]
    shard = D_MODEL // N_CHIPS

    def body(dx_ref, ga0, ga1, gb0, gb1, ya_ref, yb_ref, wa_ref, wb_ref, wo_ref, dya_ref, dyb_ref, dg_ref, doa_ref, dob_ref):
        dm = _dot_nt(_bf(dx_ref[...]), wo_ref[0])
        sa = _sig(_cat([ga0[...], ga1[...]]))
        sb = _sig(_cat([gb0[...], gb1[...]]))
        dya, dyb = _bf(dm * sa), _bf(dm * sb)
        dya_ref[...] = dya
        dyb_ref[...] = dyb
        dg_ref[:, :D_MODEL] = (dm * ya_ref[...].astype(F32) * sa * (1.0 - sa)).astype(dg_ref.dtype)
        dg_ref[:, D_MODEL:] = (dm * yb_ref[...].astype(F32) * sb * (1.0 - sb)).astype(dg_ref.dtype)
        doa_ref[...] = _dot_nt(dya, wa_ref[0])
        dob = _dot_nt(dyb[:, :shard], wb_ref[0])
        for s in range(1, N_CHIPS):
            dob = dob + _dot_nt(dyb[:, s * shard:(s + 1) * shard], wb_ref[s])
        dob_ref[...] = dob

    row = lambda w: pl.BlockSpec((MIX_ROWS, w), lambda i: (i, 0))
    return pl.pallas_call(
        body, grid=(t // MIX_ROWS,),
        in_specs=[row(D_MODEL)] + _gate_specs() + [row(D_MODEL), row(D_MODEL), _whole(w_a), _whole(w_b), _whole(w_o)],
        out_specs=[row(D_MODEL), row(D_MODEL), row(2 * D_MODEL), row(D_MODEL), row(ATT_GW)],
        out_shape=[jax.ShapeDtypeStruct((t, D_MODEL), BF16), jax.ShapeDtypeStruct((t, D_MODEL), BF16),
                   jax.ShapeDtypeStruct((t, 2 * D_MODEL), BF16), jax.ShapeDtypeStruct((t, D_MODEL), F32),
                   jax.ShapeDtypeStruct((t, ATT_GW), F32)],
        name=name, compiler_params=_params(("parallel",)))(dx, proj, proj, proj, proj, ya, yb, w_a, w_b, w_o)


def _ffn_fwd(x, g, src, l, pre):
    tag = f"l{l}_{pre}"
    w_in = src.weight(l, pre + "_w_in")
    h, ab, u = _ffn_in_act(x, g, w_in, name=tag + "_in_act", rider=src.ride(tag + "_in_act"))
    w_out = src.weight(l, pre + "_w_out")
    y = _mm_nn(u, w_out, name=tag + "_out", tm=512, tn=D_MODEL, out_dtype=F32, res=x, alpha=0.5, rider=src.ride(tag + "_out"))
    return y, (x, h, ab, u, w_in, w_out)


def _ffn_bwd(dx, saved, g, src, l, pre):
    tag = f"l{l}_{pre}"
    x, h, ab, u, w_in, w_out = saved
    g_out = _mm_tn(u, dx, nb=1, name=tag + "_bwd_wout", tm=1024, tk=1408, tn=D_MODEL, alpha=0.5, rider=src.ride(tag + "_bwd_wout"))
    src.grads(l, {pre + "_w_out": g_out.reshape(N_CHIPS, D_FF // N_CHIPS, D_MODEL)})
    dab = _ffn_bwd_du_act(dx, w_out, ab, name=tag + "_bwd_du_act", rider=src.ride(tag + "_bwd_du_act"))
    g_in = _mm_tn(h, dab, nb=N_CHIPS, name=tag + "_bwd_win", tm=2048, tk=D_MODEL, tn=FF_SHARD, rider=src.ride(tag + "_bwd_win"))
    src.grads(l, {pre + "_w_in": g_in})
    dh = _mm_nt(dab, w_in, name=tag + "_bwd_dh", tm=1024, tp=D_MODEL, tn=FF_SHARD, out_dtype=F32, rider=src.ride(tag + "_bwd_dh"))
    return _norm_bwd(dh, x, g, dx, name=tag + "_bwd_norm")


def _mix_fwd(x, small, lb, cos, sin, src, l):
    tag = f"l{l}_mix"
    w = {}
    h = _norm_fwd(x, small["mix_norm"], name=tag + "_norm")
    w["w_in"] = src.weight(l, "w_in")
    proj = _mm_nn(h, w["w_in"], name=tag + "_in", tm=1024, tn=896, out_dtype=F32, rider=src.ride(tag + "_in"))
    oscan, oa, sall = _hgrn_fwd(proj, lb, small["hgrn_out_norm"], name=tag + "_hgrn", rider=src.ride(tag + "_hgrn"))
    qk = _qk_fwd(proj, cos, sin, small["attn_q_norm"], small["attn_k_norm"], name=tag + "_qk")
    outs, lses = [], []
    for g in range(ATT_GROUPS):
        o, lse = _attn_fwd(qk[g], qk[3 + g], qk[6 + g], g, name=f"{tag}_attn{g}")
        outs.append(o)
        lses.append(lse)
    ob = _merge_fwd(outs, lses, name=tag + "_merge")
    w.update({n: src.weight(l, n) for n in ("w_branch_a", "w_branch_b", "w_out")})
    y, merged, ya, yb = _mix_tail_fwd(oa, ob, proj, x, w["w_branch_a"], w["w_branch_b"], w["w_out"], name=tag + "_tail")
    return y, (x, h, proj, oscan, oa, sall, qk, outs, lses, ob, ya, yb, merged, w)


def _mix_bwd(dx, saved, small, lb, cos, sin, src, l, lb_live):
    tag = f"l{l}_mix"
    x, h, proj, oscan, oa, sall, qk, outs, lses, ob, ya, yb, merged, w = saved
    g_wout = _mm_tn(merged, dx, nb=1, name=tag + "_bwd_wout", tm=1024, tk=D_MODEL, tn=D_MODEL)
    dya, dyb, dgab, doa, dob = _mix_tail_bwd(dx, proj, ya, yb, w["w_branch_a"], w["w_branch_b"], w["w_out"], name=tag + "_bwd_tail")
    g_wa = _mm_tn(oa, dya, nb=1, name=tag + "_bwd_wa", tm=1024, tk=D_MODEL, tn=D_MODEL)
    g_wb =_mm_tn(ob, dyb, nb=N_CHIPS, name=tag + "_bwd_wb", tm=2048, tk=ATT_GW, tn=256)
    mb = _merge_bwd(dob, outs, lses, name=tag + "_bwd_merge")
    dqk, dvs = [None] * 6, []
    for g in range(ATT_GROUPS):
        dq, dk, dv = _attn_bwd(qk[g], qk[3 + g], qk[6 + g], mb[g], lses[g], mb[3 + g], g, name=f"{tag}_bwd_attn{g}")
        dqk[g], dqk[3 + g] = dq, dk
        dvs.append(dv)
    dqk_cols, dqn, dkn = _qk_bwd(dqk, proj, cos, sin, small["attn_q_norm"], small["attn_k_norm"], name=tag + "_bwd_qk")
    dh4, dgn, dlb = _hgrn_bwd(doa, oscan, proj, sall, lb, small["hgrn_out_norm"], name=tag + "_bwd_hgrn", precise=lb_live,
                              rider=src.ride(tag + "_bwd_hgrn"))
    dproj = _assemble_dproj(dh4, dqk_cols, dvs, dgab, name=tag + "_bwd_cat")
    src.grads(l, dict(w_branch_a=g_wa.reshape(N_CHIPS, D_MODEL // N_CHIPS, D_MODEL), w_branch_b=g_wb,
                      w_out=g_wout.reshape(N_CHIPS, D_MODEL // N_CHIPS, D_MODEL)))
    g_win = _mm_tn(h, dproj, nb=N_CHIPS, name=tag + "_bwd_win", tm=2048, tk=D_MODEL, tn=896, rider=src.ride(tag + "_bwd_win"))
    src.grads(l, dict(w_in=g_win))
    dh = _mm_nt(dproj, w["w_in"], name=tag + "_bwd_dh", tm=1024, tp=D_MODEL, tn=2688, out_dtype=F32, rider=src.ride(tag + "_bwd_dh"))
    dx, dg = _norm_bwd(dh, x, small["mix_norm"], dx, name=tag + "_bwd_norm")
    return dx, dict(mix_norm=dg, hgrn_out_norm=dgn, lb=dlb, attn_q_norm=dqn, attn_k_norm=dkn)


BIG = ("ffn1_w_in", "ffn1_w_out", "w_in", "w_branch_a", "w_branch_b", "w_out", "ffn2_w_in", "ffn2_w_out")
ROW_SHARDED = ("ffn1_w_out", "w_branch_a", "w_out", "ffn2_w_out")
SMALL = ("ffn1_norm", "mix_norm", "hgrn_lb_logits", "hgrn_out_norm", "attn_q_norm", "attn_k_norm", "ffn2_norm")
WEIGHTS = ("ffn1_norm", "ffn1_w_in", "ffn1_w_out", "mix_norm", "w_in", "hgrn_lb_logits", "hgrn_out_norm", "attn_q_norm",
           "attn_k_norm", "w_branch_a", "w_branch_b", "w_out", "ffn2_norm", "ffn2_w_in", "ffn2_w_out")
SMALL_ROWS = 8


def _matmul_ready(name, a):
    return a.reshape(1, a.shape[0] * a.shape[1], a.shape[2]) if name in ROW_SHARDED else a


def _layer_small(small, l):
    s = {n: small[n][l].reshape(1, D_MODEL) for n in ("ffn1_norm", "mix_norm", "hgrn_out_norm", "ffn2_norm")}
    s.update({n: small[n][l] for n in ("attn_q_norm", "attn_k_norm")})
    return s


def _local_step(x, target, small, src):
    t = x.shape[0]
    cos, sin = _rope_tables(t)
    lbs = _lower_bounds(small["hgrn_lb_logits"])
    saved = []
    for l in range(2):
        sm = _layer_small(small, l)
        lb = lbs[l].reshape(1, D_MODEL)
        x, s1 = _ffn_fwd(x, sm["ffn1_norm"], src, l, "ffn1")
        x, s2 = _mix_fwd(x, sm, lb, cos, sin, src, l)
        x, s3 = _ffn_fwd(x, sm["ffn2_norm"], src, l, "ffn2")
        saved.append((sm, lb, s1, s2, s3))
    dx, sq = _loss_fwd_bwd(x, target, name="loss")
    small_rows = [None, None]
    for l in (1, 0):
        sm, lb, s1, s2, s3 = saved[l]
        dx, dg2 = _ffn_bwd(dx, s3, sm["ffn2_norm"], src, l, "ffn2")
        dx, g = _mix_bwd(dx, s2, sm, lb, cos, sin, src, l, lb_live=l > 0)
        dx, dg1 = _ffn_bwd(dx, s1, sm["ffn1_norm"], src, l, "ffn1")
        pad = lambda a: jnp.pad(a[:ATT_GROUPS].reshape(1, ATT_GROUPS * HEAD), ((0, 0), (0, D_MODEL - ATT_GROUPS * HEAD)))
        small_rows[l] = jnp.concatenate(
            [dg1, g["mix_norm"], g["lb"], g["hgrn_out_norm"], pad(g["attn_q_norm"]), pad(g["attn_k_norm"]), dg2,
             jnp.zeros((SMALL_ROWS - 7, D_MODEL), F32)], axis=0)
    return jnp.sum(sq), dx, jnp.concatenate(small_rows, axis=0)


def _coords():
    return lax.axis_index("x"), lax.axis_index("y"), lax.axis_index("c")


def _other_chips(x, y):
    return [(1 - x, y), (x, 1 - y), (1 - x, 1 - y)]


def _half_rows(rows, which):
    return pl.ds(which * (rows // 2), rows // 2)


def _gather_rider(shards):
    n = len(shards)

    def copies(w, full, sems):
        send, recv, fsend, frecv = sems
        x, y, c = _coords()
        slot = 2 * x + y
        chips = _other_chips(x, y)

        def copy(i, j, blk, src, pair, to):
            return pltpu.make_async_remote_copy(src_ref=src, dst_ref=blk, send_sem=pair[0].at[i * 3 + j],
                                                recv_sem=pair[1].at[i * 3 + j], device_id=to, device_id_type=MESH)

        def block(i, chip_slot, core):
            return full[i].at[chip_slot, _half_rows(shards[i].shape[0], core)]

        pairs = [(i, j, chip) for i in range(n) for j, chip in enumerate(chips)]

        def first():
            return [copy(i, j, block(i, slot, c), w[i].at[_half_rows(shards[i].shape[0], c)], (send, recv), (*chip, c))
                    for i, j, chip in pairs]

        def landed(core, pair):
            return [copy(i, j, block(i, 2 * chip[0] + chip[1], core), block(i, 2 * chip[0] + chip[1], core), pair, (x, y, 1 - c))
                    for i, j, chip in pairs]

        return first, landed

    def begin(w, full, sems):
        for cp in copies(w, full, sems)[0]():
            cp.start()

    def end(w, full, sems):
        first, landed = copies(w, full, sems)
        forwards = landed(lax.axis_index("c"), sems[2:])
        for arrival, forward in zip(landed(lax.axis_index("c"), sems[:2]), forwards):
            arrival.wait_recv()
            forward.start()
        for cp in landed(1 - lax.axis_index("c"), sems[2:]):
            cp.wait_recv()
        for cp in first() + forwards:
            cp.wait_send()

    out_shape = [jax.ShapeDtypeStruct((N_CHIPS,) + s.shape, s.dtype) for s in shards]
    return _Rider(shards, out_shape, [pltpu.SemaphoreType.DMA((3 * n,))] * 4, begin, end)


N_RECV = 7


def _scatter_rider(parts):
    n = len(parts)

    def copies(p, out, sems):
        send, recv = sems
        x, y, c = _coords()
        slot = 2 * x + y
        chips = _other_chips(x, y)

        def arrivals():
            return [pltpu.make_async_remote_copy(
                src_ref=out[i].at[k], dst_ref=out[i].at[k], send_sem=send.at[0], recv_sem=recv.at[i * N_RECV + k],
                device_id=(x, y, c), device_id_type=MESH) for i in range(n) for k in range(N_RECV)]

        sends = []
        for i in range(n):
            rows = parts[i].shape[1]
            for j, chip in enumerate(chips):
                for core in (0, 1):
                    sends.append(pltpu.make_async_remote_copy(
                        src_ref=p[i].at[2 * chip[0] + chip[1], _half_rows(rows, core)], dst_ref=out[i].at[2 * j + c],
                        send_sem=send.at[i * N_RECV + 2 * j + core], recv_sem=recv.at[i * N_RECV + 2 * j + c],
                        device_id=(*chip, core), device_id_type=MESH))
            sends.append(pltpu.make_async_remote_copy(
                src_ref=p[i].at[slot, _half_rows(rows, 1 - c)], dst_ref=out[i].at[6], send_sem=send.at[i * N_RECV + 6],
                recv_sem=recv.at[i * N_RECV + 6], device_id=(x, y, 1 - c), device_id_type=MESH))
        return sends, arrivals

    def begin(p, out, sems):
        for cp in copies(p, out, sems)[0]:
            cp.start()

    def end(p, out, sems):
        sends, arrivals = copies(p, out, sems)
        for cp in arrivals():
            cp.wait_recv()
        for cp in sends:
            cp.wait_send()

    out_shape = [jax.ShapeDtypeStruct((N_RECV, a.shape[1] // 2, a.shape[2]), a.dtype) for a in parts]
    return _Rider(parts, out_shape, [pltpu.SemaphoreType.DMA((N_RECV * n,))] * 2, begin, end)


def _run_alone(rider, name):
    _pcall(lambda: None, grid=(), in_specs=[], out_specs=[], out_shape=[], name=name, sem=(), args=(), rider=rider)
    return rider.result


def _sum_partials(own, parts, name):
    r, wd = own.shape
    tm = next(t for t in (256, 128, 64, 32, 16) if r % t == 0)

    def body(own_ref, p_ref, o_ref):
        acc = own_ref[...].astype(F32)
        for k in range(N_RECV):
            acc = acc + p_ref[k].astype(F32)
        o_ref[...] = acc

    return pl.pallas_call(
        body, grid=(r // tm,),
        in_specs=[pl.BlockSpec((tm, wd), lambda i: (i, 0)), pl.BlockSpec((N_RECV, tm, wd), lambda i: (0, i, 0))],
        out_specs=pl.BlockSpec((tm, wd), lambda i: (i, 0)), out_shape=jax.ShapeDtypeStruct((r, wd), F32),
        name=name, compiler_params=_params(("parallel",)))(own, parts)


def _exchange_halves(reduced, name):
    n = len(reduced)

    def body(*refs):
        r, out = refs[:n], refs[n:2 * n]
        send, recv = refs[2 * n:]
        x, y, c = _coords()
        sib = [pltpu.make_async_remote_copy(src_ref=r[i], dst_ref=out[i], send_sem=send.at[i], recv_sem=recv.at[i],
                                            device_id=(x, y, 1 - c), device_id_type=MESH) for i in range(n)]
        for cp in sib:
            cp.start()
        for cp in sib:
            cp.wait_recv()
        for cp in sib:
            cp.wait_send()

    out_shape = [jax.ShapeDtypeStruct(a.shape, a.dtype) for a in reduced]
    return pl.pallas_call(body, in_specs=[ANY] * n, out_specs=[ANY] * n, out_shape=out_shape,
                          scratch_shapes=[pltpu.SemaphoreType.DMA((n,))] * 2, name=name)(*reduced)


def _reduce_finish(parts, recv, tag):
    x, y, c = _coords()
    slot = 2 * x + y
    halves = []
    for i, (p, r) in enumerate(zip(parts, recv)):
        half = p.shape[1] // 2
        own = lax.dynamic_slice(p, (slot, c * half, 0), (1, half, p.shape[2]))[0]
        halves.append(_sum_partials(own, r, name=f"{tag}_sum{i}"))
    theirs = _exchange_halves(halves, name=tag + "_exchange")
    return [jnp.where(c == 0, jnp.concatenate([h, t], axis=0), jnp.concatenate([t, h], axis=0)) for h, t in zip(halves, theirs)]


GATHER_RIDES = {
    "l0_ffn1_in_act": ((0, "w_in"),),
    "l0_ffn1_out": ((0, "w_branch_a"), (0, "w_branch_b"), (0, "w_out")),
    "l0_mix_in": ((0, "ffn2_w_in"), (0, "ffn2_w_out"), (1, "ffn1_w_in"), (1, "ffn1_w_out")),
    "l0_mix_hgrn": ((1, "w_in"), (1, "w_branch_a"), (1, "w_branch_b"), (1, "w_out")),
    "l0_ffn2_in_act": ((1, "ffn2_w_in"), (1, "ffn2_w_out")),
}
ALONE_FIRST = ((0, "ffn1_w_in"), (0, "ffn1_w_out"))
SCATTER_RIDES = {
    "l1_mix_bwd_hgrn": ((1, "ffn2_w_in"), (1, "ffn2_w_out")),
    "l0_ffn2_bwd_win": ((1, "ffn1_w_in"),),
    "l0_ffn2_bwd_dh": ((1, "ffn1_w_out"), (1, "w_branch_a"), (1, "w_branch_b"), (1, "w_out")),
    "l0_mix_bwd_hgrn": ((1, "w_in"), (0, "ffn2_w_out")),
    "l0_mix_bwd_win": ((0, "ffn2_w_in"),),
    "l0_mix_bwd_dh": ((0, "w_in"),),
    "l0_ffn1_bwd_wout": ((0, "w_branch_a"), (0, "w_branch_b"), (0, "w_out")),
    "l0_ffn1_bwd_du_act": ((0, "ffn1_w_out"),),
    "l0_ffn1_bwd_dh": ((0, "ffn1_w_in"),),
}


class _Exchange:
    def __init__(self, shards):
        self.shards = shards
        self.pending = []
        self.full = {}
        self.parts = {}
        self.recv = {}

    def _gather(self, keys):
        return _gather_rider([self.shards[n][l] for l, n in keys]), "gather", list(keys)

    def _scatter(self, keys):
        return _scatter_rider([self.parts[k] for k in keys]), "scatter", list(keys)

    def _unpack(self):
        slot = 2 * lax.axis_index("x") + lax.axis_index("y")
        waiting = []
        for rider, kind, keys in self.pending:
            if rider.result is None:
                waiting.append((rider, kind, keys))
            elif kind == "gather":
                for (l, n), got in zip(keys, rider.result):
                    self.full[(l, n)] = lax.dynamic_update_slice(got, self.shards[n][l][None], (slot, 0, 0))
            else:
                self.recv.update(zip(keys, rider.result))
        self.pending = waiting

    def ride(self, host):
        if host in GATHER_RIDES:
            self.pending.append(self._gather(GATHER_RIDES[host]))
        elif host in SCATTER_RIDES:
            self.pending.append(self._scatter(SCATTER_RIDES[host]))
        else:
            return None
        return self.pending[-1][0]

    def weight(self, l, name):
        self._unpack()
        if (l, name) not in self.full:
            assert (l, name) in ALONE_FIRST, (l, name)
            job = self._gather(ALONE_FIRST)
            _run_alone(job[0], name="gather_first")
            self.pending.append(job)
            self._unpack()
        return _matmul_ready(name, self.full[(l, name)])

    def grads(self, l, partials):
        self.parts.update({(l, n): a for n, a in partials.items()})

    def reduce(self):
        self._unpack()
        assert not self.pending and set(self.recv) == set(self.parts)
        out = {}
        for l in range(2):
            done = _reduce_finish([self.parts[(l, n)] for n in BIG], [self.recv[(l, n)] for n in BIG], f"reduce_l{l}")
            out[l] = dict(zip(BIG, done))
        return {n: jnp.stack([out[0][n], out[1][n]], axis=0) for n in BIG}


def _all_reduce_small(rows):
    r = rows.shape[0]

    def body(x_ref, o_ref, buf, send, recv):
        x, y, c = _coords()
        me = 4 * x + 2 * y + c
        buf[me] = x_ref[...]
        copies = []
        for k in range(1, 8):
            peer = (x ^ (k >> 2), y ^ ((k >> 1) & 1), c ^ (k & 1))
            cp = pltpu.make_async_remote_copy(src_ref=x_ref, dst_ref=buf.at[me], send_sem=send.at[k - 1], recv_sem=recv.at[me],
                                              device_id=peer, device_id_type=MESH)
            cp.start()
            copies.append(cp)
        for k in range(1, 8):
            src = 4 * (x ^ (k >> 2)) + 2 * (y ^ ((k >> 1) & 1)) + (c ^ (k & 1))
            pltpu.make_async_remote_copy(src_ref=x_ref, dst_ref=buf.at[src], send_sem=send.at[0], recv_sem=recv.at[src],
                                         device_id=(x, y, c), device_id_type=MESH).wait_recv()
        for cp in copies:
            cp.wait_send()
        acc = buf[0]
        for k in range(1, 8):
            acc = acc + buf[k]
        o_ref[...] = acc

    vm = pl.BlockSpec(memory_space=pltpu.VMEM)
    return pl.pallas_call(
        body, in_specs=[vm], out_specs=vm, out_shape=jax.ShapeDtypeStruct(rows.shape, F32),
        scratch_shapes=[pltpu.VMEM((8, r, D_MODEL), F32), pltpu.SemaphoreType.DMA((7,)), pltpu.SemaphoreType.DMA((8,))],
        name="all_reduce_small")(rows)


def _adamw_math(w, g, m, v):
    m = ADAM_B1 * m + (1.0 - ADAM_B1) * g
    v = ADAM_B2 * v + (1.0 - ADAM_B2) * (g * g)
    m_hat = m / (1.0 - ADAM_B1 ** ADAM_STEP)
    v_hat = v / (1.0 - ADAM_B2 ** ADAM_STEP)
    return -ADAM_LR * (m_hat / (jnp.sqrt(v_hat) + ADAM_EPS) + ADAM_WD * w), m, v


def _adamw(w, g, m, v, name):
    shape = w.shape
    cols = shape[-1]
    flat = lambda a: a.reshape(-1, cols)
    rows = flat(w).shape[0]
    tm = 128 if rows % 128 == 0 else rows
    ins = [('t', flat(a), cols, 0) for a in (w, g, m, v)]
    res = _ew(_adamw_math, ins, [('t', cols, F32)] * 3, rows=rows, tm=tm, name=name)
    return [a.reshape(shape) for a in res]


def _small_update(sums, logits, w, m, v):
    def body(s_ref, lg_ref, w_ref, m_ref, v_ref, g_ref, d_ref, nm_ref, nv_ref):
        s = s_ref[...]
        l0, l1 = lg_ref[0:1, :], lg_ref[1:2, :]
        mx = jnp.maximum(l0, l1)
        e0, e1 = jnp.exp(l0 - mx), jnp.exp(l1 - mx)
        sm0, sm1 = e0 / (e0 + e1), e1 / (e0 + e1)
        dl1 = s_ref[SMALL_ROWS + 2:SMALL_ROWS + 3, :] * sm0 * sm1
        row = lax.broadcasted_iota(jnp.int32, s.shape, 0)
        g = jnp.where(row == 2, -dl1, jnp.where(row == SMALL_ROWS + 2, dl1, s))
        d, nm, nv = _adamw_math(w_ref[...], g, m_ref[...], v_ref[...])
        g_ref[...] = g
        d_ref[...] = d
        nm_ref[...] = nm
        nv_ref[...] = nv

    vm = pl.BlockSpec(memory_space=pltpu.VMEM)
    return pl.pallas_call(body, in_specs=[vm] * 5, out_specs=[vm] * 4,
                          out_shape=[jax.ShapeDtypeStruct(sums.shape, F32)] * 4, name="small_update")(sums, logits, w, m, v)


def _pack_small(vals):
    rows = []
    for l in range(2):
        for n in ("ffn1_norm", "mix_norm", "hgrn_lb_logits", "hgrn_out_norm", "attn_q_norm", "attn_k_norm", "ffn2_norm"):
            a = vals[n][l].reshape(1, -1)
            rows.append(jnp.pad(a, ((0, 0), (0, D_MODEL - a.shape[1]))))
        rows.append(jnp.zeros((SMALL_ROWS - 7, D_MODEL), F32))
    return jnp.concatenate(rows, axis=0)


def _unpack_small(packed):
    out = {}
    for k, n in enumerate(("ffn1_norm", "mix_norm", "hgrn_lb_logits", "hgrn_out_norm", "attn_q_norm", "attn_k_norm", "ffn2_norm")):
        a = jnp.stack([packed[k], packed[SMALL_ROWS + k]], axis=0)
        out[n] = a[:, :ATT_GROUPS * HEAD].reshape(2, ATT_GROUPS, HEAD) if n.startswith("attn") else a
    return out


def kernel(x, ffn1_norm, ffn1_w_in, ffn1_w_out, mix_norm, w_in, hgrn_lb_logits, hgrn_out_norm, attn_q_norm, attn_k_norm, w_branch_a, w_branch_b, w_out, ffn2_norm, ffn2_w_in, ffn2_w_out, loss_target, m_ffn1_norm, m_ffn1_w_in, m_ffn1_w_out, m_mix_norm, m_w_in, m_hgrn_lb_logits, m_hgrn_out_norm, m_attn_q_norm, m_attn_k_norm, m_w_branch_a, m_w_branch_b, m_w_out, m_ffn2_norm, m_ffn2_w_in, m_ffn2_w_out, v_ffn1_norm, v_ffn1_w_in, v_ffn1_w_out, v_mix_norm, v_w_in, v_hgrn_lb_logits, v_hgrn_out_norm, v_attn_q_norm, v_attn_k_norm, v_w_branch_a, v_w_branch_b, v_w_out, v_ffn2_norm, v_ffn2_w_in, v_ffn2_w_out):
    a = locals()
    w = {n: a[n] for n in WEIGHTS}
    m = {n: a["m_" + n] for n in WEIGHTS}
    v = {n: a["v_" + n] for n in WEIGHTS}

    exchange = _Exchange({n: w[n].astype(BF16) for n in BIG})
    small = {n: w[n] for n in SMALL}
    sq, grad_x, small_rows = _local_step(x[0], loss_target[0], small, exchange)
    loss = lax.psum(sq, ("x", "y", "c")) * (0.5 / D_MODEL)
    grads = exchange.reduce()

    sums = _all_reduce_small(small_rows)
    g_s, d_s, m_s, v_s = _small_update(sums, w["hgrn_lb_logits"], _pack_small(small), _pack_small({n: m[n] for n in SMALL}),
                                       _pack_small({n: v[n] for n in SMALL}))
    grads.update(_unpack_small(g_s))
    delta, new_m, new_v = _unpack_small(d_s), _unpack_small(m_s), _unpack_small(v_s)
    for n in BIG:
        delta[n], new_m[n], new_v[n] = _adamw(w[n], grads[n], m[n], v[n], name="adamw_" + n)

    return (loss, grad_x[None], *[grads[n] for n in WEIGHTS], *[delta[n] for n in WEIGHTS],
            *[new_m[n] for n in WEIGHTS], *[new_v[n] for n in WEIGHTS])
```

```python
import functools

import jax
import jax.numpy as jnp
from jax import lax
from jax.experimental import pallas as pl
from jax.experimental.pallas import tpu as pltpu

F32 = jnp.float32
BF16 = jnp.bfloat16
MESH = pl.DeviceIdType.MESH

D_MODEL = 1024
D_FF = 2816
N_CHIPS = 4
HEAD = 128
HG_HEADS = 8
HG_CHUNK = 64
ATT_GROUPS = 3
ATT_HEADS = 4
ATT_GW = ATT_HEADS * HEAD
DILATIONS = (1, 4, 16)
ATT_BLK = 128
ATT_STEP_BLOCKS = 4
P_IN = 10752
CB_AQ, CB_AK, CB_AV, CB_GA, CB_GB = 8, 11, 14, 17, 19
EPS = 1e-6
ROPE_THETA = 10000.0
ADAM_LR, ADAM_B1, ADAM_B2, ADAM_EPS, ADAM_WD, ADAM_STEP = 0.001, 0.9, 0.999, 1e-08, 0.01, 10
VMEM_LIMIT_V7X = 56 * 1024 * 1024
NEG = -1e30


def _params(sem):
    return pltpu.CompilerParams(dimension_semantics=sem, vmem_limit_bytes=VMEM_LIMIT_V7X)


def _sig(x):
    return 1.0 / (1.0 + jnp.exp(-x))


def _dot(a, b):
    return jnp.dot(a, b, preferred_element_type=F32)


def _dot_nt(a, b):
    return lax.dot_general(a, b, (((1,), (1,)), ((), ())), preferred_element_type=F32)


def _dot_tn(a, b):
    return lax.dot_general(a, b, (((0,), (0,)), ((), ())), preferred_element_type=F32)


def _bf(x):
    return x.astype(BF16)


ANY = pl.BlockSpec(memory_space=pl.ANY)


class _Rider:
    def __init__(self, args, out_shape, sems, begin, end):
        self.args, self.out_shape, self.sems, self.begin, self.end = list(args), list(out_shape), list(sems), begin, end
        self.result = None


def _pcall(body, *, grid, in_specs, out_specs, out_shape, name, sem, args, scratch_shapes=(), rider=None):
    multi = isinstance(out_shape, (list, tuple))
    o_specs = list(out_specs) if multi else [out_specs]
    o_shape = list(out_shape) if multi else [out_shape]
    if rider is None:
        res = pl.pallas_call(body, grid=grid, in_specs=list(in_specs), out_specs=o_specs, out_shape=o_shape,
                             scratch_shapes=list(scratch_shapes), name=name, compiler_params=_params(sem))(*args)
        return list(res) if multi else res[0]
    counts = [len(in_specs), len(rider.args), len(o_specs), len(rider.out_shape), len(scratch_shapes)]

    def wrapped(*refs):
        groups, at = [], 0
        for c in counts:
            groups.append(refs[at:at + c])
            at += c
        h_in, r_in, h_out, r_out, h_scratch = groups
        r_sems = refs[at:]
        if grid:
            ids = [pl.program_id(a) for a in range(len(grid))]
            first = functools.reduce(jnp.logical_and, [i == 0 for i in ids])
            last = functools.reduce(jnp.logical_and, [i == g - 1 for i, g in zip(ids, grid)])
            pl.when(first)(lambda: rider.begin(r_in, r_out, r_sems))
            body(*h_in, *h_out, *h_scratch)
            pl.when(last)(lambda: rider.end(r_in, r_out, r_sems))
        else:
            rider.begin(r_in, r_out, r_sems)
            body(*h_in, *h_out, *h_scratch)
            rider.end(r_in, r_out, r_sems)

    res = pl.pallas_call(
        wrapped, grid=grid, in_specs=list(in_specs) + [ANY] * counts[1], out_specs=o_specs + [ANY] * counts[3],
        out_shape=o_shape + rider.out_shape, scratch_shapes=list(scratch_shapes) + rider.sems, name=name,
        compiler_params=_params(("arbitrary",) * len(grid)))(*args, *rider.args)
    rider.result = list(res[counts[2]:])
    return list(res[:counts[2]]) if multi else res[0]


def _mm_nn(a, b3, *, name, tm, tn, out_dtype, res=None, alpha=1.0, rider=None):
    m, k = a.shape
    nb, _, nw = b3.shape
    per = nw // tn
    assert nw % tn == 0 and m % tm == 0
    has_res = res is not None

    def body(*refs):
        if has_res:
            a_ref, b_ref, r_ref, o_ref = refs
        else:
            a_ref, b_ref, o_ref = refs
        acc = _dot(_bf(a_ref[...]), b_ref[...])
        if alpha != 1.0:
            acc = alpha * acc
        if has_res:
            acc = r_ref[...] + acc
        o_ref[...] = acc.astype(o_ref.dtype)

    in_specs = [pl.BlockSpec((tm, k), lambda i, j: (i, 0)),
                pl.BlockSpec((None, k, tn), lambda i, j: (j // per, 0, j % per))]
    args = [a, b3]
    if has_res:
        in_specs.append(pl.BlockSpec((tm, tn), lambda i, j: (i, j)))
        args.append(res)
    return _pcall(body, grid=(m // tm, nb * per), in_specs=in_specs, out_specs=pl.BlockSpec((tm, tn), lambda i, j: (i, j)),
                  out_shape=jax.ShapeDtypeStruct((m, nb * nw), out_dtype), name=name, sem=("parallel", "arbitrary"),
                  args=args, rider=rider)


def _mm_nt(d, b3, *, name, tm, tp, tn, out_dtype, alpha=1.0, rider=None):
    m, n = d.shape
    nb, p, nw = b3.shape
    per = nw // tn
    nk = n // tn
    assert nb * nw == n and nw % tn == 0 and p % tp == 0 and m % tm == 0

    def body(d_ref, b_ref, o_ref, acc_ref):
        kk = pl.program_id(2)

        @pl.when(kk == 0)
        def _():
            acc_ref[...] = jnp.zeros_like(acc_ref)

        acc_ref[...] += _dot_nt(_bf(d_ref[...]), b_ref[...])

        @pl.when(kk == nk - 1)
        def _():
            o_ref[...] = (alpha * acc_ref[...]).astype(o_ref.dtype)

    return _pcall(
        body, grid=(m // tm, p // tp, nk),
        in_specs=[pl.BlockSpec((tm, tn), lambda i, j, kk: (i, kk)),
                  pl.BlockSpec((None, tp, tn), lambda i, j, kk: (kk // per, j, kk % per))],
        out_specs=pl.BlockSpec((tm, tp), lambda i, j, kk: (i, j)),
        out_shape=jax.ShapeDtypeStruct((m, p), out_dtype),
        scratch_shapes=[pltpu.VMEM((tm, tp), F32)],
        name=name, sem=("parallel", "parallel", "arbitrary"), args=(d, b3), rider=rider)


def _mm_tn(a, d, *, nb, name, tm, tk, tn, alpha=1.0, rider=None):
    m, k = a.shape
    _, n = d.shape
    nw = n // nb
    per = nw // tn
    nm = m // tm
    assert nw % tn == 0 and k % tk == 0 and m % tm == 0

    def body(a_ref, d_ref, o_ref, acc_ref):
        mm = pl.program_id(2)

        @pl.when(mm == 0)
        def _():
            acc_ref[...] = jnp.zeros_like(acc_ref)

        acc_ref[...] += _dot_tn(_bf(a_ref[...]), _bf(d_ref[...]))

        @pl.when(mm == nm - 1)
        def _():
            o_ref[...] = (alpha * acc_ref[...]).astype(o_ref.dtype)

    return _pcall(
        body, grid=(k // tk, nb * per, nm),
        in_specs=[pl.BlockSpec((tm, tk), lambda i, j, mm: (mm, i)),
                  pl.BlockSpec((tm, tn), lambda i, j, mm: (mm, j))],
        out_specs=pl.BlockSpec((None, tk, tn), lambda i, j, mm: (j // per, i, j % per)),
        out_shape=jax.ShapeDtypeStruct((nb, k, nw), BF16),
        scratch_shapes=[pltpu.VMEM((tk, tn), F32)],
        name=name, sem=("parallel", "parallel", "arbitrary"), args=(a, d), rider=rider)


def _ew(fn, ins, outs, *, rows, tm, name):
    in_specs, args = [], []
    for s in ins:
        if s[0] == 't':
            _, arr, w, cb = s
            in_specs.append(pl.BlockSpec((tm, w), lambda i, cb=cb: (i, cb)))
        else:
            arr = s[1]
            in_specs.append(pl.BlockSpec(arr.shape, lambda i, nd=arr.ndim: (0,) * nd))
        args.append(arr)
    out_specs, out_shape = [], []
    for s in outs:
        if s[0] == 't':
            _, w, dt = s
            out_specs.append(pl.BlockSpec((tm, w), lambda i: (i, 0)))
            out_shape.append(jax.ShapeDtypeStruct((rows, w), dt))
        else:
            out_specs.append(pl.BlockSpec(s[1], lambda i: (0, 0)))
            out_shape.append(jax.ShapeDtypeStruct(s[1], F32))
    n_in = len(ins)

    def body(*refs):
        res = fn(*[r[...] for r in refs[:n_in]])
        if not isinstance(res, (tuple, list)):
            res = (res,)
        for r, s, v in zip(refs[n_in:], outs, res):
            if s[0] == 't':
                r[...] = v.astype(r.dtype)
            else:
                @pl.when(pl.program_id(0) == 0)
                def _(r=r):
                    r[...] = jnp.zeros_like(r)

                r[...] += v

    res = pl.pallas_call(
        body, grid=(rows // tm,), in_specs=in_specs, out_specs=out_specs, out_shape=out_shape,
        name=name, compiler_params=_params(("arbitrary",)))(*args)
    return res


def _heads(x):
    return [x[:, h * HEAD:(h + 1) * HEAD] for h in range(x.shape[1] // HEAD)]


def _cat(xs):
    return jnp.concatenate(xs, axis=1)


def _head_mean(x):
    return _cat([jnp.broadcast_to(jnp.mean(h, axis=1, keepdims=True), h.shape) for h in _heads(x)])


def _rms_rows(x):
    return lax.rsqrt(jnp.mean(x * x, axis=1, keepdims=True) + EPS)


def _norm_fwd(x, g, name):
    return _ew(lambda xv, gv: xv * _rms_rows(xv) * gv,
               [('t', x, D_MODEL, 0), ('f', g)], [('t', D_MODEL, BF16)], rows=x.shape[0], tm=512, name=name)[0]


def _norm_bwd(dh, x, g, dx, name):
    def fn(dhv, xv, gv, dxv):
        r = _rms_rows(xv)
        xh = xv * r
        dxh = dhv * gv
        out = dxv + r * (dxh - xh * jnp.mean(dxh * xh, axis=1, keepdims=True))
        return out, jnp.sum(dhv * xh, axis=0, keepdims=True)

    return _ew(fn, [('t', dh, D_MODEL, 0), ('t', x, D_MODEL, 0), ('f', g), ('t', dx, D_MODEL, 0)],
               [('t', D_MODEL, F32), ('acc', (1, D_MODEL))], rows=x.shape[0], tm=512, name=name)


def _loss_fwd_bwd(y, target, name):
    def fn(yv, tv):
        e = yv - tv
        return e * (1.0 / D_MODEL), jnp.sum(e * e, axis=0, keepdims=True)

    return _ew(fn, [('t', y, D_MODEL, 0), ('t', target, D_MODEL, 0)], [('t', D_MODEL, F32), ('acc', (1, D_MODEL))],
               rows=y.shape[0], tm=512, name=name)


def _rot(x):
    sgn = jnp.where(lax.broadcasted_iota(jnp.int32, x.shape, 1) < HEAD // 2, -1.0, 1.0)
    return pltpu.roll(x, HEAD // 2, 1) * sgn


def _gain_rows(qn, kn):
    return [a[g:g + 1] for a in (qn, kn) for g in range(ATT_GROUPS)]


def _qk_fwd(proj, cos, sin, qn, kn, name):
    def fn(*v):
        xs, cosv, sinv, gains, vs = v[:6], v[6], v[7], v[8:14], v[14:17]
        outs = []
        for j, x in enumerate(xs):
            gain = gains[j]
            ys = []
            for xh in _heads(x):
                xn = xh * _rms_rows(xh) * gain
                ys.append(xn * cosv + _rot(xn) * sinv)
            outs.append(_cat(ys))
        return outs + list(vs)

    ins = ([('t', proj, 512, CB_AQ + j) for j in range(6)] + [('t', cos, HEAD, 0), ('t', sin, HEAD, 0)]
           + [('f', a) for a in _gain_rows(qn, kn)] + [('t', proj, 512, CB_AV + g) for g in range(ATT_GROUPS)])
    return _ew(fn, ins, [('t', ATT_GW, BF16)] * 9, rows=proj.shape[0], tm=512, name=name)


def _qk_bwd(dqk, proj, cos, sin, qn, kn, name):
    def fn(*v):
        ds, xs, cosv, sinv, gains = v[:6], v[6:12], v[12], v[13], v[14:20]
        rows8 = lax.broadcasted_iota(jnp.int32, (8, HEAD), 0)
        outs, dgs = [], [jnp.zeros((8, HEAD), F32)] * 2
        for j in range(6):
            gain = gains[j]
            dx, dg = [], jnp.zeros((1, HEAD), F32)
            for dyh, xh in zip(_heads(ds[j]), _heads(xs[j])):
                r = _rms_rows(xh)
                xhat = xh * r
                dxn = dyh * cosv - _rot(dyh * sinv)
                dg = dg + jnp.sum(dxn * xhat, axis=0, keepdims=True)
                dxh = dxn * gain
                dx.append(r * (dxh - xhat * jnp.mean(dxh * xhat, axis=1, keepdims=True)))
            outs.append(_cat(dx))
            dgs[j // 3] = dgs[j // 3] + jnp.where(rows8 == j % 3, dg, 0.0)
        return _cat(outs), dgs[0], dgs[1]

    ins = ([('t', a, ATT_GW, 0) for a in dqk] + [('t', proj, 512, CB_AQ + j) for j in range(6)]
           + [('t', cos, HEAD, 0), ('t', sin, HEAD, 0)] + [('f', a) for a in _gain_rows(qn, kn)])
    return _ew(fn, ins, [('t', 6 * ATT_GW, BF16), ('acc', (8, HEAD)), ('acc', (8, HEAD))],
               rows=proj.shape[0], tm=256, name=name)


def _pick(x, h):
    lanes = lax.broadcasted_iota(jnp.int32, x.shape, 1)
    return jnp.sum(jnp.where(lanes == h, x, 0.0), axis=1, keepdims=True)


def _spread(x):
    return _cat([jnp.broadcast_to(_pick(x, h), (x.shape[0], HEAD)) for h in range(ATT_HEADS)])


def _compact(x):
    lanes = lax.broadcasted_iota(jnp.int32, (x.shape[0], HEAD), 1)
    out = jnp.zeros((x.shape[0], HEAD), F32)
    for h, xh in enumerate(_heads(x)):
        out = jnp.where(lanes == h, xh, out)
    return out


def _group_weights(l0, l1, l2):
    l0, l1, l2 = _spread(l0), _spread(l1), _spread(l2)
    m = jnp.maximum(jnp.maximum(l0, l1), l2)
    e0, e1, e2 = jnp.exp(l0 - m), jnp.exp(l1 - m), jnp.exp(l2 - m)
    inv = 1.0 / (e0 + e1 + e2)
    return e0 * inv, e1 * inv, e2 * inv


def _merge_fwd(outs, lses, name):
    def fn(o0, o1, o2, l0, l1, l2):
        a0, a1, a2 = _group_weights(l0, l1, l2)
        return a0 * o0 + a1 * o1 + a2 * o2

    ins = [('t', a, ATT_GW, 0) for a in outs] + [('t', a, HEAD, 0) for a in lses]
    return _ew(fn, ins, [('t', ATT_GW, BF16)], rows=outs[0].shape[0], tm=512, name=name)[0]


def _merge_bwd(dob, outs, lses, name):
    def fn(dov, o0, o1, o2, l0, l1, l2):
        a0, a1, a2 = _group_weights(l0, l1, l2)
        ob = a0 * o0 + a1 * o1 + a2 * o2
        s = _head_mean(dov * ob) * float(HEAD)
        return a0 * dov, a1 * dov, a2 * dov, _compact(a0 * s), _compact(a1 * s), _compact(a2 * s)

    ins = [('t', dob, ATT_GW, 0)] + [('t', a, ATT_GW, 0) for a in outs] + [('t', a, HEAD, 0) for a in lses]
    return _ew(fn, ins, [('t', ATT_GW, BF16)] * 3 + [('t', HEAD, F32)] * 3, rows=dob.shape[0], tm=512, name=name)


def _assemble_dproj(dh4, dqk, dvs, dgab, name):
    fn = lambda *v: _cat(list(v))
    ins = [('t', dh4, 4 * D_MODEL, 0), ('t', dqk, 6 * ATT_GW, 0)] + [('t', a, ATT_GW, 0) for a in dvs] + [('t', dgab, 2 * D_MODEL, 0)]
    return _ew(fn, ins, [('t', P_IN, BF16)], rows=dh4.shape[0], tm=256, name=name)[0]


HG_ROWS = 256


def _hg_gates(hq, hf, hi, lbv):
    sig = _sig(hf)
    f = lbv + (1.0 - lbv) * sig
    return hq * _sig(hq), 1.0 - f, hi, jnp.log(f), sig, f


def _split3(x):
    hi = _bf(x)
    r1 = x - hi.astype(F32)
    mid = _bf(r1)
    return hi, mid, _bf(r1 - mid.astype(F32))


def _tri_dot(tri, x):
    hi, mid, lo = _split3(x)
    return _dot(tri, hi) + _dot(tri, mid) + _dot(tri, lo)


def _row(x, i):
    rows = lax.broadcasted_iota(jnp.int32, x.shape, 0)
    return jnp.sum(jnp.where(rows == i, x, 0.0), axis=0, keepdims=True)


def _hg_decay(logf, q, k):
    c = HG_CHUNK
    row = lax.broadcasted_iota(jnp.int32, (c, c), 0)
    col = lax.broadcasted_iota(jnp.int32, (c, c), 1)
    g = _tri_dot((row >= col).astype(BF16), logf)
    gm = _row(g, c // 2 - 1)
    gl = _row(g, c - 1)
    return g, gm, gl, q * jnp.exp(g), q * jnp.exp(g - gm), k * jnp.exp(gm - g), k * jnp.exp(gl - g)


def _hg_out_fwd(o, hg, gain):
    r = lax.rsqrt(_head_mean(o * o) + EPS)
    return o * r * gain * (hg * _sig(hg))


def _hgrn_fwd(proj, lb, gain, name, rider=None):
    t = proj.shape[0]
    nck = HG_ROWS // HG_CHUNK

    def body(hq_ref, hf_ref, hi_ref, hg_ref, lb_ref, gn_ref, o_ref, oa_ref, sall_ref, st_ref):
        @pl.when(pl.program_id(0) == 0)
        def _():
            st_ref[...] = jnp.zeros_like(st_ref)

        lbv = lb_ref[...]
        gnv = gn_ref[...]
        c = HG_CHUNK
        mask = lax.broadcasted_iota(jnp.int32, (c, c), 0) >= lax.broadcasted_iota(jnp.int32, (c, c), 1)

        def chunk(cc, carry):
            sl = pl.ds(pl.multiple_of(cc * c, c), c)
            q, k, v, logf, _, _ = _hg_gates(hq_ref[sl, :], hf_ref[sl, :], hi_ref[sl, :], lbv)
            _, _, gl, qg, qt, kt, kd = _hg_decay(logf, q, k)
            egl = jnp.exp(gl)
            os = []
            for h in range(HG_HEADS):
                hs = slice(h * HEAD, (h + 1) * HEAD)
                st = st_ref[h]
                sall_ref[cc, h] = st
                a = jnp.where(mask, _dot_nt(_bf(qt[:, hs]), _bf(kt[:, hs])), 0.0)
                os.append(_dot(_bf(a), _bf(v[:, hs])) + _dot_nt(_bf(qg[:, hs]), _bf(st)))
                st_ref[h] = egl[:, hs] * st + _dot_tn(_bf(v[:, hs]), _bf(kd[:, hs]))
            o = _cat(os)
            o_ref[sl, :] = o
            oa_ref[sl, :] = _hg_out_fwd(o, hg_ref[sl, :], gnv).astype(oa_ref.dtype)
            return carry

        lax.fori_loop(0, nck, chunk, 0)

    col = lambda j: pl.BlockSpec((HG_ROWS, D_MODEL), lambda i, j=j: (i, j))
    small = pl.BlockSpec((1, D_MODEL), lambda i: (0, 0))
    return _pcall(
        body, grid=(t // HG_ROWS,),
        in_specs=[col(0), col(1), col(2), col(3), small, small],
        out_specs=[col(0), col(0), pl.BlockSpec((nck, HG_HEADS, HEAD, HEAD), lambda i: (i, 0, 0, 0))],
        out_shape=[jax.ShapeDtypeStruct((t, D_MODEL), F32), jax.ShapeDtypeStruct((t, D_MODEL), BF16),
                   jax.ShapeDtypeStruct((t // HG_CHUNK, HG_HEADS, HEAD, HEAD), F32)],
        scratch_shapes=[pltpu.VMEM((HG_HEADS, HEAD, HEAD), F32)],
        name=name, sem=("arbitrary",), args=(proj, proj, proj, proj, lb, gain), rider=rider)


def _terms(x, precise):
    hi = _bf(x)
    return (hi, _bf(x - hi.astype(F32))) if precise else (hi,)


def _mm(dot, a, b):
    out = dot(a[0], b[0])
    if len(a) > 1:
        out = out + dot(a[1], b[0])
    if len(b) > 1:
        out = out + dot(a[0], b[1])
    return out


def _hgrn_bwd(doa, oscan, proj, sall, lb, gain, name, precise, rider=None):
    t = proj.shape[0]
    nck = HG_ROWS // HG_CHUNK
    nsteps = t // HG_ROWS
    terms = functools.partial(_terms, precise=precise)

    def body(doa_ref, os_ref, hq_ref, hf_ref, hi_ref, hg_ref, sall_ref, lb_ref, gn_ref,
             d4_ref, dgn_ref, dlb_ref, dst_ref):
        @pl.when(pl.program_id(0) == 0)
        def _():
            dst_ref[...] = jnp.zeros_like(dst_ref)
            dgn_ref[...] = jnp.zeros_like(dgn_ref)
            dlb_ref[...] = jnp.zeros_like(dlb_ref)

        lbv = lb_ref[...]
        gnv = gn_ref[...]
        c = HG_CHUNK
        row = lax.broadcasted_iota(jnp.int32, (c, c), 0)
        colm = lax.broadcasted_iota(jnp.int32, (c, c), 1)
        mask = row >= colm
        triu = (row <= colm).astype(BF16)
        last = lax.broadcasted_iota(jnp.int32, (c, HEAD), 0) == c - 1

        def chunk(ci, carry):
            cc = nck - 1 - ci
            sl = pl.ds(pl.multiple_of(cc * c, c), c)
            hq, hf, hg = hq_ref[sl, :], hf_ref[sl, :], hg_ref[sl, :]
            q, k, v, logf, sig, f = _hg_gates(hq, hf, hi_ref[sl, :], lbv)
            g, gm, gl, qg, qt, kt, kd = _hg_decay(logf, q, k)
            egl = jnp.exp(gl)
            o = os_ref[sl, :]
            dy = doa_ref[sl, :]
            r = lax.rsqrt(_head_mean(o * o) + EPS)
            oh = o * r
            sg = _sig(hg)
            silu_g = hg * sg
            dgn_ref[...] += jnp.sum(dy * oh * silu_g, axis=0, keepdims=True)
            dhg = dy * oh * gnv * (sg * (1.0 + hg * (1.0 - sg)))
            doh = dy * gnv * silu_g
            do = r * (doh - oh * _head_mean(doh * oh))
            dqs, dks, dvs, dgs = [], [], [], []
            for h in range(HG_HEADS):
                hs = slice(h * HEAD, (h + 1) * HEAD)
                st = sall_ref[cc, h]
                dst = dst_ref[h]
                qt_h, kt_h, qg_h, kd_h = qt[:, hs], kt[:, hs], qg[:, hs], kd[:, hs]
                do_p, v_p, qt_p, kt_p, qg_p = terms(do[:, hs]), terms(v[:, hs]), terms(qt_h), terms(kt_h), terms(qg_h)
                st_p, dst_p = terms(st), terms(dst)
                a = jnp.where(mask, _dot_nt(qt_p[0], kt_p[0]), 0.0)
                da = terms(jnp.where(mask, _mm(_dot_nt, do_p, v_p), 0.0))
                dqt = _mm(_dot, da, kt_p)
                dkt = _mm(_dot_tn, da, qt_p)
                dqg = _mm(_dot, do_p, st_p)
                dv = _dot_tn(_bf(a), do_p[0]) + _dot_nt(_bf(kd_h), dst_p[0])
                dkd = _mm(_dot, v_p, dst_p)
                dgl = egl[:, hs] * jnp.sum(st * dst, axis=0, keepdims=True) + jnp.sum(dkd * kd_h, axis=0, keepdims=True)
                dst_ref[h] = egl[:, hs] * dst + _mm(_dot_tn, do_p, qg_p)
                g_h = g[:, hs]
                gm_h = gm[:, hs]
                gl_h = gl[:, hs]
                dqs.append(dqt * jnp.exp(g_h - gm_h) + dqg * jnp.exp(g_h))
                dks.append(dkt * jnp.exp(gm_h - g_h) + dkd * jnp.exp(gl_h - g_h))
                dvs.append(dv)
                dgs.append(dqt * qt_h - dkt * kt_h + dqg * qg_h - dkd * kd_h + jnp.where(last, dgl, 0.0))
            dq, dk, dv, dg = _cat(dqs), _cat(dks), _cat(dvs), _cat(dgs)
            dlogf = _tri_dot(triu, dg)
            df = dlogf / f - dk
            dlb_ref[...] += jnp.sum(df * (1.0 - sig), axis=0, keepdims=True)
            dhf = df * (1.0 - lbv) * sig * (1.0 - sig)
            sq = _sig(hq)
            dhq = dq * (sq * (1.0 + hq * (1.0 - sq)))
            d4_ref[sl, :] = _cat([dhq, dhf, dv, dhg]).astype(d4_ref.dtype)
            return carry

        lax.fori_loop(0, nck, chunk, 0)

    rev = lambda j: pl.BlockSpec((HG_ROWS, D_MODEL), lambda i, j=j: (nsteps - 1 - i, j))
    small = pl.BlockSpec((1, D_MODEL), lambda i: (0, 0))
    return _pcall(
        body, grid=(nsteps,),
        in_specs=[rev(0), rev(0), rev(0), rev(1), rev(2), rev(3),
                  pl.BlockSpec((nck, HG_HEADS, HEAD, HEAD), lambda i: (nsteps - 1 - i, 0, 0, 0)), small, small],
        out_specs=[pl.BlockSpec((HG_ROWS, 4 * D_MODEL), lambda i: (nsteps - 1 - i, 0)), small, small],
        out_shape=[jax.ShapeDtypeStruct((t, 4 * D_MODEL), BF16), jax.ShapeDtypeStruct((1, D_MODEL), F32),
                   jax.ShapeDtypeStruct((1, D_MODEL), F32)],
        scratch_shapes=[pltpu.VMEM((HG_HEADS, HEAD, HEAD), F32)],
        name=name, sem=("arbitrary",), args=(doa, oscan, proj, proj, proj, proj, sall, lb, gain), rider=rider)


def _band_masks():
    qi = lax.broadcasted_iota(jnp.int32, (ATT_BLK, ATT_BLK), 0)
    ki = lax.broadcasted_iota(jnp.int32, (ATT_BLK, ATT_BLK), 1)
    return ki >= qi, ki <= qi


def _attn_cfg(t, g):
    d = DILATIONS[g]
    length = t // d
    nb = length // ATT_BLK
    return d, length, nb, min(ATT_STEP_BLOCKS, nb)


def _attn_fwd(qg, kg, vg, g, name):
    t = qg.shape[0]
    d, length, nb, rb = _attn_cfg(t, g)
    scale = HEAD ** -0.5

    def body(q_ref, k_ref, v_ref, kp_ref, vp_ref, o_ref, l_ref):
        n = pl.program_id(1)
        prev_m, own_m = _band_masks()
        first_m = jnp.logical_and(prev_m, n > 0)
        lanes = lax.broadcasted_iota(jnp.int32, (ATT_BLK, HEAD), 1)
        for j in range(rb):
            rows = slice(j * ATT_BLK, (j + 1) * ATT_BLK)
            before = slice((j - 1) * ATT_BLK, j * ATT_BLK)
            lse = jnp.zeros((ATT_BLK, HEAD), F32)
            for h in range(ATT_HEADS):
                hs = slice(h * HEAD, (h + 1) * HEAD)
                q = q_ref[rows, hs]
                k0, v0, m0 = (kp_ref[:, hs], vp_ref[:, hs], first_m) if j == 0 else (k_ref[before, hs], v_ref[before, hs], prev_m)
                s0 = jnp.where(m0, _dot_nt(q, k0) * scale, NEG)
                s1 = jnp.where(own_m, _dot_nt(q, k_ref[rows, hs]) * scale, NEG)
                m = jnp.maximum(jnp.max(s0, axis=1, keepdims=True), jnp.max(s1, axis=1, keepdims=True))
                p0, p1 = jnp.exp(s0 - m), jnp.exp(s1 - m)
                l = jnp.sum(p0, axis=1, keepdims=True) + jnp.sum(p1, axis=1, keepdims=True)
                o = _dot(_bf(p0), v0) + _dot(_bf(p1), v_ref[rows, hs])
                o_ref[rows, hs] = (o / l).astype(o_ref.dtype)
                lse = jnp.where(lanes == h, m + jnp.log(l), lse)
            l_ref[rows, :] = lse

    own = pl.BlockSpec((rb * ATT_BLK, ATT_GW), lambda r, n: (n, r))
    own_head = pl.BlockSpec((rb * ATT_BLK, HEAD), lambda r, n: (n, r))
    prev = pl.BlockSpec((ATT_BLK, ATT_GW), lambda r, n: (jnp.maximum(n * rb - 1, 0), r))
    view = lambda a: a.reshape(length, d * a.shape[1])
    o, lse = pl.pallas_call(
        body, grid=(d, nb // rb), in_specs=[own, own, own, prev, prev], out_specs=[own, own_head],
        out_shape=[jax.ShapeDtypeStruct((length, d * ATT_GW), BF16), jax.ShapeDtypeStruct((length, d * HEAD), F32)],
        name=name, compiler_params=_params(("parallel", "arbitrary")))(view(qg), view(kg), view(vg), view(kg), view(vg))
    return o.reshape(t, ATT_GW), lse.reshape(t, HEAD)


def _attn_bwd(qg, kg, vg, dog, lse, delta, g, name):
    t = qg.shape[0]
    d, length, nb, rb = _attn_cfg(t, g)
    nsteps = nb // rb
    scale = HEAD ** -0.5

    def body(q_ref, k_ref, v_ref, do_ref, l_ref, dl_ref, kp_ref, vp_ref, qn_ref, don_ref, ln_ref, dln_ref,
             dq_ref, dk_ref, dv_ref):
        n = pl.program_id(1)
        prev_m, own_m = _band_masks()
        first_m = jnp.logical_and(prev_m, n > 0)
        next_m = jnp.logical_and(prev_m, n < nsteps - 1)
        for h in range(ATT_HEADS):
            hs = slice(h * HEAD, (h + 1) * HEAD)
            dk, dv = [None] * rb, [None] * rb
            for j in range(rb + 1):
                rows = slice(j * ATT_BLK, (j + 1) * ATT_BLK)
                before = slice((j - 1) * ATT_BLK, j * ATT_BLK)
                if j < rb:
                    q, do, lse_q, dl_q = q_ref[rows, hs], do_ref[rows, hs], _pick(l_ref[rows, :], h), _pick(dl_ref[rows, :], h)
                else:
                    q, do, lse_q, dl_q = qn_ref[:, hs], don_ref[:, hs], _pick(ln_ref[...], h), _pick(dln_ref[...], h)
                if j == 0:
                    k0, v0, m0 = kp_ref[:, hs], vp_ref[:, hs], first_m
                else:
                    k0, v0, m0 = k_ref[before, hs], v_ref[before, hs], (prev_m if j < rb else next_m)
                p0 = jnp.where(m0, jnp.exp(_dot_nt(q, k0) * scale - lse_q), 0.0)
                ds0 = _bf(p0 * (_dot_nt(do, v0) - dl_q) * scale)
                if j >= 1:
                    dk[j - 1] = dk[j - 1] + _dot_tn(ds0, q)
                    dv[j - 1] = dv[j - 1] + _dot_tn(_bf(p0), do)
                if j < rb:
                    k1, v1 = k_ref[rows, hs], v_ref[rows, hs]
                    p1 = jnp.where(own_m, jnp.exp(_dot_nt(q, k1) * scale - lse_q), 0.0)
                    ds1 = _bf(p1 * (_dot_nt(do, v1) - dl_q) * scale)
                    dq_ref[rows, hs] = (_dot(ds0, k0) + _dot(ds1, k1)).astype(dq_ref.dtype)
                    dk[j] = _dot_tn(ds1, q)
                    dv[j] = _dot_tn(_bf(p1), do)
            for j in range(rb):
                rows = slice(j * ATT_BLK, (j + 1) * ATT_BLK)
                dk_ref[rows, hs] = dk[j].astype(dk_ref.dtype)
                dv_ref[rows, hs] = dv[j].astype(dv_ref.dtype)

    own = pl.BlockSpec((rb * ATT_BLK, ATT_GW), lambda r, n: (n, r))
    prev = pl.BlockSpec((ATT_BLK, ATT_GW), lambda r, n: (jnp.maximum(n * rb - 1, 0), r))
    nxt = pl.BlockSpec((ATT_BLK, ATT_GW), lambda r, n: (jnp.minimum((n + 1) * rb, nb - 1), r))
    own_head = pl.BlockSpec((rb * ATT_BLK, HEAD), lambda r, n: (n, r))
    nxt_head = pl.BlockSpec((ATT_BLK, HEAD), lambda r, n: (jnp.minimum((n + 1) * rb, nb - 1), r))
    view = lambda a: a.reshape(length, d * a.shape[1])
    dq, dk, dv = pl.pallas_call(
        body, grid=(d, nsteps), in_specs=[own] * 4 + [own_head] * 2 + [prev, prev, nxt, nxt, nxt_head, nxt_head],
        out_specs=[own, own, own], out_shape=[jax.ShapeDtypeStruct((length, d * ATT_GW), BF16)] * 3,
        name=name, compiler_params=_params(("parallel", "arbitrary")))(
            view(qg), view(kg), view(vg), view(dog), view(lse), view(delta), view(kg), view(vg),
            view(qg), view(dog), view(lse), view(delta))
    return dq.reshape(t, ATT_GW), dk.reshape(t, ATT_GW), dv.reshape(t, ATT_GW)


def _rope_tables(t):
    pos = jnp.arange(t, dtype=F32)
    inv = ROPE_THETA ** (-jnp.arange(0, HEAD, 2, dtype=F32) / HEAD)
    ang = pos[:, None] * inv[None, :]
    ang = jnp.concatenate([ang, ang], axis=-1)
    return jnp.cos(ang), jnp.sin(ang)


def _lower_bounds(logits):
    lb = jnp.cumsum(jax.nn.softmax(logits.astype(F32), axis=0), axis=0)
    return lb - lb[0:1]


FFN_ROWS = 256
FF_SHARD = 2 * D_FF // N_CHIPS


def _ffn_in_act(x, g, w_in, name, rider=None):
    t = x.shape[0]

    def body(x_ref, g_ref, w_ref, h_ref, ab_ref, u_ref):
        xv = x_ref[...]
        h = _bf(xv * _rms_rows(xv) * g_ref[...])
        h_ref[...] = h
        for s in range(N_CHIPS // 2):
            cols = slice(s * FF_SHARD, (s + 1) * FF_SHARD)
            a = _dot(h, w_ref[s])
            b = _dot(h, w_ref[s + N_CHIPS // 2])
            ab_ref[:, cols] = a.astype(ab_ref.dtype)
            ab_ref[:, D_FF + s * FF_SHARD:D_FF + (s + 1) * FF_SHARD] = b.astype(ab_ref.dtype)
            u_ref[:, cols] = (a * _sig(a) * b).astype(u_ref.dtype)

    row = lambda w: pl.BlockSpec((FFN_ROWS, w), lambda i: (i, 0))
    return _pcall(
        body, grid=(t // FFN_ROWS,),
        in_specs=[row(D_MODEL), pl.BlockSpec((1, D_MODEL), lambda i: (0, 0)),
                  pl.BlockSpec(w_in.shape, lambda i: (0, 0, 0))],
        out_specs=[row(D_MODEL), row(2 * D_FF), row(D_FF)],
        out_shape=[jax.ShapeDtypeStruct((t, D_MODEL), BF16), jax.ShapeDtypeStruct((t, 2 * D_FF), BF16),
                   jax.ShapeDtypeStruct((t, D_FF), BF16)],
        name=name, sem=("parallel",), args=(x, g, w_in), rider=rider)


def _ffn_bwd_du_act(dx, w_out, ab, name, rider=None):
    t = dx.shape[0]

    def body(dx_ref, w_ref, ab_ref, o_ref):
        du = 0.5 * _dot_nt(_bf(dx_ref[...]), w_ref[0])
        a = ab_ref[:, :D_FF].astype(F32)
        b = ab_ref[:, D_FF:].astype(F32)
        s = _sig(a)
        o_ref[:, :D_FF] = (du * b * (s * (1.0 + a * (1.0 - s)))).astype(o_ref.dtype)
        o_ref[:, D_FF:] = (du * a * s).astype(o_ref.dtype)

    row = lambda w: pl.BlockSpec((FFN_ROWS, w), lambda i: (i, 0))
    return _pcall(
        body, grid=(t // FFN_ROWS,),
        in_specs=[row(D_MODEL), pl.BlockSpec(w_out.shape, lambda i: (0, 0, 0)), row(2 * D_FF)],
        out_specs=row(2 * D_FF), out_shape=jax.ShapeDtypeStruct((t, 2 * D_FF), BF16),
        name=name, sem=("parallel",), args=(dx, w_out, ab), rider=rider)


MIX_ROWS = 512


def _gate_specs():
    return [pl.BlockSpec((MIX_ROWS, 512), lambda i, cb=cb: (i, cb)) for cb in (CB_GA, CB_GA + 1, CB_GB, CB_GB + 1)]


def _whole(a):
    return pl.BlockSpec(a.shape, lambda i: (0,) * a.ndim)


def _mix_tail_fwd(oa, ob, proj, x, w_a, w_b, w_o, name):
    t = x.shape[0]

    def body(oa_ref, ob_ref, ga0, ga1, gb0, gb1, x_ref, wa_ref, wb_ref, wo_ref, y_ref, m_ref, ya_ref, yb_ref):
        ya = _dot(oa_ref[...], wa_ref[0])
        yb = _cat([_dot(ob_ref[...], wb_ref[s]) for s in range(N_CHIPS)])
        merged = _bf(_sig(_cat([ga0[...], ga1[...]])) * ya + _sig(_cat([gb0[...], gb1[...]])) * yb)
        m_ref[...] = merged
        ya_ref[...] = ya.astype(ya_ref.dtype)
        yb_ref[...] = yb.astype(yb_ref.dtype)
        y_ref[...] = x_ref[...] + _dot(merged, wo_ref[0])

    row = lambda w: pl.BlockSpec((MIX_ROWS, w), lambda i: (i, 0))
    return pl.pallas_call(
        body, grid=(t // MIX_ROWS,),
        in_specs=[row(D_MODEL), row(ATT_GW)] + _gate_specs() + [row(D_MODEL), _whole(w_a), _whole(w_b), _whole(w_o)],
        out_specs=[row(D_MODEL)] * 4,
        out_shape=[jax.ShapeDtypeStruct((t, D_MODEL), F32)] + [jax.ShapeDtypeStruct((t, D_MODEL), BF16)] * 3,
        name=name, compiler_params=_params(("parallel",)))(oa, ob, proj, proj, proj, proj, x, w_a, w_b, w_o)


def _mix_tail_bwd(dx, proj, ya, yb, w_a, w_b, w_o, name):
    t = dx.shape[0]
    shard = D_MODEL // N_CHIPS

    def body(dx_ref, ga0, ga1, gb0, gb1, ya_ref, yb_ref, wa_ref, wb_ref, wo_ref, dya_ref, dyb_ref, dg_ref, doa_ref, dob_ref):
        dm = _dot_nt(_bf(dx_ref[...]), wo_ref[0])
        sa = _sig(_cat([ga0[...], ga1[...]]))
        sb = _sig(_cat([gb0[...], gb1[...]]))
        dya, dyb = _bf(dm * sa), _bf(dm * sb)
        dya_ref[...] = dya
        dyb_ref[...] = dyb
        dg_ref[:, :D_MODEL] = (dm * ya_ref[...].astype(F32) * sa * (1.0 - sa)).astype(dg_ref.dtype)
        dg_ref[:, D_MODEL:] = (dm * yb_ref[...].astype(F32) * sb * (1.0 - sb)).astype(dg_ref.dtype)
        doa_ref[...] = _dot_nt(dya, wa_ref[0])
        dob = _dot_nt(dyb[:, :shard], wb_ref[0])
        for s in range(1, N_CHIPS):
            dob = dob + _dot_nt(dyb[:, s * shard:(s + 1) * shard], wb_ref[s])
        dob_ref[...] = dob

    row = lambda w: pl.BlockSpec((MIX_ROWS, w), lambda i: (i, 0))
    return pl.pallas_call(
        body, grid=(t // MIX_ROWS,),
        in_specs=[row(D_MODEL)] + _gate_specs() + [row(D_MODEL), row(D_MODEL), _whole(w_a), _whole(w_b), _whole(w_o)],
        out_specs=[row(D_MODEL), row(D_MODEL), row(2 * D_MODEL), row(D_MODEL), row(ATT_GW)],
        out_shape=[jax.ShapeDtypeStruct((t, D_MODEL), BF16), jax.ShapeDtypeStruct((t, D_MODEL), BF16),
                   jax.ShapeDtypeStruct((t, 2 * D_MODEL), BF16), jax.ShapeDtypeStruct((t, D_MODEL), F32),
                   jax.ShapeDtypeStruct((t, ATT_GW), F32)],
        name=name, compiler_params=_params(("parallel",)))(dx, proj, proj, proj, proj, ya, yb, w_a, w_b, w_o)


def _ffn_fwd(x, g, src, l, pre):
    tag = f"l{l}_{pre}"
    w_in = src.weight(l, pre + "_w_in")
    h, ab, u = _ffn_in_act(x, g, w_in, name=tag + "_in_act", rider=src.ride(tag + "_in_act"))
    w_out = src.weight(l, pre + "_w_out")
    y = _mm_nn(u, w_out, name=tag + "_out", tm=512, tn=D_MODEL, out_dtype=F32, res=x, alpha=0.5, rider=src.ride(tag + "_out"))
    return y, (x, h, ab, u, w_in, w_out)


def _ffn_bwd(dx, saved, g, src, l, pre):
    tag = f"l{l}_{pre}"
    x, h, ab, u, w_in, w_out = saved
    g_out = _mm_tn(u, dx, nb=1, name=tag + "_bwd_wout", tm=1024, tk=1408, tn=D_MODEL, alpha=0.5, rider=src.ride(tag + "_bwd_wout"))
    src.grads(l, {pre + "_w_out": g_out.reshape(N_CHIPS, D_FF // N_CHIPS, D_MODEL)})
    dab = _ffn_bwd_du_act(dx, w_out, ab, name=tag + "_bwd_du_act", rider=src.ride(tag + "_bwd_du_act"))
    g_in = _mm_tn(h, dab, nb=N_CHIPS, name=tag + "_bwd_win", tm=2048, tk=D_MODEL, tn=FF_SHARD, rider=src.ride(tag + "_bwd_win"))
    src.grads(l, {pre + "_w_in": g_in})
    dh = _mm_nt(dab, w_in, name=tag + "_bwd_dh", tm=1024, tp=D_MODEL, tn=FF_SHARD, out_dtype=F32, rider=src.ride(tag + "_bwd_dh"))
    return _norm_bwd(dh, x, g, dx, name=tag + "_bwd_norm")


def _mix_fwd(x, small, lb, cos, sin, src, l):
    tag = f"l{l}_mix"
    w = {}
    h = _norm_fwd(x, small["mix_norm"], name=tag + "_norm")
    w["w_in"] = src.weight(l, "w_in")
    proj = _mm_nn(h, w["w_in"], name=tag + "_in", tm=1024, tn=896, out_dtype=F32, rider=src.ride(tag + "_in"))
    oscan, oa, sall = _hgrn_fwd(proj, lb, small["hgrn_out_norm"], name=tag + "_hgrn", rider=src.ride(tag + "_hgrn"))
    qk = _qk_fwd(proj, cos, sin, small["attn_q_norm"], small["attn_k_norm"], name=tag + "_qk")
    outs, lses = [], []
    for g in range(ATT_GROUPS):
        o, lse = _attn_fwd(qk[g], qk[3 + g], qk[6 + g], g, name=f"{tag}_attn{g}")
        outs.append(o)
        lses.append(lse)
    ob = _merge_fwd(outs, lses, name=tag + "_merge")
    w.update({n: src.weight(l, n) for n in ("w_branch_a", "w_branch_b", "w_out")})
    y, merged, ya, yb = _mix_tail_fwd(oa, ob, proj, x, w["w_branch_a"], w["w_branch_b"], w["w_out"], name=tag + "_tail")
    return y, (x, h, proj, oscan, oa, sall, qk, outs, lses, ob, ya, yb, merged, w)


def _mix_bwd(dx, saved, small, lb, cos, sin, src, l, lb_live):
    tag = f"l{l}_mix"
    x, h, proj, oscan, oa, sall, qk, outs, lses, ob, ya, yb, merged, w = saved
    g_wout = _mm_tn(merged, dx, nb=1, name=tag + "_bwd_wout", tm=1024, tk=D_MODEL, tn=D_MODEL)
    dya, dyb, dgab, doa, dob = _mix_tail_bwd(dx, proj, ya, yb, w["w_branch_a"], w["w_branch_b"], w["w_out"], name=tag + "_bwd_tail")
    g_wa = _mm_tn(oa, dya, nb=1, name=tag + "_bwd_wa", tm=1024, tk=D_MODEL, tn=D_MODEL)
    g_wb = _mm_tn(ob, dyb, nb=N_CHIPS, name=tag + "_bwd_wb", tm=2048, tk=ATT_GW, tn=256)
    mb = _merge_bwd(dob, outs, lses, name=tag + "_bwd_merge")
    dqk, dvs = [None] * 6, []
    for g in range(ATT_GROUPS):
        dq, dk, dv = _attn_bwd(qk[g], qk[3 + g], qk[6 + g], mb[g], lses[g], mb[3 + g], g, name=f"{tag}_bwd_attn{g}")
        dqk[g], dqk[3 + g] = dq, dk
        dvs.append(dv)
    dqk_cols, dqn, dkn = _qk_bwd(dqk, proj, cos, sin, small["attn_q_norm"], small["attn_k_norm"], name=tag + "_bwd_qk")
    dh4, dgn, dlb = _hgrn_bwd(doa, oscan, proj, sall, lb, small["hgrn_out_norm"], name=tag + "_bwd_hgrn", precise=lb_live,
                              rider=src.ride(tag + "_bwd_hgrn"))
    dproj = _assemble_dproj(dh4, dqk_cols, dvs, dgab, name=tag + "_bwd_cat")
    src.grads(l, dict(w_branch_a=g_wa.reshape(N_CHIPS, D_MODEL // N_CHIPS, D_MODEL), w_branch_b=g_wb,
                      w_out=g_wout.reshape(N_CHIPS, D_MODEL // N_CHIPS, D_MODEL)))
    g_win = _mm_tn(h, dproj, nb=N_CHIPS, name=tag + "_bwd_win", tm=2048, tk=D_MODEL, tn=896, rider=src.ride(tag + "_bwd_win"))
    src.grads(l, dict(w_in=g_win))
    dh = _mm_nt(dproj, w["w_in"], name=tag + "_bwd_dh", tm=1024, tp=D_MODEL, tn=2688, out_dtype=F32, rider=src.ride(tag + "_bwd_dh"))
    dx, dg = _norm_bwd(dh, x, small["mix_norm"], dx, name=tag + "_bwd_norm")
    return dx, dict(mix_norm=dg, hgrn_out_norm=dgn, lb=dlb, attn_q_norm=dqn, attn_k_norm=dkn)


BIG = ("ffn1_w_in", "ffn1_w_out", "w_in", "w_branch_a", "w_branch_b", "w_out", "ffn2_w_in", "ffn2_w_out")
ROW_SHARDED = ("ffn1_w_out", "w_branch_a", "w_out", "ffn2_w_out")
SMALL = ("ffn1_norm", "mix_norm", "hgrn_lb_logits", "hgrn_out_norm", "attn_q_norm", "attn_k_norm", "ffn2_norm")
WEIGHTS = ("ffn1_norm", "ffn1_w_in", "ffn1_w_out", "mix_norm", "w_in", "hgrn_lb_logits", "hgrn_out_norm", "attn_q_norm",
           "attn_k_norm", "w_branch_a", "w_branch_b", "w_out", "ffn2_norm", "ffn2_w_in", "ffn2_w_out")
SMALL_ROWS = 8


def _matmul_ready(name, a):
    return a.reshape(1, a.shape[0] * a.shape[1], a.shape[2]) if name in ROW_SHARDED else a


def _layer_small(small, l):
    s = {n: small[n][l].reshape(1, D_MODEL) for n in ("ffn1_norm", "mix_norm", "hgrn_out_norm", "ffn2_norm")}
    s.update({n: small[n][l] for n in ("attn_q_norm", "attn_k_norm")})
    return s


def _local_step(x, target, small, src):
    t = x.shape[0]
    cos, sin = _rope_tables(t)
    lbs = _lower_bounds(small["hgrn_lb_logits"])
    saved = []
    for l in range(2):
        sm = _layer_small(small, l)
        lb = lbs[l].reshape(1, D_MODEL)
        x, s1 = _ffn_fwd(x, sm["ffn1_norm"], src, l, "ffn1")
        x, s2 = _mix_fwd(x, sm, lb, cos, sin, src, l)
        x, s3 = _ffn_fwd(x, sm["ffn2_norm"], src, l, "ffn2")
        saved.append((sm, lb, s1, s2, s3))
    dx, sq = _loss_fwd_bwd(x, target, name="loss")
    small_rows = [None, None]
    for l in (1, 0):
        sm, lb, s1, s2, s3 = saved[l]
        dx, dg2 = _ffn_bwd(dx, s3, sm["ffn2_norm"], src, l, "ffn2")
        dx, g = _mix_bwd(dx, s2, sm, lb, cos, sin, src, l, lb_live=l > 0)
        dx, dg1 = _ffn_bwd(dx, s1, sm["ffn1_norm"], src, l, "ffn1")
        pad = lambda a: jnp.pad(a[:ATT_GROUPS].reshape(1, ATT_GROUPS * HEAD), ((0, 0), (0, D_MODEL - ATT_GROUPS * HEAD)))
        small_rows[l] = jnp.concatenate(
            [dg1, g["mix_norm"], g["lb"], g["hgrn_out_norm"], pad(g["attn_q_norm"]), pad(g["attn_k_norm"]), dg2,
             jnp.zeros((SMALL_ROWS - 7, D_MODEL), F32)], axis=0)
    return jnp.sum(sq), dx, jnp.concatenate(small_rows, axis=0)


def _coords():
    return lax.axis_index("x"), lax.axis_index("y"), lax.axis_index("c")


def _other_chips(x, y):
    return [(1 - x, y), (x, 1 - y), (1 - x, 1 - y)]


def _half_rows(rows, which):
    return pl.ds(which * (rows // 2), rows // 2)


def _gather_rider(shards):
    n = len(shards)

    def copies(w, full, sems):
        send, recv, fsend, frecv = sems
        x, y, c = _coords()
        slot = 2 * x + y
        chips = _other_chips(x, y)

        def copy(i, j, blk, src, pair, to):
            return pltpu.make_async_remote_copy(src_ref=src, dst_ref=blk, send_sem=pair[0].at[i * 3 + j],
                                                recv_sem=pair[1].at[i * 3 + j], device_id=to, device_id_type=MESH)

        def block(i, chip_slot, core):
            return full[i].at[chip_slot, _half_rows(shards[i].shape[0], core)]

        pairs = [(i, j, chip) for i in range(n) for j, chip in enumerate(chips)]

        def first():
            return [copy(i, j, block(i, slot, c), w[i].at[_half_rows(shards[i].shape[0], c)], (send, recv), (*chip, c))
                    for i, j, chip in pairs]

        def landed(core, pair):
            return [copy(i, j, block(i, 2 * chip[0] + chip[1], core), block(i, 2 * chip[0] + chip[1], core), pair, (x, y, 1 - c))
                    for i, j, chip in pairs]

        return first, landed

    def begin(w, full, sems):
        for cp in copies(w, full, sems)[0]():
            cp.start()

    def end(w, full, sems):
        first, landed = copies(w, full, sems)
        forwards = landed(lax.axis_index("c"), sems[2:])
        for arrival, forward in zip(landed(lax.axis_index("c"), sems[:2]), forwards):
            arrival.wait_recv()
            forward.start()
        for cp in landed(1 - lax.axis_index("c"), sems[2:]):
            cp.wait_recv()
        for cp in first() + forwards:
            cp.wait_send()

    out_shape = [jax.ShapeDtypeStruct((N_CHIPS,) + s.shape, s.dtype) for s in shards]
    return _Rider(shards, out_shape, [pltpu.SemaphoreType.DMA((3 * n,))] * 4, begin, end)


N_RECV = 7


def _scatter_rider(parts):
    n = len(parts)

    def copies(p, out, sems):
        send, recv = sems
        x, y, c = _coords()
        slot = 2 * x + y
        chips = _other_chips(x, y)

        def arrivals():
            return [pltpu.make_async_remote_copy(
                src_ref=out[i].at[k], dst_ref=out[i].at[k], send_sem=send.at[0], recv_sem=recv.at[i * N_RECV + k],
                device_id=(x, y, c), device_id_type=MESH) for i in range(n) for k in range(N_RECV)]

        sends = []
        for i in range(n):
            rows = parts[i].shape[1]
            for j, chip in enumerate(chips):
                for core in (0, 1):
                    sends.append(pltpu.make_async_remote_copy(
                        src_ref=p[i].at[2 * chip[0] + chip[1], _half_rows(rows, core)], dst_ref=out[i].at[2 * j + c],
                        send_sem=send.at[i * N_RECV + 2 * j + core], recv_sem=recv.at[i * N_RECV + 2 * j + c],
                        device_id=(*chip, core), device_id_type=MESH))
            sends.append(pltpu.make_async_remote_copy(
                src_ref=p[i].at[slot, _half_rows(rows, 1 - c)], dst_ref=out[i].at[6], send_sem=send.at[i * N_RECV + 6],
                recv_sem=recv.at[i * N_RECV + 6], device_id=(x, y, 1 - c), device_id_type=MESH))
        return sends, arrivals

    def begin(p, out, sems):
        for cp in copies(p, out, sems)[0]:
            cp.start()

    def end(p, out, sems):
        sends, arrivals = copies(p, out, sems)
        for cp in arrivals():
            cp.wait_recv()
        for cp in sends:
            cp.wait_send()

    out_shape = [jax.ShapeDtypeStruct((N_RECV, a.shape[1] // 2, a.shape[2]), a.dtype) for a in parts]
    return _Rider(parts, out_shape, [pltpu.SemaphoreType.DMA((N_RECV * n,))] * 2, begin, end)


def _run_alone(rider, name):
    _pcall(lambda: None, grid=(), in_specs=[], out_specs=[], out_shape=[], name=name, sem=(), args=(), rider=rider)
    return rider.result


def _sum_partials(own, parts, name):
    r, wd = own.shape
    tm = next(t for t in (256, 128, 64, 32, 16) if r % t == 0)

    def body(own_ref, p_ref, o_ref):
        acc = own_ref[...].astype(F32)
        for k in range(N_RECV):
            acc = acc + p_ref[k].astype(F32)
        o_ref[...] = acc

    return pl.pallas_call(
        body, grid=(r // tm,),
        in_specs=[pl.BlockSpec((tm, wd), lambda i: (i, 0)), pl.BlockSpec((N_RECV, tm, wd), lambda i: (0, i, 0))],
        out_specs=pl.BlockSpec((tm, wd), lambda i: (i, 0)), out_shape=jax.ShapeDtypeStruct((r, wd), F32),
        name=name, compiler_params=_params(("parallel",)))(own, parts)


def _exchange_halves(reduced, name):
    n = len(reduced)

    def body(*refs):
        r, out = refs[:n], refs[n:2 * n]
        send, recv = refs[2 * n:]
        x, y, c = _coords()
        sib = [pltpu.make_async_remote_copy(src_ref=r[i], dst_ref=out[i], send_sem=send.at[i], recv_sem=recv.at[i],
                                            device_id=(x, y, 1 - c), device_id_type=MESH) for i in range(n)]
        for cp in sib:
            cp.start()
        for cp in sib:
            cp.wait_recv()
        for cp in sib:
            cp.wait_send()

    out_shape = [jax.ShapeDtypeStruct(a.shape, a.dtype) for a in reduced]
    return pl.pallas_call(body, in_specs=[ANY] * n, out_specs=[ANY] * n, out_shape=out_shape,
                          scratch_shapes=[pltpu.SemaphoreType.DMA((n,))] * 2, name=name)(*reduced)


def _reduce_finish(parts, recv, tag):
    x, y, c = _coords()
    slot = 2 * x + y
    halves = []
    for i, (p, r) in enumerate(zip(parts, recv)):
        half = p.shape[1] // 2
        own = lax.dynamic_slice(p, (slot, c * half, 0), (1, half, p.shape[2]))[0]
        halves.append(_sum_partials(own, r, name=f"{tag}_sum{i}"))
    theirs = _exchange_halves(halves, name=tag + "_exchange")
    return [jnp.where(c == 0, jnp.concatenate([h, t], axis=0), jnp.concatenate([t, h], axis=0)) for h, t in zip(halves, theirs)]


GATHER_RIDES = {
    "l0_ffn1_in_act": ((0, "w_in"),),
    "l0_ffn1_out": ((0, "w_branch_a"), (0, "w_branch_b"), (0, "w_out")),
    "l0_mix_in": ((0, "ffn2_w_in"), (0, "ffn2_w_out"), (1, "ffn1_w_in"), (1, "ffn1_w_out")),
    "l0_mix_hgrn": ((1, "w_in"), (1, "w_branch_a"), (1, "w_branch_b"), (1, "w_out")),
    "l0_ffn2_in_act": ((1, "ffn2_w_in"), (1, "ffn2_w_out")),
}
ALONE_FIRST = ((0, "ffn1_w_in"), (0, "ffn1_w_out"))
SCATTER_RIDES = {
    "l1_mix_bwd_hgrn": ((1, "ffn2_w_in"), (1, "ffn2_w_out")),
    "l0_ffn2_bwd_win": ((1, "ffn1_w_in"),),
    "l0_ffn2_bwd_dh": ((1, "ffn1_w_out"), (1, "w_branch_a"), (1, "w_branch_b"), (1, "w_out")),
    "l0_mix_bwd_hgrn": ((1, "w_in"), (0, "ffn2_w_out")),
    "l0_mix_bwd_win": ((0, "ffn2_w_in"),),
    "l0_mix_bwd_dh": ((0, "w_in"),),
    "l0_ffn1_bwd_wout": ((0, "w_branch_a"), (0, "w_branch_b"), (0, "w_out")),
    "l0_ffn1_bwd_du_act": ((0, "ffn1_w_out"),),
    "l0_ffn1_bwd_dh": ((0, "ffn1_w_in"),),
}


class _Exchange:
    def __init__(self, shards):
        self.shards = shards
        self.pending = []
        self.full = {}
        self.parts = {}
        self.recv = {}

    def _gather(self, keys):
        return _gather_rider([self.shards[n][l] for l, n in keys]), "gather", list(keys)

    def _scatter(self, keys):
        return _scatter_rider([self.parts[k] for k in keys]), "scatter", list(keys)

    def _unpack(self):
        slot = 2 * lax.axis_index("x") + lax.axis_index("y")
        waiting = []
        for rider, kind, keys in self.pending:
            if rider.result is None:
                waiting.append((rider, kind, keys))
            elif kind == "gather":
                for (l, n), got in zip(keys, rider.result):
                    self.full[(l, n)] = lax.dynamic_update_slice(got, self.shards[n][l][None], (slot, 0, 0))
            else:
                self.recv.update(zip(keys, rider.result))
        self.pending = waiting

    def ride(self, host):
        if host in GATHER_RIDES:
            self.pending.append(self._gather(GATHER_RIDES[host]))
        elif host in SCATTER_RIDES:
            self.pending.append(self._scatter(SCATTER_RIDES[host]))
        else:
            return None
        return self.pending[-1][0]

    def weight(self, l, name):
        self._unpack()
        if (l, name) not in self.full:
            assert (l, name) in ALONE_FIRST, (l, name)
            job = self._gather(ALONE_FIRST)
            _run_alone(job[0], name="gather_first")
            self.pending.append(job)
            self._unpack()
        return _matmul_ready(name, self.full[(l, name)])

    def grads(self, l, partials):
        self.parts.update({(l, n): a for n, a in partials.items()})

    def reduce(self):
        self._unpack()
        assert not self.pending and set(self.recv) == set(self.parts)
        out = {}
        for l in range(2):
            done = _reduce_finish([self.parts[(l, n)] for n in BIG], [self.recv[(l, n)] for n in BIG], f"reduce_l{l}")
            out[l] = dict(zip(BIG, done))
        return {n: jnp.stack([out[0][n], out[1][n]], axis=0) for n in BIG}


def _all_reduce_small(rows):
    r = rows.shape[0]

    def body(x_ref, o_ref, buf, send, recv):
        x, y, c = _coords()
        me = 4 * x + 2 * y + c
        buf[me] = x_ref[...]
        copies = []
        for k in range(1, 8):
            peer = (x ^ (k >> 2), y ^ ((k >> 1) & 1), c ^ (k & 1))
            cp = pltpu.make_async_remote_copy(src_ref=x_ref, dst_ref=buf.at[me], send_sem=send.at[k - 1], recv_sem=recv.at[me],
                                              device_id=peer, device_id_type=MESH)
            cp.start()
            copies.append(cp)
        for k in range(1, 8):
            src = 4 * (x ^ (k >> 2)) + 2 * (y ^ ((k >> 1) & 1)) + (c ^ (k & 1))
            pltpu.make_async_remote_copy(src_ref=x_ref, dst_ref=buf.at[src], send_sem=send.at[0], recv_sem=recv.at[src],
                                         device_id=(x, y, c), device_id_type=MESH).wait_recv()
        for cp in copies:
            cp.wait_send()
        acc = buf[0]
        for k in range(1, 8):
            acc = acc + buf[k]
        o_ref[...] = acc

    vm = pl.BlockSpec(memory_space=pltpu.VMEM)
    return pl.pallas_call(
        body, in_specs=[vm], out_specs=vm, out_shape=jax.ShapeDtypeStruct(rows.shape, F32),
        scratch_shapes=[pltpu.VMEM((8, r, D_MODEL), F32), pltpu.SemaphoreType.DMA((7,)), pltpu.SemaphoreType.DMA((8,))],
        name="all_reduce_small")(rows)


def _adamw_math(w, g, m, v):
    m = ADAM_B1 * m + (1.0 - ADAM_B1) * g
    v = ADAM_B2 * v + (1.0 - ADAM_B2) * (g * g)
    m_hat = m / (1.0 - ADAM_B1 ** ADAM_STEP)
    v_hat = v / (1.0 - ADAM_B2 ** ADAM_STEP)
    return -ADAM_LR * (m_hat / (jnp.sqrt(v_hat) + ADAM_EPS) + ADAM_WD * w), m, v


def _adamw(w, g, m, v, name):
    shape = w.shape
    cols = shape[-1]
    flat = lambda a: a.reshape(-1, cols)
    rows = flat(w).shape[0]
    tm = 128 if rows % 128 == 0 else rows
    ins = [('t', flat(a), cols, 0) for a in (w, g, m, v)]
    res = _ew(_adamw_math, ins, [('t', cols, F32)] * 3, rows=rows, tm=tm, name=name)
    return [a.reshape(shape) for a in res]


def _small_update(sums, logits, w, m, v):
    def body(s_ref, lg_ref, w_ref, m_ref, v_ref, g_ref, d_ref, nm_ref, nv_ref):
        s = s_ref[...]
        l0, l1 = lg_ref[0:1, :], lg_ref[1:2, :]
        mx = jnp.maximum(l0, l1)
        e0, e1 = jnp.exp(l0 - mx), jnp.exp(l1 - mx)
        sm0, sm1 = e0 / (e0 + e1), e1 / (e0 + e1)
        dl1 = s_ref[SMALL_ROWS + 2:SMALL_ROWS + 3, :] * sm0 * sm1
        row = lax.broadcasted_iota(jnp.int32, s.shape, 0)
        g = jnp.where(row == 2, -dl1, jnp.where(row == SMALL_ROWS + 2, dl1, s))
        d, nm, nv = _adamw_math(w_ref[...], g, m_ref[...], v_ref[...])
        g_ref[...] = g
        d_ref[...] = d
        nm_ref[...] = nm
        nv_ref[...] = nv

    vm = pl.BlockSpec(memory_space=pltpu.VMEM)
    return pl.pallas_call(body, in_specs=[vm] * 5, out_specs=[vm] * 4,
                          out_shape=[jax.ShapeDtypeStruct(sums.shape, F32)] * 4, name="small_update")(sums, logits, w, m, v)


def _pack_small(vals):
    rows = []
    for l in range(2):
        for n in ("ffn1_norm", "mix_norm", "hgrn_lb_logits", "hgrn_out_norm", "attn_q_norm", "attn_k_norm", "ffn2_norm"):
            a = vals[n][l].reshape(1, -1)
            rows.append(jnp.pad(a, ((0, 0), (0, D_MODEL - a.shape[1]))))
        rows.append(jnp.zeros((SMALL_ROWS - 7, D_MODEL), F32))
    return jnp.concatenate(rows, axis=0)


def _unpack_small(packed):
    out = {}
    for k, n in enumerate(("ffn1_norm", "mix_norm", "hgrn_lb_logits", "hgrn_out_norm", "attn_q_norm", "attn_k_norm", "ffn2_norm")):
        a = jnp.stack([packed[k], packed[SMALL_ROWS + k]], axis=0)
        out[n] = a[:, :ATT_GROUPS * HEAD].reshape(2, ATT_GROUPS, HEAD) if n.startswith("attn") else a
    return out


def kernel(x, ffn1_norm, ffn1_w_in, ffn1_w_out, mix_norm, w_in, hgrn_lb_logits, hgrn_out_norm, attn_q_norm, attn_k_norm, w_branch_a, w_branch_b, w_out, ffn2_norm, ffn2_w_in, ffn2_w_out, loss_target, m_ffn1_norm, m_ffn1_w_in, m_ffn1_w_out, m_mix_norm, m_w_in, m_hgrn_lb_logits, m_hgrn_out_norm, m_attn_q_norm, m_attn_k_norm, m_w_branch_a, m_w_branch_b, m_w_out, m_ffn2_norm, m_ffn2_w_in, m_ffn2_w_out, v_ffn1_norm, v_ffn1_w_in, v_ffn1_w_out, v_mix_norm, v_w_in, v_hgrn_lb_logits, v_hgrn_out_norm, v_attn_q_norm, v_attn_k_norm, v_w_branch_a, v_w_branch_b, v_w_out, v_ffn2_norm, v_ffn2_w_in, v_ffn2_w_out):
    a = locals()
    w = {n: a[n] for n in WEIGHTS}
    m = {n: a["m_" + n] for n in WEIGHTS}
    v = {n: a["v_" + n] for n in WEIGHTS}

    exchange = _Exchange({n: w[n].astype(BF16) for n in BIG})
    small = {n: w[n] for n in SMALL}
    sq, grad_x, small_rows = _local_step(x[0], loss_target[0], small, exchange)
    loss = lax.psum(sq, ("x", "y", "c")) * (0.5 / D_MODEL)
    grads = exchange.reduce()

    sums = _all_reduce_small(small_rows)
    g_s, d_s, m_s, v_s = _small_update(sums, w["hgrn_lb_logits"], _pack_small(small), _pack_small({n: m[n] for n in SMALL}),
                                       _pack_small({n: v[n] for n in SMALL}))
    grads.update(_unpack_small(g_s))
    delta, new_m, new_v = _unpack_small(d_s), _unpack_small(m_s), _unpack_small(v_s)
    for n in BIG:
        delta[n], new_m[n], new_v[n] = _adamw(w[n], grads[n], m[n], v[n], name="adamw_" + n)

    return (loss, grad_x[None], *[grads[n] for n in WEIGHTS], *[delta[n] for n in WEIGHTS],
            *[new_m[n] for n in WEIGHTS], *[new_v[n] for n in WEIGHTS])
```

```python
import functools

import jax
import jax.numpy as jnp
from jax import lax
from jax.experimental import pallas as pl
from jax.experimental.pallas import tpu as pltpu

F32 = jnp.float32
BF16 = jnp.bfloat16
MESH = pl.DeviceIdType.MESH

D_MODEL = 1024
D_FF = 2816
N_CHIPS = 4
HEAD = 128
HG_HEADS = 8
HG_CHUNK = 64
ATT_GROUPS = 3
ATT_HEADS = 4
ATT_GW = ATT_HEADS * HEAD
DILATIONS = (1, 4, 16)
ATT_BLK = 128
ATT_STEP_BLOCKS = 4
P_IN = 10752
CB_AQ, CB_AK, CB_AV, CB_GA, CB_GB = 8, 11, 14, 17, 19
EPS = 1e-6
ROPE_THETA = 10000.0
ADAM_LR, ADAM_B1, ADAM_B2, ADAM_EPS, ADAM_WD, ADAM_STEP = 0.001, 0.9, 0.999, 1e-08, 0.01, 10
VMEM_LIMIT_V7X = 56 * 1024 * 1024
NEG = -1e30


def _params(sem):
    return pltpu.CompilerParams(dimension_semantics=sem, vmem_limit_bytes=VMEM_LIMIT_V7X)


def _sig(x):
    return 1.0 / (1.0 + jnp.exp(-x))


def _dot(a, b):
    return jnp.dot(a, b, preferred_element_type=F32)


def _dot_nt(a, b):
    return lax.dot_general(a, b, (((1,), (1,)), ((), ())), preferred_element_type=F32)


def _dot_tn(a, b):
    return lax.dot_general(a, b, (((0,), (0,)), ((), ())), preferred_element_type=F32)


def _bf(x):
    return x.astype(BF16)


ANY = pl.BlockSpec(memory_space=pl.ANY)


class _Rider:
    def __init__(self, args, out_shape, sems, begin, end):
        self.args, self.out_shape, self.sems, self.begin, self.end = list(args), list(out_shape), list(sems), begin, end
        self.result = None


def _pcall(body, *, grid, in_specs, out_specs, out_shape, name, sem, args, scratch_shapes=(), rider=None):
    multi = isinstance(out_shape, (list, tuple))
    o_specs = list(out_specs) if multi else [out_specs]
    o_shape = list(out_shape) if multi else [out_shape]
    if rider is None:
        res = pl.pallas_call(body, grid=grid, in_specs=list(in_specs), out_specs=o_specs, out_shape=o_shape,
                             scratch_shapes=list(scratch_shapes), name=name, compiler_params=_params(sem))(*args)
        return list(res) if multi else res[0]
    counts = [len(in_specs), len(rider.args), len(o_specs), len(rider.out_shape), len(scratch_shapes)]

    def wrapped(*refs):
        groups, at = [], 0
        for c in counts:
            groups.append(refs[at:at + c])
            at += c
        h_in, r_in, h_out, r_out, h_scratch = groups
        r_sems = refs[at:]
        if grid:
            ids = [pl.program_id(a) for a in range(len(grid))]
            first = functools.reduce(jnp.logical_and, [i == 0 for i in ids])
            last = functools.reduce(jnp.logical_and, [i == g - 1 for i, g in zip(ids, grid)])
            pl.when(first)(lambda: rider.begin(r_in, r_out, r_sems))
            body(*h_in, *h_out, *h_scratch)
            pl.when(last)(lambda: rider.end(r_in, r_out, r_sems))
        else:
            rider.begin(r_in, r_out, r_sems)
            body(*h_in, *h_out, *h_scratch)
            rider.end(r_in, r_out, r_sems)

    res = pl.pallas_call(
        wrapped, grid=grid, in_specs=list(in_specs) + [ANY] * counts[1], out_specs=o_specs + [ANY] * counts[3],
        out_shape=o_shape + rider.out_shape, scratch_shapes=list(scratch_shapes) + rider.sems, name=name,
        compiler_params=_params(("arbitrary",) * len(grid)))(*args, *rider.args)
    rider.result = list(res[counts[2]:])
    return list(res[:counts[2]]) if multi else res[0]


def _mm_nn(a, b3, *, name, tm, tn, out_dtype, res=None, alpha=1.0, rider=None):
    m, k = a.shape
    nb, _, nw = b3.shape
    per = nw // tn
    assert nw % tn == 0 and m % tm == 0
    has_res = res is not None

    def body(*refs):
        if has_res:
            a_ref, b_ref, r_ref, o_ref = refs
        else:
            a_ref, b_ref, o_ref = refs
        acc = _dot(_bf(a_ref[...]), b_ref[...])
        if alpha != 1.0:
            acc = alpha * acc
        if has_res:
            acc = r_ref[...] + acc
        o_ref[...] = acc.astype(o_ref.dtype)

    in_specs = [pl.BlockSpec((tm, k), lambda i, j: (i, 0)),
                pl.BlockSpec((None, k, tn), lambda i, j: (j // per, 0, j % per))]
    args = [a, b3]
    if has_res:
        in_specs.append(pl.BlockSpec((tm, tn), lambda i, j: (i, j)))
        args.append(res)
    return _pcall(body, grid=(m // tm, nb * per), in_specs=in_specs, out_specs=pl.BlockSpec((tm, tn), lambda i, j: (i, j)),
                  out_shape=jax.ShapeDtypeStruct((m, nb * nw), out_dtype), name=name, sem=("parallel", "arbitrary"),
                  args=args, rider=rider)


def _mm_nt(d, b3, *, name, tm, tp, tn, out_dtype, alpha=1.0, rider=None):
    m, n = d.shape
    nb, p, nw = b3.shape
    per = nw // tn
    nk = n // tn
    assert nb * nw == n and nw % tn == 0 and p % tp == 0 and m % tm == 0

    def body(d_ref, b_ref, o_ref, acc_ref):
        kk = pl.program_id(2)

        @pl.when(kk == 0)
        def _():
            acc_ref[...] = jnp.zeros_like(acc_ref)

        acc_ref[...] += _dot_nt(_bf(d_ref[...]), b_ref[...])

        @pl.when(kk == nk - 1)
        def _():
            o_ref[...] = (alpha * acc_ref[...]).astype(o_ref.dtype)

    return _pcall(
        body, grid=(m // tm, p // tp, nk),
        in_specs=[pl.BlockSpec((tm, tn), lambda i, j, kk: (i, kk)),
                  pl.BlockSpec((None, tp, tn), lambda i, j, kk: (kk // per, j, kk % per))],
        out_specs=pl.BlockSpec((tm, tp), lambda i, j, kk: (i, j)),
        out_shape=jax.ShapeDtypeStruct((m, p), out_dtype),
        scratch_shapes=[pltpu.VMEM((tm, tp), F32)],
        name=name, sem=("parallel", "parallel", "arbitrary"), args=(d, b3), rider=rider)


def _mm_tn(a, d, *, nb, name, tm, tk, tn, alpha=1.0, rider=None):
    m, k = a.shape
    _, n = d.shape
    nw = n // nb
    per = nw // tn
    nm = m // tm
    assert nw % tn == 0 and k % tk == 0 and m % tm == 0

    def body(a_ref, d_ref, o_ref, acc_ref):
        mm = pl.program_id(2)

        @pl.when(mm == 0)
        def _():
            acc_ref[...] = jnp.zeros_like(acc_ref)

        acc_ref[...] += _dot_tn(_bf(a_ref[...]), _bf(d_ref[...]))

        @pl.when(mm == nm - 1)
        def _():
            o_ref[...] = (alpha * acc_ref[...]).astype(o_ref.dtype)

    return _pcall(
        body, grid=(k // tk, nb * per, nm),
        in_specs=[pl.BlockSpec((tm, tk), lambda i, j, mm: (mm, i)),
                  pl.BlockSpec((tm, tn), lambda i, j, mm: (mm, j))],
        out_specs=pl.BlockSpec((None, tk, tn), lambda i, j, mm: (j // per, i, j % per)),
        out_shape=jax.ShapeDtypeStruct((nb, k, nw), BF16),
        scratch_shapes=[pltpu.VMEM((tk, tn), F32)],
        name=name, sem=("parallel", "parallel", "arbitrary"), args=(a, d), rider=rider)


def _ew(fn, ins, outs, *, rows, tm, name):
    in_specs, args, scratch = [], [], []
    for s in ins:
        if s[0] == 't':
            _, arr, w, cb = s
            in_specs.append(pl.BlockSpec((tm, w), lambda i, cb=cb: (i, cb)))
        elif s[0] == 'v':
            _, arr, w, d = s
            in_specs.append(pl.BlockSpec((tm // d, d * w), lambda i: (i, 0)))
            scratch.append(pltpu.VMEM((w // HEAD, tm, HEAD), F32))
        else:
            arr = s[1]
            in_specs.append(pl.BlockSpec(arr.shape, lambda i, nd=arr.ndim: (0,) * nd))
        args.append(arr)
    out_specs, out_shape = [], []
    for s in outs:
        if s[0] == 't':
            _, w, dt = s
            out_specs.append(pl.BlockSpec((tm, w), lambda i: (i, 0)))
            out_shape.append(jax.ShapeDtypeStruct((rows, w), dt))
        elif s[0] == 'v':
            _, w, dt, d = s
            out_specs.append(pl.BlockSpec((tm // d, d * w), lambda i: (i, 0)))
            out_shape.append(jax.ShapeDtypeStruct((rows // d, d * w), dt))
            scratch.append(pltpu.VMEM((w // HEAD, tm, HEAD), F32))
        else:
            out_specs.append(pl.BlockSpec(s[1], lambda i: (0, 0)))
            out_shape.append(jax.ShapeDtypeStruct(s[1], F32))
    n_in, n_out = len(ins), len(outs)

    def body(*refs):
        bufs = list(refs[n_in + n_out:])
        vals = []
        for r, s in zip(refs[:n_in], ins):
            if s[0] == 'v':
                w, d, buf = s[2], s[3], bufs.pop(0)
                for k in range(d):
                    for c in range(w // HEAD):
                        lanes = slice(k * w + c * HEAD, k * w + (c + 1) * HEAD)
                        buf.at[c][pl.ds(k, tm // d, stride=d), :] = r[:, lanes].astype(F32)
                vals.append(_cat([buf[c] for c in range(w // HEAD)]))
            else:
                vals.append(r[...])
        res = fn(*vals)
        if not isinstance(res, (tuple, list)):
            res = (res,)
        for r, s, v in zip(refs[n_in:n_in + n_out], outs, res):
            if s[0] == 't':
                r[...] = v.astype(r.dtype)
            elif s[0] == 'v':
                w, d, buf = s[1], s[3], bufs.pop(0)
                for c in range(w // HEAD):
                    buf[c] = v[:, c * HEAD:(c + 1) * HEAD].astype(F32)
                for k in range(d):
                    for c in range(w // HEAD):
                        lanes = slice(k * w + c * HEAD, k * w + (c + 1) * HEAD)
                        r[:, lanes] = buf.at[c][pl.ds(k, tm // d, stride=d), :].astype(r.dtype)
            else:
                @pl.when(pl.program_id(0) == 0)
                def _(r=r):
                    r[...] = jnp.zeros_like(r)

                r[...] += v

    res = pl.pallas_call(
        body, grid=(rows // tm,), in_specs=in_specs, out_specs=out_specs, out_shape=out_shape, scratch_shapes=scratch,
        name=name, compiler_params=_params(("arbitrary",)))(*args)
    return res


def _tile(arr, w, g):
    return ('t', arr, w, 0) if DILATIONS[g] == 1 else ('v', arr, w, DILATIONS[g])


def _tile_out(w, dtype, g):
    return ('t', w, dtype) if DILATIONS[g] == 1 else ('v', w, dtype, DILATIONS[g])


def _heads(x):
    return [x[:, h * HEAD:(h + 1) * HEAD] for h in range(x.shape[1] // HEAD)]


def _cat(xs):
    return jnp.concatenate(xs, axis=1)


def _head_mean(x):
    return _cat([jnp.broadcast_to(jnp.mean(h, axis=1, keepdims=True), h.shape) for h in _heads(x)])


def _rms_rows(x):
    return lax.rsqrt(jnp.mean(x * x, axis=1, keepdims=True) + EPS)


def _norm_fwd(x, g, name):
    return _ew(lambda xv, gv: xv * _rms_rows(xv) * gv,
               [('t', x, D_MODEL, 0), ('f', g)], [('t', D_MODEL, BF16)], rows=x.shape[0], tm=512, name=name)[0]


def _norm_bwd(dh, x, g, dx, name):
    def fn(dhv, xv, gv, dxv):
        r = _rms_rows(xv)
        xh = xv * r
        dxh = dhv * gv
        out = dxv + r * (dxh - xh * jnp.mean(dxh * xh, axis=1, keepdims=True))
        return out, jnp.sum(dhv * xh, axis=0, keepdims=True)

    return _ew(fn, [('t', dh, D_MODEL, 0), ('t', x, D_MODEL, 0), ('f', g), ('t', dx, D_MODEL, 0)],
               [('t', D_MODEL, F32), ('acc', (1, D_MODEL))], rows=x.shape[0], tm=512, name=name)


def _loss_fwd_bwd(y, target, name):
    def fn(yv, tv):
        e = yv - tv
        return e * (1.0 / D_MODEL), jnp.sum(e * e, axis=0, keepdims=True)

    return _ew(fn, [('t', y, D_MODEL, 0), ('t', target, D_MODEL, 0)], [('t', D_MODEL, F32), ('acc', (1, D_MODEL))],
               rows=y.shape[0], tm=512, name=name)


def _rot(x):
    sgn = jnp.where(lax.broadcasted_iota(jnp.int32, x.shape, 1) < HEAD // 2, -1.0, 1.0)
    return pltpu.roll(x, HEAD // 2, 1) * sgn


def _gain_rows(qn, kn):
    return [a[g:g + 1] for a in (qn, kn) for g in range(ATT_GROUPS)]


def _qk_fwd(proj, cos, sin, qn, kn, name):
    def fn(*v):
        xs, cosv, sinv, gains, vs = v[:6], v[6], v[7], v[8:14], v[14:17]
        outs = []
        for j, x in enumerate(xs):
            gain = gains[j]
            ys = []
            for xh in _heads(x):
                xn = xh * _rms_rows(xh) * gain
                ys.append(xn * cosv + _rot(xn) * sinv)
            outs.append(_cat(ys))
        return outs + list(vs)

    ins = ([('t', proj, 512, CB_AQ + j) for j in range(6)] + [('t', cos, HEAD, 0), ('t', sin, HEAD, 0)]
           + [('f', a) for a in _gain_rows(qn, kn)] + [('t', proj, 512, CB_AV + g) for g in range(ATT_GROUPS)])
    return _ew(fn, ins, [_tile_out(ATT_GW, BF16, j % ATT_GROUPS) for j in range(9)], rows=proj.shape[0], tm=512, name=name)


def _qk_bwd(dqk, proj, cos, sin, qn, kn, name):
    def fn(*v):
        ds, xs, cosv, sinv, gains = v[:6], v[6:12], v[12], v[13], v[14:20]
        rows8 = lax.broadcasted_iota(jnp.int32, (8, HEAD), 0)
        outs, dgs = [], [jnp.zeros((8, HEAD), F32)] * 2
        for j in range(6):
            gain = gains[j]
            dx, dg = [], jnp.zeros((1, HEAD), F32)
            for dyh, xh in zip(_heads(ds[j]), _heads(xs[j])):
                r = _rms_rows(xh)
                xhat = xh * r
                dxn = dyh * cosv - _rot(dyh * sinv)
                dg = dg + jnp.sum(dxn * xhat, axis=0, keepdims=True)
                dxh = dxn * gain
                dx.append(r * (dxh - xhat * jnp.mean(dxh * xhat, axis=1, keepdims=True)))
            outs.append(_cat(dx))
            dgs[j // 3] = dgs[j // 3] + jnp.where(rows8 == j % 3, dg, 0.0)
        return _cat(outs), dgs[0], dgs[1]

    ins = ([_tile(a, ATT_GW, j % ATT_GROUPS) for j, a in enumerate(dqk)] + [('t', proj, 512, CB_AQ + j) for j in range(6)]
           + [('t', cos, HEAD, 0), ('t', sin, HEAD, 0)] + [('f', a) for a in _gain_rows(qn, kn)])
    return _ew(fn, ins, [('t', 6 * ATT_GW, BF16), ('acc', (8, HEAD)), ('acc', (8, HEAD))],
               rows=proj.shape[0], tm=256, name=name)


def _pick(x, h):
    lanes = lax.broadcasted_iota(jnp.int32, x.shape, 1)
    return jnp.sum(jnp.where(lanes == h, x, 0.0), axis=1, keepdims=True)


def _spread(x):
    return _cat([jnp.broadcast_to(_pick(x, h), (x.shape[0], HEAD)) for h in range(ATT_HEADS)])


def _compact(x):
    lanes = lax.broadcasted_iota(jnp.int32, (x.shape[0], HEAD), 1)
    out = jnp.zeros((x.shape[0], HEAD), F32)
    for h, xh in enumerate(_heads(x)):
        out = jnp.where(lanes == h, xh, out)
    return out


def _group_weights(l0, l1, l2):
    l0, l1, l2 = _spread(l0), _spread(l1), _spread(l2)
    m = jnp.maximum(jnp.maximum(l0, l1), l2)
    e0, e1, e2 = jnp.exp(l0 - m), jnp.exp(l1 - m), jnp.exp(l2 - m)
    inv = 1.0 / (e0 + e1 + e2)
    return e0 * inv, e1 * inv, e2 * inv


def _merge_fwd(outs, lses, name):
    def fn(o0, o1, o2, l0, l1, l2):
        a0, a1, a2 = _group_weights(l0, l1, l2)
        return a0 * o0 + a1 * o1 + a2 * o2

    ins = [_tile(a, ATT_GW, g) for g, a in enumerate(outs)] + [_tile(a, HEAD, g) for g, a in enumerate(lses)]
    return _ew(fn, ins, [('t', ATT_GW, BF16)], rows=outs[0].shape[0], tm=512, name=name)[0]


def _merge_bwd(dob, outs, lses, name):
    def fn(dov, o0, o1, o2, l0, l1, l2):
        a0, a1, a2 = _group_weights(l0, l1, l2)
        ob = a0 * o0 + a1 * o1 + a2 * o2
        s = _head_mean(dov * ob) * float(HEAD)
        return a0 * dov, a1 * dov, a2 * dov, _compact(a0 * s), _compact(a1 * s), _compact(a2 * s)

    ins = ([('t', dob, ATT_GW, 0)] + [_tile(a, ATT_GW, g) for g, a in enumerate(outs)]
           + [_tile(a, HEAD, g) for g, a in enumerate(lses)])
    groups = range(ATT_GROUPS)
    return _ew(fn, ins, [_tile_out(ATT_GW, BF16, g) for g in groups] + [_tile_out(HEAD, F32, g) for g in groups],
               rows=dob.shape[0], tm=512, name=name)


def _assemble_dproj(dh4, dqk, dvs, dgab, name):
    fn = lambda *v: _cat(list(v))
    ins = ([('t', dh4, 4 * D_MODEL, 0), ('t', dqk, 6 * ATT_GW, 0)] + [_tile(a, ATT_GW, g) for g, a in enumerate(dvs)]
           + [('t', dgab, 2 * D_MODEL, 0)])
    return _ew(fn, ins, [('t', P_IN, BF16)], rows=dh4.shape[0], tm=256, name=name)[0]


HG_ROWS = 256


def _hg_gates(hq, hf, hi, lbv):
    sig = _sig(hf)
    f = lbv + (1.0 - lbv) * sig
    return hq * _sig(hq), 1.0 - f, hi, jnp.log(f), sig, f


def _split3(x):
    hi = _bf(x)
    r1 = x - hi.astype(F32)
    mid = _bf(r1)
    return hi, mid, _bf(r1 - mid.astype(F32))


def _tri_dot(tri, x):
    hi, mid, lo = _split3(x)
    return _dot(tri, hi) + _dot(tri, mid) + _dot(tri, lo)


def _row(x, i):
    rows = lax.broadcasted_iota(jnp.int32, x.shape, 0)
    return jnp.sum(jnp.where(rows == i, x, 0.0), axis=0, keepdims=True)


def _hg_decay(logf, q, k):
    c = HG_CHUNK
    row = lax.broadcasted_iota(jnp.int32, (c, c), 0)
    col = lax.broadcasted_iota(jnp.int32, (c, c), 1)
    g = _tri_dot((row >= col).astype(BF16), logf)
    gm = _row(g, c // 2 - 1)
    gl = _row(g, c - 1)
    return g, gm, gl, q * jnp.exp(g), q * jnp.exp(g - gm), k * jnp.exp(gm - g), k * jnp.exp(gl - g)


def _hg_out_fwd(o, hg, gain):
    r = lax.rsqrt(_head_mean(o * o) + EPS)
    return o * r * gain * (hg * _sig(hg))


def _hgrn_fwd(proj, lb, gain, name, rider=None):
    t = proj.shape[0]
    nck = HG_ROWS // HG_CHUNK

    def body(hq_ref, hf_ref, hi_ref, hg_ref, lb_ref, gn_ref, o_ref, oa_ref, sall_ref, st_ref):
        @pl.when(pl.program_id(0) == 0)
        def _():
            st_ref[...] = jnp.zeros_like(st_ref)

        lbv = lb_ref[...]
        gnv = gn_ref[...]
        c = HG_CHUNK
        mask = lax.broadcasted_iota(jnp.int32, (c, c), 0) >= lax.broadcasted_iota(jnp.int32, (c, c), 1)

        def chunk(cc, carry):
            sl = pl.ds(pl.multiple_of(cc * c, c), c)
            q, k, v, logf, _, _ = _hg_gates(hq_ref[sl, :], hf_ref[sl, :], hi_ref[sl, :], lbv)
            _, _, gl, qg, qt, kt, kd = _hg_decay(logf, q, k)
            egl = jnp.exp(gl)
            os = []
            for h in range(HG_HEADS):
                hs = slice(h * HEAD, (h + 1) * HEAD)
                st = st_ref[h]
                sall_ref[cc, h] = st
                a = jnp.where(mask, _dot_nt(_bf(qt[:, hs]), _bf(kt[:, hs])), 0.0)
                os.append(_dot(_bf(a), _bf(v[:, hs])) + _dot_nt(_bf(qg[:, hs]), _bf(st)))
                st_ref[h] = egl[:, hs] * st + _dot_tn(_bf(v[:, hs]), _bf(kd[:, hs]))
            o = _cat(os)
            o_ref[sl, :] = o
            oa_ref[sl, :] = _hg_out_fwd(o, hg_ref[sl, :], gnv).astype(oa_ref.dtype)
            return carry

        lax.fori_loop(0, nck, chunk, 0)

    col = lambda j: pl.BlockSpec((HG_ROWS, D_MODEL), lambda i, j=j: (i, j))
    small = pl.BlockSpec((1, D_MODEL), lambda i: (0, 0))
    return _pcall(
        body, grid=(t // HG_ROWS,),
        in_specs=[col(0), col(1), col(2), col(3), small, small],
        out_specs=[col(0), col(0), pl.BlockSpec((nck, HG_HEADS, HEAD, HEAD), lambda i: (i, 0, 0, 0))],
        out_shape=[jax.ShapeDtypeStruct((t, D_MODEL), F32), jax.ShapeDtypeStruct((t, D_MODEL), BF16),
                   jax.ShapeDtypeStruct((t // HG_CHUNK, HG_HEADS, HEAD, HEAD), F32)],
        scratch_shapes=[pltpu.VMEM((HG_HEADS, HEAD, HEAD), F32)],
        name=name, sem=("arbitrary",), args=(proj, proj, proj, proj, lb, gain), rider=rider)


def _terms(x, precise):
    hi = _bf(x)
    return (hi, _bf(x - hi.astype(F32))) if precise else (hi,)


def _mm(dot, a, b):
    out = dot(a[0], b[0])
    if len(a) > 1:
        out = out + dot(a[1], b[0])
    if len(b) > 1:
        out = out + dot(a[0], b[1])
    return out


def _hgrn_bwd(doa, oscan, proj, sall, lb, gain, name, precise, rider=None):
    t = proj.shape[0]
    nck = HG_ROWS // HG_CHUNK
    nsteps = t // HG_ROWS
    terms = functools.partial(_terms, precise=precise)

    def body(doa_ref, os_ref, hq_ref, hf_ref, hi_ref, hg_ref, sall_ref, lb_ref, gn_ref,
             d4_ref, dgn_ref, dlb_ref, dst_ref):
        @pl.when(pl.program_id(0) == 0)
        def _():
            dst_ref[...] = jnp.zeros_like(dst_ref)
            dgn_ref[...] = jnp.zeros_like(dgn_ref)
            dlb_ref[...] = jnp.zeros_like(dlb_ref)

        lbv = lb_ref[...]
        gnv = gn_ref[...]
        c = HG_CHUNK
        row = lax.broadcasted_iota(jnp.int32, (c, c), 0)
        colm = lax.broadcasted_iota(jnp.int32, (c, c), 1)
        mask = row >= colm
        triu = (row <= colm).astype(BF16)
        last = lax.broadcasted_iota(jnp.int32, (c, HEAD), 0) == c - 1

        def chunk(ci, carry):
            cc = nck - 1 - ci
            sl = pl.ds(pl.multiple_of(cc * c, c), c)
            hq, hf, hg = hq_ref[sl, :], hf_ref[sl, :], hg_ref[sl, :]
            q, k, v, logf, sig, f = _hg_gates(hq, hf, hi_ref[sl, :], lbv)
            g, gm, gl, qg, qt, kt, kd = _hg_decay(logf, q, k)
            egl = jnp.exp(gl)
            o = os_ref[sl, :]
            dy = doa_ref[sl, :]
            r = lax.rsqrt(_head_mean(o * o) + EPS)
            oh = o * r
            sg = _sig(hg)
            silu_g = hg * sg
            dgn_ref[...] += jnp.sum(dy * oh * silu_g, axis=0, keepdims=True)
            dhg = dy * oh * gnv * (sg * (1.0 + hg * (1.0 - sg)))
            doh = dy * gnv * silu_g
            do = r * (doh - oh * _head_mean(doh * oh))
            dqs, dks, dvs, dgs = [], [], [], []
            for h in range(HG_HEADS):
                hs = slice(h * HEAD, (h + 1) * HEAD)
                st = sall_ref[cc, h]
                dst = dst_ref[h]
                qt_h, kt_h, qg_h, kd_h = qt[:, hs], kt[:, hs], qg[:, hs], kd[:, hs]
                do_p, v_p, qt_p, kt_p, qg_p = terms(do[:, hs]), terms(v[:, hs]), terms(qt_h), terms(kt_h), terms(qg_h)
                st_p, dst_p = terms(st), terms(dst)
                a = jnp.where(mask, _dot_nt(qt_p[0], kt_p[0]), 0.0)
                da = terms(jnp.where(mask, _mm(_dot_nt, do_p, v_p), 0.0))
                dqt = _mm(_dot, da, kt_p)
                dkt = _mm(_dot_tn, da, qt_p)
                dqg = _mm(_dot, do_p, st_p)
                dv = _dot_tn(_bf(a), do_p[0]) + _dot_nt(_bf(kd_h), dst_p[0])
                dkd = _mm(_dot, v_p, dst_p)
                dgl = egl[:, hs] * jnp.sum(st * dst, axis=0, keepdims=True) + jnp.sum(dkd * kd_h, axis=0, keepdims=True)
                dst_ref[h] = egl[:, hs] * dst + _mm(_dot_tn, do_p, qg_p)
                g_h = g[:, hs]
                gm_h = gm[:, hs]
                gl_h = gl[:, hs]
                dqs.append(dqt * jnp.exp(g_h - gm_h) + dqg * jnp.exp(g_h))
                dks.append(dkt * jnp.exp(gm_h - g_h) + dkd * jnp.exp(gl_h - g_h))
                dvs.append(dv)
                dgs.append(dqt * qt_h - dkt * kt_h + dqg * qg_h - dkd * kd_h + jnp.where(last, dgl, 0.0))
            dq, dk, dv, dg = _cat(dqs), _cat(dks), _cat(dvs), _cat(dgs)
            dlogf = _tri_dot(triu, dg)
            df = dlogf / f - dk
            dlb_ref[...] += jnp.sum(df * (1.0 - sig), axis=0, keepdims=True)
            dhf = df * (1.0 - lbv) * sig * (1.0 - sig)
            sq = _sig(hq)
            dhq = dq * (sq * (1.0 + hq * (1.0 - sq)))
            d4_ref[sl, :] = _cat([dhq, dhf, dv, dhg]).astype(d4_ref.dtype)
            return carry

        lax.fori_loop(0, nck, chunk, 0)

    rev = lambda j: pl.BlockSpec((HG_ROWS, D_MODEL), lambda i, j=j: (nsteps - 1 - i, j))
    small = pl.BlockSpec((1, D_MODEL), lambda i: (0, 0))
    return _pcall(
        body, grid=(nsteps,),
        in_specs=[rev(0), rev(0), rev(0), rev(1), rev(2), rev(3),
                  pl.BlockSpec((nck, HG_HEADS, HEAD, HEAD), lambda i: (nsteps - 1 - i, 0, 0, 0)), small, small],
        out_specs=[pl.BlockSpec((HG_ROWS, 4 * D_MODEL), lambda i: (nsteps - 1 - i, 0)), small, small],
        out_shape=[jax.ShapeDtypeStruct((t, 4 * D_MODEL), BF16), jax.ShapeDtypeStruct((1, D_MODEL), F32),
                   jax.ShapeDtypeStruct((1, D_MODEL), F32)],
        scratch_shapes=[pltpu.VMEM((HG_HEADS, HEAD, HEAD), F32)],
        name=name, sem=("arbitrary",), args=(doa, oscan, proj, proj, proj, proj, sall, lb, gain), rider=rider)


def _band_masks():
    qi = lax.broadcasted_iota(jnp.int32, (ATT_BLK, ATT_BLK), 0)
    ki = lax.broadcasted_iota(jnp.int32, (ATT_BLK, ATT_BLK), 1)
    return ki >= qi, ki <= qi


def _attn_cfg(qg, g):
    d = DILATIONS[g]
    length = qg.shape[0]
    assert qg.shape[1] == d * ATT_GW
    nb = length // ATT_BLK
    return d, length, nb, min(ATT_STEP_BLOCKS, nb)


def _attn_fwd(qg, kg, vg, g, name):
    d, length, nb, rb = _attn_cfg(qg, g)
    scale = HEAD ** -0.5

    def body(q_ref, k_ref, v_ref, kp_ref, vp_ref, o_ref, l_ref):
        n = pl.program_id(1)
        prev_m, own_m = _band_masks()
        first_m = jnp.logical_and(prev_m, n > 0)
        lanes = lax.broadcasted_iota(jnp.int32, (ATT_BLK, HEAD), 1)
        for j in range(rb):
            rows = slice(j * ATT_BLK, (j + 1) * ATT_BLK)
            before = slice((j - 1) * ATT_BLK, j * ATT_BLK)
            lse = jnp.zeros((ATT_BLK, HEAD), F32)
            for h in range(ATT_HEADS):
                hs = slice(h * HEAD, (h + 1) * HEAD)
                q = q_ref[rows, hs]
                k0, v0, m0 = (kp_ref[:, hs], vp_ref[:, hs], first_m) if j == 0 else (k_ref[before, hs], v_ref[before, hs], prev_m)
                s0 = jnp.where(m0, _dot_nt(q, k0) * scale, NEG)
                s1 = jnp.where(own_m, _dot_nt(q, k_ref[rows, hs]) * scale, NEG)
                m = jnp.maximum(jnp.max(s0, axis=1, keepdims=True), jnp.max(s1, axis=1, keepdims=True))
                p0, p1 = jnp.exp(s0 - m), jnp.exp(s1 - m)
                l = jnp.sum(p0, axis=1, keepdims=True) + jnp.sum(p1, axis=1, keepdims=True)
                o = _dot(_bf(p0), v0) + _dot(_bf(p1), v_ref[rows, hs])
                o_ref[rows, hs] = (o / l).astype(o_ref.dtype)
                lse = jnp.where(lanes == h, m + jnp.log(l), lse)
            l_ref[rows, :] = lse

    own = pl.BlockSpec((rb * ATT_BLK, ATT_GW), lambda r, n: (n, r))
    own_head = pl.BlockSpec((rb * ATT_BLK, HEAD), lambda r, n: (n, r))
    prev = pl.BlockSpec((ATT_BLK, ATT_GW), lambda r, n: (jnp.maximum(n * rb - 1, 0), r))
    return pl.pallas_call(
        body, grid=(d, nb // rb), in_specs=[own, own, own, prev, prev], out_specs=[own, own_head],
        out_shape=[jax.ShapeDtypeStruct((length, d * ATT_GW), BF16), jax.ShapeDtypeStruct((length, d * HEAD), F32)],
        name=name, compiler_params=_params(("parallel", "arbitrary")))(qg, kg, vg, kg, vg)


def _attn_bwd(qg, kg, vg, dog, lse, delta, g, name):
    d, length, nb, rb = _attn_cfg(qg, g)
    nsteps = nb // rb
    scale = HEAD ** -0.5

    def body(q_ref, k_ref, v_ref, do_ref, l_ref, dl_ref, kp_ref, vp_ref, qn_ref, don_ref, ln_ref, dln_ref,
             dq_ref, dk_ref, dv_ref):
        n = pl.program_id(1)
        prev_m, own_m = _band_masks()
        first_m = jnp.logical_and(prev_m, n > 0)
        next_m = jnp.logical_and(prev_m, n < nsteps - 1)
        for h in range(ATT_HEADS):
            hs = slice(h * HEAD, (h + 1) * HEAD)
            dk, dv = [None] * rb, [None] * rb
            for j in range(rb + 1):
                rows = slice(j * ATT_BLK, (j + 1) * ATT_BLK)
                before = slice((j - 1) * ATT_BLK, j * ATT_BLK)
                if j < rb:
                    q, do, lse_q, dl_q = q_ref[rows, hs], do_ref[rows, hs], _pick(l_ref[rows, :], h), _pick(dl_ref[rows, :], h)
                else:
                    q, do, lse_q, dl_q = qn_ref[:, hs], don_ref[:, hs], _pick(ln_ref[...], h), _pick(dln_ref[...], h)
                if j == 0:
                    k0, v0, m0 = kp_ref[:, hs], vp_ref[:, hs], first_m
                else:
                    k0, v0, m0 = k_ref[before, hs], v_ref[before, hs], (prev_m if j < rb else next_m)
                p0 = jnp.where(m0, jnp.exp(_dot_nt(q, k0) * scale - lse_q), 0.0)
                ds0 = _bf(p0 * (_dot_nt(do, v0) - dl_q) * scale)
                if j >= 1:
                    dk[j - 1] = dk[j - 1] + _dot_tn(ds0, q)
                    dv[j - 1] = dv[j - 1] + _dot_tn(_bf(p0), do)
                if j < rb:
                    k1, v1 = k_ref[rows, hs], v_ref[rows, hs]
                    p1 = jnp.where(own_m, jnp.exp(_dot_nt(q, k1) * scale - lse_q), 0.0)
                    ds1 = _bf(p1 * (_dot_nt(do, v1) - dl_q) * scale)
                    dq_ref[rows, hs] = (_dot(ds0, k0) + _dot(ds1, k1)).astype(dq_ref.dtype)
                    dk[j] = _dot_tn(ds1, q)
                    dv[j] = _dot_tn(_bf(p1), do)
            for j in range(rb):
                rows = slice(j * ATT_BLK, (j + 1) * ATT_BLK)
                dk_ref[rows, hs] = dk[j].astype(dk_ref.dtype)
                dv_ref[rows, hs] = dv[j].astype(dv_ref.dtype)

    own = pl.BlockSpec((rb * ATT_BLK, ATT_GW), lambda r, n: (n, r))
    prev = pl.BlockSpec((ATT_BLK, ATT_GW), lambda r, n: (jnp.maximum(n * rb - 1, 0), r))
    nxt = pl.BlockSpec((ATT_BLK, ATT_GW), lambda r, n: (jnp.minimum((n + 1) * rb, nb - 1), r))
    own_head = pl.BlockSpec((rb * ATT_BLK, HEAD), lambda r, n: (n, r))
    nxt_head = pl.BlockSpec((ATT_BLK, HEAD), lambda r, n: (jnp.minimum((n + 1) * rb, nb - 1), r))
    return pl.pallas_call(
        body, grid=(d, nsteps), in_specs=[own] * 4 + [own_head] * 2 + [prev, prev, nxt, nxt, nxt_head, nxt_head],
        out_specs=[own, own, own], out_shape=[jax.ShapeDtypeStruct((length, d * ATT_GW), BF16)] * 3,
        name=name, compiler_params=_params(("parallel", "arbitrary")))(
            qg, kg, vg, dog, lse, delta, kg, vg, qg, dog, lse, delta)


def _rope_tables(t):
    pos = jnp.arange(t, dtype=F32)
    inv = ROPE_THETA ** (-jnp.arange(0, HEAD, 2, dtype=F32) / HEAD)
    ang = pos[:, None] * inv[None, :]
    ang = jnp.concatenate([ang, ang], axis=-1)
    return jnp.cos(ang), jnp.sin(ang)


def _lower_bounds(logits):
    lb = jnp.cumsum(jax.nn.softmax(logits.astype(F32), axis=0), axis=0)
    return lb - lb[0:1]


FFN_ROWS = 256
FF_SHARD = 2 * D_FF // N_CHIPS


def _ffn_in_act(x, g, w_in, name, rider=None):
    t = x.shape[0]

    def body(x_ref, g_ref, w_ref, h_ref, ab_ref, u_ref):
        xv = x_ref[...]
        h = _bf(xv * _rms_rows(xv) * g_ref[...])
        h_ref[...] = h
        for s in range(N_CHIPS // 2):
            cols = slice(s * FF_SHARD, (s + 1) * FF_SHARD)
            a = _dot(h, w_ref[s])
            b = _dot(h, w_ref[s + N_CHIPS // 2])
            ab_ref[:, cols] = a.astype(ab_ref.dtype)
            ab_ref[:, D_FF + s * FF_SHARD:D_FF + (s + 1) * FF_SHARD] = b.astype(ab_ref.dtype)
            u_ref[:, cols] = (a * _sig(a) * b).astype(u_ref.dtype)

    row = lambda w: pl.BlockSpec((FFN_ROWS, w), lambda i: (i, 0))
    return _pcall(
        body, grid=(t // FFN_ROWS,),
        in_specs=[row(D_MODEL), pl.BlockSpec((1, D_MODEL), lambda i: (0, 0)),
                  pl.BlockSpec(w_in.shape, lambda i: (0, 0, 0))],
        out_specs=[row(D_MODEL), row(2 * D_FF), row(D_FF)],
        out_shape=[jax.ShapeDtypeStruct((t, D_MODEL), BF16), jax.ShapeDtypeStruct((t, 2 * D_FF), BF16),
                   jax.ShapeDtypeStruct((t, D_FF), BF16)],
        name=name, sem=("parallel",), args=(x, g, w_in), rider=rider)


def _ffn_bwd_du_act(dx, w_out, ab, name, rider=None):
    t = dx.shape[0]

    def body(dx_ref, w_ref, ab_ref, o_ref):
        du = 0.5 * _dot_nt(_bf(dx_ref[...]), w_ref[0])
        a = ab_ref[:, :D_FF].astype(F32)
        b = ab_ref[:, D_FF:].astype(F32)
        s = _sig(a)
        o_ref[:, :D_FF] = (du * b * (s * (1.0 + a * (1.0 - s)))).astype(o_ref.dtype)
        o_ref[:, D_FF:] = (du * a * s).astype(o_ref.dtype)

    row = lambda w: pl.BlockSpec((FFN_ROWS, w), lambda i: (i, 0))
    return _pcall(
        body, grid=(t // FFN_ROWS,),
        in_specs=[row(D_MODEL), pl.BlockSpec(w_out.shape, lambda i: (0, 0, 0)), row(2 * D_FF)],
        out_specs=row(2 * D_FF), out_shape=jax.ShapeDtypeStruct((t, 2 * D_FF), BF16),
        name=name, sem=("parallel",), args=(dx, w_out, ab), rider=rider)


MIX_ROWS = 512


def _gate_specs():
    return [pl.BlockSpec((MIX_ROWS, 512), lambda i, cb=cb: (i, cb)) for cb in (CB_GA, CB_GA + 1, CB_GB, CB_GB + 1)]


def _whole(a):
    return pl.BlockSpec(a.shape, lambda i: (0,) * a.ndim)


def _mix_tail_fwd(oa, ob, proj, x, w_a, w_b, w_o, name):
    t = x.shape[0]

    def body(oa_ref, ob_ref, ga0, ga1, gb0, gb1, x_ref, wa_ref, wb_ref, wo_ref, y_ref, m_ref, ya_ref, yb_ref):
        ya = _dot(oa_ref[...], wa_ref[0])
        yb = _cat([_dot(ob_ref[...], wb_ref[s]) for s in range(N_CHIPS)])
        merged = _bf(_sig(_cat([ga0[...], ga1[...]])) * ya + _sig(_cat([gb0[...], gb1[...]])) * yb)
        m_ref[...] = merged
        ya_ref[...] = ya.astype(ya_ref.dtype)
        yb_ref[...] = yb.astype(yb_ref.dtype)
        y_ref[...] = x_ref[...] + _dot(merged, wo_ref[0])

    row = lambda w: pl.BlockSpec((MIX_ROWS, w), lambda i: (i, 0))
    return pl.pallas_call(
        body, grid=(t // MIX_ROWS,),
        in_specs=[row(D_MODEL), row(ATT_GW)] + _gate_specs() + [row(D_MODEL), _whole(w_a), _whole(w_b), _whole(w_o)],
        out_specs=[row(D_MODEL)] * 4,
        out_shape=[jax.ShapeDtypeStruct((t, D_MODEL), F32)] + [jax.ShapeDtypeStruct((t, D_MODEL), BF16)] * 3,
        name=name, compiler_params=_params(("parallel",)))(oa, ob, proj, proj, proj, proj, x, w_a, w_b, w_o)


def _mix_tail_bwd(dx, proj, ya, yb, w_a, w_b, w_o, name):
    t = dx.shape[0]
    shard = D_MODEL // N_CHIPS

    def body(dx_ref, ga0, ga1, gb0, gb1, ya_ref, yb_ref, wa_ref, wb_ref, wo_ref, dya_ref, dyb_ref, dg_ref, doa_ref, dob_ref):
        dm = _dot_nt(_bf(dx_ref[...]), wo_ref[0])
        sa = _sig(_cat([ga0[...], ga1[...]]))
        sb = _sig(_cat([gb0[...], gb1[...]]))
        dya, dyb = _bf(dm * sa), _bf(dm * sb)
        dya_ref[...] = dya
        dyb_ref[...] = dyb
        dg_ref[:, :D_MODEL] = (dm * ya_ref[...].astype(F32) * sa * (1.0 - sa)).astype(dg_ref.dtype)
        dg_ref[:, D_MODEL:] = (dm * yb_ref[...].astype(F32) * sb * (1.0 - sb)).astype(dg_ref.dtype)
        doa_ref[...] = _dot_nt(dya, wa_ref[0])
        dob = _dot_nt(dyb[:, :shard], wb_ref[0])
        for s in range(1, N_CHIPS):
            dob = dob + _dot_nt(dyb[:, s * shard:(s + 1) * shard], wb_ref[s])
        dob_ref[...] = dob

    row = lambda w: pl.BlockSpec((MIX_ROWS, w), lambda i: (i, 0))
    return pl.pallas_call(
        body, grid=(t // MIX_ROWS,),
        in_specs=[row(D_MODEL)] + _gate_specs() + [row(D_MODEL), row(D_MODEL), _whole(w_a), _whole(w_b), _whole(w_o)],
        out_specs=[row(D_MODEL), row(D_MODEL), row(2 * D_MODEL), row(D_MODEL), row(ATT_GW)],
        out_shape=[jax.ShapeDtypeStruct((t, D_MODEL), BF16), jax.ShapeDtypeStruct((t, D_MODEL), BF16),
                   jax.ShapeDtypeStruct((t, 2 * D_MODEL), BF16), jax.ShapeDtypeStruct((t, D_MODEL), F32),
                   jax.ShapeDtypeStruct((t, ATT_GW), F32)],
        name=name, compiler_params=_params(("parallel",)))(dx, proj, proj, proj, proj, ya, yb, w_a, w_b, w_o)


def _ffn_fwd(x, g, src, l, pre):
    tag = f"l{l}_{pre}"
    w_in = src.weight(l, pre + "_w_in")
    h, ab, u = _ffn_in_act(x, g, w_in, name=tag + "_in_act", rider=src.ride(tag + "_in_act"))
    w_out = src.weight(l, pre + "_w_out")
    y = _mm_nn(u, w_out, name=tag + "_out", tm=512, tn=D_MODEL, out_dtype=F32, res=x, alpha=0.5, rider=src.ride(tag + "_out"))
    return y, (x, h, ab, u, w_in, w_out)


def _ffn_bwd(dx, saved, g, src, l, pre):
    tag = f"l{l}_{pre}"
    x, h, ab, u, w_in, w_out = saved
    g_out = _mm_tn(u, dx, nb=1, name=tag + "_bwd_wout", tm=1024, tk=1408, tn=D_MODEL, alpha=0.5, rider=src.ride(tag + "_bwd_wout"))
    src.grads(l, {pre + "_w_out": g_out.reshape(N_CHIPS, D_FF // N_CHIPS, D_MODEL)})
    dab = _ffn_bwd_du_act(dx, w_out, ab, name=tag + "_bwd_du_act", rider=src.ride(tag + "_bwd_du_act"))
    g_in = _mm_tn(h, dab, nb=N_CHIPS, name=tag + "_bwd_win", tm=2048, tk=D_MODEL, tn=FF_SHARD, rider=src.ride(tag + "_bwd_win"))
    src.grads(l, {pre + "_w_in": g_in})
    dh = _mm_nt(dab, w_in, name=tag + "_bwd_dh", tm=1024, tp=D_MODEL, tn=FF_SHARD, out_dtype=F32, rider=src.ride(tag + "_bwd_dh"))
    return _norm_bwd(dh, x, g, dx, name=tag + "_bwd_norm")


def _mix_fwd(x, small, lb, cos, sin, src, l):
    tag = f"l{l}_mix"
    w = {}
    h = _norm_fwd(x, small["mix_norm"], name=tag + "_norm")
    w["w_in"] = src.weight(l, "w_in")
    proj = _mm_nn(h, w["w_in"], name=tag + "_in", tm=1024, tn=896, out_dtype=F32, rider=src.ride(tag + "_in"))
    oscan, oa, sall = _hgrn_fwd(proj, lb, small["hgrn_out_norm"], name=tag + "_hgrn", rider=src.ride(tag + "_hgrn"))
    qk = _qk_fwd(proj, cos, sin, small["attn_q_norm"], small["attn_k_norm"], name=tag + "_qk")
    outs, lses = [], []
    for g in range(ATT_GROUPS):
        o, lse = _attn_fwd(qk[g], qk[3 + g], qk[6 + g], g, name=f"{tag}_attn{g}")
        outs.append(o)
        lses.append(lse)
    ob = _merge_fwd(outs, lses, name=tag + "_merge")
    w.update({n: src.weight(l, n) for n in ("w_branch_a", "w_branch_b", "w_out")})
    y, merged, ya, yb = _mix_tail_fwd(oa, ob, proj, x, w["w_branch_a"], w["w_branch_b"], w["w_out"], name=tag + "_tail")
    return y, (x, h, proj, oscan, oa, sall, qk, outs, lses, ob, ya, yb, merged, w)


def _mix_bwd(dx, saved, small, lb, cos, sin, src, l, lb_live):
    tag = f"l{l}_mix"
    x, h, proj, oscan, oa, sall, qk, outs, lses, ob, ya, yb, merged, w = saved
    g_wout = _mm_tn(merged, dx, nb=1, name=tag + "_bwd_wout", tm=1024, tk=D_MODEL, tn=D_MODEL)
    dya, dyb, dgab, doa, dob = _mix_tail_bwd(dx, proj, ya, yb, w["w_branch_a"], w["w_branch_b"], w["w_out"], name=tag + "_bwd_tail")
    g_wa = _mm_tn(oa, dya, nb=1, name=tag + "_bwd_wa", tm=1024, tk=D_MODEL, tn=D_MODEL)
    g_wb = _mm_tn(ob, dyb, nb=N_CHIPS, name=tag + "_bwd_wb", tm=2048, tk=ATT_GW, tn=256)
    mb = _merge_bwd(dob, outs, lses, name=tag + "_bwd_merge")
    dqk, dvs = [None] * 6, []
    for g in range(ATT_GROUPS):
        dq, dk, dv = _attn_bwd(qk[g], qk[3 + g], qk[6 + g], mb[g], lses[g], mb[3 + g], g, name=f"{tag}_bwd_attn{g}")
        dqk[g], dqk[3 + g] = dq, dk
        dvs.append(dv)
    dqk_cols, dqn, dkn = _qk_bwd(dqk, proj, cos, sin, small["attn_q_norm"], small["attn_k_norm"], name=tag + "_bwd_qk")
    dh4, dgn, dlb = _hgrn_bwd(doa, oscan, proj, sall, lb, small["hgrn_out_norm"], name=tag + "_bwd_hgrn", precise=lb_live,
                              rider=src.ride(tag + "_bwd_hgrn"))
    dproj = _assemble_dproj(dh4, dqk_cols, dvs, dgab, name=tag + "_bwd_cat")
    src.grads(l, dict(w_branch_a=g_wa.reshape(N_CHIPS, D_MODEL // N_CHIPS, D_MODEL), w_branch_b=g_wb,
                      w_out=g_wout.reshape(N_CHIPS, D_MODEL // N_CHIPS, D_MODEL)))
    g_win = _mm_tn(h, dproj, nb=N_CHIPS, name=tag + "_bwd_win", tm=2048, tk=D_MODEL, tn=896, rider=src.ride(tag + "_bwd_win"))
    src.grads(l, dict(w_in=g_win))
    dh = _mm_nt(dproj, w["w_in"], name=tag + "_bwd_dh", tm=1024, tp=D_MODEL, tn=2688, out_dtype=F32, rider=src.ride(tag + "_bwd_dh"))
    dx, dg = _norm_bwd(dh, x, small["mix_norm"], dx, name=tag + "_bwd_norm")
    return dx, dict(mix_norm=dg, hgrn_out_norm=dgn, lb=dlb, attn_q_norm=dqn, attn_k_norm=dkn)


BIG = ("ffn1_w_in", "ffn1_w_out", "w_in", "w_branch_a", "w_branch_b", "w_out", "ffn2_w_in", "ffn2_w_out")
ROW_SHARDED = ("ffn1_w_out", "w_branch_a", "w_out", "ffn2_w_out")
SMALL = ("ffn1_norm", "mix_norm", "hgrn_lb_logits", "hgrn_out_norm", "attn_q_norm", "attn_k_norm", "ffn2_norm")
WEIGHTS = ("ffn1_norm", "ffn1_w_in", "ffn1_w_out", "mix_norm", "w_in", "hgrn_lb_logits", "hgrn_out_norm", "attn_q_norm",
           "attn_k_norm", "w_branch_a", "w_branch_b", "w_out", "ffn2_norm", "ffn2_w_in", "ffn2_w_out")
SMALL_ROWS = 8


def _matmul_ready(name, a):
    return a.reshape(1, a.shape[0] * a.shape[1], a.shape[2]) if name in ROW_SHARDED else a


def _layer_small(small, l):
    s = {n: small[n][l].reshape(1, D_MODEL) for n in ("ffn1_norm", "mix_norm", "hgrn_out_norm", "ffn2_norm")}
    s.update({n: small[n][l] for n in ("attn_q_norm", "attn_k_norm")})
    return s


def _local_step(x, target, small, src):
    t = x.shape[0]
    cos, sin = _rope_tables(t)
    lbs = _lower_bounds(small["hgrn_lb_logits"])
    saved = []
    for l in range(2):
        sm = _layer_small(small, l)
        lb = lbs[l].reshape(1, D_MODEL)
        x, s1 = _ffn_fwd(x, sm["ffn1_norm"], src, l, "ffn1")
        x, s2 = _mix_fwd(x, sm, lb, cos, sin, src, l)
        x, s3 = _ffn_fwd(x, sm["ffn2_norm"], src, l, "ffn2")
        saved.append((sm, lb, s1, s2, s3))
    dx, sq = _loss_fwd_bwd(x, target, name="loss")
    small_rows = [None, None]
    for l in (1, 0):
        sm, lb, s1, s2, s3 = saved[l]
        dx, dg2 = _ffn_bwd(dx, s3, sm["ffn2_norm"], src, l, "ffn2")
        dx, g = _mix_bwd(dx, s2, sm, lb, cos, sin, src, l, lb_live=l > 0)
        dx, dg1 = _ffn_bwd(dx, s1, sm["ffn1_norm"], src, l, "ffn1")
        pad = lambda a: jnp.pad(a[:ATT_GROUPS].reshape(1, ATT_GROUPS * HEAD), ((0, 0), (0, D_MODEL - ATT_GROUPS * HEAD)))
        small_rows[l] = jnp.concatenate(
            [dg1, g["mix_norm"], g["lb"], g["hgrn_out_norm"], pad(g["attn_q_norm"]), pad(g["attn_k_norm"]), dg2,
             jnp.zeros((SMALL_ROWS - 7, D_MODEL), F32)], axis=0)
    return jnp.sum(sq), dx, jnp.concatenate(small_rows, axis=0)


def _coords():
    return lax.axis_index("x"), lax.axis_index("y"), lax.axis_index("c")


def _other_chips(x, y):
    return [(1 - x, y), (x, 1 - y), (1 - x, 1 - y)]


def _half_rows(rows, which):
    return pl.ds(which * (rows // 2), rows // 2)


def _gather_rider(shards):
    n = len(shards)

    def copies(w, full, sems):
        send, recv, fsend, frecv = sems
        x, y, c = _coords()
        slot = 2 * x + y
        chips = _other_chips(x, y)

        def copy(i, j, blk, src, pair, to):
            return pltpu.make_async_remote_copy(src_ref=src, dst_ref=blk, send_sem=pair[0].at[i * 3 + j],
                                                recv_sem=pair[1].at[i * 3 + j], device_id=to, device_id_type=MESH)

        def block(i, chip_slot, core):
            return full[i].at[chip_slot, _half_rows(shards[i].shape[0], core)]

        pairs = [(i, j, chip) for i in range(n) for j, chip in enumerate(chips)]

        def first():
            return [copy(i, j, block(i, slot, c), w[i].at[_half_rows(shards[i].shape[0], c)], (send, recv), (*chip, c))
                    for i, j, chip in pairs]

        def landed(core, pair):
            return [copy(i, j, block(i, 2 * chip[0] + chip[1], core), block(i, 2 * chip[0] + chip[1], core), pair, (x, y, 1 - c))
                    for i, j, chip in pairs]

        return first, landed

    def begin(w, full, sems):
        for cp in copies(w, full, sems)[0]():
            cp.start()

    def end(w, full, sems):
        first, landed = copies(w, full, sems)
        forwards = landed(lax.axis_index("c"), sems[2:])
        for arrival, forward in zip(landed(lax.axis_index("c"), sems[:2]), forwards):
            arrival.wait_recv()
            forward.start()
        for cp in landed(1 - lax.axis_index("c"), sems[2:]):
            cp.wait_recv()
        for cp in first() + forwards:
            cp.wait_send()

    out_shape = [jax.ShapeDtypeStruct((N_CHIPS,) + s.shape, s.dtype) for s in shards]
    return _Rider(shards, out_shape, [pltpu.SemaphoreType.DMA((3 * n,))] * 4, begin, end)


N_RECV = 7


def _scatter_rider(parts):
    n = len(parts)

    def copies(p, out, sems):
        send, recv = sems
        x, y, c = _coords()
        slot = 2 * x + y
        chips = _other_chips(x, y)

        def arrivals():
            return [pltpu.make_async_remote_copy(
                src_ref=out[i].at[k], dst_ref=out[i].at[k], send_sem=send.at[0], recv_sem=recv.at[i * N_RECV + k],
                device_id=(x, y, c), device_id_type=MESH) for i in range(n) for k in range(N_RECV)]

        sends = []
        for i in range(n):
            rows = parts[i].shape[1]
            for j, chip in enumerate(chips):
                for core in (0, 1):
                    sends.append(pltpu.make_async_remote_copy(
                        src_ref=p[i].at[2 * chip[0] + chip[1], _half_rows(rows, core)], dst_ref=out[i].at[2 * j + c],
                        send_sem=send.at[i * N_RECV + 2 * j + core], recv_sem=recv.at[i * N_RECV + 2 * j + c],
                        device_id=(*chip, core), device_id_type=MESH))
            sends.append(pltpu.make_async_remote_copy(
                src_ref=p[i].at[slot, _half_rows(rows, 1 - c)], dst_ref=out[i].at[6], send_sem=send.at[i * N_RECV + 6],
                recv_sem=recv.at[i * N_RECV + 6], device_id=(x, y, 1 - c), device_id_type=MESH))
        return sends, arrivals

    def begin(p, out, sems):
        for cp in copies(p, out, sems)[0]:
            cp.start()

    def end(p, out, sems):
        sends, arrivals = copies(p, out, sems)
        for cp in arrivals():
            cp.wait_recv()
        for cp in sends:
            cp.wait_send()

    out_shape = [jax.ShapeDtypeStruct((N_RECV, a.shape[1] // 2, a.shape[2]), a.dtype) for a in parts]
    return _Rider(parts, out_shape, [pltpu.SemaphoreType.DMA((N_RECV * n,))] * 2, begin, end)


def _run_alone(rider, name):
    _pcall(lambda: None, grid=(), in_specs=[], out_specs=[], out_shape=[], name=name, sem=(), args=(), rider=rider)
    return rider.result


def _sum_partials(own, parts, name):
    r, wd = own.shape
    tm = next(t for t in (256, 128, 64, 32, 16) if r % t == 0)

    def body(own_ref, p_ref, o_ref):
        acc = own_ref[...].astype(F32)
        for k in range(N_RECV):
            acc = acc + p_ref[k].astype(F32)
        o_ref[...] = acc

    return pl.pallas_call(
        body, grid=(r // tm,),
        in_specs=[pl.BlockSpec((tm, wd), lambda i: (i, 0)), pl.BlockSpec((N_RECV, tm, wd), lambda i: (0, i, 0))],
        out_specs=pl.BlockSpec((tm, wd), lambda i: (i, 0)), out_shape=jax.ShapeDtypeStruct((r, wd), F32),
        name=name, compiler_params=_params(("parallel",)))(own, parts)


def _exchange_halves(reduced, name):
    n = len(reduced)

    def body(*refs):
        r, out = refs[:n], refs[n:2 * n]
        send, recv = refs[2 * n:]
        x, y, c = _coords()
        sib = [pltpu.make_async_remote_copy(src_ref=r[i], dst_ref=out[i], send_sem=send.at[i], recv_sem=recv.at[i],
                                            device_id=(x, y, 1 - c), device_id_type=MESH) for i in range(n)]
        for cp in sib:
            cp.start()
        for cp in sib:
            cp.wait_recv()
        for cp in sib:
            cp.wait_send()

    out_shape = [jax.ShapeDtypeStruct(a.shape, a.dtype) for a in reduced]
    return pl.pallas_call(body, in_specs=[ANY] * n, out_specs=[ANY] * n, out_shape=out_shape,
                          scratch_shapes=[pltpu.SemaphoreType.DMA((n,))] * 2, name=name)(*reduced)


def _reduce_finish(parts, recv, tag):
    x, y, c = _coords()
    slot = 2 * x + y
    halves = []
    for i, (p, r) in enumerate(zip(parts, recv)):
        half = p.shape[1] // 2
        own = lax.dynamic_slice(p, (slot, c * half, 0), (1, half, p.shape[2]))[0]
        halves.append(_sum_partials(own, r, name=f"{tag}_sum{i}"))
    theirs = _exchange_halves(halves, name=tag + "_exchange")
    return [jnp.where(c == 0, jnp.concatenate([h, t], axis=0), jnp.concatenate([t, h], axis=0)) for h, t in zip(halves, theirs)]


GATHER_RIDES = {
    "l0_ffn1_in_act": ((0, "w_in"),),
    "l0_ffn1_out": ((0, "w_branch_a"), (0, "w_branch_b"), (0, "w_out")),
    "l0_mix_in": ((0, "ffn2_w_in"), (0, "ffn2_w_out"), (1, "ffn1_w_in"), (1, "ffn1_w_out")),
    "l0_mix_hgrn": ((1, "w_in"), (1, "w_branch_a"), (1, "w_branch_b"), (1, "w_out")),
    "l0_ffn2_in_act": ((1, "ffn2_w_in"), (1, "ffn2_w_out")),
}
ALONE_FIRST = ((0, "ffn1_w_in"), (0, "ffn1_w_out"))
SCATTER_RIDES = {
    "l1_mix_bwd_hgrn": ((1, "ffn2_w_in"), (1, "ffn2_w_out")),
    "l0_ffn2_bwd_win": ((1, "ffn1_w_in"),),
    "l0_ffn2_bwd_dh": ((1, "ffn1_w_out"), (1, "w_branch_a"), (1, "w_branch_b"), (1, "w_out")),
    "l0_mix_bwd_hgrn": ((1, "w_in"), (0, "ffn2_w_out")),
    "l0_mix_bwd_win": ((0, "ffn2_w_in"),),
    "l0_mix_bwd_dh": ((0, "w_in"),),
    "l0_ffn1_bwd_wout": ((0, "w_branch_a"), (0, "w_branch_b"), (0, "w_out")),
    "l0_ffn1_bwd_du_act": ((0, "ffn1_w_out"),),
    "l0_ffn1_bwd_dh": ((0, "ffn1_w_in"),),
}


class _Exchange:
    def __init__(self, shards):
        self.shards = shards
        self.pending = []
        self.full = {}
        self.parts = {}
        self.recv = {}

    def _gather(self, keys):
        return _gather_rider([self.shards[n][l] for l, n in keys]), "gather", list(keys)

    def _scatter(self, keys):
        return _scatter_rider([self.parts[k] for k in keys]), "scatter", list(keys)

    def _unpack(self):
        slot = 2 * lax.axis_index("x") + lax.axis_index("y")
        waiting = []
        for rider, kind, keys in self.pending:
            if rider.result is None:
                waiting.append((rider, kind, keys))
            elif kind == "gather":
                for (l, n), got in zip(keys, rider.result):
                    self.full[(l, n)] = lax.dynamic_update_slice(got, self.shards[n][l][None], (slot, 0, 0))
            else:
                self.recv.update(zip(keys, rider.result))
        self.pending = waiting

    def ride(self, host):
        if host in GATHER_RIDES:
            self.pending.append(self._gather(GATHER_RIDES[host]))
        elif host in SCATTER_RIDES:
            self.pending.append(self._scatter(SCATTER_RIDES[host]))
        else:
            return None
        return self.pending[-1][0]

    def weight(self, l, name):
        self._unpack()
        if (l, name) not in self.full:
            assert (l, name) in ALONE_FIRST, (l, name)
            job = self._gather(ALONE_FIRST)
            _run_alone(job[0], name="gather_first")
            self.pending.append(job)
            self._unpack()
        return _matmul_ready(name, self.full[(l, name)])

    def grads(self, l, partials):
        self.parts.update({(l, n): a for n, a in partials.items()})

    def reduce(self):
        self._unpack()
        assert not self.pending and set(self.recv) == set(self.parts)
        out = {}
        for l in range(2):
            done = _reduce_finish([self.parts[(l, n)] for n in BIG], [self.recv[(l, n)] for n in BIG], f"reduce_l{l}")
            out[l] = dict(zip(BIG, done))
        return {n: jnp.stack([out[0][n], out[1][n]], axis=0) for n in BIG}


def _all_reduce_small(rows):
    r = rows.shape[0]

    def body(x_ref, o_ref, buf, send, recv):
        x, y, c = _coords()
        me = 4 * x + 2 * y + c
        buf[me] = x_ref[...]
        copies = []
        for k in range(1, 8):
            peer = (x ^ (k >> 2), y ^ ((k >> 1) & 1), c ^ (k & 1))
            cp = pltpu.make_async_remote_copy(src_ref=x_ref, dst_ref=buf.at[me], send_sem=send.at[k - 1], recv_sem=recv.at[me],
                                              device_id=peer, device_id_type=MESH)
            cp.start()
            copies.append(cp)
        for k in range(1, 8):
            src = 4 * (x ^ (k >> 2)) + 2 * (y ^ ((k >> 1) & 1)) + (c ^ (k & 1))
            pltpu.make_async_remote_copy(src_ref=x_ref, dst_ref=buf.at[src], send_sem=send.at[0], recv_sem=recv.at[src],
                                         device_id=(x, y, c), device_id_type=MESH).wait_recv()
        for cp in copies:
            cp.wait_send()
        acc = buf[0]
        for k in range(1, 8):
            acc = acc + buf[k]
        o_ref[...] = acc

    vm = pl.BlockSpec(memory_space=pltpu.VMEM)
    return pl.pallas_call(
        body, in_specs=[vm], out_specs=vm, out_shape=jax.ShapeDtypeStruct(rows.shape, F32),
        scratch_shapes=[pltpu.VMEM((8, r, D_MODEL), F32), pltpu.SemaphoreType.DMA((7,)), pltpu.SemaphoreType.DMA((8,))],
        name="all_reduce_small")(rows)


def _adamw_math(w, g, m, v):
    m = ADAM_B1 * m + (1.0 - ADAM_B1) * g
    v = ADAM_B2 * v + (1.0 - ADAM_B2) * (g * g)
    m_hat = m / (1.0 - ADAM_B1 ** ADAM_STEP)
    v_hat = v / (1.0 - ADAM_B2 ** ADAM_STEP)
    return -ADAM_LR * (m_hat / (jnp.sqrt(v_hat) + ADAM_EPS) + ADAM_WD * w), m, v


def _adamw(w, g, m, v, name):
    shape = w.shape
    cols = shape[-1]
    flat = lambda a: a.reshape(-1, cols)
    rows = flat(w).shape[0]
    tm = 128 if rows % 128 == 0 else rows
    ins = [('t', flat(a), cols, 0) for a in (w, g, m, v)]
    res = _ew(_adamw_math, ins, [('t', cols, F32)] * 3, rows=rows, tm=tm, name=name)
    return [a.reshape(shape) for a in res]


def _small_update(sums, logits, w, m, v):
    def body(s_ref, lg_ref, w_ref, m_ref, v_ref, g_ref, d_ref, nm_ref, nv_ref):
        s = s_ref[...]
        l0, l1 = lg_ref[0:1, :], lg_ref[1:2, :]
        mx = jnp.maximum(l0, l1)
        e0, e1 = jnp.exp(l0 - mx), jnp.exp(l1 - mx)
        sm0, sm1 = e0 / (e0 + e1), e1 / (e0 + e1)
        dl1 = s_ref[SMALL_ROWS + 2:SMALL_ROWS + 3, :] * sm0 * sm1
        row = lax.broadcasted_iota(jnp.int32, s.shape, 0)
        g = jnp.where(row == 2, -dl1, jnp.where(row == SMALL_ROWS + 2, dl1, s))
        d, nm, nv = _adamw_math(w_ref[...], g, m_ref[...], v_ref[...])
        g_ref[...] = g
        d_ref[...] = d
        nm_ref[...] = nm
        nv_ref[...] = nv

    vm = pl.BlockSpec(memory_space=pltpu.VMEM)
    return pl.pallas_call(body, in_specs=[vm] * 5, out_specs=[vm] * 4,
                          out_shape=[jax.ShapeDtypeStruct(sums.shape, F32)] * 4, name="small_update")(sums, logits, w, m, v)


def _pack_small(vals):
    rows = []
    for l in range(2):
        for n in ("ffn1_norm", "mix_norm", "hgrn_lb_logits", "hgrn_out_norm", "attn_q_norm", "attn_k_norm", "ffn2_norm"):
            a = vals[n][l].reshape(1, -1)
            rows.append(jnp.pad(a, ((0, 0), (0, D_MODEL - a.shape[1]))))
        rows.append(jnp.zeros((SMALL_ROWS - 7, D_MODEL), F32))
    return jnp.concatenate(rows, axis=0)


def _unpack_small(packed):
    out = {}
    for k, n in enumerate(("ffn1_norm", "mix_norm", "hgrn_lb_logits", "hgrn_out_norm", "attn_q_norm", "attn_k_norm", "ffn2_norm")):
        a = jnp.stack([packed[k], packed[SMALL_ROWS + k]], axis=0)
        out[n] = a[:, :ATT_GROUPS * HEAD].reshape(2, ATT_GROUPS, HEAD) if n.startswith("attn") else a
    return out


def kernel(x, ffn1_norm, ffn1_w_in, ffn1_w_out, mix_norm, w_in, hgrn_lb_logits, hgrn_out_norm, attn_q_norm, attn_k_norm, w_branch_a, w_branch_b, w_out, ffn2_norm, ffn2_w_in, ffn2_w_out, loss_target, m_ffn1_norm, m_ffn1_w_in, m_ffn1_w_out, m_mix_norm, m_w_in, m_hgrn_lb_logits, m_hgrn_out_norm, m_attn_q_norm, m_attn_k_norm, m_w_branch_a, m_w_branch_b, m_w_out, m_ffn2_norm, m_ffn2_w_in, m_ffn2_w_out, v_ffn1_norm, v_ffn1_w_in, v_ffn1_w_out, v_mix_norm, v_w_in, v_hgrn_lb_logits, v_hgrn_out_norm, v_attn_q_norm, v_attn_k_norm, v_w_branch_a, v_w_branch_b, v_w_out, v_ffn2_norm, v_ffn2_w_in, v_ffn2_w_out):
    a = locals()
    w = {n: a[n] for n in WEIGHTS}
    m = {n: a["m_" + n] for n in WEIGHTS}
    v = {n: a["v_" + n] for n in WEIGHTS}

    exchange = _Exchange({n: w[n].astype(BF16) for n in BIG})
    small = {n: w[n] for n in SMALL}
    sq, grad_x, small_rows = _local_step(x[0], loss_target[0], small, exchange)
    loss = lax.psum(sq, ("x", "y", "c")) * (0.5 / D_MODEL)
    grads = exchange.reduce()

    sums = _all_reduce_small(small_rows)
    g_s, d_s, m_s, v_s = _small_update(sums, w["hgrn_lb_logits"], _pack_small(small), _pack_small({n: m[n] for n in SMALL}),
                                       _pack_small({n: v[n] for n in SMALL}))
    grads.update(_unpack_small(g_s))
    delta, new_m, new_v = _unpack_small(d_s), _unpack_small(m_s), _unpack_small(v_s)
    for n in BIG:
        delta[n], new_m[n], new_v[n] = _adamw(w[n], grads[n], m[n], v[n], name="adamw_" + n)

    return (loss, grad_x[None], *[grads[n] for n in WEIGHTS], *[delta[n] for n in WEIGHTS],
            *[new_m[n] for n in WEIGHTS], *[new_v[n] for n in WEIGHTS])
```

```python
import functools

import jax
import jax.numpy as jnp
from jax import lax
from jax.experimental import pallas as pl
from jax.experimental.pallas import tpu as pltpu

F32 = jnp.float32
BF16 = jnp.bfloat16
MESH = pl.DeviceIdType.MESH

D_MODEL = 1024
D_FF = 2816
N_CHIPS = 4
HEAD = 128
HG_HEADS = 8
HG_CHUNK = 64
ATT_GROUPS = 3
ATT_HEADS = 4
ATT_GW = ATT_HEADS * HEAD
DILATIONS = (1, 4, 16)
ATT_BLK = 128
ATT_STEP_BLOCKS = 4
P_IN = 10752
CB_AQ, CB_AK, CB_AV, CB_GA, CB_GB = 8, 11, 14, 17, 19
EPS = 1e-6
ROPE_THETA = 10000.0
ADAM_LR, ADAM_B1, ADAM_B2, ADAM_EPS, ADAM_WD, ADAM_STEP = 0.001, 0.9, 0.999, 1e-08, 0.01, 10
VMEM_LIMIT_V7X = 56 * 1024 * 1024
NEG = -1e30


def _params(sem):
    return pltpu.CompilerParams(dimension_semantics=sem, vmem_limit_bytes=VMEM_LIMIT_V7X)


def _sig(x):
    return 1.0 / (1.0 + jnp.exp(-x))


def _dot(a, b):
    return jnp.dot(a, b, preferred_element_type=F32)


def _dot_nt(a, b):
    return lax.dot_general(a, b, (((1,), (1,)), ((), ())), preferred_element_type=F32)


def _dot_tn(a, b):
    return lax.dot_general(a, b, (((0,), (0,)), ((), ())), preferred_element_type=F32)


def _bf(x):
    return x.astype(BF16)


ANY = pl.BlockSpec(memory_space=pl.ANY)


class _Rider:
    def __init__(self, args, out_shape, sems, begin, end):
        self.args, self.out_shape, self.sems, self.begin, self.end = list(args), list(out_shape), list(sems), begin, end
        self.result = None


def _pcall(body, *, grid, in_specs, out_specs, out_shape, name, sem, args, scratch_shapes=(), rider=None):
    multi = isinstance(out_shape, (list, tuple))
    o_specs = list(out_specs) if multi else [out_specs]
    o_shape = list(out_shape) if multi else [out_shape]
    if rider is None:
        res = pl.pallas_call(body, grid=grid, in_specs=list(in_specs), out_specs=o_specs, out_shape=o_shape,
                             scratch_shapes=list(scratch_shapes), name=name, compiler_params=_params(sem))(*args)
        return list(res) if multi else res[0]
    counts = [len(in_specs), len(rider.args), len(o_specs), len(rider.out_shape), len(scratch_shapes)]

    def wrapped(*refs):
        groups, at = [], 0
        for c in counts:
            groups.append(refs[at:at + c])
            at += c
        h_in, r_in, h_out, r_out, h_scratch = groups
        r_sems = refs[at:]
        if grid:
            ids = [pl.program_id(a) for a in range(len(grid))]
            first = functools.reduce(jnp.logical_and, [i == 0 for i in ids])
            last = functools.reduce(jnp.logical_and, [i == g - 1 for i, g in zip(ids, grid)])
            pl.when(first)(lambda: rider.begin(r_in, r_out, r_sems))
            body(*h_in, *h_out, *h_scratch)
            pl.when(last)(lambda: rider.end(r_in, r_out, r_sems))
        else:
            rider.begin(r_in, r_out, r_sems)
            body(*h_in, *h_out, *h_scratch)
            rider.end(r_in, r_out, r_sems)

    res = pl.pallas_call(
        wrapped, grid=grid, in_specs=list(in_specs) + [ANY] * counts[1], out_specs=o_specs + [ANY] * counts[3],
        out_shape=o_shape + rider.out_shape, scratch_shapes=list(scratch_shapes) + rider.sems, name=name,
        compiler_params=_params(("arbitrary",) * len(grid)))(*args, *rider.args)
    rider.result = list(res[counts[2]:])
    return list(res[:counts[2]]) if multi else res[0]


def _mm_nn(a, b3, *, name, tm, tn, out_dtype, res=None, alpha=1.0, rider=None):
    m, k = a.shape
    nb, _, nw = b3.shape
    per = nw // tn
    assert nw % tn == 0 and m % tm == 0
    has_res = res is not None

    def body(*refs):
        if has_res:
            a_ref, b_ref, r_ref, o_ref = refs
        else:
            a_ref, b_ref, o_ref = refs
        acc = _dot(_bf(a_ref[...]), b_ref[...])
        if alpha != 1.0:
            acc = alpha * acc
        if has_res:
            acc = r_ref[...] + acc
        o_ref[...] = acc.astype(o_ref.dtype)

    in_specs = [pl.BlockSpec((tm, k), lambda i, j: (i, 0)),
                pl.BlockSpec((None, k, tn), lambda i, j: (j // per, 0, j % per))]
    args = [a, b3]
    if has_res:
        in_specs.append(pl.BlockSpec((tm, tn), lambda i, j: (i, j)))
        args.append(res)
    return _pcall(body, grid=(m // tm, nb * per), in_specs=in_specs, out_specs=pl.BlockSpec((tm, tn), lambda i, j: (i, j)),
                  out_shape=jax.ShapeDtypeStruct((m, nb * nw), out_dtype), name=name, sem=("parallel", "arbitrary"),
                  args=args, rider=rider)


def _mm_nt(d, b3, *, name, tm, tp, tn, out_dtype, alpha=1.0, rider=None):
    m, n = d.shape
    nb, p, nw = b3.shape
    per = nw // tn
    nk = n // tn
    assert nb * nw == n and nw % tn == 0 and p % tp == 0 and m % tm == 0

    def body(d_ref, b_ref, o_ref, acc_ref):
        kk = pl.program_id(2)

        @pl.when(kk == 0)
        def _():
            acc_ref[...] = jnp.zeros_like(acc_ref)

        acc_ref[...] += _dot_nt(_bf(d_ref[...]), b_ref[...])

        @pl.when(kk == nk - 1)
        def _():
            o_ref[...] = (alpha * acc_ref[...]).astype(o_ref.dtype)

    return _pcall(
        body, grid=(m // tm, p // tp, nk),
        in_specs=[pl.BlockSpec((tm, tn), lambda i, j, kk: (i, kk)),
                  pl.BlockSpec((None, tp, tn), lambda i, j, kk: (kk // per, j, kk % per))],
        out_specs=pl.BlockSpec((tm, tp), lambda i, j, kk: (i, j)),
        out_shape=jax.ShapeDtypeStruct((m, p), out_dtype),
        scratch_shapes=[pltpu.VMEM((tm, tp), F32)],
        name=name, sem=("parallel", "parallel", "arbitrary"), args=(d, b3), rider=rider)


def _mm_tn(a, d, *, nb, name, tm, tk, tn, alpha=1.0, rider=None):
    m, k = a.shape
    _, n = d.shape
    nw = n // nb
    per = nw // tn
    nm = m // tm
    assert nw % tn == 0 and k % tk == 0 and m % tm == 0

    def body(a_ref, d_ref, o_ref, acc_ref):
        mm = pl.program_id(2)

        @pl.when(mm == 0)
        def _():
            acc_ref[...] = jnp.zeros_like(acc_ref)

        acc_ref[...] += _dot_tn(_bf(a_ref[...]), _bf(d_ref[...]))

        @pl.when(mm == nm - 1)
        def _():
            o_ref[...] = (alpha * acc_ref[...]).astype(o_ref.dtype)

    return _pcall(
        body, grid=(k // tk, nb * per, nm),
        in_specs=[pl.BlockSpec((tm, tk), lambda i, j, mm: (mm, i)),
                  pl.BlockSpec((tm, tn), lambda i, j, mm: (mm, j))],
        out_specs=pl.BlockSpec((None, tk, tn), lambda i, j, mm: (j // per, i, j % per)),
        out_shape=jax.ShapeDtypeStruct((nb, k, nw), BF16),
        scratch_shapes=[pltpu.VMEM((tk, tn), F32)],
        name=name, sem=("parallel", "parallel", "arbitrary"), args=(a, d), rider=rider)


def _ew(fn, ins, outs, *, rows, tm, name):
    in_specs, args, scratch = [], [], []
    for s in ins:
        if s[0] == 't':
            _, arr, w, cb = s
            in_specs.append(pl.BlockSpec((tm, w), lambda i, cb=cb: (i, cb)))
        elif s[0] == 'v':
            _, arr, w, d = s
            in_specs.append(pl.BlockSpec((tm // d, d * w), lambda i: (i, 0)))
            scratch.append(pltpu.VMEM((w // HEAD, tm, HEAD), F32))
        else:
            arr = s[1]
            in_specs.append(pl.BlockSpec(arr.shape, lambda i, nd=arr.ndim: (0,) * nd))
        args.append(arr)
    out_specs, out_shape = [], []
    for s in outs:
        if s[0] == 't':
            _, w, dt = s
            out_specs.append(pl.BlockSpec((tm, w), lambda i: (i, 0)))
            out_shape.append(jax.ShapeDtypeStruct((rows, w), dt))
        elif s[0] == 'v':
            _, w, dt, d = s
            out_specs.append(pl.BlockSpec((tm // d, d * w), lambda i: (i, 0)))
            out_shape.append(jax.ShapeDtypeStruct((rows // d, d * w), dt))
            scratch.append(pltpu.VMEM((w // HEAD, tm, HEAD), F32))
        else:
            out_specs.append(pl.BlockSpec(s[1], lambda i: (0, 0)))
            out_shape.append(jax.ShapeDtypeStruct(s[1], F32))
    n_in, n_out = len(ins), len(outs)

    def body(*refs):
        bufs = list(refs[n_in + n_out:])
        vals = []
        for r, s in zip(refs[:n_in], ins):
            if s[0] == 'v':
                w, d, buf = s[2], s[3], bufs.pop(0)
                for k in range(d):
                    for c in range(w // HEAD):
                        lanes = slice(k * w + c * HEAD, k * w + (c + 1) * HEAD)
                        buf.at[c][pl.ds(k, tm // d, stride=d), :] = r[:, lanes].astype(F32)
                vals.append(_cat([buf[c] for c in range(w // HEAD)]))
            else:
                vals.append(r[...])
        res = fn(*vals)
        if not isinstance(res, (tuple, list)):
            res = (res,)
        for r, s, v in zip(refs[n_in:n_in + n_out], outs, res):
            if s[0] == 't':
                r[...] = v.astype(r.dtype)
            elif s[0] == 'v':
                w, d, buf = s[1], s[3], bufs.pop(0)
                for c in range(w // HEAD):
                    buf[c] = v[:, c * HEAD:(c + 1) * HEAD].astype(F32)
                for k in range(d):
                    for c in range(w // HEAD):
                        lanes = slice(k * w + c * HEAD, k * w + (c + 1) * HEAD)
                        r[:, lanes] = buf.at[c][pl.ds(k, tm // d, stride=d), :].astype(r.dtype)
            else:
                @pl.when(pl.program_id(0) == 0)
                def _(r=r):
                    r[...] = jnp.zeros_like(r)

                r[...] += v

    res = pl.pallas_call(
        body, grid=(rows // tm,), in_specs=in_specs, out_specs=out_specs, out_shape=out_shape, scratch_shapes=scratch,
        name=name, compiler_params=_params(("arbitrary",)))(*args)
    return res


def _tile(arr, w, g):
    return ('t', arr, w, 0) if DILATIONS[g] == 1 else ('v', arr, w, DILATIONS[g])


def _tile_out(w, dtype, g):
    return ('t', w, dtype) if DILATIONS[g] == 1 else ('v', w, dtype, DILATIONS[g])


def _heads(x):
    return [x[:, h * HEAD:(h + 1) * HEAD] for h in range(x.shape[1] // HEAD)]


def _cat(xs):
    return jnp.concatenate(xs, axis=1)


def _head_mean(x):
    return _cat([jnp.broadcast_to(jnp.mean(h, axis=1, keepdims=True), h.shape) for h in _heads(x)])


def _rms_rows(x):
    return lax.rsqrt(jnp.mean(x * x, axis=1, keepdims=True) + EPS)


def _norm_fwd(x, g, name):
    return _ew(lambda xv, gv: xv * _rms_rows(xv) * gv,
               [('t', x, D_MODEL, 0), ('f', g)], [('t', D_MODEL, BF16)], rows=x.shape[0], tm=512, name=name)[0]


def _norm_bwd(dh, x, g, dx, name):
    def fn(dhv, xv, gv, dxv):
        r = _rms_rows(xv)
        xh = xv * r
        dxh = dhv * gv
        out = dxv + r * (dxh - xh * jnp.mean(dxh * xh, axis=1, keepdims=True))
        return out, jnp.sum(dhv * xh, axis=0, keepdims=True)

    return _ew(fn, [('t', dh, D_MODEL, 0), ('t', x, D_MODEL, 0), ('f', g), ('t', dx, D_MODEL, 0)],
               [('t', D_MODEL, F32), ('acc', (1, D_MODEL))], rows=x.shape[0], tm=512, name=name)


def _loss_fwd_bwd(y, target, name):
    def fn(yv, tv):
        e = yv - tv
        return e * (1.0 / D_MODEL), jnp.sum(e * e, axis=0, keepdims=True)

    return _ew(fn, [('t', y, D_MODEL, 0), ('t', target, D_MODEL, 0)], [('t', D_MODEL, F32), ('acc', (1, D_MODEL))],
               rows=y.shape[0], tm=512, name=name)


def _rot(x):
    sgn = jnp.where(lax.broadcasted_iota(jnp.int32, x.shape, 1) < HEAD // 2, -1.0, 1.0)
    return pltpu.roll(x, HEAD // 2, 1) * sgn


def _gain_rows(qn, kn):
    return [a[g:g + 1] for a in (qn, kn) for g in range(ATT_GROUPS)]


def _qk_fwd(proj, cos, sin, qn, kn, name):
    def fn(*v):
        xs, cosv, sinv, gains, vs = v[:6], v[6], v[7], v[8:14], v[14:17]
        outs = []
        for j, x in enumerate(xs):
            gain = gains[j]
            ys = []
            for xh in _heads(x):
                xn = xh * _rms_rows(xh) * gain
                ys.append(xn * cosv + _rot(xn) * sinv)
            outs.append(_cat(ys))
        return outs + list(vs)

    ins = ([('t', proj, 512, CB_AQ + j) for j in range(6)] + [('t', cos, HEAD, 0), ('t', sin, HEAD, 0)]
           + [('f', a) for a in _gain_rows(qn, kn)] + [('t', proj, 512, CB_AV + g) for g in range(ATT_GROUPS)])
    return _ew(fn, ins, [_tile_out(ATT_GW, BF16, j % ATT_GROUPS) for j in range(9)], rows=proj.shape[0], tm=512, name=name)


def _qk_bwd(dqk, proj, cos, sin, qn, kn, name):
    def fn(*v):
        ds, xs, cosv, sinv, gains = v[:6], v[6:12], v[12], v[13], v[14:20]
        rows8 = lax.broadcasted_iota(jnp.int32, (8, HEAD), 0)
        outs, dgs = [], [jnp.zeros((8, HEAD), F32)] * 2
        for j in range(6):
            gain = gains[j]
            dx, dg = [], jnp.zeros((1, HEAD), F32)
            for dyh, xh in zip(_heads(ds[j]), _heads(xs[j])):
                r = _rms_rows(xh)
                xhat = xh * r
                dxn = dyh * cosv - _rot(dyh * sinv)
                dg = dg + jnp.sum(dxn * xhat, axis=0, keepdims=True)
                dxh = dxn * gain
                dx.append(r * (dxh - xhat * jnp.mean(dxh * xhat, axis=1, keepdims=True)))
            outs.append(_cat(dx))
            dgs[j // 3] = dgs[j // 3] + jnp.where(rows8 == j % 3, dg, 0.0)
        return _cat(outs), dgs[0], dgs[1]

    ins = ([_tile(a, ATT_GW, j % ATT_GROUPS) for j, a in enumerate(dqk)] + [('t', proj, 512, CB_AQ + j) for j in range(6)]
           + [('t', cos, HEAD, 0), ('t', sin, HEAD, 0)] + [('f', a) for a in _gain_rows(qn, kn)])
    return _ew(fn, ins, [('t', 6 * ATT_GW, BF16), ('acc', (8, HEAD)), ('acc', (8, HEAD))],
               rows=proj.shape[0], tm=256, name=name)


def _pick(x, h):
    lanes = lax.broadcasted_iota(jnp.int32, x.shape, 1)
    return jnp.sum(jnp.where(lanes == h, x, 0.0), axis=1, keepdims=True)


def _spread(x):
    return _cat([jnp.broadcast_to(_pick(x, h), (x.shape[0], HEAD)) for h in range(ATT_HEADS)])


def _compact(x):
    lanes = lax.broadcasted_iota(jnp.int32, (x.shape[0], HEAD), 1)
    out = jnp.zeros((x.shape[0], HEAD), F32)
    for h, xh in enumerate(_heads(x)):
        out = jnp.where(lanes == h, xh, out)
    return out


def _group_weights(l0, l1, l2):
    l0, l1, l2 = _spread(l0), _spread(l1), _spread(l2)
    m = jnp.maximum(jnp.maximum(l0, l1), l2)
    e0, e1, e2 = jnp.exp(l0 - m), jnp.exp(l1 - m), jnp.exp(l2 - m)
    inv = 1.0 / (e0 + e1 + e2)
    return e0 * inv, e1 * inv, e2 * inv


def _merge_fwd(outs, lses, name):
    def fn(o0, o1, o2, l0, l1, l2):
        a0, a1, a2 = _group_weights(l0, l1, l2)
        return a0 * o0 + a1 * o1 + a2 * o2

    ins = [_tile(a, ATT_GW, g) for g, a in enumerate(outs)] + [_tile(a, HEAD, g) for g, a in enumerate(lses)]
    return _ew(fn, ins, [('t', ATT_GW, BF16)], rows=outs[0].shape[0], tm=512, name=name)[0]


def _merge_bwd(dob, outs, lses, name):
    def fn(dov, o0, o1, o2, l0, l1, l2):
        a0, a1, a2 = _group_weights(l0, l1, l2)
        ob = a0 * o0 + a1 * o1 + a2 * o2
        s = _head_mean(dov * ob) * float(HEAD)
        return a0 * dov, a1 * dov, a2 * dov, _compact(a0 * s), _compact(a1 * s), _compact(a2 * s)

    ins = ([('t', dob, ATT_GW, 0)] + [_tile(a, ATT_GW, g) for g, a in enumerate(outs)]
           + [_tile(a, HEAD, g) for g, a in enumerate(lses)])
    groups = range(ATT_GROUPS)
    return _ew(fn, ins, [_tile_out(ATT_GW, BF16, g) for g in groups] + [_tile_out(HEAD, F32, g) for g in groups],
               rows=dob.shape[0], tm=512, name=name)


def _assemble_dproj(dh4, dqk, dvs, dgab, name):
    fn = lambda *v: _cat(list(v))
    ins = ([('t', dh4, 4 * D_MODEL, 0), ('t', dqk, 6 * ATT_GW, 0)] + [_tile(a, ATT_GW, g) for g, a in enumerate(dvs)]
           + [('t', dgab, 2 * D_MODEL, 0)])
    return _ew(fn, ins, [('t', P_IN, BF16)], rows=dh4.shape[0], tm=256, name=name)[0]


HG_ROWS = 256


def _hg_gates(hq, hf, hi, lbv):
    sig = _sig(hf)
    f = lbv + (1.0 - lbv) * sig
    return hq * _sig(hq), 1.0 - f, hi, jnp.log(f), sig, f


def _split3(x):
    hi = _bf(x)
    r1 = x - hi.astype(F32)
    mid = _bf(r1)
    return hi, mid, _bf(r1 - mid.astype(F32))


def _tri_dot(tri, x):
    hi, mid, lo = _split3(x)
    return _dot(tri, hi) + _dot(tri, mid) + _dot(tri, lo)


def _row(x, i):
    rows = lax.broadcasted_iota(jnp.int32, x.shape, 0)
    return jnp.sum(jnp.where(rows == i, x, 0.0), axis=0, keepdims=True)


def _hg_decay(logf, q, k):
    c = HG_CHUNK
    row = lax.broadcasted_iota(jnp.int32, (c, c), 0)
    col = lax.broadcasted_iota(jnp.int32, (c, c), 1)
    g = _tri_dot((row >= col).astype(BF16), logf)
    gm = _row(g, c // 2 - 1)
    gl = _row(g, c - 1)
    return g, gm, gl, q * jnp.exp(g), q * jnp.exp(g - gm), k * jnp.exp(gm - g), k * jnp.exp(gl - g)


def _hg_out_fwd(o, hg, gain):
    r = lax.rsqrt(_head_mean(o * o) + EPS)
    return o * r * gain * (hg * _sig(hg))


def _hgrn_fwd(proj, lb, gain, name, rider=None):
    t = proj.shape[0]
    nck = HG_ROWS // HG_CHUNK

    def body(hq_ref, hf_ref, hi_ref, hg_ref, lb_ref, gn_ref, o_ref, oa_ref, sall_ref, st_ref):
        @pl.when(pl.program_id(0) == 0)
        def _():
            st_ref[...] = jnp.zeros_like(st_ref)

        lbv = lb_ref[...]
        gnv = gn_ref[...]
        c = HG_CHUNK
        mask = lax.broadcasted_iota(jnp.int32, (c, c), 0) >= lax.broadcasted_iota(jnp.int32, (c, c), 1)

        def chunk(cc, carry):
            sl = pl.ds(pl.multiple_of(cc * c, c), c)
            q, k, v, logf, _, _ = _hg_gates(hq_ref[sl, :], hf_ref[sl, :], hi_ref[sl, :], lbv)
            _, _, gl, qg, qt, kt, kd = _hg_decay(logf, q, k)
            egl = jnp.exp(gl)
            os = []
            for h in range(HG_HEADS):
                hs = slice(h * HEAD, (h + 1) * HEAD)
                st = st_ref[h]
                sall_ref[cc, h] = st
                a = jnp.where(mask, _dot_nt(_bf(qt[:, hs]), _bf(kt[:, hs])), 0.0)
                os.append(_dot(_bf(a), _bf(v[:, hs])) + _dot_nt(_bf(qg[:, hs]), _bf(st)))
                st_ref[h] = egl[:, hs] * st + _dot_tn(_bf(v[:, hs]), _bf(kd[:, hs]))
            o = _cat(os)
            o_ref[sl, :] = o
            oa_ref[sl, :] = _hg_out_fwd(o, hg_ref[sl, :], gnv).astype(oa_ref.dtype)
            return carry

        lax.fori_loop(0, nck, chunk, 0)

    col = lambda j: pl.BlockSpec((HG_ROWS, D_MODEL), lambda i, j=j: (i, j))
    small = pl.BlockSpec((1, D_MODEL), lambda i: (0, 0))
    return _pcall(
        body, grid=(t // HG_ROWS,),
        in_specs=[col(0), col(1), col(2), col(3), small, small],
        out_specs=[col(0), col(0), pl.BlockSpec((nck, HG_HEADS, HEAD, HEAD), lambda i: (i, 0, 0, 0))],
        out_shape=[jax.ShapeDtypeStruct((t, D_MODEL), F32), jax.ShapeDtypeStruct((t, D_MODEL), BF16),
                   jax.ShapeDtypeStruct((t // HG_CHUNK, HG_HEADS, HEAD, HEAD), F32)],
        scratch_shapes=[pltpu.VMEM((HG_HEADS, HEAD, HEAD), F32)],
        name=name, sem=("arbitrary",), args=(proj, proj, proj, proj, lb, gain), rider=rider)


def _terms(x, precise):
    hi = _bf(x)
    return (hi, _bf(x - hi.astype(F32))) if precise else (hi,)


def _mm(dot, a, b):
    out = dot(a[0], b[0])
    if len(a) > 1:
        out = out + dot(a[1], b[0])
    if len(b) > 1:
        out = out + dot(a[0], b[1])
    return out


def _hgrn_bwd(doa, oscan, proj, sall, lb, gain, name, precise, rider=None):
    t = proj.shape[0]
    nck = HG_ROWS // HG_CHUNK
    nsteps = t // HG_ROWS
    terms = functools.partial(_terms, precise=precise)

    def body(doa_ref, os_ref, hq_ref, hf_ref, hi_ref, hg_ref, sall_ref, lb_ref, gn_ref,
             d4_ref, dgn_ref, dlb_ref, dst_ref):
        @pl.when(pl.program_id(0) == 0)
        def _():
            dst_ref[...] = jnp.zeros_like(dst_ref)
            dgn_ref[...] = jnp.zeros_like(dgn_ref)
            dlb_ref[...] = jnp.zeros_like(dlb_ref)

        lbv = lb_ref[...]
        gnv = gn_ref[...]
        c = HG_CHUNK
        row = lax.broadcasted_iota(jnp.int32, (c, c), 0)
        colm = lax.broadcasted_iota(jnp.int32, (c, c), 1)
        mask = row >= colm
        triu = (row <= colm).astype(BF16)
        last = lax.broadcasted_iota(jnp.int32, (c, HEAD), 0) == c - 1

        def chunk(ci, carry):
            cc = nck - 1 - ci
            sl = pl.ds(pl.multiple_of(cc * c, c), c)
            hq, hf, hg = hq_ref[sl, :], hf_ref[sl, :], hg_ref[sl, :]
            q, k, v, logf, sig, f = _hg_gates(hq, hf, hi_ref[sl, :], lbv)
            g, gm, gl, qg, qt, kt, kd = _hg_decay(logf, q, k)
            egl = jnp.exp(gl)
            o = os_ref[sl, :]
            dy = doa_ref[sl, :]
            r = lax.rsqrt(_head_mean(o * o) + EPS)
            oh = o * r
            sg = _sig(hg)
            silu_g = hg * sg
            dgn_ref[...] += jnp.sum(dy * oh * silu_g, axis=0, keepdims=True)
            dhg = dy * oh * gnv * (sg * (1.0 + hg * (1.0 - sg)))
            doh = dy * gnv * silu_g
            do = r * (doh - oh * _head_mean(doh * oh))
            dqs, dks, dvs, dgs = [], [], [], []
            for h in range(HG_HEADS):
                hs = slice(h * HEAD, (h + 1) * HEAD)
                st = sall_ref[cc, h]
                dst = dst_ref[h]
                qt_h, kt_h, qg_h, kd_h = qt[:, hs], kt[:, hs], qg[:, hs], kd[:, hs]
                do_p, v_p, qt_p, kt_p, qg_p = terms(do[:, hs]), terms(v[:, hs]), terms(qt_h), terms(kt_h), terms(qg_h)
                st_p, dst_p = terms(st), terms(dst)
                a = jnp.where(mask, _dot_nt(qt_p[0], kt_p[0]), 0.0)
                da = terms(jnp.where(mask, _mm(_dot_nt, do_p, v_p), 0.0))
                dqt = _mm(_dot, da, kt_p)
                dkt = _mm(_dot_tn, da, qt_p)
                dqg = _mm(_dot, do_p, st_p)
                dv = _dot_tn(_bf(a), do_p[0]) + _dot_nt(_bf(kd_h), dst_p[0])
                dkd = _mm(_dot, v_p, dst_p)
                dgl = egl[:, hs] * jnp.sum(st * dst, axis=0, keepdims=True) + jnp.sum(dkd * kd_h, axis=0, keepdims=True)
                dst_ref[h] = egl[:, hs] * dst + _mm(_dot_tn, do_p, qg_p)
                g_h = g[:, hs]
                gm_h = gm[:, hs]
                gl_h = gl[:, hs]
                dqs.append(dqt * jnp.exp(g_h - gm_h) + dqg * jnp.exp(g_h))
                dks.append(dkt * jnp.exp(gm_h - g_h) + dkd * jnp.exp(gl_h - g_h))
                dvs.append(dv)
                dgs.append(dqt * qt_h - dkt * kt_h + dqg * qg_h - dkd * kd_h + jnp.where(last, dgl, 0.0))
            dq, dk, dv, dg = _cat(dqs), _cat(dks), _cat(dvs), _cat(dgs)
            dlogf = _tri_dot(triu, dg)
            df = dlogf / f - dk
            dlb_ref[...] += jnp.sum(df * (1.0 - sig), axis=0, keepdims=True)
            dhf = df * (1.0 - lbv) * sig * (1.0 - sig)
            sq = _sig(hq)
            dhq = dq * (sq * (1.0 + hq * (1.0 - sq)))
            d4_ref[sl, :] = _cat([dhq, dhf, dv, dhg]).astype(d4_ref.dtype)
            return carry

        lax.fori_loop(0, nck, chunk, 0)

    rev = lambda j: pl.BlockSpec((HG_ROWS, D_MODEL), lambda i, j=j: (nsteps - 1 - i, j))
    small = pl.BlockSpec((1, D_MODEL), lambda i: (0, 0))
    return _pcall(
        body, grid=(nsteps,),
        in_specs=[rev(0), rev(0), rev(0), rev(1), rev(2), rev(3),
                  pl.BlockSpec((nck, HG_HEADS, HEAD, HEAD), lambda i: (nsteps - 1 - i, 0, 0, 0)), small, small],
        out_specs=[pl.BlockSpec((HG_ROWS, 4 * D_MODEL), lambda i: (nsteps - 1 - i, 0)), small, small],
        out_shape=[jax.ShapeDtypeStruct((t, 4 * D_MODEL), BF16), jax.ShapeDtypeStruct((1, D_MODEL), F32),
                   jax.ShapeDtypeStruct((1, D_MODEL), F32)],
        scratch_shapes=[pltpu.VMEM((HG_HEADS, HEAD, HEAD), F32)],
        name=name, sem=("arbitrary",), args=(doa, oscan, proj, proj, proj, proj, sall, lb, gain), rider=rider)


def _window_masks(has_previous):
    qi = lax.broadcasted_iota(jnp.int32, (ATT_BLK, 2 * ATT_BLK), 0)
    ki = lax.broadcasted_iota(jnp.int32, (ATT_BLK, 2 * ATT_BLK), 1)
    band = jnp.logical_and(ki >= qi, ki <= qi + ATT_BLK)
    return band, jnp.logical_and(band, jnp.logical_or(ki >= ATT_BLK, has_previous))


def _two_blocks(ref, prev_ref, j, hs):
    if j == 0:
        return jnp.concatenate([prev_ref[:, hs], ref[0:ATT_BLK, hs]], axis=0)
    return ref[(j - 1) * ATT_BLK:(j + 1) * ATT_BLK, hs]


def _attn_cfg(qg, g):
    d = DILATIONS[g]
    length = qg.shape[0]
    assert qg.shape[1] == d * ATT_GW
    nb = length // ATT_BLK
    return d, length, nb, min(ATT_STEP_BLOCKS, nb)


def _attn_fwd(qg, kg, vg, g, name):
    d, length, nb, rb = _attn_cfg(qg, g)
    scale = HEAD ** -0.5

    def body(q_ref, k_ref, v_ref, kp_ref, vp_ref, o_ref, l_ref):
        n = pl.program_id(1)
        band, first_band = _window_masks(n > 0)
        lanes = lax.broadcasted_iota(jnp.int32, (ATT_BLK, HEAD), 1)
        for j in range(rb):
            rows = slice(j * ATT_BLK, (j + 1) * ATT_BLK)
            lse = jnp.zeros((ATT_BLK, HEAD), F32)
            for h in range(ATT_HEADS):
                hs = slice(h * HEAD, (h + 1) * HEAD)
                k2, v2 = _two_blocks(k_ref, kp_ref, j, hs), _two_blocks(v_ref, vp_ref, j, hs)
                s = jnp.where(first_band if j == 0 else band, _dot_nt(q_ref[rows, hs], k2) * scale, NEG)
                m = jnp.max(s, axis=1, keepdims=True)
                p = jnp.exp(s - m)
                l = jnp.sum(p, axis=1, keepdims=True)
                o_ref[rows, hs] = (_dot(_bf(p), v2) / l).astype(o_ref.dtype)
                lse = jnp.where(lanes == h, m + jnp.log(l), lse)
            l_ref[rows, :] = lse

    own = pl.BlockSpec((rb * ATT_BLK, ATT_GW), lambda r, n: (n, r))
    own_head = pl.BlockSpec((rb * ATT_BLK, HEAD), lambda r, n: (n, r))
    prev = pl.BlockSpec((ATT_BLK, ATT_GW), lambda r, n: (jnp.maximum(n * rb - 1, 0), r))
    return pl.pallas_call(
        body, grid=(d, nb // rb), in_specs=[own, own, own, prev, prev], out_specs=[own, own_head],
        out_shape=[jax.ShapeDtypeStruct((length, d * ATT_GW), BF16), jax.ShapeDtypeStruct((length, d * HEAD), F32)],
        name=name, compiler_params=_params(("parallel", "arbitrary")))(qg, kg, vg, kg, vg)


def _attn_bwd(qg, kg, vg, dog, lse, delta, g, name):
    d, length, nb, rb = _attn_cfg(qg, g)
    nsteps = nb // rb
    scale = HEAD ** -0.5

    def body(q_ref, k_ref, v_ref, do_ref, l_ref, dl_ref, kp_ref, vp_ref, qn_ref, don_ref, ln_ref, dln_ref,
             dq_ref, dk_ref, dv_ref):
        n = pl.program_id(1)
        band, first_band = _window_masks(n > 0)
        qi = lax.broadcasted_iota(jnp.int32, (ATT_BLK, ATT_BLK), 0)
        ki = lax.broadcasted_iota(jnp.int32, (ATT_BLK, ATT_BLK), 1)
        next_m = jnp.logical_and(ki >= qi, n < nsteps - 1)
        last = slice((rb - 1) * ATT_BLK, rb * ATT_BLK)
        for h in range(ATT_HEADS):
            hs = slice(h * HEAD, (h + 1) * HEAD)
            dk, dv = [None] * rb, [None] * rb
            for j in range(rb):
                rows = slice(j * ATT_BLK, (j + 1) * ATT_BLK)
                q, do = q_ref[rows, hs], do_ref[rows, hs]
                k2, v2 = _two_blocks(k_ref, kp_ref, j, hs), _two_blocks(v_ref, vp_ref, j, hs)
                p = jnp.where(first_band if j == 0 else band,
                              jnp.exp(_dot_nt(q, k2) * scale - _pick(l_ref[rows, :], h)), 0.0)
                ds = _bf(p * (_dot_nt(do, v2) - _pick(dl_ref[rows, :], h)) * scale)
                dq_ref[rows, hs] = _dot(ds, k2).astype(dq_ref.dtype)
                dk2, dv2 = _dot_tn(ds, q), _dot_tn(_bf(p), do)
                if j >= 1:
                    dk[j - 1] = dk[j - 1] + dk2[:ATT_BLK]
                    dv[j - 1] = dv[j - 1] + dv2[:ATT_BLK]
                dk[j], dv[j] = dk2[ATT_BLK:], dv2[ATT_BLK:]
            q, do = qn_ref[:, hs], don_ref[:, hs]
            p = jnp.where(next_m, jnp.exp(_dot_nt(q, k_ref[last, hs]) * scale - _pick(ln_ref[...], h)), 0.0)
            ds = _bf(p * (_dot_nt(do, v_ref[last, hs]) - _pick(dln_ref[...], h)) * scale)
            dk[rb - 1] = dk[rb - 1] + _dot_tn(ds, q)
            dv[rb - 1] = dv[rb - 1] + _dot_tn(_bf(p), do)
            for j in range(rb):
                rows = slice(j * ATT_BLK, (j + 1) * ATT_BLK)
                dk_ref[rows, hs] = dk[j].astype(dk_ref.dtype)
                dv_ref[rows, hs] = dv[j].astype(dv_ref.dtype)

    own = pl.BlockSpec((rb * ATT_BLK, ATT_GW), lambda r, n: (n, r))
    prev = pl.BlockSpec((ATT_BLK, ATT_GW), lambda r, n: (jnp.maximum(n * rb - 1, 0), r))
    nxt = pl.BlockSpec((ATT_BLK, ATT_GW), lambda r, n: (jnp.minimum((n + 1) * rb, nb - 1), r))
    own_head = pl.BlockSpec((rb * ATT_BLK, HEAD), lambda r, n: (n, r))
    nxt_head = pl.BlockSpec((ATT_BLK, HEAD), lambda r, n: (jnp.minimum((n + 1) * rb, nb - 1), r))
    return pl.pallas_call(
        body, grid=(d, nsteps), in_specs=[own] * 4 + [own_head] * 2 + [prev, prev, nxt, nxt, nxt_head, nxt_head],
        out_specs=[own, own, own], out_shape=[jax.ShapeDtypeStruct((length, d * ATT_GW), BF16)] * 3,
        name=name, compiler_params=_params(("parallel", "arbitrary")))(
            qg, kg, vg, dog, lse, delta, kg, vg, qg, dog, lse, delta)


def _rope_tables(t):
    pos = jnp.arange(t, dtype=F32)
    inv = ROPE_THETA ** (-jnp.arange(0, HEAD, 2, dtype=F32) / HEAD)
    ang = pos[:, None] * inv[None, :]
    ang = jnp.concatenate([ang, ang], axis=-1)
    return jnp.cos(ang), jnp.sin(ang)


def _lower_bounds(logits):
    lb = jnp.cumsum(jax.nn.softmax(logits.astype(F32), axis=0), axis=0)
    return lb - lb[0:1]


FFN_ROWS = 256
FF_SHARD = 2 * D_FF // N_CHIPS


def _ffn_in_act(x, g, w_in, name, rider=None):
    t = x.shape[0]

    def body(x_ref, g_ref, w_ref, h_ref, ab_ref, u_ref):
        xv = x_ref[...]
        h = _bf(xv * _rms_rows(xv) * g_ref[...])
        h_ref[...] = h
        for s in range(N_CHIPS // 2):
            cols = slice(s * FF_SHARD, (s + 1) * FF_SHARD)
            a = _dot(h, w_ref[s])
            b = _dot(h, w_ref[s + N_CHIPS // 2])
            ab_ref[:, cols] = a.astype(ab_ref.dtype)
            ab_ref[:, D_FF + s * FF_SHARD:D_FF + (s + 1) * FF_SHARD] = b.astype(ab_ref.dtype)
            u_ref[:, cols] = (a * _sig(a) * b).astype(u_ref.dtype)

    row = lambda w: pl.BlockSpec((FFN_ROWS, w), lambda i: (i, 0))
    return _pcall(
        body, grid=(t // FFN_ROWS,),
        in_specs=[row(D_MODEL), pl.BlockSpec((1, D_MODEL), lambda i: (0, 0)),
                  pl.BlockSpec(w_in.shape, lambda i: (0, 0, 0))],
        out_specs=[row(D_MODEL), row(2 * D_FF), row(D_FF)],
        out_shape=[jax.ShapeDtypeStruct((t, D_MODEL), BF16), jax.ShapeDtypeStruct((t, 2 * D_FF), BF16),
                   jax.ShapeDtypeStruct((t, D_FF), BF16)],
        name=name, sem=("parallel",), args=(x, g, w_in), rider=rider)


def _ffn_bwd_du_act(dx, w_out, ab, name, rider=None):
    t = dx.shape[0]

    def body(dx_ref, w_ref, ab_ref, o_ref):
        du = 0.5 * _dot_nt(_bf(dx_ref[...]), w_ref[0])
        a = ab_ref[:, :D_FF].astype(F32)
        b = ab_ref[:, D_FF:].astype(F32)
        s = _sig(a)
        o_ref[:, :D_FF] = (du * b * (s * (1.0 + a * (1.0 - s)))).astype(o_ref.dtype)
        o_ref[:, D_FF:] = (du * a * s).astype(o_ref.dtype)

    row = lambda w: pl.BlockSpec((FFN_ROWS, w), lambda i: (i, 0))
    return _pcall(
        body, grid=(t // FFN_ROWS,),
        in_specs=[row(D_MODEL), pl.BlockSpec(w_out.shape, lambda i: (0, 0, 0)), row(2 * D_FF)],
        out_specs=row(2 * D_FF), out_shape=jax.ShapeDtypeStruct((t, 2 * D_FF), BF16),
        name=name, sem=("parallel",), args=(dx, w_out, ab), rider=rider)


MIX_ROWS = 512


def _gate_specs():
    return [pl.BlockSpec((MIX_ROWS, 512), lambda i, cb=cb: (i, cb)) for cb in (CB_GA, CB_GA + 1, CB_GB, CB_GB + 1)]


def _whole(a):
    return pl.BlockSpec(a.shape, lambda i: (0,) * a.ndim)


def _mix_tail_fwd(oa, ob, proj, x, w_a, w_b, w_o, name):
    t = x.shape[0]

    def body(oa_ref, ob_ref, ga0, ga1, gb0, gb1, x_ref, wa_ref, wb_ref, wo_ref, y_ref, m_ref, ya_ref, yb_ref):
        ya = _dot(oa_ref[...], wa_ref[0])
        yb = _cat([_dot(ob_ref[...], wb_ref[s]) for s in range(N_CHIPS)])
        merged = _bf(_sig(_cat([ga0[...], ga1[...]])) * ya + _sig(_cat([gb0[...], gb1[...]])) * yb)
        m_ref[...] = merged
        ya_ref[...] = ya.astype(ya_ref.dtype)
        yb_ref[...] = yb.astype(yb_ref.dtype)
        y_ref[...] = x_ref[...] + _dot(merged, wo_ref[0])

    row = lambda w: pl.BlockSpec((MIX_ROWS, w), lambda i: (i, 0))
    return pl.pallas_call(
        body, grid=(t // MIX_ROWS,),
        in_specs=[row(D_MODEL), row(ATT_GW)] + _gate_specs() + [row(D_MODEL), _whole(w_a), _whole(w_b), _whole(w_o)],
        out_specs=[row(D_MODEL)] * 4,
        out_shape=[jax.ShapeDtypeStruct((t, D_MODEL), F32)] + [jax.ShapeDtypeStruct((t, D_MODEL), BF16)] * 3,
        name=name, compiler_params=_params(("parallel",)))(oa, ob, proj, proj, proj, proj, x, w_a, w_b, w_o)


def _mix_tail_bwd(dx, proj, ya, yb, w_a, w_b, w_o, name):
    t = dx.shape[0]
    shard = D_MODEL // N_CHIPS

    def body(dx_ref, ga0, ga1, gb0, gb1, ya_ref, yb_ref, wa_ref, wb_ref, wo_ref, dya_ref, dyb_ref, dg_ref, doa_ref, dob_ref):
        dm = _dot_nt(_bf(dx_ref[...]), wo_ref[0])
        sa = _sig(_cat([ga0[...], ga1[...]]))
        sb = _sig(_cat([gb0[...], gb1[...]]))
        dya, dyb = _bf(dm * sa), _bf(dm * sb)
        dya_ref[...] = dya
        dyb_ref[...] = dyb
        dg_ref[:, :D_MODEL] = (dm * ya_ref[...].astype(F32) * sa * (1.0 - sa)).astype(dg_ref.dtype)
        dg_ref[:, D_MODEL:] = (dm * yb_ref[...].astype(F32) * sb * (1.0 - sb)).astype(dg_ref.dtype)
        doa_ref[...] = _dot_nt(dya, wa_ref[0])
        dob = _dot_nt(dyb[:, :shard], wb_ref[0])
        for s in range(1, N_CHIPS):
            dob = dob + _dot_nt(dyb[:, s * shard:(s + 1) * shard], wb_ref[s])
        dob_ref[...] = dob

    row = lambda w: pl.BlockSpec((MIX_ROWS, w), lambda i: (i, 0))
    return pl.pallas_call(
        body, grid=(t // MIX_ROWS,),
        in_specs=[row(D_MODEL)] + _gate_specs() + [row(D_MODEL), row(D_MODEL), _whole(w_a), _whole(w_b), _whole(w_o)],
        out_specs=[row(D_MODEL), row(D_MODEL), row(2 * D_MODEL), row(D_MODEL), row(ATT_GW)],
        out_shape=[jax.ShapeDtypeStruct((t, D_MODEL), BF16), jax.ShapeDtypeStruct((t, D_MODEL), BF16),
                   jax.ShapeDtypeStruct((t, 2 * D_MODEL), BF16), jax.ShapeDtypeStruct((t, D_MODEL), F32),
                   jax.ShapeDtypeStruct((t, ATT_GW), F32)],
        name=name, compiler_params=_params(("parallel",)))(dx, proj, proj, proj, proj, ya, yb, w_a, w_b, w_o)


def _ffn_fwd(x, g, src, l, pre):
    tag = f"l{l}_{pre}"
    w_in = src.weight(l, pre + "_w_in")
    h, ab, u = _ffn_in_act(x, g, w_in, name=tag + "_in_act", rider=src.ride(tag + "_in_act"))
    w_out = src.weight(l, pre + "_w_out")
    y = _mm_nn(u, w_out, name=tag + "_out", tm=512, tn=D_MODEL, out_dtype=F32, res=x, alpha=0.5, rider=src.ride(tag + "_out"))
    return y, (x, h, ab, u, w_in, w_out)


def _ffn_bwd(dx, saved, g, src, l, pre):
    tag = f"l{l}_{pre}"
    x, h, ab, u, w_in, w_out = saved
    g_out = _mm_tn(u, dx, nb=1, name=tag + "_bwd_wout", tm=1024, tk=1408, tn=D_MODEL, alpha=0.5, rider=src.ride(tag + "_bwd_wout"))
    src.grads(l, {pre + "_w_out": g_out.reshape(N_CHIPS, D_FF // N_CHIPS, D_MODEL)})
    dab = _ffn_bwd_du_act(dx, w_out, ab, name=tag + "_bwd_du_act", rider=src.ride(tag + "_bwd_du_act"))
    g_in = _mm_tn(h, dab, nb=N_CHIPS, name=tag + "_bwd_win", tm=2048, tk=D_MODEL, tn=FF_SHARD, rider=src.ride(tag + "_bwd_win"))
    src.grads(l, {pre + "_w_in": g_in})
    dh = _mm_nt(dab, w_in, name=tag + "_bwd_dh", tm=1024, tp=D_MODEL, tn=FF_SHARD, out_dtype=F32, rider=src.ride(tag + "_bwd_dh"))
    return _norm_bwd(dh, x, g, dx, name=tag + "_bwd_norm")


def _mix_fwd(x, small, lb, cos, sin, src, l):
    tag = f"l{l}_mix"
    w = {}
    h = _norm_fwd(x, small["mix_norm"], name=tag + "_norm")
    w["w_in"] = src.weight(l, "w_in")
    proj = _mm_nn(h, w["w_in"], name=tag + "_in", tm=1024, tn=896, out_dtype=F32, rider=src.ride(tag + "_in"))
    oscan, oa, sall = _hgrn_fwd(proj, lb, small["hgrn_out_norm"], name=tag + "_hgrn", rider=src.ride(tag + "_hgrn"))
    qk = _qk_fwd(proj, cos, sin, small["attn_q_norm"], small["attn_k_norm"], name=tag + "_qk")
    outs, lses = [], []
    for g in range(ATT_GROUPS):
        o, lse = _attn_fwd(qk[g], qk[3 + g], qk[6 + g], g, name=f"{tag}_attn{g}")
        outs.append(o)
        lses.append(lse)
    ob = _merge_fwd(outs, lses, name=tag + "_merge")
    w.update({n: src.weight(l, n) for n in ("w_branch_a", "w_branch_b", "w_out")})
    y, merged, ya, yb = _mix_tail_fwd(oa, ob, proj, x, w["w_branch_a"], w["w_branch_b"], w["w_out"], name=tag + "_tail")
    return y, (x, h, proj, oscan, oa, sall, qk, outs, lses, ob, ya, yb, merged, w)


def _mix_bwd(dx, saved, small, lb, cos, sin, src, l, lb_live):
    tag = f"l{l}_mix"
    x, h, proj, oscan, oa, sall, qk, outs, lses, ob, ya, yb, merged, w = saved
    g_wout = _mm_tn(merged, dx, nb=1, name=tag + "_bwd_wout", tm=1024, tk=D_MODEL, tn=D_MODEL)
    dya, dyb, dgab, doa, dob = _mix_tail_bwd(dx, proj, ya, yb, w["w_branch_a"], w["w_branch_b"], w["w_out"], name=tag + "_bwd_tail")
    g_wa = _mm_tn(oa, dya, nb=1, name=tag + "_bwd_wa", tm=1024, tk=D_MODEL, tn=D_MODEL)
    g_wb = _mm_tn(ob, dyb, nb=N_CHIPS, name=tag + "_bwd_wb", tm=2048, tk=ATT_GW, tn=256)
    mb = _merge_bwd(dob, outs, lses, name=tag + "_bwd_merge")
    dqk, dvs = [None] * 6, []
    for g in range(ATT_GROUPS):
        dq, dk, dv = _attn_bwd(qk[g], qk[3 + g], qk[6 + g], mb[g], lses[g], mb[3 + g], g, name=f"{tag}_bwd_attn{g}")
        dqk[g], dqk[3 + g] = dq, dk
        dvs.append(dv)
    dqk_cols, dqn, dkn = _qk_bwd(dqk, proj, cos, sin, small["attn_q_norm"], small["attn_k_norm"], name=tag + "_bwd_qk")
    dh4, dgn, dlb = _hgrn_bwd(doa, oscan, proj, sall, lb, small["hgrn_out_norm"], name=tag + "_bwd_hgrn", precise=lb_live,
                              rider=src.ride(tag + "_bwd_hgrn"))
    dproj = _assemble_dproj(dh4, dqk_cols, dvs, dgab, name=tag + "_bwd_cat")
    src.grads(l, dict(w_branch_a=g_wa.reshape(N_CHIPS, D_MODEL // N_CHIPS, D_MODEL), w_branch_b=g_wb,
                      w_out=g_wout.reshape(N_CHIPS, D_MODEL // N_CHIPS, D_MODEL)))
    g_win = _mm_tn(h, dproj, nb=N_CHIPS, name=tag + "_bwd_win", tm=2048, tk=D_MODEL, tn=896, rider=src.ride(tag + "_bwd_win"))
    src.grads(l, dict(w_in=g_win))
    dh = _mm_nt(dproj, w["w_in"], name=tag + "_bwd_dh", tm=1024, tp=D_MODEL, tn=2688, out_dtype=F32, rider=src.ride(tag + "_bwd_dh"))
    dx, dg = _norm_bwd(dh, x, small["mix_norm"], dx, name=tag + "_bwd_norm")
    return dx, dict(mix_norm=dg, hgrn_out_norm=dgn, lb=dlb, attn_q_norm=dqn, attn_k_norm=dkn)


BIG = ("ffn1_w_in", "ffn1_w_out", "w_in", "w_branch_a", "w_branch_b", "w_out", "ffn2_w_in", "ffn2_w_out")
ROW_SHARDED = ("ffn1_w_out", "w_branch_a", "w_out", "ffn2_w_out")
SMALL = ("ffn1_norm", "mix_norm", "hgrn_lb_logits", "hgrn_out_norm", "attn_q_norm", "attn_k_norm", "ffn2_norm")
WEIGHTS = ("ffn1_norm", "ffn1_w_in", "ffn1_w_out", "mix_norm", "w_in", "hgrn_lb_logits", "hgrn_out_norm", "attn_q_norm",
           "attn_k_norm", "w_branch_a", "w_branch_b", "w_out", "ffn2_norm", "ffn2_w_in", "ffn2_w_out")
SMALL_ROWS = 8


def _matmul_ready(name, a):
    return a.reshape(1, a.shape[0] * a.shape[1], a.shape[2]) if name in ROW_SHARDED else a


def _layer_small(small, l):
    s = {n: small[n][l].reshape(1, D_MODEL) for n in ("ffn1_norm", "mix_norm", "hgrn_out_norm", "ffn2_norm")}
    s.update({n: small[n][l] for n in ("attn_q_norm", "attn_k_norm")})
    return s


def _local_step(x, target, small, src):
    t = x.shape[0]
    cos, sin = _rope_tables(t)
    lbs = _lower_bounds(small["hgrn_lb_logits"])
    saved = []
    for l in range(2):
        sm = _layer_small(small, l)
        lb = lbs[l].reshape(1, D_MODEL)
        x, s1 = _ffn_fwd(x, sm["ffn1_norm"], src, l, "ffn1")
        x, s2 = _mix_fwd(x, sm, lb, cos, sin, src, l)
        x, s3 = _ffn_fwd(x, sm["ffn2_norm"], src, l, "ffn2")
        saved.append((sm, lb, s1, s2, s3))
    dx, sq = _loss_fwd_bwd(x, target, name="loss")
    small_rows = [None, None]
    for l in (1, 0):
        sm, lb, s1, s2, s3 = saved[l]
        dx, dg2 = _ffn_bwd(dx, s3, sm["ffn2_norm"], src, l, "ffn2")
        dx, g = _mix_bwd(dx, s2, sm, lb, cos, sin, src, l, lb_live=l > 0)
        dx, dg1 = _ffn_bwd(dx, s1, sm["ffn1_norm"], src, l, "ffn1")
        pad = lambda a: jnp.pad(a[:ATT_GROUPS].reshape(1, ATT_GROUPS * HEAD), ((0, 0), (0, D_MODEL - ATT_GROUPS * HEAD)))
        small_rows[l] = jnp.concatenate(
            [dg1, g["mix_norm"], g["lb"], g["hgrn_out_norm"], pad(g["attn_q_norm"]), pad(g["attn_k_norm"]), dg2,
             jnp.zeros((SMALL_ROWS - 7, D_MODEL), F32)], axis=0)
    return jnp.sum(sq), dx, jnp.concatenate(small_rows, axis=0)


def _coords():
    return lax.axis_index("x"), lax.axis_index("y"), lax.axis_index("c")


def _other_chips(x, y):
    return [(1 - x, y), (x, 1 - y), (1 - x, 1 - y)]


def _half_rows(rows, which):
    return pl.ds(which * (rows // 2), rows // 2)


def _gather_rider(shards):
    n = len(shards)

    def copies(w, full, sems):
        send, recv, fsend, frecv = sems
        x, y, c = _coords()
        slot = 2 * x + y
        chips = _other_chips(x, y)

        def copy(i, j, blk, src, pair, to):
            return pltpu.make_async_remote_copy(src_ref=src, dst_ref=blk, send_sem=pair[0].at[i * 3 + j],
                                                recv_sem=pair[1].at[i * 3 + j], device_id=to, device_id_type=MESH)

        def block(i, chip_slot, core):
            return full[i].at[chip_slot, _half_rows(shards[i].shape[0], core)]

        pairs = [(i, j, chip) for i in range(n) for j, chip in enumerate(chips)]

        def first():
            return [copy(i, j, block(i, slot, c), w[i].at[_half_rows(shards[i].shape[0], c)], (send, recv), (*chip, c))
                    for i, j, chip in pairs]

        def landed(core, pair):
            return [copy(i, j, block(i, 2 * chip[0] + chip[1], core), block(i, 2 * chip[0] + chip[1], core), pair, (x, y, 1 - c))
                    for i, j, chip in pairs]

        return first, landed

    def begin(w, full, sems):
        for cp in copies(w, full, sems)[0]():
            cp.start()

    def end(w, full, sems):
        first, landed = copies(w, full, sems)
        forwards = landed(lax.axis_index("c"), sems[2:])
        for arrival, forward in zip(landed(lax.axis_index("c"), sems[:2]), forwards):
            arrival.wait_recv()
            forward.start()
        for cp in landed(1 - lax.axis_index("c"), sems[2:]):
            cp.wait_recv()
        for cp in first() + forwards:
            cp.wait_send()

    out_shape = [jax.ShapeDtypeStruct((N_CHIPS,) + s.shape, s.dtype) for s in shards]
    return _Rider(shards, out_shape, [pltpu.SemaphoreType.DMA((3 * n,))] * 4, begin, end)


N_RECV = 7


def _scatter_rider(parts):
    n = len(parts)

    def copies(p, out, sems):
        send, recv = sems
        x, y, c = _coords()
        slot = 2 * x + y
        chips = _other_chips(x, y)

        def arrivals():
            return [pltpu.make_async_remote_copy(
                src_ref=out[i].at[k], dst_ref=out[i].at[k], send_sem=send.at[0], recv_sem=recv.at[i * N_RECV + k],
                device_id=(x, y, c), device_id_type=MESH) for i in range(n) for k in range(N_RECV)]

        sends = []
        for i in range(n):
            rows = parts[i].shape[1]
            for j, chip in enumerate(chips):
                for core in (0, 1):
                    sends.append(pltpu.make_async_remote_copy(
                        src_ref=p[i].at[2 * chip[0] + chip[1], _half_rows(rows, core)], dst_ref=out[i].at[2 * j + c],
                        send_sem=send.at[i * N_RECV + 2 * j + core], recv_sem=recv.at[i * N_RECV + 2 * j + c],
                        device_id=(*chip, core), device_id_type=MESH))
            sends.append(pltpu.make_async_remote_copy(
                src_ref=p[i].at[slot, _half_rows(rows, 1 - c)], dst_ref=out[i].at[6], send_sem=send.at[i * N_RECV + 6],
                recv_sem=recv.at[i * N_RECV + 6], device_id=(x, y, 1 - c), device_id_type=MESH))
        return sends, arrivals

    def begin(p, out, sems):
        for cp in copies(p, out, sems)[0]:
            cp.start()

    def end(p, out, sems):
        sends, arrivals = copies(p, out, sems)
        for cp in arrivals():
            cp.wait_recv()
        for cp in sends:
            cp.wait_send()

    out_shape = [jax.ShapeDtypeStruct((N_RECV, a.shape[1] // 2, a.shape[2]), a.dtype) for a in parts]
    return _Rider(parts, out_shape, [pltpu.SemaphoreType.DMA((N_RECV * n,))] * 2, begin, end)


def _run_alone(rider, name):
    _pcall(lambda: None, grid=(), in_specs=[], out_specs=[], out_shape=[], name=name, sem=(), args=(), rider=rider)
    return rider.result


def _sum_partials(own, parts, name):
    r, wd = own.shape
    tm = next(t for t in (256, 128, 64, 32, 16) if r % t == 0)

    def body(own_ref, p_ref, o_ref):
        acc = own_ref[...].astype(F32)
        for k in range(N_RECV):
            acc = acc + p_ref[k].astype(F32)
        o_ref[...] = acc

    return pl.pallas_call(
        body, grid=(r // tm,),
        in_specs=[pl.BlockSpec((tm, wd), lambda i: (i, 0)), pl.BlockSpec((N_RECV, tm, wd), lambda i: (0, i, 0))],
        out_specs=pl.BlockSpec((tm, wd), lambda i: (i, 0)), out_shape=jax.ShapeDtypeStruct((r, wd), F32),
        name=name, compiler_params=_params(("parallel",)))(own, parts)


def _exchange_halves(reduced, name):
    n = len(reduced)

    def body(*refs):
        r, out = refs[:n], refs[n:2 * n]
        send, recv = refs[2 * n:]
        x, y, c = _coords()
        sib = [pltpu.make_async_remote_copy(src_ref=r[i], dst_ref=out[i], send_sem=send.at[i], recv_sem=recv.at[i],
                                            device_id=(x, y, 1 - c), device_id_type=MESH) for i in range(n)]
        for cp in sib:
            cp.start()
        for cp in sib:
            cp.wait_recv()
        for cp in sib:
            cp.wait_send()

    out_shape = [jax.ShapeDtypeStruct(a.shape, a.dtype) for a in reduced]
    return pl.pallas_call(body, in_specs=[ANY] * n, out_specs=[ANY] * n, out_shape=out_shape,
                          scratch_shapes=[pltpu.SemaphoreType.DMA((n,))] * 2, name=name)(*reduced)


def _reduce_finish(parts, recv, tag):
    x, y, c = _coords()
    slot = 2 * x + y
    halves = []
    for i, (p, r) in enumerate(zip(parts, recv)):
        half = p.shape[1] // 2
        own = lax.dynamic_slice(p, (slot, c * half, 0), (1, half, p.shape[2]))[0]
        halves.append(_sum_partials(own, r, name=f"{tag}_sum{i}"))
    theirs = _exchange_halves(halves, name=tag + "_exchange")
    return [jnp.where(c == 0, jnp.concatenate([h, t], axis=0), jnp.concatenate([t, h], axis=0)) for h, t in zip(halves, theirs)]


GATHER_RIDES = {
    "l0_ffn1_in_act": ((0, "w_in"),),
    "l0_ffn1_out": ((0, "w_branch_a"), (0, "w_branch_b"), (0, "w_out")),
    "l0_mix_in": ((0, "ffn2_w_in"), (0, "ffn2_w_out"), (1, "ffn1_w_in"), (1, "ffn1_w_out")),
    "l0_mix_hgrn": ((1, "w_in"), (1, "w_branch_a"), (1, "w_branch_b"), (1, "w_out")),
    "l0_ffn2_in_act": ((1, "ffn2_w_in"), (1, "ffn2_w_out")),
}
ALONE_FIRST = ((0, "ffn1_w_in"), (0, "ffn1_w_out"))
SCATTER_RIDES = {
    "l1_mix_bwd_hgrn": ((1, "ffn2_w_in"), (1, "ffn2_w_out")),
    "l0_ffn2_bwd_win": ((1, "ffn1_w_in"),),
    "l0_ffn2_bwd_dh": ((1, "ffn1_w_out"), (1, "w_branch_a"), (1, "w_branch_b"), (1, "w_out")),
    "l0_mix_bwd_hgrn": ((1, "w_in"), (0, "ffn2_w_out")),
    "l0_mix_bwd_win": ((0, "ffn2_w_in"),),
    "l0_mix_bwd_dh": ((0, "w_in"),),
    "l0_ffn1_bwd_wout": ((0, "w_branch_a"), (0, "w_branch_b"), (0, "w_out")),
    "l0_ffn1_bwd_du_act": ((0, "ffn1_w_out"),),
    "l0_ffn1_bwd_dh": ((0, "ffn1_w_in"),),
}


class _Exchange:
    def __init__(self, shards):
        self.shards = shards
        self.pending = []
        self.full = {}
        self.parts = {}
        self.recv = {}

    def _gather(self, keys):
        return _gather_rider([self.shards[n][l] for l, n in keys]), "gather", list(keys)

    def _scatter(self, keys):
        return _scatter_rider([self.parts[k] for k in keys]), "scatter", list(keys)

    def _unpack(self):
        slot = 2 * lax.axis_index("x") + lax.axis_index("y")
        waiting = []
        for rider, kind, keys in self.pending:
            if rider.result is None:
                waiting.append((rider, kind, keys))
            elif kind == "gather":
                for (l, n), got in zip(keys, rider.result):
                    self.full[(l, n)] = lax.dynamic_update_slice(got, self.shards[n][l][None], (slot, 0, 0))
            else:
                self.recv.update(zip(keys, rider.result))
        self.pending = waiting

    def ride(self, host):
        if host in GATHER_RIDES:
            self.pending.append(self._gather(GATHER_RIDES[host]))
        elif host in SCATTER_RIDES:
            self.pending.append(self._scatter(SCATTER_RIDES[host]))
        else:
            return None
        return self.pending[-1][0]

    def weight(self, l, name):
        self._unpack()
        if (l, name) not in self.full:
            assert (l, name) in ALONE_FIRST, (l, name)
            job = self._gather(ALONE_FIRST)
            _run_alone(job[0], name="gather_first")
            self.pending.append(job)
            self._unpack()
        return _matmul_ready(name, self.full[(l, name)])

    def grads(self, l, partials):
        self.parts.update({(l, n): a for n, a in partials.items()})

    def reduce(self):
        self._unpack()
        assert not self.pending and set(self.recv) == set(self.parts)
        out = {}
        for l in range(2):
            done = _reduce_finish([self.parts[(l, n)] for n in BIG], [self.recv[(l, n)] for n in BIG], f"reduce_l{l}")
            out[l] = dict(zip(BIG, done))
        return {n: jnp.stack([out[0][n], out[1][n]], axis=0) for n in BIG}


def _all_reduce_small(rows):
    r = rows.shape[0]

    def body(x_ref, o_ref, buf, send, recv):
        x, y, c = _coords()
        me = 4 * x + 2 * y + c
        buf[me] = x_ref[...]
        copies = []
        for k in range(1, 8):
            peer = (x ^ (k >> 2), y ^ ((k >> 1) & 1), c ^ (k & 1))
            cp = pltpu.make_async_remote_copy(src_ref=x_ref, dst_ref=buf.at[me], send_sem=send.at[k - 1], recv_sem=recv.at[me],
                                              device_id=peer, device_id_type=MESH)
            cp.start()
            copies.append(cp)
        for k in range(1, 8):
            src = 4 * (x ^ (k >> 2)) + 2 * (y ^ ((k >> 1) & 1)) + (c ^ (k & 1))
            pltpu.make_async_remote_copy(src_ref=x_ref, dst_ref=buf.at[src], send_sem=send.at[0], recv_sem=recv.at[src],
                                         device_id=(x, y, c), device_id_type=MESH).wait_recv()
        for cp in copies:
            cp.wait_send()
        acc = buf[0]
        for k in range(1, 8):
            acc = acc + buf[k]
        o_ref[...] = acc

    vm = pl.BlockSpec(memory_space=pltpu.VMEM)
    return pl.pallas_call(
        body, in_specs=[vm], out_specs=vm, out_shape=jax.ShapeDtypeStruct(rows.shape, F32),
        scratch_shapes=[pltpu.VMEM((8, r, D_MODEL), F32), pltpu.SemaphoreType.DMA((7,)), pltpu.SemaphoreType.DMA((8,))],
        name="all_reduce_small")(rows)


def _adamw_math(w, g, m, v):
    m = ADAM_B1 * m + (1.0 - ADAM_B1) * g
    v = ADAM_B2 * v + (1.0 - ADAM_B2) * (g * g)
    m_hat = m / (1.0 - ADAM_B1 ** ADAM_STEP)
    v_hat = v / (1.0 - ADAM_B2 ** ADAM_STEP)
    return -ADAM_LR * (m_hat / (jnp.sqrt(v_hat) + ADAM_EPS) + ADAM_WD * w), m, v


def _adamw(w, g, m, v, name):
    shape = w.shape
    cols = shape[-1]
    flat = lambda a: a.reshape(-1, cols)
    rows = flat(w).shape[0]
    tm = 128 if rows % 128 == 0 else rows
    ins = [('t', flat(a), cols, 0) for a in (w, g, m, v)]
    res = _ew(_adamw_math, ins, [('t', cols, F32)] * 3, rows=rows, tm=tm, name=name)
    return [a.reshape(shape) for a in res]


def _small_update(sums, logits, w, m, v):
    def body(s_ref, lg_ref, w_ref, m_ref, v_ref, g_ref, d_ref, nm_ref, nv_ref):
        s = s_ref[...]
        l0, l1 = lg_ref[0:1, :], lg_ref[1:2, :]
        mx = jnp.maximum(l0, l1)
        e0, e1 = jnp.exp(l0 - mx), jnp.exp(l1 - mx)
        sm0, sm1 = e0 / (e0 + e1), e1 / (e0 + e1)
        dl1 = s_ref[SMALL_ROWS + 2:SMALL_ROWS + 3, :] * sm0 * sm1
        row = lax.broadcasted_iota(jnp.int32, s.shape, 0)
        g = jnp.where(row == 2, -dl1, jnp.where(row == SMALL_ROWS + 2, dl1, s))
        d, nm, nv = _adamw_math(w_ref[...], g, m_ref[...], v_ref[...])
        g_ref[...] = g
        d_ref[...] = d
        nm_ref[...] = nm
        nv_ref[...] = nv

    vm = pl.BlockSpec(memory_space=pltpu.VMEM)
    return pl.pallas_call(body, in_specs=[vm] * 5, out_specs=[vm] * 4,
                          out_shape=[jax.ShapeDtypeStruct(sums.shape, F32)] * 4, name="small_update")(sums, logits, w, m, v)


def _pack_small(vals):
    rows = []
    for l in range(2):
        for n in ("ffn1_norm", "mix_norm", "hgrn_lb_logits", "hgrn_out_norm", "attn_q_norm", "attn_k_norm", "ffn2_norm"):
            a = vals[n][l].reshape(1, -1)
            rows.append(jnp.pad(a, ((0, 0), (0, D_MODEL - a.shape[1]))))
        rows.append(jnp.zeros((SMALL_ROWS - 7, D_MODEL), F32))
    return jnp.concatenate(rows, axis=0)


def _unpack_small(packed):
    out = {}
    for k, n in enumerate(("ffn1_norm", "mix_norm", "hgrn_lb_logits", "hgrn_out_norm", "attn_q_norm", "attn_k_norm", "ffn2_norm")):
        a = jnp.stack([packed[k], packed[SMALL_ROWS + k]], axis=0)
        out[n] = a[:, :ATT_GROUPS * HEAD].reshape(2, ATT_GROUPS, HEAD) if n.startswith("attn") else a
    return out


def kernel(x, ffn1_norm, ffn1_w_in, ffn1_w_out, mix_norm, w_in, hgrn_lb_logits, hgrn_out_norm, attn_q_norm, attn_k_norm, w_branch_a, w_branch_b, w_out, ffn2_norm, ffn2_w_in, ffn2_w_out, loss_target, m_ffn1_norm, m_ffn1_w_in, m_ffn1_w_out, m_mix_norm, m_w_in, m_hgrn_lb_logits, m_hgrn_out_norm, m_attn_q_norm, m_attn_k_norm, m_w_branch_a, m_w_branch_b, m_w_out, m_ffn2_norm, m_ffn2_w_in, m_ffn2_w_out, v_ffn1_norm, v_ffn1_w_in, v_ffn1_w_out, v_mix_norm, v_w_in, v_hgrn_lb_logits, v_hgrn_out_norm, v_attn_q_norm, v_attn_k_norm, v_w_branch_a, v_w_branch_b, v_w_out, v_ffn2_norm, v_ffn2_w_in, v_ffn2_w_out):
    a = locals()
    w = {n: a[n] for n in WEIGHTS}
    m = {n: a["m_" + n] for n in WEIGHTS}
    v = {n: a["v_" + n] for n in WEIGHTS}

    exchange = _Exchange({n: w[n].astype(BF16) for n in BIG})
    small = {n: w[n] for n in SMALL}
    sq, grad_x, small_rows = _local_step(x[0], loss_target[0], small, exchange)
    loss = lax.psum(sq, ("x", "y", "c")) * (0.5 / D_MODEL)
    grads = exchange.reduce()

    sums = _all_reduce_small(small_rows)
    g_s, d_s, m_s, v_s = _small_update(sums, w["hgrn_lb_logits"], _pack_small(small), _pack_small({n: m[n] for n in SMALL}),
                                       _pack_small({n: v[n] for n in SMALL}))
    grads.update(_unpack_small(g_s))
    delta, new_m, new_v = _unpack_small(d_s), _unpack_small(m_s), _unpack_small(v_s)
    for n in BIG:
        delta[n], new_m[n], new_v[n] = _adamw(w[n], grads[n], m[n], v[n], name="adamw_" + n)

    return (loss, grad_x[None], *[grads[n] for n in WEIGHTS], *[delta[n] for n in WEIGHTS],
            *[new_m[n] for n in WEIGHTS], *[new_v[n] for n in WEIGHTS])
```

```python
import functools

import jax
import jax.numpy as jnp
from jax import lax
from jax.experimental import pallas as pl
from jax.experimental.pallas import tpu as pltpu

F32 = jnp.float32
BF16 = jnp.bfloat16
MESH = pl.DeviceIdType.MESH

D_MODEL = 1024
D_FF = 2816
N_CHIPS = 4
HEAD = 128
HG_HEADS = 8
HG_CHUNK = 64
ATT_GROUPS = 3
ATT_HEADS = 4
ATT_GW = ATT_HEADS * HEAD
DILATIONS = (1, 4, 16)
ATT_BLK = 128
ATT_STEP_BLOCKS = 4
P_IN = 10752
CB_AQ, CB_AK, CB_AV, CB_GA, CB_GB = 8, 11, 14, 17, 19
EPS = 1e-6
ROPE_THETA = 10000.0
ADAM_LR, ADAM_B1, ADAM_B2, ADAM_EPS, ADAM_WD, ADAM_STEP = 0.001, 0.9, 0.999, 1e-08, 0.01, 10
VMEM_LIMIT_V7X = 56 * 1024 * 1024
NEG = -1e30


def _params(sem):
    return pltpu.CompilerParams(dimension_semantics=sem, vmem_limit_bytes=VMEM_LIMIT_V7X)


def _sig(x):
    return 1.0 / (1.0 + jnp.exp(-x))


def _dot(a, b):
    return jnp.dot(a, b, preferred_element_type=F32)


def _dot_nt(a, b):
    return lax.dot_general(a, b, (((1,), (1,)), ((), ())), preferred_element_type=F32)


def _dot_tn(a, b):
    return lax.dot_general(a, b, (((0,), (0,)), ((), ())), preferred_element_type=F32)


def _bf(x):
    return x.astype(BF16)


ANY = pl.BlockSpec(memory_space=pl.ANY)


class _Rider:
    def __init__(self, args, out_shape, sems, begin, end):
        self.args, self.out_shape, self.sems, self.begin, self.end = list(args), list(out_shape), list(sems), begin, end
        self.result = None


def _pcall(body, *, grid, in_specs, out_specs, out_shape, name, sem, args, scratch_shapes=(), rider=None):
    multi = isinstance(out_shape, (list, tuple))
    o_specs = list(out_specs) if multi else [out_specs]
    o_shape = list(out_shape) if multi else [out_shape]
    if rider is None:
        res = pl.pallas_call(body, grid=grid, in_specs=list(in_specs), out_specs=o_specs, out_shape=o_shape,
                             scratch_shapes=list(scratch_shapes), name=name, compiler_params=_params(sem))(*args)
        return list(res) if multi else res[0]
    counts = [len(in_specs), len(rider.args), len(o_specs), len(rider.out_shape), len(scratch_shapes)]

    def wrapped(*refs):
        groups, at = [], 0
        for c in counts:
            groups.append(refs[at:at + c])
            at += c
        h_in, r_in, h_out, r_out, h_scratch = groups
        r_sems = refs[at:]
        if grid:
            ids = [pl.program_id(a) for a in range(len(grid))]
            first = functools.reduce(jnp.logical_and, [i == 0 for i in ids])
            last = functools.reduce(jnp.logical_and, [i == g - 1 for i, g in zip(ids, grid)])
            pl.when(first)(lambda: rider.begin(r_in, r_out, r_sems))
            body(*h_in, *h_out, *h_scratch)
            pl.when(last)(lambda: rider.end(r_in, r_out, r_sems))
        else:
            rider.begin(r_in, r_out, r_sems)
            body(*h_in, *h_out, *h_scratch)
            rider.end(r_in, r_out, r_sems)

    res = pl.pallas_call(
        wrapped, grid=grid, in_specs=list(in_specs) + [ANY] * counts[1], out_specs=o_specs + [ANY] * counts[3],
        out_shape=o_shape + rider.out_shape, scratch_shapes=list(scratch_shapes) + rider.sems, name=name,
        compiler_params=_params(("arbitrary",) * len(grid)))(*args, *rider.args)
    rider.result = list(res[counts[2]:])
    return list(res[:counts[2]]) if multi else res[0]


def _mm_nn(a, b3, *, name, tm, tn, out_dtype, res=None, alpha=1.0, rider=None):
    m, k = a.shape
    nb, _, nw = b3.shape
    per = nw // tn
    assert nw % tn == 0 and m % tm == 0
    has_res = res is not None

    def body(*refs):
        if has_res:
            a_ref, b_ref, r_ref, o_ref = refs
        else:
            a_ref, b_ref, o_ref = refs
        acc = _dot(_bf(a_ref[...]), b_ref[...])
        if alpha != 1.0:
            acc = alpha * acc
        if has_res:
            acc = r_ref[...] + acc
        o_ref[...] = acc.astype(o_ref.dtype)

    in_specs = [pl.BlockSpec((tm, k), lambda i, j: (i, 0)),
                pl.BlockSpec((None, k, tn), lambda i, j: (j // per, 0, j % per))]
    args = [a, b3]
    if has_res:
        in_specs.append(pl.BlockSpec((tm, tn), lambda i, j: (i, j)))
        args.append(res)
    return _pcall(body, grid=(m // tm, nb * per), in_specs=in_specs, out_specs=pl.BlockSpec((tm, tn), lambda i, j: (i, j)),
                  out_shape=jax.ShapeDtypeStruct((m, nb * nw), out_dtype), name=name, sem=("parallel", "arbitrary"),
                  args=args, rider=rider)


def _mm_nt(d, b3, *, name, tm, tp, tn, out_dtype, alpha=1.0, rider=None, norm=None):
    m, n = d.shape
    nb, p, nw = b3.shape
    per = nw // tn
    nk = n // tn
    assert nb * nw == n and nw % tn == 0 and p % tp == 0 and m % tm == 0 and (norm is None or tp == p)

    def body(d_ref, b_ref, *refs):
        kk = pl.program_id(2)
        acc_ref = refs[-1]

        @pl.when(kk == 0)
        def _():
            acc_ref[...] = jnp.zeros_like(acc_ref)

        acc_ref[...] += _dot_nt(_bf(d_ref[...]), b_ref[...])

        if norm is None:
            @pl.when(kk == nk - 1)
            def _():
                refs[0][...] = (alpha * acc_ref[...]).astype(refs[0].dtype)
        else:
            x_ref, g_ref, dx_ref, o_ref, dg_ref = refs[:5]

            @pl.when(jnp.logical_and(pl.program_id(0) == 0, kk == 0))
            def _():
                dg_ref[...] = jnp.zeros_like(dg_ref)

            @pl.when(kk == nk - 1)
            def _():
                dh = alpha * acc_ref[...]
                xv = x_ref[...]
                r = _rms_rows(xv)
                xh = xv * r
                dxh = dh * g_ref[...]
                o_ref[...] = dx_ref[...] + r * (dxh - xh * jnp.mean(dxh * xh, axis=1, keepdims=True))
                dg_ref[...] += jnp.sum(dh * xh, axis=0, keepdims=True)

    in_specs = [pl.BlockSpec((tm, tn), lambda i, j, kk: (i, kk)),
                pl.BlockSpec((None, tp, tn), lambda i, j, kk: (kk // per, j, kk % per))]
    tile = pl.BlockSpec((tm, tp), lambda i, j, kk: (i, j))
    if norm is None:
        return _pcall(body, grid=(m // tm, p // tp, nk), in_specs=in_specs, out_specs=tile,
                      out_shape=jax.ShapeDtypeStruct((m, p), out_dtype), scratch_shapes=[pltpu.VMEM((tm, tp), F32)],
                      name=name, sem=("parallel", "parallel", "arbitrary"), args=(d, b3), rider=rider)
    x, g, dx = norm
    row = pl.BlockSpec((1, p), lambda i, j, kk: (0, 0))
    return _pcall(body, grid=(m // tm, 1, nk), in_specs=in_specs + [tile, row, tile], out_specs=[tile, row],
                  out_shape=[jax.ShapeDtypeStruct((m, p), F32), jax.ShapeDtypeStruct((1, p), F32)],
                  scratch_shapes=[pltpu.VMEM((tm, tp), F32)], name=name, sem=("arbitrary", "arbitrary", "arbitrary"),
                  args=(d, b3, x, g, dx), rider=rider)


def _mm_tn(a, d, *, nb, name, tm, tk, tn, alpha=1.0, rider=None):
    m, k = a.shape
    _, n = d.shape
    nw = n // nb
    per = nw // tn
    nm = m // tm
    assert nw % tn == 0 and k % tk == 0 and m % tm == 0

    def body(a_ref, d_ref, o_ref, acc_ref):
        mm = pl.program_id(2)

        @pl.when(mm == 0)
        def _():
            acc_ref[...] = jnp.zeros_like(acc_ref)

        acc_ref[...] += _dot_tn(_bf(a_ref[...]), _bf(d_ref[...]))

        @pl.when(mm == nm - 1)
        def _():
            o_ref[...] = (alpha * acc_ref[...]).astype(o_ref.dtype)

    return _pcall(
        body, grid=(k // tk, nb * per, nm),
        in_specs=[pl.BlockSpec((tm, tk), lambda i, j, mm: (mm, i)),
                  pl.BlockSpec((tm, tn), lambda i, j, mm: (mm, j))],
        out_specs=pl.BlockSpec((None, tk, tn), lambda i, j, mm: (j // per, i, j % per)),
        out_shape=jax.ShapeDtypeStruct((nb, k, nw), BF16),
        scratch_shapes=[pltpu.VMEM((tk, tn), F32)],
        name=name, sem=("parallel", "parallel", "arbitrary"), args=(a, d), rider=rider)


def _rows_from_view(ref, buf, w, d, tm):
    for k in range(d):
        for c in range(w // HEAD):
            lanes = slice(k * w + c * HEAD, k * w + (c + 1) * HEAD)
            buf.at[c][pl.ds(k, tm // d, stride=d), :] = ref[:, lanes].astype(F32)
    return _cat([buf[c] for c in range(w // HEAD)])


def _ew(fn, ins, outs, *, rows, tm, name):
    in_specs, args, scratch = [], [], []
    for s in ins:
        if s[0] == 't':
            _, arr, w, cb = s
            in_specs.append(pl.BlockSpec((tm, w), lambda i, cb=cb: (i, cb)))
        elif s[0] == 'v':
            _, arr, w, d = s
            in_specs.append(pl.BlockSpec((tm // d, d * w), lambda i: (i, 0)))
            scratch.append(pltpu.VMEM((w // HEAD, tm, HEAD), F32))
        else:
            arr = s[1]
            in_specs.append(pl.BlockSpec(arr.shape, lambda i, nd=arr.ndim: (0,) * nd))
        args.append(arr)
    out_specs, out_shape = [], []
    for s in outs:
        if s[0] == 't':
            _, w, dt = s
            out_specs.append(pl.BlockSpec((tm, w), lambda i: (i, 0)))
            out_shape.append(jax.ShapeDtypeStruct((rows, w), dt))
        elif s[0] == 'v':
            _, w, dt, d = s
            out_specs.append(pl.BlockSpec((tm // d, d * w), lambda i: (i, 0)))
            out_shape.append(jax.ShapeDtypeStruct((rows // d, d * w), dt))
            scratch.append(pltpu.VMEM((w // HEAD, tm, HEAD), F32))
        else:
            out_specs.append(pl.BlockSpec(s[1], lambda i: (0, 0)))
            out_shape.append(jax.ShapeDtypeStruct(s[1], F32))
    n_in, n_out = len(ins), len(outs)

    def body(*refs):
        bufs = list(refs[n_in + n_out:])
        vals = []
        for r, s in zip(refs[:n_in], ins):
            if s[0] == 'v':
                vals.append(_rows_from_view(r, bufs.pop(0), s[2], s[3], tm))
            else:
                vals.append(r[...])
        res = fn(*vals)
        if not isinstance(res, (tuple, list)):
            res = (res,)
        for r, s, v in zip(refs[n_in:n_in + n_out], outs, res):
            if s[0] == 't':
                r[...] = v.astype(r.dtype)
            elif s[0] == 'v':
                w, d, buf = s[1], s[3], bufs.pop(0)
                for c in range(w // HEAD):
                    buf[c] = v[:, c * HEAD:(c + 1) * HEAD].astype(F32)
                for k in range(d):
                    for c in range(w // HEAD):
                        lanes = slice(k * w + c * HEAD, k * w + (c + 1) * HEAD)
                        r[:, lanes] = buf.at[c][pl.ds(k, tm // d, stride=d), :].astype(r.dtype)
            else:
                @pl.when(pl.program_id(0) == 0)
                def _(r=r):
                    r[...] = jnp.zeros_like(r)

                r[...] += v

    res = pl.pallas_call(
        body, grid=(rows // tm,), in_specs=in_specs, out_specs=out_specs, out_shape=out_shape, scratch_shapes=scratch,
        name=name, compiler_params=_params(("arbitrary",)))(*args)
    return res


def _tile(arr, w, g):
    return ('t', arr, w, 0) if DILATIONS[g] == 1 else ('v', arr, w, DILATIONS[g])


def _tile_out(w, dtype, g):
    return ('t', w, dtype) if DILATIONS[g] == 1 else ('v', w, dtype, DILATIONS[g])


def _heads(x):
    return [x[:, h * HEAD:(h + 1) * HEAD] for h in range(x.shape[1] // HEAD)]


def _cat(xs):
    return jnp.concatenate(xs, axis=1)


def _head_mean(x):
    return _cat([jnp.broadcast_to(jnp.mean(h, axis=1, keepdims=True), h.shape) for h in _heads(x)])


def _rms_rows(x):
    return lax.rsqrt(jnp.mean(x * x, axis=1, keepdims=True) + EPS)


def _norm_fwd(x, g, name):
    return _ew(lambda xv, gv: xv * _rms_rows(xv) * gv,
               [('t', x, D_MODEL, 0), ('f', g)], [('t', D_MODEL, BF16)], rows=x.shape[0], tm=512, name=name)[0]


def _loss_fwd_bwd(y, target, name):
    def fn(yv, tv):
        e = yv - tv
        return e * (1.0 / D_MODEL), jnp.sum(e * e, axis=0, keepdims=True)

    return _ew(fn, [('t', y, D_MODEL, 0), ('t', target, D_MODEL, 0)], [('t', D_MODEL, F32), ('acc', (1, D_MODEL))],
               rows=y.shape[0], tm=512, name=name)


def _rot(x):
    sgn = jnp.where(lax.broadcasted_iota(jnp.int32, x.shape, 1) < HEAD // 2, -1.0, 1.0)
    return pltpu.roll(x, HEAD // 2, 1) * sgn


def _gain_rows(qn, kn):
    return [a[g:g + 1] for a in (qn, kn) for g in range(ATT_GROUPS)]


def _qk_fwd(proj, cos, sin, qn, kn, name):
    def fn(*v):
        xs, cosv, sinv, gains, vs = v[:6], v[6], v[7], v[8:14], v[14:17]
        outs = []
        for j, x in enumerate(xs):
            gain = gains[j]
            ys = []
            for xh in _heads(x):
                xn = xh * _rms_rows(xh) * gain
                ys.append(xn * cosv + _rot(xn) * sinv)
            outs.append(_cat(ys))
        return outs + list(vs)

    ins = ([('t', proj, 512, CB_AQ + j) for j in range(6)] + [('t', cos, HEAD, 0), ('t', sin, HEAD, 0)]
           + [('f', a) for a in _gain_rows(qn, kn)] + [('t', proj, 512, CB_AV + g) for g in range(ATT_GROUPS)])
    return _ew(fn, ins, [_tile_out(ATT_GW, BF16, j % ATT_GROUPS) for j in range(9)], rows=proj.shape[0], tm=512, name=name)


def _qk_bwd(dqk, proj, cos, sin, qn, kn, name):
    def fn(*v):
        ds, xs, cosv, sinv, gains = v[:6], v[6:12], v[12], v[13], v[14:20]
        rows8 = lax.broadcasted_iota(jnp.int32, (8, HEAD), 0)
        outs, dgs = [], [jnp.zeros((8, HEAD), F32)] * 2
        for j in range(6):
            gain = gains[j]
            dx, dg = [], jnp.zeros((1, HEAD), F32)
            for dyh, xh in zip(_heads(ds[j]), _heads(xs[j])):
                r = _rms_rows(xh)
                xhat = xh * r
                dxn = dyh * cosv - _rot(dyh * sinv)
                dg = dg + jnp.sum(dxn * xhat, axis=0, keepdims=True)
                dxh = dxn * gain
                dx.append(r * (dxh - xhat * jnp.mean(dxh * xhat, axis=1, keepdims=True)))
            outs.append(_cat(dx))
            dgs[j // 3] = dgs[j // 3] + jnp.where(rows8 == j % 3, dg, 0.0)
        return _cat(outs), dgs[0], dgs[1]

    ins = ([_tile(a, ATT_GW, j % ATT_GROUPS) for j, a in enumerate(dqk)] + [('t', proj, 512, CB_AQ + j) for j in range(6)]
           + [('t', cos, HEAD, 0), ('t', sin, HEAD, 0)] + [('f', a) for a in _gain_rows(qn, kn)])
    return _ew(fn, ins, [('t', 6 * ATT_GW, BF16), ('acc', (8, HEAD)), ('acc', (8, HEAD))],
               rows=proj.shape[0], tm=256, name=name)


def _pick(x, h):
    lanes = lax.broadcasted_iota(jnp.int32, x.shape, 1)
    return jnp.sum(jnp.where(lanes == h, x, 0.0), axis=1, keepdims=True)


def _spread(x):
    return _cat([jnp.broadcast_to(_pick(x, h), (x.shape[0], HEAD)) for h in range(ATT_HEADS)])


def _compact(x):
    lanes = lax.broadcasted_iota(jnp.int32, (x.shape[0], HEAD), 1)
    out = jnp.zeros((x.shape[0], HEAD), F32)
    for h, xh in enumerate(_heads(x)):
        out = jnp.where(lanes == h, xh, out)
    return out


def _group_weights(l0, l1, l2):
    l0, l1, l2 = _spread(l0), _spread(l1), _spread(l2)
    m = jnp.maximum(jnp.maximum(l0, l1), l2)
    e0, e1, e2 = jnp.exp(l0 - m), jnp.exp(l1 - m), jnp.exp(l2 - m)
    inv = 1.0 / (e0 + e1 + e2)
    return e0 * inv, e1 * inv, e2 * inv


def _merge_fwd(outs, lses, name):
    def fn(o0, o1, o2, l0, l1, l2):
        a0, a1, a2 = _group_weights(l0, l1, l2)
        return a0 * o0 + a1 * o1 + a2 * o2

    ins = [_tile(a, ATT_GW, g) for g, a in enumerate(outs)] + [_tile(a, HEAD, g) for g, a in enumerate(lses)]
    return _ew(fn, ins, [('t', ATT_GW, BF16)], rows=outs[0].shape[0], tm=512, name=name)[0]


def _merge_bwd(dob, outs, lses, name):
    def fn(dov, o0, o1, o2, l0, l1, l2):
        a0, a1, a2 = _group_weights(l0, l1, l2)
        ob = a0 * o0 + a1 * o1 + a2 * o2
        s = _head_mean(dov * ob) * float(HEAD)
        return a0 * dov, a1 * dov, a2 * dov, _compact(a0 * s), _compact(a1 * s), _compact(a2 * s)

    ins = ([('t', dob, ATT_GW, 0)] + [_tile(a, ATT_GW, g) for g, a in enumerate(outs)]
           + [_tile(a, HEAD, g) for g, a in enumerate(lses)])
    groups = range(ATT_GROUPS)
    return _ew(fn, ins, [_tile_out(ATT_GW, BF16, g) for g in groups] + [_tile_out(HEAD, F32, g) for g in groups],
               rows=dob.shape[0], tm=512, name=name)


HG_ROWS = 256


def _hg_gates(hq, hf, hi, lbv):
    sig = _sig(hf)
    f = lbv + (1.0 - lbv) * sig
    return hq * _sig(hq), 1.0 - f, hi, jnp.log(f), sig, f


def _split3(x):
    hi = _bf(x)
    r1 = x - hi.astype(F32)
    mid = _bf(r1)
    return hi, mid, _bf(r1 - mid.astype(F32))


def _tri_dot(tri, x):
    hi, mid, lo = _split3(x)
    return _dot(tri, hi) + _dot(tri, mid) + _dot(tri, lo)


def _row(x, i):
    rows = lax.broadcasted_iota(jnp.int32, x.shape, 0)
    return jnp.sum(jnp.where(rows == i, x, 0.0), axis=0, keepdims=True)


def _hg_decay(logf, q, k):
    c = HG_CHUNK
    row = lax.broadcasted_iota(jnp.int32, (c, c), 0)
    col = lax.broadcasted_iota(jnp.int32, (c, c), 1)
    g = _tri_dot((row >= col).astype(BF16), logf)
    gm = _row(g, c // 2 - 1)
    gl = _row(g, c - 1)
    return g, gm, gl, q * jnp.exp(g), q * jnp.exp(g - gm), k * jnp.exp(gm - g), k * jnp.exp(gl - g)


def _hg_out_fwd(o, hg, gain):
    r = lax.rsqrt(_head_mean(o * o) + EPS)
    return o * r * gain * (hg * _sig(hg))


def _hgrn_fwd(proj, lb, gain, name, rider=None):
    t = proj.shape[0]
    nck = HG_ROWS // HG_CHUNK

    def body(hq_ref, hf_ref, hi_ref, hg_ref, lb_ref, gn_ref, o_ref, oa_ref, sall_ref, st_ref):
        @pl.when(pl.program_id(0) == 0)
        def _():
            st_ref[...] = jnp.zeros_like(st_ref)

        lbv = lb_ref[...]
        gnv = gn_ref[...]
        c = HG_CHUNK
        mask = lax.broadcasted_iota(jnp.int32, (c, c), 0) >= lax.broadcasted_iota(jnp.int32, (c, c), 1)

        def chunk(cc, carry):
            sl = pl.ds(pl.multiple_of(cc * c, c), c)
            q, k, v, logf, _, _ = _hg_gates(hq_ref[sl, :], hf_ref[sl, :], hi_ref[sl, :], lbv)
            _, _, gl, qg, qt, kt, kd = _hg_decay(logf, q, k)
            egl = jnp.exp(gl)
            os = []
            for h in range(HG_HEADS):
                hs = slice(h * HEAD, (h + 1) * HEAD)
                st = st_ref[h]
                sall_ref[cc, h] = st
                a = jnp.where(mask, _dot_nt(_bf(qt[:, hs]), _bf(kt[:, hs])), 0.0)
                os.append(_dot(_bf(a), _bf(v[:, hs])) + _dot_nt(_bf(qg[:, hs]), _bf(st)))
                st_ref[h] = egl[:, hs] * st + _dot_tn(_bf(v[:, hs]), _bf(kd[:, hs]))
            o = _cat(os)
            o_ref[sl, :] = o
            oa_ref[sl, :] = _hg_out_fwd(o, hg_ref[sl, :], gnv).astype(oa_ref.dtype)
            return carry

        lax.fori_loop(0, nck, chunk, 0)

    col = lambda j: pl.BlockSpec((HG_ROWS, D_MODEL), lambda i, j=j: (i, j))
    small = pl.BlockSpec((1, D_MODEL), lambda i: (0, 0))
    return _pcall(
        body, grid=(t // HG_ROWS,),
        in_specs=[col(0), col(1), col(2), col(3), small, small],
        out_specs=[col(0), col(0), pl.BlockSpec((nck, HG_HEADS, HEAD, HEAD), lambda i: (i, 0, 0, 0))],
        out_shape=[jax.ShapeDtypeStruct((t, D_MODEL), F32), jax.ShapeDtypeStruct((t, D_MODEL), BF16),
                   jax.ShapeDtypeStruct((t // HG_CHUNK, HG_HEADS, HEAD, HEAD), F32)],
        scratch_shapes=[pltpu.VMEM((HG_HEADS, HEAD, HEAD), F32)],
        name=name, sem=("arbitrary",), args=(proj, proj, proj, proj, lb, gain), rider=rider)


def _terms(x, precise):
    hi = _bf(x)
    return (hi, _bf(x - hi.astype(F32))) if precise else (hi,)


def _mm(dot, a, b):
    out = dot(a[0], b[0])
    if len(a) > 1:
        out = out + dot(a[1], b[0])
    if len(b) > 1:
        out = out + dot(a[0], b[1])
    return out


def _hgrn_bwd(doa, oscan, proj, sall, lb, gain, dqk, dvs, dgab, name, precise, rider=None):
    t = proj.shape[0]
    nck = HG_ROWS // HG_CHUNK
    nsteps = t // HG_ROWS
    terms = functools.partial(_terms, precise=precise)
    n_view = sum(d > 1 for d in DILATIONS)

    def body(doa_ref, os_ref, hq_ref, hf_ref, hi_ref, hg_ref, sall_ref, lb_ref, gn_ref, dqk_ref, dv0_ref, dv1_ref,
             dv2_ref, dgab_ref, dproj_ref, dgn_ref, dlb_ref, dst_ref, *bufs):
        @pl.when(pl.program_id(0) == 0)
        def _():
            dst_ref[...] = jnp.zeros_like(dst_ref)
            dgn_ref[...] = jnp.zeros_like(dgn_ref)
            dlb_ref[...] = jnp.zeros_like(dlb_ref)

        at = 4 * D_MODEL
        dproj_ref[:, at:at + 6 * ATT_GW] = dqk_ref[...]
        at += 6 * ATT_GW
        spare = list(bufs)
        for d, dv_ref in zip(DILATIONS, (dv0_ref, dv1_ref, dv2_ref)):
            dv = dv_ref[...] if d == 1 else _rows_from_view(dv_ref, spare.pop(0), ATT_GW, d, HG_ROWS)
            dproj_ref[:, at:at + ATT_GW] = dv.astype(dproj_ref.dtype)
            at += ATT_GW
        dproj_ref[:, at:] = dgab_ref[...]

        lbv = lb_ref[...]
        gnv = gn_ref[...]
        c = HG_CHUNK
        row = lax.broadcasted_iota(jnp.int32, (c, c), 0)
        colm = lax.broadcasted_iota(jnp.int32, (c, c), 1)
        mask = row >= colm
        triu = (row <= colm).astype(BF16)
        last = lax.broadcasted_iota(jnp.int32, (c, HEAD), 0) == c - 1

        def chunk(ci, carry):
            cc = nck - 1 - ci
            sl = pl.ds(pl.multiple_of(cc * c, c), c)
            hq, hf, hg = hq_ref[sl, :], hf_ref[sl, :], hg_ref[sl, :]
            q, k, v, logf, sig, f = _hg_gates(hq, hf, hi_ref[sl, :], lbv)
            g, gm, gl, qg, qt, kt, kd = _hg_decay(logf, q, k)
            egl = jnp.exp(gl)
            o = os_ref[sl, :]
            dy = doa_ref[sl, :]
            r = lax.rsqrt(_head_mean(o * o) + EPS)
            oh = o * r
            sg = _sig(hg)
            silu_g = hg * sg
            dgn_ref[...] += jnp.sum(dy * oh * silu_g, axis=0, keepdims=True)
            dhg = dy * oh * gnv * (sg * (1.0 + hg * (1.0 - sg)))
            doh = dy * gnv * silu_g
            do = r * (doh - oh * _head_mean(doh * oh))
            dqs, dks, dvs, dgs = [], [], [], []
            for h in range(HG_HEADS):
                hs = slice(h * HEAD, (h + 1) * HEAD)
                st = sall_ref[cc, h]
                dst = dst_ref[h]
                qt_h, kt_h, qg_h, kd_h = qt[:, hs], kt[:, hs], qg[:, hs], kd[:, hs]
                do_p, v_p, qt_p, kt_p, qg_p = terms(do[:, hs]), terms(v[:, hs]), terms(qt_h), terms(kt_h), terms(qg_h)
                st_p, dst_p = terms(st), terms(dst)
                a = jnp.where(mask, _dot_nt(qt_p[0], kt_p[0]), 0.0)
                da = terms(jnp.where(mask, _mm(_dot_nt, do_p, v_p), 0.0))
                dqt = _mm(_dot, da, kt_p)
                dkt = _mm(_dot_tn, da, qt_p)
                dqg = _mm(_dot, do_p, st_p)
                dv = _dot_tn(_bf(a), do_p[0]) + _dot_nt(_bf(kd_h), dst_p[0])
                dkd = _mm(_dot, v_p, dst_p)
                dgl = egl[:, hs] * jnp.sum(st * dst, axis=0, keepdims=True) + jnp.sum(dkd * kd_h, axis=0, keepdims=True)
                dst_ref[h] = egl[:, hs] * dst + _mm(_dot_tn, do_p, qg_p)
                g_h = g[:, hs]
                gm_h = gm[:, hs]
                gl_h = gl[:, hs]
                dqs.append(dqt * jnp.exp(g_h - gm_h) + dqg * jnp.exp(g_h))
                dks.append(dkt * jnp.exp(gm_h - g_h) + dkd * jnp.exp(gl_h - g_h))
                dvs.append(dv)
                dgs.append(dqt * qt_h - dkt * kt_h + dqg * qg_h - dkd * kd_h + jnp.where(last, dgl, 0.0))
            dq, dk, dv, dg = _cat(dqs), _cat(dks), _cat(dvs), _cat(dgs)
            dlogf = _tri_dot(triu, dg)
            df = dlogf / f - dk
            dlb_ref[...] += jnp.sum(df * (1.0 - sig), axis=0, keepdims=True)
            dhf = df * (1.0 - lbv) * sig * (1.0 - sig)
            sq = _sig(hq)
            dhq = dq * (sq * (1.0 + hq * (1.0 - sq)))
            dproj_ref[sl, :4 * D_MODEL] = _cat([dhq, dhf, dv, dhg]).astype(dproj_ref.dtype)
            return carry

        lax.fori_loop(0, nck, chunk, 0)

    rev = lambda j: pl.BlockSpec((HG_ROWS, D_MODEL), lambda i, j=j: (nsteps - 1 - i, j))
    rows = lambda a, d=1: pl.BlockSpec((HG_ROWS // d, a.shape[1]), lambda i: (nsteps - 1 - i, 0))
    small = pl.BlockSpec((1, D_MODEL), lambda i: (0, 0))
    return _pcall(
        body, grid=(nsteps,),
        in_specs=[rev(0), rev(0), rev(0), rev(1), rev(2), rev(3),
                  pl.BlockSpec((nck, HG_HEADS, HEAD, HEAD), lambda i: (nsteps - 1 - i, 0, 0, 0)), small, small,
                  rows(dqk)] + [rows(a, d) for a, d in zip(dvs, DILATIONS)] + [rows(dgab)],
        out_specs=[pl.BlockSpec((HG_ROWS, P_IN), lambda i: (nsteps - 1 - i, 0)), small, small],
        out_shape=[jax.ShapeDtypeStruct((t, P_IN), BF16), jax.ShapeDtypeStruct((1, D_MODEL), F32),
                   jax.ShapeDtypeStruct((1, D_MODEL), F32)],
        scratch_shapes=[pltpu.VMEM((HG_HEADS, HEAD, HEAD), F32)] + [pltpu.VMEM((ATT_HEADS, HG_ROWS, HEAD), F32)] * n_view,
        name=name, sem=("arbitrary",), args=(doa, oscan, proj, proj, proj, proj, sall, lb, gain, dqk, *dvs, dgab),
        rider=rider)


def _window_masks(has_previous):
    qi = lax.broadcasted_iota(jnp.int32, (ATT_BLK, 2 * ATT_BLK), 0)
    ki = lax.broadcasted_iota(jnp.int32, (ATT_BLK, 2 * ATT_BLK), 1)
    band = jnp.logical_and(ki >= qi, ki <= qi + ATT_BLK)
    return band, jnp.logical_and(band, jnp.logical_or(ki >= ATT_BLK, has_previous))


def _two_blocks(ref, prev_ref, j, hs):
    if j == 0:
        return jnp.concatenate([prev_ref[:, hs], ref[0:ATT_BLK, hs]], axis=0)
    return ref[(j - 1) * ATT_BLK:(j + 1) * ATT_BLK, hs]


def _attn_cfg(qg, g):
    d = DILATIONS[g]
    length = qg.shape[0]
    assert qg.shape[1] == d * ATT_GW
    nb = length // ATT_BLK
    return d, length, nb, min(ATT_STEP_BLOCKS, nb)


def _attn_fwd(qg, kg, vg, g, name):
    d, length, nb, rb = _attn_cfg(qg, g)
    scale = HEAD ** -0.5

    def body(q_ref, k_ref, v_ref, kp_ref, vp_ref, o_ref, l_ref):
        n = pl.program_id(1)
        band, first_band = _window_masks(n > 0)
        lanes = lax.broadcasted_iota(jnp.int32, (ATT_BLK, HEAD), 1)
        for j in range(rb):
            rows = slice(j * ATT_BLK, (j + 1) * ATT_BLK)
            lse = jnp.zeros((ATT_BLK, HEAD), F32)
            for h in range(ATT_HEADS):
                hs = slice(h * HEAD, (h + 1) * HEAD)
                k2, v2 = _two_blocks(k_ref, kp_ref, j, hs), _two_blocks(v_ref, vp_ref, j, hs)
                s = jnp.where(first_band if j == 0 else band, _dot_nt(q_ref[rows, hs], k2) * scale, NEG)
                m = jnp.max(s, axis=1, keepdims=True)
                p = jnp.exp(s - m)
                l = jnp.sum(p, axis=1, keepdims=True)
                o_ref[rows, hs] = (_dot(_bf(p), v2) / l).astype(o_ref.dtype)
                lse = jnp.where(lanes == h, m + jnp.log(l), lse)
            l_ref[rows, :] = lse

    own = pl.BlockSpec((rb * ATT_BLK, ATT_GW), lambda r, n: (n, r))
    own_head = pl.BlockSpec((rb * ATT_BLK, HEAD), lambda r, n: (n, r))
    prev = pl.BlockSpec((ATT_BLK, ATT_GW), lambda r, n: (jnp.maximum(n * rb - 1, 0), r))
    return pl.pallas_call(
        body, grid=(d, nb // rb), in_specs=[own, own, own, prev, prev], out_specs=[own, own_head],
        out_shape=[jax.ShapeDtypeStruct((length, d * ATT_GW), BF16), jax.ShapeDtypeStruct((length, d * HEAD), F32)],
        name=name, compiler_params=_params(("parallel", "arbitrary")))(qg, kg, vg, kg, vg)


def _attn_bwd(qg, kg, vg, dog, lse, delta, g, name):
    d, length, nb, rb = _attn_cfg(qg, g)
    nsteps = nb // rb
    scale = HEAD ** -0.5

    def body(q_ref, k_ref, v_ref, do_ref, l_ref, dl_ref, kp_ref, vp_ref, qn_ref, don_ref, ln_ref, dln_ref,
             dq_ref, dk_ref, dv_ref):
        n = pl.program_id(1)
        band, first_band = _window_masks(n > 0)
        qi = lax.broadcasted_iota(jnp.int32, (ATT_BLK, ATT_BLK), 0)
        ki = lax.broadcasted_iota(jnp.int32, (ATT_BLK, ATT_BLK), 1)
        next_m = jnp.logical_and(ki >= qi, n < nsteps - 1)
        last = slice((rb - 1) * ATT_BLK, rb * ATT_BLK)
        for h in range(ATT_HEADS):
            hs = slice(h * HEAD, (h + 1) * HEAD)
            dk, dv = [None] * rb, [None] * rb
            for j in range(rb):
                rows = slice(j * ATT_BLK, (j + 1) * ATT_BLK)
                q, do = q_ref[rows, hs], do_ref[rows, hs]
                k2, v2 = _two_blocks(k_ref, kp_ref, j, hs), _two_blocks(v_ref, vp_ref, j, hs)
                p = jnp.where(first_band if j == 0 else band,
                              jnp.exp(_dot_nt(q, k2) * scale - _pick(l_ref[rows, :], h)), 0.0)
                ds = _bf(p * (_dot_nt(do, v2) - _pick(dl_ref[rows, :], h)) * scale)
                dq_ref[rows, hs] = _dot(ds, k2).astype(dq_ref.dtype)
                dk2, dv2 = _dot_tn(ds, q), _dot_tn(_bf(p), do)
                if j >= 1:
                    dk[j - 1] = dk[j - 1] + dk2[:ATT_BLK]
                    dv[j - 1] = dv[j - 1] + dv2[:ATT_BLK]
                dk[j], dv[j] = dk2[ATT_BLK:], dv2[ATT_BLK:]
            q, do = qn_ref[:, hs], don_ref[:, hs]
            p = jnp.where(next_m, jnp.exp(_dot_nt(q, k_ref[last, hs]) * scale - _pick(ln_ref[...], h)), 0.0)
            ds = _bf(p * (_dot_nt(do, v_ref[last, hs]) - _pick(dln_ref[...], h)) * scale)
            dk[rb - 1] = dk[rb - 1] + _dot_tn(ds, q)
            dv[rb - 1] = dv[rb - 1] + _dot_tn(_bf(p), do)
            for j in range(rb):
                rows = slice(j * ATT_BLK, (j + 1) * ATT_BLK)
                dk_ref[rows, hs] = dk[j].astype(dk_ref.dtype)
                dv_ref[rows, hs] = dv[j].astype(dv_ref.dtype)

    own = pl.BlockSpec((rb * ATT_BLK, ATT_GW), lambda r, n: (n, r))
    prev = pl.BlockSpec((ATT_BLK, ATT_GW), lambda r, n: (jnp.maximum(n * rb - 1, 0), r))
    nxt = pl.BlockSpec((ATT_BLK, ATT_GW), lambda r, n: (jnp.minimum((n + 1) * rb, nb - 1), r))
    own_head = pl.BlockSpec((rb * ATT_BLK, HEAD), lambda r, n: (n, r))
    nxt_head = pl.BlockSpec((ATT_BLK, HEAD), lambda r, n: (jnp.minimum((n + 1) * rb, nb - 1), r))
    return pl.pallas_call(
        body, grid=(d, nsteps), in_specs=[own] * 4 + [own_head] * 2 + [prev, prev, nxt, nxt, nxt_head, nxt_head],
        out_specs=[own, own, own], out_shape=[jax.ShapeDtypeStruct((length, d * ATT_GW), BF16)] * 3,
        name=name, compiler_params=_params(("parallel", "arbitrary")))(
            qg, kg, vg, dog, lse, delta, kg, vg, qg, dog, lse, delta)


def _rope_tables(t):
    pos = jnp.arange(t, dtype=F32)
    inv = ROPE_THETA ** (-jnp.arange(0, HEAD, 2, dtype=F32) / HEAD)
    ang = pos[:, None] * inv[None, :]
    ang = jnp.concatenate([ang, ang], axis=-1)
    return jnp.cos(ang), jnp.sin(ang)


def _lower_bounds(logits):
    lb = jnp.cumsum(jax.nn.softmax(logits.astype(F32), axis=0), axis=0)
    return lb - lb[0:1]


FFN_ROWS = 256
FF_SHARD = 2 * D_FF // N_CHIPS


def _ffn_in_act(x, g, w_in, name, rider=None):
    t = x.shape[0]

    def body(x_ref, g_ref, w_ref, h_ref, ab_ref, u_ref):
        xv = x_ref[...]
        h = _bf(xv * _rms_rows(xv) * g_ref[...])
        h_ref[...] = h
        for s in range(N_CHIPS // 2):
            cols = slice(s * FF_SHARD, (s + 1) * FF_SHARD)
            a = _dot(h, w_ref[s])
            b = _dot(h, w_ref[s + N_CHIPS // 2])
            ab_ref[:, cols] = a.astype(ab_ref.dtype)
            ab_ref[:, D_FF + s * FF_SHARD:D_FF + (s + 1) * FF_SHARD] = b.astype(ab_ref.dtype)
            u_ref[:, cols] = (a * _sig(a) * b).astype(u_ref.dtype)

    row = lambda w: pl.BlockSpec((FFN_ROWS, w), lambda i: (i, 0))
    return _pcall(
        body, grid=(t // FFN_ROWS,),
        in_specs=[row(D_MODEL), pl.BlockSpec((1, D_MODEL), lambda i: (0, 0)),
                  pl.BlockSpec(w_in.shape, lambda i: (0, 0, 0))],
        out_specs=[row(D_MODEL), row(2 * D_FF), row(D_FF)],
        out_shape=[jax.ShapeDtypeStruct((t, D_MODEL), BF16), jax.ShapeDtypeStruct((t, 2 * D_FF), BF16),
                   jax.ShapeDtypeStruct((t, D_FF), BF16)],
        name=name, sem=("parallel",), args=(x, g, w_in), rider=rider)


def _ffn_bwd_du_act(dx, w_out, ab, name, rider=None):
    t = dx.shape[0]

    def body(dx_ref, w_ref, ab_ref, o_ref):
        du = 0.5 * _dot_nt(_bf(dx_ref[...]), w_ref[0])
        a = ab_ref[:, :D_FF].astype(F32)
        b = ab_ref[:, D_FF:].astype(F32)
        s = _sig(a)
        o_ref[:, :D_FF] = (du * b * (s * (1.0 + a * (1.0 - s)))).astype(o_ref.dtype)
        o_ref[:, D_FF:] = (du * a * s).astype(o_ref.dtype)

    row = lambda w: pl.BlockSpec((FFN_ROWS, w), lambda i: (i, 0))
    return _pcall(
        body, grid=(t // FFN_ROWS,),
        in_specs=[row(D_MODEL), pl.BlockSpec(w_out.shape, lambda i: (0, 0, 0)), row(2 * D_FF)],
        out_specs=row(2 * D_FF), out_shape=jax.ShapeDtypeStruct((t, 2 * D_FF), BF16),
        name=name, sem=("parallel",), args=(dx, w_out, ab), rider=rider)


MIX_ROWS = 512


def _gate_specs():
    return [pl.BlockSpec((MIX_ROWS, 512), lambda i, cb=cb: (i, cb)) for cb in (CB_GA, CB_GA + 1, CB_GB, CB_GB + 1)]


def _whole(a):
    return pl.BlockSpec(a.shape, lambda i: (0,) * a.ndim)


def _mix_tail_fwd(oa, ob, proj, x, w_a, w_b, w_o, name):
    t = x.shape[0]

    def body(oa_ref, ob_ref, ga0, ga1, gb0, gb1, x_ref, wa_ref, wb_ref, wo_ref, y_ref, m_ref, ya_ref, yb_ref):
        ya = _dot(oa_ref[...], wa_ref[0])
        yb = _cat([_dot(ob_ref[...], wb_ref[s]) for s in range(N_CHIPS)])
        merged = _bf(_sig(_cat([ga0[...], ga1[...]])) * ya + _sig(_cat([gb0[...], gb1[...]])) * yb)
        m_ref[...] = merged
        ya_ref[...] = ya.astype(ya_ref.dtype)
        yb_ref[...] = yb.astype(yb_ref.dtype)
        y_ref[...] = x_ref[...] + _dot(merged, wo_ref[0])

    row = lambda w: pl.BlockSpec((MIX_ROWS, w), lambda i: (i, 0))
    return pl.pallas_call(
        body, grid=(t // MIX_ROWS,),
        in_specs=[row(D_MODEL), row(ATT_GW)] + _gate_specs() + [row(D_MODEL), _whole(w_a), _whole(w_b), _whole(w_o)],
        out_specs=[row(D_MODEL)] * 4,
        out_shape=[jax.ShapeDtypeStruct((t, D_MODEL), F32)] + [jax.ShapeDtypeStruct((t, D_MODEL), BF16)] * 3,
        name=name, compiler_params=_params(("parallel",)))(oa, ob, proj, proj, proj, proj, x, w_a, w_b, w_o)


def _mix_tail_bwd(dx, proj, ya, yb, w_a, w_b, w_o, name):
    t = dx.shape[0]
    shard = D_MODEL // N_CHIPS

    def body(dx_ref, ga0, ga1, gb0, gb1, ya_ref, yb_ref, wa_ref, wb_ref, wo_ref, dya_ref, dyb_ref, dg_ref, doa_ref, dob_ref):
        dm = _dot_nt(_bf(dx_ref[...]), wo_ref[0])
        sa = _sig(_cat([ga0[...], ga1[...]]))
        sb = _sig(_cat([gb0[...], gb1[...]]))
        dya, dyb = _bf(dm * sa), _bf(dm * sb)
        dya_ref[...] = dya
        dyb_ref[...] = dyb
        dg_ref[:, :D_MODEL] = (dm * ya_ref[...].astype(F32) * sa * (1.0 - sa)).astype(dg_ref.dtype)
        dg_ref[:, D_MODEL:] = (dm * yb_ref[...].astype(F32) * sb * (1.0 - sb)).astype(dg_ref.dtype)
        doa_ref[...] = _dot_nt(dya, wa_ref[0])
        dob = _dot_nt(dyb[:, :shard], wb_ref[0])
        for s in range(1, N_CHIPS):
            dob = dob + _dot_nt(dyb[:, s * shard:(s + 1) * shard], wb_ref[s])
        dob_ref[...] = dob

    row = lambda w: pl.BlockSpec((MIX_ROWS, w), lambda i: (i, 0))
    return pl.pallas_call(
        body, grid=(t // MIX_ROWS,),
        in_specs=[row(D_MODEL)] + _gate_specs() + [row(D_MODEL), row(D_MODEL), _whole(w_a), _whole(w_b), _whole(w_o)],
        out_specs=[row(D_MODEL), row(D_MODEL), row(2 * D_MODEL), row(D_MODEL), row(ATT_GW)],
        out_shape=[jax.ShapeDtypeStruct((t, D_MODEL), BF16), jax.ShapeDtypeStruct((t, D_MODEL), BF16),
                   jax.ShapeDtypeStruct((t, 2 * D_MODEL), BF16), jax.ShapeDtypeStruct((t, D_MODEL), F32),
                   jax.ShapeDtypeStruct((t, ATT_GW), F32)],
        name=name, compiler_params=_params(("parallel",)))(dx, proj, proj, proj, proj, ya, yb, w_a, w_b, w_o)


def _ffn_fwd(x, g, src, l, pre):
    tag = f"l{l}_{pre}"
    w_in = src.weight(l, pre + "_w_in")
    h, ab, u = _ffn_in_act(x, g, w_in, name=tag + "_in_act", rider=src.ride(tag + "_in_act"))
    w_out = src.weight(l, pre + "_w_out")
    y = _mm_nn(u, w_out, name=tag + "_out", tm=512, tn=D_MODEL, out_dtype=F32, res=x, alpha=0.5, rider=src.ride(tag + "_out"))
    return y, (x, h, ab, u, w_in, w_out)


def _ffn_bwd(dx, saved, g, src, l, pre):
    tag = f"l{l}_{pre}"
    x, h, ab, u, w_in, w_out = saved
    g_out = _mm_tn(u, dx, nb=1, name=tag + "_bwd_wout", tm=1024, tk=1408, tn=D_MODEL, alpha=0.5, rider=src.ride(tag + "_bwd_wout"))
    src.grads(l, {pre + "_w_out": g_out.reshape(N_CHIPS, D_FF // N_CHIPS, D_MODEL)})
    dab = _ffn_bwd_du_act(dx, w_out, ab, name=tag + "_bwd_du_act", rider=src.ride(tag + "_bwd_du_act"))
    g_in = _mm_tn(h, dab, nb=N_CHIPS, name=tag + "_bwd_win", tm=2048, tk=D_MODEL, tn=FF_SHARD, rider=src.ride(tag + "_bwd_win"))
    src.grads(l, {pre + "_w_in": g_in})
    return _mm_nt(dab, w_in, name=tag + "_bwd_dh", tm=1024, tp=D_MODEL, tn=FF_SHARD, out_dtype=F32, rider=src.ride(tag + "_bwd_dh"),
                  norm=(x, g, dx))


def _mix_fwd(x, small, lb, cos, sin, src, l):
    tag = f"l{l}_mix"
    w = {}
    h = _norm_fwd(x, small["mix_norm"], name=tag + "_norm")
    w["w_in"] = src.weight(l, "w_in")
    proj = _mm_nn(h, w["w_in"], name=tag + "_in", tm=1024, tn=896, out_dtype=F32, rider=src.ride(tag + "_in"))
    oscan, oa, sall = _hgrn_fwd(proj, lb, small["hgrn_out_norm"], name=tag + "_hgrn", rider=src.ride(tag + "_hgrn"))
    qk = _qk_fwd(proj, cos, sin, small["attn_q_norm"], small["attn_k_norm"], name=tag + "_qk")
    outs, lses = [], []
    for g in range(ATT_GROUPS):
        o, lse = _attn_fwd(qk[g], qk[3 + g], qk[6 + g], g, name=f"{tag}_attn{g}")
        outs.append(o)
        lses.append(lse)
    ob = _merge_fwd(outs, lses, name=tag + "_merge")
    w.update({n: src.weight(l, n) for n in ("w_branch_a", "w_branch_b", "w_out")})
    y, merged, ya, yb = _mix_tail_fwd(oa, ob, proj, x, w["w_branch_a"], w["w_branch_b"], w["w_out"], name=tag + "_tail")
    return y, (x, h, proj, oscan, oa, sall, qk, outs, lses, ob, ya, yb, merged, w)


def _mix_bwd(dx, saved, small, lb, cos, sin, src, l, lb_live):
    tag = f"l{l}_mix"
    x, h, proj, oscan, oa, sall, qk, outs, lses, ob, ya, yb, merged, w = saved
    g_wout = _mm_tn(merged, dx, nb=1, name=tag + "_bwd_wout", tm=1024, tk=D_MODEL, tn=D_MODEL)
    dya, dyb, dgab, doa, dob = _mix_tail_bwd(dx, proj, ya, yb, w["w_branch_a"], w["w_branch_b"], w["w_out"], name=tag + "_bwd_tail")
    g_wa = _mm_tn(oa, dya, nb=1, name=tag + "_bwd_wa", tm=1024, tk=D_MODEL, tn=D_MODEL)
    g_wb = _mm_tn(ob, dyb, nb=N_CHIPS, name=tag + "_bwd_wb", tm=2048, tk=ATT_GW, tn=256)
    mb = _merge_bwd(dob, outs, lses, name=tag + "_bwd_merge")
    dqk, dvs = [None] * 6, []
    for g in range(ATT_GROUPS):
        dq, dk, dv = _attn_bwd(qk[g], qk[3 + g], qk[6 + g], mb[g], lses[g], mb[3 + g], g, name=f"{tag}_bwd_attn{g}")
        dqk[g], dqk[3 + g] = dq, dk
        dvs.append(dv)
    dqk_cols, dqn, dkn = _qk_bwd(dqk, proj, cos, sin, small["attn_q_norm"], small["attn_k_norm"], name=tag + "_bwd_qk")
    dproj, dgn, dlb = _hgrn_bwd(doa, oscan, proj, sall, lb, small["hgrn_out_norm"], dqk_cols, dvs, dgab,
                                name=tag + "_bwd_hgrn", precise=lb_live, rider=src.ride(tag + "_bwd_hgrn"))
    src.grads(l, dict(w_branch_a=g_wa.reshape(N_CHIPS, D_MODEL // N_CHIPS, D_MODEL), w_branch_b=g_wb,
                      w_out=g_wout.reshape(N_CHIPS, D_MODEL // N_CHIPS, D_MODEL)))
    g_win = _mm_tn(h, dproj, nb=N_CHIPS, name=tag + "_bwd_win", tm=2048, tk=D_MODEL, tn=896, rider=src.ride(tag + "_bwd_win"))
    src.grads(l, dict(w_in=g_win))
    dx, dg = _mm_nt(dproj, w["w_in"], name=tag + "_bwd_dh", tm=1024, tp=D_MODEL, tn=2688, out_dtype=F32,
                    rider=src.ride(tag + "_bwd_dh"), norm=(x, small["mix_norm"], dx))
    return dx, dict(mix_norm=dg, hgrn_out_norm=dgn, lb=dlb, attn_q_norm=dqn, attn_k_norm=dkn)


BIG = ("ffn1_w_in", "ffn1_w_out", "w_in", "w_branch_a", "w_branch_b", "w_out", "ffn2_w_in", "ffn2_w_out")
ROW_SHARDED = ("ffn1_w_out", "w_branch_a", "w_out", "ffn2_w_out")
SMALL = ("ffn1_norm", "mix_norm", "hgrn_lb_logits", "hgrn_out_norm", "attn_q_norm", "attn_k_norm", "ffn2_norm")
WEIGHTS = ("ffn1_norm", "ffn1_w_in", "ffn1_w_out", "mix_norm", "w_in", "hgrn_lb_logits", "hgrn_out_norm", "attn_q_norm",
           "attn_k_norm", "w_branch_a", "w_branch_b", "w_out", "ffn2_norm", "ffn2_w_in", "ffn2_w_out")
SMALL_ROWS = 8


def _matmul_ready(name, a):
    return a.reshape(1, a.shape[0] * a.shape[1], a.shape[2]) if name in ROW_SHARDED else a


def _layer_small(small, l):
    s = {n: small[n][l].reshape(1, D_MODEL) for n in ("ffn1_norm", "mix_norm", "hgrn_out_norm", "ffn2_norm")}
    s.update({n: small[n][l] for n in ("attn_q_norm", "attn_k_norm")})
    return s


def _local_step(x, target, small, src):
    t = x.shape[0]
    cos, sin = _rope_tables(t)
    lbs = _lower_bounds(small["hgrn_lb_logits"])
    saved = []
    for l in range(2):
        sm = _layer_small(small, l)
        lb = lbs[l].reshape(1, D_MODEL)
        x, s1 = _ffn_fwd(x, sm["ffn1_norm"], src, l, "ffn1")
        x, s2 = _mix_fwd(x, sm, lb, cos, sin, src, l)
        x, s3 = _ffn_fwd(x, sm["ffn2_norm"], src, l, "ffn2")
        saved.append((sm, lb, s1, s2, s3))
    dx, sq = _loss_fwd_bwd(x, target, name="loss")
    small_rows = [None, None]
    for l in (1, 0):
        sm, lb, s1, s2, s3 = saved[l]
        dx, dg2 = _ffn_bwd(dx, s3, sm["ffn2_norm"], src, l, "ffn2")
        dx, g = _mix_bwd(dx, s2, sm, lb, cos, sin, src, l, lb_live=l > 0)
        dx, dg1 = _ffn_bwd(dx, s1, sm["ffn1_norm"], src, l, "ffn1")
        pad = lambda a: jnp.pad(a[:ATT_GROUPS].reshape(1, ATT_GROUPS * HEAD), ((0, 0), (0, D_MODEL - ATT_GROUPS * HEAD)))
        small_rows[l] = jnp.concatenate(
            [dg1, g["mix_norm"], g["lb"], g["hgrn_out_norm"], pad(g["attn_q_norm"]), pad(g["attn_k_norm"]), dg2,
             jnp.zeros((SMALL_ROWS - 7, D_MODEL), F32)], axis=0)
    return jnp.sum(sq), dx, jnp.concatenate(small_rows, axis=0)


def _coords():
    return lax.axis_index("x"), lax.axis_index("y"), lax.axis_index("c")


def _other_chips(x, y):
    return [(1 - x, y), (x, 1 - y), (1 - x, 1 - y)]


def _half_rows(rows, which):
    return pl.ds(which * (rows // 2), rows // 2)


def _gather_rider(shards):
    n = len(shards)

    def copies(w, full, sems):
        send, recv, fsend, frecv = sems
        x, y, c = _coords()
        slot = 2 * x + y
        chips = _other_chips(x, y)

        def copy(i, j, blk, src, pair, to):
            return pltpu.make_async_remote_copy(src_ref=src, dst_ref=blk, send_sem=pair[0].at[i * 3 + j],
                                                recv_sem=pair[1].at[i * 3 + j], device_id=to, device_id_type=MESH)

        def block(i, chip_slot, core):
            return full[i].at[chip_slot, _half_rows(shards[i].shape[0], core)]

        pairs = [(i, j, chip) for i in range(n) for j, chip in enumerate(chips)]

        def first():
            return [copy(i, j, block(i, slot, c), w[i].at[_half_rows(shards[i].shape[0], c)], (send, recv), (*chip, c))
                    for i, j, chip in pairs]

        def landed(core, pair):
            return [copy(i, j, block(i, 2 * chip[0] + chip[1], core), block(i, 2 * chip[0] + chip[1], core), pair, (x, y, 1 - c))
                    for i, j, chip in pairs]

        return first, landed

    def begin(w, full, sems):
        for cp in copies(w, full, sems)[0]():
            cp.start()

    def end(w, full, sems):
        first, landed = copies(w, full, sems)
        forwards = landed(lax.axis_index("c"), sems[2:])
        for arrival, forward in zip(landed(lax.axis_index("c"), sems[:2]), forwards):
            arrival.wait_recv()
            forward.start()
        for cp in landed(1 - lax.axis_index("c"), sems[2:]):
            cp.wait_recv()
        for cp in first() + forwards:
            cp.wait_send()

    out_shape = [jax.ShapeDtypeStruct((N_CHIPS,) + s.shape, s.dtype) for s in shards]
    return _Rider(shards, out_shape, [pltpu.SemaphoreType.DMA((3 * n,))] * 4, begin, end)


N_RECV = 7


def _scatter_rider(parts):
    n = len(parts)

    def copies(p, out, sems):
        send, recv = sems
        x, y, c = _coords()
        slot = 2 * x + y
        chips = _other_chips(x, y)

        def arrivals():
            return [pltpu.make_async_remote_copy(
                src_ref=out[i].at[k], dst_ref=out[i].at[k], send_sem=send.at[0], recv_sem=recv.at[i * N_RECV + k],
                device_id=(x, y, c), device_id_type=MESH) for i in range(n) for k in range(N_RECV)]

        sends = []
        for i in range(n):
            rows = parts[i].shape[1]
            for j, chip in enumerate(chips):
                for core in (0, 1):
                    sends.append(pltpu.make_async_remote_copy(
                        src_ref=p[i].at[2 * chip[0] + chip[1], _half_rows(rows, core)], dst_ref=out[i].at[2 * j + c],
                        send_sem=send.at[i * N_RECV + 2 * j + core], recv_sem=recv.at[i * N_RECV + 2 * j + c],
                        device_id=(*chip, core), device_id_type=MESH))
            sends.append(pltpu.make_async_remote_copy(
                src_ref=p[i].at[slot, _half_rows(rows, 1 - c)], dst_ref=out[i].at[6], send_sem=send.at[i * N_RECV + 6],
                recv_sem=recv.at[i * N_RECV + 6], device_id=(x, y, 1 - c), device_id_type=MESH))
        return sends, arrivals

    def begin(p, out, sems):
        for cp in copies(p, out, sems)[0]:
            cp.start()

    def end(p, out, sems):
        sends, arrivals = copies(p, out, sems)
        for cp in arrivals():
            cp.wait_recv()
        for cp in sends:
            cp.wait_send()

    out_shape = [jax.ShapeDtypeStruct((N_RECV, a.shape[1] // 2, a.shape[2]), a.dtype) for a in parts]
    return _Rider(parts, out_shape, [pltpu.SemaphoreType.DMA((N_RECV * n,))] * 2, begin, end)


def _run_alone(rider, name):
    _pcall(lambda: None, grid=(), in_specs=[], out_specs=[], out_shape=[], name=name, sem=(), args=(), rider=rider)
    return rider.result


def _sum_partials(own, parts, name):
    r, wd = own.shape
    tm = next(t for t in (256, 128, 64, 32, 16) if r % t == 0)

    def body(own_ref, p_ref, o_ref):
        acc = own_ref[...].astype(F32)
        for k in range(N_RECV):
            acc = acc + p_ref[k].astype(F32)
        o_ref[...] = acc

    return pl.pallas_call(
        body, grid=(r // tm,),
        in_specs=[pl.BlockSpec((tm, wd), lambda i: (i, 0)), pl.BlockSpec((N_RECV, tm, wd), lambda i: (0, i, 0))],
        out_specs=pl.BlockSpec((tm, wd), lambda i: (i, 0)), out_shape=jax.ShapeDtypeStruct((r, wd), F32),
        name=name, compiler_params=_params(("parallel",)))(own, parts)


def _exchange_halves(reduced, name):
    n = len(reduced)

    def body(*refs):
        r, out = refs[:n], refs[n:2 * n]
        send, recv = refs[2 * n:]
        x, y, c = _coords()
        sib = [pltpu.make_async_remote_copy(src_ref=r[i], dst_ref=out[i], send_sem=send.at[i], recv_sem=recv.at[i],
                                            device_id=(x, y, 1 - c), device_id_type=MESH) for i in range(n)]
        for cp in sib:
            cp.start()
        for cp in sib:
            cp.wait_recv()
        for cp in sib:
            cp.wait_send()

    out_shape = [jax.ShapeDtypeStruct(a.shape, a.dtype) for a in reduced]
    return pl.pallas_call(body, in_specs=[ANY] * n, out_specs=[ANY] * n, out_shape=out_shape,
                          scratch_shapes=[pltpu.SemaphoreType.DMA((n,))] * 2, name=name)(*reduced)


def _reduce_finish(parts, recv, tag):
    x, y, c = _coords()
    slot = 2 * x + y
    halves = []
    for i, (p, r) in enumerate(zip(parts, recv)):
        half = p.shape[1] // 2
        own = lax.dynamic_slice(p, (slot, c * half, 0), (1, half, p.shape[2]))[0]
        halves.append(_sum_partials(own, r, name=f"{tag}_sum{i}"))
    theirs = _exchange_halves(halves, name=tag + "_exchange")
    return [jnp.where(c == 0, jnp.concatenate([h, t], axis=0), jnp.concatenate([t, h], axis=0)) for h, t in zip(halves, theirs)]


GATHER_RIDES = {
    "l0_ffn1_in_act": ((0, "w_in"),),
    "l0_ffn1_out": ((0, "w_branch_a"), (0, "w_branch_b"), (0, "w_out")),
    "l0_mix_in": ((0, "ffn2_w_in"), (0, "ffn2_w_out"), (1, "ffn1_w_in"), (1, "ffn1_w_out")),
    "l0_mix_hgrn": ((1, "w_in"), (1, "w_branch_a"), (1, "w_branch_b"), (1, "w_out")),
    "l0_ffn2_in_act": ((1, "ffn2_w_in"), (1, "ffn2_w_out")),
}
ALONE_FIRST = ((0, "ffn1_w_in"), (0, "ffn1_w_out"))
SCATTER_RIDES = {
    "l1_mix_bwd_hgrn": ((1, "ffn2_w_in"), (1, "ffn2_w_out")),
    "l0_ffn2_bwd_win": ((1, "ffn1_w_in"),),
    "l0_ffn2_bwd_dh": ((1, "ffn1_w_out"), (1, "w_branch_a"), (1, "w_branch_b"), (1, "w_out")),
    "l0_mix_bwd_hgrn": ((1, "w_in"), (0, "ffn2_w_out")),
    "l0_mix_bwd_win": ((0, "ffn2_w_in"),),
    "l0_mix_bwd_dh": ((0, "w_in"),),
    "l0_ffn1_bwd_wout": ((0, "w_branch_a"), (0, "w_branch_b"), (0, "w_out")),
    "l0_ffn1_bwd_du_act": ((0, "ffn1_w_out"),),
    "l0_ffn1_bwd_dh": ((0, "ffn1_w_in"),),
}


class _Exchange:
    def __init__(self, shards):
        self.shards = shards
        self.pending = []
        self.full = {}
        self.parts = {}
        self.recv = {}

    def _gather(self, keys):
        return _gather_rider([self.shards[n][l] for l, n in keys]), "gather", list(keys)

    def _scatter(self, keys):
        return _scatter_rider([self.parts[k] for k in keys]), "scatter", list(keys)

    def _unpack(self):
        slot = 2 * lax.axis_index("x") + lax.axis_index("y")
        waiting = []
        for rider, kind, keys in self.pending:
            if rider.result is None:
                waiting.append((rider, kind, keys))
            elif kind == "gather":
                for (l, n), got in zip(keys, rider.result):
                    self.full[(l, n)] = lax.dynamic_update_slice(got, self.shards[n][l][None], (slot, 0, 0))
            else:
                self.recv.update(zip(keys, rider.result))
        self.pending = waiting

    def ride(self, host):
        if host in GATHER_RIDES:
            self.pending.append(self._gather(GATHER_RIDES[host]))
        elif host in SCATTER_RIDES:
            self.pending.append(self._scatter(SCATTER_RIDES[host]))
        else:
            return None
        return self.pending[-1][0]

    def weight(self, l, name):
        self._unpack()
        if (l, name) not in self.full:
            assert (l, name) in ALONE_FIRST, (l, name)
            job = self._gather(ALONE_FIRST)
            _run_alone(job[0], name="gather_first")
            self.pending.append(job)
            self._unpack()
        return _matmul_ready(name, self.full[(l, name)])

    def grads(self, l, partials):
        self.parts.update({(l, n): a for n, a in partials.items()})

    def reduce(self):
        self._unpack()
        assert not self.pending and set(self.recv) == set(self.parts)
        out = {}
        for l in range(2):
            done = _reduce_finish([self.parts[(l, n)] for n in BIG], [self.recv[(l, n)] for n in BIG], f"reduce_l{l}")
            out[l] = dict(zip(BIG, done))
        return {n: jnp.stack([out[0][n], out[1][n]], axis=0) for n in BIG}


def _all_reduce_small(rows):
    r = rows.shape[0]

    def body(x_ref, o_ref, buf, send, recv):
        x, y, c = _coords()
        me = 4 * x + 2 * y + c
        buf[me] = x_ref[...]
        copies = []
        for k in range(1, 8):
            peer = (x ^ (k >> 2), y ^ ((k >> 1) & 1), c ^ (k & 1))
            cp = pltpu.make_async_remote_copy(src_ref=x_ref, dst_ref=buf.at[me], send_sem=send.at[k - 1], recv_sem=recv.at[me],
                                              device_id=peer, device_id_type=MESH)
            cp.start()
            copies.append(cp)
        for k in range(1, 8):
            src = 4 * (x ^ (k >> 2)) + 2 * (y ^ ((k >> 1) & 1)) + (c ^ (k & 1))
            pltpu.make_async_remote_copy(src_ref=x_ref, dst_ref=buf.at[src], send_sem=send.at[0], recv_sem=recv.at[src],
                                         device_id=(x, y, c), device_id_type=MESH).wait_recv()
        for cp in copies:
            cp.wait_send()
        acc = buf[0]
        for k in range(1, 8):
            acc = acc + buf[k]
        o_ref[...] = acc

    vm = pl.BlockSpec(memory_space=pltpu.VMEM)
    return pl.pallas_call(
        body, in_specs=[vm], out_specs=vm, out_shape=jax.ShapeDtypeStruct(rows.shape, F32),
        scratch_shapes=[pltpu.VMEM((8, r, D_MODEL), F32), pltpu.SemaphoreType.DMA((7,)), pltpu.SemaphoreType.DMA((8,))],
        name="all_reduce_small")(rows)


def _adamw_math(w, g, m, v):
    m = ADAM_B1 * m + (1.0 - ADAM_B1) * g
    v = ADAM_B2 * v + (1.0 - ADAM_B2) * (g * g)
    m_hat = m / (1.0 - ADAM_B1 ** ADAM_STEP)
    v_hat = v / (1.0 - ADAM_B2 ** ADAM_STEP)
    return -ADAM_LR * (m_hat / (jnp.sqrt(v_hat) + ADAM_EPS) + ADAM_WD * w), m, v


def _adamw(w, g, m, v, name):
    shape = w.shape
    cols = shape[-1]
    flat = lambda a: a.reshape(-1, cols)
    rows = flat(w).shape[0]
    tm = 128 if rows % 128 == 0 else rows
    ins = [('t', flat(a), cols, 0) for a in (w, g, m, v)]
    res = _ew(_adamw_math, ins, [('t', cols, F32)] * 3, rows=rows, tm=tm, name=name)
    return [a.reshape(shape) for a in res]


def _small_update(sums, logits, w, m, v):
    def body(s_ref, lg_ref, w_ref, m_ref, v_ref, g_ref, d_ref, nm_ref, nv_ref):
        s = s_ref[...]
        l0, l1 = lg_ref[0:1, :], lg_ref[1:2, :]
        mx = jnp.maximum(l0, l1)
        e0, e1 = jnp.exp(l0 - mx), jnp.exp(l1 - mx)
        sm0, sm1 = e0 / (e0 + e1), e1 / (e0 + e1)
        dl1 = s_ref[SMALL_ROWS + 2:SMALL_ROWS + 3, :] * sm0 * sm1
        row = lax.broadcasted_iota(jnp.int32, s.shape, 0)
        g = jnp.where(row == 2, -dl1, jnp.where(row == SMALL_ROWS + 2, dl1, s))
        d, nm, nv = _adamw_math(w_ref[...], g, m_ref[...], v_ref[...])
        g_ref[...] = g
        d_ref[...] = d
        nm_ref[...] = nm
        nv_ref[...] = nv

    vm = pl.BlockSpec(memory_space=pltpu.VMEM)
    return pl.pallas_call(body, in_specs=[vm] * 5, out_specs=[vm] * 4,
                          out_shape=[jax.ShapeDtypeStruct(sums.shape, F32)] * 4, name="small_update")(sums, logits, w, m, v)


def _pack_small(vals):
    rows = []
    for l in range(2):
        for n in ("ffn1_norm", "mix_norm", "hgrn_lb_logits", "hgrn_out_norm", "attn_q_norm", "attn_k_norm", "ffn2_norm"):
            a = vals[n][l].reshape(1, -1)
            rows.append(jnp.pad(a, ((0, 0), (0, D_MODEL - a.shape[1]))))
        rows.append(jnp.zeros((SMALL_ROWS - 7, D_MODEL), F32))
    return jnp.concatenate(rows, axis=0)


def _unpack_small(packed):
    out = {}
    for k, n in enumerate(("ffn1_norm", "mix_norm", "hgrn_lb_logits", "hgrn_out_norm", "attn_q_norm", "attn_k_norm", "ffn2_norm")):
        a = jnp.stack([packed[k], packed[SMALL_ROWS + k]], axis=0)
        out[n] = a[:, :ATT_GROUPS * HEAD].reshape(2, ATT_GROUPS, HEAD) if n.startswith("attn") else a
    return out


def kernel(x, ffn1_norm, ffn1_w_in, ffn1_w_out, mix_norm, w_in, hgrn_lb_logits, hgrn_out_norm, attn_q_norm, attn_k_norm, w_branch_a, w_branch_b, w_out, ffn2_norm, ffn2_w_in, ffn2_w_out, loss_target, m_ffn1_norm, m_ffn1_w_in, m_ffn1_w_out, m_mix_norm, m_w_in, m_hgrn_lb_logits, m_hgrn_out_norm, m_attn_q_norm, m_attn_k_norm, m_w_branch_a, m_w_branch_b, m_w_out, m_ffn2_norm, m_ffn2_w_in, m_ffn2_w_out, v_ffn1_norm, v_ffn1_w_in, v_ffn1_w_out, v_mix_norm, v_w_in, v_hgrn_lb_logits, v_hgrn_out_norm, v_attn_q_norm, v_attn_k_norm, v_w_branch_a, v_w_branch_b, v_w_out, v_ffn2_norm, v_ffn2_w_in, v_ffn2_w_out):
    a = locals()
    w = {n: a[n] for n in WEIGHTS}
    m = {n: a["m_" + n] for n in WEIGHTS}
    v = {n: a["v_" + n] for n in WEIGHTS}

    exchange = _Exchange({n: w[n].astype(BF16) for n in BIG})
    small = {n: w[n] for n in SMALL}
    sq, grad_x, small_rows = _local_step(x[0], loss_target[0], small, exchange)
    loss = lax.psum(sq, ("x", "y", "c")) * (0.5 / D_MODEL)
    grads = exchange.reduce()

    sums = _all_reduce_small(small_rows)
    g_s, d_s, m_s, v_s = _small_update(sums, w["hgrn_lb_logits"], _pack_small(small), _pack_small({n: m[n] for n in SMALL}),
                                       _pack_small({n: v[n] for n in SMALL}))
    grads.update(_unpack_small(g_s))
    delta, new_m, new_v = _unpack_small(d_s), _unpack_small(m_s), _unpack_small(v_s)
    for n in BIG:
        delta[n], new_m[n], new_v[n] = _adamw(w[n], grads[n], m[n], v[n], name="adamw_" + n)

    return (loss, grad_x[None], *[grads[n] for n in WEIGHTS], *[delta[n] for n in WEIGHTS],
            *[new_m[n] for n in WEIGHTS], *[new_v[n] for n in WEIGHTS])
```

```python
import functools

import jax
import jax.numpy as jnp
from jax import lax
from jax.experimental import pallas as pl
from jax.experimental.pallas import tpu as pltpu

F32 = jnp.float32
BF16 = jnp.bfloat16
MESH = pl.DeviceIdType.MESH

D_MODEL = 1024
D_FF = 2816
N_CHIPS = 4
HEAD = 128
HG_HEADS = 8
HG_CHUNK = 64
ATT_GROUPS = 3
ATT_HEADS = 4
ATT_GW = ATT_HEADS * HEAD
DILATIONS = (1, 4, 16)
ATT_BLK = 128
ATT_STEP_BLOCKS = 4
P_IN = 10752
CB_AQ, CB_AK, CB_AV, CB_GA, CB_GB = 8, 11, 14, 17, 19
EPS = 1e-6
ROPE_THETA = 10000.0
ADAM_LR, ADAM_B1, ADAM_B2, ADAM_EPS, ADAM_WD, ADAM_STEP = 0.001, 0.9, 0.999, 1e-08, 0.01, 10
VMEM_LIMIT_V7X = 56 * 1024 * 1024
NEG = -1e30


def _params(sem):
    return pltpu.CompilerParams(dimension_semantics=sem, vmem_limit_bytes=VMEM_LIMIT_V7X)


def _sig(x):
    return 1.0 / (1.0 + jnp.exp(-x))


def _dot(a, b):
    return jnp.dot(a, b, preferred_element_type=F32)


def _dot_nt(a, b):
    return lax.dot_general(a, b, (((1,), (1,)), ((), ())), preferred_element_type=F32)


def _dot_tn(a, b):
    return lax.dot_general(a, b, (((0,), (0,)), ((), ())), preferred_element_type=F32)


def _bf(x):
    return x.astype(BF16)


ANY = pl.BlockSpec(memory_space=pl.ANY)


class _Rider:
    def __init__(self, args, out_shape, sems, begin, end):
        self.args, self.out_shape, self.sems, self.begin, self.end = list(args), list(out_shape), list(sems), begin, end
        self.result = None


def _pcall(body, *, grid, in_specs, out_specs, out_shape, name, sem, args, scratch_shapes=(), rider=None):
    multi = isinstance(out_shape, (list, tuple))
    o_specs = list(out_specs) if multi else [out_specs]
    o_shape = list(out_shape) if multi else [out_shape]
    if rider is None:
        res = pl.pallas_call(body, grid=grid, in_specs=list(in_specs), out_specs=o_specs, out_shape=o_shape,
                             scratch_shapes=list(scratch_shapes), name=name, compiler_params=_params(sem))(*args)
        return list(res) if multi else res[0]
    counts = [len(in_specs), len(rider.args), len(o_specs), len(rider.out_shape), len(scratch_shapes)]

    def wrapped(*refs):
        groups, at = [], 0
        for c in counts:
            groups.append(refs[at:at + c])
            at += c
        h_in, r_in, h_out, r_out, h_scratch = groups
        r_sems = refs[at:]
        if grid:
            ids = [pl.program_id(a) for a in range(len(grid))]
            first = functools.reduce(jnp.logical_and, [i == 0 for i in ids])
            last = functools.reduce(jnp.logical_and, [i == g - 1 for i, g in zip(ids, grid)])
            pl.when(first)(lambda: rider.begin(r_in, r_out, r_sems))
            body(*h_in, *h_out, *h_scratch)
            pl.when(last)(lambda: rider.end(r_in, r_out, r_sems))
        else:
            rider.begin(r_in, r_out, r_sems)
            body(*h_in, *h_out, *h_scratch)
            rider.end(r_in, r_out, r_sems)

    res = pl.pallas_call(
        wrapped, grid=grid, in_specs=list(in_specs) + [ANY] * counts[1], out_specs=o_specs + [ANY] * counts[3],
        out_shape=o_shape + rider.out_shape, scratch_shapes=list(scratch_shapes) + rider.sems, name=name,
        compiler_params=_params(("arbitrary",) * len(grid)))(*args, *rider.args)
    rider.result = list(res[counts[2]:])
    return list(res[:counts[2]]) if multi else res[0]


def _mm_nn(a, b3, *, name, tm, tn, out_dtype, res=None, alpha=1.0, rider=None):
    m, k = a.shape
    nb, _, nw = b3.shape
    per = nw // tn
    assert nw % tn == 0 and m % tm == 0
    has_res = res is not None

    def body(*refs):
        if has_res:
            a_ref, b_ref, r_ref, o_ref = refs
        else:
            a_ref, b_ref, o_ref = refs
        acc = _dot(_bf(a_ref[...]), b_ref[...])
        if alpha != 1.0:
            acc = alpha * acc
        if has_res:
            acc = r_ref[...] + acc
        o_ref[...] = acc.astype(o_ref.dtype)

    in_specs = [pl.BlockSpec((tm, k), lambda i, j: (i, 0)),
                pl.BlockSpec((None, k, tn), lambda i, j: (j // per, 0, j % per))]
    args = [a, b3]
    if has_res:
        in_specs.append(pl.BlockSpec((tm, tn), lambda i, j: (i, j)))
        args.append(res)
    return _pcall(body, grid=(m // tm, nb * per), in_specs=in_specs, out_specs=pl.BlockSpec((tm, tn), lambda i, j: (i, j)),
                  out_shape=jax.ShapeDtypeStruct((m, nb * nw), out_dtype), name=name, sem=("parallel", "arbitrary"),
                  args=args, rider=rider)


def _mm_nt(d, b3, *, name, tm, tp, tn, out_dtype, alpha=1.0, rider=None, norm=None):
    m, n = d.shape
    nb, p, nw = b3.shape
    per = nw // tn
    nk = n // tn
    assert nb * nw == n and nw % tn == 0 and p % tp == 0 and m % tm == 0 and (norm is None or tp == p)

    def body(d_ref, b_ref, *refs):
        kk = pl.program_id(2)
        acc_ref = refs[-1]

        @pl.when(kk == 0)
        def _():
            acc_ref[...] = jnp.zeros_like(acc_ref)

        acc_ref[...] += _dot_nt(_bf(d_ref[...]), b_ref[...])

        if norm is None:
            @pl.when(kk == nk - 1)
            def _():
                refs[0][...] = (alpha * acc_ref[...]).astype(refs[0].dtype)
        else:
            x_ref, g_ref, dx_ref, o_ref, dg_ref = refs[:5]

            @pl.when(jnp.logical_and(pl.program_id(0) == 0, kk == 0))
            def _():
                dg_ref[...] = jnp.zeros_like(dg_ref)

            @pl.when(kk == nk - 1)
            def _():
                dh = alpha * acc_ref[...]
                xv = x_ref[...]
                r = _rms_rows(xv)
                xh = xv * r
                dxh = dh * g_ref[...]
                o_ref[...] = dx_ref[...] + r * (dxh - xh * jnp.mean(dxh * xh, axis=1, keepdims=True))
                dg_ref[...] += jnp.sum(dh * xh, axis=0, keepdims=True)

    in_specs = [pl.BlockSpec((tm, tn), lambda i, j, kk: (i, kk)),
                pl.BlockSpec((None, tp, tn), lambda i, j, kk: (kk // per, j, kk % per))]
    tile = pl.BlockSpec((tm, tp), lambda i, j, kk: (i, j))
    if norm is None:
        return _pcall(body, grid=(m // tm, p // tp, nk), in_specs=in_specs, out_specs=tile,
                      out_shape=jax.ShapeDtypeStruct((m, p), out_dtype), scratch_shapes=[pltpu.VMEM((tm, tp), F32)],
                      name=name, sem=("parallel", "parallel", "arbitrary"), args=(d, b3), rider=rider)
    x, g, dx = norm
    row = pl.BlockSpec((1, p), lambda i, j, kk: (0, 0))
    return _pcall(body, grid=(m // tm, 1, nk), in_specs=in_specs + [tile, row, tile], out_specs=[tile, row],
                  out_shape=[jax.ShapeDtypeStruct((m, p), F32), jax.ShapeDtypeStruct((1, p), F32)],
                  scratch_shapes=[pltpu.VMEM((tm, tp), F32)], name=name, sem=("arbitrary", "arbitrary", "arbitrary"),
                  args=(d, b3, x, g, dx), rider=rider)


def _mm_tn(a, d, *, nb, name, tm, tk, tn, alpha=1.0, rider=None):
    m, k = a.shape
    _, n = d.shape
    nw = n // nb
    per = nw // tn
    nm = m // tm
    assert nw % tn == 0 and k % tk == 0 and m % tm == 0

    def body(a_ref, d_ref, o_ref, acc_ref):
        mm = pl.program_id(2)

        @pl.when(mm == 0)
        def _():
            acc_ref[...] = jnp.zeros_like(acc_ref)

        acc_ref[...] += _dot_tn(_bf(a_ref[...]), _bf(d_ref[...]))

        @pl.when(mm == nm - 1)
        def _():
            o_ref[...] = (alpha * acc_ref[...]).astype(o_ref.dtype)

    return _pcall(
        body, grid=(k // tk, nb * per, nm),
        in_specs=[pl.BlockSpec((tm, tk), lambda i, j, mm: (mm, i)),
                  pl.BlockSpec((tm, tn), lambda i, j, mm: (mm, j))],
        out_specs=pl.BlockSpec((None, tk, tn), lambda i, j, mm: (j // per, i, j % per)),
        out_shape=jax.ShapeDtypeStruct((nb, k, nw), BF16),
        scratch_shapes=[pltpu.VMEM((tk, tn), F32)],
        name=name, sem=("parallel", "parallel", "arbitrary"), args=(a, d), rider=rider)


def _rows_from_view(ref, buf, w, d, tm):
    for k in range(d):
        for c in range(w // HEAD):
            lanes = slice(k * w + c * HEAD, k * w + (c + 1) * HEAD)
            buf.at[c][pl.ds(k, tm // d, stride=d), :] = ref[:, lanes].astype(F32)
    return _cat([buf[c] for c in range(w // HEAD)])


def _ew(fn, ins, outs, *, rows, tm, name):
    in_specs, args, scratch = [], [], []
    for s in ins:
        if s[0] == 't':
            _, arr, w, cb = s
            in_specs.append(pl.BlockSpec((tm, w), lambda i, cb=cb: (i, cb)))
        elif s[0] == 'v':
            _, arr, w, d = s
            in_specs.append(pl.BlockSpec((tm // d, d * w), lambda i: (i, 0)))
            scratch.append(pltpu.VMEM((w // HEAD, tm, HEAD), F32))
        else:
            arr = s[1]
            in_specs.append(pl.BlockSpec(arr.shape, lambda i, nd=arr.ndim: (0,) * nd))
        args.append(arr)
    out_specs, out_shape = [], []
    for s in outs:
        if s[0] == 't':
            _, w, dt = s
            out_specs.append(pl.BlockSpec((tm, w), lambda i: (i, 0)))
            out_shape.append(jax.ShapeDtypeStruct((rows, w), dt))
        elif s[0] == 'v':
            _, w, dt, d = s
            out_specs.append(pl.BlockSpec((tm // d, d * w), lambda i: (i, 0)))
            out_shape.append(jax.ShapeDtypeStruct((rows // d, d * w), dt))
            scratch.append(pltpu.VMEM((w // HEAD, tm, HEAD), F32))
        else:
            out_specs.append(pl.BlockSpec(s[1], lambda i: (0, 0)))
            out_shape.append(jax.ShapeDtypeStruct(s[1], F32))
    n_in, n_out = len(ins), len(outs)

    def body(*refs):
        bufs = list(refs[n_in + n_out:])
        vals = []
        for r, s in zip(refs[:n_in], ins):
            if s[0] == 'v':
                vals.append(_rows_from_view(r, bufs.pop(0), s[2], s[3], tm))
            else:
                vals.append(r[...])
        res = fn(*vals)
        if not isinstance(res, (tuple, list)):
            res = (res,)
        for r, s, v in zip(refs[n_in:n_in + n_out], outs, res):
            if s[0] == 't':
                r[...] = v.astype(r.dtype)
            elif s[0] == 'v':
                w, d, buf = s[1], s[3], bufs.pop(0)
                for c in range(w // HEAD):
                    buf[c] = v[:, c * HEAD:(c + 1) * HEAD].astype(F32)
                for k in range(d):
                    for c in range(w // HEAD):
                        lanes = slice(k * w + c * HEAD, k * w + (c + 1) * HEAD)
                        r[:, lanes] = buf.at[c][pl.ds(k, tm // d, stride=d), :].astype(r.dtype)
            else:
                @pl.when(pl.program_id(0) == 0)
                def _(r=r):
                    r[...] = jnp.zeros_like(r)

                r[...] += v

    res = pl.pallas_call(
        body, grid=(rows // tm,), in_specs=in_specs, out_specs=out_specs, out_shape=out_shape, scratch_shapes=scratch,
        name=name, compiler_params=_params(("arbitrary",)))(*args)
    return res


def _tile(arr, w, g):
    return ('t', arr, w, 0) if DILATIONS[g] == 1 else ('v', arr, w, DILATIONS[g])


def _tile_out(w, dtype, g):
    return ('t', w, dtype) if DILATIONS[g] == 1 else ('v', w, dtype, DILATIONS[g])


def _heads(x):
    return [x[:, h * HEAD:(h + 1) * HEAD] for h in range(x.shape[1] // HEAD)]


def _cat(xs):
    return jnp.concatenate(xs, axis=1)


def _head_mean(x):
    return _cat([jnp.broadcast_to(jnp.mean(h, axis=1, keepdims=True), h.shape) for h in _heads(x)])


def _rms_rows(x):
    return lax.rsqrt(jnp.mean(x * x, axis=1, keepdims=True) + EPS)


def _norm_fwd(x, g, name):
    return _ew(lambda xv, gv: xv * _rms_rows(xv) * gv,
               [('t', x, D_MODEL, 0), ('f', g)], [('t', D_MODEL, BF16)], rows=x.shape[0], tm=512, name=name)[0]


def _loss_fwd_bwd(y, target, name):
    def fn(yv, tv):
        e = yv - tv
        return e * (1.0 / D_MODEL), jnp.sum(e * e, axis=0, keepdims=True)

    return _ew(fn, [('t', y, D_MODEL, 0), ('t', target, D_MODEL, 0)], [('t', D_MODEL, F32), ('acc', (1, D_MODEL))],
               rows=y.shape[0], tm=512, name=name)


def _rot(x):
    sgn = jnp.where(lax.broadcasted_iota(jnp.int32, x.shape, 1) < HEAD // 2, -1.0, 1.0)
    return pltpu.roll(x, HEAD // 2, 1) * sgn


def _gain_rows(qn, kn):
    return [a[g:g + 1] for a in (qn, kn) for g in range(ATT_GROUPS)]


def _qk_fwd(proj, cos, sin, qn, kn, name):
    def fn(*v):
        xs, cosv, sinv, gains, vs = v[:6], v[6], v[7], v[8:14], v[14:17]
        outs = []
        for j, x in enumerate(xs):
            gain = gains[j]
            ys = []
            for xh in _heads(x.astype(F32)):
                xn = xh * _rms_rows(xh) * gain
                ys.append(xn * cosv + _rot(xn) * sinv)
            outs.append(_cat(ys))
        return outs + list(vs)

    ins = ([('t', proj, 512, CB_AQ + j) for j in range(6)] + [('t', cos, HEAD, 0), ('t', sin, HEAD, 0)]
           + [('f', a) for a in _gain_rows(qn, kn)] + [('t', proj, 512, CB_AV + g) for g in range(ATT_GROUPS)])
    return _ew(fn, ins, [_tile_out(ATT_GW, BF16, j % ATT_GROUPS) for j in range(9)], rows=proj.shape[0], tm=512, name=name)


def _qk_bwd(dqk, proj, cos, sin, qn, kn, name):
    def fn(*v):
        ds, xs, cosv, sinv, gains = v[:6], v[6:12], v[12], v[13], v[14:20]
        rows8 = lax.broadcasted_iota(jnp.int32, (8, HEAD), 0)
        outs, dgs = [], [jnp.zeros((8, HEAD), F32)] * 2
        for j in range(6):
            gain = gains[j]
            dx, dg = [], jnp.zeros((1, HEAD), F32)
            for dyh, xh in zip(_heads(ds[j]), _heads(xs[j].astype(F32))):
                r = _rms_rows(xh)
                xhat = xh * r
                dxn = dyh * cosv - _rot(dyh * sinv)
                dg = dg + jnp.sum(dxn * xhat, axis=0, keepdims=True)
                dxh = dxn * gain
                dx.append(r * (dxh - xhat * jnp.mean(dxh * xhat, axis=1, keepdims=True)))
            outs.append(_cat(dx))
            dgs[j // 3] = dgs[j // 3] + jnp.where(rows8 == j % 3, dg, 0.0)
        return _cat(outs), dgs[0], dgs[1]

    ins = ([_tile(a, ATT_GW, j % ATT_GROUPS) for j, a in enumerate(dqk)] + [('t', proj, 512, CB_AQ + j) for j in range(6)]
           + [('t', cos, HEAD, 0), ('t', sin, HEAD, 0)] + [('f', a) for a in _gain_rows(qn, kn)])
    return _ew(fn, ins, [('t', 6 * ATT_GW, BF16), ('acc', (8, HEAD)), ('acc', (8, HEAD))],
               rows=proj.shape[0], tm=256, name=name)


def _pick(x, h):
    lanes = lax.broadcasted_iota(jnp.int32, x.shape, 1)
    return jnp.sum(jnp.where(lanes == h, x, 0.0), axis=1, keepdims=True)


def _spread(x):
    return _cat([jnp.broadcast_to(_pick(x, h), (x.shape[0], HEAD)) for h in range(ATT_HEADS)])


def _compact(x):
    lanes = lax.broadcasted_iota(jnp.int32, (x.shape[0], HEAD), 1)
    out = jnp.zeros((x.shape[0], HEAD), F32)
    for h, xh in enumerate(_heads(x)):
        out = jnp.where(lanes == h, xh, out)
    return out


def _group_weights(l0, l1, l2):
    l0, l1, l2 = _spread(l0), _spread(l1), _spread(l2)
    m = jnp.maximum(jnp.maximum(l0, l1), l2)
    e0, e1, e2 = jnp.exp(l0 - m), jnp.exp(l1 - m), jnp.exp(l2 - m)
    inv = 1.0 / (e0 + e1 + e2)
    return e0 * inv, e1 * inv, e2 * inv


def _merge_fwd(outs, lses, name):
    def fn(o0, o1, o2, l0, l1, l2):
        a0, a1, a2 = _group_weights(l0, l1, l2)
        return a0 * o0 + a1 * o1 + a2 * o2

    ins = [_tile(a, ATT_GW, g) for g, a in enumerate(outs)] + [_tile(a, HEAD, g) for g, a in enumerate(lses)]
    return _ew(fn, ins, [('t', ATT_GW, BF16)], rows=outs[0].shape[0], tm=512, name=name)[0]


def _merge_bwd(dob, outs, lses, name):
    def fn(dov, o0, o1, o2, l0, l1, l2):
        a0, a1, a2 = _group_weights(l0, l1, l2)
        ob = a0 * o0 + a1 * o1 + a2 * o2
        s = _head_mean(dov * ob) * float(HEAD)
        return a0 * dov, a1 * dov, a2 * dov, _compact(a0 * s), _compact(a1 * s), _compact(a2 * s)

    ins = ([('t', dob, ATT_GW, 0)] + [_tile(a, ATT_GW, g) for g, a in enumerate(outs)]
           + [_tile(a, HEAD, g) for g, a in enumerate(lses)])
    groups = range(ATT_GROUPS)
    return _ew(fn, ins, [_tile_out(ATT_GW, BF16, g) for g in groups] + [_tile_out(HEAD, F32, g) for g in groups],
               rows=dob.shape[0], tm=512, name=name)


HG_ROWS = 256


def _hg_gates(hq, hf, hi, lbv):
    sig = _sig(hf)
    f = lbv + (1.0 - lbv) * sig
    return hq * _sig(hq), 1.0 - f, hi, jnp.log(f), sig, f


def _split3(x):
    hi = _bf(x)
    r1 = x - hi.astype(F32)
    mid = _bf(r1)
    return hi, mid, _bf(r1 - mid.astype(F32))


def _tri_dot(tri, x):
    hi, mid, lo = _split3(x)
    return _dot(tri, hi) + _dot(tri, mid) + _dot(tri, lo)


def _row(x, i):
    rows = lax.broadcasted_iota(jnp.int32, x.shape, 0)
    return jnp.sum(jnp.where(rows == i, x, 0.0), axis=0, keepdims=True)


def _hg_decay(logf, q, k):
    c = HG_CHUNK
    row = lax.broadcasted_iota(jnp.int32, (c, c), 0)
    col = lax.broadcasted_iota(jnp.int32, (c, c), 1)
    g = _tri_dot((row >= col).astype(BF16), logf)
    gm = _row(g, c // 2 - 1)
    gl = _row(g, c - 1)
    return g, gm, gl, q * jnp.exp(g), q * jnp.exp(g - gm), k * jnp.exp(gm - g), k * jnp.exp(gl - g)


def _hg_out_fwd(o, hg, gain):
    r = lax.rsqrt(_head_mean(o * o) + EPS)
    return o * r * gain * (hg * _sig(hg))


def _hgrn_fwd(proj, hf, lb, gain, name, rider=None):
    t = proj.shape[0]
    nck = HG_ROWS // HG_CHUNK

    def body(hq_ref, hf_ref, hi_ref, hg_ref, lb_ref, gn_ref, o_ref, oa_ref, sall_ref, st_ref):
        @pl.when(pl.program_id(0) == 0)
        def _():
            st_ref[...] = jnp.zeros_like(st_ref)

        lbv = lb_ref[...]
        gnv = gn_ref[...]
        c = HG_CHUNK
        mask = lax.broadcasted_iota(jnp.int32, (c, c), 0) >= lax.broadcasted_iota(jnp.int32, (c, c), 1)

        def chunk(cc, carry):
            sl = pl.ds(pl.multiple_of(cc * c, c), c)
            q, k, v, logf, _, _ = _hg_gates(hq_ref[sl, :].astype(F32), hf_ref[sl, :], hi_ref[sl, :].astype(F32), lbv)
            _, _, gl, qg, qt, kt, kd = _hg_decay(logf, q, k)
            egl = jnp.exp(gl)
            os = []
            for h in range(HG_HEADS):
                hs = slice(h * HEAD, (h + 1) * HEAD)
                st = st_ref[h]
                sall_ref[cc, h] = st
                a = jnp.where(mask, _dot_nt(_bf(qt[:, hs]), _bf(kt[:, hs])), 0.0)
                os.append(_dot(_bf(a), _bf(v[:, hs])) + _dot_nt(_bf(qg[:, hs]), _bf(st)))
                st_ref[h] = egl[:, hs] * st + _dot_tn(_bf(v[:, hs]), _bf(kd[:, hs]))
            o = _cat(os)
            o_ref[sl, :] = o
            oa_ref[sl, :] = _hg_out_fwd(o, hg_ref[sl, :].astype(F32), gnv).astype(oa_ref.dtype)
            return carry

        lax.fori_loop(0, nck, chunk, 0)

    col = lambda j: pl.BlockSpec((HG_ROWS, D_MODEL), lambda i, j=j: (i, j))
    small = pl.BlockSpec((1, D_MODEL), lambda i: (0, 0))
    return _pcall(
        body, grid=(t // HG_ROWS,),
        in_specs=[col(0), col(0), col(2), col(3), small, small],
        out_specs=[col(0), col(0), pl.BlockSpec((nck, HG_HEADS, HEAD, HEAD), lambda i: (i, 0, 0, 0))],
        out_shape=[jax.ShapeDtypeStruct((t, D_MODEL), F32), jax.ShapeDtypeStruct((t, D_MODEL), BF16),
                   jax.ShapeDtypeStruct((t // HG_CHUNK, HG_HEADS, HEAD, HEAD), F32)],
        scratch_shapes=[pltpu.VMEM((HG_HEADS, HEAD, HEAD), F32)],
        name=name, sem=("arbitrary",), args=(proj, hf, proj, proj, lb, gain), rider=rider)


def _terms(x, precise):
    hi = _bf(x)
    return (hi, _bf(x - hi.astype(F32))) if precise else (hi,)


def _mm(dot, a, b):
    out = dot(a[0], b[0])
    if len(a) > 1:
        out = out + dot(a[1], b[0])
    if len(b) > 1:
        out = out + dot(a[0], b[1])
    return out


def _hgrn_bwd(doa, oscan, proj, hf, sall, lb, gain, dqk, dvs, dgab, name, precise, rider=None):
    t = proj.shape[0]
    nck = HG_ROWS // HG_CHUNK
    nsteps = t // HG_ROWS
    terms = functools.partial(_terms, precise=precise)
    n_view = sum(d > 1 for d in DILATIONS)

    def body(doa_ref, os_ref, hq_ref, hf_ref, hi_ref, hg_ref, sall_ref, lb_ref, gn_ref, dqk_ref, dv0_ref, dv1_ref,
             dv2_ref, dgab_ref, dproj_ref, dgn_ref, dlb_ref, dst_ref, *bufs):
        @pl.when(pl.program_id(0) == 0)
        def _():
            dst_ref[...] = jnp.zeros_like(dst_ref)
            dgn_ref[...] = jnp.zeros_like(dgn_ref)
            dlb_ref[...] = jnp.zeros_like(dlb_ref)

        at = 4 * D_MODEL
        dproj_ref[:, at:at + 6 * ATT_GW] = dqk_ref[...]
        at += 6 * ATT_GW
        spare = list(bufs)
        for d, dv_ref in zip(DILATIONS, (dv0_ref, dv1_ref, dv2_ref)):
            dv = dv_ref[...] if d == 1 else _rows_from_view(dv_ref, spare.pop(0), ATT_GW, d, HG_ROWS)
            dproj_ref[:, at:at + ATT_GW] = dv.astype(dproj_ref.dtype)
            at += ATT_GW
        dproj_ref[:, at:] = dgab_ref[...]

        lbv = lb_ref[...]
        gnv = gn_ref[...]
        c = HG_CHUNK
        row = lax.broadcasted_iota(jnp.int32, (c, c), 0)
        colm = lax.broadcasted_iota(jnp.int32, (c, c), 1)
        mask = row >= colm
        triu = (row <= colm).astype(BF16)
        last = lax.broadcasted_iota(jnp.int32, (c, HEAD), 0) == c - 1

        def chunk(ci, carry):
            cc = nck - 1 - ci
            sl = pl.ds(pl.multiple_of(cc * c, c), c)
            hq, hg = hq_ref[sl, :].astype(F32), hg_ref[sl, :].astype(F32)
            q, k, v, logf, sig, f = _hg_gates(hq, hf_ref[sl, :], hi_ref[sl, :].astype(F32), lbv)
            g, gm, gl, qg, qt, kt, kd = _hg_decay(logf, q, k)
            egl = jnp.exp(gl)
            o = os_ref[sl, :]
            dy = doa_ref[sl, :]
            r = lax.rsqrt(_head_mean(o * o) + EPS)
            oh = o * r
            sg = _sig(hg)
            silu_g = hg * sg
            dgn_ref[...] += jnp.sum(dy * oh * silu_g, axis=0, keepdims=True)
            dhg = dy * oh * gnv * (sg * (1.0 + hg * (1.0 - sg)))
            doh = dy * gnv * silu_g
            do = r * (doh - oh * _head_mean(doh * oh))
            dqs, dks, dvs, dgs = [], [], [], []
            for h in range(HG_HEADS):
                hs = slice(h * HEAD, (h + 1) * HEAD)
                st = sall_ref[cc, h]
                dst = dst_ref[h]
                qt_h, kt_h, qg_h, kd_h = qt[:, hs], kt[:, hs], qg[:, hs], kd[:, hs]
                do_p, v_p, qt_p, kt_p, qg_p = terms(do[:, hs]), terms(v[:, hs]), terms(qt_h), terms(kt_h), terms(qg_h)
                st_p, dst_p = terms(st), terms(dst)
                a = jnp.where(mask, _dot_nt(qt_p[0], kt_p[0]), 0.0)
                da = terms(jnp.where(mask, _mm(_dot_nt, do_p, v_p), 0.0))
                dqt = _mm(_dot, da, kt_p)
                dkt = _mm(_dot_tn, da, qt_p)
                dqg = _mm(_dot, do_p, st_p)
                dv = _dot_tn(_bf(a), do_p[0]) + _dot_nt(_bf(kd_h), dst_p[0])
                dkd = _mm(_dot, v_p, dst_p)
                dgl = egl[:, hs] * jnp.sum(st * dst, axis=0, keepdims=True) + jnp.sum(dkd * kd_h, axis=0, keepdims=True)
                dst_ref[h] = egl[:, hs] * dst + _mm(_dot_tn, do_p, qg_p)
                g_h = g[:, hs]
                gm_h = gm[:, hs]
                gl_h = gl[:, hs]
                dqs.append(dqt * jnp.exp(g_h - gm_h) + dqg * jnp.exp(g_h))
                dks.append(dkt * jnp.exp(gm_h - g_h) + dkd * jnp.exp(gl_h - g_h))
                dvs.append(dv)
                dgs.append(dqt * qt_h - dkt * kt_h + dqg * qg_h - dkd * kd_h + jnp.where(last, dgl, 0.0))
            dq, dk, dv, dg = _cat(dqs), _cat(dks), _cat(dvs), _cat(dgs)
            dlogf = _tri_dot(triu, dg)
            df = dlogf / f - dk
            dlb_ref[...] += jnp.sum(df * (1.0 - sig), axis=0, keepdims=True)
            dhf = df * (1.0 - lbv) * sig * (1.0 - sig)
            sq = _sig(hq)
            dhq = dq * (sq * (1.0 + hq * (1.0 - sq)))
            dproj_ref[sl, :4 * D_MODEL] = _cat([dhq, dhf, dv, dhg]).astype(dproj_ref.dtype)
            return carry

        lax.fori_loop(0, nck, chunk, 0)

    rev = lambda j: pl.BlockSpec((HG_ROWS, D_MODEL), lambda i, j=j: (nsteps - 1 - i, j))
    rows = lambda a, d=1: pl.BlockSpec((HG_ROWS // d, a.shape[1]), lambda i: (nsteps - 1 - i, 0))
    small = pl.BlockSpec((1, D_MODEL), lambda i: (0, 0))
    return _pcall(
        body, grid=(nsteps,),
        in_specs=[rev(0), rev(0), rev(0), rev(0), rev(2), rev(3),
                  pl.BlockSpec((nck, HG_HEADS, HEAD, HEAD), lambda i: (nsteps - 1 - i, 0, 0, 0)), small, small,
                  rows(dqk)] + [rows(a, d) for a, d in zip(dvs, DILATIONS)] + [rows(dgab)],
        out_specs=[pl.BlockSpec((HG_ROWS, P_IN), lambda i: (nsteps - 1 - i, 0)), small, small],
        out_shape=[jax.ShapeDtypeStruct((t, P_IN), BF16), jax.ShapeDtypeStruct((1, D_MODEL), F32),
                   jax.ShapeDtypeStruct((1, D_MODEL), F32)],
        scratch_shapes=[pltpu.VMEM((HG_HEADS, HEAD, HEAD), F32)] + [pltpu.VMEM((ATT_HEADS, HG_ROWS, HEAD), F32)] * n_view,
        name=name, sem=("arbitrary",), args=(doa, oscan, proj, hf, proj, proj, sall, lb, gain, dqk, *dvs, dgab),
        rider=rider)


def _window_masks(has_previous):
    qi = lax.broadcasted_iota(jnp.int32, (ATT_BLK, 2 * ATT_BLK), 0)
    ki = lax.broadcasted_iota(jnp.int32, (ATT_BLK, 2 * ATT_BLK), 1)
    band = jnp.logical_and(ki >= qi, ki <= qi + ATT_BLK)
    return band, jnp.logical_and(band, jnp.logical_or(ki >= ATT_BLK, has_previous))


def _two_blocks(ref, prev_ref, j, hs):
    if j == 0:
        return jnp.concatenate([prev_ref[:, hs], ref[0:ATT_BLK, hs]], axis=0)
    return ref[(j - 1) * ATT_BLK:(j + 1) * ATT_BLK, hs]


def _attn_cfg(qg, g):
    d = DILATIONS[g]
    length = qg.shape[0]
    assert qg.shape[1] == d * ATT_GW
    nb = length // ATT_BLK
    return d, length, nb, min(ATT_STEP_BLOCKS, nb)


def _attn_fwd(qg, kg, vg, g, name):
    d, length, nb, rb = _attn_cfg(qg, g)
    scale = HEAD ** -0.5

    def body(q_ref, k_ref, v_ref, kp_ref, vp_ref, o_ref, l_ref):
        n = pl.program_id(1)
        band, first_band = _window_masks(n > 0)
        lanes = lax.broadcasted_iota(jnp.int32, (ATT_BLK, HEAD), 1)
        for j in range(rb):
            rows = slice(j * ATT_BLK, (j + 1) * ATT_BLK)
            lse = jnp.zeros((ATT_BLK, HEAD), F32)
            for h in range(ATT_HEADS):
                hs = slice(h * HEAD, (h + 1) * HEAD)
                k2, v2 = _two_blocks(k_ref, kp_ref, j, hs), _two_blocks(v_ref, vp_ref, j, hs)
                s = jnp.where(first_band if j == 0 else band, _dot_nt(q_ref[rows, hs], k2) * scale, NEG)
                m = jnp.max(s, axis=1, keepdims=True)
                p = jnp.exp(s - m)
                l = jnp.sum(p, axis=1, keepdims=True)
                o_ref[rows, hs] = (_dot(_bf(p), v2) / l).astype(o_ref.dtype)
                lse = jnp.where(lanes == h, m + jnp.log(l), lse)
            l_ref[rows, :] = lse

    own = pl.BlockSpec((rb * ATT_BLK, ATT_GW), lambda r, n: (n, r))
    own_head = pl.BlockSpec((rb * ATT_BLK, HEAD), lambda r, n: (n, r))
    prev = pl.BlockSpec((ATT_BLK, ATT_GW), lambda r, n: (jnp.maximum(n * rb - 1, 0), r))
    return pl.pallas_call(
        body, grid=(d, nb // rb), in_specs=[own, own, own, prev, prev], out_specs=[own, own_head],
        out_shape=[jax.ShapeDtypeStruct((length, d * ATT_GW), BF16), jax.ShapeDtypeStruct((length, d * HEAD), F32)],
        name=name, compiler_params=_params(("parallel", "arbitrary")))(qg, kg, vg, kg, vg)


def _attn_bwd(qg, kg, vg, dog, lse, delta, g, name):
    d, length, nb, rb = _attn_cfg(qg, g)
    nsteps = nb // rb
    scale = HEAD ** -0.5

    def body(q_ref, k_ref, v_ref, do_ref, l_ref, dl_ref, kp_ref, vp_ref, qn_ref, don_ref, ln_ref, dln_ref,
             dq_ref, dk_ref, dv_ref):
        n = pl.program_id(1)
        band, first_band = _window_masks(n > 0)
        qi = lax.broadcasted_iota(jnp.int32, (ATT_BLK, ATT_BLK), 0)
        ki = lax.broadcasted_iota(jnp.int32, (ATT_BLK, ATT_BLK), 1)
        next_m = jnp.logical_and(ki >= qi, n < nsteps - 1)
        last = slice((rb - 1) * ATT_BLK, rb * ATT_BLK)
        for h in range(ATT_HEADS):
            hs = slice(h * HEAD, (h + 1) * HEAD)
            dk, dv = [None] * rb, [None] * rb
            for j in range(rb):
                rows = slice(j * ATT_BLK, (j + 1) * ATT_BLK)
                q, do = q_ref[rows, hs], do_ref[rows, hs]
                k2, v2 = _two_blocks(k_ref, kp_ref, j, hs), _two_blocks(v_ref, vp_ref, j, hs)
                p = jnp.where(first_band if j == 0 else band,
                              jnp.exp(_dot_nt(q, k2) * scale - _pick(l_ref[rows, :], h)), 0.0)
                ds = _bf(p * (_dot_nt(do, v2) - _pick(dl_ref[rows, :], h)) * scale)
                dq_ref[rows, hs] = _dot(ds, k2).astype(dq_ref.dtype)
                dk2, dv2 = _dot_tn(ds, q), _dot_tn(_bf(p), do)
                if j >= 1:
                    dk[j - 1] = dk[j - 1] + dk2[:ATT_BLK]
                    dv[j - 1] = dv[j - 1] + dv2[:ATT_BLK]
                dk[j], dv[j] = dk2[ATT_BLK:], dv2[ATT_BLK:]
            q, do = qn_ref[:, hs], don_ref[:, hs]
            p = jnp.where(next_m, jnp.exp(_dot_nt(q, k_ref[last, hs]) * scale - _pick(ln_ref[...], h)), 0.0)
            ds = _bf(p * (_dot_nt(do, v_ref[last, hs]) - _pick(dln_ref[...], h)) * scale)
            dk[rb - 1] = dk[rb - 1] + _dot_tn(ds, q)
            dv[rb - 1] = dv[rb - 1] + _dot_tn(_bf(p), do)
            for j in range(rb):
                rows = slice(j * ATT_BLK, (j + 1) * ATT_BLK)
                dk_ref[rows, hs] = dk[j].astype(dk_ref.dtype)
                dv_ref[rows, hs] = dv[j].astype(dv_ref.dtype)

    own = pl.BlockSpec((rb * ATT_BLK, ATT_GW), lambda r, n: (n, r))
    prev = pl.BlockSpec((ATT_BLK, ATT_GW), lambda r, n: (jnp.maximum(n * rb - 1, 0), r))
    nxt = pl.BlockSpec((ATT_BLK, ATT_GW), lambda r, n: (jnp.minimum((n + 1) * rb, nb - 1), r))
    own_head = pl.BlockSpec((rb * ATT_BLK, HEAD), lambda r, n: (n, r))
    nxt_head = pl.BlockSpec((ATT_BLK, HEAD), lambda r, n: (jnp.minimum((n + 1) * rb, nb - 1), r))
    return pl.pallas_call(
        body, grid=(d, nsteps), in_specs=[own] * 4 + [own_head] * 2 + [prev, prev, nxt, nxt, nxt_head, nxt_head],
        out_specs=[own, own, own], out_shape=[jax.ShapeDtypeStruct((length, d * ATT_GW), BF16)] * 3,
        name=name, compiler_params=_params(("parallel", "arbitrary")))(
            qg, kg, vg, dog, lse, delta, kg, vg, qg, dog, lse, delta)


def _rope_tables(t):
    pos = jnp.arange(t, dtype=F32)
    inv = ROPE_THETA ** (-jnp.arange(0, HEAD, 2, dtype=F32) / HEAD)
    ang = pos[:, None] * inv[None, :]
    ang = jnp.concatenate([ang, ang], axis=-1)
    return jnp.cos(ang), jnp.sin(ang)


def _lower_bounds(logits):
    lb = jnp.cumsum(jax.nn.softmax(logits.astype(F32), axis=0), axis=0)
    return lb - lb[0:1]


FFN_ROWS = 256
FF_SHARD = 2 * D_FF // N_CHIPS


def _ffn_in_act(x, g, w_in, name, rider=None):
    t = x.shape[0]

    def body(x_ref, g_ref, w_ref, h_ref, ab_ref, u_ref):
        xv = x_ref[...]
        h = _bf(xv * _rms_rows(xv) * g_ref[...])
        h_ref[...] = h
        for s in range(N_CHIPS // 2):
            cols = slice(s * FF_SHARD, (s + 1) * FF_SHARD)
            a = _dot(h, w_ref[s])
            b = _dot(h, w_ref[s + N_CHIPS // 2])
            ab_ref[:, cols] = a.astype(ab_ref.dtype)
            ab_ref[:, D_FF + s * FF_SHARD:D_FF + (s + 1) * FF_SHARD] = b.astype(ab_ref.dtype)
            u_ref[:, cols] = (a * _sig(a) * b).astype(u_ref.dtype)

    row = lambda w: pl.BlockSpec((FFN_ROWS, w), lambda i: (i, 0))
    return _pcall(
        body, grid=(t // FFN_ROWS,),
        in_specs=[row(D_MODEL), pl.BlockSpec((1, D_MODEL), lambda i: (0, 0)),
                  pl.BlockSpec(w_in.shape, lambda i: (0, 0, 0))],
        out_specs=[row(D_MODEL), row(2 * D_FF), row(D_FF)],
        out_shape=[jax.ShapeDtypeStruct((t, D_MODEL), BF16), jax.ShapeDtypeStruct((t, 2 * D_FF), BF16),
                   jax.ShapeDtypeStruct((t, D_FF), BF16)],
        name=name, sem=("parallel",), args=(x, g, w_in), rider=rider)


def _ffn_bwd_du_act(dx, w_out, ab, name, rider=None):
    t = dx.shape[0]

    def body(dx_ref, w_ref, ab_ref, o_ref):
        du = 0.5 * _dot_nt(_bf(dx_ref[...]), w_ref[0])
        a = ab_ref[:, :D_FF].astype(F32)
        b = ab_ref[:, D_FF:].astype(F32)
        s = _sig(a)
        o_ref[:, :D_FF] = (du * b * (s * (1.0 + a * (1.0 - s)))).astype(o_ref.dtype)
        o_ref[:, D_FF:] = (du * a * s).astype(o_ref.dtype)

    row = lambda w: pl.BlockSpec((FFN_ROWS, w), lambda i: (i, 0))
    return _pcall(
        body, grid=(t // FFN_ROWS,),
        in_specs=[row(D_MODEL), pl.BlockSpec(w_out.shape, lambda i: (0, 0, 0)), row(2 * D_FF)],
        out_specs=row(2 * D_FF), out_shape=jax.ShapeDtypeStruct((t, 2 * D_FF), BF16),
        name=name, sem=("parallel",), args=(dx, w_out, ab), rider=rider)


MIX_ROWS = 512


def _gate_specs():
    return [pl.BlockSpec((MIX_ROWS, 512), lambda i, cb=cb: (i, cb)) for cb in (CB_GA, CB_GA + 1, CB_GB, CB_GB + 1)]


def _gate(lo_ref, hi_ref):
    return _sig(_cat([lo_ref[...], hi_ref[...]]).astype(F32))


def _whole(a):
    return pl.BlockSpec(a.shape, lambda i: (0,) * a.ndim)


def _mix_tail_fwd(oa, ob, proj, x, w_a, w_b, w_o, name):
    t = x.shape[0]

    def body(oa_ref, ob_ref, ga0, ga1, gb0, gb1, x_ref, wa_ref, wb_ref, wo_ref, y_ref, m_ref, ya_ref, yb_ref):
        ya = _dot(oa_ref[...], wa_ref[0])
        yb = _cat([_dot(ob_ref[...], wb_ref[s]) for s in range(N_CHIPS)])
        merged = _bf(_gate(ga0, ga1) * ya + _gate(gb0, gb1) * yb)
        m_ref[...] = merged
        ya_ref[...] = ya.astype(ya_ref.dtype)
        yb_ref[...] = yb.astype(yb_ref.dtype)
        y_ref[...] = x_ref[...] + _dot(merged, wo_ref[0])

    row = lambda w: pl.BlockSpec((MIX_ROWS, w), lambda i: (i, 0))
    return pl.pallas_call(
        body, grid=(t // MIX_ROWS,),
        in_specs=[row(D_MODEL), row(ATT_GW)] + _gate_specs() + [row(D_MODEL), _whole(w_a), _whole(w_b), _whole(w_o)],
        out_specs=[row(D_MODEL)] * 4,
        out_shape=[jax.ShapeDtypeStruct((t, D_MODEL), F32)] + [jax.ShapeDtypeStruct((t, D_MODEL), BF16)] * 3,
        name=name, compiler_params=_params(("parallel",)))(oa, ob, proj, proj, proj, proj, x, w_a, w_b, w_o)


def _mix_tail_bwd(dx, proj, ya, yb, w_a, w_b, w_o, name):
    t = dx.shape[0]
    shard = D_MODEL // N_CHIPS

    def body(dx_ref, ga0, ga1, gb0, gb1, ya_ref, yb_ref, wa_ref, wb_ref, wo_ref, dya_ref, dyb_ref, dg_ref, doa_ref, dob_ref):
        dm = _dot_nt(_bf(dx_ref[...]), wo_ref[0])
        sa, sb = _gate(ga0, ga1), _gate(gb0, gb1)
        dya, dyb = _bf(dm * sa), _bf(dm * sb)
        dya_ref[...] = dya
        dyb_ref[...] = dyb
        dg_ref[:, :D_MODEL] = (dm * ya_ref[...].astype(F32) * sa * (1.0 - sa)).astype(dg_ref.dtype)
        dg_ref[:, D_MODEL:] = (dm * yb_ref[...].astype(F32) * sb * (1.0 - sb)).astype(dg_ref.dtype)
        doa_ref[...] = _dot_nt(dya, wa_ref[0])
        dob = _dot_nt(dyb[:, :shard], wb_ref[0])
        for s in range(1, N_CHIPS):
            dob = dob + _dot_nt(dyb[:, s * shard:(s + 1) * shard], wb_ref[s])
        dob_ref[...] = dob

    row = lambda w: pl.BlockSpec((MIX_ROWS, w), lambda i: (i, 0))
    return pl.pallas_call(
        body, grid=(t // MIX_ROWS,),
        in_specs=[row(D_MODEL)] + _gate_specs() + [row(D_MODEL), row(D_MODEL), _whole(w_a), _whole(w_b), _whole(w_o)],
        out_specs=[row(D_MODEL), row(D_MODEL), row(2 * D_MODEL), row(D_MODEL), row(ATT_GW)],
        out_shape=[jax.ShapeDtypeStruct((t, D_MODEL), BF16), jax.ShapeDtypeStruct((t, D_MODEL), BF16),
                   jax.ShapeDtypeStruct((t, 2 * D_MODEL), BF16), jax.ShapeDtypeStruct((t, D_MODEL), F32),
                   jax.ShapeDtypeStruct((t, ATT_GW), F32)],
        name=name, compiler_params=_params(("parallel",)))(dx, proj, proj, proj, proj, ya, yb, w_a, w_b, w_o)


def _ffn_fwd(x, g, src, l, pre):
    tag = f"l{l}_{pre}"
    w_in = src.weight(l, pre + "_w_in")
    h, ab, u = _ffn_in_act(x, g, w_in, name=tag + "_in_act", rider=src.ride(tag + "_in_act"))
    w_out = src.weight(l, pre + "_w_out")
    y = _mm_nn(u, w_out, name=tag + "_out", tm=512, tn=D_MODEL, out_dtype=F32, res=x, alpha=0.5, rider=src.ride(tag + "_out"))
    return y, (x, h, ab, u, w_in, w_out)


def _ffn_bwd(dx, saved, g, src, l, pre):
    tag = f"l{l}_{pre}"
    x, h, ab, u, w_in, w_out = saved
    g_out = _mm_tn(u, dx, nb=1, name=tag + "_bwd_wout", tm=1024, tk=1408, tn=D_MODEL, alpha=0.5, rider=src.ride(tag + "_bwd_wout"))
    src.grads(l, {pre + "_w_out": g_out.reshape(N_CHIPS, D_FF // N_CHIPS, D_MODEL)})
    dab = _ffn_bwd_du_act(dx, w_out, ab, name=tag + "_bwd_du_act", rider=src.ride(tag + "_bwd_du_act"))
    g_in = _mm_tn(h, dab, nb=N_CHIPS, name=tag + "_bwd_win", tm=2048, tk=D_MODEL, tn=FF_SHARD, rider=src.ride(tag + "_bwd_win"))
    src.grads(l, {pre + "_w_in": g_in})
    return _mm_nt(dab, w_in, name=tag + "_bwd_dh", tm=1024, tp=D_MODEL, tn=FF_SHARD, out_dtype=F32, rider=src.ride(tag + "_bwd_dh"),
                  norm=(x, g, dx))


def _mix_fwd(x, small, lb, cos, sin, src, l):
    tag = f"l{l}_mix"
    w = {}
    h = _norm_fwd(x, small["mix_norm"], name=tag + "_norm")
    w["w_in"] = src.weight(l, "w_in")
    proj = _mm_nn(h, w["w_in"], name=tag + "_in", tm=1024, tn=896, out_dtype=BF16, rider=src.ride(tag + "_in"))
    hf = _mm_nn(h, w["w_in"][0:1, :, D_MODEL:2 * D_MODEL], name=tag + "_hf", tm=1024, tn=D_MODEL, out_dtype=F32)
    oscan, oa, sall = _hgrn_fwd(proj, hf, lb, small["hgrn_out_norm"], name=tag + "_hgrn", rider=src.ride(tag + "_hgrn"))
    qk = _qk_fwd(proj, cos, sin, small["attn_q_norm"], small["attn_k_norm"], name=tag + "_qk")
    outs, lses = [], []
    for g in range(ATT_GROUPS):
        o, lse = _attn_fwd(qk[g], qk[3 + g], qk[6 + g], g, name=f"{tag}_attn{g}")
        outs.append(o)
        lses.append(lse)
    ob = _merge_fwd(outs, lses, name=tag + "_merge")
    w.update({n: src.weight(l, n) for n in ("w_branch_a", "w_branch_b", "w_out")})
    y, merged, ya, yb = _mix_tail_fwd(oa, ob, proj, x, w["w_branch_a"], w["w_branch_b"], w["w_out"], name=tag + "_tail")
    return y, (x, h, proj, hf, oscan, oa, sall, qk, outs, lses, ob, ya, yb, merged, w)


def _mix_bwd(dx, saved, small, lb, cos, sin, src, l, lb_live):
    tag = f"l{l}_mix"
    x, h, proj, hf, oscan, oa, sall, qk, outs, lses, ob, ya, yb, merged, w = saved
    g_wout = _mm_tn(merged, dx, nb=1, name=tag + "_bwd_wout", tm=1024, tk=D_MODEL, tn=D_MODEL)
    dya, dyb, dgab, doa, dob = _mix_tail_bwd(dx, proj, ya, yb, w["w_branch_a"], w["w_branch_b"], w["w_out"], name=tag + "_bwd_tail")
    g_wa = _mm_tn(oa, dya, nb=1, name=tag + "_bwd_wa", tm=1024, tk=D_MODEL, tn=D_MODEL)
    g_wb = _mm_tn(ob, dyb, nb=N_CHIPS, name=tag + "_bwd_wb", tm=2048, tk=ATT_GW, tn=256)
    mb = _merge_bwd(dob, outs, lses, name=tag + "_bwd_merge")
    dqk, dvs = [None] * 6, []
    for g in range(ATT_GROUPS):
        dq, dk, dv = _attn_bwd(qk[g], qk[3 + g], qk[6 + g], mb[g], lses[g], mb[3 + g], g, name=f"{tag}_bwd_attn{g}")
        dqk[g], dqk[3 + g] = dq, dk
        dvs.append(dv)
    dqk_cols, dqn, dkn = _qk_bwd(dqk, proj, cos, sin, small["attn_q_norm"], small["attn_k_norm"], name=tag + "_bwd_qk")
    dproj, dgn, dlb = _hgrn_bwd(doa, oscan, proj, hf, sall, lb, small["hgrn_out_norm"], dqk_cols, dvs, dgab,
                                name=tag + "_bwd_hgrn", precise=lb_live, rider=src.ride(tag + "_bwd_hgrn"))
    src.grads(l, dict(w_branch_a=g_wa.reshape(N_CHIPS, D_MODEL // N_CHIPS, D_MODEL), w_branch_b=g_wb,
                      w_out=g_wout.reshape(N_CHIPS, D_MODEL // N_CHIPS, D_MODEL)))
    g_win = _mm_tn(h, dproj, nb=N_CHIPS, name=tag + "_bwd_win", tm=2048, tk=D_MODEL, tn=896, rider=src.ride(tag + "_bwd_win"))
    src.grads(l, dict(w_in=g_win))
    dx, dg = _mm_nt(dproj, w["w_in"], name=tag + "_bwd_dh", tm=1024, tp=D_MODEL, tn=2688, out_dtype=F32,
                    rider=src.ride(tag + "_bwd_dh"), norm=(x, small["mix_norm"], dx))
    return dx, dict(mix_norm=dg, hgrn_out_norm=dgn, lb=dlb, attn_q_norm=dqn, attn_k_norm=dkn)


BIG = ("ffn1_w_in", "ffn1_w_out", "w_in", "w_branch_a", "w_branch_b", "w_out", "ffn2_w_in", "ffn2_w_out")
ROW_SHARDED = ("ffn1_w_out", "w_branch_a", "w_out", "ffn2_w_out")
SMALL = ("ffn1_norm", "mix_norm", "hgrn_lb_logits", "hgrn_out_norm", "attn_q_norm", "attn_k_norm", "ffn2_norm")
WEIGHTS = ("ffn1_norm", "ffn1_w_in", "ffn1_w_out", "mix_norm", "w_in", "hgrn_lb_logits", "hgrn_out_norm", "attn_q_norm",
           "attn_k_norm", "w_branch_a", "w_branch_b", "w_out", "ffn2_norm", "ffn2_w_in", "ffn2_w_out")
SMALL_ROWS = 8


def _matmul_ready(name, a):
    return a.reshape(1, a.shape[0] * a.shape[1], a.shape[2]) if name in ROW_SHARDED else a


def _layer_small(small, l):
    s = {n: small[n][l].reshape(1, D_MODEL) for n in ("ffn1_norm", "mix_norm", "hgrn_out_norm", "ffn2_norm")}
    s.update({n: small[n][l] for n in ("attn_q_norm", "attn_k_norm")})
    return s


def _local_step(x, target, small, src):
    t = x.shape[0]
    cos, sin = _rope_tables(t)
    lbs = _lower_bounds(small["hgrn_lb_logits"])
    saved = []
    for l in range(2):
        sm = _layer_small(small, l)
        lb = lbs[l].reshape(1, D_MODEL)
        x, s1 = _ffn_fwd(x, sm["ffn1_norm"], src, l, "ffn1")
        x, s2 = _mix_fwd(x, sm, lb, cos, sin, src, l)
        x, s3 = _ffn_fwd(x, sm["ffn2_norm"], src, l, "ffn2")
        saved.append((sm, lb, s1, s2, s3))
    dx, sq = _loss_fwd_bwd(x, target, name="loss")
    small_rows = [None, None]
    for l in (1, 0):
        sm, lb, s1, s2, s3 = saved[l]
        dx, dg2 = _ffn_bwd(dx, s3, sm["ffn2_norm"], src, l, "ffn2")
        dx, g = _mix_bwd(dx, s2, sm, lb, cos, sin, src, l, lb_live=l > 0)
        dx, dg1 = _ffn_bwd(dx, s1, sm["ffn1_norm"], src, l, "ffn1")
        pad = lambda a: jnp.pad(a[:ATT_GROUPS].reshape(1, ATT_GROUPS * HEAD), ((0, 0), (0, D_MODEL - ATT_GROUPS * HEAD)))
        small_rows[l] = jnp.concatenate(
            [dg1, g["mix_norm"], g["lb"], g["hgrn_out_norm"], pad(g["attn_q_norm"]), pad(g["attn_k_norm"]), dg2,
             jnp.zeros((SMALL_ROWS - 7, D_MODEL), F32)], axis=0)
    return jnp.sum(sq), dx, jnp.concatenate(small_rows, axis=0)


def _coords():
    return lax.axis_index("x"), lax.axis_index("y"), lax.axis_index("c")


def _other_chips(x, y):
    return [(1 - x, y), (x, 1 - y), (1 - x, 1 - y)]


def _half_rows(rows, which):
    return pl.ds(which * (rows // 2), rows // 2)


def _gather_rider(shards):
    n = len(shards)

    def copies(w, full, sems):
        send, recv, fsend, frecv, osend, orecv = sems
        x, y, c = _coords()
        slot = 2 * x + y
        chips = _other_chips(x, y)

        def copy(i, j, blk, src, pair, to):
            return pltpu.make_async_remote_copy(src_ref=src, dst_ref=blk, send_sem=pair[0].at[i * 3 + j],
                                                recv_sem=pair[1].at[i * 3 + j], device_id=to, device_id_type=MESH)

        def block(i, chip_slot, core):
            return full[i].at[chip_slot, _half_rows(shards[i].shape[0], core)]

        pairs = [(i, j, chip) for i in range(n) for j, chip in enumerate(chips)]

        def first():
            return [copy(i, j, block(i, slot, c), w[i].at[_half_rows(shards[i].shape[0], c)], (send, recv), (*chip, c))
                    for i, j, chip in pairs]

        def landed(core, pair):
            return [copy(i, j, block(i, 2 * chip[0] + chip[1], core), block(i, 2 * chip[0] + chip[1], core), pair, (x, y, 1 - c))
                    for i, j, chip in pairs]

        def own():
            return [pltpu.make_async_remote_copy(src_ref=w[i], dst_ref=full[i].at[slot], send_sem=osend.at[i],
                                                 recv_sem=orecv.at[i], device_id=(x, y, 1 - c), device_id_type=MESH)
                    for i in range(n)]

        return first, landed, own

    def begin(w, full, sems):
        first, _, own = copies(w, full, sems)
        for cp in first() + own():
            cp.start()

    def end(w, full, sems):
        first, landed, own = copies(w, full, sems)
        forwards = landed(lax.axis_index("c"), sems[2:4])
        for arrival, forward in zip(landed(lax.axis_index("c"), sems[:2]), forwards):
            arrival.wait_recv()
            forward.start()
        for cp in landed(1 - lax.axis_index("c"), sems[2:4]) + own():
            cp.wait_recv()
        for cp in first() + forwards + own():
            cp.wait_send()

    out_shape = [jax.ShapeDtypeStruct((N_CHIPS,) + s.shape, s.dtype) for s in shards]
    sems = [pltpu.SemaphoreType.DMA((3 * n,))] * 4 + [pltpu.SemaphoreType.DMA((n,))] * 2
    return _Rider(shards, out_shape, sems, begin, end)


N_RECV = 7


def _scatter_rider(parts):
    n = len(parts)

    def copies(p, out, sems):
        send, recv = sems
        x, y, c = _coords()
        slot = 2 * x + y
        chips = _other_chips(x, y)

        def arrivals():
            return [pltpu.make_async_remote_copy(
                src_ref=out[i].at[k], dst_ref=out[i].at[k], send_sem=send.at[0], recv_sem=recv.at[i * N_RECV + k],
                device_id=(x, y, c), device_id_type=MESH) for i in range(n) for k in range(N_RECV)]

        sends = []
        for i in range(n):
            rows = parts[i].shape[1]
            for j, chip in enumerate(chips):
                for core in (0, 1):
                    sends.append(pltpu.make_async_remote_copy(
                        src_ref=p[i].at[2 * chip[0] + chip[1], _half_rows(rows, core)], dst_ref=out[i].at[2 * j + c],
                        send_sem=send.at[i * N_RECV + 2 * j + core], recv_sem=recv.at[i * N_RECV + 2 * j + c],
                        device_id=(*chip, core), device_id_type=MESH))
            sends.append(pltpu.make_async_remote_copy(
                src_ref=p[i].at[slot, _half_rows(rows, 1 - c)], dst_ref=out[i].at[6], send_sem=send.at[i * N_RECV + 6],
                recv_sem=recv.at[i * N_RECV + 6], device_id=(x, y, 1 - c), device_id_type=MESH))
        return sends, arrivals

    def begin(p, out, sems):
        for cp in copies(p, out, sems)[0]:
            cp.start()

    def end(p, out, sems):
        sends, arrivals = copies(p, out, sems)
        for cp in arrivals():
            cp.wait_recv()
        for cp in sends:
            cp.wait_send()

    out_shape = [jax.ShapeDtypeStruct((N_RECV, a.shape[1] // 2, a.shape[2]), a.dtype) for a in parts]
    return _Rider(parts, out_shape, [pltpu.SemaphoreType.DMA((N_RECV * n,))] * 2, begin, end)


def _run_alone(rider, name):
    _pcall(lambda: None, grid=(), in_specs=[], out_specs=[], out_shape=[], name=name, sem=(), args=(), rider=rider)
    return rider.result


def _sum_partials(own, parts, name):
    r, wd = own.shape
    tm = next(t for t in (256, 128, 64, 32, 16) if r % t == 0)

    def body(own_ref, p_ref, o_ref):
        acc = own_ref[...].astype(F32)
        for k in range(N_RECV):
            acc = acc + p_ref[k].astype(F32)
        o_ref[...] = acc

    return pl.pallas_call(
        body, grid=(r // tm,),
        in_specs=[pl.BlockSpec((tm, wd), lambda i: (i, 0)), pl.BlockSpec((N_RECV, tm, wd), lambda i: (0, i, 0))],
        out_specs=pl.BlockSpec((tm, wd), lambda i: (i, 0)), out_shape=jax.ShapeDtypeStruct((r, wd), F32),
        name=name, compiler_params=_params(("parallel",)))(own, parts)


def _exchange_halves(reduced, name):
    n = len(reduced)

    def body(*refs):
        r, out = refs[:n], refs[n:2 * n]
        send, recv = refs[2 * n:]
        x, y, c = _coords()
        sib = [pltpu.make_async_remote_copy(src_ref=r[i], dst_ref=out[i], send_sem=send.at[i], recv_sem=recv.at[i],
                                            device_id=(x, y, 1 - c), device_id_type=MESH) for i in range(n)]
        for cp in sib:
            cp.start()
        for cp in sib:
            cp.wait_recv()
        for cp in sib:
            cp.wait_send()

    out_shape = [jax.ShapeDtypeStruct(a.shape, a.dtype) for a in reduced]
    return pl.pallas_call(body, in_specs=[ANY] * n, out_specs=[ANY] * n, out_shape=out_shape,
                          scratch_shapes=[pltpu.SemaphoreType.DMA((n,))] * 2, name=name)(*reduced)


def _reduce_finish(parts, recv, tag):
    x, y, c = _coords()
    slot = 2 * x + y
    halves = []
    for i, (p, r) in enumerate(zip(parts, recv)):
        half = p.shape[1] // 2
        own = lax.dynamic_slice(p, (slot, c * half, 0), (1, half, p.shape[2]))[0]
        halves.append(_sum_partials(own, r, name=f"{tag}_sum{i}"))
    theirs = _exchange_halves(halves, name=tag + "_exchange")
    return [jnp.where(c == 0, jnp.concatenate([h, t], axis=0), jnp.concatenate([t, h], axis=0)) for h, t in zip(halves, theirs)]


GATHER_RIDES = {
    "l0_ffn1_in_act": ((0, "w_in"),),
    "l0_ffn1_out": ((0, "w_branch_a"), (0, "w_branch_b"), (0, "w_out")),
    "l0_mix_in": ((0, "ffn2_w_in"), (0, "ffn2_w_out"), (1, "ffn1_w_in"), (1, "ffn1_w_out")),
    "l0_mix_hgrn": ((1, "w_in"), (1, "w_branch_a"), (1, "w_branch_b"), (1, "w_out")),
    "l0_ffn2_in_act": ((1, "ffn2_w_in"), (1, "ffn2_w_out")),
}
ALONE_FIRST = ((0, "ffn1_w_in"), (0, "ffn1_w_out"))
SCATTER_RIDES = {
    "l1_mix_bwd_hgrn": ((1, "ffn2_w_in"), (1, "ffn2_w_out")),
    "l0_ffn2_bwd_win": ((1, "ffn1_w_in"),),
    "l0_ffn2_bwd_dh": ((1, "ffn1_w_out"), (1, "w_branch_a"), (1, "w_branch_b"), (1, "w_out")),
    "l0_mix_bwd_hgrn": ((1, "w_in"), (0, "ffn2_w_out")),
    "l0_mix_bwd_win": ((0, "ffn2_w_in"),),
    "l0_mix_bwd_dh": ((0, "w_in"),),
    "l0_ffn1_bwd_wout": ((0, "w_branch_a"), (0, "w_branch_b"), (0, "w_out")),
    "l0_ffn1_bwd_du_act": ((0, "ffn1_w_out"),),
    "l0_ffn1_bwd_dh": ((0, "ffn1_w_in"),),
}


class _Exchange:
    def __init__(self, shards):
        self.shards = shards
        self.pending = []
        self.full = {}
        self.parts = {}
        self.recv = {}

    def _gather(self, keys):
        return _gather_rider([self.shards[n][l] for l, n in keys]), "gather", list(keys)

    def _scatter(self, keys):
        return _scatter_rider([self.parts[k] for k in keys]), "scatter", list(keys)

    def _unpack(self):
        waiting = []
        for rider, kind, keys in self.pending:
            if rider.result is None:
                waiting.append((rider, kind, keys))
            elif kind == "gather":
                self.full.update(zip(keys, rider.result))
            else:
                self.recv.update(zip(keys, rider.result))
        self.pending = waiting

    def ride(self, host):
        if host in GATHER_RIDES:
            self.pending.append(self._gather(GATHER_RIDES[host]))
        elif host in SCATTER_RIDES:
            self.pending.append(self._scatter(SCATTER_RIDES[host]))
        else:
            return None
        return self.pending[-1][0]

    def weight(self, l, name):
        self._unpack()
        if (l, name) not in self.full:
            assert (l, name) in ALONE_FIRST, (l, name)
            job = self._gather(ALONE_FIRST)
            _run_alone(job[0], name="gather_first")
            self.pending.append(job)
            self._unpack()
        return _matmul_ready(name, self.full[(l, name)])

    def grads(self, l, partials):
        self.parts.update({(l, n): a for n, a in partials.items()})

    def reduce(self):
        self._unpack()
        assert not self.pending and set(self.recv) == set(self.parts)
        out = {}
        for l in range(2):
            done = _reduce_finish([self.parts[(l, n)] for n in BIG], [self.recv[(l, n)] for n in BIG], f"reduce_l{l}")
            out[l] = dict(zip(BIG, done))
        return {n: jnp.stack([out[0][n], out[1][n]], axis=0) for n in BIG}


def _all_reduce_small(rows):
    r = rows.shape[0]

    def body(x_ref, o_ref, buf, send, recv):
        x, y, c = _coords()
        me = 4 * x + 2 * y + c
        buf[me] = x_ref[...]
        copies = []
        for k in range(1, 8):
            peer = (x ^ (k >> 2), y ^ ((k >> 1) & 1), c ^ (k & 1))
            cp = pltpu.make_async_remote_copy(src_ref=x_ref, dst_ref=buf.at[me], send_sem=send.at[k - 1], recv_sem=recv.at[me],
                                              device_id=peer, device_id_type=MESH)
            cp.start()
            copies.append(cp)
        for k in range(1, 8):
            src = 4 * (x ^ (k >> 2)) + 2 * (y ^ ((k >> 1) & 1)) + (c ^ (k & 1))
            pltpu.make_async_remote_copy(src_ref=x_ref, dst_ref=buf.at[src], send_sem=send.at[0], recv_sem=recv.at[src],
                                         device_id=(x, y, c), device_id_type=MESH).wait_recv()
        for cp in copies:
            cp.wait_send()
        acc = buf[0]
        for k in range(1, 8):
            acc = acc + buf[k]
        o_ref[...] = acc

    vm = pl.BlockSpec(memory_space=pltpu.VMEM)
    return pl.pallas_call(
        body, in_specs=[vm], out_specs=vm, out_shape=jax.ShapeDtypeStruct(rows.shape, F32),
        scratch_shapes=[pltpu.VMEM((8, r, D_MODEL), F32), pltpu.SemaphoreType.DMA((7,)), pltpu.SemaphoreType.DMA((8,))],
        name="all_reduce_small")(rows)


def _adamw_math(w, g, m, v):
    m = ADAM_B1 * m + (1.0 - ADAM_B1) * g
    v = ADAM_B2 * v + (1.0 - ADAM_B2) * (g * g)
    m_hat = m / (1.0 - ADAM_B1 ** ADAM_STEP)
    v_hat = v / (1.0 - ADAM_B2 ** ADAM_STEP)
    return -ADAM_LR * (m_hat / (jnp.sqrt(v_hat) + ADAM_EPS) + ADAM_WD * w), m, v


def _adamw(w, g, m, v, name):
    shape = w.shape
    cols = shape[-1]
    flat = lambda a: a.reshape(-1, cols)
    rows = flat(w).shape[0]
    tm = 128 if rows % 128 == 0 else rows
    ins = [('t', flat(a), cols, 0) for a in (w, g, m, v)]
    res = _ew(_adamw_math, ins, [('t', cols, F32)] * 3, rows=rows, tm=tm, name=name)
    return [a.reshape(shape) for a in res]


def _small_update(sums, logits, w, m, v):
    def body(s_ref, lg_ref, w_ref, m_ref, v_ref, g_ref, d_ref, nm_ref, nv_ref):
        s = s_ref[...]
        l0, l1 = lg_ref[0:1, :], lg_ref[1:2, :]
        mx = jnp.maximum(l0, l1)
        e0, e1 = jnp.exp(l0 - mx), jnp.exp(l1 - mx)
        sm0, sm1 = e0 / (e0 + e1), e1 / (e0 + e1)
        dl1 = s_ref[SMALL_ROWS + 2:SMALL_ROWS + 3, :] * sm0 * sm1
        row = lax.broadcasted_iota(jnp.int32, s.shape, 0)
        g = jnp.where(row == 2, -dl1, jnp.where(row == SMALL_ROWS + 2, dl1, s))
        d, nm, nv = _adamw_math(w_ref[...], g, m_ref[...], v_ref[...])
        g_ref[...] = g
        d_ref[...] = d
        nm_ref[...] = nm
        nv_ref[...] = nv

    vm = pl.BlockSpec(memory_space=pltpu.VMEM)
    return pl.pallas_call(body, in_specs=[vm] * 5, out_specs=[vm] * 4,
                          out_shape=[jax.ShapeDtypeStruct(sums.shape, F32)] * 4, name="small_update")(sums, logits, w, m, v)


def _pack_small(vals):
    rows = []
    for l in range(2):
        for n in ("ffn1_norm", "mix_norm", "hgrn_lb_logits", "hgrn_out_norm", "attn_q_norm", "attn_k_norm", "ffn2_norm"):
            a = vals[n][l].reshape(1, -1)
            rows.append(jnp.pad(a, ((0, 0), (0, D_MODEL - a.shape[1]))))
        rows.append(jnp.zeros((SMALL_ROWS - 7, D_MODEL), F32))
    return jnp.concatenate(rows, axis=0)


def _unpack_small(packed):
    out = {}
    for k, n in enumerate(("ffn1_norm", "mix_norm", "hgrn_lb_logits", "hgrn_out_norm", "attn_q_norm", "attn_k_norm", "ffn2_norm")):
        a = jnp.stack([packed[k], packed[SMALL_ROWS + k]], axis=0)
        out[n] = a[:, :ATT_GROUPS * HEAD].reshape(2, ATT_GROUPS, HEAD) if n.startswith("attn") else a
    return out


def kernel(x, ffn1_norm, ffn1_w_in, ffn1_w_out, mix_norm, w_in, hgrn_lb_logits, hgrn_out_norm, attn_q_norm, attn_k_norm, w_branch_a, w_branch_b, w_out, ffn2_norm, ffn2_w_in, ffn2_w_out, loss_target, m_ffn1_norm, m_ffn1_w_in, m_ffn1_w_out, m_mix_norm, m_w_in, m_hgrn_lb_logits, m_hgrn_out_norm, m_attn_q_norm, m_attn_k_norm, m_w_branch_a, m_w_branch_b, m_w_out, m_ffn2_norm, m_ffn2_w_in, m_ffn2_w_out, v_ffn1_norm, v_ffn1_w_in, v_ffn1_w_out, v_mix_norm, v_w_in, v_hgrn_lb_logits, v_hgrn_out_norm, v_attn_q_norm, v_attn_k_norm, v_w_branch_a, v_w_branch_b, v_w_out, v_ffn2_norm, v_ffn2_w_in, v_ffn2_w_out):
    a = locals()
    w = {n: a[n] for n in WEIGHTS}
    m = {n: a["m_" + n] for n in WEIGHTS}
    v = {n: a["v_" + n] for n in WEIGHTS}

    exchange = _Exchange({n: w[n].astype(BF16) for n in BIG})
    small = {n: w[n] for n in SMALL}
    sq, grad_x, small_rows = _local_step(x[0], loss_target[0], small, exchange)
    loss = lax.psum(sq, ("x", "y", "c")) * (0.5 / D_MODEL)
    grads = exchange.reduce()

    sums = _all_reduce_small(small_rows)
    g_s, d_s, m_s, v_s = _small_update(sums, w["hgrn_lb_logits"], _pack_small(small), _pack_small({n: m[n] for n in SMALL}),
                                       _pack_small({n: v[n] for n in SMALL}))
    grads.update(_unpack_small(g_s))
    delta, new_m, new_v = _unpack_small(d_s), _unpack_small(m_s), _unpack_small(v_s)
    for n in BIG:
        delta[n], new_m[n], new_v[n] = _adamw(w[n], grads[n], m[n], v[n], name="adamw_" + n)

    return (loss, grad_x[None], *[grads[n] for n in WEIGHTS], *[delta[n] for n in WEIGHTS],
            *[new_m[n] for n in WEIGHTS], *[new_v[n] for n in WEIGHTS])
```

```python
import functools

import jax
import jax.numpy as jnp
from jax import lax
from jax.experimental import pallas as pl
from jax.experimental.pallas import tpu as pltpu

F32 = jnp.float32
BF16 = jnp.bfloat16
MESH = pl.DeviceIdType.MESH

D_MODEL = 1024
D_FF = 2816
N_CHIPS = 4
HEAD = 128
HG_HEADS = 8
HG_CHUNK = 64
ATT_GROUPS = 3
ATT_HEADS = 4
ATT_GW = ATT_HEADS * HEAD
DILATIONS = (1, 4, 16)
ATT_BLK = 128
ATT_STEP_BLOCKS = 4
P_IN = 10752
CB_AQ, CB_AK, CB_AV, CB_GA, CB_GB = 8, 11, 14, 17, 19
EPS = 1e-6
ROPE_THETA = 10000.0
ADAM_LR, ADAM_B1, ADAM_B2, ADAM_EPS, ADAM_WD, ADAM_STEP = 0.001, 0.9, 0.999, 1e-08, 0.01, 10
VMEM_LIMIT_V7X = 56 * 1024 * 1024
NEG = -1e30


def _params(sem):
    return pltpu.CompilerParams(dimension_semantics=sem, vmem_limit_bytes=VMEM_LIMIT_V7X)


def _sig(x):
    return 1.0 / (1.0 + jnp.exp(-x))


def _sig_approx(x):
    return pl.reciprocal(1.0 + jnp.exp(-x), approx=True)


def _dot(a, b):
    return jnp.dot(a, b, preferred_element_type=F32)


def _dot_nt(a, b):
    return lax.dot_general(a, b, (((1,), (1,)), ((), ())), preferred_element_type=F32)


def _dot_tn(a, b):
    return lax.dot_general(a, b, (((0,), (0,)), ((), ())), preferred_element_type=F32)


def _bf(x):
    return x.astype(BF16)


ANY = pl.BlockSpec(memory_space=pl.ANY)


class _Rider:
    def __init__(self, args, out_shape, sems, begin, end):
        self.args, self.out_shape, self.sems, self.begin, self.end = list(args), list(out_shape), list(sems), begin, end
        self.result = None


def _pcall(body, *, grid, in_specs, out_specs, out_shape, name, sem, args, scratch_shapes=(), rider=None):
    multi = isinstance(out_shape, (list, tuple))
    o_specs = list(out_specs) if multi else [out_specs]
    o_shape = list(out_shape) if multi else [out_shape]
    if rider is None:
        res = pl.pallas_call(body, grid=grid, in_specs=list(in_specs), out_specs=o_specs, out_shape=o_shape,
                             scratch_shapes=list(scratch_shapes), name=name, compiler_params=_params(sem))(*args)
        return list(res) if multi else res[0]
    counts = [len(in_specs), len(rider.args), len(o_specs), len(rider.out_shape), len(scratch_shapes)]

    def wrapped(*refs):
        groups, at = [], 0
        for c in counts:
            groups.append(refs[at:at + c])
            at += c
        h_in, r_in, h_out, r_out, h_scratch = groups
        r_sems = refs[at:]
        if grid:
            ids = [pl.program_id(a) for a in range(len(grid))]
            first = functools.reduce(jnp.logical_and, [i == 0 for i in ids])
            last = functools.reduce(jnp.logical_and, [i == g - 1 for i, g in zip(ids, grid)])
            pl.when(first)(lambda: rider.begin(r_in, r_out, r_sems))
            body(*h_in, *h_out, *h_scratch)
            pl.when(last)(lambda: rider.end(r_in, r_out, r_sems))
        else:
            rider.begin(r_in, r_out, r_sems)
            body(*h_in, *h_out, *h_scratch)
            rider.end(r_in, r_out, r_sems)

    res = pl.pallas_call(
        wrapped, grid=grid, in_specs=list(in_specs) + [ANY] * counts[1], out_specs=o_specs + [ANY] * counts[3],
        out_shape=o_shape + rider.out_shape, scratch_shapes=list(scratch_shapes) + rider.sems, name=name,
        compiler_params=_params(("arbitrary",) * len(grid)))(*args, *rider.args)
    rider.result = list(res[counts[2]:])
    return list(res[:counts[2]]) if multi else res[0]


def _mm_nn(a, b3, *, name, tm, tn, out_dtype, res=None, alpha=1.0, rider=None):
    m, k = a.shape
    nb, _, nw = b3.shape
    per = nw // tn
    assert nw % tn == 0 and m % tm == 0
    has_res = res is not None

    def body(*refs):
        if has_res:
            a_ref, b_ref, r_ref, o_ref = refs
        else:
            a_ref, b_ref, o_ref = refs
        acc = _dot(_bf(a_ref[...]), b_ref[...])
        if alpha != 1.0:
            acc = alpha * acc
        if has_res:
            acc = r_ref[...] + acc
        o_ref[...] = acc.astype(o_ref.dtype)

    in_specs = [pl.BlockSpec((tm, k), lambda i, j: (i, 0)),
                pl.BlockSpec((None, k, tn), lambda i, j: (j // per, 0, j % per))]
    args = [a, b3]
    if has_res:
        in_specs.append(pl.BlockSpec((tm, tn), lambda i, j: (i, j)))
        args.append(res)
    return _pcall(body, grid=(m // tm, nb * per), in_specs=in_specs, out_specs=pl.BlockSpec((tm, tn), lambda i, j: (i, j)),
                  out_shape=jax.ShapeDtypeStruct((m, nb * nw), out_dtype), name=name, sem=("parallel", "arbitrary"),
                  args=args, rider=rider)


def _mm_nt(d, b3, *, name, tm, tp, tn, out_dtype, alpha=1.0, rider=None, norm=None):
    m, n = d.shape
    nb, p, nw = b3.shape
    per = nw // tn
    nk = n // tn
    assert nb * nw == n and nw % tn == 0 and p % tp == 0 and m % tm == 0 and (norm is None or tp == p)

    def body(d_ref, b_ref, *refs):
        kk = pl.program_id(2)
        acc_ref = refs[-1]

        @pl.when(kk == 0)
        def _():
            acc_ref[...] = jnp.zeros_like(acc_ref)

        acc_ref[...] += _dot_nt(_bf(d_ref[...]), b_ref[...])

        if norm is None:
            @pl.when(kk == nk - 1)
            def _():
                refs[0][...] = (alpha * acc_ref[...]).astype(refs[0].dtype)
        else:
            x_ref, g_ref, dx_ref, o_ref, dg_ref = refs[:5]

            @pl.when(jnp.logical_and(pl.program_id(0) == 0, kk == 0))
            def _():
                dg_ref[...] = jnp.zeros_like(dg_ref)

            @pl.when(kk == nk - 1)
            def _():
                dh = alpha * acc_ref[...]
                xv = x_ref[...]
                r = _rms_rows(xv)
                xh = xv * r
                dxh = dh * g_ref[...]
                o_ref[...] = dx_ref[...] + r * (dxh - xh * jnp.mean(dxh * xh, axis=1, keepdims=True))
                dg_ref[...] += jnp.sum(dh * xh, axis=0, keepdims=True)

    in_specs = [pl.BlockSpec((tm, tn), lambda i, j, kk: (i, kk)),
                pl.BlockSpec((None, tp, tn), lambda i, j, kk: (kk // per, j, kk % per))]
    tile = pl.BlockSpec((tm, tp), lambda i, j, kk: (i, j))
    if norm is None:
        return _pcall(body, grid=(m // tm, p // tp, nk), in_specs=in_specs, out_specs=tile,
                      out_shape=jax.ShapeDtypeStruct((m, p), out_dtype), scratch_shapes=[pltpu.VMEM((tm, tp), F32)],
                      name=name, sem=("parallel", "parallel", "arbitrary"), args=(d, b3), rider=rider)
    x, g, dx = norm
    row = pl.BlockSpec((1, p), lambda i, j, kk: (0, 0))
    return _pcall(body, grid=(m // tm, 1, nk), in_specs=in_specs + [tile, row, tile], out_specs=[tile, row],
                  out_shape=[jax.ShapeDtypeStruct((m, p), F32), jax.ShapeDtypeStruct((1, p), F32)],
                  scratch_shapes=[pltpu.VMEM((tm, tp), F32)], name=name, sem=("arbitrary", "arbitrary", "arbitrary"),
                  args=(d, b3, x, g, dx), rider=rider)


def _mm_tn(a, d, *, nb, name, tm, tk, tn, alpha=1.0, rider=None):
    m, k = a.shape
    _, n = d.shape
    nw = n // nb
    per = nw // tn
    nm = m // tm
    assert nw % tn == 0 and k % tk == 0 and m % tm == 0

    def body(a_ref, d_ref, o_ref, acc_ref):
        mm = pl.program_id(2)

        @pl.when(mm == 0)
        def _():
            acc_ref[...] = jnp.zeros_like(acc_ref)

        acc_ref[...] += _dot_tn(_bf(a_ref[...]), _bf(d_ref[...]))

        @pl.when(mm == nm - 1)
        def _():
            o_ref[...] = (alpha * acc_ref[...]).astype(o_ref.dtype)

    return _pcall(
        body, grid=(k // tk, nb * per, nm),
        in_specs=[pl.BlockSpec((tm, tk), lambda i, j, mm: (mm, i)),
                  pl.BlockSpec((tm, tn), lambda i, j, mm: (mm, j))],
        out_specs=pl.BlockSpec((None, tk, tn), lambda i, j, mm: (j // per, i, j % per)),
        out_shape=jax.ShapeDtypeStruct((nb, k, nw), BF16),
        scratch_shapes=[pltpu.VMEM((tk, tn), F32)],
        name=name, sem=("parallel", "parallel", "arbitrary"), args=(a, d), rider=rider)


def _rows_from_view(ref, buf, w, d, tm):
    for k in range(d):
        for c in range(w // HEAD):
            lanes = slice(k * w + c * HEAD, k * w + (c + 1) * HEAD)
            buf.at[c][pl.ds(k, tm // d, stride=d), :] = ref[:, lanes].astype(F32)
    return _cat([buf[c] for c in range(w // HEAD)])


def _ew(fn, ins, outs, *, rows, tm, name):
    in_specs, args, scratch = [], [], []
    for s in ins:
        if s[0] == 't':
            _, arr, w, cb = s
            in_specs.append(pl.BlockSpec((tm, w), lambda i, cb=cb: (i, cb)))
        elif s[0] == 'v':
            _, arr, w, d = s
            in_specs.append(pl.BlockSpec((tm // d, d * w), lambda i: (i, 0)))
            scratch.append(pltpu.VMEM((w // HEAD, tm, HEAD), F32))
        else:
            arr = s[1]
            in_specs.append(pl.BlockSpec(arr.shape, lambda i, nd=arr.ndim: (0,) * nd))
        args.append(arr)
    out_specs, out_shape = [], []
    for s in outs:
        if s[0] == 't':
            _, w, dt = s
            out_specs.append(pl.BlockSpec((tm, w), lambda i: (i, 0)))
            out_shape.append(jax.ShapeDtypeStruct((rows, w), dt))
        elif s[0] == 'v':
            _, w, dt, d = s
            out_specs.append(pl.BlockSpec((tm // d, d * w), lambda i: (i, 0)))
            out_shape.append(jax.ShapeDtypeStruct((rows // d, d * w), dt))
            scratch.append(pltpu.VMEM((w // HEAD, tm, HEAD), F32))
        else:
            out_specs.append(pl.BlockSpec(s[1], lambda i: (0, 0)))
            out_shape.append(jax.ShapeDtypeStruct(s[1], F32))
    n_in, n_out = len(ins), len(outs)

    def body(*refs):
        bufs = list(refs[n_in + n_out:])
        vals = []
        for r, s in zip(refs[:n_in], ins):
            if s[0] == 'v':
                vals.append(_rows_from_view(r, bufs.pop(0), s[2], s[3], tm))
            else:
                vals.append(r[...])
        res = fn(*vals)
        if not isinstance(res, (tuple, list)):
            res = (res,)
        for r, s, v in zip(refs[n_in:n_in + n_out], outs, res):
            if s[0] == 't':
                r[...] = v.astype(r.dtype)
            elif s[0] == 'v':
                w, d, buf = s[1], s[3], bufs.pop(0)
                for c in range(w // HEAD):
                    buf[c] = v[:, c * HEAD:(c + 1) * HEAD].astype(F32)
                for k in range(d):
                    for c in range(w // HEAD):
                        lanes = slice(k * w + c * HEAD, k * w + (c + 1) * HEAD)
                        r[:, lanes] = buf.at[c][pl.ds(k, tm // d, stride=d), :].astype(r.dtype)
            else:
                @pl.when(pl.program_id(0) == 0)
                def _(r=r):
                    r[...] = jnp.zeros_like(r)

                r[...] += v

    res = pl.pallas_call(
        body, grid=(rows // tm,), in_specs=in_specs, out_specs=out_specs, out_shape=out_shape, scratch_shapes=scratch,
        name=name, compiler_params=_params(("arbitrary",)))(*args)
    return res


def _tile(arr, w, g):
    return ('t', arr, w, 0) if DILATIONS[g] == 1 else ('v', arr, w, DILATIONS[g])


def _tile_out(w, dtype, g):
    return ('t', w, dtype) if DILATIONS[g] == 1 else ('v', w, dtype, DILATIONS[g])


def _heads(x):
    return [x[:, h * HEAD:(h + 1) * HEAD] for h in range(x.shape[1] // HEAD)]


def _cat(xs):
    return jnp.concatenate(xs, axis=1)


def _head_mean(x):
    return _cat([jnp.broadcast_to(jnp.mean(h, axis=1, keepdims=True), h.shape) for h in _heads(x)])


def _rms_rows(x):
    return lax.rsqrt(jnp.mean(x * x, axis=1, keepdims=True) + EPS)


def _norm_fwd(x, g, name):
    return _ew(lambda xv, gv: xv * _rms_rows(xv) * gv,
               [('t', x, D_MODEL, 0), ('f', g)], [('t', D_MODEL, BF16)], rows=x.shape[0], tm=512, name=name)[0]


def _loss_fwd_bwd(y, target, name):
    def fn(yv, tv):
        e = yv - tv
        return e * (1.0 / D_MODEL), jnp.sum(e * e, axis=0, keepdims=True)

    return _ew(fn, [('t', y, D_MODEL, 0), ('t', target, D_MODEL, 0)], [('t', D_MODEL, F32), ('acc', (1, D_MODEL))],
               rows=y.shape[0], tm=512, name=name)


def _rot(x):
    sgn = jnp.where(lax.broadcasted_iota(jnp.int32, x.shape, 1) < HEAD // 2, -1.0, 1.0)
    return pltpu.roll(x, HEAD // 2, 1) * sgn


def _gain_rows(qn, kn):
    return [a[g:g + 1] for a in (qn, kn) for g in range(ATT_GROUPS)]


def _qk_fwd(proj, cos, sin, qn, kn, name):
    def fn(*v):
        xs, cosv, sinv, gains, vs = v[:6], v[6], v[7], v[8:14], v[14:17]
        outs = []
        for j, x in enumerate(xs):
            gain = gains[j]
            ys = []
            for xh in _heads(x.astype(F32)):
                xn = xh * _rms_rows(xh) * gain
                ys.append(xn * cosv + _rot(xn) * sinv)
            outs.append(_cat(ys))
        return outs + list(vs)

    ins = ([('t', proj, 512, CB_AQ + j) for j in range(6)] + [('t', cos, HEAD, 0), ('t', sin, HEAD, 0)]
           + [('f', a) for a in _gain_rows(qn, kn)] + [('t', proj, 512, CB_AV + g) for g in range(ATT_GROUPS)])
    return _ew(fn, ins, [_tile_out(ATT_GW, BF16, j % ATT_GROUPS) for j in range(9)], rows=proj.shape[0], tm=512, name=name)


def _qk_bwd(dqk, proj, cos, sin, qn, kn, name):
    def fn(*v):
        ds, xs, cosv, sinv, gains = v[:6], v[6:12], v[12], v[13], v[14:20]
        rows8 = lax.broadcasted_iota(jnp.int32, (8, HEAD), 0)
        outs, dgs = [], [jnp.zeros((8, HEAD), F32)] * 2
        for j in range(6):
            gain = gains[j]
            dx, dg = [], jnp.zeros((1, HEAD), F32)
            for dyh, xh in zip(_heads(ds[j]), _heads(xs[j].astype(F32))):
                r = _rms_rows(xh)
                xhat = xh * r
                dxn = dyh * cosv - _rot(dyh * sinv)
                dg = dg + jnp.sum(dxn * xhat, axis=0, keepdims=True)
                dxh = dxn * gain
                dx.append(r * (dxh - xhat * jnp.mean(dxh * xhat, axis=1, keepdims=True)))
            outs.append(_cat(dx))
            dgs[j // 3] = dgs[j // 3] + jnp.where(rows8 == j % 3, dg, 0.0)
        return _cat(outs), dgs[0], dgs[1]

    ins = ([_tile(a, ATT_GW, j % ATT_GROUPS) for j, a in enumerate(dqk)] + [('t', proj, 512, CB_AQ + j) for j in range(6)]
           + [('t', cos, HEAD, 0), ('t', sin, HEAD, 0)] + [('f', a) for a in _gain_rows(qn, kn)])
    return _ew(fn, ins, [('t', 6 * ATT_GW, BF16), ('acc', (8, HEAD)), ('acc', (8, HEAD))],
               rows=proj.shape[0], tm=256, name=name)


def _pick(x, h):
    lanes = lax.broadcasted_iota(jnp.int32, x.shape, 1)
    return jnp.sum(jnp.where(lanes == h, x, 0.0), axis=1, keepdims=True)


def _spread(x):
    return _cat([jnp.broadcast_to(_pick(x, h), (x.shape[0], HEAD)) for h in range(ATT_HEADS)])


def _compact(x):
    lanes = lax.broadcasted_iota(jnp.int32, (x.shape[0], HEAD), 1)
    out = jnp.zeros((x.shape[0], HEAD), F32)
    for h, xh in enumerate(_heads(x)):
        out = jnp.where(lanes == h, xh, out)
    return out


def _group_weights(l0, l1, l2):
    l0, l1, l2 = _spread(l0), _spread(l1), _spread(l2)
    m = jnp.maximum(jnp.maximum(l0, l1), l2)
    e0, e1, e2 = jnp.exp(l0 - m), jnp.exp(l1 - m), jnp.exp(l2 - m)
    inv = 1.0 / (e0 + e1 + e2)
    return e0 * inv, e1 * inv, e2 * inv


def _merge_fwd(outs, lses, name):
    def fn(o0, o1, o2, l0, l1, l2):
        a0, a1, a2 = _group_weights(l0, l1, l2)
        return a0 * o0 + a1 * o1 + a2 * o2

    ins = [_tile(a, ATT_GW, g) for g, a in enumerate(outs)] + [_tile(a, HEAD, g) for g, a in enumerate(lses)]
    return _ew(fn, ins, [('t', ATT_GW, BF16)], rows=outs[0].shape[0], tm=512, name=name)[0]


def _merge_bwd(dob, outs, lses, name):
    def fn(dov, o0, o1, o2, l0, l1, l2):
        a0, a1, a2 = _group_weights(l0, l1, l2)
        ob = a0 * o0 + a1 * o1 + a2 * o2
        s = _head_mean(dov * ob) * float(HEAD)
        return a0 * dov, a1 * dov, a2 * dov, _compact(a0 * s), _compact(a1 * s), _compact(a2 * s)

    ins = ([('t', dob, ATT_GW, 0)] + [_tile(a, ATT_GW, g) for g, a in enumerate(outs)]
           + [_tile(a, HEAD, g) for g, a in enumerate(lses)])
    groups = range(ATT_GROUPS)
    return _ew(fn, ins, [_tile_out(ATT_GW, BF16, g) for g in groups] + [_tile_out(HEAD, F32, g) for g in groups],
               rows=dob.shape[0], tm=512, name=name)


HG_ROWS = 256


def _hg_gates(hq, hf, hi, lbv):
    sig = _sig(hf)
    f = lbv + (1.0 - lbv) * sig
    return hq * _sig(hq), 1.0 - f, hi, jnp.log(f), sig, f


def _split3(x):
    hi = _bf(x)
    r1 = x - hi.astype(F32)
    mid = _bf(r1)
    return hi, mid, _bf(r1 - mid.astype(F32))


def _tri_dot(tri, x):
    hi, mid, lo = _split3(x)
    return _dot(tri, hi) + _dot(tri, mid) + _dot(tri, lo)


def _row(x, i):
    rows = lax.broadcasted_iota(jnp.int32, x.shape, 0)
    return jnp.sum(jnp.where(rows == i, x, 0.0), axis=0, keepdims=True)


def _hg_decay(logf, q, k):
    c = HG_CHUNK
    row = lax.broadcasted_iota(jnp.int32, (c, c), 0)
    col = lax.broadcasted_iota(jnp.int32, (c, c), 1)
    g = _tri_dot((row >= col).astype(BF16), logf)
    gm = _row(g, c // 2 - 1)
    gl = _row(g, c - 1)
    decays = jnp.exp(g), jnp.exp(g - gm), jnp.exp(gm - g), jnp.exp(gl - g)
    return gl, decays, q * decays[0], q * decays[1], k * decays[2], k * decays[3]


def _hg_out_fwd(o, hg, gain):
    r = lax.rsqrt(_head_mean(o * o) + EPS)
    return o * r * gain * (hg * _sig(hg))


def _hgrn_fwd(proj, hf, lb, gain, name, rider=None):
    t = proj.shape[0]
    nck = HG_ROWS // HG_CHUNK

    def body(hq_ref, hf_ref, hi_ref, hg_ref, lb_ref, gn_ref, o_ref, oa_ref, sall_ref, st_ref):
        @pl.when(pl.program_id(0) == 0)
        def _():
            st_ref[...] = jnp.zeros_like(st_ref)

        lbv = lb_ref[...]
        gnv = gn_ref[...]
        c = HG_CHUNK
        mask = lax.broadcasted_iota(jnp.int32, (c, c), 0) >= lax.broadcasted_iota(jnp.int32, (c, c), 1)

        def chunk(cc, carry):
            sl = pl.ds(pl.multiple_of(cc * c, c), c)
            q, k, v, logf, _, _ = _hg_gates(hq_ref[sl, :].astype(F32), hf_ref[sl, :], hi_ref[sl, :].astype(F32), lbv)
            gl, _, qg, qt, kt, kd = _hg_decay(logf, q, k)
            egl = jnp.exp(gl)
            os = []
            for h in range(HG_HEADS):
                hs = slice(h * HEAD, (h + 1) * HEAD)
                st = st_ref[h]
                sall_ref[cc, h] = st
                a = jnp.where(mask, _dot_nt(_bf(qt[:, hs]), _bf(kt[:, hs])), 0.0)
                os.append(_dot(_bf(a), _bf(v[:, hs])) + _dot_nt(_bf(qg[:, hs]), _bf(st)))
                st_ref[h] = egl[:, hs] * st + _dot_tn(_bf(v[:, hs]), _bf(kd[:, hs]))
            o = _cat(os)
            o_ref[sl, :] = o
            oa_ref[sl, :] = _hg_out_fwd(o, hg_ref[sl, :].astype(F32), gnv).astype(oa_ref.dtype)
            return carry

        lax.fori_loop(0, nck, chunk, 0)

    col = lambda j: pl.BlockSpec((HG_ROWS, D_MODEL), lambda i, j=j: (i, j))
    small = pl.BlockSpec((1, D_MODEL), lambda i: (0, 0))
    return _pcall(
        body, grid=(t // HG_ROWS,),
        in_specs=[col(0), col(0), col(2), col(3), small, small],
        out_specs=[col(0), col(0), pl.BlockSpec((nck, HG_HEADS, HEAD, HEAD), lambda i: (i, 0, 0, 0))],
        out_shape=[jax.ShapeDtypeStruct((t, D_MODEL), F32), jax.ShapeDtypeStruct((t, D_MODEL), BF16),
                   jax.ShapeDtypeStruct((t // HG_CHUNK, HG_HEADS, HEAD, HEAD), F32)],
        scratch_shapes=[pltpu.VMEM((HG_HEADS, HEAD, HEAD), F32)],
        name=name, sem=("arbitrary",), args=(proj, hf, proj, proj, lb, gain), rider=rider)


def _terms(x, precise):
    hi = _bf(x)
    return (hi, _bf(x - hi.astype(F32))) if precise else (hi,)


def _mm(dot, a, b):
    out = dot(a[0], b[0])
    if len(a) > 1:
        out = out + dot(a[1], b[0])
    if len(b) > 1:
        out = out + dot(a[0], b[1])
    return out


def _hgrn_bwd(doa, oscan, proj, hf, sall, lb, gain, dqk, dvs, dgab, name, precise, rider=None):
    t = proj.shape[0]
    nck = HG_ROWS // HG_CHUNK
    nsteps = t // HG_ROWS
    terms = functools.partial(_terms, precise=precise)
    n_view = sum(d > 1 for d in DILATIONS)

    def body(doa_ref, os_ref, hq_ref, hf_ref, hi_ref, hg_ref, sall_ref, lb_ref, gn_ref, dqk_ref, dv0_ref, dv1_ref,
             dv2_ref, dgab_ref, dproj_ref, dgn_ref, dlb_ref, dst_ref, *bufs):
        @pl.when(pl.program_id(0) == 0)
        def _():
            dst_ref[...] = jnp.zeros_like(dst_ref)
            dgn_ref[...] = jnp.zeros_like(dgn_ref)
            dlb_ref[...] = jnp.zeros_like(dlb_ref)

        at = 4 * D_MODEL
        dproj_ref[:, at:at + 6 * ATT_GW] = dqk_ref[...]
        at += 6 * ATT_GW
        spare = list(bufs)
        for d, dv_ref in zip(DILATIONS, (dv0_ref, dv1_ref, dv2_ref)):
            dv = dv_ref[...] if d == 1 else _rows_from_view(dv_ref, spare.pop(0), ATT_GW, d, HG_ROWS)
            dproj_ref[:, at:at + ATT_GW] = dv.astype(dproj_ref.dtype)
            at += ATT_GW
        dproj_ref[:, at:] = dgab_ref[...]

        lbv = lb_ref[...]
        gnv = gn_ref[...]
        c = HG_CHUNK
        row = lax.broadcasted_iota(jnp.int32, (c, c), 0)
        colm = lax.broadcasted_iota(jnp.int32, (c, c), 1)
        mask = row >= colm
        triu = (row <= colm).astype(BF16)
        last = lax.broadcasted_iota(jnp.int32, (c, HEAD), 0) == c - 1

        def chunk(ci, carry):
            cc = nck - 1 - ci
            sl = pl.ds(pl.multiple_of(cc * c, c), c)
            hq, hg = hq_ref[sl, :].astype(F32), hg_ref[sl, :].astype(F32)
            q, k, v, logf, sig, f = _hg_gates(hq, hf_ref[sl, :], hi_ref[sl, :].astype(F32), lbv)
            gl, (e_qg, e_qt, e_kt, e_kd), qg, qt, kt, kd = _hg_decay(logf, q, k)
            egl = jnp.exp(gl)
            o = os_ref[sl, :]
            dy = doa_ref[sl, :]
            r = lax.rsqrt(_head_mean(o * o) + EPS)
            oh = o * r
            sg = _sig(hg)
            silu_g = hg * sg
            dgn_ref[...] += jnp.sum(dy * oh * silu_g, axis=0, keepdims=True)
            dhg = dy * oh * gnv * (sg * (1.0 + hg * (1.0 - sg)))
            doh = dy * gnv * silu_g
            do = r * (doh - oh * _head_mean(doh * oh))
            dqs, dks, dvs, dgs = [], [], [], []
            for h in range(HG_HEADS):
                hs = slice(h * HEAD, (h + 1) * HEAD)
                st = sall_ref[cc, h]
                dst = dst_ref[h]
                qt_h, kt_h, qg_h, kd_h = qt[:, hs], kt[:, hs], qg[:, hs], kd[:, hs]
                do_p, v_p, qt_p, kt_p, qg_p = terms(do[:, hs]), terms(v[:, hs]), terms(qt_h), terms(kt_h), terms(qg_h)
                st_p, dst_p = terms(st), terms(dst)
                a = jnp.where(mask, _dot_nt(qt_p[0], kt_p[0]), 0.0)
                da = terms(jnp.where(mask, _mm(_dot_nt, do_p, v_p), 0.0))
                dqt = _mm(_dot, da, kt_p)
                dkt = _mm(_dot_tn, da, qt_p)
                dqg = _mm(_dot, do_p, st_p)
                dv = _dot_tn(_bf(a), do_p[0]) + _dot_nt(_bf(kd_h), dst_p[0])
                dkd = _mm(_dot, v_p, dst_p)
                dgl = egl[:, hs] * jnp.sum(st * dst, axis=0, keepdims=True) + jnp.sum(dkd * kd_h, axis=0, keepdims=True)
                dst_ref[h] = egl[:, hs] * dst + _mm(_dot_tn, do_p, qg_p)
                dqs.append(dqt * e_qt[:, hs] + dqg * e_qg[:, hs])
                dks.append(dkt * e_kt[:, hs] + dkd * e_kd[:, hs])
                dvs.append(dv)
                dgs.append(dqt * qt_h - dkt * kt_h + dqg * qg_h - dkd * kd_h + jnp.where(last, dgl, 0.0))
            dq, dk, dv, dg = _cat(dqs), _cat(dks), _cat(dvs), _cat(dgs)
            dlogf = _tri_dot(triu, dg)
            df = dlogf / f - dk
            dlb_ref[...] += jnp.sum(df * (1.0 - sig), axis=0, keepdims=True)
            dhf = df * (1.0 - lbv) * sig * (1.0 - sig)
            sq = _sig(hq)
            dhq = dq * (sq * (1.0 + hq * (1.0 - sq)))
            dproj_ref[sl, :4 * D_MODEL] = _cat([dhq, dhf, dv, dhg]).astype(dproj_ref.dtype)
            return carry

        lax.fori_loop(0, nck, chunk, 0)

    rev = lambda j: pl.BlockSpec((HG_ROWS, D_MODEL), lambda i, j=j: (nsteps - 1 - i, j))
    rows = lambda a, d=1: pl.BlockSpec((HG_ROWS // d, a.shape[1]), lambda i: (nsteps - 1 - i, 0))
    small = pl.BlockSpec((1, D_MODEL), lambda i: (0, 0))
    return _pcall(
        body, grid=(nsteps,),
        in_specs=[rev(0), rev(0), rev(0), rev(0), rev(2), rev(3),
                  pl.BlockSpec((nck, HG_HEADS, HEAD, HEAD), lambda i: (nsteps - 1 - i, 0, 0, 0)), small, small,
                  rows(dqk)] + [rows(a, d) for a, d in zip(dvs, DILATIONS)] + [rows(dgab)],
        out_specs=[pl.BlockSpec((HG_ROWS, P_IN), lambda i: (nsteps - 1 - i, 0)), small, small],
        out_shape=[jax.ShapeDtypeStruct((t, P_IN), BF16), jax.ShapeDtypeStruct((1, D_MODEL), F32),
                   jax.ShapeDtypeStruct((1, D_MODEL), F32)],
        scratch_shapes=[pltpu.VMEM((HG_HEADS, HEAD, HEAD), F32)] + [pltpu.VMEM((ATT_HEADS, HG_ROWS, HEAD), F32)] * n_view,
        name=name, sem=("arbitrary",), args=(doa, oscan, proj, hf, proj, proj, sall, lb, gain, dqk, *dvs, dgab),
        rider=rider)


def _window_masks(has_previous):
    qi = lax.broadcasted_iota(jnp.int32, (ATT_BLK, 2 * ATT_BLK), 0)
    ki = lax.broadcasted_iota(jnp.int32, (ATT_BLK, 2 * ATT_BLK), 1)
    band = jnp.logical_and(ki >= qi, ki <= qi + ATT_BLK)
    return band, jnp.logical_and(band, jnp.logical_or(ki >= ATT_BLK, has_previous))


def _two_blocks(ref, prev_ref, j, hs):
    if j == 0:
        return jnp.concatenate([prev_ref[:, hs], ref[0:ATT_BLK, hs]], axis=0)
    return ref[(j - 1) * ATT_BLK:(j + 1) * ATT_BLK, hs]


def _attn_cfg(qg, g):
    d = DILATIONS[g]
    length = qg.shape[0]
    assert qg.shape[1] == d * ATT_GW
    nb = length // ATT_BLK
    return d, length, nb, min(ATT_STEP_BLOCKS, nb)


def _attn_fwd(qg, kg, vg, g, name):
    d, length, nb, rb = _attn_cfg(qg, g)
    scale = HEAD ** -0.5

    def body(q_ref, k_ref, v_ref, kp_ref, vp_ref, o_ref, l_ref):
        n = pl.program_id(1)
        band, first_band = _window_masks(n > 0)
        lanes = lax.broadcasted_iota(jnp.int32, (ATT_BLK, HEAD), 1)
        for j in range(rb):
            rows = slice(j * ATT_BLK, (j + 1) * ATT_BLK)
            lse = jnp.zeros((ATT_BLK, HEAD), F32)
            for h in range(ATT_HEADS):
                hs = slice(h * HEAD, (h + 1) * HEAD)
                k2, v2 = _two_blocks(k_ref, kp_ref, j, hs), _two_blocks(v_ref, vp_ref, j, hs)
                s = jnp.where(first_band if j == 0 else band, _dot_nt(q_ref[rows, hs], k2) * scale, NEG)
                m = jnp.max(s, axis=1, keepdims=True)
                p = jnp.exp(s - m)
                l = jnp.sum(p, axis=1, keepdims=True)
                o_ref[rows, hs] = (_dot(_bf(p), v2) / l).astype(o_ref.dtype)
                lse = jnp.where(lanes == h, m + jnp.log(l), lse)
            l_ref[rows, :] = lse

    own = pl.BlockSpec((rb * ATT_BLK, ATT_GW), lambda r, n: (n, r))
    own_head = pl.BlockSpec((rb * ATT_BLK, HEAD), lambda r, n: (n, r))
    prev = pl.BlockSpec((ATT_BLK, ATT_GW), lambda r, n: (jnp.maximum(n * rb - 1, 0), r))
    return pl.pallas_call(
        body, grid=(d, nb // rb), in_specs=[own, own, own, prev, prev], out_specs=[own, own_head],
        out_shape=[jax.ShapeDtypeStruct((length, d * ATT_GW), BF16), jax.ShapeDtypeStruct((length, d * HEAD), F32)],
        name=name, compiler_params=_params(("parallel", "arbitrary")))(qg, kg, vg, kg, vg)


def _attn_bwd(qg, kg, vg, dog, lse, delta, g, name):
    d, length, nb, rb = _attn_cfg(qg, g)
    nsteps = nb // rb
    scale = HEAD ** -0.5

    def body(q_ref, k_ref, v_ref, do_ref, l_ref, dl_ref, kp_ref, vp_ref, qn_ref, don_ref, ln_ref, dln_ref,
             dq_ref, dk_ref, dv_ref):
        n = pl.program_id(1)
        band, first_band = _window_masks(n > 0)
        qi = lax.broadcasted_iota(jnp.int32, (ATT_BLK, ATT_BLK), 0)
        ki = lax.broadcasted_iota(jnp.int32, (ATT_BLK, ATT_BLK), 1)
        next_m = jnp.logical_and(ki >= qi, n < nsteps - 1)
        last = slice((rb - 1) * ATT_BLK, rb * ATT_BLK)
        for h in range(ATT_HEADS):
            hs = slice(h * HEAD, (h + 1) * HEAD)
            dk, dv = [None] * rb, [None] * rb
            for j in range(rb):
                rows = slice(j * ATT_BLK, (j + 1) * ATT_BLK)
                q, do = q_ref[rows, hs], do_ref[rows, hs]
                k2, v2 = _two_blocks(k_ref, kp_ref, j, hs), _two_blocks(v_ref, vp_ref, j, hs)
                p = jnp.where(first_band if j == 0 else band,
                              jnp.exp(_dot_nt(q, k2) * scale - _pick(l_ref[rows, :], h)), 0.0)
                ds = _bf(p * (_dot_nt(do, v2) - _pick(dl_ref[rows, :], h)) * scale)
                dq_ref[rows, hs] = _dot(ds, k2).astype(dq_ref.dtype)
                dk2, dv2 = _dot_tn(ds, q), _dot_tn(_bf(p), do)
                if j >= 1:
                    dk[j - 1] = dk[j - 1] + dk2[:ATT_BLK]
                    dv[j - 1] = dv[j - 1] + dv2[:ATT_BLK]
                dk[j], dv[j] = dk2[ATT_BLK:], dv2[ATT_BLK:]
            q, do = qn_ref[:, hs], don_ref[:, hs]
            p = jnp.where(next_m, jnp.exp(_dot_nt(q, k_ref[last, hs]) * scale - _pick(ln_ref[...], h)), 0.0)
            ds = _bf(p * (_dot_nt(do, v_ref[last, hs]) - _pick(dln_ref[...], h)) * scale)
            dk[rb - 1] = dk[rb - 1] + _dot_tn(ds, q)
            dv[rb - 1] = dv[rb - 1] + _dot_tn(_bf(p), do)
            for j in range(rb):
                rows = slice(j * ATT_BLK, (j + 1) * ATT_BLK)
                dk_ref[rows, hs] = dk[j].astype(dk_ref.dtype)
                dv_ref[rows, hs] = dv[j].astype(dv_ref.dtype)

    own = pl.BlockSpec((rb * ATT_BLK, ATT_GW), lambda r, n: (n, r))
    prev = pl.BlockSpec((ATT_BLK, ATT_GW), lambda r, n: (jnp.maximum(n * rb - 1, 0), r))
    nxt = pl.BlockSpec((ATT_BLK, ATT_GW), lambda r, n: (jnp.minimum((n + 1) * rb, nb - 1), r))
    own_head = pl.BlockSpec((rb * ATT_BLK, HEAD), lambda r, n: (n, r))
    nxt_head = pl.BlockSpec((ATT_BLK, HEAD), lambda r, n: (jnp.minimum((n + 1) * rb, nb - 1), r))
    return pl.pallas_call(
        body, grid=(d, nsteps), in_specs=[own] * 4 + [own_head] * 2 + [prev, prev, nxt, nxt, nxt_head, nxt_head],
        out_specs=[own, own, own], out_shape=[jax.ShapeDtypeStruct((length, d * ATT_GW), BF16)] * 3,
        name=name, compiler_params=_params(("parallel", "arbitrary")))(
            qg, kg, vg, dog, lse, delta, kg, vg, qg, dog, lse, delta)


def _rope_tables(t):
    pos = jnp.arange(t, dtype=F32)
    inv = ROPE_THETA ** (-jnp.arange(0, HEAD, 2, dtype=F32) / HEAD)
    ang = pos[:, None] * inv[None, :]
    ang = jnp.concatenate([ang, ang], axis=-1)
    return jnp.cos(ang), jnp.sin(ang)


def _lower_bounds(logits):
    lb = jnp.cumsum(jax.nn.softmax(logits.astype(F32), axis=0), axis=0)
    return lb - lb[0:1]


FFN_ROWS = 256
FF_SHARD = 2 * D_FF // N_CHIPS


def _ffn_in_act(x, g, w_in, name, rider=None):
    t = x.shape[0]

    def body(x_ref, g_ref, w_ref, h_ref, ab_ref, u_ref):
        xv = x_ref[...]
        h = _bf(xv * _rms_rows(xv) * g_ref[...])
        h_ref[...] = h
        for s in range(N_CHIPS // 2):
            cols = slice(s * FF_SHARD, (s + 1) * FF_SHARD)
            a = _dot(h, w_ref[s])
            b = _dot(h, w_ref[s + N_CHIPS // 2])
            ab_ref[:, cols] = a.astype(ab_ref.dtype)
            ab_ref[:, D_FF + s * FF_SHARD:D_FF + (s + 1) * FF_SHARD] = b.astype(ab_ref.dtype)
            u_ref[:, cols] = (a * _sig_approx(a) * b).astype(u_ref.dtype)

    row = lambda w: pl.BlockSpec((FFN_ROWS, w), lambda i: (i, 0))
    return _pcall(
        body, grid=(t // FFN_ROWS,),
        in_specs=[row(D_MODEL), pl.BlockSpec((1, D_MODEL), lambda i: (0, 0)),
                  pl.BlockSpec(w_in.shape, lambda i: (0, 0, 0))],
        out_specs=[row(D_MODEL), row(2 * D_FF), row(D_FF)],
        out_shape=[jax.ShapeDtypeStruct((t, D_MODEL), BF16), jax.ShapeDtypeStruct((t, 2 * D_FF), BF16),
                   jax.ShapeDtypeStruct((t, D_FF), BF16)],
        name=name, sem=("parallel",), args=(x, g, w_in), rider=rider)


def _ffn_bwd_du_act(dx, w_out, ab, name, rider=None):
    t = dx.shape[0]

    def body(dx_ref, w_ref, ab_ref, o_ref):
        du = 0.5 * _dot_nt(_bf(dx_ref[...]), w_ref[0])
        a = ab_ref[:, :D_FF].astype(F32)
        b = ab_ref[:, D_FF:].astype(F32)
        s = _sig_approx(a)
        o_ref[:, :D_FF] = (du * b * (s * (1.0 + a * (1.0 - s)))).astype(o_ref.dtype)
        o_ref[:, D_FF:] = (du * a * s).astype(o_ref.dtype)

    row = lambda w: pl.BlockSpec((FFN_ROWS, w), lambda i: (i, 0))
    return _pcall(
        body, grid=(t // FFN_ROWS,),
        in_specs=[row(D_MODEL), pl.BlockSpec(w_out.shape, lambda i: (0, 0, 0)), row(2 * D_FF)],
        out_specs=row(2 * D_FF), out_shape=jax.ShapeDtypeStruct((t, 2 * D_FF), BF16),
        name=name, sem=("parallel",), args=(dx, w_out, ab), rider=rider)


MIX_ROWS = 512


def _gate_specs():
    return [pl.BlockSpec((MIX_ROWS, 512), lambda i, cb=cb: (i, cb)) for cb in (CB_GA, CB_GA + 1, CB_GB, CB_GB + 1)]


def _gate(lo_ref, hi_ref):
    return _sig_approx(_cat([lo_ref[...], hi_ref[...]]).astype(F32))


def _whole(a):
    return pl.BlockSpec(a.shape, lambda i: (0,) * a.ndim)


def _mix_tail_fwd(oa, ob, proj, x, w_a, w_b, w_o, name):
    t = x.shape[0]

    def body(oa_ref, ob_ref, ga0, ga1, gb0, gb1, x_ref, wa_ref, wb_ref, wo_ref, y_ref, m_ref, ya_ref, yb_ref):
        ya = _dot(oa_ref[...], wa_ref[0])
        yb = _cat([_dot(ob_ref[...], wb_ref[s]) for s in range(N_CHIPS)])
        merged = _bf(_gate(ga0, ga1) * ya + _gate(gb0, gb1) * yb)
        m_ref[...] = merged
        ya_ref[...] = ya.astype(ya_ref.dtype)
        yb_ref[...] = yb.astype(yb_ref.dtype)
        y_ref[...] = x_ref[...] + _dot(merged, wo_ref[0])

    row = lambda w: pl.BlockSpec((MIX_ROWS, w), lambda i: (i, 0))
    return pl.pallas_call(
        body, grid=(t // MIX_ROWS,),
        in_specs=[row(D_MODEL), row(ATT_GW)] + _gate_specs() + [row(D_MODEL), _whole(w_a), _whole(w_b), _whole(w_o)],
        out_specs=[row(D_MODEL)] * 4,
        out_shape=[jax.ShapeDtypeStruct((t, D_MODEL), F32)] + [jax.ShapeDtypeStruct((t, D_MODEL), BF16)] * 3,
        name=name, compiler_params=_params(("parallel",)))(oa, ob, proj, proj, proj, proj, x, w_a, w_b, w_o)


def _mix_tail_bwd(dx, proj, ya, yb, w_a, w_b, w_o, name):
    t = dx.shape[0]
    shard = D_MODEL // N_CHIPS

    def body(dx_ref, ga0, ga1, gb0, gb1, ya_ref, yb_ref, wa_ref, wb_ref, wo_ref, dya_ref, dyb_ref, dg_ref, doa_ref, dob_ref):
        dm = _dot_nt(_bf(dx_ref[...]), wo_ref[0])
        sa, sb = _gate(ga0, ga1), _gate(gb0, gb1)
        dya, dyb = _bf(dm * sa), _bf(dm * sb)
        dya_ref[...] = dya
        dyb_ref[...] = dyb
        dg_ref[:, :D_MODEL] = (dm * ya_ref[...].astype(F32) * sa * (1.0 - sa)).astype(dg_ref.dtype)
        dg_ref[:, D_MODEL:] = (dm * yb_ref[...].astype(F32) * sb * (1.0 - sb)).astype(dg_ref.dtype)
        doa_ref[...] = _dot_nt(dya, wa_ref[0])
        dob = _dot_nt(dyb[:, :shard], wb_ref[0])
        for s in range(1, N_CHIPS):
            dob = dob + _dot_nt(dyb[:, s * shard:(s + 1) * shard], wb_ref[s])
        dob_ref[...] = dob

    row = lambda w: pl.BlockSpec((MIX_ROWS, w), lambda i: (i, 0))
    return pl.pallas_call(
        body, grid=(t // MIX_ROWS,),
        in_specs=[row(D_MODEL)] + _gate_specs() + [row(D_MODEL), row(D_MODEL), _whole(w_a), _whole(w_b), _whole(w_o)],
        out_specs=[row(D_MODEL), row(D_MODEL), row(2 * D_MODEL), row(D_MODEL), row(ATT_GW)],
        out_shape=[jax.ShapeDtypeStruct((t, D_MODEL), BF16), jax.ShapeDtypeStruct((t, D_MODEL), BF16),
                   jax.ShapeDtypeStruct((t, 2 * D_MODEL), BF16), jax.ShapeDtypeStruct((t, D_MODEL), F32),
                   jax.ShapeDtypeStruct((t, ATT_GW), F32)],
        name=name, compiler_params=_params(("parallel",)))(dx, proj, proj, proj, proj, ya, yb, w_a, w_b, w_o)


def _ffn_fwd(x, g, src, l, pre):
    tag = f"l{l}_{pre}"
    w_in = src.weight(l, pre + "_w_in")
    h, ab, u = _ffn_in_act(x, g, w_in, name=tag + "_in_act", rider=src.ride(tag + "_in_act"))
    w_out = src.weight(l, pre + "_w_out")
    y = _mm_nn(u, w_out, name=tag + "_out", tm=512, tn=D_MODEL, out_dtype=F32, res=x, alpha=0.5, rider=src.ride(tag + "_out"))
    return y, (x, h, ab, u, w_in, w_out)


def _ffn_bwd(dx, saved, g, src, l, pre):
    tag = f"l{l}_{pre}"
    x, h, ab, u, w_in, w_out = saved
    g_out = _mm_tn(u, dx, nb=1, name=tag + "_bwd_wout", tm=1024, tk=1408, tn=D_MODEL, alpha=0.5, rider=src.ride(tag + "_bwd_wout"))
    src.grads(l, {pre + "_w_out": g_out.reshape(N_CHIPS, D_FF // N_CHIPS, D_MODEL)})
    dab = _ffn_bwd_du_act(dx, w_out, ab, name=tag + "_bwd_du_act", rider=src.ride(tag + "_bwd_du_act"))
    g_in = _mm_tn(h, dab, nb=N_CHIPS, name=tag + "_bwd_win", tm=2048, tk=D_MODEL, tn=FF_SHARD, rider=src.ride(tag + "_bwd_win"))
    src.grads(l, {pre + "_w_in": g_in})
    return _mm_nt(dab, w_in, name=tag + "_bwd_dh", tm=1024, tp=D_MODEL, tn=FF_SHARD, out_dtype=F32, rider=src.ride(tag + "_bwd_dh"),
                  norm=(x, g, dx))


def _mix_fwd(x, small, lb, cos, sin, src, l):
    tag = f"l{l}_mix"
    w = {}
    h = _norm_fwd(x, small["mix_norm"], name=tag + "_norm")
    w["w_in"] = src.weight(l, "w_in")
    proj = _mm_nn(h, w["w_in"], name=tag + "_in", tm=1024, tn=896, out_dtype=BF16, rider=src.ride(tag + "_in"))
    hf = _mm_nn(h, w["w_in"][0:1, :, D_MODEL:2 * D_MODEL], name=tag + "_hf", tm=1024, tn=D_MODEL, out_dtype=F32)
    oscan, oa, sall = _hgrn_fwd(proj, hf, lb, small["hgrn_out_norm"], name=tag + "_hgrn", rider=src.ride(tag + "_hgrn"))
    qk = _qk_fwd(proj, cos, sin, small["attn_q_norm"], small["attn_k_norm"], name=tag + "_qk")
    outs, lses = [], []
    for g in range(ATT_GROUPS):
        o, lse = _attn_fwd(qk[g], qk[3 + g], qk[6 + g], g, name=f"{tag}_attn{g}")
        outs.append(o)
        lses.append(lse)
    ob = _merge_fwd(outs, lses, name=tag + "_merge")
    w.update({n: src.weight(l, n) for n in ("w_branch_a", "w_branch_b", "w_out")})
    y, merged, ya, yb = _mix_tail_fwd(oa, ob, proj, x, w["w_branch_a"], w["w_branch_b"], w["w_out"], name=tag + "_tail")
    return y, (x, h, proj, hf, oscan, oa, sall, qk, outs, lses, ob, ya, yb, merged, w)


def _mix_bwd(dx, saved, small, lb, cos, sin, src, l, lb_live):
    tag = f"l{l}_mix"
    x, h, proj, hf, oscan, oa, sall, qk, outs, lses, ob, ya, yb, merged, w = saved
    g_wout = _mm_tn(merged, dx, nb=1, name=tag + "_bwd_wout", tm=1024, tk=D_MODEL, tn=D_MODEL)
    dya, dyb, dgab, doa, dob = _mix_tail_bwd(dx, proj, ya, yb, w["w_branch_a"], w["w_branch_b"], w["w_out"], name=tag + "_bwd_tail")
    g_wa = _mm_tn(oa, dya, nb=1, name=tag + "_bwd_wa", tm=1024, tk=D_MODEL, tn=D_MODEL)
    g_wb = _mm_tn(ob, dyb, nb=N_CHIPS, name=tag + "_bwd_wb", tm=2048, tk=ATT_GW, tn=256)
    mb = _merge_bwd(dob, outs, lses, name=tag + "_bwd_merge")
    dqk, dvs = [None] * 6, []
    for g in range(ATT_GROUPS):
        dq, dk, dv = _attn_bwd(qk[g], qk[3 + g], qk[6 + g], mb[g], lses[g], mb[3 + g], g, name=f"{tag}_bwd_attn{g}")
        dqk[g], dqk[3 + g] = dq, dk
        dvs.append(dv)
    dqk_cols, dqn, dkn = _qk_bwd(dqk, proj, cos, sin, small["attn_q_norm"], small["attn_k_norm"], name=tag + "_bwd_qk")
    dproj, dgn, dlb = _hgrn_bwd(doa, oscan, proj, hf, sall, lb, small["hgrn_out_norm"], dqk_cols, dvs, dgab,
                                name=tag + "_bwd_hgrn", precise=lb_live, rider=src.ride(tag + "_bwd_hgrn"))
    src.grads(l, dict(w_branch_a=g_wa.reshape(N_CHIPS, D_MODEL // N_CHIPS, D_MODEL), w_branch_b=g_wb,
                      w_out=g_wout.reshape(N_CHIPS, D_MODEL // N_CHIPS, D_MODEL)))
    g_win = _mm_tn(h, dproj, nb=N_CHIPS, name=tag + "_bwd_win", tm=2048, tk=D_MODEL, tn=896, rider=src.ride(tag + "_bwd_win"))
    src.grads(l, dict(w_in=g_win))
    dx, dg = _mm_nt(dproj, w["w_in"], name=tag + "_bwd_dh", tm=1024, tp=D_MODEL, tn=2688, out_dtype=F32,
                    rider=src.ride(tag + "_bwd_dh"), norm=(x, small["mix_norm"], dx))
    return dx, dict(mix_norm=dg, hgrn_out_norm=dgn, lb=dlb, attn_q_norm=dqn, attn_k_norm=dkn)


BIG = ("ffn1_w_in", "ffn1_w_out", "w_in", "w_branch_a", "w_branch_b", "w_out", "ffn2_w_in", "ffn2_w_out")
ROW_SHARDED = ("ffn1_w_out", "w_branch_a", "w_out", "ffn2_w_out")
SMALL = ("ffn1_norm", "mix_norm", "hgrn_lb_logits", "hgrn_out_norm", "attn_q_norm", "attn_k_norm", "ffn2_norm")
WEIGHTS = ("ffn1_norm", "ffn1_w_in", "ffn1_w_out", "mix_norm", "w_in", "hgrn_lb_logits", "hgrn_out_norm", "attn_q_norm",
           "attn_k_norm", "w_branch_a", "w_branch_b", "w_out", "ffn2_norm", "ffn2_w_in", "ffn2_w_out")
SMALL_ROWS = 8


def _matmul_ready(name, a):
    return a.reshape(1, a.shape[0] * a.shape[1], a.shape[2]) if name in ROW_SHARDED else a


def _layer_small(small, l):
    s = {n: small[n][l].reshape(1, D_MODEL) for n in ("ffn1_norm", "mix_norm", "hgrn_out_norm", "ffn2_norm")}
    s.update({n: small[n][l] for n in ("attn_q_norm", "attn_k_norm")})
    return s


def _local_step(x, target, small, src):
    t = x.shape[0]
    cos, sin = _rope_tables(t)
    lbs = _lower_bounds(small["hgrn_lb_logits"])
    saved = []
    for l in range(2):
        sm = _layer_small(small, l)
        lb = lbs[l].reshape(1, D_MODEL)
        x, s1 = _ffn_fwd(x, sm["ffn1_norm"], src, l, "ffn1")
        x, s2 = _mix_fwd(x, sm, lb, cos, sin, src, l)
        x, s3 = _ffn_fwd(x, sm["ffn2_norm"], src, l, "ffn2")
        saved.append((sm, lb, s1, s2, s3))
    dx, sq = _loss_fwd_bwd(x, target, name="loss")
    small_rows = [None, None]
    for l in (1, 0):
        sm, lb, s1, s2, s3 = saved[l]
        dx, dg2 = _ffn_bwd(dx, s3, sm["ffn2_norm"], src, l, "ffn2")
        dx, g = _mix_bwd(dx, s2, sm, lb, cos, sin, src, l, lb_live=l > 0)
        dx, dg1 = _ffn_bwd(dx, s1, sm["ffn1_norm"], src, l, "ffn1")
        pad = lambda a: jnp.pad(a[:ATT_GROUPS].reshape(1, ATT_GROUPS * HEAD), ((0, 0), (0, D_MODEL - ATT_GROUPS * HEAD)))
        small_rows[l] = jnp.concatenate(
            [dg1, g["mix_norm"], g["lb"], g["hgrn_out_norm"], pad(g["attn_q_norm"]), pad(g["attn_k_norm"]), dg2,
             jnp.zeros((SMALL_ROWS - 7, D_MODEL), F32)], axis=0)
    return jnp.sum(sq), dx, jnp.concatenate(small_rows, axis=0)


def _coords():
    return lax.axis_index("x"), lax.axis_index("y"), lax.axis_index("c")


def _other_chips(x, y):
    return [(1 - x, y), (x, 1 - y), (1 - x, 1 - y)]


def _half_rows(rows, which):
    return pl.ds(which * (rows // 2), rows // 2)


def _gather_rider(shards):
    n = len(shards)

    def copies(w, full, sems):
        send, recv, fsend, frecv, osend, orecv = sems
        x, y, c = _coords()
        slot = 2 * x + y
        chips = _other_chips(x, y)

        def copy(i, j, blk, src, pair, to):
            return pltpu.make_async_remote_copy(src_ref=src, dst_ref=blk, send_sem=pair[0].at[i * 3 + j],
                                                recv_sem=pair[1].at[i * 3 + j], device_id=to, device_id_type=MESH)

        def block(i, chip_slot, core):
            return full[i].at[chip_slot, _half_rows(shards[i].shape[0], core)]

        pairs = [(i, j, chip) for i in range(n) for j, chip in enumerate(chips)]

        def first():
            return [copy(i, j, block(i, slot, c), w[i].at[_half_rows(shards[i].shape[0], c)], (send, recv), (*chip, c))
                    for i, j, chip in pairs]

        def landed(core, pair):
            return [copy(i, j, block(i, 2 * chip[0] + chip[1], core), block(i, 2 * chip[0] + chip[1], core), pair, (x, y, 1 - c))
                    for i, j, chip in pairs]

        def own():
            return [pltpu.make_async_remote_copy(src_ref=w[i], dst_ref=full[i].at[slot], send_sem=osend.at[i],
                                                 recv_sem=orecv.at[i], device_id=(x, y, 1 - c), device_id_type=MESH)
                    for i in range(n)]

        return first, landed, own

    def begin(w, full, sems):
        first, _, own = copies(w, full, sems)
        for cp in first() + own():
            cp.start()

    def end(w, full, sems):
        first, landed, own = copies(w, full, sems)
        forwards = landed(lax.axis_index("c"), sems[2:4])
        for arrival, forward in zip(landed(lax.axis_index("c"), sems[:2]), forwards):
            arrival.wait_recv()
            forward.start()
        for cp in landed(1 - lax.axis_index("c"), sems[2:4]) + own():
            cp.wait_recv()
        for cp in first() + forwards + own():
            cp.wait_send()

    out_shape = [jax.ShapeDtypeStruct((N_CHIPS,) + s.shape, s.dtype) for s in shards]
    sems = [pltpu.SemaphoreType.DMA((3 * n,))] * 4 + [pltpu.SemaphoreType.DMA((n,))] * 2
    return _Rider(shards, out_shape, sems, begin, end)


N_RECV = 7


def _scatter_rider(parts):
    n = len(parts)

    def copies(p, out, sems):
        send, recv = sems
        x, y, c = _coords()
        slot = 2 * x + y
        chips = _other_chips(x, y)

        def arrivals():
            return [pltpu.make_async_remote_copy(
                src_ref=out[i].at[k], dst_ref=out[i].at[k], send_sem=send.at[0], recv_sem=recv.at[i * N_RECV + k],
                device_id=(x, y, c), device_id_type=MESH) for i in range(n) for k in range(N_RECV)]

        sends = []
        for i in range(n):
            rows = parts[i].shape[1]
            for j, chip in enumerate(chips):
                for core in (0, 1):
                    sends.append(pltpu.make_async_remote_copy(
                        src_ref=p[i].at[2 * chip[0] + chip[1], _half_rows(rows, core)], dst_ref=out[i].at[2 * j + c],
                        send_sem=send.at[i * N_RECV + 2 * j + core], recv_sem=recv.at[i * N_RECV + 2 * j + c],
                        device_id=(*chip, core), device_id_type=MESH))
            sends.append(pltpu.make_async_remote_copy(
                src_ref=p[i].at[slot, _half_rows(rows, 1 - c)], dst_ref=out[i].at[6], send_sem=send.at[i * N_RECV + 6],
                recv_sem=recv.at[i * N_RECV + 6], device_id=(x, y, 1 - c), device_id_type=MESH))
        return sends, arrivals

    def begin(p, out, sems):
        for cp in copies(p, out, sems)[0]:
            cp.start()

    def end(p, out, sems):
        sends, arrivals = copies(p, out, sems)
        for cp in arrivals():
            cp.wait_recv()
        for cp in sends:
            cp.wait_send()

    out_shape = [jax.ShapeDtypeStruct((N_RECV, a.shape[1] // 2, a.shape[2]), a.dtype) for a in parts]
    return _Rider(parts, out_shape, [pltpu.SemaphoreType.DMA((N_RECV * n,))] * 2, begin, end)


def _run_alone(rider, name):
    _pcall(lambda: None, grid=(), in_specs=[], out_specs=[], out_shape=[], name=name, sem=(), args=(), rider=rider)
    return rider.result


def _sum_partials(own, parts, name):
    r, wd = own.shape
    tm = next(t for t in (256, 128, 64, 32, 16) if r % t == 0)

    def body(own_ref, p_ref, o_ref):
        acc = own_ref[...].astype(F32)
        for k in range(N_RECV):
            acc = acc + p_ref[k].astype(F32)
        o_ref[...] = acc

    return pl.pallas_call(
        body, grid=(r // tm,),
        in_specs=[pl.BlockSpec((tm, wd), lambda i: (i, 0)), pl.BlockSpec((N_RECV, tm, wd), lambda i: (0, i, 0))],
        out_specs=pl.BlockSpec((tm, wd), lambda i: (i, 0)), out_shape=jax.ShapeDtypeStruct((r, wd), F32),
        name=name, compiler_params=_params(("parallel",)))(own, parts)


def _exchange_halves(reduced, name):
    n = len(reduced)

    def body(*refs):
        r, out = refs[:n], refs[n:2 * n]
        send, recv = refs[2 * n:]
        x, y, c = _coords()
        sib = [pltpu.make_async_remote_copy(src_ref=r[i], dst_ref=out[i], send_sem=send.at[i], recv_sem=recv.at[i],
                                            device_id=(x, y, 1 - c), device_id_type=MESH) for i in range(n)]
        for cp in sib:
            cp.start()
        for cp in sib:
            cp.wait_recv()
        for cp in sib:
            cp.wait_send()

    out_shape = [jax.ShapeDtypeStruct(a.shape, a.dtype) for a in reduced]
    return pl.pallas_call(body, in_specs=[ANY] * n, out_specs=[ANY] * n, out_shape=out_shape,
                          scratch_shapes=[pltpu.SemaphoreType.DMA((n,))] * 2, name=name)(*reduced)


def _reduce_finish(parts, recv, tag):
    x, y, c = _coords()
    slot = 2 * x + y
    halves = []
    for i, (p, r) in enumerate(zip(parts, recv)):
        half = p.shape[1] // 2
        own = lax.dynamic_slice(p, (slot, c * half, 0), (1, half, p.shape[2]))[0]
        halves.append(_sum_partials(own, r, name=f"{tag}_sum{i}"))
    theirs = _exchange_halves(halves, name=tag + "_exchange")
    return [jnp.where(c == 0, jnp.concatenate([h, t], axis=0), jnp.concatenate([t, h], axis=0)) for h, t in zip(halves, theirs)]


GATHER_RIDES = {
    "l0_ffn1_in_act": ((0, "w_in"),),
    "l0_ffn1_out": ((0, "w_branch_a"), (0, "w_branch_b"), (0, "w_out")),
    "l0_mix_in": ((0, "ffn2_w_in"), (0, "ffn2_w_out"), (1, "ffn1_w_in"), (1, "ffn1_w_out")),
    "l0_mix_hgrn": ((1, "w_in"), (1, "w_branch_a"), (1, "w_branch_b"), (1, "w_out")),
    "l0_ffn2_in_act": ((1, "ffn2_w_in"), (1, "ffn2_w_out")),
}
ALONE_FIRST = ((0, "ffn1_w_in"), (0, "ffn1_w_out"))
SCATTER_RIDES = {
    "l1_mix_bwd_hgrn": ((1, "ffn2_w_in"), (1, "ffn2_w_out")),
    "l0_ffn2_bwd_win": ((1, "ffn1_w_in"),),
    "l0_ffn2_bwd_dh": ((1, "ffn1_w_out"), (1, "w_branch_a"), (1, "w_branch_b"), (1, "w_out")),
    "l0_mix_bwd_hgrn": ((1, "w_in"), (0, "ffn2_w_out")),
    "l0_mix_bwd_win": ((0, "ffn2_w_in"),),
    "l0_mix_bwd_dh": ((0, "w_in"),),
    "l0_ffn1_bwd_wout": ((0, "w_branch_a"), (0, "w_branch_b"), (0, "w_out")),
    "l0_ffn1_bwd_du_act": ((0, "ffn1_w_out"),),
    "l0_ffn1_bwd_dh": ((0, "ffn1_w_in"),),
}


class _Exchange:
    def __init__(self, shards):
        self.shards = shards
        self.pending = []
        self.full = {}
        self.parts = {}
        self.recv = {}

    def _gather(self, keys):
        return _gather_rider([self.shards[n][l] for l, n in keys]), "gather", list(keys)

    def _scatter(self, keys):
        return _scatter_rider([self.parts[k] for k in keys]), "scatter", list(keys)

    def _unpack(self):
        waiting = []
        for rider, kind, keys in self.pending:
            if rider.result is None:
                waiting.append((rider, kind, keys))
            elif kind == "gather":
                self.full.update(zip(keys, rider.result))
            else:
                self.recv.update(zip(keys, rider.result))
        self.pending = waiting

    def ride(self, host):
        if host in GATHER_RIDES:
            self.pending.append(self._gather(GATHER_RIDES[host]))
        elif host in SCATTER_RIDES:
            self.pending.append(self._scatter(SCATTER_RIDES[host]))
        else:
            return None
        return self.pending[-1][0]

    def weight(self, l, name):
        self._unpack()
        if (l, name) not in self.full:
            assert (l, name) in ALONE_FIRST, (l, name)
            job = self._gather(ALONE_FIRST)
            _run_alone(job[0], name="gather_first")
            self.pending.append(job)
            self._unpack()
        return _matmul_ready(name, self.full[(l, name)])

    def grads(self, l, partials):
        self.parts.update({(l, n): a for n, a in partials.items()})

    def reduce(self):
        self._unpack()
        assert not self.pending and set(self.recv) == set(self.parts)
        out = {}
        for l in range(2):
            done = _reduce_finish([self.parts[(l, n)] for n in BIG], [self.recv[(l, n)] for n in BIG], f"reduce_l{l}")
            out[l] = dict(zip(BIG, done))
        return {n: jnp.stack([out[0][n], out[1][n]], axis=0) for n in BIG}


def _all_reduce_small(rows):
    r = rows.shape[0]

    def body(x_ref, o_ref, buf, send, recv):
        x, y, c = _coords()
        me = 4 * x + 2 * y + c
        buf[me] = x_ref[...]
        copies = []
        for k in range(1, 8):
            peer = (x ^ (k >> 2), y ^ ((k >> 1) & 1), c ^ (k & 1))
            cp = pltpu.make_async_remote_copy(src_ref=x_ref, dst_ref=buf.at[me], send_sem=send.at[k - 1], recv_sem=recv.at[me],
                                              device_id=peer, device_id_type=MESH)
            cp.start()
            copies.append(cp)
        for k in range(1, 8):
            src = 4 * (x ^ (k >> 2)) + 2 * (y ^ ((k >> 1) & 1)) + (c ^ (k & 1))
            pltpu.make_async_remote_copy(src_ref=x_ref, dst_ref=buf.at[src], send_sem=send.at[0], recv_sem=recv.at[src],
                                         device_id=(x, y, c), device_id_type=MESH).wait_recv()
        for cp in copies:
            cp.wait_send()
        acc = buf[0]
        for k in range(1, 8):
            acc = acc + buf[k]
        o_ref[...] = acc

    vm = pl.BlockSpec(memory_space=pltpu.VMEM)
    return pl.pallas_call(
        body, in_specs=[vm], out_specs=vm, out_shape=jax.ShapeDtypeStruct(rows.shape, F32),
        scratch_shapes=[pltpu.VMEM((8, r, D_MODEL), F32), pltpu.SemaphoreType.DMA((7,)), pltpu.SemaphoreType.DMA((8,))],
        name="all_reduce_small")(rows)


def _adamw_math(w, g, m, v):
    m = ADAM_B1 * m + (1.0 - ADAM_B1) * g
    v = ADAM_B2 * v + (1.0 - ADAM_B2) * (g * g)
    m_hat = m / (1.0 - ADAM_B1 ** ADAM_STEP)
    v_hat = v / (1.0 - ADAM_B2 ** ADAM_STEP)
    return -ADAM_LR * (m_hat / (jnp.sqrt(v_hat) + ADAM_EPS) + ADAM_WD * w), m, v


def _adamw(w, g, m, v, name):
    shape = w.shape
    cols = shape[-1]
    flat = lambda a: a.reshape(-1, cols)
    rows = flat(w).shape[0]
    tm = 128 if rows % 128 == 0 else rows
    ins = [('t', flat(a), cols, 0) for a in (w, g, m, v)]
    res = _ew(_adamw_math, ins, [('t', cols, F32)] * 3, rows=rows, tm=tm, name=name)
    return [a.reshape(shape) for a in res]


def _small_update(sums, logits, w, m, v):
    def body(s_ref, lg_ref, w_ref, m_ref, v_ref, g_ref, d_ref, nm_ref, nv_ref):
        s = s_ref[...]
        l0, l1 = lg_ref[0:1, :], lg_ref[1:2, :]
        mx = jnp.maximum(l0, l1)
        e0, e1 = jnp.exp(l0 - mx), jnp.exp(l1 - mx)
        sm0, sm1 = e0 / (e0 + e1), e1 / (e0 + e1)
        dl1 = s_ref[SMALL_ROWS + 2:SMALL_ROWS + 3, :] * sm0 * sm1
        row = lax.broadcasted_iota(jnp.int32, s.shape, 0)
        g = jnp.where(row == 2, -dl1, jnp.where(row == SMALL_ROWS + 2, dl1, s))
        d, nm, nv = _adamw_math(w_ref[...], g, m_ref[...], v_ref[...])
        g_ref[...] = g
        d_ref[...] = d
        nm_ref[...] = nm
        nv_ref[...] = nv

    vm = pl.BlockSpec(memory_space=pltpu.VMEM)
    return pl.pallas_call(body, in_specs=[vm] * 5, out_specs=[vm] * 4,
                          out_shape=[jax.ShapeDtypeStruct(sums.shape, F32)] * 4, name="small_update")(sums, logits, w, m, v)


def _pack_small(vals):
    rows = []
    for l in range(2):
        for n in ("ffn1_norm", "mix_norm", "hgrn_lb_logits", "hgrn_out_norm", "attn_q_norm", "attn_k_norm", "ffn2_norm"):
            a = vals[n][l].reshape(1, -1)
            rows.append(jnp.pad(a, ((0, 0), (0, D_MODEL - a.shape[1]))))
        rows.append(jnp.zeros((SMALL_ROWS - 7, D_MODEL), F32))
    return jnp.concatenate(rows, axis=0)


def _unpack_small(packed):
    out = {}
    for k, n in enumerate(("ffn1_norm", "mix_norm", "hgrn_lb_logits", "hgrn_out_norm", "attn_q_norm", "attn_k_norm", "ffn2_norm")):
        a = jnp.stack([packed[k], packed[SMALL_ROWS + k]], axis=0)
        out[n] = a[:, :ATT_GROUPS * HEAD].reshape(2, ATT_GROUPS, HEAD) if n.startswith("attn") else a
    return out


def kernel(x, ffn1_norm, ffn1_w_in, ffn1_w_out, mix_norm, w_in, hgrn_lb_logits, hgrn_out_norm, attn_q_norm, attn_k_norm, w_branch_a, w_branch_b, w_out, ffn2_norm, ffn2_w_in, ffn2_w_out, loss_target, m_ffn1_norm, m_ffn1_w_in, m_ffn1_w_out, m_mix_norm, m_w_in, m_hgrn_lb_logits, m_hgrn_out_norm, m_attn_q_norm, m_attn_k_norm, m_w_branch_a, m_w_branch_b, m_w_out, m_ffn2_norm, m_ffn2_w_in, m_ffn2_w_out, v_ffn1_norm, v_ffn1_w_in, v_ffn1_w_out, v_mix_norm, v_w_in, v_hgrn_lb_logits, v_hgrn_out_norm, v_attn_q_norm, v_attn_k_norm, v_w_branch_a, v_w_branch_b, v_w_out, v_ffn2_norm, v_ffn2_w_in, v_ffn2_w_out):
    a = locals()
    w = {n: a[n] for n in WEIGHTS}
    m = {n: a["m_" + n] for n in WEIGHTS}
    v = {n: a["v_" + n] for n in WEIGHTS}

    exchange = _Exchange({n: w[n].astype(BF16) for n in BIG})
    small = {n: w[n] for n in SMALL}
    sq, grad_x, small_rows = _local_step(x[0], loss_target[0], small, exchange)
    loss = lax.psum(sq, ("x", "y", "c")) * (0.5 / D_MODEL)
    grads = exchange.reduce()

    sums = _all_reduce_small(small_rows)
    g_s, d_s, m_s, v_s = _small_update(sums, w["hgrn_lb_logits"], _pack_small(small), _pack_small({n: m[n] for n in SMALL}),
                                       _pack_small({n: v[n] for n in SMALL}))
    grads.update(_unpack_small(g_s))
    delta, new_m, new_v = _unpack_small(d_s), _unpack_small(m_s), _unpack_small(v_s)
    for n in BIG:
        delta[n], new_m[n], new_v[n] = _adamw(w[n], grads[n], m[n], v[n], name="adamw_" + n)

    return (loss, grad_x[None], *[grads[n] for n in WEIGHTS], *[delta[n] for n in WEIGHTS],
            *[new_m[n] for n in WEIGHTS], *[new_v[n] for n in WEIGHTS])
```

```python
import functools

import jax
import jax.numpy as jnp
from jax import lax
from jax.experimental import pallas as pl
from jax.experimental.pallas import tpu as pltpu

F32 = jnp.float32
BF16 = jnp.bfloat16
MESH = pl.DeviceIdType.MESH

D_MODEL = 1024
D_FF = 2816
N_CHIPS = 4
HEAD = 128
HG_HEADS = 8
HG_CHUNK = 64
ATT_GROUPS = 3
ATT_HEADS = 4
ATT_GW = ATT_HEADS * HEAD
DILATIONS = (1, 4, 16)
ATT_BLK = 128
ATT_STEP_BLOCKS = 4
P_IN = 10752
CB_AQ, CB_AK, CB_AV, CB_GA, CB_GB = 8, 11, 14, 17, 19
EPS = 1e-6
ROPE_THETA = 10000.0
ADAM_LR, ADAM_B1, ADAM_B2, ADAM_EPS, ADAM_WD, ADAM_STEP = 0.001, 0.9, 0.999, 1e-08, 0.01, 10
VMEM_LIMIT_V7X = 56 * 1024 * 1024
NEG = -1e30


def _params(sem):
    return pltpu.CompilerParams(dimension_semantics=sem, vmem_limit_bytes=VMEM_LIMIT_V7X)


def _sig(x):
    return 1.0 / (1.0 + jnp.exp(-x))


def _dot(a, b):
    return jnp.dot(a, b, preferred_element_type=F32)


def _dot_nt(a, b):
    return lax.dot_general(a, b, (((1,), (1,)), ((), ())), preferred_element_type=F32)


def _dot_tn(a, b):
    return lax.dot_general(a, b, (((0,), (0,)), ((), ())), preferred_element_type=F32)


def _bf(x):
    return x.astype(BF16)


ANY = pl.BlockSpec(memory_space=pl.ANY)


class _Rider:
    def __init__(self, args, out_shape, sems, begin, end):
        self.args, self.out_shape, self.sems, self.begin, self.end = list(args), list(out_shape), list(sems), begin, end
        self.result = None


def _pcall(body, *, grid, in_specs, out_specs, out_shape, name, sem, args, scratch_shapes=(), rider=None):
    multi = isinstance(out_shape, (list, tuple))
    o_specs = list(out_specs) if multi else [out_specs]
    o_shape = list(out_shape) if multi else [out_shape]
    if rider is None:
        res = pl.pallas_call(body, grid=grid, in_specs=list(in_specs), out_specs=o_specs, out_shape=o_shape,
                             scratch_shapes=list(scratch_shapes), name=name, compiler_params=_params(sem))(*args)
        return list(res) if multi else res[0]
    counts = [len(in_specs), len(rider.args), len(o_specs), len(rider.out_shape), len(scratch_shapes)]

    def wrapped(*refs):
        groups, at = [], 0
        for c in counts:
            groups.append(refs[at:at + c])
            at += c
        h_in, r_in, h_out, r_out, h_scratch = groups
        r_sems = refs[at:]
        if grid:
            ids = [pl.program_id(a) for a in range(len(grid))]
            first = functools.reduce(jnp.logical_and, [i == 0 for i in ids])
            last = functools.reduce(jnp.logical_and, [i == g - 1 for i, g in zip(ids, grid)])
            pl.when(first)(lambda: rider.begin(r_in, r_out, r_sems))
            body(*h_in, *h_out, *h_scratch)
            pl.when(last)(lambda: rider.end(r_in, r_out, r_sems))
        else:
            rider.begin(r_in, r_out, r_sems)
            body(*h_in, *h_out, *h_scratch)
            rider.end(r_in, r_out, r_sems)

    res = pl.pallas_call(
        wrapped, grid=grid, in_specs=list(in_specs) + [ANY] * counts[1], out_specs=o_specs + [ANY] * counts[3],
        out_shape=o_shape + rider.out_shape, scratch_shapes=list(scratch_shapes) + rider.sems, name=name,
        compiler_params=_params(("arbitrary",) * len(grid)))(*args, *rider.args)
    rider.result = list(res[counts[2]:])
    return list(res[:counts[2]]) if multi else res[0]


def _mm_nn(a, b3, *, name, tm, tn, out_dtype, res=None, alpha=1.0, rider=None):
    m, k = a.shape
    nb, _, nw = b3.shape
    per = nw // tn
    assert nw % tn == 0 and m % tm == 0
    has_res = res is not None

    def body(*refs):
        if has_res:
            a_ref, b_ref, r_ref, o_ref = refs
        else:
            a_ref, b_ref, o_ref = refs
        acc = _dot(_bf(a_ref[...]), b_ref[...])
        if alpha != 1.0:
            acc = alpha * acc
        if has_res:
            acc = r_ref[...] + acc
        o_ref[...] = acc.astype(o_ref.dtype)

    in_specs = [pl.BlockSpec((tm, k), lambda i, j: (i, 0)),
                pl.BlockSpec((None, k, tn), lambda i, j: (j // per, 0, j % per))]
    args = [a, b3]
    if has_res:
        in_specs.append(pl.BlockSpec((tm, tn), lambda i, j: (i, j)))
        args.append(res)
    return _pcall(body, grid=(m // tm, nb * per), in_specs=in_specs, out_specs=pl.BlockSpec((tm, tn), lambda i, j: (i, j)),
                  out_shape=jax.ShapeDtypeStruct((m, nb * nw), out_dtype), name=name, sem=("parallel", "arbitrary"),
                  args=args, rider=rider)


def _mm_nt(d, b3, *, name, tm, tp, tn, out_dtype, alpha=1.0, rider=None, norm=None):
    m, n = d.shape
    nb, p, nw = b3.shape
    per = nw // tn
    nk = n // tn
    assert nb * nw == n and nw % tn == 0 and p % tp == 0 and m % tm == 0 and (norm is None or tp == p)

    def body(d_ref, b_ref, *refs):
        kk = pl.program_id(2)
        acc_ref = refs[-1]

        @pl.when(kk == 0)
        def _():
            acc_ref[...] = jnp.zeros_like(acc_ref)

        acc_ref[...] += _dot_nt(_bf(d_ref[...]), b_ref[...])

        if norm is None:
            @pl.when(kk == nk - 1)
            def _():
                refs[0][...] = (alpha * acc_ref[...]).astype(refs[0].dtype)
        else:
            x_ref, g_ref, dx_ref, o_ref, dg_ref = refs[:5]

            @pl.when(jnp.logical_and(pl.program_id(0) == 0, kk == 0))
            def _():
                dg_ref[...] = jnp.zeros_like(dg_ref)

            @pl.when(kk == nk - 1)
            def _():
                dh = alpha * acc_ref[...]
                xv = x_ref[...]
                r = _rms_rows(xv)
                xh = xv * r
                dxh = dh * g_ref[...]
                o_ref[...] = dx_ref[...] + r * (dxh - xh * jnp.mean(dxh * xh, axis=1, keepdims=True))
                dg_ref[...] += jnp.sum(dh * xh, axis=0, keepdims=True)

    in_specs = [pl.BlockSpec((tm, tn), lambda i, j, kk: (i, kk)),
                pl.BlockSpec((None, tp, tn), lambda i, j, kk: (kk // per, j, kk % per))]
    tile = pl.BlockSpec((tm, tp), lambda i, j, kk: (i, j))
    if norm is None:
        return _pcall(body, grid=(m // tm, p // tp, nk), in_specs=in_specs, out_specs=tile,
                      out_shape=jax.ShapeDtypeStruct((m, p), out_dtype), scratch_shapes=[pltpu.VMEM((tm, tp), F32)],
                      name=name, sem=("parallel", "parallel", "arbitrary"), args=(d, b3), rider=rider)
    x, g, dx = norm
    row = pl.BlockSpec((1, p), lambda i, j, kk: (0, 0))
    return _pcall(body, grid=(m // tm, 1, nk), in_specs=in_specs + [tile, row, tile], out_specs=[tile, row],
                  out_shape=[jax.ShapeDtypeStruct((m, p), F32), jax.ShapeDtypeStruct((1, p), F32)],
                  scratch_shapes=[pltpu.VMEM((tm, tp), F32)], name=name, sem=("arbitrary", "arbitrary", "arbitrary"),
                  args=(d, b3, x, g, dx), rider=rider)


def _mm_tn(a, d, *, nb, name, tm, tk, tn, alpha=1.0, rider=None):
    m, k = a.shape
    _, n = d.shape
    nw = n // nb
    per = nw // tn
    nm = m // tm
    assert nw % tn == 0 and k % tk == 0 and m % tm == 0

    def body(a_ref, d_ref, o_ref, acc_ref):
        mm = pl.program_id(2)

        @pl.when(mm == 0)
        def _():
            acc_ref[...] = jnp.zeros_like(acc_ref)

        acc_ref[...] += _dot_tn(_bf(a_ref[...]), _bf(d_ref[...]))

        @pl.when(mm == nm - 1)
        def _():
            o_ref[...] = (alpha * acc_ref[...]).astype(o_ref.dtype)

    return _pcall(
        body, grid=(k // tk, nb * per, nm),
        in_specs=[pl.BlockSpec((tm, tk), lambda i, j, mm: (mm, i)),
                  pl.BlockSpec((tm, tn), lambda i, j, mm: (mm, j))],
        out_specs=pl.BlockSpec((None, tk, tn), lambda i, j, mm: (j // per, i, j % per)),
        out_shape=jax.ShapeDtypeStruct((nb, k, nw), BF16),
        scratch_shapes=[pltpu.VMEM((tk, tn), F32)],
        name=name, sem=("parallel", "parallel", "arbitrary"), args=(a, d), rider=rider)


def _rows_from_view(ref, buf, w, d, tm):
    for k in range(d):
        for c in range(w // HEAD):
            lanes = slice(k * w + c * HEAD, k * w + (c + 1) * HEAD)
            buf.at[c][pl.ds(k, tm // d, stride=d), :] = ref[:, lanes].astype(F32)
    return _cat([buf[c] for c in range(w // HEAD)])


def _ew(fn, ins, outs, *, rows, tm, name):
    in_specs, args, scratch = [], [], []
    for s in ins:
        if s[0] == 't':
            _, arr, w, cb = s
            in_specs.append(pl.BlockSpec((tm, w), lambda i, cb=cb: (i, cb)))
        elif s[0] == 'v':
            _, arr, w, d = s
            in_specs.append(pl.BlockSpec((tm // d, d * w), lambda i: (i, 0)))
            scratch.append(pltpu.VMEM((w // HEAD, tm, HEAD), F32))
        else:
            arr = s[1]
            in_specs.append(pl.BlockSpec(arr.shape, lambda i, nd=arr.ndim: (0,) * nd))
        args.append(arr)
    out_specs, out_shape = [], []
    for s in outs:
        if s[0] == 't':
            _, w, dt = s
            out_specs.append(pl.BlockSpec((tm, w), lambda i: (i, 0)))
            out_shape.append(jax.ShapeDtypeStruct((rows, w), dt))
        elif s[0] == 'v':
            _, w, dt, d = s
            out_specs.append(pl.BlockSpec((tm // d, d * w), lambda i: (i, 0)))
            out_shape.append(jax.ShapeDtypeStruct((rows // d, d * w), dt))
            scratch.append(pltpu.VMEM((w // HEAD, tm, HEAD), F32))
        else:
            out_specs.append(pl.BlockSpec(s[1], lambda i: (0, 0)))
            out_shape.append(jax.ShapeDtypeStruct(s[1], F32))
    n_in, n_out = len(ins), len(outs)

    def body(*refs):
        bufs = list(refs[n_in + n_out:])
        vals = []
        for r, s in zip(refs[:n_in], ins):
            if s[0] == 'v':
                vals.append(_rows_from_view(r, bufs.pop(0), s[2], s[3], tm))
            else:
                vals.append(r[...])
        res = fn(*vals)
        if not isinstance(res, (tuple, list)):
            res = (res,)
        for r, s, v in zip(refs[n_in:n_in + n_out], outs, res):
            if s[0] == 't':
                r[...] = v.astype(r.dtype)
            elif s[0] == 'v':
                w, d, buf = s[1], s[3], bufs.pop(0)
                for c in range(w // HEAD):
                    buf[c] = v[:, c * HEAD:(c + 1) * HEAD].astype(F32)
                for k in range(d):
                    for c in range(w // HEAD):
                        lanes = slice(k * w + c * HEAD, k * w + (c + 1) * HEAD)
                        r[:, lanes] = buf.at[c][pl.ds(k, tm // d, stride=d), :].astype(r.dtype)
            else:
                @pl.when(pl.program_id(0) == 0)
                def _(r=r):
                    r[...] = jnp.zeros_like(r)

                r[...] += v

    res = pl.pallas_call(
        body, grid=(rows // tm,), in_specs=in_specs, out_specs=out_specs, out_shape=out_shape, scratch_shapes=scratch,
        name=name, compiler_params=_params(("arbitrary",)))(*args)
    return res


def _tile(arr, w, g):
    return ('t', arr, w, 0) if DILATIONS[g] == 1 else ('v', arr, w, DILATIONS[g])


def _tile_out(w, dtype, g):
    return ('t', w, dtype) if DILATIONS[g] == 1 else ('v', w, dtype, DILATIONS[g])


def _heads(x):
    return [x[:, h * HEAD:(h + 1) * HEAD] for h in range(x.shape[1] // HEAD)]


def _cat(xs):
    return jnp.concatenate(xs, axis=1)


def _head_mean(x):
    return _cat([jnp.broadcast_to(jnp.mean(h, axis=1, keepdims=True), h.shape) for h in _heads(x)])


def _rms_rows(x):
    return lax.rsqrt(jnp.mean(x * x, axis=1, keepdims=True) + EPS)


def _norm_fwd(x, g, name):
    return _ew(lambda xv, gv: xv * _rms_rows(xv) * gv,
               [('t', x, D_MODEL, 0), ('f', g)], [('t', D_MODEL, BF16)], rows=x.shape[0], tm=512, name=name)[0]


def _loss_fwd_bwd(y, target, name):
    def fn(yv, tv):
        e = yv - tv
        return e * (1.0 / D_MODEL), jnp.sum(e * e, axis=0, keepdims=True)

    return _ew(fn, [('t', y, D_MODEL, 0), ('t', target, D_MODEL, 0)], [('t', D_MODEL, F32), ('acc', (1, D_MODEL))],
               rows=y.shape[0], tm=512, name=name)


def _rot(x):
    sgn = jnp.where(lax.broadcasted_iota(jnp.int32, x.shape, 1) < HEAD // 2, -1.0, 1.0)
    return pltpu.roll(x, HEAD // 2, 1) * sgn


def _gain_rows(qn, kn):
    return [a[g:g + 1] for a in (qn, kn) for g in range(ATT_GROUPS)]


def _qk_fwd(proj, cos, sin, qn, kn, name):
    def fn(*v):
        xs, cosv, sinv, gains, vs = v[:6], v[6], v[7], v[8:14], v[14:17]
        outs = []
        for j, x in enumerate(xs):
            gain = gains[j]
            ys = []
            for xh in _heads(x.astype(F32)):
                xn = xh * _rms_rows(xh) * gain
                ys.append(xn * cosv + _rot(xn) * sinv)
            outs.append(_cat(ys))
        return outs + list(vs)

    ins = ([('t', proj, 512, CB_AQ + j) for j in range(6)] + [('t', cos, HEAD, 0), ('t', sin, HEAD, 0)]
           + [('f', a) for a in _gain_rows(qn, kn)] + [('t', proj, 512, CB_AV + g) for g in range(ATT_GROUPS)])
    return _ew(fn, ins, [_tile_out(ATT_GW, BF16, j % ATT_GROUPS) for j in range(9)], rows=proj.shape[0], tm=512, name=name)


def _qk_bwd(dqk, proj, cos, sin, qn, kn, name):
    def fn(*v):
        ds, xs, cosv, sinv, gains = v[:6], v[6:12], v[12], v[13], v[14:20]
        rows8 = lax.broadcasted_iota(jnp.int32, (8, HEAD), 0)
        outs, dgs = [], [jnp.zeros((8, HEAD), F32)] * 2
        for j in range(6):
            gain = gains[j]
            dx, dg = [], jnp.zeros((1, HEAD), F32)
            for dyh, xh in zip(_heads(ds[j]), _heads(xs[j].astype(F32))):
                r = _rms_rows(xh)
                xhat = xh * r
                dxn = dyh * cosv - _rot(dyh * sinv)
                dg = dg + jnp.sum(dxn * xhat, axis=0, keepdims=True)
                dxh = dxn * gain
                dx.append(r * (dxh - xhat * jnp.mean(dxh * xhat, axis=1, keepdims=True)))
            outs.append(_cat(dx))
            dgs[j // 3] = dgs[j // 3] + jnp.where(rows8 == j % 3, dg, 0.0)
        return _cat(outs), dgs[0], dgs[1]

    ins = ([_tile(a, ATT_GW, j % ATT_GROUPS) for j, a in enumerate(dqk)] + [('t', proj, 512, CB_AQ + j) for j in range(6)]
           + [('t', cos, HEAD, 0), ('t', sin, HEAD, 0)] + [('f', a) for a in _gain_rows(qn, kn)])
    return _ew(fn, ins, [('t', 6 * ATT_GW, BF16), ('acc', (8, HEAD)), ('acc', (8, HEAD))],
               rows=proj.shape[0], tm=256, name=name)


def _pick(x, h):
    lanes = lax.broadcasted_iota(jnp.int32, x.shape, 1)
    return jnp.sum(jnp.where(lanes == h, x, 0.0), axis=1, keepdims=True)


def _spread(x):
    return _cat([jnp.broadcast_to(_pick(x, h), (x.shape[0], HEAD)) for h in range(ATT_HEADS)])


def _compact(x):
    lanes = lax.broadcasted_iota(jnp.int32, (x.shape[0], HEAD), 1)
    out = jnp.zeros((x.shape[0], HEAD), F32)
    for h, xh in enumerate(_heads(x)):
        out = jnp.where(lanes == h, xh, out)
    return out


def _group_weights(l0, l1, l2):
    l0, l1, l2 = _spread(l0), _spread(l1), _spread(l2)
    m = jnp.maximum(jnp.maximum(l0, l1), l2)
    e0, e1, e2 = jnp.exp(l0 - m), jnp.exp(l1 - m), jnp.exp(l2 - m)
    inv = 1.0 / (e0 + e1 + e2)
    return e0 * inv, e1 * inv, e2 * inv


def _merge_fwd(outs, lses, name):
    def fn(o0, o1, o2, l0, l1, l2):
        a0, a1, a2 = _group_weights(l0, l1, l2)
        return a0 * o0 + a1 * o1 + a2 * o2

    ins = [_tile(a, ATT_GW, g) for g, a in enumerate(outs)] + [_tile(a, HEAD, g) for g, a in enumerate(lses)]
    return _ew(fn, ins, [('t', ATT_GW, BF16)], rows=outs[0].shape[0], tm=512, name=name)[0]


def _merge_bwd(dob, outs, lses, name):
    def fn(dov, o0, o1, o2, l0, l1, l2):
        a0, a1, a2 = _group_weights(l0, l1, l2)
        ob = a0 * o0 + a1 * o1 + a2 * o2
        s = _head_mean(dov * ob) * float(HEAD)
        return a0 * dov, a1 * dov, a2 * dov, _compact(a0 * s), _compact(a1 * s), _compact(a2 * s)

    ins = ([('t', dob, ATT_GW, 0)] + [_tile(a, ATT_GW, g) for g, a in enumerate(outs)]
           + [_tile(a, HEAD, g) for g, a in enumerate(lses)])
    groups = range(ATT_GROUPS)
    return _ew(fn, ins, [_tile_out(ATT_GW, BF16, g) for g in groups] + [_tile_out(HEAD, F32, g) for g in groups],
               rows=dob.shape[0], tm=512, name=name)


HG_ROWS = 256


def _hg_gates(hq, hf, hi, lbv):
    sig = _sig(hf)
    f = lbv + (1.0 - lbv) * sig
    return hq * _sig(hq), 1.0 - f, hi, jnp.log(f), sig, f


def _split3(x):
    hi = _bf(x)
    r1 = x - hi.astype(F32)
    mid = _bf(r1)
    return hi, mid, _bf(r1 - mid.astype(F32))


def _tri_dot(tri, x):
    hi, mid, lo = _split3(x)
    return _dot(tri, hi) + _dot(tri, mid) + _dot(tri, lo)


def _row(x, i):
    rows = lax.broadcasted_iota(jnp.int32, x.shape, 0)
    return jnp.sum(jnp.where(rows == i, x, 0.0), axis=0, keepdims=True)


def _hg_decay(logf, q, k):
    c = HG_CHUNK
    row = lax.broadcasted_iota(jnp.int32, (c, c), 0)
    col = lax.broadcasted_iota(jnp.int32, (c, c), 1)
    g = _tri_dot((row >= col).astype(BF16), logf)
    gm = _row(g, c // 2 - 1)
    gl = _row(g, c - 1)
    decays = jnp.exp(g), jnp.exp(g - gm), jnp.exp(gm - g), jnp.exp(gl - g)
    return gl, decays, q * decays[0], q * decays[1], k * decays[2], k * decays[3]


def _hg_out_fwd(o, hg, gain):
    r = lax.rsqrt(_head_mean(o * o) + EPS)
    return o * r * gain * (hg * _sig(hg))


def _hgrn_fwd(proj, hf, lb, gain, name, rider=None):
    t = proj.shape[0]
    nck = HG_ROWS // HG_CHUNK

    def body(hq_ref, hf_ref, hi_ref, hg_ref, lb_ref, gn_ref, o_ref, oa_ref, sall_ref, st_ref):
        @pl.when(pl.program_id(0) == 0)
        def _():
            st_ref[...] = jnp.zeros_like(st_ref)

        lbv = lb_ref[...]
        gnv = gn_ref[...]
        c = HG_CHUNK
        mask = lax.broadcasted_iota(jnp.int32, (c, c), 0) >= lax.broadcasted_iota(jnp.int32, (c, c), 1)

        def chunk(cc, carry):
            sl = pl.ds(pl.multiple_of(cc * c, c), c)
            q, k, v, logf, _, _ = _hg_gates(hq_ref[sl, :].astype(F32), hf_ref[sl, :], hi_ref[sl, :].astype(F32), lbv)
            gl, _, qg, qt, kt, kd = _hg_decay(logf, q, k)
            egl = jnp.exp(gl)
            os = []
            for h in range(HG_HEADS):
                hs = slice(h * HEAD, (h + 1) * HEAD)
                st = st_ref[h]
                sall_ref[cc, h] = st
                a = jnp.where(mask, _dot_nt(_bf(qt[:, hs]), _bf(kt[:, hs])), 0.0)
                os.append(_dot(_bf(a), _bf(v[:, hs])) + _dot_nt(_bf(qg[:, hs]), _bf(st)))
                st_ref[h] = egl[:, hs] * st + _dot_tn(_bf(v[:, hs]), _bf(kd[:, hs]))
            o = _cat(os)
            o_ref[sl, :] = o
            oa_ref[sl, :] = _hg_out_fwd(o, hg_ref[sl, :].astype(F32), gnv).astype(oa_ref.dtype)
            return carry

        lax.fori_loop(0, nck, chunk, 0)

    col = lambda j: pl.BlockSpec((HG_ROWS, D_MODEL), lambda i, j=j: (i, j))
    small = pl.BlockSpec((1, D_MODEL), lambda i: (0, 0))
    return _pcall(
        body, grid=(t // HG_ROWS,),
        in_specs=[col(0), col(0), col(2), col(3), small, small],
        out_specs=[col(0), col(0), pl.BlockSpec((nck, HG_HEADS, HEAD, HEAD), lambda i: (i, 0, 0, 0))],
        out_shape=[jax.ShapeDtypeStruct((t, D_MODEL), F32), jax.ShapeDtypeStruct((t, D_MODEL), BF16),
                   jax.ShapeDtypeStruct((t // HG_CHUNK, HG_HEADS, HEAD, HEAD), F32)],
        scratch_shapes=[pltpu.VMEM((HG_HEADS, HEAD, HEAD), F32)],
        name=name, sem=("arbitrary",), args=(proj, hf, proj, proj, lb, gain), rider=rider)


def _terms(x, precise):
    hi = _bf(x)
    return (hi, _bf(x - hi.astype(F32))) if precise else (hi,)


def _mm(dot, a, b):
    out = dot(a[0], b[0])
    if len(a) > 1:
        out = out + dot(a[1], b[0])
    if len(b) > 1:
        out = out + dot(a[0], b[1])
    return out


def _hgrn_bwd(doa, oscan, proj, hf, sall, lb, gain, dqk, dvs, dgab, name, precise, rider=None):
    t = proj.shape[0]
    nck = HG_ROWS // HG_CHUNK
    nsteps = t // HG_ROWS
    terms = functools.partial(_terms, precise=precise)
    n_view = sum(d > 1 for d in DILATIONS)

    def body(doa_ref, os_ref, hq_ref, hf_ref, hi_ref, hg_ref, sall_ref, lb_ref, gn_ref, dqk_ref, dv0_ref, dv1_ref,
             dv2_ref, dgab_ref, dproj_ref, dgn_ref, dlb_ref, dst_ref, *bufs):
        @pl.when(pl.program_id(0) == 0)
        def _():
            dst_ref[...] = jnp.zeros_like(dst_ref)
            dgn_ref[...] = jnp.zeros_like(dgn_ref)
            dlb_ref[...] = jnp.zeros_like(dlb_ref)

        at = 4 * D_MODEL
        dproj_ref[:, at:at + 6 * ATT_GW] = dqk_ref[...]
        at += 6 * ATT_GW
        spare = list(bufs)
        for d, dv_ref in zip(DILATIONS, (dv0_ref, dv1_ref, dv2_ref)):
            dv = dv_ref[...] if d == 1 else _rows_from_view(dv_ref, spare.pop(0), ATT_GW, d, HG_ROWS)
            dproj_ref[:, at:at + ATT_GW] = dv.astype(dproj_ref.dtype)
            at += ATT_GW
        dproj_ref[:, at:] = dgab_ref[...]

        lbv = lb_ref[...]
        gnv = gn_ref[...]
        c = HG_CHUNK
        row = lax.broadcasted_iota(jnp.int32, (c, c), 0)
        colm = lax.broadcasted_iota(jnp.int32, (c, c), 1)
        mask = row >= colm
        triu = (row <= colm).astype(BF16)
        last = lax.broadcasted_iota(jnp.int32, (c, HEAD), 0) == c - 1

        def chunk(ci, carry):
            cc = nck - 1 - ci
            sl = pl.ds(pl.multiple_of(cc * c, c), c)
            hq, hg = hq_ref[sl, :].astype(F32), hg_ref[sl, :].astype(F32)
            q, k, v, logf, sig, f = _hg_gates(hq, hf_ref[sl, :], hi_ref[sl, :].astype(F32), lbv)
            gl, (e_qg, e_qt, e_kt, e_kd), qg, qt, kt, kd = _hg_decay(logf, q, k)
            egl = jnp.exp(gl)
            o = os_ref[sl, :]
            dy = doa_ref[sl, :]
            r = lax.rsqrt(_head_mean(o * o) + EPS)
            oh = o * r
            sg = _sig(hg)
            silu_g = hg * sg
            dgn_ref[...] += jnp.sum(dy * oh * silu_g, axis=0, keepdims=True)
            dhg = dy * oh * gnv * (sg * (1.0 + hg * (1.0 - sg)))
            doh = dy * gnv * silu_g
            do = r * (doh - oh * _head_mean(doh * oh))
            dqs, dks, dvs, dgs = [], [], [], []
            for h in range(HG_HEADS):
                hs = slice(h * HEAD, (h + 1) * HEAD)
                st = sall_ref[cc, h]
                dst = dst_ref[h]
                qt_h, kt_h, qg_h, kd_h = qt[:, hs], kt[:, hs], qg[:, hs], kd[:, hs]
                do_p, v_p, qt_p, kt_p, qg_p = terms(do[:, hs]), terms(v[:, hs]), terms(qt_h), terms(kt_h), terms(qg_h)
                st_p, dst_p = terms(st), terms(dst)
                a = jnp.where(mask, _dot_nt(qt_p[0], kt_p[0]), 0.0)
                da = terms(jnp.where(mask, _mm(_dot_nt, do_p, v_p), 0.0))
                dqt = _mm(_dot, da, kt_p)
                dkt = _mm(_dot_tn, da, qt_p)
                dqg = _mm(_dot, do_p, st_p)
                dv = _dot_tn(_bf(a), do_p[0]) + _dot_nt(_bf(kd_h), dst_p[0])
                dkd = _mm(_dot, v_p, dst_p)
                dgl = egl[:, hs] * jnp.sum(st * dst, axis=0, keepdims=True) + jnp.sum(dkd * kd_h, axis=0, keepdims=True)
                dst_ref[h] = egl[:, hs] * dst + _mm(_dot_tn, do_p, qg_p)
                dqs.append(dqt * e_qt[:, hs] + dqg * e_qg[:, hs])
                dks.append(dkt * e_kt[:, hs] + dkd * e_kd[:, hs])
                dvs.append(dv)
                dgs.append(dqt * qt_h - dkt * kt_h + dqg * qg_h - dkd * kd_h + jnp.where(last, dgl, 0.0))
            dq, dk, dv, dg = _cat(dqs), _cat(dks), _cat(dvs), _cat(dgs)
            dlogf = _tri_dot(triu, dg)
            df = dlogf / f - dk
            dlb_ref[...] += jnp.sum(df * (1.0 - sig), axis=0, keepdims=True)
            dhf = df * (1.0 - lbv) * sig * (1.0 - sig)
            sq = _sig(hq)
            dhq = dq * (sq * (1.0 + hq * (1.0 - sq)))
            dproj_ref[sl, :4 * D_MODEL] = _cat([dhq, dhf, dv, dhg]).astype(dproj_ref.dtype)
            return carry

        lax.fori_loop(0, nck, chunk, 0)

    rev = lambda j: pl.BlockSpec((HG_ROWS, D_MODEL), lambda i, j=j: (nsteps - 1 - i, j))
    rows = lambda a, d=1: pl.BlockSpec((HG_ROWS // d, a.shape[1]), lambda i: (nsteps - 1 - i, 0))
    small = pl.BlockSpec((1, D_MODEL), lambda i: (0, 0))
    return _pcall(
        body, grid=(nsteps,),
        in_specs=[rev(0), rev(0), rev(0), rev(0), rev(2), rev(3),
                  pl.BlockSpec((nck, HG_HEADS, HEAD, HEAD), lambda i: (nsteps - 1 - i, 0, 0, 0)), small, small,
                  rows(dqk)] + [rows(a, d) for a, d in zip(dvs, DILATIONS)] + [rows(dgab)],
        out_specs=[pl.BlockSpec((HG_ROWS, P_IN), lambda i: (nsteps - 1 - i, 0)), small, small],
        out_shape=[jax.ShapeDtypeStruct((t, P_IN), BF16), jax.ShapeDtypeStruct((1, D_MODEL), F32),
                   jax.ShapeDtypeStruct((1, D_MODEL), F32)],
        scratch_shapes=[pltpu.VMEM((HG_HEADS, HEAD, HEAD), F32)] + [pltpu.VMEM((ATT_HEADS, HG_ROWS, HEAD), F32)] * n_view,
        name=name, sem=("arbitrary",), args=(doa, oscan, proj, hf, proj, proj, sall, lb, gain, dqk, *dvs, dgab),
        rider=rider)


def _window_masks(has_previous):
    qi = lax.broadcasted_iota(jnp.int32, (ATT_BLK, 2 * ATT_BLK), 0)
    ki = lax.broadcasted_iota(jnp.int32, (ATT_BLK, 2 * ATT_BLK), 1)
    band = jnp.logical_and(ki >= qi, ki <= qi + ATT_BLK)
    return band, jnp.logical_and(band, jnp.logical_or(ki >= ATT_BLK, has_previous))


def _two_blocks(ref, prev_ref, j, hs):
    if j == 0:
        return jnp.concatenate([prev_ref[:, hs], ref[0:ATT_BLK, hs]], axis=0)
    return ref[(j - 1) * ATT_BLK:(j + 1) * ATT_BLK, hs]


def _attn_cfg(qg, g):
    d = DILATIONS[g]
    length = qg.shape[0]
    assert qg.shape[1] == d * ATT_GW
    nb = length // ATT_BLK
    return d, length, nb, min(ATT_STEP_BLOCKS, nb)


def _attn_fwd(qg, kg, vg, g, name):
    d, length, nb, rb = _attn_cfg(qg, g)
    scale = HEAD ** -0.5

    def body(q_ref, k_ref, v_ref, kp_ref, vp_ref, o_ref, l_ref):
        n = pl.program_id(1)
        band, first_band = _window_masks(n > 0)
        lanes = lax.broadcasted_iota(jnp.int32, (ATT_BLK, HEAD), 1)
        for j in range(rb):
            rows = slice(j * ATT_BLK, (j + 1) * ATT_BLK)
            lse = jnp.zeros((ATT_BLK, HEAD), F32)
            for h in range(ATT_HEADS):
                hs = slice(h * HEAD, (h + 1) * HEAD)
                k2, v2 = _two_blocks(k_ref, kp_ref, j, hs), _two_blocks(v_ref, vp_ref, j, hs)
                s = jnp.where(first_band if j == 0 else band, _dot_nt(q_ref[rows, hs], k2) * scale, NEG)
                m = jnp.max(s, axis=1, keepdims=True)
                p = jnp.exp(s - m)
                l = jnp.sum(p, axis=1, keepdims=True)
                o_ref[rows, hs] = (_dot(_bf(p), v2) / l).astype(o_ref.dtype)
                lse = jnp.where(lanes == h, m + jnp.log(l), lse)
            l_ref[rows, :] = lse

    own = pl.BlockSpec((rb * ATT_BLK, ATT_GW), lambda r, n: (n, r))
    own_head = pl.BlockSpec((rb * ATT_BLK, HEAD), lambda r, n: (n, r))
    prev = pl.BlockSpec((ATT_BLK, ATT_GW), lambda r, n: (jnp.maximum(n * rb - 1, 0), r))
    return pl.pallas_call(
        body, grid=(d, nb // rb), in_specs=[own, own, own, prev, prev], out_specs=[own, own_head],
        out_shape=[jax.ShapeDtypeStruct((length, d * ATT_GW), BF16), jax.ShapeDtypeStruct((length, d * HEAD), F32)],
        name=name, compiler_params=_params(("parallel", "arbitrary")))(qg, kg, vg, kg, vg)


def _attn_bwd(qg, kg, vg, dog, lse, delta, g, name):
    d, length, nb, rb = _attn_cfg(qg, g)
    nsteps = nb // rb
    scale = HEAD ** -0.5

    def body(q_ref, k_ref, v_ref, do_ref, l_ref, dl_ref, kp_ref, vp_ref, qn_ref, don_ref, ln_ref, dln_ref,
             dq_ref, dk_ref, dv_ref):
        n = pl.program_id(1)
        band, first_band = _window_masks(n > 0)
        qi = lax.broadcasted_iota(jnp.int32, (ATT_BLK, ATT_BLK), 0)
        ki = lax.broadcasted_iota(jnp.int32, (ATT_BLK, ATT_BLK), 1)
        next_m = jnp.logical_and(ki >= qi, n < nsteps - 1)
        last = slice((rb - 1) * ATT_BLK, rb * ATT_BLK)
        for h in range(ATT_HEADS):
            hs = slice(h * HEAD, (h + 1) * HEAD)
            dk, dv = [None] * rb, [None] * rb
            for j in range(rb):
                rows = slice(j * ATT_BLK, (j + 1) * ATT_BLK)
                q, do = q_ref[rows, hs], do_ref[rows, hs]
                k2, v2 = _two_blocks(k_ref, kp_ref, j, hs), _two_blocks(v_ref, vp_ref, j, hs)
                p = jnp.where(first_band if j == 0 else band,
                              jnp.exp(_dot_nt(q, k2) * scale - _pick(l_ref[rows, :], h)), 0.0)
                ds = _bf(p * (_dot_nt(do, v2) - _pick(dl_ref[rows, :], h)) * scale)
                dq_ref[rows, hs] = _dot(ds, k2).astype(dq_ref.dtype)
                dk2, dv2 = _dot_tn(ds, q), _dot_tn(_bf(p), do)
                if j >= 1:
                    dk[j - 1] = dk[j - 1] + dk2[:ATT_BLK]
                    dv[j - 1] = dv[j - 1] + dv2[:ATT_BLK]
                dk[j], dv[j] = dk2[ATT_BLK:], dv2[ATT_BLK:]
            q, do = qn_ref[:, hs], don_ref[:, hs]
            p = jnp.where(next_m, jnp.exp(_dot_nt(q, k_ref[last, hs]) * scale - _pick(ln_ref[...], h)), 0.0)
            ds = _bf(p * (_dot_nt(do, v_ref[last, hs]) - _pick(dln_ref[...], h)) * scale)
            dk[rb - 1] = dk[rb - 1] + _dot_tn(ds, q)
            dv[rb - 1] = dv[rb - 1] + _dot_tn(_bf(p), do)
            for j in range(rb):
                rows = slice(j * ATT_BLK, (j + 1) * ATT_BLK)
                dk_ref[rows, hs] = dk[j].astype(dk_ref.dtype)
                dv_ref[rows, hs] = dv[j].astype(dv_ref.dtype)

    own = pl.BlockSpec((rb * ATT_BLK, ATT_GW), lambda r, n: (n, r))
    prev = pl.BlockSpec((ATT_BLK, ATT_GW), lambda r, n: (jnp.maximum(n * rb - 1, 0), r))
    nxt = pl.BlockSpec((ATT_BLK, ATT_GW), lambda r, n: (jnp.minimum((n + 1) * rb, nb - 1), r))
    own_head = pl.BlockSpec((rb * ATT_BLK, HEAD), lambda r, n: (n, r))
    nxt_head = pl.BlockSpec((ATT_BLK, HEAD), lambda r, n: (jnp.minimum((n + 1) * rb, nb - 1), r))
    return pl.pallas_call(
        body, grid=(d, nsteps), in_specs=[own] * 4 + [own_head] * 2 + [prev, prev, nxt, nxt, nxt_head, nxt_head],
        out_specs=[own, own, own], out_shape=[jax.ShapeDtypeStruct((length, d * ATT_GW), BF16)] * 3,
        name=name, compiler_params=_params(("parallel", "arbitrary")))(
            qg, kg, vg, dog, lse, delta, kg, vg, qg, dog, lse, delta)


def _rope_tables(t):
    pos = jnp.arange(t, dtype=F32)
    inv = ROPE_THETA ** (-jnp.arange(0, HEAD, 2, dtype=F32) / HEAD)
    ang = pos[:, None] * inv[None, :]
    ang = jnp.concatenate([ang, ang], axis=-1)
    return jnp.cos(ang), jnp.sin(ang)


def _lower_bounds(logits):
    lb = jnp.cumsum(jax.nn.softmax(logits.astype(F32), axis=0), axis=0)
    return lb - lb[0:1]


FFN_ROWS = 256
FF_SHARD = 2 * D_FF // N_CHIPS


def _ffn_in_act(x, g, w_in, name, rider=None):
    t = x.shape[0]

    def body(x_ref, g_ref, w_ref, h_ref, ab_ref, u_ref):
        xv = x_ref[...]
        h = _bf(xv * _rms_rows(xv) * g_ref[...])
        h_ref[...] = h
        for s in range(N_CHIPS // 2):
            cols = slice(s * FF_SHARD, (s + 1) * FF_SHARD)
            a = _dot(h, w_ref[s])
            b = _dot(h, w_ref[s + N_CHIPS // 2])
            ab_ref[:, cols] = a.astype(ab_ref.dtype)
            ab_ref[:, D_FF + s * FF_SHARD:D_FF + (s + 1) * FF_SHARD] = b.astype(ab_ref.dtype)
            u_ref[:, cols] = (a * _sig(a) * b).astype(u_ref.dtype)

    row = lambda w: pl.BlockSpec((FFN_ROWS, w), lambda i: (i, 0))
    return _pcall(
        body, grid=(t // FFN_ROWS,),
        in_specs=[row(D_MODEL), pl.BlockSpec((1, D_MODEL), lambda i: (0, 0)),
                  pl.BlockSpec(w_in.shape, lambda i: (0, 0, 0))],
        out_specs=[row(D_MODEL), row(2 * D_FF), row(D_FF)],
        out_shape=[jax.ShapeDtypeStruct((t, D_MODEL), BF16), jax.ShapeDtypeStruct((t, 2 * D_FF), BF16),
                   jax.ShapeDtypeStruct((t, D_FF), BF16)],
        name=name, sem=("parallel",), args=(x, g, w_in), rider=rider)


def _ffn_bwd_du_act(dx, w_out, ab, name, rider=None):
    t = dx.shape[0]

    def body(dx_ref, w_ref, ab_ref, o_ref):
        du = 0.5 * _dot_nt(_bf(dx_ref[...]), w_ref[0])
        a = ab_ref[:, :D_FF].astype(F32)
        b = ab_ref[:, D_FF:].astype(F32)
        s = _sig(a)
        o_ref[:, :D_FF] = (du * b * (s * (1.0 + a * (1.0 - s)))).astype(o_ref.dtype)
        o_ref[:, D_FF:] = (du * a * s).astype(o_ref.dtype)

    row = lambda w: pl.BlockSpec((FFN_ROWS, w), lambda i: (i, 0))
    return _pcall(
        body, grid=(t // FFN_ROWS,),
        in_specs=[row(D_MODEL), pl.BlockSpec(w_out.shape, lambda i: (0, 0, 0)), row(2 * D_FF)],
        out_specs=row(2 * D_FF), out_shape=jax.ShapeDtypeStruct((t, 2 * D_FF), BF16),
        name=name, sem=("parallel",), args=(dx, w_out, ab), rider=rider)


MIX_ROWS = 512


def _gate_specs():
    return [pl.BlockSpec((MIX_ROWS, 512), lambda i, cb=cb: (i, cb)) for cb in (CB_GA, CB_GA + 1, CB_GB, CB_GB + 1)]


def _gate(lo_ref, hi_ref):
    return _sig(_cat([lo_ref[...], hi_ref[...]]).astype(F32))


def _whole(a):
    return pl.BlockSpec(a.shape, lambda i: (0,) * a.ndim)


def _mix_tail_fwd(oa, ob, proj, x, w_a, w_b, w_o, name):
    t = x.shape[0]

    def body(oa_ref, ob_ref, ga0, ga1, gb0, gb1, x_ref, wa_ref, wb_ref, wo_ref, y_ref, m_ref, ya_ref, yb_ref):
        ya = _dot(oa_ref[...], wa_ref[0])
        yb = _cat([_dot(ob_ref[...], wb_ref[s]) for s in range(N_CHIPS)])
        merged = _bf(_gate(ga0, ga1) * ya + _gate(gb0, gb1) * yb)
        m_ref[...] = merged
        ya_ref[...] = ya.astype(ya_ref.dtype)
        yb_ref[...] = yb.astype(yb_ref.dtype)
        y_ref[...] = x_ref[...] + _dot(merged, wo_ref[0])

    row = lambda w: pl.BlockSpec((MIX_ROWS, w), lambda i: (i, 0))
    return pl.pallas_call(
        body, grid=(t // MIX_ROWS,),
        in_specs=[row(D_MODEL), row(ATT_GW)] + _gate_specs() + [row(D_MODEL), _whole(w_a), _whole(w_b), _whole(w_o)],
        out_specs=[row(D_MODEL)] * 4,
        out_shape=[jax.ShapeDtypeStruct((t, D_MODEL), F32)] + [jax.ShapeDtypeStruct((t, D_MODEL), BF16)] * 3,
        name=name, compiler_params=_params(("parallel",)))(oa, ob, proj, proj, proj, proj, x, w_a, w_b, w_o)


def _mix_tail_bwd(dx, proj, ya, yb, w_a, w_b, w_o, name):
    t = dx.shape[0]
    shard = D_MODEL // N_CHIPS

    def body(dx_ref, ga0, ga1, gb0, gb1, ya_ref, yb_ref, wa_ref, wb_ref, wo_ref, dya_ref, dyb_ref, dg_ref, doa_ref, dob_ref):
        dm = _dot_nt(_bf(dx_ref[...]), wo_ref[0])
        sa, sb = _gate(ga0, ga1), _gate(gb0, gb1)
        dya, dyb = _bf(dm * sa), _bf(dm * sb)
        dya_ref[...] = dya
        dyb_ref[...] = dyb
        dg_ref[:, :D_MODEL] = (dm * ya_ref[...].astype(F32) * sa * (1.0 - sa)).astype(dg_ref.dtype)
        dg_ref[:, D_MODEL:] = (dm * yb_ref[...].astype(F32) * sb * (1.0 - sb)).astype(dg_ref.dtype)
        doa_ref[...] = _dot_nt(dya, wa_ref[0])
        dob = _dot_nt(dyb[:, :shard], wb_ref[0])
        for s in range(1, N_CHIPS):
            dob = dob + _dot_nt(dyb[:, s * shard:(s + 1) * shard], wb_ref[s])
        dob_ref[...] = dob

    row = lambda w: pl.BlockSpec((MIX_ROWS, w), lambda i: (i, 0))
    return pl.pallas_call(
        body, grid=(t // MIX_ROWS,),
        in_specs=[row(D_MODEL)] + _gate_specs() + [row(D_MODEL), row(D_MODEL), _whole(w_a), _whole(w_b), _whole(w_o)],
        out_specs=[row(D_MODEL), row(D_MODEL), row(2 * D_MODEL), row(D_MODEL), row(ATT_GW)],
        out_shape=[jax.ShapeDtypeStruct((t, D_MODEL), BF16), jax.ShapeDtypeStruct((t, D_MODEL), BF16),
                   jax.ShapeDtypeStruct((t, 2 * D_MODEL), BF16), jax.ShapeDtypeStruct((t, D_MODEL), F32),
                   jax.ShapeDtypeStruct((t, ATT_GW), F32)],
        name=name, compiler_params=_params(("parallel",)))(dx, proj, proj, proj, proj, ya, yb, w_a, w_b, w_o)


def _ffn_fwd(x, g, src, l, pre):
    tag = f"l{l}_{pre}"
    w_in = src.weight(l, pre + "_w_in")
    h, ab, u = _ffn_in_act(x, g, w_in, name=tag + "_in_act", rider=src.ride(tag + "_in_act"))
    w_out = src.weight(l, pre + "_w_out")
    y = _mm_nn(u, w_out, name=tag + "_out", tm=512, tn=D_MODEL, out_dtype=F32, res=x, alpha=0.5, rider=src.ride(tag + "_out"))
    return y, (x, h, ab, u, w_in, w_out)


def _ffn_bwd(dx, saved, g, src, l, pre):
    tag = f"l{l}_{pre}"
    x, h, ab, u, w_in, w_out = saved
    g_out = _mm_tn(u, dx, nb=1, name=tag + "_bwd_wout", tm=1024, tk=1408, tn=D_MODEL, alpha=0.5, rider=src.ride(tag + "_bwd_wout"))
    src.grads(l, {pre + "_w_out": g_out.reshape(N_CHIPS, D_FF // N_CHIPS, D_MODEL)})
    dab = _ffn_bwd_du_act(dx, w_out, ab, name=tag + "_bwd_du_act", rider=src.ride(tag + "_bwd_du_act"))
    g_in = _mm_tn(h, dab, nb=N_CHIPS, name=tag + "_bwd_win", tm=2048, tk=D_MODEL, tn=FF_SHARD, rider=src.ride(tag + "_bwd_win"))
    src.grads(l, {pre + "_w_in": g_in})
    return _mm_nt(dab, w_in, name=tag + "_bwd_dh", tm=1024, tp=D_MODEL, tn=FF_SHARD, out_dtype=F32, rider=src.ride(tag + "_bwd_dh"),
                  norm=(x, g, dx))


def _mix_fwd(x, small, lb, cos, sin, src, l):
    tag = f"l{l}_mix"
    w = {}
    h = _norm_fwd(x, small["mix_norm"], name=tag + "_norm")
    w["w_in"] = src.weight(l, "w_in")
    proj = _mm_nn(h, w["w_in"], name=tag + "_in", tm=2048, tn=896, out_dtype=BF16, rider=src.ride(tag + "_in"))
    hf = _mm_nn(h, w["w_in"][0:1, :, D_MODEL:2 * D_MODEL], name=tag + "_hf", tm=1024, tn=D_MODEL, out_dtype=F32)
    oscan, oa, sall = _hgrn_fwd(proj, hf, lb, small["hgrn_out_norm"], name=tag + "_hgrn", rider=src.ride(tag + "_hgrn"))
    qk = _qk_fwd(proj, cos, sin, small["attn_q_norm"], small["attn_k_norm"], name=tag + "_qk")
    outs, lses = [], []
    for g in range(ATT_GROUPS):
        o, lse = _attn_fwd(qk[g], qk[3 + g], qk[6 + g], g, name=f"{tag}_attn{g}")
        outs.append(o)
        lses.append(lse)
    ob = _merge_fwd(outs, lses, name=tag + "_merge")
    w.update({n: src.weight(l, n) for n in ("w_branch_a", "w_branch_b", "w_out")})
    y, merged, ya, yb = _mix_tail_fwd(oa, ob, proj, x, w["w_branch_a"], w["w_branch_b"], w["w_out"], name=tag + "_tail")
    return y, (x, h, proj, hf, oscan, oa, sall, qk, outs, lses, ob, ya, yb, merged, w)


def _mix_bwd(dx, saved, small, lb, cos, sin, src, l, lb_live):
    tag = f"l{l}_mix"
    x, h, proj, hf, oscan, oa, sall, qk, outs, lses, ob, ya, yb, merged, w = saved
    g_wout = _mm_tn(merged, dx, nb=1, name=tag + "_bwd_wout", tm=1024, tk=D_MODEL, tn=D_MODEL)
    dya, dyb, dgab, doa, dob = _mix_tail_bwd(dx, proj, ya, yb, w["w_branch_a"], w["w_branch_b"], w["w_out"], name=tag + "_bwd_tail")
    g_wa = _mm_tn(oa, dya, nb=1, name=tag + "_bwd_wa", tm=1024, tk=D_MODEL, tn=D_MODEL)
    g_wb = _mm_tn(ob, dyb, nb=N_CHIPS, name=tag + "_bwd_wb", tm=2048, tk=ATT_GW, tn=256)
    mb = _merge_bwd(dob, outs, lses, name=tag + "_bwd_merge")
    dqk, dvs = [None] * 6, []
    for g in range(ATT_GROUPS):
        dq, dk, dv = _attn_bwd(qk[g], qk[3 + g], qk[6 + g], mb[g], lses[g], mb[3 + g], g, name=f"{tag}_bwd_attn{g}")
        dqk[g], dqk[3 + g] = dq, dk
        dvs.append(dv)
    dqk_cols, dqn, dkn = _qk_bwd(dqk, proj, cos, sin, small["attn_q_norm"], small["attn_k_norm"], name=tag + "_bwd_qk")
    dproj, dgn, dlb = _hgrn_bwd(doa, oscan, proj, hf, sall, lb, small["hgrn_out_norm"], dqk_cols, dvs, dgab,
                                name=tag + "_bwd_hgrn", precise=lb_live, rider=src.ride(tag + "_bwd_hgrn"))
    src.grads(l, dict(w_branch_a=g_wa.reshape(N_CHIPS, D_MODEL // N_CHIPS, D_MODEL), w_branch_b=g_wb,
                      w_out=g_wout.reshape(N_CHIPS, D_MODEL // N_CHIPS, D_MODEL)))
    g_win = _mm_tn(h, dproj, nb=N_CHIPS, name=tag + "_bwd_win", tm=1024, tk=D_MODEL, tn=2688, rider=src.ride(tag + "_bwd_win"))
    src.grads(l, dict(w_in=g_win))
    dx, dg = _mm_nt(dproj, w["w_in"], name=tag + "_bwd_dh", tm=1024, tp=D_MODEL, tn=2688, out_dtype=F32,
                    rider=src.ride(tag + "_bwd_dh"), norm=(x, small["mix_norm"], dx))
    return dx, dict(mix_norm=dg, hgrn_out_norm=dgn, lb=dlb, attn_q_norm=dqn, attn_k_norm=dkn)


BIG = ("ffn1_w_in", "ffn1_w_out", "w_in", "w_branch_a", "w_branch_b", "w_out", "ffn2_w_in", "ffn2_w_out")
ROW_SHARDED = ("ffn1_w_out", "w_branch_a", "w_out", "ffn2_w_out")
SMALL = ("ffn1_norm", "mix_norm", "hgrn_lb_logits", "hgrn_out_norm", "attn_q_norm", "attn_k_norm", "ffn2_norm")
WEIGHTS = ("ffn1_norm", "ffn1_w_in", "ffn1_w_out", "mix_norm", "w_in", "hgrn_lb_logits", "hgrn_out_norm", "attn_q_norm",
           "attn_k_norm", "w_branch_a", "w_branch_b", "w_out", "ffn2_norm", "ffn2_w_in", "ffn2_w_out")
SMALL_ROWS = 8


def _matmul_ready(name, a):
    return a.reshape(1, a.shape[0] * a.shape[1], a.shape[2]) if name in ROW_SHARDED else a


def _layer_small(small, l):
    s = {n: small[n][l].reshape(1, D_MODEL) for n in ("ffn1_norm", "mix_norm", "hgrn_out_norm", "ffn2_norm")}
    s.update({n: small[n][l] for n in ("attn_q_norm", "attn_k_norm")})
    return s


def _local_step(x, target, small, src):
    t = x.shape[0]
    cos, sin = _rope_tables(t)
    lbs = _lower_bounds(small["hgrn_lb_logits"])
    saved = []
    for l in range(2):
        sm = _layer_small(small, l)
        lb = lbs[l].reshape(1, D_MODEL)
        x, s1 = _ffn_fwd(x, sm["ffn1_norm"], src, l, "ffn1")
        x, s2 = _mix_fwd(x, sm, lb, cos, sin, src, l)
        x, s3 = _ffn_fwd(x, sm["ffn2_norm"], src, l, "ffn2")
        saved.append((sm, lb, s1, s2, s3))
    dx, sq = _loss_fwd_bwd(x, target, name="loss")
    small_rows = [None, None]
    for l in (1, 0):
        sm, lb, s1, s2, s3 = saved[l]
        dx, dg2 = _ffn_bwd(dx, s3, sm["ffn2_norm"], src, l, "ffn2")
        dx, g = _mix_bwd(dx, s2, sm, lb, cos, sin, src, l, lb_live=l > 0)
        dx, dg1 = _ffn_bwd(dx, s1, sm["ffn1_norm"], src, l, "ffn1")
        pad = lambda a: jnp.pad(a[:ATT_GROUPS].reshape(1, ATT_GROUPS * HEAD), ((0, 0), (0, D_MODEL - ATT_GROUPS * HEAD)))
        small_rows[l] = jnp.concatenate(
            [dg1, g["mix_norm"], g["lb"], g["hgrn_out_norm"], pad(g["attn_q_norm"]), pad(g["attn_k_norm"]), dg2,
             jnp.zeros((SMALL_ROWS - 7, D_MODEL), F32)], axis=0)
    return jnp.sum(sq), dx, jnp.concatenate(small_rows, axis=0)


def _coords():
    return lax.axis_index("x"), lax.axis_index("y"), lax.axis_index("c")


def _other_chips(x, y):
    return [(1 - x, y), (x, 1 - y), (1 - x, 1 - y)]


def _half_rows(rows, which):
    return pl.ds(which * (rows // 2), rows // 2)


def _gather_rider(shards):
    n = len(shards)

    def copies(w, full, sems):
        send, recv, fsend, frecv, osend, orecv = sems
        x, y, c = _coords()
        slot = 2 * x + y
        chips = _other_chips(x, y)

        def copy(i, j, blk, src, pair, to):
            return pltpu.make_async_remote_copy(src_ref=src, dst_ref=blk, send_sem=pair[0].at[i * 3 + j],
                                                recv_sem=pair[1].at[i * 3 + j], device_id=to, device_id_type=MESH)

        def block(i, chip_slot, core):
            return full[i].at[chip_slot, _half_rows(shards[i].shape[0], core)]

        pairs = [(i, j, chip) for i in range(n) for j, chip in enumerate(chips)]

        def first():
            return [copy(i, j, block(i, slot, c), w[i].at[_half_rows(shards[i].shape[0], c)], (send, recv), (*chip, c))
                    for i, j, chip in pairs]

        def landed(core, pair):
            return [copy(i, j, block(i, 2 * chip[0] + chip[1], core), block(i, 2 * chip[0] + chip[1], core), pair, (x, y, 1 - c))
                    for i, j, chip in pairs]

        def own():
            return [pltpu.make_async_remote_copy(src_ref=w[i], dst_ref=full[i].at[slot], send_sem=osend.at[i],
                                                 recv_sem=orecv.at[i], device_id=(x, y, 1 - c), device_id_type=MESH)
                    for i in range(n)]

        return first, landed, own

    def begin(w, full, sems):
        first, _, own = copies(w, full, sems)
        for cp in first() + own():
            cp.start()

    def end(w, full, sems):
        first, landed, own = copies(w, full, sems)
        forwards = landed(lax.axis_index("c"), sems[2:4])
        for arrival, forward in zip(landed(lax.axis_index("c"), sems[:2]), forwards):
            arrival.wait_recv()
            forward.start()
        for cp in landed(1 - lax.axis_index("c"), sems[2:4]) + own():
            cp.wait_recv()
        for cp in first() + forwards + own():
            cp.wait_send()

    out_shape = [jax.ShapeDtypeStruct((N_CHIPS,) + s.shape, s.dtype) for s in shards]
    sems = [pltpu.SemaphoreType.DMA((3 * n,))] * 4 + [pltpu.SemaphoreType.DMA((n,))] * 2
    return _Rider(shards, out_shape, sems, begin, end)


N_RECV = 7


def _scatter_rider(parts):
    n = len(parts)

    def copies(p, out, sems):
        send, recv = sems
        x, y, c = _coords()
        slot = 2 * x + y
        chips = _other_chips(x, y)

        def arrivals():
            return [pltpu.make_async_remote_copy(
                src_ref=out[i].at[k], dst_ref=out[i].at[k], send_sem=send.at[0], recv_sem=recv.at[i * N_RECV + k],
                device_id=(x, y, c), device_id_type=MESH) for i in range(n) for k in range(N_RECV)]

        sends = []
        for i in range(n):
            rows = parts[i].shape[1]
            for j, chip in enumerate(chips):
                for core in (0, 1):
                    sends.append(pltpu.make_async_remote_copy(
                        src_ref=p[i].at[2 * chip[0] + chip[1], _half_rows(rows, core)], dst_ref=out[i].at[2 * j + c],
                        send_sem=send.at[i * N_RECV + 2 * j + core], recv_sem=recv.at[i * N_RECV + 2 * j + c],
                        device_id=(*chip, core), device_id_type=MESH))
            sends.append(pltpu.make_async_remote_copy(
                src_ref=p[i].at[slot, _half_rows(rows, 1 - c)], dst_ref=out[i].at[6], send_sem=send.at[i * N_RECV + 6],
                recv_sem=recv.at[i * N_RECV + 6], device_id=(x, y, 1 - c), device_id_type=MESH))
        return sends, arrivals

    def begin(p, out, sems):
        for cp in copies(p, out, sems)[0]:
            cp.start()

    def end(p, out, sems):
        sends, arrivals = copies(p, out, sems)
        for cp in arrivals():
            cp.wait_recv()
        for cp in sends:
            cp.wait_send()

    out_shape = [jax.ShapeDtypeStruct((N_RECV, a.shape[1] // 2, a.shape[2]), a.dtype) for a in parts]
    return _Rider(parts, out_shape, [pltpu.SemaphoreType.DMA((N_RECV * n,))] * 2, begin, end)


def _run_alone(rider, name):
    _pcall(lambda: None, grid=(), in_specs=[], out_specs=[], out_shape=[], name=name, sem=(), args=(), rider=rider)
    return rider.result


def _sum_partials(own, parts, name):
    r, wd = own.shape
    tm = next(t for t in (256, 128, 64, 32, 16) if r % t == 0)

    def body(own_ref, p_ref, o_ref):
        acc = own_ref[...].astype(F32)
        for k in range(N_RECV):
            acc = acc + p_ref[k].astype(F32)
        o_ref[...] = acc

    return pl.pallas_call(
        body, grid=(r // tm,),
        in_specs=[pl.BlockSpec((tm, wd), lambda i: (i, 0)), pl.BlockSpec((N_RECV, tm, wd), lambda i: (0, i, 0))],
        out_specs=pl.BlockSpec((tm, wd), lambda i: (i, 0)), out_shape=jax.ShapeDtypeStruct((r, wd), F32),
        name=name, compiler_params=_params(("parallel",)))(own, parts)


def _exchange_halves(reduced, name):
    n = len(reduced)

    def body(*refs):
        r, out = refs[:n], refs[n:2 * n]
        send, recv = refs[2 * n:]
        x, y, c = _coords()
        sib = [pltpu.make_async_remote_copy(src_ref=r[i], dst_ref=out[i], send_sem=send.at[i], recv_sem=recv.at[i],
                                            device_id=(x, y, 1 - c), device_id_type=MESH) for i in range(n)]
        for cp in sib:
            cp.start()
        for cp in sib:
            cp.wait_recv()
        for cp in sib:
            cp.wait_send()

    out_shape = [jax.ShapeDtypeStruct(a.shape, a.dtype) for a in reduced]
    return pl.pallas_call(body, in_specs=[ANY] * n, out_specs=[ANY] * n, out_shape=out_shape,
                          scratch_shapes=[pltpu.SemaphoreType.DMA((n,))] * 2, name=name)(*reduced)


def _reduce_finish(parts, recv, tag):
    x, y, c = _coords()
    slot = 2 * x + y
    halves = []
    for i, (p, r) in enumerate(zip(parts, recv)):
        half = p.shape[1] // 2
        own = lax.dynamic_slice(p, (slot, c * half, 0), (1, half, p.shape[2]))[0]
        halves.append(_sum_partials(own, r, name=f"{tag}_sum{i}"))
    theirs = _exchange_halves(halves, name=tag + "_exchange")
    return [jnp.where(c == 0, jnp.concatenate([h, t], axis=0), jnp.concatenate([t, h], axis=0)) for h, t in zip(halves, theirs)]


GATHER_RIDES = {
    "l0_ffn1_in_act": ((0, "w_in"),),
    "l0_ffn1_out": ((0, "w_branch_a"), (0, "w_branch_b"), (0, "w_out")),
    "l0_mix_in": ((0, "ffn2_w_in"), (0, "ffn2_w_out"), (1, "ffn1_w_in"), (1, "ffn1_w_out")),
    "l0_mix_hgrn": ((1, "w_in"), (1, "w_branch_a"), (1, "w_branch_b"), (1, "w_out")),
    "l0_ffn2_in_act": ((1, "ffn2_w_in"), (1, "ffn2_w_out")),
}
ALONE_FIRST = ((0, "ffn1_w_in"), (0, "ffn1_w_out"))
SCATTER_RIDES = {
    "l1_mix_bwd_hgrn": ((1, "ffn2_w_in"), (1, "ffn2_w_out")),
    "l0_ffn2_bwd_win": ((1, "ffn1_w_in"),),
    "l0_ffn2_bwd_dh": ((1, "ffn1_w_out"), (1, "w_branch_a"), (1, "w_branch_b"), (1, "w_out")),
    "l0_mix_bwd_hgrn": ((1, "w_in"), (0, "ffn2_w_out")),
    "l0_mix_bwd_win": ((0, "ffn2_w_in"),),
    "l0_mix_bwd_dh": ((0, "w_in"),),
    "l0_ffn1_bwd_wout": ((0, "w_branch_a"), (0, "w_branch_b"), (0, "w_out")),
    "l0_ffn1_bwd_du_act": ((0, "ffn1_w_out"),),
    "l0_ffn1_bwd_dh": ((0, "ffn1_w_in"),),
}


class _Exchange:
    def __init__(self, shards):
        self.shards = shards
        self.pending = []
        self.full = {}
        self.parts = {}
        self.recv = {}

    def _gather(self, keys):
        return _gather_rider([self.shards[n][l] for l, n in keys]), "gather", list(keys)

    def _scatter(self, keys):
        return _scatter_rider([self.parts[k] for k in keys]), "scatter", list(keys)

    def _unpack(self):
        waiting = []
        for rider, kind, keys in self.pending:
            if rider.result is None:
                waiting.append((rider, kind, keys))
            elif kind == "gather":
                self.full.update(zip(keys, rider.result))
            else:
                self.recv.update(zip(keys, rider.result))
        self.pending = waiting

    def ride(self, host):
        if host in GATHER_RIDES:
            self.pending.append(self._gather(GATHER_RIDES[host]))
        elif host in SCATTER_RIDES:
            self.pending.append(self._scatter(SCATTER_RIDES[host]))
        else:
            return None
        return self.pending[-1][0]

    def weight(self, l, name):
        self._unpack()
        if (l, name) not in self.full:
            assert (l, name) in ALONE_FIRST, (l, name)
            job = self._gather(ALONE_FIRST)
            _run_alone(job[0], name="gather_first")
            self.pending.append(job)
            self._unpack()
        return _matmul_ready(name, self.full[(l, name)])

    def grads(self, l, partials):
        self.parts.update({(l, n): a for n, a in partials.items()})

    def reduce(self):
        self._unpack()
        assert not self.pending and set(self.recv) == set(self.parts)
        out = {}
        for l in range(2):
            done = _reduce_finish([self.parts[(l, n)] for n in BIG], [self.recv[(l, n)] for n in BIG], f"reduce_l{l}")
            out[l] = dict(zip(BIG, done))
        return {n: jnp.stack([out[0][n], out[1][n]], axis=0) for n in BIG}


def _all_reduce_small(rows):
    r = rows.shape[0]

    def body(x_ref, o_ref, buf, send, recv):
        x, y, c = _coords()
        me = 4 * x + 2 * y + c
        buf[me] = x_ref[...]
        copies = []
        for k in range(1, 8):
            peer = (x ^ (k >> 2), y ^ ((k >> 1) & 1), c ^ (k & 1))
            cp = pltpu.make_async_remote_copy(src_ref=x_ref, dst_ref=buf.at[me], send_sem=send.at[k - 1], recv_sem=recv.at[me],
                                              device_id=peer, device_id_type=MESH)
            cp.start()
            copies.append(cp)
        for k in range(1, 8):
            src = 4 * (x ^ (k >> 2)) + 2 * (y ^ ((k >> 1) & 1)) + (c ^ (k & 1))
            pltpu.make_async_remote_copy(src_ref=x_ref, dst_ref=buf.at[src], send_sem=send.at[0], recv_sem=recv.at[src],
                                         device_id=(x, y, c), device_id_type=MESH).wait_recv()
        for cp in copies:
            cp.wait_send()
        acc = buf[0]
        for k in range(1, 8):
            acc = acc + buf[k]
        o_ref[...] = acc

    vm = pl.BlockSpec(memory_space=pltpu.VMEM)
    return pl.pallas_call(
        body, in_specs=[vm], out_specs=vm, out_shape=jax.ShapeDtypeStruct(rows.shape, F32),
        scratch_shapes=[pltpu.VMEM((8, r, D_MODEL), F32), pltpu.SemaphoreType.DMA((7,)), pltpu.SemaphoreType.DMA((8,))],
        name="all_reduce_small")(rows)


def _adamw_math(w, g, m, v):
    m = ADAM_B1 * m + (1.0 - ADAM_B1) * g
    v = ADAM_B2 * v + (1.0 - ADAM_B2) * (g * g)
    m_hat = m / (1.0 - ADAM_B1 ** ADAM_STEP)
    v_hat = v / (1.0 - ADAM_B2 ** ADAM_STEP)
    return -ADAM_LR * (m_hat / (jnp.sqrt(v_hat) + ADAM_EPS) + ADAM_WD * w), m, v


def _adamw(w, g, m, v, name):
    shape = w.shape
    cols = shape[-1]
    flat = lambda a: a.reshape(-1, cols)
    rows = flat(w).shape[0]
    tm = 128 if rows % 128 == 0 else rows
    ins = [('t', flat(a), cols, 0) for a in (w, g, m, v)]
    res = _ew(_adamw_math, ins, [('t', cols, F32)] * 3, rows=rows, tm=tm, name=name)
    return [a.reshape(shape) for a in res]


def _small_update(sums, logits, w, m, v):
    def body(s_ref, lg_ref, w_ref, m_ref, v_ref, g_ref, d_ref, nm_ref, nv_ref):
        s = s_ref[...]
        l0, l1 = lg_ref[0:1, :], lg_ref[1:2, :]
        mx = jnp.maximum(l0, l1)
        e0, e1 = jnp.exp(l0 - mx), jnp.exp(l1 - mx)
        sm0, sm1 = e0 / (e0 + e1), e1 / (e0 + e1)
        dl1 = s_ref[SMALL_ROWS + 2:SMALL_ROWS + 3, :] * sm0 * sm1
        row = lax.broadcasted_iota(jnp.int32, s.shape, 0)
        g = jnp.where(row == 2, -dl1, jnp.where(row == SMALL_ROWS + 2, dl1, s))
        d, nm, nv = _adamw_math(w_ref[...], g, m_ref[...], v_ref[...])
        g_ref[...] = g
        d_ref[...] = d
        nm_ref[...] = nm
        nv_ref[...] = nv

    vm = pl.BlockSpec(memory_space=pltpu.VMEM)
    return pl.pallas_call(body, in_specs=[vm] * 5, out_specs=[vm] * 4,
                          out_shape=[jax.ShapeDtypeStruct(sums.shape, F32)] * 4, name="small_update")(sums, logits, w, m, v)


def _pack_small(vals):
    rows = []
    for l in range(2):
        for n in ("ffn1_norm", "mix_norm", "hgrn_lb_logits", "hgrn_out_norm", "attn_q_norm", "attn_k_norm", "ffn2_norm"):
            a = vals[n][l].reshape(1, -1)
            rows.append(jnp.pad(a, ((0, 0), (0, D_MODEL - a.shape[1]))))
        rows.append(jnp.zeros((SMALL_ROWS - 7, D_MODEL), F32))
    return jnp.concatenate(rows, axis=0)


def _unpack_small(packed):
    out = {}
    for k, n in enumerate(("ffn1_norm", "mix_norm", "hgrn_lb_logits", "hgrn_out_norm", "attn_q_norm", "attn_k_norm", "ffn2_norm")):
        a = jnp.stack([packed[k], packed[SMALL_ROWS + k]], axis=0)
        out[n] = a[:, :ATT_GROUPS * HEAD].reshape(2, ATT_GROUPS, HEAD) if n.startswith("attn") else a
    return out


def kernel(x, ffn1_norm, ffn1_w_in, ffn1_w_out, mix_norm, w_in, hgrn_lb_logits, hgrn_out_norm, attn_q_norm, attn_k_norm, w_branch_a, w_branch_b, w_out, ffn2_norm, ffn2_w_in, ffn2_w_out, loss_target, m_ffn1_norm, m_ffn1_w_in, m_ffn1_w_out, m_mix_norm, m_w_in, m_hgrn_lb_logits, m_hgrn_out_norm, m_attn_q_norm, m_attn_k_norm, m_w_branch_a, m_w_branch_b, m_w_out, m_ffn2_norm, m_ffn2_w_in, m_ffn2_w_out, v_ffn1_norm, v_ffn1_w_in, v_ffn1_w_out, v_mix_norm, v_w_in, v_hgrn_lb_logits, v_hgrn_out_norm, v_attn_q_norm, v_attn_k_norm, v_w_branch_a, v_w_branch_b, v_w_out, v_ffn2_norm, v_ffn2_w_in, v_ffn2_w_out):
    a = locals()
    w = {n: a[n] for n in WEIGHTS}
    m = {n: a["m_" + n] for n in WEIGHTS}
    v = {n: a["v_" + n] for n in WEIGHTS}

    exchange = _Exchange({n: w[n].astype(BF16) for n in BIG})
    small = {n: w[n] for n in SMALL}
    sq, grad_x, small_rows = _local_step(x[0], loss_target[0], small, exchange)
    loss = lax.psum(sq, ("x", "y", "c")) * (0.5 / D_MODEL)
    grads = exchange.reduce()

    sums = _all_reduce_small(small_rows)
    g_s, d_s, m_s, v_s = _small_update(sums, w["hgrn_lb_logits"], _pack_small(small), _pack_small({n: m[n] for n in SMALL}),
                                       _pack_small({n: v[n] for n in SMALL}))
    grads.update(_unpack_small(g_s))
    delta, new_m, new_v = _unpack_small(d_s), _unpack_small(m_s), _unpack_small(v_s)
    for n in BIG:
        delta[n], new_m[n], new_v[n] = _adamw(w[n], grads[n], m[n], v[n], name="adamw_" + n)

    return (loss, grad_x[None], *[grads[n] for n in WEIGHTS], *[delta[n] for n in WEIGHTS],
            *[new_m[n] for n in WEIGHTS], *[new_v[n] for n in WEIGHTS])
```

```python
import functools

import jax
import jax.numpy as jnp
from jax import lax
from jax.experimental import pallas as pl
from jax.experimental.pallas import tpu as pltpu

F32 = jnp.float32
BF16 = jnp.bfloat16
MESH = pl.DeviceIdType.MESH

D_MODEL = 1024
D_FF = 2816
N_CHIPS = 4
HEAD = 128
HG_HEADS = 8
HG_CHUNK = 64
ATT_GROUPS = 3
ATT_HEADS = 4
ATT_GW = ATT_HEADS * HEAD
DILATIONS = (1, 4, 16)
ATT_BLK = 128
ATT_STEP_BLOCKS = 8
P_IN = 10752
CB_AQ, CB_AK, CB_AV, CB_GA, CB_GB = 8, 11, 14, 17, 19
EPS = 1e-6
ROPE_THETA = 10000.0
ADAM_LR, ADAM_B1, ADAM_B2, ADAM_EPS, ADAM_WD, ADAM_STEP = 0.001, 0.9, 0.999, 1e-08, 0.01, 10
VMEM_LIMIT_V7X = 56 * 1024 * 1024
NEG = -1e30


def _params(sem):
    return pltpu.CompilerParams(dimension_semantics=sem, vmem_limit_bytes=VMEM_LIMIT_V7X)


def _sig(x):
    return 1.0 / (1.0 + jnp.exp(-x))


def _dot(a, b):
    return jnp.dot(a, b, preferred_element_type=F32)


def _dot_nt(a, b):
    return lax.dot_general(a, b, (((1,), (1,)), ((), ())), preferred_element_type=F32)


def _dot_tn(a, b):
    return lax.dot_general(a, b, (((0,), (0,)), ((), ())), preferred_element_type=F32)


def _bf(x):
    return x.astype(BF16)


ANY = pl.BlockSpec(memory_space=pl.ANY)


class _Rider:
    def __init__(self, args, out_shape, sems, begin, end):
        self.args, self.out_shape, self.sems, self.begin, self.end = list(args), list(out_shape), list(sems), begin, end
        self.result = None


def _pcall(body, *, grid, in_specs, out_specs, out_shape, name, sem, args, scratch_shapes=(), rider=None):
    multi = isinstance(out_shape, (list, tuple))
    o_specs = list(out_specs) if multi else [out_specs]
    o_shape = list(out_shape) if multi else [out_shape]
    if rider is None:
        res = pl.pallas_call(body, grid=grid, in_specs=list(in_specs), out_specs=o_specs, out_shape=o_shape,
                             scratch_shapes=list(scratch_shapes), name=name, compiler_params=_params(sem))(*args)
        return list(res) if multi else res[0]
    counts = [len(in_specs), len(rider.args), len(o_specs), len(rider.out_shape), len(scratch_shapes)]

    def wrapped(*refs):
        groups, at = [], 0
        for c in counts:
            groups.append(refs[at:at + c])
            at += c
        h_in, r_in, h_out, r_out, h_scratch = groups
        r_sems = refs[at:]
        if grid:
            ids = [pl.program_id(a) for a in range(len(grid))]
            first = functools.reduce(jnp.logical_and, [i == 0 for i in ids])
            last = functools.reduce(jnp.logical_and, [i == g - 1 for i, g in zip(ids, grid)])
            pl.when(first)(lambda: rider.begin(r_in, r_out, r_sems))
            body(*h_in, *h_out, *h_scratch)
            pl.when(last)(lambda: rider.end(r_in, r_out, r_sems))
        else:
            rider.begin(r_in, r_out, r_sems)
            body(*h_in, *h_out, *h_scratch)
            rider.end(r_in, r_out, r_sems)

    res = pl.pallas_call(
        wrapped, grid=grid, in_specs=list(in_specs) + [ANY] * counts[1], out_specs=o_specs + [ANY] * counts[3],
        out_shape=o_shape + rider.out_shape, scratch_shapes=list(scratch_shapes) + rider.sems, name=name,
        compiler_params=_params(("arbitrary",) * len(grid)))(*args, *rider.args)
    rider.result = list(res[counts[2]:])
    return list(res[:counts[2]]) if multi else res[0]


def _mm_nn(a, b3, *, name, tm, tn, out_dtype, res=None, alpha=1.0, rider=None):
    m, k = a.shape
    nb, _, nw = b3.shape
    per = nw // tn
    assert nw % tn == 0 and m % tm == 0
    has_res = res is not None

    def body(*refs):
        if has_res:
            a_ref, b_ref, r_ref, o_ref = refs
        else:
            a_ref, b_ref, o_ref = refs
        acc = _dot(_bf(a_ref[...]), b_ref[...])
        if alpha != 1.0:
            acc = alpha * acc
        if has_res:
            acc = r_ref[...] + acc
        o_ref[...] = acc.astype(o_ref.dtype)

    in_specs = [pl.BlockSpec((tm, k), lambda i, j: (i, 0)),
                pl.BlockSpec((None, k, tn), lambda i, j: (j // per, 0, j % per))]
    args = [a, b3]
    if has_res:
        in_specs.append(pl.BlockSpec((tm, tn), lambda i, j: (i, j)))
        args.append(res)
    return _pcall(body, grid=(m // tm, nb * per), in_specs=in_specs, out_specs=pl.BlockSpec((tm, tn), lambda i, j: (i, j)),
                  out_shape=jax.ShapeDtypeStruct((m, nb * nw), out_dtype), name=name, sem=("parallel", "arbitrary"),
                  args=args, rider=rider)


def _mm_nt(d, b3, *, name, tm, tp, tn, out_dtype, alpha=1.0, rider=None, norm=None):
    m, n = d.shape
    nb, p, nw = b3.shape
    per = nw // tn
    nk = n // tn
    assert nb * nw == n and nw % tn == 0 and p % tp == 0 and m % tm == 0 and (norm is None or tp == p)

    def body(d_ref, b_ref, *refs):
        kk = pl.program_id(2)
        acc_ref = refs[-1]

        @pl.when(kk == 0)
        def _():
            acc_ref[...] = jnp.zeros_like(acc_ref)

        acc_ref[...] += _dot_nt(_bf(d_ref[...]), b_ref[...])

        if norm is None:
            @pl.when(kk == nk - 1)
            def _():
                refs[0][...] = (alpha * acc_ref[...]).astype(refs[0].dtype)
        else:
            x_ref, g_ref, dx_ref, o_ref, dg_ref = refs[:5]

            @pl.when(jnp.logical_and(pl.program_id(0) == 0, kk == 0))
            def _():
                dg_ref[...] = jnp.zeros_like(dg_ref)

            @pl.when(kk == nk - 1)
            def _():
                dh = alpha * acc_ref[...]
                xv = x_ref[...]
                r = _rms_rows(xv)
                xh = xv * r
                dxh = dh * g_ref[...]
                o_ref[...] = dx_ref[...] + r * (dxh - xh * jnp.mean(dxh * xh, axis=1, keepdims=True))
                dg_ref[...] += jnp.sum(dh * xh, axis=0, keepdims=True)

    in_specs = [pl.BlockSpec((tm, tn), lambda i, j, kk: (i, kk)),
                pl.BlockSpec((None, tp, tn), lambda i, j, kk: (kk // per, j, kk % per))]
    tile = pl.BlockSpec((tm, tp), lambda i, j, kk: (i, j))
    if norm is None:
        return _pcall(body, grid=(m // tm, p // tp, nk), in_specs=in_specs, out_specs=tile,
                      out_shape=jax.ShapeDtypeStruct((m, p), out_dtype), scratch_shapes=[pltpu.VMEM((tm, tp), F32)],
                      name=name, sem=("parallel", "parallel", "arbitrary"), args=(d, b3), rider=rider)
    x, g, dx = norm
    row = pl.BlockSpec((1, p), lambda i, j, kk: (0, 0))
    return _pcall(body, grid=(m // tm, 1, nk), in_specs=in_specs + [tile, row, tile], out_specs=[tile, row],
                  out_shape=[jax.ShapeDtypeStruct((m, p), F32), jax.ShapeDtypeStruct((1, p), F32)],
                  scratch_shapes=[pltpu.VMEM((tm, tp), F32)], name=name, sem=("arbitrary", "arbitrary", "arbitrary"),
                  args=(d, b3, x, g, dx), rider=rider)


def _mm_tn(a, d, *, nb, name, tm, tk, tn, alpha=1.0, rider=None):
    m, k = a.shape
    _, n = d.shape
    nw = n // nb
    per = nw // tn
    nm = m // tm
    assert nw % tn == 0 and k % tk == 0 and m % tm == 0

    def body(a_ref, d_ref, o_ref, acc_ref):
        mm = pl.program_id(2)

        @pl.when(mm == 0)
        def _():
            acc_ref[...] = jnp.zeros_like(acc_ref)

        acc_ref[...] += _dot_tn(_bf(a_ref[...]), _bf(d_ref[...]))

        @pl.when(mm == nm - 1)
        def _():
            o_ref[...] = (alpha * acc_ref[...]).astype(o_ref.dtype)

    return _pcall(
        body, grid=(k // tk, nb * per, nm),
        in_specs=[pl.BlockSpec((tm, tk), lambda i, j, mm: (mm, i)),
                  pl.BlockSpec((tm, tn), lambda i, j, mm: (mm, j))],
        out_specs=pl.BlockSpec((None, tk, tn), lambda i, j, mm: (j // per, i, j % per)),
        out_shape=jax.ShapeDtypeStruct((nb, k, nw), BF16),
        scratch_shapes=[pltpu.VMEM((tk, tn), F32)],
        name=name, sem=("parallel", "parallel", "arbitrary"), args=(a, d), rider=rider)


def _rows_from_view(ref, buf, w, d, tm):
    for k in range(d):
        for c in range(w // HEAD):
            lanes = slice(k * w + c * HEAD, k * w + (c + 1) * HEAD)
            buf.at[c][pl.ds(k, tm // d, stride=d), :] = ref[:, lanes].astype(F32)
    return _cat([buf[c] for c in range(w // HEAD)])


def _ew(fn, ins, outs, *, rows, tm, name):
    in_specs, args, scratch = [], [], []
    for s in ins:
        if s[0] == 't':
            _, arr, w, cb = s
            in_specs.append(pl.BlockSpec((tm, w), lambda i, cb=cb: (i, cb)))
        elif s[0] == 'v':
            _, arr, w, d = s
            in_specs.append(pl.BlockSpec((tm // d, d * w), lambda i: (i, 0)))
            scratch.append(pltpu.VMEM((w // HEAD, tm, HEAD), F32))
        else:
            arr = s[1]
            in_specs.append(pl.BlockSpec(arr.shape, lambda i, nd=arr.ndim: (0,) * nd))
        args.append(arr)
    out_specs, out_shape = [], []
    for s in outs:
        if s[0] == 't':
            _, w, dt = s
            out_specs.append(pl.BlockSpec((tm, w), lambda i: (i, 0)))
            out_shape.append(jax.ShapeDtypeStruct((rows, w), dt))
        elif s[0] == 'v':
            _, w, dt, d = s
            out_specs.append(pl.BlockSpec((tm // d, d * w), lambda i: (i, 0)))
            out_shape.append(jax.ShapeDtypeStruct((rows // d, d * w), dt))
            scratch.append(pltpu.VMEM((w // HEAD, tm, HEAD), F32))
        else:
            out_specs.append(pl.BlockSpec(s[1], lambda i: (0, 0)))
            out_shape.append(jax.ShapeDtypeStruct(s[1], F32))
    n_in, n_out = len(ins), len(outs)

    def body(*refs):
        bufs = list(refs[n_in + n_out:])
        vals = []
        for r, s in zip(refs[:n_in], ins):
            if s[0] == 'v':
                vals.append(_rows_from_view(r, bufs.pop(0), s[2], s[3], tm))
            else:
                vals.append(r[...])
        res = fn(*vals)
        if not isinstance(res, (tuple, list)):
            res = (res,)
        for r, s, v in zip(refs[n_in:n_in + n_out], outs, res):
            if s[0] == 't':
                r[...] = v.astype(r.dtype)
            elif s[0] == 'v':
                w, d, buf = s[1], s[3], bufs.pop(0)
                for c in range(w // HEAD):
                    buf[c] = v[:, c * HEAD:(c + 1) * HEAD].astype(F32)
                for k in range(d):
                    for c in range(w // HEAD):
                        lanes = slice(k * w + c * HEAD, k * w + (c + 1) * HEAD)
                        r[:, lanes] = buf.at[c][pl.ds(k, tm // d, stride=d), :].astype(r.dtype)
            else:
                @pl.when(pl.program_id(0) == 0)
                def _(r=r):
                    r[...] = jnp.zeros_like(r)

                r[...] += v

    res = pl.pallas_call(
        body, grid=(rows // tm,), in_specs=in_specs, out_specs=out_specs, out_shape=out_shape, scratch_shapes=scratch,
        name=name, compiler_params=_params(("arbitrary",)))(*args)
    return res


def _tile(arr, w, g):
    return ('t', arr, w, 0) if DILATIONS[g] == 1 else ('v', arr, w, DILATIONS[g])


def _tile_out(w, dtype, g):
    return ('t', w, dtype) if DILATIONS[g] == 1 else ('v', w, dtype, DILATIONS[g])


def _heads(x):
    return [x[:, h * HEAD:(h + 1) * HEAD] for h in range(x.shape[1] // HEAD)]


def _cat(xs):
    return jnp.concatenate(xs, axis=1)


def _head_mean(x):
    return _cat([jnp.broadcast_to(jnp.mean(h, axis=1, keepdims=True), h.shape) for h in _heads(x)])


def _rms_rows(x):
    return lax.rsqrt(jnp.mean(x * x, axis=1, keepdims=True) + EPS)


def _norm_fwd(x, g, name):
    return _ew(lambda xv, gv: xv * _rms_rows(xv) * gv,
               [('t', x, D_MODEL, 0), ('f', g)], [('t', D_MODEL, BF16)], rows=x.shape[0], tm=512, name=name)[0]


def _loss_fwd_bwd(y, target, name):
    def fn(yv, tv):
        e = yv - tv
        return e * (1.0 / D_MODEL), jnp.sum(e * e, axis=0, keepdims=True)

    return _ew(fn, [('t', y, D_MODEL, 0), ('t', target, D_MODEL, 0)], [('t', D_MODEL, F32), ('acc', (1, D_MODEL))],
               rows=y.shape[0], tm=512, name=name)


def _rot(x):
    sgn = jnp.where(lax.broadcasted_iota(jnp.int32, x.shape, 1) < HEAD // 2, -1.0, 1.0)
    return pltpu.roll(x, HEAD // 2, 1) * sgn


def _gain_rows(qn, kn):
    return [a[g:g + 1] for a in (qn, kn) for g in range(ATT_GROUPS)]


def _qk_fwd(proj, cos, sin, qn, kn, name):
    def fn(*v):
        xs, cosv, sinv, gains, vs = v[:6], v[6], v[7], v[8:14], v[14:17]
        outs = []
        for j, x in enumerate(xs):
            gain = gains[j]
            ys = []
            for xh in _heads(x.astype(F32)):
                xn = xh * _rms_rows(xh) * gain
                ys.append(xn * cosv + _rot(xn) * sinv)
            outs.append(_cat(ys))
        return outs + list(vs)

    ins = ([('t', proj, 512, CB_AQ + j) for j in range(6)] + [('t', cos, HEAD, 0), ('t', sin, HEAD, 0)]
           + [('f', a) for a in _gain_rows(qn, kn)] + [('t', proj, 512, CB_AV + g) for g in range(ATT_GROUPS)])
    return _ew(fn, ins, [_tile_out(ATT_GW, BF16, j % ATT_GROUPS) for j in range(9)], rows=proj.shape[0], tm=512, name=name)


def _qk_bwd(dqk, proj, cos, sin, qn, kn, name):
    def fn(*v):
        ds, xs, cosv, sinv, gains = v[:6], v[6:12], v[12], v[13], v[14:20]
        rows8 = lax.broadcasted_iota(jnp.int32, (8, HEAD), 0)
        outs, dgs = [], [jnp.zeros((8, HEAD), F32)] * 2
        for j in range(6):
            gain = gains[j]
            dx, dg = [], jnp.zeros((1, HEAD), F32)
            for dyh, xh in zip(_heads(ds[j]), _heads(xs[j].astype(F32))):
                r = _rms_rows(xh)
                xhat = xh * r
                dxn = dyh * cosv - _rot(dyh * sinv)
                dg = dg + jnp.sum(dxn * xhat, axis=0, keepdims=True)
                dxh = dxn * gain
                dx.append(r * (dxh - xhat * jnp.mean(dxh * xhat, axis=1, keepdims=True)))
            outs.append(_cat(dx))
            dgs[j // 3] = dgs[j // 3] + jnp.where(rows8 == j % 3, dg, 0.0)
        return _cat(outs), dgs[0], dgs[1]

    ins = ([_tile(a, ATT_GW, j % ATT_GROUPS) for j, a in enumerate(dqk)] + [('t', proj, 512, CB_AQ + j) for j in range(6)]
           + [('t', cos, HEAD, 0), ('t', sin, HEAD, 0)] + [('f', a) for a in _gain_rows(qn, kn)])
    return _ew(fn, ins, [('t', 6 * ATT_GW, BF16), ('acc', (8, HEAD)), ('acc', (8, HEAD))],
               rows=proj.shape[0], tm=256, name=name)


def _pick(x, h):
    lanes = lax.broadcasted_iota(jnp.int32, x.shape, 1)
    return jnp.sum(jnp.where(lanes == h, x, 0.0), axis=1, keepdims=True)


def _spread(x):
    return _cat([jnp.broadcast_to(_pick(x, h), (x.shape[0], HEAD)) for h in range(ATT_HEADS)])


def _compact(x):
    lanes = lax.broadcasted_iota(jnp.int32, (x.shape[0], HEAD), 1)
    out = jnp.zeros((x.shape[0], HEAD), F32)
    for h, xh in enumerate(_heads(x)):
        out = jnp.where(lanes == h, xh, out)
    return out


def _group_weights(l0, l1, l2):
    l0, l1, l2 = _spread(l0), _spread(l1), _spread(l2)
    m = jnp.maximum(jnp.maximum(l0, l1), l2)
    e0, e1, e2 = jnp.exp(l0 - m), jnp.exp(l1 - m), jnp.exp(l2 - m)
    inv = 1.0 / (e0 + e1 + e2)
    return e0 * inv, e1 * inv, e2 * inv


def _merge_fwd(outs, lses, name):
    def fn(o0, o1, o2, l0, l1, l2):
        a0, a1, a2 = _group_weights(l0, l1, l2)
        return a0 * o0 + a1 * o1 + a2 * o2

    ins = [_tile(a, ATT_GW, g) for g, a in enumerate(outs)] + [_tile(a, HEAD, g) for g, a in enumerate(lses)]
    return _ew(fn, ins, [('t', ATT_GW, BF16)], rows=outs[0].shape[0], tm=512, name=name)[0]


def _merge_bwd(dob, outs, lses, name):
    def fn(dov, o0, o1, o2, l0, l1, l2):
        a0, a1, a2 = _group_weights(l0, l1, l2)
        ob = a0 * o0 + a1 * o1 + a2 * o2
        s = _head_mean(dov * ob) * float(HEAD)
        return a0 * dov, a1 * dov, a2 * dov, _compact(a0 * s), _compact(a1 * s), _compact(a2 * s)

    ins = ([('t', dob, ATT_GW, 0)] + [_tile(a, ATT_GW, g) for g, a in enumerate(outs)]
           + [_tile(a, HEAD, g) for g, a in enumerate(lses)])
    groups = range(ATT_GROUPS)
    return _ew(fn, ins, [_tile_out(ATT_GW, BF16, g) for g in groups] + [_tile_out(HEAD, F32, g) for g in groups],
               rows=dob.shape[0], tm=512, name=name)


HG_ROWS = 256


def _hg_gates(hq, hf, hi, lbv):
    sig = _sig(hf)
    f = lbv + (1.0 - lbv) * sig
    return hq * _sig(hq), 1.0 - f, hi, jnp.log(f), sig, f


def _split3(x):
    hi = _bf(x)
    r1 = x - hi.astype(F32)
    mid = _bf(r1)
    return hi, mid, _bf(r1 - mid.astype(F32))


def _tri_dot(tri, x):
    hi, mid, lo = _split3(x)
    return _dot(tri, hi) + _dot(tri, mid) + _dot(tri, lo)


def _row(x, i):
    rows = lax.broadcasted_iota(jnp.int32, x.shape, 0)
    return jnp.sum(jnp.where(rows == i, x, 0.0), axis=0, keepdims=True)


def _hg_decay(logf, q, k):
    c = HG_CHUNK
    row = lax.broadcasted_iota(jnp.int32, (c, c), 0)
    col = lax.broadcasted_iota(jnp.int32, (c, c), 1)
    g = _tri_dot((row >= col).astype(BF16), logf)
    gm = _row(g, c // 2 - 1)
    gl = _row(g, c - 1)
    decays = jnp.exp(g), jnp.exp(g - gm), jnp.exp(gm - g), jnp.exp(gl - g)
    return gl, decays, q * decays[0], q * decays[1], k * decays[2], k * decays[3]


def _hg_out_fwd(o, hg, gain):
    r = lax.rsqrt(_head_mean(o * o) + EPS)
    return o * r * gain * (hg * _sig(hg))


def _hgrn_fwd(proj, hn, w_hf, lb, gain, name, rider=None):
    t = proj.shape[0]
    nck = HG_ROWS // HG_CHUNK

    def body(hq_ref, hn_ref, whf_ref, hi_ref, hg_ref, lb_ref, gn_ref, o_ref, oa_ref, sall_ref, st_ref):
        @pl.when(pl.program_id(0) == 0)
        def _():
            st_ref[...] = jnp.zeros_like(st_ref)

        lbv = lb_ref[...]
        gnv = gn_ref[...]
        c = HG_CHUNK
        mask = lax.broadcasted_iota(jnp.int32, (c, c), 0) >= lax.broadcasted_iota(jnp.int32, (c, c), 1)

        def chunk(cc, carry):
            sl = pl.ds(pl.multiple_of(cc * c, c), c)
            hf = _dot(hn_ref[sl, :], whf_ref[0])
            q, k, v, logf, _, _ = _hg_gates(hq_ref[sl, :].astype(F32), hf, hi_ref[sl, :].astype(F32), lbv)
            gl, _, qg, qt, kt, kd = _hg_decay(logf, q, k)
            egl = jnp.exp(gl)
            os = []
            for h in range(HG_HEADS):
                hs = slice(h * HEAD, (h + 1) * HEAD)
                st = st_ref[h]
                sall_ref[cc, h] = st
                a = jnp.where(mask, _dot_nt(_bf(qt[:, hs]), _bf(kt[:, hs])), 0.0)
                os.append(_dot(_bf(a), _bf(v[:, hs])) + _dot_nt(_bf(qg[:, hs]), _bf(st)))
                st_ref[h] = egl[:, hs] * st + _dot_tn(_bf(v[:, hs]), _bf(kd[:, hs]))
            o = _cat(os)
            o_ref[sl, :] = o
            oa_ref[sl, :] = _hg_out_fwd(o, hg_ref[sl, :].astype(F32), gnv).astype(oa_ref.dtype)
            return carry

        lax.fori_loop(0, nck, chunk, 0)

    col = lambda j: pl.BlockSpec((HG_ROWS, D_MODEL), lambda i, j=j: (i, j))
    small = pl.BlockSpec((1, D_MODEL), lambda i: (0, 0))
    return _pcall(
        body, grid=(t // HG_ROWS,),
        in_specs=[col(0), col(0), pl.BlockSpec(w_hf.shape, lambda i: (0, 0, 0)), col(2), col(3), small, small],
        out_specs=[col(0), col(0), pl.BlockSpec((nck, HG_HEADS, HEAD, HEAD), lambda i: (i, 0, 0, 0))],
        out_shape=[jax.ShapeDtypeStruct((t, D_MODEL), F32), jax.ShapeDtypeStruct((t, D_MODEL), BF16),
                   jax.ShapeDtypeStruct((t // HG_CHUNK, HG_HEADS, HEAD, HEAD), F32)],
        scratch_shapes=[pltpu.VMEM((HG_HEADS, HEAD, HEAD), F32)],
        name=name, sem=("arbitrary",), args=(proj, hn, w_hf, proj, proj, lb, gain), rider=rider)


def _terms(x, precise):
    hi = _bf(x)
    return (hi, _bf(x - hi.astype(F32))) if precise else (hi,)


def _mm(dot, a, b):
    out = dot(a[0], b[0])
    if len(a) > 1:
        out = out + dot(a[1], b[0])
    if len(b) > 1:
        out = out + dot(a[0], b[1])
    return out


def _hgrn_bwd(doa, oscan, proj, hn, w_hf, sall, lb, gain, dqk, dvs, dgab, name, precise, rider=None):
    t = proj.shape[0]
    nck = HG_ROWS // HG_CHUNK
    nsteps = t // HG_ROWS
    terms = functools.partial(_terms, precise=precise)
    n_view = sum(d > 1 for d in DILATIONS)

    def body(doa_ref, os_ref, hq_ref, hn_ref, whf_ref, hi_ref, hg_ref, sall_ref, lb_ref, gn_ref, dqk_ref, dv0_ref, dv1_ref,
             dv2_ref, dgab_ref, dproj_ref, dgn_ref, dlb_ref, dst_ref, *bufs):
        @pl.when(pl.program_id(0) == 0)
        def _():
            dst_ref[...] = jnp.zeros_like(dst_ref)
            dgn_ref[...] = jnp.zeros_like(dgn_ref)
            dlb_ref[...] = jnp.zeros_like(dlb_ref)

        at = 4 * D_MODEL
        dproj_ref[:, at:at + 6 * ATT_GW] = dqk_ref[...]
        at += 6 * ATT_GW
        spare = list(bufs)
        for d, dv_ref in zip(DILATIONS, (dv0_ref, dv1_ref, dv2_ref)):
            dv = dv_ref[...] if d == 1 else _rows_from_view(dv_ref, spare.pop(0), ATT_GW, d, HG_ROWS)
            dproj_ref[:, at:at + ATT_GW] = dv.astype(dproj_ref.dtype)
            at += ATT_GW
        dproj_ref[:, at:] = dgab_ref[...]

        lbv = lb_ref[...]
        gnv = gn_ref[...]
        c = HG_CHUNK
        row = lax.broadcasted_iota(jnp.int32, (c, c), 0)
        colm = lax.broadcasted_iota(jnp.int32, (c, c), 1)
        mask = row >= colm
        triu = (row <= colm).astype(BF16)
        last = lax.broadcasted_iota(jnp.int32, (c, HEAD), 0) == c - 1

        def chunk(ci, carry):
            cc = nck - 1 - ci
            sl = pl.ds(pl.multiple_of(cc * c, c), c)
            hq, hg = hq_ref[sl, :].astype(F32), hg_ref[sl, :].astype(F32)
            hf = _dot(hn_ref[sl, :], whf_ref[0])
            q, k, v, logf, sig, f = _hg_gates(hq, hf, hi_ref[sl, :].astype(F32), lbv)
            gl, (e_qg, e_qt, e_kt, e_kd), qg, qt, kt, kd = _hg_decay(logf, q, k)
            egl = jnp.exp(gl)
            o = os_ref[sl, :]
            dy = doa_ref[sl, :]
            r = lax.rsqrt(_head_mean(o * o) + EPS)
            oh = o * r
            sg = _sig(hg)
            silu_g = hg * sg
            dgn_ref[...] += jnp.sum(dy * oh * silu_g, axis=0, keepdims=True)
            dhg = dy * oh * gnv * (sg * (1.0 + hg * (1.0 - sg)))
            doh = dy * gnv * silu_g
            do = r * (doh - oh * _head_mean(doh * oh))
            dqs, dks, dvs, dgs = [], [], [], []
            for h in range(HG_HEADS):
                hs = slice(h * HEAD, (h + 1) * HEAD)
                st = sall_ref[cc, h]
                dst = dst_ref[h]
                qt_h, kt_h, qg_h, kd_h = qt[:, hs], kt[:, hs], qg[:, hs], kd[:, hs]
                do_p, v_p, qt_p, kt_p, qg_p = terms(do[:, hs]), terms(v[:, hs]), terms(qt_h), terms(kt_h), terms(qg_h)
                st_p, dst_p = terms(st), terms(dst)
                a = jnp.where(mask, _dot_nt(qt_p[0], kt_p[0]), 0.0)
                da = terms(jnp.where(mask, _mm(_dot_nt, do_p, v_p), 0.0))
                dqt = _mm(_dot, da, kt_p)
                dkt = _mm(_dot_tn, da, qt_p)
                dqg = _mm(_dot, do_p, st_p)
                dv = _dot_tn(_bf(a), do_p[0]) + _dot_nt(_bf(kd_h), dst_p[0])
                dkd = _mm(_dot, v_p, dst_p)
                dgl = egl[:, hs] * jnp.sum(st * dst, axis=0, keepdims=True) + jnp.sum(dkd * kd_h, axis=0, keepdims=True)
                dst_ref[h] = egl[:, hs] * dst + _mm(_dot_tn, do_p, qg_p)
                dqs.append(dqt * e_qt[:, hs] + dqg * e_qg[:, hs])
                dks.append(dkt * e_kt[:, hs] + dkd * e_kd[:, hs])
                dvs.append(dv)
                dgs.append(dqt * qt_h - dkt * kt_h + dqg * qg_h - dkd * kd_h + jnp.where(last, dgl, 0.0))
            dq, dk, dv, dg = _cat(dqs), _cat(dks), _cat(dvs), _cat(dgs)
            dlogf = _tri_dot(triu, dg)
            df = dlogf / f - dk
            dlb_ref[...] += jnp.sum(df * (1.0 - sig), axis=0, keepdims=True)
            dhf = df * (1.0 - lbv) * sig * (1.0 - sig)
            sq = _sig(hq)
            dhq = dq * (sq * (1.0 + hq * (1.0 - sq)))
            dproj_ref[sl, :4 * D_MODEL] = _cat([dhq, dhf, dv, dhg]).astype(dproj_ref.dtype)
            return carry

        lax.fori_loop(0, nck, chunk, 0)

    rev = lambda j: pl.BlockSpec((HG_ROWS, D_MODEL), lambda i, j=j: (nsteps - 1 - i, j))
    rows = lambda a, d=1: pl.BlockSpec((HG_ROWS // d, a.shape[1]), lambda i: (nsteps - 1 - i, 0))
    small = pl.BlockSpec((1, D_MODEL), lambda i: (0, 0))
    return _pcall(
        body, grid=(nsteps,),
        in_specs=[rev(0), rev(0), rev(0), rev(0), pl.BlockSpec(w_hf.shape, lambda i: (0, 0, 0)), rev(2), rev(3),
                  pl.BlockSpec((nck, HG_HEADS, HEAD, HEAD), lambda i: (nsteps - 1 - i, 0, 0, 0)), small, small,
                  rows(dqk)] + [rows(a, d) for a, d in zip(dvs, DILATIONS)] + [rows(dgab)],
        out_specs=[pl.BlockSpec((HG_ROWS, P_IN), lambda i: (nsteps - 1 - i, 0)), small, small],
        out_shape=[jax.ShapeDtypeStruct((t, P_IN), BF16), jax.ShapeDtypeStruct((1, D_MODEL), F32),
                   jax.ShapeDtypeStruct((1, D_MODEL), F32)],
        scratch_shapes=[pltpu.VMEM((HG_HEADS, HEAD, HEAD), F32)] + [pltpu.VMEM((ATT_HEADS, HG_ROWS, HEAD), F32)] * n_view,
        name=name, sem=("arbitrary",), args=(doa, oscan, proj, hn, w_hf, proj, proj, sall, lb, gain, dqk, *dvs, dgab),
        rider=rider)


def _window_masks(has_previous):
    qi = lax.broadcasted_iota(jnp.int32, (ATT_BLK, 2 * ATT_BLK), 0)
    ki = lax.broadcasted_iota(jnp.int32, (ATT_BLK, 2 * ATT_BLK), 1)
    band = jnp.logical_and(ki >= qi, ki <= qi + ATT_BLK)
    return band, jnp.logical_and(band, jnp.logical_or(ki >= ATT_BLK, has_previous))


def _two_blocks(ref, prev_ref, j, hs):
    if j == 0:
        return jnp.concatenate([prev_ref[:, hs], ref[0:ATT_BLK, hs]], axis=0)
    return ref[(j - 1) * ATT_BLK:(j + 1) * ATT_BLK, hs]


def _attn_cfg(qg, g):
    d = DILATIONS[g]
    length = qg.shape[0]
    assert qg.shape[1] == d * ATT_GW
    nb = length // ATT_BLK
    return d, length, nb, min(ATT_STEP_BLOCKS, nb)


def _attn_fwd(qg, kg, vg, g, name):
    d, length, nb, rb = _attn_cfg(qg, g)
    scale = HEAD ** -0.5

    def body(q_ref, k_ref, v_ref, kp_ref, vp_ref, o_ref, l_ref):
        n = pl.program_id(1)
        band, first_band = _window_masks(n > 0)
        lanes = lax.broadcasted_iota(jnp.int32, (ATT_BLK, HEAD), 1)
        for j in range(rb):
            rows = slice(j * ATT_BLK, (j + 1) * ATT_BLK)
            lse = jnp.zeros((ATT_BLK, HEAD), F32)
            for h in range(ATT_HEADS):
                hs = slice(h * HEAD, (h + 1) * HEAD)
                k2, v2 = _two_blocks(k_ref, kp_ref, j, hs), _two_blocks(v_ref, vp_ref, j, hs)
                s = jnp.where(first_band if j == 0 else band, _dot_nt(q_ref[rows, hs], k2) * scale, NEG)
                m = jnp.max(s, axis=1, keepdims=True)
                p = jnp.exp(s - m)
                l = jnp.sum(p, axis=1, keepdims=True)
                o_ref[rows, hs] = (_dot(_bf(p), v2) / l).astype(o_ref.dtype)
                lse = jnp.where(lanes == h, m + jnp.log(l), lse)
            l_ref[rows, :] = lse

    own = pl.BlockSpec((rb * ATT_BLK, ATT_GW), lambda r, n: (n, r))
    own_head = pl.BlockSpec((rb * ATT_BLK, HEAD), lambda r, n: (n, r))
    prev = pl.BlockSpec((ATT_BLK, ATT_GW), lambda r, n: (jnp.maximum(n * rb - 1, 0), r))
    return pl.pallas_call(
        body, grid=(d, nb // rb), in_specs=[own, own, own, prev, prev], out_specs=[own, own_head],
        out_shape=[jax.ShapeDtypeStruct((length, d * ATT_GW), BF16), jax.ShapeDtypeStruct((length, d * HEAD), F32)],
        name=name, compiler_params=_params(("parallel", "arbitrary")))(qg, kg, vg, kg, vg)


def _attn_bwd(qg, kg, vg, dog, lse, delta, g, name):
    d, length, nb, rb = _attn_cfg(qg, g)
    nsteps = nb // rb
    scale = HEAD ** -0.5

    def body(q_ref, k_ref, v_ref, do_ref, l_ref, dl_ref, kp_ref, vp_ref, qn_ref, don_ref, ln_ref, dln_ref,
             dq_ref, dk_ref, dv_ref):
        n = pl.program_id(1)
        band, first_band = _window_masks(n > 0)
        qi = lax.broadcasted_iota(jnp.int32, (ATT_BLK, ATT_BLK), 0)
        ki = lax.broadcasted_iota(jnp.int32, (ATT_BLK, ATT_BLK), 1)
        next_m = jnp.logical_and(ki >= qi, n < nsteps - 1)
        last = slice((rb - 1) * ATT_BLK, rb * ATT_BLK)
        for h in range(ATT_HEADS):
            hs = slice(h * HEAD, (h + 1) * HEAD)
            dk, dv = [None] * rb, [None] * rb
            for j in range(rb):
                rows = slice(j * ATT_BLK, (j + 1) * ATT_BLK)
                q, do = q_ref[rows, hs], do_ref[rows, hs]
                k2, v2 = _two_blocks(k_ref, kp_ref, j, hs), _two_blocks(v_ref, vp_ref, j, hs)
                p = jnp.where(first_band if j == 0 else band,
                              jnp.exp(_dot_nt(q, k2) * scale - _pick(l_ref[rows, :], h)), 0.0)
                ds = _bf(p * (_dot_nt(do, v2) - _pick(dl_ref[rows, :], h)) * scale)
                dq_ref[rows, hs] = _dot(ds, k2).astype(dq_ref.dtype)
                dk2, dv2 = _dot_tn(ds, q), _dot_tn(_bf(p), do)
                if j >= 1:
                    dk[j - 1] = dk[j - 1] + dk2[:ATT_BLK]
                    dv[j - 1] = dv[j - 1] + dv2[:ATT_BLK]
                dk[j], dv[j] = dk2[ATT_BLK:], dv2[ATT_BLK:]
            q, do = qn_ref[:, hs], don_ref[:, hs]
            p = jnp.where(next_m, jnp.exp(_dot_nt(q, k_ref[last, hs]) * scale - _pick(ln_ref[...], h)), 0.0)
            ds = _bf(p * (_dot_nt(do, v_ref[last, hs]) - _pick(dln_ref[...], h)) * scale)
            dk[rb - 1] = dk[rb - 1] + _dot_tn(ds, q)
            dv[rb - 1] = dv[rb - 1] + _dot_tn(_bf(p), do)
            for j in range(rb):
                rows = slice(j * ATT_BLK, (j + 1) * ATT_BLK)
                dk_ref[rows, hs] = dk[j].astype(dk_ref.dtype)
                dv_ref[rows, hs] = dv[j].astype(dv_ref.dtype)

    own = pl.BlockSpec((rb * ATT_BLK, ATT_GW), lambda r, n: (n, r))
    prev = pl.BlockSpec((ATT_BLK, ATT_GW), lambda r, n: (jnp.maximum(n * rb - 1, 0), r))
    nxt = pl.BlockSpec((ATT_BLK, ATT_GW), lambda r, n: (jnp.minimum((n + 1) * rb, nb - 1), r))
    own_head = pl.BlockSpec((rb * ATT_BLK, HEAD), lambda r, n: (n, r))
    nxt_head = pl.BlockSpec((ATT_BLK, HEAD), lambda r, n: (jnp.minimum((n + 1) * rb, nb - 1), r))
    return pl.pallas_call(
        body, grid=(d, nsteps), in_specs=[own] * 4 + [own_head] * 2 + [prev, prev, nxt, nxt, nxt_head, nxt_head],
        out_specs=[own, own, own], out_shape=[jax.ShapeDtypeStruct((length, d * ATT_GW), BF16)] * 3,
        name=name, compiler_params=_params(("parallel", "arbitrary")))(
            qg, kg, vg, dog, lse, delta, kg, vg, qg, dog, lse, delta)


def _rope_tables(t):
    pos = jnp.arange(t, dtype=F32)
    inv = ROPE_THETA ** (-jnp.arange(0, HEAD, 2, dtype=F32) / HEAD)
    ang = pos[:, None] * inv[None, :]
    ang = jnp.concatenate([ang, ang], axis=-1)
    return jnp.cos(ang), jnp.sin(ang)


def _lower_bounds(logits):
    lb = jnp.cumsum(jax.nn.softmax(logits.astype(F32), axis=0), axis=0)
    return lb - lb[0:1]


FFN_ROWS = 256
FF_SHARD = 2 * D_FF // N_CHIPS


def _ffn_in_act(x, g, w_in, name, rider=None):
    t = x.shape[0]

    def body(x_ref, g_ref, w_ref, h_ref, ab_ref, u_ref):
        xv = x_ref[...]
        h = _bf(xv * _rms_rows(xv) * g_ref[...])
        h_ref[...] = h
        for s in range(N_CHIPS // 2):
            cols = slice(s * FF_SHARD, (s + 1) * FF_SHARD)
            a = _dot(h, w_ref[s])
            b = _dot(h, w_ref[s + N_CHIPS // 2])
            ab_ref[:, cols] = a.astype(ab_ref.dtype)
            ab_ref[:, D_FF + s * FF_SHARD:D_FF + (s + 1) * FF_SHARD] = b.astype(ab_ref.dtype)
            u_ref[:, cols] = (a * _sig(a) * b).astype(u_ref.dtype)

    row = lambda w: pl.BlockSpec((FFN_ROWS, w), lambda i: (i, 0))
    return _pcall(
        body, grid=(t // FFN_ROWS,),
        in_specs=[row(D_MODEL), pl.BlockSpec((1, D_MODEL), lambda i: (0, 0)),
                  pl.BlockSpec(w_in.shape, lambda i: (0, 0, 0))],
        out_specs=[row(D_MODEL), row(2 * D_FF), row(D_FF)],
        out_shape=[jax.ShapeDtypeStruct((t, D_MODEL), BF16), jax.ShapeDtypeStruct((t, 2 * D_FF), BF16),
                   jax.ShapeDtypeStruct((t, D_FF), BF16)],
        name=name, sem=("parallel",), args=(x, g, w_in), rider=rider)


def _ffn_bwd_du_act(dx, w_out, ab, name, rider=None):
    t = dx.shape[0]

    def body(dx_ref, w_ref, ab_ref, o_ref):
        du = 0.5 * _dot_nt(_bf(dx_ref[...]), w_ref[0])
        a = ab_ref[:, :D_FF].astype(F32)
        b = ab_ref[:, D_FF:].astype(F32)
        s = _sig(a)
        o_ref[:, :D_FF] = (du * b * (s * (1.0 + a * (1.0 - s)))).astype(o_ref.dtype)
        o_ref[:, D_FF:] = (du * a * s).astype(o_ref.dtype)

    row = lambda w: pl.BlockSpec((FFN_ROWS, w), lambda i: (i, 0))
    return _pcall(
        body, grid=(t // FFN_ROWS,),
        in_specs=[row(D_MODEL), pl.BlockSpec(w_out.shape, lambda i: (0, 0, 0)), row(2 * D_FF)],
        out_specs=row(2 * D_FF), out_shape=jax.ShapeDtypeStruct((t, 2 * D_FF), BF16),
        name=name, sem=("parallel",), args=(dx, w_out, ab), rider=rider)


MIX_ROWS = 512


def _gate_specs():
    return [pl.BlockSpec((MIX_ROWS, 512), lambda i, cb=cb: (i, cb)) for cb in (CB_GA, CB_GA + 1, CB_GB, CB_GB + 1)]


def _gate(lo_ref, hi_ref):
    return _sig(_cat([lo_ref[...], hi_ref[...]]).astype(F32))


def _whole(a):
    return pl.BlockSpec(a.shape, lambda i: (0,) * a.ndim)


def _mix_tail_fwd(oa, ob, proj, x, w_a, w_b, w_o, name):
    t = x.shape[0]

    def body(oa_ref, ob_ref, ga0, ga1, gb0, gb1, x_ref, wa_ref, wb_ref, wo_ref, y_ref, m_ref, ya_ref, yb_ref):
        ya = _dot(oa_ref[...], wa_ref[0])
        yb = _cat([_dot(ob_ref[...], wb_ref[s]) for s in range(N_CHIPS)])
        merged = _bf(_gate(ga0, ga1) * ya + _gate(gb0, gb1) * yb)
        m_ref[...] = merged
        ya_ref[...] = ya.astype(ya_ref.dtype)
        yb_ref[...] = yb.astype(yb_ref.dtype)
        y_ref[...] = x_ref[...] + _dot(merged, wo_ref[0])

    row = lambda w: pl.BlockSpec((MIX_ROWS, w), lambda i: (i, 0))
    return pl.pallas_call(
        body, grid=(t // MIX_ROWS,),
        in_specs=[row(D_MODEL), row(ATT_GW)] + _gate_specs() + [row(D_MODEL), _whole(w_a), _whole(w_b), _whole(w_o)],
        out_specs=[row(D_MODEL)] * 4,
        out_shape=[jax.ShapeDtypeStruct((t, D_MODEL), F32)] + [jax.ShapeDtypeStruct((t, D_MODEL), BF16)] * 3,
        name=name, compiler_params=_params(("parallel",)))(oa, ob, proj, proj, proj, proj, x, w_a, w_b, w_o)


def _mix_tail_bwd(dx, proj, ya, yb, w_a, w_b, w_o, name):
    t = dx.shape[0]
    shard = D_MODEL // N_CHIPS

    def body(dx_ref, ga0, ga1, gb0, gb1, ya_ref, yb_ref, wa_ref, wb_ref, wo_ref, dya_ref, dyb_ref, dg_ref, doa_ref, dob_ref):
        dm = _dot_nt(_bf(dx_ref[...]), wo_ref[0])
        sa, sb = _gate(ga0, ga1), _gate(gb0, gb1)
        dya, dyb = _bf(dm * sa), _bf(dm * sb)
        dya_ref[...] = dya
        dyb_ref[...] = dyb
        dg_ref[:, :D_MODEL] = (dm * ya_ref[...].astype(F32) * sa * (1.0 - sa)).astype(dg_ref.dtype)
        dg_ref[:, D_MODEL:] = (dm * yb_ref[...].astype(F32) * sb * (1.0 - sb)).astype(dg_ref.dtype)
        doa_ref[...] = _dot_nt(dya, wa_ref[0])
        dob = _dot_nt(dyb[:, :shard], wb_ref[0])
        for s in range(1, N_CHIPS):
            dob = dob + _dot_nt(dyb[:, s * shard:(s + 1) * shard], wb_ref[s])
        dob_ref[...] = dob

    row = lambda w: pl.BlockSpec((MIX_ROWS, w), lambda i: (i, 0))
    return pl.pallas_call(
        body, grid=(t // MIX_ROWS,),
        in_specs=[row(D_MODEL)] + _gate_specs() + [row(D_MODEL), row(D_MODEL), _whole(w_a), _whole(w_b), _whole(w_o)],
        out_specs=[row(D_MODEL), row(D_MODEL), row(2 * D_MODEL), row(D_MODEL), row(ATT_GW)],
        out_shape=[jax.ShapeDtypeStruct((t, D_MODEL), BF16), jax.ShapeDtypeStruct((t, D_MODEL), BF16),
                   jax.ShapeDtypeStruct((t, 2 * D_MODEL), BF16), jax.ShapeDtypeStruct((t, D_MODEL), F32),
                   jax.ShapeDtypeStruct((t, ATT_GW), F32)],
        name=name, compiler_params=_params(("parallel",)))(dx, proj, proj, proj, proj, ya, yb, w_a, w_b, w_o)


def _ffn_fwd(x, g, src, l, pre):
    tag = f"l{l}_{pre}"
    w_in = src.weight(l, pre + "_w_in")
    h, ab, u = _ffn_in_act(x, g, w_in, name=tag + "_in_act", rider=src.ride(tag + "_in_act"))
    w_out = src.weight(l, pre + "_w_out")
    y = _mm_nn(u, w_out, name=tag + "_out", tm=512, tn=D_MODEL, out_dtype=F32, res=x, alpha=0.5, rider=src.ride(tag + "_out"))
    return y, (x, h, ab, u, w_in, w_out)


def _ffn_bwd(dx, saved, g, src, l, pre):
    tag = f"l{l}_{pre}"
    x, h, ab, u, w_in, w_out = saved
    g_out = _mm_tn(u, dx, nb=1, name=tag + "_bwd_wout", tm=1024, tk=1408, tn=D_MODEL, alpha=0.5, rider=src.ride(tag + "_bwd_wout"))
    src.grads(l, {pre + "_w_out": g_out.reshape(N_CHIPS, D_FF // N_CHIPS, D_MODEL)})
    dab = _ffn_bwd_du_act(dx, w_out, ab, name=tag + "_bwd_du_act", rider=src.ride(tag + "_bwd_du_act"))
    g_in = _mm_tn(h, dab, nb=N_CHIPS, name=tag + "_bwd_win", tm=2048, tk=D_MODEL, tn=FF_SHARD, rider=src.ride(tag + "_bwd_win"))
    src.grads(l, {pre + "_w_in": g_in})
    return _mm_nt(dab, w_in, name=tag + "_bwd_dh", tm=1024, tp=D_MODEL, tn=FF_SHARD, out_dtype=F32, rider=src.ride(tag + "_bwd_dh"),
                  norm=(x, g, dx))


def _mix_fwd(x, small, lb, cos, sin, src, l):
    tag = f"l{l}_mix"
    w = {}
    h = _norm_fwd(x, small["mix_norm"], name=tag + "_norm")
    w["w_in"] = src.weight(l, "w_in")
    proj = _mm_nn(h, w["w_in"], name=tag + "_in", tm=2048, tn=896, out_dtype=BF16, rider=src.ride(tag + "_in"))
    w["hf"] = w["w_in"][0:1, :, D_MODEL:2 * D_MODEL]
    oscan, oa, sall = _hgrn_fwd(proj, h, w["hf"], lb, small["hgrn_out_norm"], name=tag + "_hgrn", rider=src.ride(tag + "_hgrn"))
    qk = _qk_fwd(proj, cos, sin, small["attn_q_norm"], small["attn_k_norm"], name=tag + "_qk")
    outs, lses = [], []
    for g in range(ATT_GROUPS):
        o, lse = _attn_fwd(qk[g], qk[3 + g], qk[6 + g], g, name=f"{tag}_attn{g}")
        outs.append(o)
        lses.append(lse)
    ob = _merge_fwd(outs, lses, name=tag + "_merge")
    w.update({n: src.weight(l, n) for n in ("w_branch_a", "w_branch_b", "w_out")})
    y, merged, ya, yb = _mix_tail_fwd(oa, ob, proj, x, w["w_branch_a"], w["w_branch_b"], w["w_out"], name=tag + "_tail")
    return y, (x, h, proj, oscan, oa, sall, qk, outs, lses, ob, ya, yb, merged, w)


def _mix_bwd(dx, saved, small, lb, cos, sin, src, l, lb_live):
    tag = f"l{l}_mix"
    x, h, proj, oscan, oa, sall, qk, outs, lses, ob, ya, yb, merged, w = saved
    g_wout = _mm_tn(merged, dx, nb=1, name=tag + "_bwd_wout", tm=1024, tk=D_MODEL, tn=D_MODEL)
    dya, dyb, dgab, doa, dob = _mix_tail_bwd(dx, proj, ya, yb, w["w_branch_a"], w["w_branch_b"], w["w_out"], name=tag + "_bwd_tail")
    g_wa = _mm_tn(oa, dya, nb=1, name=tag + "_bwd_wa", tm=1024, tk=D_MODEL, tn=D_MODEL)
    g_wb = _mm_tn(ob, dyb, nb=N_CHIPS, name=tag + "_bwd_wb", tm=2048, tk=ATT_GW, tn=256)
    mb = _merge_bwd(dob, outs, lses, name=tag + "_bwd_merge")
    dqk, dvs = [None] * 6, []
    for g in range(ATT_GROUPS):
        dq, dk, dv = _attn_bwd(qk[g], qk[3 + g], qk[6 + g], mb[g], lses[g], mb[3 + g], g, name=f"{tag}_bwd_attn{g}")
        dqk[g], dqk[3 + g] = dq, dk
        dvs.append(dv)
    dqk_cols, dqn, dkn = _qk_bwd(dqk, proj, cos, sin, small["attn_q_norm"], small["attn_k_norm"], name=tag + "_bwd_qk")
    dproj, dgn, dlb = _hgrn_bwd(doa, oscan, proj, h, w["hf"], sall, lb, small["hgrn_out_norm"], dqk_cols, dvs, dgab,
                                name=tag + "_bwd_hgrn", precise=lb_live, rider=src.ride(tag + "_bwd_hgrn"))
    src.grads(l, dict(w_branch_a=g_wa.reshape(N_CHIPS, D_MODEL // N_CHIPS, D_MODEL), w_branch_b=g_wb,
                      w_out=g_wout.reshape(N_CHIPS, D_MODEL // N_CHIPS, D_MODEL)))
    g_win = _mm_tn(h, dproj, nb=N_CHIPS, name=tag + "_bwd_win", tm=1024, tk=D_MODEL, tn=2688, rider=src.ride(tag + "_bwd_win"))
    src.grads(l, dict(w_in=g_win))
    dx, dg = _mm_nt(dproj, w["w_in"], name=tag + "_bwd_dh", tm=1024, tp=D_MODEL, tn=2688, out_dtype=F32,
                    rider=src.ride(tag + "_bwd_dh"), norm=(x, small["mix_norm"], dx))
    return dx, dict(mix_norm=dg, hgrn_out_norm=dgn, lb=dlb, attn_q_norm=dqn, attn_k_norm=dkn)


BIG = ("ffn1_w_in", "ffn1_w_out", "w_in", "w_branch_a", "w_branch_b", "w_out", "ffn2_w_in", "ffn2_w_out")
ROW_SHARDED = ("ffn1_w_out", "w_branch_a", "w_out", "ffn2_w_out")
SMALL = ("ffn1_norm", "mix_norm", "hgrn_lb_logits", "hgrn_out_norm", "attn_q_norm", "attn_k_norm", "ffn2_norm")
WEIGHTS = ("ffn1_norm", "ffn1_w_in", "ffn1_w_out", "mix_norm", "w_in", "hgrn_lb_logits", "hgrn_out_norm", "attn_q_norm",
           "attn_k_norm", "w_branch_a", "w_branch_b", "w_out", "ffn2_norm", "ffn2_w_in", "ffn2_w_out")
SMALL_ROWS = 8


def _matmul_ready(name, a):
    return a.reshape(1, a.shape[0] * a.shape[1], a.shape[2]) if name in ROW_SHARDED else a


def _layer_small(small, l):
    s = {n: small[n][l].reshape(1, D_MODEL) for n in ("ffn1_norm", "mix_norm", "hgrn_out_norm", "ffn2_norm")}
    s.update({n: small[n][l] for n in ("attn_q_norm", "attn_k_norm")})
    return s


def _local_step(x, target, small, src):
    t = x.shape[0]
    cos, sin = _rope_tables(t)
    lbs = _lower_bounds(small["hgrn_lb_logits"])
    saved = []
    for l in range(2):
        sm = _layer_small(small, l)
        lb = lbs[l].reshape(1, D_MODEL)
        x, s1 = _ffn_fwd(x, sm["ffn1_norm"], src, l, "ffn1")
        x, s2 = _mix_fwd(x, sm, lb, cos, sin, src, l)
        x, s3 = _ffn_fwd(x, sm["ffn2_norm"], src, l, "ffn2")
        saved.append((sm, lb, s1, s2, s3))
    dx, sq = _loss_fwd_bwd(x, target, name="loss")
    small_rows = [None, None]
    for l in (1, 0):
        sm, lb, s1, s2, s3 = saved[l]
        dx, dg2 = _ffn_bwd(dx, s3, sm["ffn2_norm"], src, l, "ffn2")
        dx, g = _mix_bwd(dx, s2, sm, lb, cos, sin, src, l, lb_live=l > 0)
        dx, dg1 = _ffn_bwd(dx, s1, sm["ffn1_norm"], src, l, "ffn1")
        pad = lambda a: jnp.pad(a[:ATT_GROUPS].reshape(1, ATT_GROUPS * HEAD), ((0, 0), (0, D_MODEL - ATT_GROUPS * HEAD)))
        small_rows[l] = jnp.concatenate(
            [dg1, g["mix_norm"], g["lb"], g["hgrn_out_norm"], pad(g["attn_q_norm"]), pad(g["attn_k_norm"]), dg2,
             jnp.zeros((SMALL_ROWS - 7, D_MODEL), F32)], axis=0)
    return jnp.sum(sq), dx, jnp.concatenate(small_rows, axis=0)


def _coords():
    return lax.axis_index("x"), lax.axis_index("y"), lax.axis_index("c")


def _other_chips(x, y):
    return [(1 - x, y), (x, 1 - y), (1 - x, 1 - y)]


def _half_rows(rows, which):
    return pl.ds(which * (rows // 2), rows // 2)


def _gather_rider(shards):
    n = len(shards)

    def copies(w, full, sems):
        send, recv, fsend, frecv, osend, orecv = sems
        x, y, c = _coords()
        slot = 2 * x + y
        chips = _other_chips(x, y)

        def copy(i, j, blk, src, pair, to):
            return pltpu.make_async_remote_copy(src_ref=src, dst_ref=blk, send_sem=pair[0].at[i * 3 + j],
                                                recv_sem=pair[1].at[i * 3 + j], device_id=to, device_id_type=MESH)

        def block(i, chip_slot, core):
            return full[i].at[chip_slot, _half_rows(shards[i].shape[0], core)]

        pairs = [(i, j, chip) for i in range(n) for j, chip in enumerate(chips)]

        def first():
            return [copy(i, j, block(i, slot, c), w[i].at[_half_rows(shards[i].shape[0], c)], (send, recv), (*chip, c))
                    for i, j, chip in pairs]

        def landed(core, pair):
            return [copy(i, j, block(i, 2 * chip[0] + chip[1], core), block(i, 2 * chip[0] + chip[1], core), pair, (x, y, 1 - c))
                    for i, j, chip in pairs]

        def own():
            return [pltpu.make_async_remote_copy(src_ref=w[i], dst_ref=full[i].at[slot], send_sem=osend.at[i],
                                                 recv_sem=orecv.at[i], device_id=(x, y, 1 - c), device_id_type=MESH)
                    for i in range(n)]

        return first, landed, own

    def begin(w, full, sems):
        first, _, own = copies(w, full, sems)
        for cp in first() + own():
            cp.start()

    def end(w, full, sems):
        first, landed, own = copies(w, full, sems)
        forwards = landed(lax.axis_index("c"), sems[2:4])
        for arrival, forward in zip(landed(lax.axis_index("c"), sems[:2]), forwards):
            arrival.wait_recv()
            forward.start()
        for cp in landed(1 - lax.axis_index("c"), sems[2:4]) + own():
            cp.wait_recv()
        for cp in first() + forwards + own():
            cp.wait_send()

    out_shape = [jax.ShapeDtypeStruct((N_CHIPS,) + s.shape, s.dtype) for s in shards]
    sems = [pltpu.SemaphoreType.DMA((3 * n,))] * 4 + [pltpu.SemaphoreType.DMA((n,))] * 2
    return _Rider(shards, out_shape, sems, begin, end)


N_RECV = 7


def _scatter_rider(parts):
    n = len(parts)

    def copies(p, out, sems):
        send, recv = sems
        x, y, c = _coords()
        slot = 2 * x + y
        chips = _other_chips(x, y)

        def arrivals():
            return [pltpu.make_async_remote_copy(
                src_ref=out[i].at[k], dst_ref=out[i].at[k], send_sem=send.at[0], recv_sem=recv.at[i * N_RECV + k],
                device_id=(x, y, c), device_id_type=MESH) for i in range(n) for k in range(N_RECV)]

        sends = []
        for i in range(n):
            rows = parts[i].shape[1]
            for j, chip in enumerate(chips):
                for core in (0, 1):
                    sends.append(pltpu.make_async_remote_copy(
                        src_ref=p[i].at[2 * chip[0] + chip[1], _half_rows(rows, core)], dst_ref=out[i].at[2 * j + c],
                        send_sem=send.at[i * N_RECV + 2 * j + core], recv_sem=recv.at[i * N_RECV + 2 * j + c],
                        device_id=(*chip, core), device_id_type=MESH))
            sends.append(pltpu.make_async_remote_copy(
                src_ref=p[i].at[slot, _half_rows(rows, 1 - c)], dst_ref=out[i].at[6], send_sem=send.at[i * N_RECV + 6],
                recv_sem=recv.at[i * N_RECV + 6], device_id=(x, y, 1 - c), device_id_type=MESH))
        return sends, arrivals

    def begin(p, out, sems):
        for cp in copies(p, out, sems)[0]:
            cp.start()

    def end(p, out, sems):
        sends, arrivals = copies(p, out, sems)
        for cp in arrivals():
            cp.wait_recv()
        for cp in sends:
            cp.wait_send()

    out_shape = [jax.ShapeDtypeStruct((N_RECV, a.shape[1] // 2, a.shape[2]), a.dtype) for a in parts]
    return _Rider(parts, out_shape, [pltpu.SemaphoreType.DMA((N_RECV * n,))] * 2, begin, end)


def _run_alone(rider, name):
    _pcall(lambda: None, grid=(), in_specs=[], out_specs=[], out_shape=[], name=name, sem=(), args=(), rider=rider)
    return rider.result


def _sum_partials(own, parts, name):
    r, wd = own.shape
    tm = next(t for t in (256, 128, 64, 32, 16) if r % t == 0)

    def body(own_ref, p_ref, o_ref):
        acc = own_ref[...].astype(F32)
        for k in range(N_RECV):
            acc = acc + p_ref[k].astype(F32)
        o_ref[...] = acc

    return pl.pallas_call(
        body, grid=(r // tm,),
        in_specs=[pl.BlockSpec((tm, wd), lambda i: (i, 0)), pl.BlockSpec((N_RECV, tm, wd), lambda i: (0, i, 0))],
        out_specs=pl.BlockSpec((tm, wd), lambda i: (i, 0)), out_shape=jax.ShapeDtypeStruct((r, wd), F32),
        name=name, compiler_params=_params(("parallel",)))(own, parts)


def _exchange_halves(reduced, name):
    n = len(reduced)

    def body(*refs):
        r, out = refs[:n], refs[n:2 * n]
        send, recv = refs[2 * n:]
        x, y, c = _coords()
        sib = [pltpu.make_async_remote_copy(src_ref=r[i], dst_ref=out[i], send_sem=send.at[i], recv_sem=recv.at[i],
                                            device_id=(x, y, 1 - c), device_id_type=MESH) for i in range(n)]
        for cp in sib:
            cp.start()
        for cp in sib:
            cp.wait_recv()
        for cp in sib:
            cp.wait_send()

    out_shape = [jax.ShapeDtypeStruct(a.shape, a.dtype) for a in reduced]
    return pl.pallas_call(body, in_specs=[ANY] * n, out_specs=[ANY] * n, out_shape=out_shape,
                          scratch_shapes=[pltpu.SemaphoreType.DMA((n,))] * 2, name=name)(*reduced)


def _reduce_finish(parts, recv, tag):
    x, y, c = _coords()
    slot = 2 * x + y
    halves = []
    for i, (p, r) in enumerate(zip(parts, recv)):
        half = p.shape[1] // 2
        own = lax.dynamic_slice(p, (slot, c * half, 0), (1, half, p.shape[2]))[0]
        halves.append(_sum_partials(own, r, name=f"{tag}_sum{i}"))
    theirs = _exchange_halves(halves, name=tag + "_exchange")
    return [jnp.where(c == 0, jnp.concatenate([h, t], axis=0), jnp.concatenate([t, h], axis=0)) for h, t in zip(halves, theirs)]


GATHER_RIDES = {
    "l0_ffn1_in_act": ((0, "w_in"),),
    "l0_ffn1_out": ((0, "w_branch_a"), (0, "w_branch_b"), (0, "w_out")),
    "l0_mix_in": ((0, "ffn2_w_in"), (0, "ffn2_w_out"), (1, "ffn1_w_in"), (1, "ffn1_w_out")),
    "l0_mix_hgrn": ((1, "w_in"), (1, "w_branch_a"), (1, "w_branch_b"), (1, "w_out")),
    "l0_ffn2_in_act": ((1, "ffn2_w_in"), (1, "ffn2_w_out")),
}
ALONE_FIRST = ((0, "ffn1_w_in"), (0, "ffn1_w_out"))
SCATTER_RIDES = {
    "l1_mix_bwd_hgrn": ((1, "ffn2_w_in"), (1, "ffn2_w_out")),
    "l0_ffn2_bwd_win": ((1, "ffn1_w_in"),),
    "l0_ffn2_bwd_dh": ((1, "ffn1_w_out"), (1, "w_branch_a"), (1, "w_branch_b"), (1, "w_out")),
    "l0_mix_bwd_hgrn": ((1, "w_in"), (0, "ffn2_w_out")),
    "l0_mix_bwd_win": ((0, "ffn2_w_in"),),
    "l0_mix_bwd_dh": ((0, "w_in"),),
    "l0_ffn1_bwd_wout": ((0, "w_branch_a"), (0, "w_branch_b"), (0, "w_out")),
    "l0_ffn1_bwd_du_act": ((0, "ffn1_w_out"),),
    "l0_ffn1_bwd_dh": ((0, "ffn1_w_in"),),
}


class _Exchange:
    def __init__(self, shards):
        self.shards = shards
        self.pending = []
        self.full = {}
        self.parts = {}
        self.recv = {}

    def _gather(self, keys):
        return _gather_rider([self.shards[n][l] for l, n in keys]), "gather", list(keys)

    def _scatter(self, keys):
        return _scatter_rider([self.parts[k] for k in keys]), "scatter", list(keys)

    def _unpack(self):
        waiting = []
        for rider, kind, keys in self.pending:
            if rider.result is None:
                waiting.append((rider, kind, keys))
            elif kind == "gather":
                self.full.update(zip(keys, rider.result))
            else:
                self.recv.update(zip(keys, rider.result))
        self.pending = waiting

    def ride(self, host):
        if host in GATHER_RIDES:
            self.pending.append(self._gather(GATHER_RIDES[host]))
        elif host in SCATTER_RIDES:
            self.pending.append(self._scatter(SCATTER_RIDES[host]))
        else:
            return None
        return self.pending[-1][0]

    def weight(self, l, name):
        self._unpack()
        if (l, name) not in self.full:
            assert (l, name) in ALONE_FIRST, (l, name)
            job = self._gather(ALONE_FIRST)
            _run_alone(job[0], name="gather_first")
            self.pending.append(job)
            self._unpack()
        return _matmul_ready(name, self.full[(l, name)])

    def grads(self, l, partials):
        self.parts.update({(l, n): a for n, a in partials.items()})

    def reduce(self):
        self._unpack()
        assert not self.pending and set(self.recv) == set(self.parts)
        out = {}
        for l in range(2):
            done = _reduce_finish([self.parts[(l, n)] for n in BIG], [self.recv[(l, n)] for n in BIG], f"reduce_l{l}")
            out[l] = dict(zip(BIG, done))
        return {n: jnp.stack([out[0][n], out[1][n]], axis=0) for n in BIG}


def _all_reduce_small(rows):
    r = rows.shape[0]

    def body(x_ref, o_ref, buf, send, recv):
        x, y, c = _coords()
        me = 4 * x + 2 * y + c
        buf[me] = x_ref[...]
        copies = []
        for k in range(1, 8):
            peer = (x ^ (k >> 2), y ^ ((k >> 1) & 1), c ^ (k & 1))
            cp = pltpu.make_async_remote_copy(src_ref=x_ref, dst_ref=buf.at[me], send_sem=send.at[k - 1], recv_sem=recv.at[me],
                                              device_id=peer, device_id_type=MESH)
            cp.start()
            copies.append(cp)
        for k in range(1, 8):
            src = 4 * (x ^ (k >> 2)) + 2 * (y ^ ((k >> 1) & 1)) + (c ^ (k & 1))
            pltpu.make_async_remote_copy(src_ref=x_ref, dst_ref=buf.at[src], send_sem=send.at[0], recv_sem=recv.at[src],
                                         device_id=(x, y, c), device_id_type=MESH).wait_recv()
        for cp in copies:
            cp.wait_send()
        acc = buf[0]
        for k in range(1, 8):
            acc = acc + buf[k]
        o_ref[...] = acc

    vm = pl.BlockSpec(memory_space=pltpu.VMEM)
    return pl.pallas_call(
        body, in_specs=[vm], out_specs=vm, out_shape=jax.ShapeDtypeStruct(rows.shape, F32),
        scratch_shapes=[pltpu.VMEM((8, r, D_MODEL), F32), pltpu.SemaphoreType.DMA((7,)), pltpu.SemaphoreType.DMA((8,))],
        name="all_reduce_small")(rows)


def _adamw_math(w, g, m, v):
    m = ADAM_B1 * m + (1.0 - ADAM_B1) * g
    v = ADAM_B2 * v + (1.0 - ADAM_B2) * (g * g)
    m_hat = m / (1.0 - ADAM_B1 ** ADAM_STEP)
    v_hat = v / (1.0 - ADAM_B2 ** ADAM_STEP)
    return -ADAM_LR * (m_hat / (jnp.sqrt(v_hat) + ADAM_EPS) + ADAM_WD * w), m, v


def _adamw(w, g, m, v, name):
    shape = w.shape
    cols = shape[-1]
    flat = lambda a: a.reshape(-1, cols)
    rows = flat(w).shape[0]
    tm = 128 if rows % 128 == 0 else rows
    ins = [('t', flat(a), cols, 0) for a in (w, g, m, v)]
    res = _ew(_adamw_math, ins, [('t', cols, F32)] * 3, rows=rows, tm=tm, name=name)
    return [a.reshape(shape) for a in res]


def _small_update(sums, logits, w, m, v):
    def body(s_ref, lg_ref, w_ref, m_ref, v_ref, g_ref, d_ref, nm_ref, nv_ref):
        s = s_ref[...]
        l0, l1 = lg_ref[0:1, :], lg_ref[1:2, :]
        mx = jnp.maximum(l0, l1)
        e0, e1 = jnp.exp(l0 - mx), jnp.exp(l1 - mx)
        sm0, sm1 = e0 / (e0 + e1), e1 / (e0 + e1)
        dl1 = s_ref[SMALL_ROWS + 2:SMALL_ROWS + 3, :] * sm0 * sm1
        row = lax.broadcasted_iota(jnp.int32, s.shape, 0)
        g = jnp.where(row == 2, -dl1, jnp.where(row == SMALL_ROWS + 2, dl1, s))
        d, nm, nv = _adamw_math(w_ref[...], g, m_ref[...], v_ref[...])
        g_ref[...] = g
        d_ref[...] = d
        nm_ref[...] = nm
        nv_ref[...] = nv

    vm = pl.BlockSpec(memory_space=pltpu.VMEM)
    return pl.pallas_call(body, in_specs=[vm] * 5, out_specs=[vm] * 4,
                          out_shape=[jax.ShapeDtypeStruct(sums.shape, F32)] * 4, name="small_update")(sums, logits, w, m, v)


def _pack_small(vals):
    rows = []
    for l in range(2):
        for n in ("ffn1_norm", "mix_norm", "hgrn_lb_logits", "hgrn_out_norm", "attn_q_norm", "attn_k_norm", "ffn2_norm"):
            a = vals[n][l].reshape(1, -1)
            rows.append(jnp.pad(a, ((0, 0), (0, D_MODEL - a.shape[1]))))
        rows.append(jnp.zeros((SMALL_ROWS - 7, D_MODEL), F32))
    return jnp.concatenate(rows, axis=0)


def _unpack_small(packed):
    out = {}
    for k, n in enumerate(("ffn1_norm", "mix_norm", "hgrn_lb_logits", "hgrn_out_norm", "attn_q_norm", "attn_k_norm", "ffn2_norm")):
        a = jnp.stack([packed[k], packed[SMALL_ROWS + k]], axis=0)
        out[n] = a[:, :ATT_GROUPS * HEAD].reshape(2, ATT_GROUPS, HEAD) if n.startswith("attn") else a
    return out


def kernel(x, ffn1_norm, ffn1_w_in, ffn1_w_out, mix_norm, w_in, hgrn_lb_logits, hgrn_out_norm, attn_q_norm, attn_k_norm, w_branch_a, w_branch_b, w_out, ffn2_norm, ffn2_w_in, ffn2_w_out, loss_target, m_ffn1_norm, m_ffn1_w_in, m_ffn1_w_out, m_mix_norm, m_w_in, m_hgrn_lb_logits, m_hgrn_out_norm, m_attn_q_norm, m_attn_k_norm, m_w_branch_a, m_w_branch_b, m_w_out, m_ffn2_norm, m_ffn2_w_in, m_ffn2_w_out, v_ffn1_norm, v_ffn1_w_in, v_ffn1_w_out, v_mix_norm, v_w_in, v_hgrn_lb_logits, v_hgrn_out_norm, v_attn_q_norm, v_attn_k_norm, v_w_branch_a, v_w_branch_b, v_w_out, v_ffn2_norm, v_ffn2_w_in, v_ffn2_w_out):
    a = locals()
    w = {n: a[n] for n in WEIGHTS}
    m = {n: a["m_" + n] for n in WEIGHTS}
    v = {n: a["v_" + n] for n in WEIGHTS}

    exchange = _Exchange({n: w[n].astype(BF16) for n in BIG})
    small = {n: w[n] for n in SMALL}
    sq, grad_x, small_rows = _local_step(x[0], loss_target[0], small, exchange)
    loss = lax.psum(sq, ("x", "y", "c")) * (0.5 / D_MODEL)
    grads = exchange.reduce()

    sums = _all_reduce_small(small_rows)
    g_s, d_s, m_s, v_s = _small_update(sums, w["hgrn_lb_logits"], _pack_small(small), _pack_small({n: m[n] for n in SMALL}),
                                       _pack_small({n: v[n] for n in SMALL}))
    grads.update(_unpack_small(g_s))
    delta, new_m, new_v = _unpack_small(d_s), _unpack_small(m_s), _unpack_small(v_s)
    for n in BIG:
        delta[n], new_m[n], new_v[n] = _adamw(w[n], grads[n], m[n], v[n], name="adamw_" + n)

    return (loss, grad_x[None], *[grads[n] for n in WEIGHTS], *[delta[n] for n in WEIGHTS],
            *[new_m[n] for n in WEIGHTS], *[new_v[n] for n in WEIGHTS])
```

```python
import functools

import jax
import jax.numpy as jnp
from jax import lax
from jax.experimental import pallas as pl
from jax.experimental.pallas import tpu as pltpu

F32 = jnp.float32
BF16 = jnp.bfloat16
MESH = pl.DeviceIdType.MESH

D_MODEL = 1024
D_FF = 2816
N_CHIPS = 4
HEAD = 128
HG_HEADS = 8
HG_CHUNK = 64
ATT_GROUPS = 3
ATT_HEADS = 4
ATT_GW = ATT_HEADS * HEAD
DILATIONS = (1, 4, 16)
ATT_BLK = 128
ATT_STEP_BLOCKS = 8
P_IN = 10752
CB_AQ, CB_AK, CB_AV, CB_GA, CB_GB = 8, 11, 14, 17, 19
EPS = 1e-6
ROPE_THETA = 10000.0
ADAM_LR, ADAM_B1, ADAM_B2, ADAM_EPS, ADAM_WD, ADAM_STEP = 0.001, 0.9, 0.999, 1e-08, 0.01, 10
VMEM_LIMIT_V7X = 56 * 1024 * 1024
NEG = -1e30


def _params(sem):
    return pltpu.CompilerParams(dimension_semantics=sem, vmem_limit_bytes=VMEM_LIMIT_V7X)


def _sig(x):
    return 1.0 / (1.0 + jnp.exp(-x))


def _dot(a, b):
    return jnp.dot(a, b, preferred_element_type=F32)


def _dot_nt(a, b):
    return lax.dot_general(a, b, (((1,), (1,)), ((), ())), preferred_element_type=F32)


def _dot_tn(a, b):
    return lax.dot_general(a, b, (((0,), (0,)), ((), ())), preferred_element_type=F32)


def _bf(x):
    return x.astype(BF16)


ANY = pl.BlockSpec(memory_space=pl.ANY)


class _Rider:
    def __init__(self, args, out_shape, sems, begin, end):
        self.args, self.out_shape, self.sems, self.begin, self.end = list(args), list(out_shape), list(sems), begin, end
        self.result = None


def _pcall(body, *, grid, in_specs, out_specs, out_shape, name, sem, args, scratch_shapes=(), rider=None):
    multi = isinstance(out_shape, (list, tuple))
    o_specs = list(out_specs) if multi else [out_specs]
    o_shape = list(out_shape) if multi else [out_shape]
    if rider is None:
        res = pl.pallas_call(body, grid=grid, in_specs=list(in_specs), out_specs=o_specs, out_shape=o_shape,
                             scratch_shapes=list(scratch_shapes), name=name, compiler_params=_params(sem))(*args)
        return list(res) if multi else res[0]
    counts = [len(in_specs), len(rider.args), len(o_specs), len(rider.out_shape), len(scratch_shapes)]

    def wrapped(*refs):
        groups, at = [], 0
        for c in counts:
            groups.append(refs[at:at + c])
            at += c
        h_in, r_in, h_out, r_out, h_scratch = groups
        r_sems = refs[at:]
        if grid:
            ids = [pl.program_id(a) for a in range(len(grid))]
            first = functools.reduce(jnp.logical_and, [i == 0 for i in ids])
            last = functools.reduce(jnp.logical_and, [i == g - 1 for i, g in zip(ids, grid)])
            pl.when(first)(lambda: rider.begin(r_in, r_out, r_sems))
            body(*h_in, *h_out, *h_scratch)
            pl.when(last)(lambda: rider.end(r_in, r_out, r_sems))
        else:
            rider.begin(r_in, r_out, r_sems)
            body(*h_in, *h_out, *h_scratch)
            rider.end(r_in, r_out, r_sems)

    res = pl.pallas_call(
        wrapped, grid=grid, in_specs=list(in_specs) + [ANY] * counts[1], out_specs=o_specs + [ANY] * counts[3],
        out_shape=o_shape + rider.out_shape, scratch_shapes=list(scratch_shapes) + rider.sems, name=name,
        compiler_params=_params(("arbitrary",) * len(grid)))(*args, *rider.args)
    rider.result = list(res[counts[2]:])
    return list(res[:counts[2]]) if multi else res[0]


def _mm_nn(a, b3, *, name, tm, tn, out_dtype, res=None, alpha=1.0, rider=None):
    m, k = a.shape
    nb, _, nw = b3.shape
    per = nw // tn
    assert nw % tn == 0 and m % tm == 0
    has_res = res is not None

    def body(*refs):
        if has_res:
            a_ref, b_ref, r_ref, o_ref = refs
        else:
            a_ref, b_ref, o_ref = refs
        acc = _dot(_bf(a_ref[...]), b_ref[...])
        if alpha != 1.0:
            acc = alpha * acc
        if has_res:
            acc = r_ref[...] + acc
        o_ref[...] = acc.astype(o_ref.dtype)

    in_specs = [pl.BlockSpec((tm, k), lambda i, j: (i, 0)),
                pl.BlockSpec((None, k, tn), lambda i, j: (j // per, 0, j % per))]
    args = [a, b3]
    if has_res:
        in_specs.append(pl.BlockSpec((tm, tn), lambda i, j: (i, j)))
        args.append(res)
    return _pcall(body, grid=(m // tm, nb * per), in_specs=in_specs, out_specs=pl.BlockSpec((tm, tn), lambda i, j: (i, j)),
                  out_shape=jax.ShapeDtypeStruct((m, nb * nw), out_dtype), name=name, sem=("parallel", "arbitrary"),
                  args=args, rider=rider)


def _mm_nt(d, b3, *, name, tm, tp, tn, out_dtype, alpha=1.0, rider=None, norm=None):
    m, n = d.shape
    nb, p, nw = b3.shape
    per = nw // tn
    nk = n // tn
    assert nb * nw == n and nw % tn == 0 and p % tp == 0 and m % tm == 0 and (norm is None or tp == p)

    def body(d_ref, b_ref, *refs):
        kk = pl.program_id(2)
        acc_ref = refs[-1]

        @pl.when(kk == 0)
        def _():
            acc_ref[...] = jnp.zeros_like(acc_ref)

        acc_ref[...] += _dot_nt(_bf(d_ref[...]), b_ref[...])

        if norm is None:
            @pl.when(kk == nk - 1)
            def _():
                refs[0][...] = (alpha * acc_ref[...]).astype(refs[0].dtype)
        else:
            x_ref, g_ref, dx_ref, o_ref, dg_ref = refs[:5]

            @pl.when(jnp.logical_and(pl.program_id(0) == 0, kk == 0))
            def _():
                dg_ref[...] = jnp.zeros_like(dg_ref)

            @pl.when(kk == nk - 1)
            def _():
                dh = alpha * acc_ref[...]
                xv = x_ref[...]
                r = _rms_rows(xv)
                xh = xv * r
                dxh = dh * g_ref[...]
                o_ref[...] = dx_ref[...] + r * (dxh - xh * jnp.mean(dxh * xh, axis=1, keepdims=True))
                dg_ref[...] += jnp.sum(dh * xh, axis=0, keepdims=True)

    in_specs = [pl.BlockSpec((tm, tn), lambda i, j, kk: (i, kk)),
                pl.BlockSpec((None, tp, tn), lambda i, j, kk: (kk // per, j, kk % per))]
    tile = pl.BlockSpec((tm, tp), lambda i, j, kk: (i, j))
    if norm is None:
        return _pcall(body, grid=(m // tm, p // tp, nk), in_specs=in_specs, out_specs=tile,
                      out_shape=jax.ShapeDtypeStruct((m, p), out_dtype), scratch_shapes=[pltpu.VMEM((tm, tp), F32)],
                      name=name, sem=("parallel", "parallel", "arbitrary"), args=(d, b3), rider=rider)
    x, g, dx = norm
    row = pl.BlockSpec((1, p), lambda i, j, kk: (0, 0))
    return _pcall(body, grid=(m // tm, 1, nk), in_specs=in_specs + [tile, row, tile], out_specs=[tile, row],
                  out_shape=[jax.ShapeDtypeStruct((m, p), F32), jax.ShapeDtypeStruct((1, p), F32)],
                  scratch_shapes=[pltpu.VMEM((tm, tp), F32)], name=name, sem=("arbitrary", "arbitrary", "arbitrary"),
                  args=(d, b3, x, g, dx), rider=rider)


def _mm_tn(a, d, *, nb, name, tm, tk, tn, alpha=1.0, rider=None):
    m, k = a.shape
    _, n = d.shape
    nw = n // nb
    per = nw // tn
    nm = m // tm
    assert nw % tn == 0 and k % tk == 0 and m % tm == 0

    def body(a_ref, d_ref, o_ref, acc_ref):
        mm = pl.program_id(2)

        @pl.when(mm == 0)
        def _():
            acc_ref[...] = jnp.zeros_like(acc_ref)

        acc_ref[...] += _dot_tn(_bf(a_ref[...]), _bf(d_ref[...]))

        @pl.when(mm == nm - 1)
        def _():
            o_ref[...] = (alpha * acc_ref[...]).astype(o_ref.dtype)

    return _pcall(
        body, grid=(k // tk, nb * per, nm),
        in_specs=[pl.BlockSpec((tm, tk), lambda i, j, mm: (mm, i)),
                  pl.BlockSpec((tm, tn), lambda i, j, mm: (mm, j))],
        out_specs=pl.BlockSpec((None, tk, tn), lambda i, j, mm: (j // per, i, j % per)),
        out_shape=jax.ShapeDtypeStruct((nb, k, nw), BF16),
        scratch_shapes=[pltpu.VMEM((tk, tn), F32)],
        name=name, sem=("parallel", "parallel", "arbitrary"), args=(a, d), rider=rider)


def _rows_from_view(ref, buf, w, d, tm):
    for k in range(d):
        for c in range(w // HEAD):
            lanes = slice(k * w + c * HEAD, k * w + (c + 1) * HEAD)
            buf.at[c][pl.ds(k, tm // d, stride=d), :] = ref[:, lanes].astype(F32)
    return _cat([buf[c] for c in range(w // HEAD)])


def _ew(fn, ins, outs, *, rows, tm, name):
    in_specs, args, scratch = [], [], []
    for s in ins:
        if s[0] == 't':
            _, arr, w, cb = s
            in_specs.append(pl.BlockSpec((tm, w), lambda i, cb=cb: (i, cb)))
        elif s[0] == 'v':
            _, arr, w, d = s
            in_specs.append(pl.BlockSpec((tm // d, d * w), lambda i: (i, 0)))
            scratch.append(pltpu.VMEM((w // HEAD, tm, HEAD), F32))
        else:
            arr = s[1]
            in_specs.append(pl.BlockSpec(arr.shape, lambda i, nd=arr.ndim: (0,) * nd))
        args.append(arr)
    out_specs, out_shape = [], []
    for s in outs:
        if s[0] == 't':
            _, w, dt = s
            out_specs.append(pl.BlockSpec((tm, w), lambda i: (i, 0)))
            out_shape.append(jax.ShapeDtypeStruct((rows, w), dt))
        elif s[0] == 'v':
            _, w, dt, d = s
            out_specs.append(pl.BlockSpec((tm // d, d * w), lambda i: (i, 0)))
            out_shape.append(jax.ShapeDtypeStruct((rows // d, d * w), dt))
            scratch.append(pltpu.VMEM((w // HEAD, tm, HEAD), F32))
        else:
            out_specs.append(pl.BlockSpec(s[1], lambda i: (0, 0)))
            out_shape.append(jax.ShapeDtypeStruct(s[1], F32))
    n_in, n_out = len(ins), len(outs)

    def body(*refs):
        bufs = list(refs[n_in + n_out:])
        vals = []
        for r, s in zip(refs[:n_in], ins):
            if s[0] == 'v':
                vals.append(_rows_from_view(r, bufs.pop(0), s[2], s[3], tm))
            else:
                vals.append(r[...])
        res = fn(*vals)
        if not isinstance(res, (tuple, list)):
            res = (res,)
        for r, s, v in zip(refs[n_in:n_in + n_out], outs, res):
            if s[0] == 't':
                r[...] = v.astype(r.dtype)
            elif s[0] == 'v':
                w, d, buf = s[1], s[3], bufs.pop(0)
                for c in range(w // HEAD):
                    buf[c] = v[:, c * HEAD:(c + 1) * HEAD].astype(F32)
                for k in range(d):
                    for c in range(w // HEAD):
                        lanes = slice(k * w + c * HEAD, k * w + (c + 1) * HEAD)
                        r[:, lanes] = buf.at[c][pl.ds(k, tm // d, stride=d), :].astype(r.dtype)
            else:
                @pl.when(pl.program_id(0) == 0)
                def _(r=r):
                    r[...] = jnp.zeros_like(r)

                r[...] += v

    res = pl.pallas_call(
        body, grid=(rows // tm,), in_specs=in_specs, out_specs=out_specs, out_shape=out_shape, scratch_shapes=scratch,
        name=name, compiler_params=_params(("arbitrary",)))(*args)
    return res


def _tile(arr, w, g):
    return ('t', arr, w, 0) if DILATIONS[g] == 1 else ('v', arr, w, DILATIONS[g])


def _tile_out(w, dtype, g):
    return ('t', w, dtype) if DILATIONS[g] == 1 else ('v', w, dtype, DILATIONS[g])


def _heads(x):
    return [x[:, h * HEAD:(h + 1) * HEAD] for h in range(x.shape[1] // HEAD)]


def _cat(xs):
    return jnp.concatenate(xs, axis=1)


def _head_mean(x):
    return _cat([jnp.broadcast_to(jnp.mean(h, axis=1, keepdims=True), h.shape) for h in _heads(x)])


def _rms_rows(x):
    return lax.rsqrt(jnp.mean(x * x, axis=1, keepdims=True) + EPS)


def _norm_fwd(x, g, name):
    return _ew(lambda xv, gv: xv * _rms_rows(xv) * gv,
               [('t', x, D_MODEL, 0), ('f', g)], [('t', D_MODEL, BF16)], rows=x.shape[0], tm=512, name=name)[0]


def _loss_fwd_bwd(y, target, name):
    def fn(yv, tv):
        e = yv - tv
        return e * (1.0 / D_MODEL), jnp.sum(e * e, axis=0, keepdims=True)

    return _ew(fn, [('t', y, D_MODEL, 0), ('t', target, D_MODEL, 0)], [('t', D_MODEL, F32), ('acc', (1, D_MODEL))],
               rows=y.shape[0], tm=512, name=name)


def _rot(x):
    sgn = jnp.where(lax.broadcasted_iota(jnp.int32, x.shape, 1) < HEAD // 2, -1.0, 1.0)
    return pltpu.roll(x, HEAD // 2, 1) * sgn


def _gain_rows(qn, kn):
    return [a[g:g + 1] for a in (qn, kn) for g in range(ATT_GROUPS)]


def _qk_fwd(proj, cos, sin, qn, kn, name):
    def fn(*v):
        xs, cosv, sinv, gains, vs = v[:6], v[6], v[7], v[8:14], v[14:17]
        outs = []
        for j, x in enumerate(xs):
            gain = gains[j]
            ys = []
            for xh in _heads(x.astype(F32)):
                xn = xh * _rms_rows(xh) * gain
                ys.append(xn * cosv + _rot(xn) * sinv)
            outs.append(_cat(ys))
        return outs + list(vs)

    ins = ([('t', proj, 512, CB_AQ + j) for j in range(6)] + [('t', cos, HEAD, 0), ('t', sin, HEAD, 0)]
           + [('f', a) for a in _gain_rows(qn, kn)] + [('t', proj, 512, CB_AV + g) for g in range(ATT_GROUPS)])
    return _ew(fn, ins, [_tile_out(ATT_GW, BF16, j % ATT_GROUPS) for j in range(9)], rows=proj.shape[0], tm=512, name=name)


def _qk_bwd(dqk, proj, cos, sin, qn, kn, name):
    def fn(*v):
        ds, xs, cosv, sinv, gains = v[:6], v[6:12], v[12], v[13], v[14:20]
        rows8 = lax.broadcasted_iota(jnp.int32, (8, HEAD), 0)
        outs, dgs = [], [jnp.zeros((8, HEAD), F32)] * 2
        for j in range(6):
            gain = gains[j]
            dx, dg = [], jnp.zeros((1, HEAD), F32)
            for dyh, xh in zip(_heads(ds[j]), _heads(xs[j].astype(F32))):
                r = _rms_rows(xh)
                xhat = xh * r
                dxn = dyh * cosv - _rot(dyh * sinv)
                dg = dg + jnp.sum(dxn * xhat, axis=0, keepdims=True)
                dxh = dxn * gain
                dx.append(r * (dxh - xhat * jnp.mean(dxh * xhat, axis=1, keepdims=True)))
            outs.append(_cat(dx))
            dgs[j // 3] = dgs[j // 3] + jnp.where(rows8 == j % 3, dg, 0.0)
        return _cat(outs), dgs[0], dgs[1]

    ins = ([_tile(a, ATT_GW, j % ATT_GROUPS) for j, a in enumerate(dqk)] + [('t', proj, 512, CB_AQ + j) for j in range(6)]
           + [('t', cos, HEAD, 0), ('t', sin, HEAD, 0)] + [('f', a) for a in _gain_rows(qn, kn)])
    return _ew(fn, ins, [('t', 6 * ATT_GW, BF16), ('acc', (8, HEAD)), ('acc', (8, HEAD))],
               rows=proj.shape[0], tm=256, name=name)


def _pick(x, h):
    lanes = lax.broadcasted_iota(jnp.int32, x.shape, 1)
    return jnp.sum(jnp.where(lanes == h, x, 0.0), axis=1, keepdims=True)


def _spread(x):
    return _cat([jnp.broadcast_to(_pick(x, h), (x.shape[0], HEAD)) for h in range(ATT_HEADS)])


def _compact(x):
    lanes = lax.broadcasted_iota(jnp.int32, (x.shape[0], HEAD), 1)
    out = jnp.zeros((x.shape[0], HEAD), F32)
    for h, xh in enumerate(_heads(x)):
        out = jnp.where(lanes == h, xh, out)
    return out


def _group_weights(l0, l1, l2):
    l0, l1, l2 = _spread(l0), _spread(l1), _spread(l2)
    m = jnp.maximum(jnp.maximum(l0, l1), l2)
    e0, e1, e2 = jnp.exp(l0 - m), jnp.exp(l1 - m), jnp.exp(l2 - m)
    inv = 1.0 / (e0 + e1 + e2)
    return e0 * inv, e1 * inv, e2 * inv


def _merge_fwd(outs, lses, name):
    def fn(o0, o1, o2, l0, l1, l2):
        a0, a1, a2 = _group_weights(l0, l1, l2)
        return a0 * o0 + a1 * o1 + a2 * o2

    ins = [_tile(a, ATT_GW, g) for g, a in enumerate(outs)] + [_tile(a, HEAD, g) for g, a in enumerate(lses)]
    return _ew(fn, ins, [('t', ATT_GW, BF16)], rows=outs[0].shape[0], tm=512, name=name)[0]


def _merge_bwd(dob, outs, lses, name):
    def fn(dov, o0, o1, o2, l0, l1, l2):
        a0, a1, a2 = _group_weights(l0, l1, l2)
        ob = a0 * o0 + a1 * o1 + a2 * o2
        s = _head_mean(dov * ob) * float(HEAD)
        return a0 * dov, a1 * dov, a2 * dov, _compact(a0 * s), _compact(a1 * s), _compact(a2 * s)

    ins = ([('t', dob, ATT_GW, 0)] + [_tile(a, ATT_GW, g) for g, a in enumerate(outs)]
           + [_tile(a, HEAD, g) for g, a in enumerate(lses)])
    groups = range(ATT_GROUPS)
    return _ew(fn, ins, [_tile_out(ATT_GW, BF16, g) for g in groups] + [_tile_out(HEAD, F32, g) for g in groups],
               rows=dob.shape[0], tm=512, name=name)


HG_ROWS = 256


def _hg_gates(hq, hf, hi, lbv):
    sig = _sig(hf)
    f = lbv + (1.0 - lbv) * sig
    return hq * _sig(hq), 1.0 - f, hi, jnp.log(f), sig, f


def _split3(x):
    hi = _bf(x)
    r1 = x - hi.astype(F32)
    mid = _bf(r1)
    return hi, mid, _bf(r1 - mid.astype(F32))


def _tri_dot(tri, x):
    hi, mid, lo = _split3(x)
    return _dot(tri, hi) + _dot(tri, mid) + _dot(tri, lo)


def _row(x, i):
    rows = lax.broadcasted_iota(jnp.int32, x.shape, 0)
    return jnp.sum(jnp.where(rows == i, x, 0.0), axis=0, keepdims=True)


def _hg_decay(logf, q, k):
    c = HG_CHUNK
    row = lax.broadcasted_iota(jnp.int32, (c, c), 0)
    col = lax.broadcasted_iota(jnp.int32, (c, c), 1)
    g = _tri_dot((row >= col).astype(BF16), logf)
    gm = _row(g, c // 2 - 1)
    gl = _row(g, c - 1)
    decays = jnp.exp(g), jnp.exp(g - gm), jnp.exp(gm - g), jnp.exp(gl - g)
    return gl, decays, q * decays[0], q * decays[1], k * decays[2], k * decays[3]


def _hg_out_fwd(o, hg, gain):
    r = lax.rsqrt(_head_mean(o * o) + EPS)
    return o * r * gain * (hg * _sig(hg))


def _hgrn_fwd(proj, hf, lb, gain, name, rider=None):
    t = proj.shape[0]
    nck = HG_ROWS // HG_CHUNK

    def body(hq_ref, hf_ref, hi_ref, hg_ref, lb_ref, gn_ref, o_ref, oa_ref, sall_ref, st_ref):
        @pl.when(pl.program_id(0) == 0)
        def _():
            st_ref[...] = jnp.zeros_like(st_ref)

        lbv = lb_ref[...]
        gnv = gn_ref[...]
        c = HG_CHUNK
        mask = lax.broadcasted_iota(jnp.int32, (c, c), 0) >= lax.broadcasted_iota(jnp.int32, (c, c), 1)

        def chunk(cc, carry):
            sl = pl.ds(pl.multiple_of(cc * c, c), c)
            q, k, v, logf, _, _ = _hg_gates(hq_ref[sl, :].astype(F32), hf_ref[sl, :], hi_ref[sl, :].astype(F32), lbv)
            gl, _, qg, qt, kt, kd = _hg_decay(logf, q, k)
            egl = jnp.exp(gl)
            os = []
            for h in range(HG_HEADS):
                hs = slice(h * HEAD, (h + 1) * HEAD)
                st = st_ref[h]
                sall_ref[cc, h] = st
                a = jnp.where(mask, _dot_nt(_bf(qt[:, hs]), _bf(kt[:, hs])), 0.0)
                os.append(_dot(_bf(a), _bf(v[:, hs])) + _dot_nt(_bf(qg[:, hs]), _bf(st)))
                st_ref[h] = egl[:, hs] * st + _dot_tn(_bf(v[:, hs]), _bf(kd[:, hs]))
            o = _cat(os)
            o_ref[sl, :] = o
            oa_ref[sl, :] = _hg_out_fwd(o, hg_ref[sl, :].astype(F32), gnv).astype(oa_ref.dtype)
            return carry

        lax.fori_loop(0, nck, chunk, 0)

    col = lambda j: pl.BlockSpec((HG_ROWS, D_MODEL), lambda i, j=j: (i, j))
    small = pl.BlockSpec((1, D_MODEL), lambda i: (0, 0))
    return _pcall(
        body, grid=(t // HG_ROWS,),
        in_specs=[col(0), col(0), col(2), col(3), small, small],
        out_specs=[col(0), col(0), pl.BlockSpec((nck, HG_HEADS, HEAD, HEAD), lambda i: (i, 0, 0, 0))],
        out_shape=[jax.ShapeDtypeStruct((t, D_MODEL), F32), jax.ShapeDtypeStruct((t, D_MODEL), BF16),
                   jax.ShapeDtypeStruct((t // HG_CHUNK, HG_HEADS, HEAD, HEAD), F32)],
        scratch_shapes=[pltpu.VMEM((HG_HEADS, HEAD, HEAD), F32)],
        name=name, sem=("arbitrary",), args=(proj, hf, proj, proj, lb, gain), rider=rider)


def _terms(x, precise):
    hi = _bf(x)
    return (hi, _bf(x - hi.astype(F32))) if precise else (hi,)


def _mm(dot, a, b):
    out = dot(a[0], b[0])
    if len(a) > 1:
        out = out + dot(a[1], b[0])
    if len(b) > 1:
        out = out + dot(a[0], b[1])
    return out


def _hgrn_bwd(doa, oscan, proj, hf, sall, lb, gain, dqk, dvs, dgab, name, precise, rider=None):
    t = proj.shape[0]
    nck = HG_ROWS // HG_CHUNK
    nsteps = t // HG_ROWS
    terms = functools.partial(_terms, precise=precise)
    n_view = sum(d > 1 for d in DILATIONS)

    def body(doa_ref, os_ref, hq_ref, hf_ref, hi_ref, hg_ref, sall_ref, lb_ref, gn_ref, dqk_ref, dv0_ref, dv1_ref,
             dv2_ref, dgab_ref, dproj_ref, dgn_ref, dlb_ref, dst_ref, *bufs):
        @pl.when(pl.program_id(0) == 0)
        def _():
            dst_ref[...] = jnp.zeros_like(dst_ref)
            dgn_ref[...] = jnp.zeros_like(dgn_ref)
            dlb_ref[...] = jnp.zeros_like(dlb_ref)

        at = 4 * D_MODEL
        dproj_ref[:, at:at + 6 * ATT_GW] = dqk_ref[...]
        at += 6 * ATT_GW
        spare = list(bufs)
        for d, dv_ref in zip(DILATIONS, (dv0_ref, dv1_ref, dv2_ref)):
            dv = dv_ref[...] if d == 1 else _rows_from_view(dv_ref, spare.pop(0), ATT_GW, d, HG_ROWS)
            dproj_ref[:, at:at + ATT_GW] = dv.astype(dproj_ref.dtype)
            at += ATT_GW
        dproj_ref[:, at:] = dgab_ref[...]

        lbv = lb_ref[...]
        gnv = gn_ref[...]
        c = HG_CHUNK
        row = lax.broadcasted_iota(jnp.int32, (c, c), 0)
        colm = lax.broadcasted_iota(jnp.int32, (c, c), 1)
        mask = row >= colm
        triu = (row <= colm).astype(BF16)
        last = lax.broadcasted_iota(jnp.int32, (c, HEAD), 0) == c - 1

        def chunk(ci, carry):
            cc = nck - 1 - ci
            sl = pl.ds(pl.multiple_of(cc * c, c), c)
            hq, hg = hq_ref[sl, :].astype(F32), hg_ref[sl, :].astype(F32)
            q, k, v, logf, sig, f = _hg_gates(hq, hf_ref[sl, :], hi_ref[sl, :].astype(F32), lbv)
            gl, (e_qg, e_qt, e_kt, e_kd), qg, qt, kt, kd = _hg_decay(logf, q, k)
            egl = jnp.exp(gl)
            o = os_ref[sl, :]
            dy = doa_ref[sl, :]
            r = lax.rsqrt(_head_mean(o * o) + EPS)
            oh = o * r
            sg = _sig(hg)
            silu_g = hg * sg
            dgn_ref[...] += jnp.sum(dy * oh * silu_g, axis=0, keepdims=True)
            dhg = dy * oh * gnv * (sg * (1.0 + hg * (1.0 - sg)))
            doh = dy * gnv * silu_g
            do = r * (doh - oh * _head_mean(doh * oh))
            dqs, dks, dvs, dgs = [], [], [], []
            for h in range(HG_HEADS):
                hs = slice(h * HEAD, (h + 1) * HEAD)
                st = sall_ref[cc, h]
                dst = dst_ref[h]
                qt_h, kt_h, qg_h, kd_h = qt[:, hs], kt[:, hs], qg[:, hs], kd[:, hs]
                do_p, v_p, qt_p, kt_p, qg_p = terms(do[:, hs]), terms(v[:, hs]), terms(qt_h), terms(kt_h), terms(qg_h)
                st_p, dst_p = terms(st), terms(dst)
                a = jnp.where(mask, _dot_nt(qt_p[0], kt_p[0]), 0.0)
                da = terms(jnp.where(mask, _mm(_dot_nt, do_p, v_p), 0.0))
                dqt = _mm(_dot, da, kt_p)
                dkt = _mm(_dot_tn, da, qt_p)
                dqg = _mm(_dot, do_p, st_p)
                dv = _dot_tn(_bf(a), do_p[0]) + _dot_nt(_bf(kd_h), dst_p[0])
                dkd = _mm(_dot, v_p, dst_p)
                dgl = egl[:, hs] * jnp.sum(st * dst, axis=0, keepdims=True) + jnp.sum(dkd * kd_h, axis=0, keepdims=True)
                dst_ref[h] = egl[:, hs] * dst + _mm(_dot_tn, do_p, qg_p)
                dqs.append(dqt * e_qt[:, hs] + dqg * e_qg[:, hs])
                dks.append(dkt * e_kt[:, hs] + dkd * e_kd[:, hs])
                dvs.append(dv)
                dgs.append(dqt * qt_h - dkt * kt_h + dqg * qg_h - dkd * kd_h + jnp.where(last, dgl, 0.0))
            dq, dk, dv, dg = _cat(dqs), _cat(dks), _cat(dvs), _cat(dgs)
            dlogf = _tri_dot(triu, dg)
            df = dlogf / f - dk
            dlb_ref[...] += jnp.sum(df * (1.0 - sig), axis=0, keepdims=True)
            dhf = df * (1.0 - lbv) * sig * (1.0 - sig)
            sq = _sig(hq)
            dhq = dq * (sq * (1.0 + hq * (1.0 - sq)))
            dproj_ref[sl, :4 * D_MODEL] = _cat([dhq, dhf, dv, dhg]).astype(dproj_ref.dtype)
            return carry

        lax.fori_loop(0, nck, chunk, 0)

    rev = lambda j: pl.BlockSpec((HG_ROWS, D_MODEL), lambda i, j=j: (nsteps - 1 - i, j))
    rows = lambda a, d=1: pl.BlockSpec((HG_ROWS // d, a.shape[1]), lambda i: (nsteps - 1 - i, 0))
    small = pl.BlockSpec((1, D_MODEL), lambda i: (0, 0))
    return _pcall(
        body, grid=(nsteps,),
        in_specs=[rev(0), rev(0), rev(0), rev(0), rev(2), rev(3),
                  pl.BlockSpec((nck, HG_HEADS, HEAD, HEAD), lambda i: (nsteps - 1 - i, 0, 0, 0)), small, small,
                  rows(dqk)] + [rows(a, d) for a, d in zip(dvs, DILATIONS)] + [rows(dgab)],
        out_specs=[pl.BlockSpec((HG_ROWS, P_IN), lambda i: (nsteps - 1 - i, 0)), small, small],
        out_shape=[jax.ShapeDtypeStruct((t, P_IN), BF16), jax.ShapeDtypeStruct((1, D_MODEL), F32),
                   jax.ShapeDtypeStruct((1, D_MODEL), F32)],
        scratch_shapes=[pltpu.VMEM((HG_HEADS, HEAD, HEAD), F32)] + [pltpu.VMEM((ATT_HEADS, HG_ROWS, HEAD), F32)] * n_view,
        name=name, sem=("arbitrary",), args=(doa, oscan, proj, hf, proj, proj, sall, lb, gain, dqk, *dvs, dgab),
        rider=rider)


def _window_masks(has_previous):
    qi = lax.broadcasted_iota(jnp.int32, (ATT_BLK, 2 * ATT_BLK), 0)
    ki = lax.broadcasted_iota(jnp.int32, (ATT_BLK, 2 * ATT_BLK), 1)
    band = jnp.logical_and(ki >= qi, ki <= qi + ATT_BLK)
    return band, jnp.logical_and(band, jnp.logical_or(ki >= ATT_BLK, has_previous))


def _two_blocks(ref, prev_ref, j, hs):
    if j == 0:
        return jnp.concatenate([prev_ref[:, hs], ref[0:ATT_BLK, hs]], axis=0)
    return ref[(j - 1) * ATT_BLK:(j + 1) * ATT_BLK, hs]


def _attn_cfg(qg, g):
    d = DILATIONS[g]
    length = qg.shape[0]
    assert qg.shape[1] == d * ATT_GW
    nb = length // ATT_BLK
    return d, length, nb, min(ATT_STEP_BLOCKS, nb)


def _attn_fwd(qg, kg, vg, g, name):
    d, length, nb, rb = _attn_cfg(qg, g)
    scale = HEAD ** -0.5

    def body(q_ref, k_ref, v_ref, kp_ref, vp_ref, o_ref, l_ref):
        n = pl.program_id(1)
        band, first_band = _window_masks(n > 0)
        lanes = lax.broadcasted_iota(jnp.int32, (ATT_BLK, HEAD), 1)
        for j in range(rb):
            rows = slice(j * ATT_BLK, (j + 1) * ATT_BLK)
            lse = jnp.zeros((ATT_BLK, HEAD), F32)
            for h in range(ATT_HEADS):
                hs = slice(h * HEAD, (h + 1) * HEAD)
                k2, v2 = _two_blocks(k_ref, kp_ref, j, hs), _two_blocks(v_ref, vp_ref, j, hs)
                s = jnp.where(first_band if j == 0 else band, _dot_nt(q_ref[rows, hs], k2) * scale, NEG)
                m = jnp.max(s, axis=1, keepdims=True)
                p = jnp.exp(s - m)
                l = jnp.sum(p, axis=1, keepdims=True)
                o_ref[rows, hs] = (_dot(_bf(p), v2) / l).astype(o_ref.dtype)
                lse = jnp.where(lanes == h, m + jnp.log(l), lse)
            l_ref[rows, :] = lse

    own = pl.BlockSpec((rb * ATT_BLK, ATT_GW), lambda r, n: (n, r))
    own_head = pl.BlockSpec((rb * ATT_BLK, HEAD), lambda r, n: (n, r))
    prev = pl.BlockSpec((ATT_BLK, ATT_GW), lambda r, n: (jnp.maximum(n * rb - 1, 0), r))
    return pl.pallas_call(
        body, grid=(d, nb // rb), in_specs=[own, own, own, prev, prev], out_specs=[own, own_head],
        out_shape=[jax.ShapeDtypeStruct((length, d * ATT_GW), BF16), jax.ShapeDtypeStruct((length, d * HEAD), F32)],
        name=name, compiler_params=_params(("parallel", "arbitrary")))(qg, kg, vg, kg, vg)


def _attn_bwd(qg, kg, vg, dog, lse, delta, g, name):
    d, length, nb, rb = _attn_cfg(qg, g)
    nsteps = nb // rb
    scale = HEAD ** -0.5

    def body(q_ref, k_ref, v_ref, do_ref, l_ref, dl_ref, kp_ref, vp_ref, qn_ref, don_ref, ln_ref, dln_ref,
             dq_ref, dk_ref, dv_ref):
        n = pl.program_id(1)
        band, first_band = _window_masks(n > 0)
        qi = lax.broadcasted_iota(jnp.int32, (ATT_BLK, ATT_BLK), 0)
        ki = lax.broadcasted_iota(jnp.int32, (ATT_BLK, ATT_BLK), 1)
        next_m = jnp.logical_and(ki >= qi, n < nsteps - 1)
        last = slice((rb - 1) * ATT_BLK, rb * ATT_BLK)
        for h in range(ATT_HEADS):
            hs = slice(h * HEAD, (h + 1) * HEAD)
            dk, dv = [None] * rb, [None] * rb
            for j in range(rb):
                rows = slice(j * ATT_BLK, (j + 1) * ATT_BLK)
                q, do = q_ref[rows, hs], do_ref[rows, hs]
                k2, v2 = _two_blocks(k_ref, kp_ref, j, hs), _two_blocks(v_ref, vp_ref, j, hs)
                p = jnp.where(first_band if j == 0 else band,
                              jnp.exp(_dot_nt(q, k2) * scale - _pick(l_ref[rows, :], h)), 0.0)
                ds = _bf(p * (_dot_nt(do, v2) - _pick(dl_ref[rows, :], h)) * scale)
                dq_ref[rows, hs] = _dot(ds, k2).astype(dq_ref.dtype)
                dk2, dv2 = _dot_tn(ds, q), _dot_tn(_bf(p), do)
                if j >= 1:
                    dk[j - 1] = dk[j - 1] + dk2[:ATT_BLK]
                    dv[j - 1] = dv[j - 1] + dv2[:ATT_BLK]
                dk[j], dv[j] = dk2[ATT_BLK:], dv2[ATT_BLK:]
            q, do = qn_ref[:, hs], don_ref[:, hs]
            p = jnp.where(next_m, jnp.exp(_dot_nt(q, k_ref[last, hs]) * scale - _pick(ln_ref[...], h)), 0.0)
            ds = _bf(p * (_dot_nt(do, v_ref[last, hs]) - _pick(dln_ref[...], h)) * scale)
            dk[rb - 1] = dk[rb - 1] + _dot_tn(ds, q)
            dv[rb - 1] = dv[rb - 1] + _dot_tn(_bf(p), do)
            for j in range(rb):
                rows = slice(j * ATT_BLK, (j + 1) * ATT_BLK)
                dk_ref[rows, hs] = dk[j].astype(dk_ref.dtype)
                dv_ref[rows, hs] = dv[j].astype(dv_ref.dtype)

    own = pl.BlockSpec((rb * ATT_BLK, ATT_GW), lambda r, n: (n, r))
    prev = pl.BlockSpec((ATT_BLK, ATT_GW), lambda r, n: (jnp.maximum(n * rb - 1, 0), r))
    nxt = pl.BlockSpec((ATT_BLK, ATT_GW), lambda r, n: (jnp.minimum((n + 1) * rb, nb - 1), r))
    own_head = pl.BlockSpec((rb * ATT_BLK, HEAD), lambda r, n: (n, r))
    nxt_head = pl.BlockSpec((ATT_BLK, HEAD), lambda r, n: (jnp.minimum((n + 1) * rb, nb - 1), r))
    return pl.pallas_call(
        body, grid=(d, nsteps), in_specs=[own] * 4 + [own_head] * 2 + [prev, prev, nxt, nxt, nxt_head, nxt_head],
        out_specs=[own, own, own], out_shape=[jax.ShapeDtypeStruct((length, d * ATT_GW), BF16)] * 3,
        name=name, compiler_params=_params(("parallel", "arbitrary")))(
            qg, kg, vg, dog, lse, delta, kg, vg, qg, dog, lse, delta)


def _rope_tables(t):
    pos = jnp.arange(t, dtype=F32)
    inv = ROPE_THETA ** (-jnp.arange(0, HEAD, 2, dtype=F32) / HEAD)
    ang = pos[:, None] * inv[None, :]
    ang = jnp.concatenate([ang, ang], axis=-1)
    return jnp.cos(ang), jnp.sin(ang)


def _lower_bounds(logits):
    lb = jnp.cumsum(jax.nn.softmax(logits.astype(F32), axis=0), axis=0)
    return lb - lb[0:1]


FFN_ROWS = 256
FF_SHARD = 2 * D_FF // N_CHIPS


def _ffn_in_act(x, g, w_in, name, rider=None):
    t = x.shape[0]

    def body(x_ref, g_ref, w_ref, h_ref, ab_ref, u_ref):
        xv = x_ref[...]
        h = _bf(xv * _rms_rows(xv) * g_ref[...])
        h_ref[...] = h
        for s in range(N_CHIPS // 2):
            cols = slice(s * FF_SHARD, (s + 1) * FF_SHARD)
            a = _dot(h, w_ref[s])
            b = _dot(h, w_ref[s + N_CHIPS // 2])
            ab_ref[:, cols] = a.astype(ab_ref.dtype)
            ab_ref[:, D_FF + s * FF_SHARD:D_FF + (s + 1) * FF_SHARD] = b.astype(ab_ref.dtype)
            u_ref[:, cols] = (a * _sig(a) * b).astype(u_ref.dtype)

    row = lambda w: pl.BlockSpec((FFN_ROWS, w), lambda i: (i, 0))
    return _pcall(
        body, grid=(t // FFN_ROWS,),
        in_specs=[row(D_MODEL), pl.BlockSpec((1, D_MODEL), lambda i: (0, 0)),
                  pl.BlockSpec(w_in.shape, lambda i: (0, 0, 0))],
        out_specs=[row(D_MODEL), row(2 * D_FF), row(D_FF)],
        out_shape=[jax.ShapeDtypeStruct((t, D_MODEL), BF16), jax.ShapeDtypeStruct((t, 2 * D_FF), BF16),
                   jax.ShapeDtypeStruct((t, D_FF), BF16)],
        name=name, sem=("parallel",), args=(x, g, w_in), rider=rider)


def _ffn_bwd_du_act(dx, w_out, ab, name, rider=None):
    t = dx.shape[0]

    def body(dx_ref, w_ref, ab_ref, o_ref):
        du = 0.5 * _dot_nt(_bf(dx_ref[...]), w_ref[0])
        a = ab_ref[:, :D_FF].astype(F32)
        b = ab_ref[:, D_FF:].astype(F32)
        s = _sig(a)
        o_ref[:, :D_FF] = (du * b * (s * (1.0 + a * (1.0 - s)))).astype(o_ref.dtype)
        o_ref[:, D_FF:] = (du * a * s).astype(o_ref.dtype)

    row = lambda w: pl.BlockSpec((FFN_ROWS, w), lambda i: (i, 0))
    return _pcall(
        body, grid=(t // FFN_ROWS,),
        in_specs=[row(D_MODEL), pl.BlockSpec(w_out.shape, lambda i: (0, 0, 0)), row(2 * D_FF)],
        out_specs=row(2 * D_FF), out_shape=jax.ShapeDtypeStruct((t, 2 * D_FF), BF16),
        name=name, sem=("parallel",), args=(dx, w_out, ab), rider=rider)


MIX_ROWS = 512


def _gate_specs():
    return [pl.BlockSpec((MIX_ROWS, 512), lambda i, cb=cb: (i, cb)) for cb in (CB_GA, CB_GA + 1, CB_GB, CB_GB + 1)]


def _gate(lo_ref, hi_ref):
    return _sig(_cat([lo_ref[...], hi_ref[...]]).astype(F32))


def _whole(a):
    return pl.BlockSpec(a.shape, lambda i: (0,) * a.ndim)


def _mix_tail_fwd(oa, ob, proj, x, w_a, w_b, w_o, name):
    t = x.shape[0]

    def body(oa_ref, ob_ref, ga0, ga1, gb0, gb1, x_ref, wa_ref, wb_ref, wo_ref, y_ref, m_ref, ya_ref, yb_ref):
        ya = _dot(oa_ref[...], wa_ref[0])
        yb = _cat([_dot(ob_ref[...], wb_ref[s]) for s in range(N_CHIPS)])
        merged = _bf(_gate(ga0, ga1) * ya + _gate(gb0, gb1) * yb)
        m_ref[...] = merged
        ya_ref[...] = ya.astype(ya_ref.dtype)
        yb_ref[...] = yb.astype(yb_ref.dtype)
        y_ref[...] = x_ref[...] + _dot(merged, wo_ref[0])

    row = lambda w: pl.BlockSpec((MIX_ROWS, w), lambda i: (i, 0))
    return pl.pallas_call(
        body, grid=(t // MIX_ROWS,),
        in_specs=[row(D_MODEL), row(ATT_GW)] + _gate_specs() + [row(D_MODEL), _whole(w_a), _whole(w_b), _whole(w_o)],
        out_specs=[row(D_MODEL)] * 4,
        out_shape=[jax.ShapeDtypeStruct((t, D_MODEL), F32)] + [jax.ShapeDtypeStruct((t, D_MODEL), BF16)] * 3,
        name=name, compiler_params=_params(("parallel",)))(oa, ob, proj, proj, proj, proj, x, w_a, w_b, w_o)


def _mix_tail_bwd(dx, proj, ya, yb, w_a, w_b, w_o, name):
    t = dx.shape[0]
    shard = D_MODEL // N_CHIPS

    def body(dx_ref, ga0, ga1, gb0, gb1, ya_ref, yb_ref, wa_ref, wb_ref, wo_ref, dya_ref, dyb_ref, dg_ref, doa_ref, dob_ref):
        dm = _dot_nt(_bf(dx_ref[...]), wo_ref[0])
        sa, sb = _gate(ga0, ga1), _gate(gb0, gb1)
        dya, dyb = _bf(dm * sa), _bf(dm * sb)
        dya_ref[...] = dya
        dyb_ref[...] = dyb
        dg_ref[:, :D_MODEL] = (dm * ya_ref[...].astype(F32) * sa * (1.0 - sa)).astype(dg_ref.dtype)
        dg_ref[:, D_MODEL:] = (dm * yb_ref[...].astype(F32) * sb * (1.0 - sb)).astype(dg_ref.dtype)
        doa_ref[...] = _dot_nt(dya, wa_ref[0])
        dob = _dot_nt(dyb[:, :shard], wb_ref[0])
        for s in range(1, N_CHIPS):
            dob = dob + _dot_nt(dyb[:, s * shard:(s + 1) * shard], wb_ref[s])
        dob_ref[...] = dob

    row = lambda w: pl.BlockSpec((MIX_ROWS, w), lambda i: (i, 0))
    return pl.pallas_call(
        body, grid=(t // MIX_ROWS,),
        in_specs=[row(D_MODEL)] + _gate_specs() + [row(D_MODEL), row(D_MODEL), _whole(w_a), _whole(w_b), _whole(w_o)],
        out_specs=[row(D_MODEL), row(D_MODEL), row(2 * D_MODEL), row(D_MODEL), row(ATT_GW)],
        out_shape=[jax.ShapeDtypeStruct((t, D_MODEL), BF16), jax.ShapeDtypeStruct((t, D_MODEL), BF16),
                   jax.ShapeDtypeStruct((t, 2 * D_MODEL), BF16), jax.ShapeDtypeStruct((t, D_MODEL), F32),
                   jax.ShapeDtypeStruct((t, ATT_GW), F32)],
        name=name, compiler_params=_params(("parallel",)))(dx, proj, proj, proj, proj, ya, yb, w_a, w_b, w_o)


def _ffn_fwd(x, g, src, l, pre):
    tag = f"l{l}_{pre}"
    w_in = src.weight(l, pre + "_w_in")
    h, ab, u = _ffn_in_act(x, g, w_in, name=tag + "_in_act", rider=src.ride(tag + "_in_act"))
    w_out = src.weight(l, pre + "_w_out")
    y = _mm_nn(u, w_out, name=tag + "_out", tm=512, tn=D_MODEL, out_dtype=F32, res=x, alpha=0.5, rider=src.ride(tag + "_out"))
    return y, (x, h, ab, u, w_in, w_out)


def _ffn_bwd(dx, saved, g, src, l, pre):
    tag = f"l{l}_{pre}"
    x, h, ab, u, w_in, w_out = saved
    g_out = _mm_tn(u, dx, nb=1, name=tag + "_bwd_wout", tm=1024, tk=1408, tn=D_MODEL, alpha=0.5, rider=src.ride(tag + "_bwd_wout"))
    src.grads(l, {pre + "_w_out": g_out.reshape(N_CHIPS, D_FF // N_CHIPS, D_MODEL)})
    dab = _ffn_bwd_du_act(dx, w_out, ab, name=tag + "_bwd_du_act", rider=src.ride(tag + "_bwd_du_act"))
    g_in = _mm_tn(h, dab, nb=N_CHIPS, name=tag + "_bwd_win", tm=2048, tk=D_MODEL, tn=FF_SHARD, rider=src.ride(tag + "_bwd_win"))
    src.grads(l, {pre + "_w_in": g_in})
    return _mm_nt(dab, w_in, name=tag + "_bwd_dh", tm=1024, tp=D_MODEL, tn=FF_SHARD, out_dtype=F32, rider=src.ride(tag + "_bwd_dh"),
                  norm=(x, g, dx))


def _mix_fwd(x, small, lb, cos, sin, src, l):
    tag = f"l{l}_mix"
    w = {}
    h = _norm_fwd(x, small["mix_norm"], name=tag + "_norm")
    w["w_in"] = src.weight(l, "w_in")
    proj = _mm_nn(h, w["w_in"], name=tag + "_in", tm=2048, tn=896, out_dtype=BF16, rider=src.ride(tag + "_in"))
    hf = _mm_nn(h, w["w_in"][0:1, :, D_MODEL:2 * D_MODEL], name=tag + "_hf", tm=1024, tn=D_MODEL, out_dtype=F32)
    oscan, oa, sall = _hgrn_fwd(proj, hf, lb, small["hgrn_out_norm"], name=tag + "_hgrn", rider=src.ride(tag + "_hgrn"))
    qk = _qk_fwd(proj, cos, sin, small["attn_q_norm"], small["attn_k_norm"], name=tag + "_qk")
    outs, lses = [], []
    for g in range(ATT_GROUPS):
        o, lse = _attn_fwd(qk[g], qk[3 + g], qk[6 + g], g, name=f"{tag}_attn{g}")
        outs.append(o)
        lses.append(lse)
    ob = _merge_fwd(outs, lses, name=tag + "_merge")
    w.update({n: src.weight(l, n) for n in ("w_branch_a", "w_branch_b", "w_out")})
    y, merged, ya, yb = _mix_tail_fwd(oa, ob, proj, x, w["w_branch_a"], w["w_branch_b"], w["w_out"], name=tag + "_tail")
    return y, (x, h, proj, hf, oscan, oa, sall, qk, outs, lses, ob, ya, yb, merged, w)


def _mix_bwd(dx, saved, small, lb, cos, sin, src, l, lb_live):
    tag = f"l{l}_mix"
    x, h, proj, hf, oscan, oa, sall, qk, outs, lses, ob, ya, yb, merged, w = saved
    g_wout = _mm_tn(merged, dx, nb=1, name=tag + "_bwd_wout", tm=1024, tk=D_MODEL, tn=D_MODEL)
    dya, dyb, dgab, doa, dob = _mix_tail_bwd(dx, proj, ya, yb, w["w_branch_a"], w["w_branch_b"], w["w_out"], name=tag + "_bwd_tail")
    g_wa = _mm_tn(oa, dya, nb=1, name=tag + "_bwd_wa", tm=1024, tk=D_MODEL, tn=D_MODEL)
    g_wb = _mm_tn(ob, dyb, nb=N_CHIPS, name=tag + "_bwd_wb", tm=2048, tk=ATT_GW, tn=256)
    mb = _merge_bwd(dob, outs, lses, name=tag + "_bwd_merge")
    dqk, dvs = [None] * 6, []
    for g in range(ATT_GROUPS):
        dq, dk, dv = _attn_bwd(qk[g], qk[3 + g], qk[6 + g], mb[g], lses[g], mb[3 + g], g, name=f"{tag}_bwd_attn{g}")
        dqk[g], dqk[3 + g] = dq, dk
        dvs.append(dv)
    dqk_cols, dqn, dkn = _qk_bwd(dqk, proj, cos, sin, small["attn_q_norm"], small["attn_k_norm"], name=tag + "_bwd_qk")
    dproj, dgn, dlb = _hgrn_bwd(doa, oscan, proj, hf, sall, lb, small["hgrn_out_norm"], dqk_cols, dvs, dgab,
                                name=tag + "_bwd_hgrn", precise=lb_live, rider=src.ride(tag + "_bwd_hgrn"))
    src.grads(l, dict(w_branch_a=g_wa.reshape(N_CHIPS, D_MODEL // N_CHIPS, D_MODEL), w_branch_b=g_wb,
                      w_out=g_wout.reshape(N_CHIPS, D_MODEL // N_CHIPS, D_MODEL)))
    g_win = _mm_tn(h, dproj, nb=N_CHIPS, name=tag + "_bwd_win", tm=1024, tk=D_MODEL, tn=2688, rider=src.ride(tag + "_bwd_win"))
    src.grads(l, dict(w_in=g_win))
    dx, dg = _mm_nt(dproj, w["w_in"], name=tag + "_bwd_dh", tm=1024, tp=D_MODEL, tn=2688, out_dtype=F32,
                    rider=src.ride(tag + "_bwd_dh"), norm=(x, small["mix_norm"], dx))
    return dx, dict(mix_norm=dg, hgrn_out_norm=dgn, lb=dlb, attn_q_norm=dqn, attn_k_norm=dkn)


BIG = ("ffn1_w_in", "ffn1_w_out", "w_in", "w_branch_a", "w_branch_b", "w_out", "ffn2_w_in", "ffn2_w_out")
ROW_SHARDED = ("ffn1_w_out", "w_branch_a", "w_out", "ffn2_w_out")
SMALL = ("ffn1_norm", "mix_norm", "hgrn_lb_logits", "hgrn_out_norm", "attn_q_norm", "attn_k_norm", "ffn2_norm")
WEIGHTS = ("ffn1_norm", "ffn1_w_in", "ffn1_w_out", "mix_norm", "w_in", "hgrn_lb_logits", "hgrn_out_norm", "attn_q_norm",
           "attn_k_norm", "w_branch_a", "w_branch_b", "w_out", "ffn2_norm", "ffn2_w_in", "ffn2_w_out")
SMALL_ROWS = 8


def _matmul_ready(name, a):
    return a.reshape(1, a.shape[0] * a.shape[1], a.shape[2]) if name in ROW_SHARDED else a


def _layer_small(small, l):
    s = {n: small[n][l].reshape(1, D_MODEL) for n in ("ffn1_norm", "mix_norm", "hgrn_out_norm", "ffn2_norm")}
    s.update({n: small[n][l] for n in ("attn_q_norm", "attn_k_norm")})
    return s


def _local_step(x, target, small, src):
    t = x.shape[0]
    cos, sin = _rope_tables(t)
    lbs = _lower_bounds(small["hgrn_lb_logits"])
    saved = []
    for l in range(2):
        sm = _layer_small(small, l)
        lb = lbs[l].reshape(1, D_MODEL)
        x, s1 = _ffn_fwd(x, sm["ffn1_norm"], src, l, "ffn1")
        x, s2 = _mix_fwd(x, sm, lb, cos, sin, src, l)
        x, s3 = _ffn_fwd(x, sm["ffn2_norm"], src, l, "ffn2")
        saved.append((sm, lb, s1, s2, s3))
    dx, sq = _loss_fwd_bwd(x, target, name="loss")
    small_rows = [None, None]
    for l in (1, 0):
        sm, lb, s1, s2, s3 = saved[l]
        dx, dg2 = _ffn_bwd(dx, s3, sm["ffn2_norm"], src, l, "ffn2")
        dx, g = _mix_bwd(dx, s2, sm, lb, cos, sin, src, l, lb_live=l > 0)
        dx, dg1 = _ffn_bwd(dx, s1, sm["ffn1_norm"], src, l, "ffn1")
        pad = lambda a: jnp.pad(a[:ATT_GROUPS].reshape(1, ATT_GROUPS * HEAD), ((0, 0), (0, D_MODEL - ATT_GROUPS * HEAD)))
        small_rows[l] = jnp.concatenate(
            [dg1, g["mix_norm"], g["lb"], g["hgrn_out_norm"], pad(g["attn_q_norm"]), pad(g["attn_k_norm"]), dg2,
             jnp.zeros((SMALL_ROWS - 7, D_MODEL), F32)], axis=0)
    return jnp.sum(sq), dx, jnp.concatenate(small_rows, axis=0)


def _coords():
    return lax.axis_index("x"), lax.axis_index("y"), lax.axis_index("c")


def _other_chips(x, y):
    return [(1 - x, y), (x, 1 - y), (1 - x, 1 - y)]


def _half_rows(rows, which):
    return pl.ds(which * (rows // 2), rows // 2)


def _gather_rider(shards):
    n = len(shards)

    def copies(w, full, sems):
        send, recv, fsend, frecv, osend, orecv = sems
        x, y, c = _coords()
        slot = 2 * x + y
        chips = _other_chips(x, y)

        def copy(i, j, blk, src, pair, to):
            return pltpu.make_async_remote_copy(src_ref=src, dst_ref=blk, send_sem=pair[0].at[i * 3 + j],
                                                recv_sem=pair[1].at[i * 3 + j], device_id=to, device_id_type=MESH)

        def block(i, chip_slot, core):
            return full[i].at[chip_slot, _half_rows(shards[i].shape[0], core)]

        pairs = [(i, j, chip) for i in range(n) for j, chip in enumerate(chips)]

        def first():
            return [copy(i, j, block(i, slot, c), w[i].at[_half_rows(shards[i].shape[0], c)], (send, recv), (*chip, c))
                    for i, j, chip in pairs]

        def landed(core, pair):
            return [copy(i, j, block(i, 2 * chip[0] + chip[1], core), block(i, 2 * chip[0] + chip[1], core), pair, (x, y, 1 - c))
                    for i, j, chip in pairs]

        def own():
            return [pltpu.make_async_remote_copy(src_ref=w[i], dst_ref=full[i].at[slot], send_sem=osend.at[i],
                                                 recv_sem=orecv.at[i], device_id=(x, y, 1 - c), device_id_type=MESH)
                    for i in range(n)]

        return first, landed, own

    def begin(w, full, sems):
        first, _, own = copies(w, full, sems)
        for cp in first() + own():
            cp.start()

    def end(w, full, sems):
        first, landed, own = copies(w, full, sems)
        forwards = landed(lax.axis_index("c"), sems[2:4])
        for arrival, forward in zip(landed(lax.axis_index("c"), sems[:2]), forwards):
            arrival.wait_recv()
            forward.start()
        for cp in landed(1 - lax.axis_index("c"), sems[2:4]) + own():
            cp.wait_recv()
        for cp in first() + forwards + own():
            cp.wait_send()

    out_shape = [jax.ShapeDtypeStruct((N_CHIPS,) + s.shape, s.dtype) for s in shards]
    sems = [pltpu.SemaphoreType.DMA((3 * n,))] * 4 + [pltpu.SemaphoreType.DMA((n,))] * 2
    return _Rider(shards, out_shape, sems, begin, end)


N_RECV = 7


def _scatter_rider(parts):
    n = len(parts)

    def copies(p, out, sems):
        send, recv = sems
        x, y, c = _coords()
        slot = 2 * x + y
        chips = _other_chips(x, y)

        def arrivals():
            return [pltpu.make_async_remote_copy(
                src_ref=out[i].at[k], dst_ref=out[i].at[k], send_sem=send.at[0], recv_sem=recv.at[i * N_RECV + k],
                device_id=(x, y, c), device_id_type=MESH) for i in range(n) for k in range(N_RECV)]

        sends = []
        for i in range(n):
            rows = parts[i].shape[1]
            for j, chip in enumerate(chips):
                for core in (0, 1):
                    sends.append(pltpu.make_async_remote_copy(
                        src_ref=p[i].at[2 * chip[0] + chip[1], _half_rows(rows, core)], dst_ref=out[i].at[2 * j + c],
                        send_sem=send.at[i * N_RECV + 2 * j + core], recv_sem=recv.at[i * N_RECV + 2 * j + c],
                        device_id=(*chip, core), device_id_type=MESH))
            sends.append(pltpu.make_async_remote_copy(
                src_ref=p[i].at[slot, _half_rows(rows, 1 - c)], dst_ref=out[i].at[6], send_sem=send.at[i * N_RECV + 6],
                recv_sem=recv.at[i * N_RECV + 6], device_id=(x, y, 1 - c), device_id_type=MESH))
        return sends, arrivals

    def begin(p, out, sems):
        for cp in copies(p, out, sems)[0]:
            cp.start()

    def end(p, out, sems):
        sends, arrivals = copies(p, out, sems)
        for cp in arrivals():
            cp.wait_recv()
        for cp in sends:
            cp.wait_send()

    out_shape = [jax.ShapeDtypeStruct((N_RECV, a.shape[1] // 2, a.shape[2]), a.dtype) for a in parts]
    return _Rider(parts, out_shape, [pltpu.SemaphoreType.DMA((N_RECV * n,))] * 2, begin, end)


def _run_alone(rider, name):
    _pcall(lambda: None, grid=(), in_specs=[], out_specs=[], out_shape=[], name=name, sem=(), args=(), rider=rider)
    return rider.result


def _sum_partials(own, parts, name):
    r, wd = own.shape
    tm = next(t for t in (256, 128, 64, 32, 16) if r % t == 0)

    def body(own_ref, p_ref, o_ref):
        acc = own_ref[...].astype(F32)
        for k in range(N_RECV):
            acc = acc + p_ref[k].astype(F32)
        o_ref[...] = acc

    return pl.pallas_call(
        body, grid=(r // tm,),
        in_specs=[pl.BlockSpec((tm, wd), lambda i: (i, 0)), pl.BlockSpec((N_RECV, tm, wd), lambda i: (0, i, 0))],
        out_specs=pl.BlockSpec((tm, wd), lambda i: (i, 0)), out_shape=jax.ShapeDtypeStruct((r, wd), F32),
        name=name, compiler_params=_params(("parallel",)))(own, parts)


def _exchange_halves(reduced, name):
    n = len(reduced)

    def body(*refs):
        r, out = refs[:n], refs[n:2 * n]
        send, recv = refs[2 * n:]
        x, y, c = _coords()
        sib = [pltpu.make_async_remote_copy(src_ref=r[i], dst_ref=out[i], send_sem=send.at[i], recv_sem=recv.at[i],
                                            device_id=(x, y, 1 - c), device_id_type=MESH) for i in range(n)]
        for cp in sib:
            cp.start()
        for cp in sib:
            cp.wait_recv()
        for cp in sib:
            cp.wait_send()

    out_shape = [jax.ShapeDtypeStruct(a.shape, a.dtype) for a in reduced]
    return pl.pallas_call(body, in_specs=[ANY] * n, out_specs=[ANY] * n, out_shape=out_shape,
                          scratch_shapes=[pltpu.SemaphoreType.DMA((n,))] * 2, name=name)(*reduced)


def _reduce_finish(parts, recv, tag):
    x, y, c = _coords()
    slot = 2 * x + y
    halves = []
    for i, (p, r) in enumerate(zip(parts, recv)):
        half = p.shape[1] // 2
        own = lax.dynamic_slice(p, (slot, c * half, 0), (1, half, p.shape[2]))[0]
        halves.append(_sum_partials(own, r, name=f"{tag}_sum{i}"))
    theirs = _exchange_halves(halves, name=tag + "_exchange")
    return [jnp.where(c == 0, jnp.concatenate([h, t], axis=0), jnp.concatenate([t, h], axis=0)) for h, t in zip(halves, theirs)]


GATHER_RIDES = {
    "l0_ffn1_in_act": ((0, "w_in"),),
    "l0_ffn1_out": ((0, "w_branch_a"), (0, "w_branch_b"), (0, "w_out")),
    "l0_mix_in": ((0, "ffn2_w_in"), (0, "ffn2_w_out"), (1, "ffn1_w_in"), (1, "ffn1_w_out")),
    "l0_mix_hgrn": ((1, "w_in"), (1, "w_branch_a"), (1, "w_branch_b"), (1, "w_out")),
    "l0_ffn2_in_act": ((1, "ffn2_w_in"), (1, "ffn2_w_out")),
}
ALONE_FIRST = ((0, "ffn1_w_in"), (0, "ffn1_w_out"))
SCATTER_RIDES = {
    "l1_mix_bwd_hgrn": ((1, "ffn2_w_in"), (1, "ffn2_w_out")),
    "l0_ffn2_bwd_win": ((1, "ffn1_w_in"),),
    "l0_ffn2_bwd_dh": ((1, "ffn1_w_out"), (1, "w_branch_a"), (1, "w_branch_b"), (1, "w_out")),
    "l0_mix_bwd_hgrn": ((1, "w_in"), (0, "ffn2_w_out")),
    "l0_mix_bwd_win": ((0, "ffn2_w_in"),),
    "l0_mix_bwd_dh": ((0, "w_in"),),
    "l0_ffn1_bwd_wout": ((0, "w_branch_a"), (0, "w_branch_b"), (0, "w_out")),
    "l0_ffn1_bwd_du_act": ((0, "ffn1_w_out"),),
    "l0_ffn1_bwd_dh": ((0, "ffn1_w_in"),),
}


class _Exchange:
    def __init__(self, shards):
        self.shards = shards
        self.pending = []
        self.full = {}
        self.parts = {}
        self.recv = {}

    def _gather(self, keys):
        return _gather_rider([self.shards[n][l] for l, n in keys]), "gather", list(keys)

    def _scatter(self, keys):
        return _scatter_rider([self.parts[k] for k in keys]), "scatter", list(keys)

    def _unpack(self):
        waiting = []
        for rider, kind, keys in self.pending:
            if rider.result is None:
                waiting.append((rider, kind, keys))
            elif kind == "gather":
                self.full.update(zip(keys, rider.result))
            else:
                self.recv.update(zip(keys, rider.result))
        self.pending = waiting

    def ride(self, host):
        if host in GATHER_RIDES:
            self.pending.append(self._gather(GATHER_RIDES[host]))
        elif host in SCATTER_RIDES:
            self.pending.append(self._scatter(SCATTER_RIDES[host]))
        else:
            return None
        return self.pending[-1][0]

    def weight(self, l, name):
        self._unpack()
        if (l, name) not in self.full:
            assert (l, name) in ALONE_FIRST, (l, name)
            job = self._gather(ALONE_FIRST)
            _run_alone(job[0], name="gather_first")
            self.pending.append(job)
            self._unpack()
        return _matmul_ready(name, self.full[(l, name)])

    def grads(self, l, partials):
        self.parts.update({(l, n): a for n, a in partials.items()})

    def reduce(self):
        self._unpack()
        assert not self.pending and set(self.recv) == set(self.parts)
        out = {}
        for l in range(2):
            done = _reduce_finish([self.parts[(l, n)] for n in BIG], [self.recv[(l, n)] for n in BIG], f"reduce_l{l}")
            out[l] = dict(zip(BIG, done))
        return {n: jnp.stack([out[0][n], out[1][n]], axis=0) for n in BIG}


def _all_reduce_small(rows):
    r = rows.shape[0]

    def body(x_ref, o_ref, buf, send, recv):
        x, y, c = _coords()
        me = 4 * x + 2 * y + c
        buf[me] = x_ref[...]
        copies = []
        for k in range(1, 8):
            peer = (x ^ (k >> 2), y ^ ((k >> 1) & 1), c ^ (k & 1))
            cp = pltpu.make_async_remote_copy(src_ref=x_ref, dst_ref=buf.at[me], send_sem=send.at[k - 1], recv_sem=recv.at[me],
                                              device_id=peer, device_id_type=MESH)
            cp.start()
            copies.append(cp)
        for k in range(1, 8):
            src = 4 * (x ^ (k >> 2)) + 2 * (y ^ ((k >> 1) & 1)) + (c ^ (k & 1))
            pltpu.make_async_remote_copy(src_ref=x_ref, dst_ref=buf.at[src], send_sem=send.at[0], recv_sem=recv.at[src],
                                         device_id=(x, y, c), device_id_type=MESH).wait_recv()
        for cp in copies:
            cp.wait_send()
        acc = buf[0]
        for k in range(1, 8):
            acc = acc + buf[k]
        o_ref[...] = acc

    vm = pl.BlockSpec(memory_space=pltpu.VMEM)
    return pl.pallas_call(
        body, in_specs=[vm], out_specs=vm, out_shape=jax.ShapeDtypeStruct(rows.shape, F32),
        scratch_shapes=[pltpu.VMEM((8, r, D_MODEL), F32), pltpu.SemaphoreType.DMA((7,)), pltpu.SemaphoreType.DMA((8,))],
        name="all_reduce_small")(rows)


def _adamw_math(w, g, m, v):
    m = ADAM_B1 * m + (1.0 - ADAM_B1) * g
    v = ADAM_B2 * v + (1.0 - ADAM_B2) * (g * g)
    m_hat = m / (1.0 - ADAM_B1 ** ADAM_STEP)
    v_hat = v / (1.0 - ADAM_B2 ** ADAM_STEP)
    return -ADAM_LR * (m_hat / (jnp.sqrt(v_hat) + ADAM_EPS) + ADAM_WD * w), m, v


def _adamw(w, g, m, v, name):
    shape = w.shape
    cols = shape[-1]
    flat = lambda a: a.reshape(-1, cols)
    rows = flat(w).shape[0]
    tm = 128 if rows % 128 == 0 else rows
    ins = [('t', flat(a), cols, 0) for a in (w, g, m, v)]
    res = _ew(_adamw_math, ins, [('t', cols, F32)] * 3, rows=rows, tm=tm, name=name)
    return [a.reshape(shape) for a in res]


def _small_update(sums, logits, w, m, v):
    def body(s_ref, lg_ref, w_ref, m_ref, v_ref, g_ref, d_ref, nm_ref, nv_ref):
        s = s_ref[...]
        l0, l1 = lg_ref[0:1, :], lg_ref[1:2, :]
        mx = jnp.maximum(l0, l1)
        e0, e1 = jnp.exp(l0 - mx), jnp.exp(l1 - mx)
        sm0, sm1 = e0 / (e0 + e1), e1 / (e0 + e1)
        dl1 = s_ref[SMALL_ROWS + 2:SMALL_ROWS + 3, :] * sm0 * sm1
        row = lax.broadcasted_iota(jnp.int32, s.shape, 0)
        g = jnp.where(row == 2, -dl1, jnp.where(row == SMALL_ROWS + 2, dl1, s))
        d, nm, nv = _adamw_math(w_ref[...], g, m_ref[...], v_ref[...])
        g_ref[...] = g
        d_ref[...] = d
        nm_ref[...] = nm
        nv_ref[...] = nv

    vm = pl.BlockSpec(memory_space=pltpu.VMEM)
    return pl.pallas_call(body, in_specs=[vm] * 5, out_specs=[vm] * 4,
                          out_shape=[jax.ShapeDtypeStruct(sums.shape, F32)] * 4, name="small_update")(sums, logits, w, m, v)


def _pack_small(vals):
    rows = []
    for l in range(2):
        for n in ("ffn1_norm", "mix_norm", "hgrn_lb_logits", "hgrn_out_norm", "attn_q_norm", "attn_k_norm", "ffn2_norm"):
            a = vals[n][l].reshape(1, -1)
            rows.append(jnp.pad(a, ((0, 0), (0, D_MODEL - a.shape[1]))))
        rows.append(jnp.zeros((SMALL_ROWS - 7, D_MODEL), F32))
    return jnp.concatenate(rows, axis=0)


def _unpack_small(packed):
    out = {}
    for k, n in enumerate(("ffn1_norm", "mix_norm", "hgrn_lb_logits", "hgrn_out_norm", "attn_q_norm", "attn_k_norm", "ffn2_norm")):
        a = jnp.stack([packed[k], packed[SMALL_ROWS + k]], axis=0)
        out[n] = a[:, :ATT_GROUPS * HEAD].reshape(2, ATT_GROUPS, HEAD) if n.startswith("attn") else a
    return out


def kernel(x, ffn1_norm, ffn1_w_in, ffn1_w_out, mix_norm, w_in, hgrn_lb_logits, hgrn_out_norm, attn_q_norm, attn_k_norm, w_branch_a, w_branch_b, w_out, ffn2_norm, ffn2_w_in, ffn2_w_out, loss_target, m_ffn1_norm, m_ffn1_w_in, m_ffn1_w_out, m_mix_norm, m_w_in, m_hgrn_lb_logits, m_hgrn_out_norm, m_attn_q_norm, m_attn_k_norm, m_w_branch_a, m_w_branch_b, m_w_out, m_ffn2_norm, m_ffn2_w_in, m_ffn2_w_out, v_ffn1_norm, v_ffn1_w_in, v_ffn1_w_out, v_mix_norm, v_w_in, v_hgrn_lb_logits, v_hgrn_out_norm, v_attn_q_norm, v_attn_k_norm, v_w_branch_a, v_w_branch_b, v_w_out, v_ffn2_norm, v_ffn2_w_in, v_ffn2_w_out):
    a = locals()
    w = {n: a[n] for n in WEIGHTS}
    m = {n: a["m_" + n] for n in WEIGHTS}
    v = {n: a["v_" + n] for n in WEIGHTS}

    exchange = _Exchange({n: w[n].astype(BF16) for n in BIG})
    small = {n: w[n] for n in SMALL}
    sq, grad_x, small_rows = _local_step(x[0], loss_target[0], small, exchange)
    loss = lax.psum(sq, ("x", "y", "c")) * (0.5 / D_MODEL)
    grads = exchange.reduce()

    sums = _all_reduce_small(small_rows)
    g_s, d_s, m_s, v_s = _small_update(sums, w["hgrn_lb_logits"], _pack_small(small), _pack_small({n: m[n] for n in SMALL}),
                                       _pack_small({n: v[n] for n in SMALL}))
    grads.update(_unpack_small(g_s))
    delta, new_m, new_v = _unpack_small(d_s), _unpack_small(m_s), _unpack_small(v_s)
    for n in BIG:
        delta[n], new_m[n], new_v[n] = _adamw(w[n], grads[n], m[n], v[n], name="adamw_" + n)

    return (loss, grad_x[None], *[grads[n] for n in WEIGHTS], *[delta[n] for n in WEIGHTS],
            *[new_m[n] for n in WEIGHTS], *[new_v[n] for n in WEIGHTS])
```

```python
import functools

import jax
import jax.numpy as jnp
from jax import lax
from jax.experimental import pallas as pl
from jax.experimental.pallas import tpu as pltpu

F32 = jnp.float32
BF16 = jnp.bfloat16
MESH = pl.DeviceIdType.MESH

D_MODEL = 1024
D_FF = 2816
N_CHIPS = 4
HEAD = 128
HG_HEADS = 8
HG_CHUNK = 64
ATT_GROUPS = 3
ATT_HEADS = 4
ATT_GW = ATT_HEADS * HEAD
DILATIONS = (1, 4, 16)
ATT_BLK = 128
ATT_STEP_BLOCKS = 8
P_IN = 10752
CB_AQ, CB_AK, CB_AV, CB_GA, CB_GB = 8, 11, 14, 17, 19
EPS = 1e-6
ROPE_THETA = 10000.0
ADAM_LR, ADAM_B1, ADAM_B2, ADAM_EPS, ADAM_WD, ADAM_STEP = 0.001, 0.9, 0.999, 1e-08, 0.01, 10
VMEM_LIMIT_V7X = 56 * 1024 * 1024
NEG = -1e30


def _params(sem):
    return pltpu.CompilerParams(dimension_semantics=sem, vmem_limit_bytes=VMEM_LIMIT_V7X)


def _sig(x):
    return 1.0 / (1.0 + jnp.exp(-x))


def _dot(a, b):
    return jnp.dot(a, b, preferred_element_type=F32)


def _dot_nt(a, b):
    return lax.dot_general(a, b, (((1,), (1,)), ((), ())), preferred_element_type=F32)


def _dot_tn(a, b):
    return lax.dot_general(a, b, (((0,), (0,)), ((), ())), preferred_element_type=F32)


def _bf(x):
    return x.astype(BF16)


ANY = pl.BlockSpec(memory_space=pl.ANY)


class _Rider:
    def __init__(self, args, out_shape, sems, begin, end):
        self.args, self.out_shape, self.sems, self.begin, self.end = list(args), list(out_shape), list(sems), begin, end
        self.result = None


def _pcall(body, *, grid, in_specs, out_specs, out_shape, name, sem, args, scratch_shapes=(), rider=None):
    multi = isinstance(out_shape, (list, tuple))
    o_specs = list(out_specs) if multi else [out_specs]
    o_shape = list(out_shape) if multi else [out_shape]
    if rider is None:
        res = pl.pallas_call(body, grid=grid, in_specs=list(in_specs), out_specs=o_specs, out_shape=o_shape,
                             scratch_shapes=list(scratch_shapes), name=name, compiler_params=_params(sem))(*args)
        return list(res) if multi else res[0]
    counts = [len(in_specs), len(rider.args), len(o_specs), len(rider.out_shape), len(scratch_shapes)]

    def wrapped(*refs):
        groups, at = [], 0
        for c in counts:
            groups.append(refs[at:at + c])
            at += c
        h_in, r_in, h_out, r_out, h_scratch = groups
        r_sems = refs[at:]
        if grid:
            ids = [pl.program_id(a) for a in range(len(grid))]
            first = functools.reduce(jnp.logical_and, [i == 0 for i in ids])
            last = functools.reduce(jnp.logical_and, [i == g - 1 for i, g in zip(ids, grid)])
            pl.when(first)(lambda: rider.begin(r_in, r_out, r_sems))
            body(*h_in, *h_out, *h_scratch)
            pl.when(last)(lambda: rider.end(r_in, r_out, r_sems))
        else:
            rider.begin(r_in, r_out, r_sems)
            body(*h_in, *h_out, *h_scratch)
            rider.end(r_in, r_out, r_sems)

    res = pl.pallas_call(
        wrapped, grid=grid, in_specs=list(in_specs) + [ANY] * counts[1], out_specs=o_specs + [ANY] * counts[3],
        out_shape=o_shape + rider.out_shape, scratch_shapes=list(scratch_shapes) + rider.sems, name=name,
        compiler_params=_params(("arbitrary",) * len(grid)))(*args, *rider.args)
    rider.result = list(res[counts[2]:])
    return list(res[:counts[2]]) if multi else res[0]


def _mm_nn(a, b3, *, name, tm, tn, out_dtype, res=None, alpha=1.0, rider=None):
    m, k = a.shape
    nb, _, nw = b3.shape
    per = nw // tn
    assert nw % tn == 0 and m % tm == 0
    has_res = res is not None

    def body(*refs):
        if has_res:
            a_ref, b_ref, r_ref, o_ref = refs
        else:
            a_ref, b_ref, o_ref = refs
        acc = _dot(_bf(a_ref[...]), b_ref[...])
        if alpha != 1.0:
            acc = alpha * acc
        if has_res:
            acc = r_ref[...] + acc
        o_ref[...] = acc.astype(o_ref.dtype)

    in_specs = [pl.BlockSpec((tm, k), lambda i, j: (i, 0)),
                pl.BlockSpec((None, k, tn), lambda i, j: (j // per, 0, j % per))]
    args = [a, b3]
    if has_res:
        in_specs.append(pl.BlockSpec((tm, tn), lambda i, j: (i, j)))
        args.append(res)
    return _pcall(body, grid=(m // tm, nb * per), in_specs=in_specs, out_specs=pl.BlockSpec((tm, tn), lambda i, j: (i, j)),
                  out_shape=jax.ShapeDtypeStruct((m, nb * nw), out_dtype), name=name, sem=("parallel", "arbitrary"),
                  args=args, rider=rider)


def _mm_nt(d, b3, *, name, tm, tp, tn, out_dtype, alpha=1.0, rider=None, norm=None):
    m, n = d.shape
    nb, p, nw = b3.shape
    per = nw // tn
    nk = n // tn
    assert nb * nw == n and nw % tn == 0 and p % tp == 0 and m % tm == 0 and (norm is None or tp == p)

    def body(d_ref, b_ref, *refs):
        kk = pl.program_id(2)
        acc_ref = refs[-1]

        @pl.when(kk == 0)
        def _():
            acc_ref[...] = jnp.zeros_like(acc_ref)

        acc_ref[...] += _dot_nt(_bf(d_ref[...]), b_ref[...])

        if norm is None:
            @pl.when(kk == nk - 1)
            def _():
                refs[0][...] = (alpha * acc_ref[...]).astype(refs[0].dtype)
        else:
            x_ref, g_ref, dx_ref, o_ref, dg_ref = refs[:5]

            @pl.when(jnp.logical_and(pl.program_id(0) == 0, kk == 0))
            def _():
                dg_ref[...] = jnp.zeros_like(dg_ref)

            @pl.when(kk == nk - 1)
            def _():
                dh = alpha * acc_ref[...]
                xv = x_ref[...]
                r = _rms_rows(xv)
                xh = xv * r
                dxh = dh * g_ref[...]
                o_ref[...] = dx_ref[...] + r * (dxh - xh * jnp.mean(dxh * xh, axis=1, keepdims=True))
                dg_ref[...] += jnp.sum(dh * xh, axis=0, keepdims=True)

    in_specs = [pl.BlockSpec((tm, tn), lambda i, j, kk: (i, kk)),
                pl.BlockSpec((None, tp, tn), lambda i, j, kk: (kk // per, j, kk % per))]
    tile = pl.BlockSpec((tm, tp), lambda i, j, kk: (i, j))
    if norm is None:
        return _pcall(body, grid=(m // tm, p // tp, nk), in_specs=in_specs, out_specs=tile,
                      out_shape=jax.ShapeDtypeStruct((m, p), out_dtype), scratch_shapes=[pltpu.VMEM((tm, tp), F32)],
                      name=name, sem=("parallel", "parallel", "arbitrary"), args=(d, b3), rider=rider)
    x, g, dx = norm
    row = pl.BlockSpec((1, p), lambda i, j, kk: (0, 0))
    return _pcall(body, grid=(m // tm, 1, nk), in_specs=in_specs + [tile, row, tile], out_specs=[tile, row],
                  out_shape=[jax.ShapeDtypeStruct((m, p), F32), jax.ShapeDtypeStruct((1, p), F32)],
                  scratch_shapes=[pltpu.VMEM((tm, tp), F32)], name=name, sem=("arbitrary", "arbitrary", "arbitrary"),
                  args=(d, b3, x, g, dx), rider=rider)


def _mm_tn(a, d, *, nb, name, tm, tk, tn, alpha=1.0, rider=None):
    m, k = a.shape
    _, n = d.shape
    nw = n // nb
    per = nw // tn
    nm = m // tm
    assert nw % tn == 0 and k % tk == 0 and m % tm == 0

    def body(a_ref, d_ref, o_ref, acc_ref):
        mm = pl.program_id(2)

        @pl.when(mm == 0)
        def _():
            acc_ref[...] = jnp.zeros_like(acc_ref)

        acc_ref[...] += _dot_tn(_bf(a_ref[...]), _bf(d_ref[...]))

        @pl.when(mm == nm - 1)
        def _():
            o_ref[...] = (alpha * acc_ref[...]).astype(o_ref.dtype)

    return _pcall(
        body, grid=(k // tk, nb * per, nm),
        in_specs=[pl.BlockSpec((tm, tk), lambda i, j, mm: (mm, i)),
                  pl.BlockSpec((tm, tn), lambda i, j, mm: (mm, j))],
        out_specs=pl.BlockSpec((None, tk, tn), lambda i, j, mm: (j // per, i, j % per)),
        out_shape=jax.ShapeDtypeStruct((nb, k, nw), BF16),
        scratch_shapes=[pltpu.VMEM((tk, tn), F32)],
        name=name, sem=("parallel", "parallel", "arbitrary"), args=(a, d), rider=rider)


def _rows_from_view(ref, buf, w, d, tm):
    for k in range(d):
        for c in range(w // HEAD):
            lanes = slice(k * w + c * HEAD, k * w + (c + 1) * HEAD)
            buf.at[c][pl.ds(k, tm // d, stride=d), :] = ref[:, lanes].astype(F32)
    return _cat([buf[c] for c in range(w // HEAD)])


def _ew(fn, ins, outs, *, rows, tm, name):
    in_specs, args, scratch = [], [], []
    for s in ins:
        if s[0] == 't':
            _, arr, w, cb = s
            in_specs.append(pl.BlockSpec((tm, w), lambda i, cb=cb: (i, cb)))
        elif s[0] == 'v':
            _, arr, w, d = s
            in_specs.append(pl.BlockSpec((tm // d, d * w), lambda i: (i, 0)))
            scratch.append(pltpu.VMEM((w // HEAD, tm, HEAD), F32))
        else:
            arr = s[1]
            in_specs.append(pl.BlockSpec(arr.shape, lambda i, nd=arr.ndim: (0,) * nd))
        args.append(arr)
    out_specs, out_shape = [], []
    for s in outs:
        if s[0] == 't':
            _, w, dt = s
            out_specs.append(pl.BlockSpec((tm, w), lambda i: (i, 0)))
            out_shape.append(jax.ShapeDtypeStruct((rows, w), dt))
        elif s[0] == 'v':
            _, w, dt, d = s
            out_specs.append(pl.BlockSpec((tm // d, d * w), lambda i: (i, 0)))
            out_shape.append(jax.ShapeDtypeStruct((rows // d, d * w), dt))
            scratch.append(pltpu.VMEM((w // HEAD, tm, HEAD), F32))
        else:
            out_specs.append(pl.BlockSpec(s[1], lambda i: (0, 0)))
            out_shape.append(jax.ShapeDtypeStruct(s[1], F32))
    n_in, n_out = len(ins), len(outs)

    def body(*refs):
        bufs = list(refs[n_in + n_out:])
        vals = []
        for r, s in zip(refs[:n_in], ins):
            if s[0] == 'v':
                vals.append(_rows_from_view(r, bufs.pop(0), s[2], s[3], tm))
            else:
                vals.append(r[...])
        res = fn(*vals)
        if not isinstance(res, (tuple, list)):
            res = (res,)
        for r, s, v in zip(refs[n_in:n_in + n_out], outs, res):
            if s[0] == 't':
                r[...] = v.astype(r.dtype)
            elif s[0] == 'v':
                w, d, buf = s[1], s[3], bufs.pop(0)
                for c in range(w // HEAD):
                    buf[c] = v[:, c * HEAD:(c + 1) * HEAD].astype(F32)
                for k in range(d):
                    for c in range(w // HEAD):
                        lanes = slice(k * w + c * HEAD, k * w + (c + 1) * HEAD)
                        r[:, lanes] = buf.at[c][pl.ds(k, tm // d, stride=d), :].astype(r.dtype)
            else:
                @pl.when(pl.program_id(0) == 0)
                def _(r=r):
                    r[...] = jnp.zeros_like(r)

                r[...] += v

    res = pl.pallas_call(
        body, grid=(rows // tm,), in_specs=in_specs, out_specs=out_specs, out_shape=out_shape, scratch_shapes=scratch,
        name=name, compiler_params=_params(("arbitrary",)))(*args)
    return res


def _tile(arr, w, g):
    return ('t', arr, w, 0) if DILATIONS[g] == 1 else ('v', arr, w, DILATIONS[g])


def _tile_out(w, dtype, g):
    return ('t', w, dtype) if DILATIONS[g] == 1 else ('v', w, dtype, DILATIONS[g])


def _heads(x):
    return [x[:, h * HEAD:(h + 1) * HEAD] for h in range(x.shape[1] // HEAD)]


def _cat(xs):
    return jnp.concatenate(xs, axis=1)


def _head_mean(x):
    return _cat([jnp.broadcast_to(jnp.mean(h, axis=1, keepdims=True), h.shape) for h in _heads(x)])


def _rms_rows(x):
    return lax.rsqrt(jnp.mean(x * x, axis=1, keepdims=True) + EPS)


def _norm_fwd(x, g, name):
    return _ew(lambda xv, gv: xv * _rms_rows(xv) * gv,
               [('t', x, D_MODEL, 0), ('f', g)], [('t', D_MODEL, BF16)], rows=x.shape[0], tm=512, name=name)[0]


def _loss_fwd_bwd(y, target, name):
    def fn(yv, tv):
        e = yv - tv
        return e * (1.0 / D_MODEL), jnp.sum(e * e, axis=0, keepdims=True)

    return _ew(fn, [('t', y, D_MODEL, 0), ('t', target, D_MODEL, 0)], [('t', D_MODEL, F32), ('acc', (1, D_MODEL))],
               rows=y.shape[0], tm=512, name=name)


def _rot(x):
    sgn = jnp.where(lax.broadcasted_iota(jnp.int32, x.shape, 1) < HEAD // 2, -1.0, 1.0)
    return pltpu.roll(x, HEAD // 2, 1) * sgn


def _gain_rows(qn, kn):
    return [a[g:g + 1] for a in (qn, kn) for g in range(ATT_GROUPS)]


def _qk_fwd(proj, cos, sin, qn, kn, name):
    def fn(*v):
        xs, cosv, sinv, gains, vs = v[:6], v[6], v[7], v[8:14], v[14:17]
        outs = []
        for j, x in enumerate(xs):
            gain = gains[j]
            ys = []
            for xh in _heads(x.astype(F32)):
                xn = xh * _rms_rows(xh) * gain
                ys.append(xn * cosv + _rot(xn) * sinv)
            outs.append(_cat(ys))
        return outs + list(vs)

    ins = ([('t', proj, 512, CB_AQ + j) for j in range(6)] + [('t', cos, HEAD, 0), ('t', sin, HEAD, 0)]
           + [('f', a) for a in _gain_rows(qn, kn)] + [('t', proj, 512, CB_AV + g) for g in range(ATT_GROUPS)])
    return _ew(fn, ins, [_tile_out(ATT_GW, BF16, j % ATT_GROUPS) for j in range(9)], rows=proj.shape[0], tm=512, name=name)


def _qk_bwd(dqk, proj, cos, sin, qn, kn, name):
    def fn(*v):
        ds, xs, cosv, sinv, gains = v[:6], v[6:12], v[12], v[13], v[14:20]
        rows8 = lax.broadcasted_iota(jnp.int32, (8, HEAD), 0)
        outs, dgs = [], [jnp.zeros((8, HEAD), F32)] * 2
        for j in range(6):
            gain = gains[j]
            dx, dg = [], jnp.zeros((1, HEAD), F32)
            for dyh, xh in zip(_heads(ds[j]), _heads(xs[j].astype(F32))):
                r = _rms_rows(xh)
                xhat = xh * r
                dxn = dyh * cosv - _rot(dyh * sinv)
                dg = dg + jnp.sum(dxn * xhat, axis=0, keepdims=True)
                dxh = dxn * gain
                dx.append(r * (dxh - xhat * jnp.mean(dxh * xhat, axis=1, keepdims=True)))
            outs.append(_cat(dx))
            dgs[j // 3] = dgs[j // 3] + jnp.where(rows8 == j % 3, dg, 0.0)
        return _cat(outs), dgs[0], dgs[1]

    ins = ([_tile(a, ATT_GW, j % ATT_GROUPS) for j, a in enumerate(dqk)] + [('t', proj, 512, CB_AQ + j) for j in range(6)]
           + [('t', cos, HEAD, 0), ('t', sin, HEAD, 0)] + [('f', a) for a in _gain_rows(qn, kn)])
    return _ew(fn, ins, [('t', 6 * ATT_GW, BF16), ('acc', (8, HEAD)), ('acc', (8, HEAD))],
               rows=proj.shape[0], tm=256, name=name)


def _pick(x, h):
    lanes = lax.broadcasted_iota(jnp.int32, x.shape, 1)
    return jnp.sum(jnp.where(lanes == h, x, 0.0), axis=1, keepdims=True)


def _spread(x):
    return _cat([jnp.broadcast_to(_pick(x, h), (x.shape[0], HEAD)) for h in range(ATT_HEADS)])


def _compact(x):
    lanes = lax.broadcasted_iota(jnp.int32, (x.shape[0], HEAD), 1)
    out = jnp.zeros((x.shape[0], HEAD), F32)
    for h, xh in enumerate(_heads(x)):
        out = jnp.where(lanes == h, xh, out)
    return out


def _group_weights(l0, l1, l2):
    l0, l1, l2 = _spread(l0), _spread(l1), _spread(l2)
    m = jnp.maximum(jnp.maximum(l0, l1), l2)
    e0, e1, e2 = jnp.exp(l0 - m), jnp.exp(l1 - m), jnp.exp(l2 - m)
    inv = 1.0 / (e0 + e1 + e2)
    return e0 * inv, e1 * inv, e2 * inv


def _merge_fwd(outs, lses, name):
    def fn(o0, o1, o2, l0, l1, l2):
        a0, a1, a2 = _group_weights(l0, l1, l2)
        return a0 * o0 + a1 * o1 + a2 * o2

    ins = [_tile(a, ATT_GW, g) for g, a in enumerate(outs)] + [_tile(a, HEAD, g) for g, a in enumerate(lses)]
    return _ew(fn, ins, [('t', ATT_GW, BF16)], rows=outs[0].shape[0], tm=512, name=name)[0]


def _merge_bwd(dob, outs, lses, name):
    def fn(dov, o0, o1, o2, l0, l1, l2):
        a0, a1, a2 = _group_weights(l0, l1, l2)
        ob = a0 * o0 + a1 * o1 + a2 * o2
        s = _head_mean(dov * ob) * float(HEAD)
        return a0 * dov, a1 * dov, a2 * dov, _compact(a0 * s), _compact(a1 * s), _compact(a2 * s)

    ins = ([('t', dob, ATT_GW, 0)] + [_tile(a, ATT_GW, g) for g, a in enumerate(outs)]
           + [_tile(a, HEAD, g) for g, a in enumerate(lses)])
    groups = range(ATT_GROUPS)
    return _ew(fn, ins, [_tile_out(ATT_GW, BF16, g) for g in groups] + [_tile_out(HEAD, F32, g) for g in groups],
               rows=dob.shape[0], tm=512, name=name)


HG_ROWS = 256


def _hg_gates(hq, hf, hi, lbv):
    sig = _sig(hf)
    f = lbv + (1.0 - lbv) * sig
    return hq * _sig(hq), 1.0 - f, hi, jnp.log(f), sig, f


def _split3(x):
    hi = _bf(x)
    r1 = x - hi.astype(F32)
    mid = _bf(r1)
    return hi, mid, _bf(r1 - mid.astype(F32))


def _tri_dot(tri, x):
    hi, mid, lo = _split3(x)
    return _dot(tri, hi) + _dot(tri, mid) + _dot(tri, lo)


def _row(x, i):
    rows = lax.broadcasted_iota(jnp.int32, x.shape, 0)
    return jnp.sum(jnp.where(rows == i, x, 0.0), axis=0, keepdims=True)


def _hg_decay(logf, q, k):
    c = HG_CHUNK
    row = lax.broadcasted_iota(jnp.int32, (c, c), 0)
    col = lax.broadcasted_iota(jnp.int32, (c, c), 1)
    g = _tri_dot((row >= col).astype(BF16), logf)
    gm = _row(g, c // 2 - 1)
    gl = _row(g, c - 1)
    decays = jnp.exp(g), jnp.exp(g - gm), jnp.exp(gm - g), jnp.exp(gl - g)
    return gl, decays, q * decays[0], q * decays[1], k * decays[2], k * decays[3]


def _hg_out_fwd(o, hg, gain):
    r = lax.rsqrt(_head_mean(o * o) + EPS)
    return o * r * gain * (hg * _sig(hg))


def _hgrn_fwd(proj, hf, lb, gain, name, rider=None):
    t = proj.shape[0]
    nck = HG_ROWS // HG_CHUNK

    def body(hq_ref, hf_ref, hi_ref, hg_ref, lb_ref, gn_ref, o_ref, oa_ref, sall_ref, st_ref):
        @pl.when(pl.program_id(0) == 0)
        def _():
            st_ref[...] = jnp.zeros_like(st_ref)

        lbv = lb_ref[...]
        gnv = gn_ref[...]
        c = HG_CHUNK
        mask = lax.broadcasted_iota(jnp.int32, (c, c), 0) >= lax.broadcasted_iota(jnp.int32, (c, c), 1)

        def chunk(cc, carry):
            sl = pl.ds(pl.multiple_of(cc * c, c), c)
            q, k, v, logf, _, _ = _hg_gates(hq_ref[sl, :].astype(F32), hf_ref[sl, :], hi_ref[sl, :].astype(F32), lbv)
            gl, _, qg, qt, kt, kd = _hg_decay(logf, q, k)
            egl = jnp.exp(gl)
            os = []
            for h in range(HG_HEADS):
                hs = slice(h * HEAD, (h + 1) * HEAD)
                st = st_ref[h]
                sall_ref[cc, h] = st
                a = jnp.where(mask, _dot_nt(_bf(qt[:, hs]), _bf(kt[:, hs])), 0.0)
                os.append(_dot(_bf(a), _bf(v[:, hs])) + _dot_nt(_bf(qg[:, hs]), _bf(st)))
                st_ref[h] = egl[:, hs] * st + _dot_tn(_bf(v[:, hs]), _bf(kd[:, hs]))
            o = _cat(os)
            o_ref[sl, :] = o
            oa_ref[sl, :] = _hg_out_fwd(o, hg_ref[sl, :].astype(F32), gnv).astype(oa_ref.dtype)
            return carry

        lax.fori_loop(0, nck, chunk, 0)

    col = lambda j: pl.BlockSpec((HG_ROWS, D_MODEL), lambda i, j=j: (i, j))
    small = pl.BlockSpec((1, D_MODEL), lambda i: (0, 0))
    return _pcall(
        body, grid=(t // HG_ROWS,),
        in_specs=[col(0), col(0), col(2), col(3), small, small],
        out_specs=[col(0), col(0), pl.BlockSpec((nck, HG_HEADS, HEAD, HEAD), lambda i: (i, 0, 0, 0))],
        out_shape=[jax.ShapeDtypeStruct((t, D_MODEL), F32), jax.ShapeDtypeStruct((t, D_MODEL), BF16),
                   jax.ShapeDtypeStruct((t // HG_CHUNK, HG_HEADS, HEAD, HEAD), F32)],
        scratch_shapes=[pltpu.VMEM((HG_HEADS, HEAD, HEAD), F32)],
        name=name, sem=("arbitrary",), args=(proj, hf, proj, proj, lb, gain), rider=rider)


def _terms(x, precise):
    hi = _bf(x)
    return (hi, _bf(x - hi.astype(F32))) if precise else (hi,)


def _mm(dot, a, b):
    out = dot(a[0], b[0])
    if len(a) > 1:
        out = out + dot(a[1], b[0])
    if len(b) > 1:
        out = out + dot(a[0], b[1])
    return out


def _hgrn_bwd(doa, oscan, proj, hf, sall, lb, gain, dqk, dvs, dgab, name, precise, rider=None):
    t = proj.shape[0]
    nck = HG_ROWS // HG_CHUNK
    nsteps = t // HG_ROWS
    terms = functools.partial(_terms, precise=precise)
    n_view = sum(d > 1 for d in DILATIONS)

    def body(doa_ref, os_ref, hq_ref, hf_ref, hi_ref, hg_ref, sall_ref, lb_ref, gn_ref, dqk_ref, dv0_ref, dv1_ref,
             dv2_ref, dgab_ref, dproj_ref, dgn_ref, dlb_ref, dst_ref, *bufs):
        @pl.when(pl.program_id(0) == 0)
        def _():
            dst_ref[...] = jnp.zeros_like(dst_ref)
            dgn_ref[...] = jnp.zeros_like(dgn_ref)
            dlb_ref[...] = jnp.zeros_like(dlb_ref)

        at = 4 * D_MODEL
        dproj_ref[:, at:at + 6 * ATT_GW] = dqk_ref[...]
        at += 6 * ATT_GW
        spare = list(bufs)
        for d, dv_ref in zip(DILATIONS, (dv0_ref, dv1_ref, dv2_ref)):
            dv = dv_ref[...] if d == 1 else _rows_from_view(dv_ref, spare.pop(0), ATT_GW, d, HG_ROWS)
            dproj_ref[:, at:at + ATT_GW] = dv.astype(dproj_ref.dtype)
            at += ATT_GW
        dproj_ref[:, at:] = dgab_ref[...]

        lbv = lb_ref[...]
        gnv = gn_ref[...]
        c = HG_CHUNK
        row = lax.broadcasted_iota(jnp.int32, (c, c), 0)
        colm = lax.broadcasted_iota(jnp.int32, (c, c), 1)
        mask = row >= colm
        triu = (row <= colm).astype(BF16)
        last = lax.broadcasted_iota(jnp.int32, (c, HEAD), 0) == c - 1

        def chunk(ci, carry):
            cc = nck - 1 - ci
            sl = pl.ds(pl.multiple_of(cc * c, c), c)
            hq, hg = hq_ref[sl, :].astype(F32), hg_ref[sl, :].astype(F32)
            q, k, v, logf, sig, f = _hg_gates(hq, hf_ref[sl, :], hi_ref[sl, :].astype(F32), lbv)
            gl, (e_qg, e_qt, e_kt, e_kd), qg, qt, kt, kd = _hg_decay(logf, q, k)
            egl = jnp.exp(gl)
            o = os_ref[sl, :]
            dy = doa_ref[sl, :]
            r = lax.rsqrt(_head_mean(o * o) + EPS)
            oh = o * r
            sg = _sig(hg)
            silu_g = hg * sg
            dgn_ref[...] += jnp.sum(dy * oh * silu_g, axis=0, keepdims=True)
            dhg = dy * oh * gnv * (sg * (1.0 + hg * (1.0 - sg)))
            doh = dy * gnv * silu_g
            do = r * (doh - oh * _head_mean(doh * oh))
            dqs, dks, dvs, dgs = [], [], [], []
            for h in range(HG_HEADS):
                hs = slice(h * HEAD, (h + 1) * HEAD)
                st = sall_ref[cc, h]
                dst = dst_ref[h]
                qt_h, kt_h, qg_h, kd_h = qt[:, hs], kt[:, hs], qg[:, hs], kd[:, hs]
                do_p, v_p, qt_p, kt_p, qg_p = terms(do[:, hs]), terms(v[:, hs]), terms(qt_h), terms(kt_h), terms(qg_h)
                st_p, dst_p = terms(st), terms(dst)
                a = jnp.where(mask, _dot_nt(qt_p[0], kt_p[0]), 0.0)
                da = terms(jnp.where(mask, _mm(_dot_nt, do_p, v_p), 0.0))
                dqt = _mm(_dot, da, kt_p)
                dkt = _mm(_dot_tn, da, qt_p)
                dqg = _mm(_dot, do_p, st_p)
                dv = _dot_tn(_bf(a), do_p[0]) + _dot_nt(_bf(kd_h), dst_p[0])
                dkd = _mm(_dot, v_p, dst_p)
                dgl = egl[:, hs] * jnp.sum(st * dst, axis=0, keepdims=True) + jnp.sum(dkd * kd_h, axis=0, keepdims=True)
                dst_ref[h] = egl[:, hs] * dst + _mm(_dot_tn, do_p, qg_p)
                dqs.append(dqt * e_qt[:, hs] + dqg * e_qg[:, hs])
                dks.append(dkt * e_kt[:, hs] + dkd * e_kd[:, hs])
                dvs.append(dv)
                dgs.append(dqt * qt_h - dkt * kt_h + dqg * qg_h - dkd * kd_h + jnp.where(last, dgl, 0.0))
            dq, dk, dv, dg = _cat(dqs), _cat(dks), _cat(dvs), _cat(dgs)
            dlogf = _tri_dot(triu, dg)
            df = dlogf / f - dk
            dlb_ref[...] += jnp.sum(df * (1.0 - sig), axis=0, keepdims=True)
            dhf = df * (1.0 - lbv) * sig * (1.0 - sig)
            sq = _sig(hq)
            dhq = dq * (sq * (1.0 + hq * (1.0 - sq)))
            dproj_ref[sl, :4 * D_MODEL] = _cat([dhq, dhf, dv, dhg]).astype(dproj_ref.dtype)
            return carry

        lax.fori_loop(0, nck, chunk, 0)

    rev = lambda j: pl.BlockSpec((HG_ROWS, D_MODEL), lambda i, j=j: (nsteps - 1 - i, j))
    rows = lambda a, d=1: pl.BlockSpec((HG_ROWS // d, a.shape[1]), lambda i: (nsteps - 1 - i, 0))
    small = pl.BlockSpec((1, D_MODEL), lambda i: (0, 0))
    return _pcall(
        body, grid=(nsteps,),
        in_specs=[rev(0), rev(0), rev(0), rev(0), rev(2), rev(3),
                  pl.BlockSpec((nck, HG_HEADS, HEAD, HEAD), lambda i: (nsteps - 1 - i, 0, 0, 0)), small, small,
                  rows(dqk)] + [rows(a, d) for a, d in zip(dvs, DILATIONS)] + [rows(dgab)],
        out_specs=[pl.BlockSpec((HG_ROWS, P_IN), lambda i: (nsteps - 1 - i, 0)), small, small],
        out_shape=[jax.ShapeDtypeStruct((t, P_IN), BF16), jax.ShapeDtypeStruct((1, D_MODEL), F32),
                   jax.ShapeDtypeStruct((1, D_MODEL), F32)],
        scratch_shapes=[pltpu.VMEM((HG_HEADS, HEAD, HEAD), F32)] + [pltpu.VMEM((ATT_HEADS, HG_ROWS, HEAD), F32)] * n_view,
        name=name, sem=("arbitrary",), args=(doa, oscan, proj, hf, proj, proj, sall, lb, gain, dqk, *dvs, dgab),
        rider=rider)


def _window_masks(has_previous):
    qi = lax.broadcasted_iota(jnp.int32, (ATT_BLK, 2 * ATT_BLK), 0)
    ki = lax.broadcasted_iota(jnp.int32, (ATT_BLK, 2 * ATT_BLK), 1)
    band = jnp.logical_and(ki >= qi, ki <= qi + ATT_BLK)
    return band, jnp.logical_and(band, jnp.logical_or(ki >= ATT_BLK, has_previous))


def _two_blocks(ref, prev_ref, j, hs):
    if j == 0:
        return jnp.concatenate([prev_ref[:, hs], ref[0:ATT_BLK, hs]], axis=0)
    return ref[(j - 1) * ATT_BLK:(j + 1) * ATT_BLK, hs]


def _attn_cfg(qg, g):
    d = DILATIONS[g]
    length = qg.shape[0]
    assert qg.shape[1] == d * ATT_GW
    nb = length // ATT_BLK
    return d, length, nb, min(ATT_STEP_BLOCKS, nb)


def _attn_fwd(qg, kg, vg, g, name):
    d, length, nb, rb = _attn_cfg(qg, g)
    scale = HEAD ** -0.5

    def body(q_ref, k_ref, v_ref, kp_ref, vp_ref, o_ref, l_ref):
        n = pl.program_id(1)
        band, first_band = _window_masks(n > 0)
        lanes = lax.broadcasted_iota(jnp.int32, (ATT_BLK, HEAD), 1)
        for j in range(rb):
            rows = slice(j * ATT_BLK, (j + 1) * ATT_BLK)
            lse = jnp.zeros((ATT_BLK, HEAD), F32)
            for h in range(ATT_HEADS):
                hs = slice(h * HEAD, (h + 1) * HEAD)
                k2, v2 = _two_blocks(k_ref, kp_ref, j, hs), _two_blocks(v_ref, vp_ref, j, hs)
                s = jnp.where(first_band if j == 0 else band, _dot_nt(q_ref[rows, hs], k2) * scale, NEG)
                m = jnp.max(s, axis=1, keepdims=True)
                p = jnp.exp(s - m)
                l = jnp.sum(p, axis=1, keepdims=True)
                o_ref[rows, hs] = (_dot(_bf(p), v2) / l).astype(o_ref.dtype)
                lse = jnp.where(lanes == h, m + jnp.log(l), lse)
            l_ref[rows, :] = lse

    own = pl.BlockSpec((rb * ATT_BLK, ATT_GW), lambda r, n: (n, r))
    own_head = pl.BlockSpec((rb * ATT_BLK, HEAD), lambda r, n: (n, r))
    prev = pl.BlockSpec((ATT_BLK, ATT_GW), lambda r, n: (jnp.maximum(n * rb - 1, 0), r))
    return pl.pallas_call(
        body, grid=(d, nb // rb), in_specs=[own, own, own, prev, prev], out_specs=[own, own_head],
        out_shape=[jax.ShapeDtypeStruct((length, d * ATT_GW), BF16), jax.ShapeDtypeStruct((length, d * HEAD), F32)],
        name=name, compiler_params=_params(("parallel", "arbitrary")))(qg, kg, vg, kg, vg)


def _attn_bwd(qg, kg, vg, dog, lse, delta, g, name):
    d, length, nb, rb = _attn_cfg(qg, g)
    nsteps = nb // rb
    scale = HEAD ** -0.5

    def body(q_ref, k_ref, v_ref, do_ref, l_ref, dl_ref, kp_ref, vp_ref, qn_ref, don_ref, ln_ref, dln_ref,
             dq_ref, dk_ref, dv_ref):
        n = pl.program_id(1)
        band, first_band = _window_masks(n > 0)
        qi = lax.broadcasted_iota(jnp.int32, (ATT_BLK, ATT_BLK), 0)
        ki = lax.broadcasted_iota(jnp.int32, (ATT_BLK, ATT_BLK), 1)
        next_m = jnp.logical_and(ki >= qi, n < nsteps - 1)
        last = slice((rb - 1) * ATT_BLK, rb * ATT_BLK)
        for h in range(ATT_HEADS):
            hs = slice(h * HEAD, (h + 1) * HEAD)
            dk, dv = [None] * rb, [None] * rb
            for j in range(rb):
                rows = slice(j * ATT_BLK, (j + 1) * ATT_BLK)
                q, do = q_ref[rows, hs], do_ref[rows, hs]
                k2, v2 = _two_blocks(k_ref, kp_ref, j, hs), _two_blocks(v_ref, vp_ref, j, hs)
                p = jnp.where(first_band if j == 0 else band,
                              jnp.exp(_dot_nt(q, k2) * scale - _pick(l_ref[rows, :], h)), 0.0)
                ds = _bf(p * (_dot_nt(do, v2) - _pick(dl_ref[rows, :], h)) * scale)
                dq_ref[rows, hs] = _dot(ds, k2).astype(dq_ref.dtype)
                dk2, dv2 = _dot_tn(ds, q), _dot_tn(_bf(p), do)
                if j >= 1:
                    dk[j - 1] = dk[j - 1] + dk2[:ATT_BLK]
                    dv[j - 1] = dv[j - 1] + dv2[:ATT_BLK]
                dk[j], dv[j] = dk2[ATT_BLK:], dv2[ATT_BLK:]
            q, do = qn_ref[:, hs], don_ref[:, hs]
            p = jnp.where(next_m, jnp.exp(_dot_nt(q, k_ref[last, hs]) * scale - _pick(ln_ref[...], h)), 0.0)
            ds = _bf(p * (_dot_nt(do, v_ref[last, hs]) - _pick(dln_ref[...], h)) * scale)
            dk[rb - 1] = dk[rb - 1] + _dot_tn(ds, q)
            dv[rb - 1] = dv[rb - 1] + _dot_tn(_bf(p), do)
            for j in range(rb):
                rows = slice(j * ATT_BLK, (j + 1) * ATT_BLK)
                dk_ref[rows, hs] = dk[j].astype(dk_ref.dtype)
                dv_ref[rows, hs] = dv[j].astype(dv_ref.dtype)

    own = pl.BlockSpec((rb * ATT_BLK, ATT_GW), lambda r, n: (n, r))
    prev = pl.BlockSpec((ATT_BLK, ATT_GW), lambda r, n: (jnp.maximum(n * rb - 1, 0), r))
    nxt = pl.BlockSpec((ATT_BLK, ATT_GW), lambda r, n: (jnp.minimum((n + 1) * rb, nb - 1), r))
    own_head = pl.BlockSpec((rb * ATT_BLK, HEAD), lambda r, n: (n, r))
    nxt_head = pl.BlockSpec((ATT_BLK, HEAD), lambda r, n: (jnp.minimum((n + 1) * rb, nb - 1), r))
    return pl.pallas_call(
        body, grid=(d, nsteps), in_specs=[own] * 4 + [own_head] * 2 + [prev, prev, nxt, nxt, nxt_head, nxt_head],
        out_specs=[own, own, own], out_shape=[jax.ShapeDtypeStruct((length, d * ATT_GW), BF16)] * 3,
        name=name, compiler_params=_params(("parallel", "arbitrary")))(
            qg, kg, vg, dog, lse, delta, kg, vg, qg, dog, lse, delta)


def _rope_tables(t):
    pos = jnp.arange(t, dtype=F32)
    inv = ROPE_THETA ** (-jnp.arange(0, HEAD, 2, dtype=F32) / HEAD)
    ang = pos[:, None] * inv[None, :]
    ang = jnp.concatenate([ang, ang], axis=-1)
    return jnp.cos(ang), jnp.sin(ang)


def _lower_bounds(logits):
    lb = jnp.cumsum(jax.nn.softmax(logits.astype(F32), axis=0), axis=0)
    return lb - lb[0:1]


FFN_ROWS = 256
FFN_IN_ROWS = 512
FF_SHARD = 2 * D_FF // N_CHIPS


def _ffn_in_act(x, g, w_in, name, rider=None):
    t = x.shape[0]

    def body(x_ref, g_ref, w_ref, h_ref, ab_ref, u_ref):
        xv = x_ref[...]
        h = _bf(xv * _rms_rows(xv) * g_ref[...])
        h_ref[...] = h
        for s in range(N_CHIPS // 2):
            cols = slice(s * FF_SHARD, (s + 1) * FF_SHARD)
            a = _dot(h, w_ref[s])
            b = _dot(h, w_ref[s + N_CHIPS // 2])
            ab_ref[:, cols] = a.astype(ab_ref.dtype)
            ab_ref[:, D_FF + s * FF_SHARD:D_FF + (s + 1) * FF_SHARD] = b.astype(ab_ref.dtype)
            u_ref[:, cols] = (a * _sig(a) * b).astype(u_ref.dtype)

    row = lambda w: pl.BlockSpec((FFN_IN_ROWS, w), lambda i: (i, 0))
    return _pcall(
        body, grid=(t // FFN_IN_ROWS,),
        in_specs=[row(D_MODEL), pl.BlockSpec((1, D_MODEL), lambda i: (0, 0)),
                  pl.BlockSpec(w_in.shape, lambda i: (0, 0, 0))],
        out_specs=[row(D_MODEL), row(2 * D_FF), row(D_FF)],
        out_shape=[jax.ShapeDtypeStruct((t, D_MODEL), BF16), jax.ShapeDtypeStruct((t, 2 * D_FF), BF16),
                   jax.ShapeDtypeStruct((t, D_FF), BF16)],
        name=name, sem=("parallel",), args=(x, g, w_in), rider=rider)


def _ffn_bwd_du_act(dx, w_out, ab, name, rider=None):
    t = dx.shape[0]

    def body(dx_ref, w_ref, ab_ref, o_ref):
        du = 0.5 * _dot_nt(_bf(dx_ref[...]), w_ref[0])
        a = ab_ref[:, :D_FF].astype(F32)
        b = ab_ref[:, D_FF:].astype(F32)
        s = _sig(a)
        o_ref[:, :D_FF] = (du * b * (s * (1.0 + a * (1.0 - s)))).astype(o_ref.dtype)
        o_ref[:, D_FF:] = (du * a * s).astype(o_ref.dtype)

    row = lambda w: pl.BlockSpec((FFN_ROWS, w), lambda i: (i, 0))
    return _pcall(
        body, grid=(t // FFN_ROWS,),
        in_specs=[row(D_MODEL), pl.BlockSpec(w_out.shape, lambda i: (0, 0, 0)), row(2 * D_FF)],
        out_specs=row(2 * D_FF), out_shape=jax.ShapeDtypeStruct((t, 2 * D_FF), BF16),
        name=name, sem=("parallel",), args=(dx, w_out, ab), rider=rider)


MIX_ROWS = 512


def _gate_specs():
    return [pl.BlockSpec((MIX_ROWS, 512), lambda i, cb=cb: (i, cb)) for cb in (CB_GA, CB_GA + 1, CB_GB, CB_GB + 1)]


def _gate(lo_ref, hi_ref):
    return _sig(_cat([lo_ref[...], hi_ref[...]]).astype(F32))


def _whole(a):
    return pl.BlockSpec(a.shape, lambda i: (0,) * a.ndim)


def _mix_tail_fwd(oa, ob, proj, x, w_a, w_b, w_o, name):
    t = x.shape[0]

    def body(oa_ref, ob_ref, ga0, ga1, gb0, gb1, x_ref, wa_ref, wb_ref, wo_ref, y_ref, m_ref, ya_ref, yb_ref):
        ya = _dot(oa_ref[...], wa_ref[0])
        yb = _cat([_dot(ob_ref[...], wb_ref[s]) for s in range(N_CHIPS)])
        merged = _bf(_gate(ga0, ga1) * ya + _gate(gb0, gb1) * yb)
        m_ref[...] = merged
        ya_ref[...] = ya.astype(ya_ref.dtype)
        yb_ref[...] = yb.astype(yb_ref.dtype)
        y_ref[...] = x_ref[...] + _dot(merged, wo_ref[0])

    row = lambda w: pl.BlockSpec((MIX_ROWS, w), lambda i: (i, 0))
    return pl.pallas_call(
        body, grid=(t // MIX_ROWS,),
        in_specs=[row(D_MODEL), row(ATT_GW)] + _gate_specs() + [row(D_MODEL), _whole(w_a), _whole(w_b), _whole(w_o)],
        out_specs=[row(D_MODEL)] * 4,
        out_shape=[jax.ShapeDtypeStruct((t, D_MODEL), F32)] + [jax.ShapeDtypeStruct((t, D_MODEL), BF16)] * 3,
        name=name, compiler_params=_params(("parallel",)))(oa, ob, proj, proj, proj, proj, x, w_a, w_b, w_o)


def _mix_tail_bwd(dx, proj, ya, yb, w_a, w_b, w_o, name):
    t = dx.shape[0]
    shard = D_MODEL // N_CHIPS

    def body(dx_ref, ga0, ga1, gb0, gb1, ya_ref, yb_ref, wa_ref, wb_ref, wo_ref, dya_ref, dyb_ref, dg_ref, doa_ref, dob_ref):
        dm = _dot_nt(_bf(dx_ref[...]), wo_ref[0])
        sa, sb = _gate(ga0, ga1), _gate(gb0, gb1)
        dya, dyb = _bf(dm * sa), _bf(dm * sb)
        dya_ref[...] = dya
        dyb_ref[...] = dyb
        dg_ref[:, :D_MODEL] = (dm * ya_ref[...].astype(F32) * sa * (1.0 - sa)).astype(dg_ref.dtype)
        dg_ref[:, D_MODEL:] = (dm * yb_ref[...].astype(F32) * sb * (1.0 - sb)).astype(dg_ref.dtype)
        doa_ref[...] = _dot_nt(dya, wa_ref[0])
        dob = _dot_nt(dyb[:, :shard], wb_ref[0])
        for s in range(1, N_CHIPS):
            dob = dob + _dot_nt(dyb[:, s * shard:(s + 1) * shard], wb_ref[s])
        dob_ref[...] = dob

    row = lambda w: pl.BlockSpec((MIX_ROWS, w), lambda i: (i, 0))
    return pl.pallas_call(
        body, grid=(t // MIX_ROWS,),
        in_specs=[row(D_MODEL)] + _gate_specs() + [row(D_MODEL), row(D_MODEL), _whole(w_a), _whole(w_b), _whole(w_o)],
        out_specs=[row(D_MODEL), row(D_MODEL), row(2 * D_MODEL), row(D_MODEL), row(ATT_GW)],
        out_shape=[jax.ShapeDtypeStruct((t, D_MODEL), BF16), jax.ShapeDtypeStruct((t, D_MODEL), BF16),
                   jax.ShapeDtypeStruct((t, 2 * D_MODEL), BF16), jax.ShapeDtypeStruct((t, D_MODEL), F32),
                   jax.ShapeDtypeStruct((t, ATT_GW), F32)],
        name=name, compiler_params=_params(("parallel",)))(dx, proj, proj, proj, proj, ya, yb, w_a, w_b, w_o)


def _ffn_fwd(x, g, src, l, pre):
    tag = f"l{l}_{pre}"
    w_in = src.weight(l, pre + "_w_in")
    h, ab, u = _ffn_in_act(x, g, w_in, name=tag + "_in_act", rider=src.ride(tag + "_in_act"))
    w_out = src.weight(l, pre + "_w_out")
    y = _mm_nn(u, w_out, name=tag + "_out", tm=512, tn=D_MODEL, out_dtype=F32, res=x, alpha=0.5, rider=src.ride(tag + "_out"))
    return y, (x, h, ab, u, w_in, w_out)


def _ffn_bwd(dx, saved, g, src, l, pre):
    tag = f"l{l}_{pre}"
    x, h, ab, u, w_in, w_out = saved
    g_out = _mm_tn(u, dx, nb=1, name=tag + "_bwd_wout", tm=1024, tk=1408, tn=D_MODEL, alpha=0.5, rider=src.ride(tag + "_bwd_wout"))
    src.grads(l, {pre + "_w_out": g_out.reshape(N_CHIPS, D_FF // N_CHIPS, D_MODEL)})
    dab = _ffn_bwd_du_act(dx, w_out, ab, name=tag + "_bwd_du_act", rider=src.ride(tag + "_bwd_du_act"))
    g_in = _mm_tn(h, dab, nb=N_CHIPS, name=tag + "_bwd_win", tm=2048, tk=D_MODEL, tn=FF_SHARD, rider=src.ride(tag + "_bwd_win"))
    src.grads(l, {pre + "_w_in": g_in})
    return _mm_nt(dab, w_in, name=tag + "_bwd_dh", tm=1024, tp=D_MODEL, tn=FF_SHARD, out_dtype=F32, rider=src.ride(tag + "_bwd_dh"),
                  norm=(x, g, dx))


def _mix_fwd(x, small, lb, cos, sin, src, l):
    tag = f"l{l}_mix"
    w = {}
    h = _norm_fwd(x, small["mix_norm"], name=tag + "_norm")
    w["w_in"] = src.weight(l, "w_in")
    proj = _mm_nn(h, w["w_in"], name=tag + "_in", tm=2048, tn=896, out_dtype=BF16, rider=src.ride(tag + "_in"))
    hf = _mm_nn(h, w["w_in"][0:1, :, D_MODEL:2 * D_MODEL], name=tag + "_hf", tm=1024, tn=D_MODEL, out_dtype=F32)
    oscan, oa, sall = _hgrn_fwd(proj, hf, lb, small["hgrn_out_norm"], name=tag + "_hgrn", rider=src.ride(tag + "_hgrn"))
    qk = _qk_fwd(proj, cos, sin, small["attn_q_norm"], small["attn_k_norm"], name=tag + "_qk")
    outs, lses = [], []
    for g in range(ATT_GROUPS):
        o, lse = _attn_fwd(qk[g], qk[3 + g], qk[6 + g], g, name=f"{tag}_attn{g}")
        outs.append(o)
        lses.append(lse)
    ob = _merge_fwd(outs, lses, name=tag + "_merge")
    w.update({n: src.weight(l, n) for n in ("w_branch_a", "w_branch_b", "w_out")})
    y, merged, ya, yb = _mix_tail_fwd(oa, ob, proj, x, w["w_branch_a"], w["w_branch_b"], w["w_out"], name=tag + "_tail")
    return y, (x, h, proj, hf, oscan, oa, sall, qk, outs, lses, ob, ya, yb, merged, w)


def _mix_bwd(dx, saved, small, lb, cos, sin, src, l, lb_live):
    tag = f"l{l}_mix"
    x, h, proj, hf, oscan, oa, sall, qk, outs, lses, ob, ya, yb, merged, w = saved
    g_wout = _mm_tn(merged, dx, nb=1, name=tag + "_bwd_wout", tm=1024, tk=D_MODEL, tn=D_MODEL)
    dya, dyb, dgab, doa, dob = _mix_tail_bwd(dx, proj, ya, yb, w["w_branch_a"], w["w_branch_b"], w["w_out"], name=tag + "_bwd_tail")
    g_wa = _mm_tn(oa, dya, nb=1, name=tag + "_bwd_wa", tm=1024, tk=D_MODEL, tn=D_MODEL)
    g_wb = _mm_tn(ob, dyb, nb=N_CHIPS, name=tag + "_bwd_wb", tm=2048, tk=ATT_GW, tn=256)
    mb = _merge_bwd(dob, outs, lses, name=tag + "_bwd_merge")
    dqk, dvs = [None] * 6, []
    for g in range(ATT_GROUPS):
        dq, dk, dv = _attn_bwd(qk[g], qk[3 + g], qk[6 + g], mb[g], lses[g], mb[3 + g], g, name=f"{tag}_bwd_attn{g}")
        dqk[g], dqk[3 + g] = dq, dk
        dvs.append(dv)
    dqk_cols, dqn, dkn = _qk_bwd(dqk, proj, cos, sin, small["attn_q_norm"], small["attn_k_norm"], name=tag + "_bwd_qk")
    dproj, dgn, dlb = _hgrn_bwd(doa, oscan, proj, hf, sall, lb, small["hgrn_out_norm"], dqk_cols, dvs, dgab,
                                name=tag + "_bwd_hgrn", precise=lb_live, rider=src.ride(tag + "_bwd_hgrn"))
    src.grads(l, dict(w_branch_a=g_wa.reshape(N_CHIPS, D_MODEL // N_CHIPS, D_MODEL), w_branch_b=g_wb,
                      w_out=g_wout.reshape(N_CHIPS, D_MODEL // N_CHIPS, D_MODEL)))
    g_win = _mm_tn(h, dproj, nb=N_CHIPS, name=tag + "_bwd_win", tm=1024, tk=D_MODEL, tn=2688, rider=src.ride(tag + "_bwd_win"))
    src.grads(l, dict(w_in=g_win))
    dx, dg = _mm_nt(dproj, w["w_in"], name=tag + "_bwd_dh", tm=1024, tp=D_MODEL, tn=2688, out_dtype=F32,
                    rider=src.ride(tag + "_bwd_dh"), norm=(x, small["mix_norm"], dx))
    return dx, dict(mix_norm=dg, hgrn_out_norm=dgn, lb=dlb, attn_q_norm=dqn, attn_k_norm=dkn)


BIG = ("ffn1_w_in", "ffn1_w_out", "w_in", "w_branch_a", "w_branch_b", "w_out", "ffn2_w_in", "ffn2_w_out")
ROW_SHARDED = ("ffn1_w_out", "w_branch_a", "w_out", "ffn2_w_out")
SMALL = ("ffn1_norm", "mix_norm", "hgrn_lb_logits", "hgrn_out_norm", "attn_q_norm", "attn_k_norm", "ffn2_norm")
WEIGHTS = ("ffn1_norm", "ffn1_w_in", "ffn1_w_out", "mix_norm", "w_in", "hgrn_lb_logits", "hgrn_out_norm", "attn_q_norm",
           "attn_k_norm", "w_branch_a", "w_branch_b", "w_out", "ffn2_norm", "ffn2_w_in", "ffn2_w_out")
SMALL_ROWS = 8


def _matmul_ready(name, a):
    return a.reshape(1, a.shape[0] * a.shape[1], a.shape[2]) if name in ROW_SHARDED else a


def _layer_small(small, l):
    s = {n: small[n][l].reshape(1, D_MODEL) for n in ("ffn1_norm", "mix_norm", "hgrn_out_norm", "ffn2_norm")}
    s.update({n: small[n][l] for n in ("attn_q_norm", "attn_k_norm")})
    return s


def _local_step(x, target, small, src):
    t = x.shape[0]
    cos, sin = _rope_tables(t)
    lbs = _lower_bounds(small["hgrn_lb_logits"])
    saved = []
    for l in range(2):
        sm = _layer_small(small, l)
        lb = lbs[l].reshape(1, D_MODEL)
        x, s1 = _ffn_fwd(x, sm["ffn1_norm"], src, l, "ffn1")
        x, s2 = _mix_fwd(x, sm, lb, cos, sin, src, l)
        x, s3 = _ffn_fwd(x, sm["ffn2_norm"], src, l, "ffn2")
        saved.append((sm, lb, s1, s2, s3))
    dx, sq = _loss_fwd_bwd(x, target, name="loss")
    small_rows = [None, None]
    for l in (1, 0):
        sm, lb, s1, s2, s3 = saved[l]
        dx, dg2 = _ffn_bwd(dx, s3, sm["ffn2_norm"], src, l, "ffn2")
        dx, g = _mix_bwd(dx, s2, sm, lb, cos, sin, src, l, lb_live=l > 0)
        dx, dg1 = _ffn_bwd(dx, s1, sm["ffn1_norm"], src, l, "ffn1")
        pad = lambda a: jnp.pad(a[:ATT_GROUPS].reshape(1, ATT_GROUPS * HEAD), ((0, 0), (0, D_MODEL - ATT_GROUPS * HEAD)))
        small_rows[l] = jnp.concatenate(
            [dg1, g["mix_norm"], g["lb"], g["hgrn_out_norm"], pad(g["attn_q_norm"]), pad(g["attn_k_norm"]), dg2,
             jnp.zeros((SMALL_ROWS - 7, D_MODEL), F32)], axis=0)
    return jnp.sum(sq), dx, jnp.concatenate(small_rows, axis=0)


def _coords():
    return lax.axis_index("x"), lax.axis_index("y"), lax.axis_index("c")


def _other_chips(x, y):
    return [(1 - x, y), (x, 1 - y), (1 - x, 1 - y)]


def _half_rows(rows, which):
    return pl.ds(which * (rows // 2), rows // 2)


def _gather_rider(shards):
    n = len(shards)

    def copies(w, full, sems):
        send, recv, fsend, frecv, osend, orecv = sems
        x, y, c = _coords()
        slot = 2 * x + y
        chips = _other_chips(x, y)

        def copy(i, j, blk, src, pair, to):
            return pltpu.make_async_remote_copy(src_ref=src, dst_ref=blk, send_sem=pair[0].at[i * 3 + j],
                                                recv_sem=pair[1].at[i * 3 + j], device_id=to, device_id_type=MESH)

        def block(i, chip_slot, core):
            return full[i].at[chip_slot, _half_rows(shards[i].shape[0], core)]

        pairs = [(i, j, chip) for i in range(n) for j, chip in enumerate(chips)]

        def first():
            return [copy(i, j, block(i, slot, c), w[i].at[_half_rows(shards[i].shape[0], c)], (send, recv), (*chip, c))
                    for i, j, chip in pairs]

        def landed(core, pair):
            return [copy(i, j, block(i, 2 * chip[0] + chip[1], core), block(i, 2 * chip[0] + chip[1], core), pair, (x, y, 1 - c))
                    for i, j, chip in pairs]

        def own():
            return [pltpu.make_async_remote_copy(src_ref=w[i], dst_ref=full[i].at[slot], send_sem=osend.at[i],
                                                 recv_sem=orecv.at[i], device_id=(x, y, 1 - c), device_id_type=MESH)
                    for i in range(n)]

        return first, landed, own

    def begin(w, full, sems):
        first, _, own = copies(w, full, sems)
        for cp in first() + own():
            cp.start()

    def end(w, full, sems):
        first, landed, own = copies(w, full, sems)
        forwards = landed(lax.axis_index("c"), sems[2:4])
        for arrival, forward in zip(landed(lax.axis_index("c"), sems[:2]), forwards):
            arrival.wait_recv()
            forward.start()
        for cp in landed(1 - lax.axis_index("c"), sems[2:4]) + own():
            cp.wait_recv()
        for cp in first() + forwards + own():
            cp.wait_send()

    out_shape = [jax.ShapeDtypeStruct((N_CHIPS,) + s.shape, s.dtype) for s in shards]
    sems = [pltpu.SemaphoreType.DMA((3 * n,))] * 4 + [pltpu.SemaphoreType.DMA((n,))] * 2
    return _Rider(shards, out_shape, sems, begin, end)


N_RECV = 7


def _scatter_rider(parts):
    n = len(parts)

    def copies(p, out, sems):
        send, recv = sems
        x, y, c = _coords()
        slot = 2 * x + y
        chips = _other_chips(x, y)

        def arrivals():
            return [pltpu.make_async_remote_copy(
                src_ref=out[i].at[k], dst_ref=out[i].at[k], send_sem=send.at[0], recv_sem=recv.at[i * N_RECV + k],
                device_id=(x, y, c), device_id_type=MESH) for i in range(n) for k in range(N_RECV)]

        sends = []
        for i in range(n):
            rows = parts[i].shape[1]
            for j, chip in enumerate(chips):
                for core in (0, 1):
                    sends.append(pltpu.make_async_remote_copy(
                        src_ref=p[i].at[2 * chip[0] + chip[1], _half_rows(rows, core)], dst_ref=out[i].at[2 * j + c],
                        send_sem=send.at[i * N_RECV + 2 * j + core], recv_sem=recv.at[i * N_RECV + 2 * j + c],
                        device_id=(*chip, core), device_id_type=MESH))
            sends.append(pltpu.make_async_remote_copy(
                src_ref=p[i].at[slot, _half_rows(rows, 1 - c)], dst_ref=out[i].at[6], send_sem=send.at[i * N_RECV + 6],
                recv_sem=recv.at[i * N_RECV + 6], device_id=(x, y, 1 - c), device_id_type=MESH))
        return sends, arrivals

    def begin(p, out, sems):
        for cp in copies(p, out, sems)[0]:
            cp.start()

    def end(p, out, sems):
        sends, arrivals = copies(p, out, sems)
        for cp in arrivals():
            cp.wait_recv()
        for cp in sends:
            cp.wait_send()

    out_shape = [jax.ShapeDtypeStruct((N_RECV, a.shape[1] // 2, a.shape[2]), a.dtype) for a in parts]
    return _Rider(parts, out_shape, [pltpu.SemaphoreType.DMA((N_RECV * n,))] * 2, begin, end)


def _run_alone(rider, name):
    _pcall(lambda: None, grid=(), in_specs=[], out_specs=[], out_shape=[], name=name, sem=(), args=(), rider=rider)
    return rider.result


def _sum_partials(own, parts, name):
    r, wd = own.shape
    tm = next(t for t in (256, 128, 64, 32, 16) if r % t == 0)

    def body(own_ref, p_ref, o_ref):
        acc = own_ref[...].astype(F32)
        for k in range(N_RECV):
            acc = acc + p_ref[k].astype(F32)
        o_ref[...] = acc

    return pl.pallas_call(
        body, grid=(r // tm,),
        in_specs=[pl.BlockSpec((tm, wd), lambda i: (i, 0)), pl.BlockSpec((N_RECV, tm, wd), lambda i: (0, i, 0))],
        out_specs=pl.BlockSpec((tm, wd), lambda i: (i, 0)), out_shape=jax.ShapeDtypeStruct((r, wd), F32),
        name=name, compiler_params=_params(("parallel",)))(own, parts)


def _exchange_halves(reduced, name):
    n = len(reduced)

    def body(*refs):
        r, out = refs[:n], refs[n:2 * n]
        send, recv = refs[2 * n:]
        x, y, c = _coords()
        sib = [pltpu.make_async_remote_copy(src_ref=r[i], dst_ref=out[i], send_sem=send.at[i], recv_sem=recv.at[i],
                                            device_id=(x, y, 1 - c), device_id_type=MESH) for i in range(n)]
        for cp in sib:
            cp.start()
        for cp in sib:
            cp.wait_recv()
        for cp in sib:
            cp.wait_send()

    out_shape = [jax.ShapeDtypeStruct(a.shape, a.dtype) for a in reduced]
    return pl.pallas_call(body, in_specs=[ANY] * n, out_specs=[ANY] * n, out_shape=out_shape,
                          scratch_shapes=[pltpu.SemaphoreType.DMA((n,))] * 2, name=name)(*reduced)


def _reduce_finish(parts, recv, tag):
    x, y, c = _coords()
    slot = 2 * x + y
    halves = []
    for i, (p, r) in enumerate(zip(parts, recv)):
        half = p.shape[1] // 2
        own = lax.dynamic_slice(p, (slot, c * half, 0), (1, half, p.shape[2]))[0]
        halves.append(_sum_partials(own, r, name=f"{tag}_sum{i}"))
    theirs = _exchange_halves(halves, name=tag + "_exchange")
    return [jnp.where(c == 0, jnp.concatenate([h, t], axis=0), jnp.concatenate([t, h], axis=0)) for h, t in zip(halves, theirs)]


GATHER_RIDES = {
    "l0_ffn1_in_act": ((0, "w_in"),),
    "l0_ffn1_out": ((0, "w_branch_a"), (0, "w_branch_b"), (0, "w_out")),
    "l0_mix_in": ((0, "ffn2_w_in"), (0, "ffn2_w_out"), (1, "ffn1_w_in"), (1, "ffn1_w_out")),
    "l0_mix_hgrn": ((1, "w_in"), (1, "w_branch_a"), (1, "w_branch_b"), (1, "w_out")),
    "l0_ffn2_in_act": ((1, "ffn2_w_in"), (1, "ffn2_w_out")),
}
ALONE_FIRST = ((0, "ffn1_w_in"), (0, "ffn1_w_out"))
SCATTER_RIDES = {
    "l1_mix_bwd_hgrn": ((1, "ffn2_w_in"), (1, "ffn2_w_out")),
    "l0_ffn2_bwd_win": ((1, "ffn1_w_in"),),
    "l0_ffn2_bwd_dh": ((1, "ffn1_w_out"), (1, "w_branch_a"), (1, "w_branch_b"), (1, "w_out")),
    "l0_mix_bwd_hgrn": ((1, "w_in"), (0, "ffn2_w_out")),
    "l0_mix_bwd_win": ((0, "ffn2_w_in"),),
    "l0_mix_bwd_dh": ((0, "w_in"),),
    "l0_ffn1_bwd_wout": ((0, "w_branch_a"), (0, "w_branch_b"), (0, "w_out")),
    "l0_ffn1_bwd_du_act": ((0, "ffn1_w_out"),),
    "l0_ffn1_bwd_dh": ((0, "ffn1_w_in"),),
}


class _Exchange:
    def __init__(self, shards):
        self.shards = shards
        self.pending = []
        self.full = {}
        self.parts = {}
        self.recv = {}

    def _gather(self, keys):
        return _gather_rider([self.shards[n][l] for l, n in keys]), "gather", list(keys)

    def _scatter(self, keys):
        return _scatter_rider([self.parts[k] for k in keys]), "scatter", list(keys)

    def _unpack(self):
        waiting = []
        for rider, kind, keys in self.pending:
            if rider.result is None:
                waiting.append((rider, kind, keys))
            elif kind == "gather":
                self.full.update(zip(keys, rider.result))
            else:
                self.recv.update(zip(keys, rider.result))
        self.pending = waiting

    def ride(self, host):
        if host in GATHER_RIDES:
            self.pending.append(self._gather(GATHER_RIDES[host]))
        elif host in SCATTER_RIDES:
            self.pending.append(self._scatter(SCATTER_RIDES[host]))
        else:
            return None
        return self.pending[-1][0]

    def weight(self, l, name):
        self._unpack()
        if (l, name) not in self.full:
            assert (l, name) in ALONE_FIRST, (l, name)
            job = self._gather(ALONE_FIRST)
            _run_alone(job[0], name="gather_first")
            self.pending.append(job)
            self._unpack()
        return _matmul_ready(name, self.full[(l, name)])

    def grads(self, l, partials):
        self.parts.update({(l, n): a for n, a in partials.items()})

    def reduce(self):
        self._unpack()
        assert not self.pending and set(self.recv) == set(self.parts)
        out = {}
        for l in range(2):
            done = _reduce_finish([self.parts[(l, n)] for n in BIG], [self.recv[(l, n)] for n in BIG], f"reduce_l{l}")
            out[l] = dict(zip(BIG, done))
        return {n: jnp.stack([out[0][n], out[1][n]], axis=0) for n in BIG}


def _all_reduce_small(rows):
    r = rows.shape[0]

    def body(x_ref, o_ref, buf, send, recv):
        x, y, c = _coords()
        me = 4 * x + 2 * y + c
        buf[me] = x_ref[...]
        copies = []
        for k in range(1, 8):
            peer = (x ^ (k >> 2), y ^ ((k >> 1) & 1), c ^ (k & 1))
            cp = pltpu.make_async_remote_copy(src_ref=x_ref, dst_ref=buf.at[me], send_sem=send.at[k - 1], recv_sem=recv.at[me],
                                              device_id=peer, device_id_type=MESH)
            cp.start()
            copies.append(cp)
        for k in range(1, 8):
            src = 4 * (x ^ (k >> 2)) + 2 * (y ^ ((k >> 1) & 1)) + (c ^ (k & 1))
            pltpu.make_async_remote_copy(src_ref=x_ref, dst_ref=buf.at[src], send_sem=send.at[0], recv_sem=recv.at[src],
                                         device_id=(x, y, c), device_id_type=MESH).wait_recv()
        for cp in copies:
            cp.wait_send()
        acc = buf[0]
        for k in range(1, 8):
            acc = acc + buf[k]
        o_ref[...] = acc

    vm = pl.BlockSpec(memory_space=pltpu.VMEM)
    return pl.pallas_call(
        body, in_specs=[vm], out_specs=vm, out_shape=jax.ShapeDtypeStruct(rows.shape, F32),
        scratch_shapes=[pltpu.VMEM((8, r, D_MODEL), F32), pltpu.SemaphoreType.DMA((7,)), pltpu.SemaphoreType.DMA((8,))],
        name="all_reduce_small")(rows)


def _adamw_math(w, g, m, v):
    m = ADAM_B1 * m + (1.0 - ADAM_B1) * g
    v = ADAM_B2 * v + (1.0 - ADAM_B2) * (g * g)
    m_hat = m / (1.0 - ADAM_B1 ** ADAM_STEP)
    v_hat = v / (1.0 - ADAM_B2 ** ADAM_STEP)
    return -ADAM_LR * (m_hat / (jnp.sqrt(v_hat) + ADAM_EPS) + ADAM_WD * w), m, v


def _adamw(w, g, m, v, name):
    shape = w.shape
    cols = shape[-1]
    flat = lambda a: a.reshape(-1, cols)
    rows = flat(w).shape[0]
    tm = 128 if rows % 128 == 0 else rows
    ins = [('t', flat(a), cols, 0) for a in (w, g, m, v)]
    res = _ew(_adamw_math, ins, [('t', cols, F32)] * 3, rows=rows, tm=tm, name=name)
    return [a.reshape(shape) for a in res]


def _small_update(sums, logits, w, m, v):
    def body(s_ref, lg_ref, w_ref, m_ref, v_ref, g_ref, d_ref, nm_ref, nv_ref):
        s = s_ref[...]
        l0, l1 = lg_ref[0:1, :], lg_ref[1:2, :]
        mx = jnp.maximum(l0, l1)
        e0, e1 = jnp.exp(l0 - mx), jnp.exp(l1 - mx)
        sm0, sm1 = e0 / (e0 + e1), e1 / (e0 + e1)
        dl1 = s_ref[SMALL_ROWS + 2:SMALL_ROWS + 3, :] * sm0 * sm1
        row = lax.broadcasted_iota(jnp.int32, s.shape, 0)
        g = jnp.where(row == 2, -dl1, jnp.where(row == SMALL_ROWS + 2, dl1, s))
        d, nm, nv = _adamw_math(w_ref[...], g, m_ref[...], v_ref[...])
        g_ref[...] = g
        d_ref[...] = d
        nm_ref[...] = nm
        nv_ref[...] = nv

    vm = pl.BlockSpec(memory_space=pltpu.VMEM)
    return pl.pallas_call(body, in_specs=[vm] * 5, out_specs=[vm] * 4,
                          out_shape=[jax.ShapeDtypeStruct(sums.shape, F32)] * 4, name="small_update")(sums, logits, w, m, v)


def _pack_small(vals):
    rows = []
    for l in range(2):
        for n in ("ffn1_norm", "mix_norm", "hgrn_lb_logits", "hgrn_out_norm", "attn_q_norm", "attn_k_norm", "ffn2_norm"):
            a = vals[n][l].reshape(1, -1)
            rows.append(jnp.pad(a, ((0, 0), (0, D_MODEL - a.shape[1]))))
        rows.append(jnp.zeros((SMALL_ROWS - 7, D_MODEL), F32))
    return jnp.concatenate(rows, axis=0)


def _unpack_small(packed):
    out = {}
    for k, n in enumerate(("ffn1_norm", "mix_norm", "hgrn_lb_logits", "hgrn_out_norm", "attn_q_norm", "attn_k_norm", "ffn2_norm")):
        a = jnp.stack([packed[k], packed[SMALL_ROWS + k]], axis=0)
        out[n] = a[:, :ATT_GROUPS * HEAD].reshape(2, ATT_GROUPS, HEAD) if n.startswith("attn") else a
    return out


def kernel(x, ffn1_norm, ffn1_w_in, ffn1_w_out, mix_norm, w_in, hgrn_lb_logits, hgrn_out_norm, attn_q_norm, attn_k_norm, w_branch_a, w_branch_b, w_out, ffn2_norm, ffn2_w_in, ffn2_w_out, loss_target, m_ffn1_norm, m_ffn1_w_in, m_ffn1_w_out, m_mix_norm, m_w_in, m_hgrn_lb_logits, m_hgrn_out_norm, m_attn_q_norm, m_attn_k_norm, m_w_branch_a, m_w_branch_b, m_w_out, m_ffn2_norm, m_ffn2_w_in, m_ffn2_w_out, v_ffn1_norm, v_ffn1_w_in, v_ffn1_w_out, v_mix_norm, v_w_in, v_hgrn_lb_logits, v_hgrn_out_norm, v_attn_q_norm, v_attn_k_norm, v_w_branch_a, v_w_branch_b, v_w_out, v_ffn2_norm, v_ffn2_w_in, v_ffn2_w_out):
    a = locals()
    w = {n: a[n] for n in WEIGHTS}
    m = {n: a["m_" + n] for n in WEIGHTS}
    v = {n: a["v_" + n] for n in WEIGHTS}

    exchange = _Exchange({n: w[n].astype(BF16) for n in BIG})
    small = {n: w[n] for n in SMALL}
    sq, grad_x, small_rows = _local_step(x[0], loss_target[0], small, exchange)
    loss = lax.psum(sq, ("x", "y", "c")) * (0.5 / D_MODEL)
    grads = exchange.reduce()

    sums = _all_reduce_small(small_rows)
    g_s, d_s, m_s, v_s = _small_update(sums, w["hgrn_lb_logits"], _pack_small(small), _pack_small({n: m[n] for n in SMALL}),
                                       _pack_small({n: v[n] for n in SMALL}))
    grads.update(_unpack_small(g_s))
    delta, new_m, new_v = _unpack_small(d_s), _unpack_small(m_s), _unpack_small(v_s)
    for n in BIG:
        delta[n], new_m[n], new_v[n] = _adamw(w[n], grads[n], m[n], v[n], name="adamw_" + n)

    return (loss, grad_x[None], *[grads[n] for n in WEIGHTS], *[delta[n] for n in WEIGHTS],
            *[new_m[n] for n in WEIGHTS], *[new_v[n] for n in WEIGHTS])
```

```python
import functools

import jax
import jax.numpy as jnp
from jax import lax
from jax.experimental import pallas as pl
from jax.experimental.pallas import tpu as pltpu

F32 = jnp.float32
BF16 = jnp.bfloat16
MESH = pl.DeviceIdType.MESH

D_MODEL = 1024
D_FF = 2816
N_CHIPS = 4
HEAD = 128
HG_HEADS = 8
HG_CHUNK = 64
ATT_GROUPS = 3
ATT_HEADS = 4
ATT_GW = ATT_HEADS * HEAD
DILATIONS = (1, 4, 16)
ATT_BLK = 128
ATT_STEP_BLOCKS = 8
P_IN = 10752
CB_AQ, CB_AK, CB_AV, CB_GA, CB_GB = 8, 11, 14, 17, 19
EPS = 1e-6
ROPE_THETA = 10000.0
ADAM_LR, ADAM_B1, ADAM_B2, ADAM_EPS, ADAM_WD, ADAM_STEP = 0.001, 0.9, 0.999, 1e-08, 0.01, 10
VMEM_LIMIT_V7X = 56 * 1024 * 1024
NEG = -1e30


def _params(sem):
    return pltpu.CompilerParams(dimension_semantics=sem, vmem_limit_bytes=VMEM_LIMIT_V7X)


def _sig(x):
    return 1.0 / (1.0 + jnp.exp(-x))


def _dot(a, b):
    return jnp.dot(a, b, preferred_element_type=F32)


def _dot_nt(a, b):
    return lax.dot_general(a, b, (((1,), (1,)), ((), ())), preferred_element_type=F32)


def _dot_tn(a, b):
    return lax.dot_general(a, b, (((0,), (0,)), ((), ())), preferred_element_type=F32)


def _bf(x):
    return x.astype(BF16)


ANY = pl.BlockSpec(memory_space=pl.ANY)


class _Rider:
    def __init__(self, args, out_shape, sems, begin, end):
        self.args, self.out_shape, self.sems, self.begin, self.end = list(args), list(out_shape), list(sems), begin, end
        self.result = None


def _pcall(body, *, grid, in_specs, out_specs, out_shape, name, sem, args, scratch_shapes=(), rider=None):
    multi = isinstance(out_shape, (list, tuple))
    o_specs = list(out_specs) if multi else [out_specs]
    o_shape = list(out_shape) if multi else [out_shape]
    if rider is None:
        res = pl.pallas_call(body, grid=grid, in_specs=list(in_specs), out_specs=o_specs, out_shape=o_shape,
                             scratch_shapes=list(scratch_shapes), name=name, compiler_params=_params(sem))(*args)
        return list(res) if multi else res[0]
    counts = [len(in_specs), len(rider.args), len(o_specs), len(rider.out_shape), len(scratch_shapes)]

    def wrapped(*refs):
        groups, at = [], 0
        for c in counts:
            groups.append(refs[at:at + c])
            at += c
        h_in, r_in, h_out, r_out, h_scratch = groups
        r_sems = refs[at:]
        if grid:
            ids = [pl.program_id(a) for a in range(len(grid))]
            first = functools.reduce(jnp.logical_and, [i == 0 for i in ids])
            last = functools.reduce(jnp.logical_and, [i == g - 1 for i, g in zip(ids, grid)])
            pl.when(first)(lambda: rider.begin(r_in, r_out, r_sems))
            body(*h_in, *h_out, *h_scratch)
            pl.when(last)(lambda: rider.end(r_in, r_out, r_sems))
        else:
            rider.begin(r_in, r_out, r_sems)
            body(*h_in, *h_out, *h_scratch)
            rider.end(r_in, r_out, r_sems)

    res = pl.pallas_call(
        wrapped, grid=grid, in_specs=list(in_specs) + [ANY] * counts[1], out_specs=o_specs + [ANY] * counts[3],
        out_shape=o_shape + rider.out_shape, scratch_shapes=list(scratch_shapes) + rider.sems, name=name,
        compiler_params=_params(("arbitrary",) * len(grid)))(*args, *rider.args)
    rider.result = list(res[counts[2]:])
    return list(res[:counts[2]]) if multi else res[0]


def _mm_nn(a, b3, *, name, tm, tn, out_dtype, res=None, alpha=1.0, rider=None):
    m, k = a.shape
    nb, _, nw = b3.shape
    per = nw // tn
    assert nw % tn == 0 and m % tm == 0
    has_res = res is not None

    def body(*refs):
        if has_res:
            a_ref, b_ref, r_ref, o_ref = refs
        else:
            a_ref, b_ref, o_ref = refs
        acc = _dot(_bf(a_ref[...]), b_ref[...])
        if alpha != 1.0:
            acc = alpha * acc
        if has_res:
            acc = r_ref[...] + acc
        o_ref[...] = acc.astype(o_ref.dtype)

    in_specs = [pl.BlockSpec((tm, k), lambda i, j: (i, 0)),
                pl.BlockSpec((None, k, tn), lambda i, j: (j // per, 0, j % per))]
    args = [a, b3]
    if has_res:
        in_specs.append(pl.BlockSpec((tm, tn), lambda i, j: (i, j)))
        args.append(res)
    return _pcall(body, grid=(m // tm, nb * per), in_specs=in_specs, out_specs=pl.BlockSpec((tm, tn), lambda i, j: (i, j)),
                  out_shape=jax.ShapeDtypeStruct((m, nb * nw), out_dtype), name=name, sem=("parallel", "arbitrary"),
                  args=args, rider=rider)


def _mm_nt(d, b3, *, name, tm, tp, tn, out_dtype, alpha=1.0, rider=None, norm=None):
    m, n = d.shape
    nb, p, nw = b3.shape
    per = nw // tn
    nk = n // tn
    assert nb * nw == n and nw % tn == 0 and p % tp == 0 and m % tm == 0 and (norm is None or tp == p)

    def body(d_ref, b_ref, *refs):
        kk = pl.program_id(2)
        acc_ref = refs[-1]

        @pl.when(kk == 0)
        def _():
            acc_ref[...] = jnp.zeros_like(acc_ref)

        acc_ref[...] += _dot_nt(_bf(d_ref[...]), b_ref[...])

        if norm is None:
            @pl.when(kk == nk - 1)
            def _():
                refs[0][...] = (alpha * acc_ref[...]).astype(refs[0].dtype)
        else:
            x_ref, g_ref, dx_ref, o_ref, dg_ref = refs[:5]

            @pl.when(jnp.logical_and(pl.program_id(0) == 0, kk == 0))
            def _():
                dg_ref[...] = jnp.zeros_like(dg_ref)

            @pl.when(kk == nk - 1)
            def _():
                dh = alpha * acc_ref[...]
                xv = x_ref[...]
                r = _rms_rows(xv)
                xh = xv * r
                dxh = dh * g_ref[...]
                o_ref[...] = dx_ref[...] + r * (dxh - xh * jnp.mean(dxh * xh, axis=1, keepdims=True))
                dg_ref[...] += jnp.sum(dh * xh, axis=0, keepdims=True)

    in_specs = [pl.BlockSpec((tm, tn), lambda i, j, kk: (i, kk)),
                pl.BlockSpec((None, tp, tn), lambda i, j, kk: (kk // per, j, kk % per))]
    tile = pl.BlockSpec((tm, tp), lambda i, j, kk: (i, j))
    if norm is None:
        return _pcall(body, grid=(m // tm, p // tp, nk), in_specs=in_specs, out_specs=tile,
                      out_shape=jax.ShapeDtypeStruct((m, p), out_dtype), scratch_shapes=[pltpu.VMEM((tm, tp), F32)],
                      name=name, sem=("parallel", "parallel", "arbitrary"), args=(d, b3), rider=rider)
    x, g, dx = norm
    row = pl.BlockSpec((1, p), lambda i, j, kk: (0, 0))
    return _pcall(body, grid=(m // tm, 1, nk), in_specs=in_specs + [tile, row, tile], out_specs=[tile, row],
                  out_shape=[jax.ShapeDtypeStruct((m, p), F32), jax.ShapeDtypeStruct((1, p), F32)],
                  scratch_shapes=[pltpu.VMEM((tm, tp), F32)], name=name, sem=("arbitrary", "arbitrary", "arbitrary"),
                  args=(d, b3, x, g, dx), rider=rider)


def _mm_tn(a, d, *, nb, name, tm, tk, tn, alpha=1.0, rider=None):
    m, k = a.shape
    _, n = d.shape
    nw = n // nb
    per = nw // tn
    nm = m // tm
    assert nw % tn == 0 and k % tk == 0 and m % tm == 0

    def body(a_ref, d_ref, o_ref, acc_ref):
        mm = pl.program_id(2)

        @pl.when(mm == 0)
        def _():
            acc_ref[...] = jnp.zeros_like(acc_ref)

        acc_ref[...] += _dot_tn(_bf(a_ref[...]), _bf(d_ref[...]))

        @pl.when(mm == nm - 1)
        def _():
            o_ref[...] = (alpha * acc_ref[...]).astype(o_ref.dtype)

    return _pcall(
        body, grid=(k // tk, nb * per, nm),
        in_specs=[pl.BlockSpec((tm, tk), lambda i, j, mm: (mm, i)),
                  pl.BlockSpec((tm, tn), lambda i, j, mm: (mm, j))],
        out_specs=pl.BlockSpec((None, tk, tn), lambda i, j, mm: (j // per, i, j % per)),
        out_shape=jax.ShapeDtypeStruct((nb, k, nw), BF16),
        scratch_shapes=[pltpu.VMEM((tk, tn), F32)],
        name=name, sem=("parallel", "parallel", "arbitrary"), args=(a, d), rider=rider)


def _rows_from_view(ref, buf, w, d, tm):
    for k in range(d):
        for c in range(w // HEAD):
            lanes = slice(k * w + c * HEAD, k * w + (c + 1) * HEAD)
            buf.at[c][pl.ds(k, tm // d, stride=d), :] = ref[:, lanes].astype(F32)
    return _cat([buf[c] for c in range(w // HEAD)])


def _ew(fn, ins, outs, *, rows, tm, name):
    in_specs, args, scratch = [], [], []
    for s in ins:
        if s[0] == 't':
            _, arr, w, cb = s
            in_specs.append(pl.BlockSpec((tm, w), lambda i, cb=cb: (i, cb)))
        elif s[0] == 'v':
            _, arr, w, d = s
            in_specs.append(pl.BlockSpec((tm // d, d * w), lambda i: (i, 0)))
            scratch.append(pltpu.VMEM((w // HEAD, tm, HEAD), F32))
        else:
            arr = s[1]
            in_specs.append(pl.BlockSpec(arr.shape, lambda i, nd=arr.ndim: (0,) * nd))
        args.append(arr)
    out_specs, out_shape = [], []
    for s in outs:
        if s[0] == 't':
            _, w, dt = s
            out_specs.append(pl.BlockSpec((tm, w), lambda i: (i, 0)))
            out_shape.append(jax.ShapeDtypeStruct((rows, w), dt))
        elif s[0] == 'v':
            _, w, dt, d = s
            out_specs.append(pl.BlockSpec((tm // d, d * w), lambda i: (i, 0)))
            out_shape.append(jax.ShapeDtypeStruct((rows // d, d * w), dt))
            scratch.append(pltpu.VMEM((w // HEAD, tm, HEAD), F32))
        else:
            out_specs.append(pl.BlockSpec(s[1], lambda i: (0, 0)))
            out_shape.append(jax.ShapeDtypeStruct(s[1], F32))
    n_in, n_out = len(ins), len(outs)

    def body(*refs):
        bufs = list(refs[n_in + n_out:])
        vals = []
        for r, s in zip(refs[:n_in], ins):
            if s[0] == 'v':
                vals.append(_rows_from_view(r, bufs.pop(0), s[2], s[3], tm))
            else:
                vals.append(r[...])
        res = fn(*vals)
        if not isinstance(res, (tuple, list)):
            res = (res,)
        for r, s, v in zip(refs[n_in:n_in + n_out], outs, res):
            if s[0] == 't':
                r[...] = v.astype(r.dtype)
            elif s[0] == 'v':
                w, d, buf = s[1], s[3], bufs.pop(0)
                for c in range(w // HEAD):
                    buf[c] = v[:, c * HEAD:(c + 1) * HEAD].astype(F32)
                for k in range(d):
                    for c in range(w // HEAD):
                        lanes = slice(k * w + c * HEAD, k * w + (c + 1) * HEAD)
                        r[:, lanes] = buf.at[c][pl.ds(k, tm // d, stride=d), :].astype(r.dtype)
            else:
                @pl.when(pl.program_id(0) == 0)
                def _(r=r):
                    r[...] = jnp.zeros_like(r)

                r[...] += v

    res = pl.pallas_call(
        body, grid=(rows // tm,), in_specs=in_specs, out_specs=out_specs, out_shape=out_shape, scratch_shapes=scratch,
        name=name, compiler_params=_params(("arbitrary",)))(*args)
    return res


def _tile(arr, w, g):
    return ('t', arr, w, 0) if DILATIONS[g] == 1 else ('v', arr, w, DILATIONS[g])


def _tile_out(w, dtype, g):
    return ('t', w, dtype) if DILATIONS[g] == 1 else ('v', w, dtype, DILATIONS[g])


def _heads(x):
    return [x[:, h * HEAD:(h + 1) * HEAD] for h in range(x.shape[1] // HEAD)]


def _cat(xs):
    return jnp.concatenate(xs, axis=1)


def _head_mean(x):
    return _cat([jnp.broadcast_to(jnp.mean(h, axis=1, keepdims=True), h.shape) for h in _heads(x)])


def _rms_rows(x):
    return lax.rsqrt(jnp.mean(x * x, axis=1, keepdims=True) + EPS)


def _norm_fwd(x, g, name):
    return _ew(lambda xv, gv: xv * _rms_rows(xv) * gv,
               [('t', x, D_MODEL, 0), ('f', g)], [('t', D_MODEL, BF16)], rows=x.shape[0], tm=512, name=name)[0]


def _loss_fwd_bwd(y, target, name):
    def fn(yv, tv):
        e = yv - tv
        return e * (1.0 / D_MODEL), jnp.sum(e * e, axis=0, keepdims=True)

    return _ew(fn, [('t', y, D_MODEL, 0), ('t', target, D_MODEL, 0)], [('t', D_MODEL, F32), ('acc', (1, D_MODEL))],
               rows=y.shape[0], tm=512, name=name)


def _rot(x):
    sgn = jnp.where(lax.broadcasted_iota(jnp.int32, x.shape, 1) < HEAD // 2, -1.0, 1.0)
    return pltpu.roll(x, HEAD // 2, 1) * sgn


def _gain_rows(qn, kn):
    return [a[g:g + 1] for a in (qn, kn) for g in range(ATT_GROUPS)]


def _qk_fwd(proj, cos, sin, qn, kn, name):
    def fn(*v):
        xs, cosv, sinv, gains, vs = v[:6], v[6], v[7], v[8:14], v[14:17]
        outs = []
        for j, x in enumerate(xs):
            gain = gains[j]
            ys = []
            for xh in _heads(x.astype(F32)):
                xn = xh * _rms_rows(xh) * gain
                ys.append(xn * cosv + _rot(xn) * sinv)
            outs.append(_cat(ys))
        return outs + list(vs)

    ins = ([('t', proj, 512, CB_AQ + j) for j in range(6)] + [('t', cos, HEAD, 0), ('t', sin, HEAD, 0)]
           + [('f', a) for a in _gain_rows(qn, kn)] + [('t', proj, 512, CB_AV + g) for g in range(ATT_GROUPS)])
    return _ew(fn, ins, [_tile_out(ATT_GW, BF16, j % ATT_GROUPS) for j in range(9)], rows=proj.shape[0], tm=512, name=name)


def _qk_bwd(dqk, proj, cos, sin, qn, kn, name):
    def fn(*v):
        ds, xs, cosv, sinv, gains = v[:6], v[6:12], v[12], v[13], v[14:20]
        rows8 = lax.broadcasted_iota(jnp.int32, (8, HEAD), 0)
        outs, dgs = [], [jnp.zeros((8, HEAD), F32)] * 2
        for j in range(6):
            gain = gains[j]
            dx, dg = [], jnp.zeros((1, HEAD), F32)
            for dyh, xh in zip(_heads(ds[j]), _heads(xs[j].astype(F32))):
                r = _rms_rows(xh)
                xhat = xh * r
                dxn = dyh * cosv - _rot(dyh * sinv)
                dg = dg + jnp.sum(dxn * xhat, axis=0, keepdims=True)
                dxh = dxn * gain
                dx.append(r * (dxh - xhat * jnp.mean(dxh * xhat, axis=1, keepdims=True)))
            outs.append(_cat(dx))
            dgs[j // 3] = dgs[j // 3] + jnp.where(rows8 == j % 3, dg, 0.0)
        return _cat(outs), dgs[0], dgs[1]

    ins = ([_tile(a, ATT_GW, j % ATT_GROUPS) for j, a in enumerate(dqk)] + [('t', proj, 512, CB_AQ + j) for j in range(6)]
           + [('t', cos, HEAD, 0), ('t', sin, HEAD, 0)] + [('f', a) for a in _gain_rows(qn, kn)])
    return _ew(fn, ins, [('t', 6 * ATT_GW, BF16), ('acc', (8, HEAD)), ('acc', (8, HEAD))],
               rows=proj.shape[0], tm=512, name=name)


def _pick(x, h):
    lanes = lax.broadcasted_iota(jnp.int32, x.shape, 1)
    return jnp.sum(jnp.where(lanes == h, x, 0.0), axis=1, keepdims=True)


def _spread(x):
    return _cat([jnp.broadcast_to(_pick(x, h), (x.shape[0], HEAD)) for h in range(ATT_HEADS)])


def _compact(x):
    lanes = lax.broadcasted_iota(jnp.int32, (x.shape[0], HEAD), 1)
    out = jnp.zeros((x.shape[0], HEAD), F32)
    for h, xh in enumerate(_heads(x)):
        out = jnp.where(lanes == h, xh, out)
    return out


def _group_weights(l0, l1, l2):
    l0, l1, l2 = _spread(l0), _spread(l1), _spread(l2)
    m = jnp.maximum(jnp.maximum(l0, l1), l2)
    e0, e1, e2 = jnp.exp(l0 - m), jnp.exp(l1 - m), jnp.exp(l2 - m)
    inv = 1.0 / (e0 + e1 + e2)
    return e0 * inv, e1 * inv, e2 * inv


def _merge_fwd(outs, lses, name):
    def fn(o0, o1, o2, l0, l1, l2):
        a0, a1, a2 = _group_weights(l0, l1, l2)
        return a0 * o0 + a1 * o1 + a2 * o2

    ins = [_tile(a, ATT_GW, g) for g, a in enumerate(outs)] + [_tile(a, HEAD, g) for g, a in enumerate(lses)]
    return _ew(fn, ins, [('t', ATT_GW, BF16)], rows=outs[0].shape[0], tm=512, name=name)[0]


def _merge_bwd(dob, outs, lses, name):
    def fn(dov, o0, o1, o2, l0, l1, l2):
        a0, a1, a2 = _group_weights(l0, l1, l2)
        ob = a0 * o0 + a1 * o1 + a2 * o2
        s = _head_mean(dov * ob) * float(HEAD)
        return a0 * dov, a1 * dov, a2 * dov, _compact(a0 * s), _compact(a1 * s), _compact(a2 * s)

    ins = ([('t', dob, ATT_GW, 0)] + [_tile(a, ATT_GW, g) for g, a in enumerate(outs)]
           + [_tile(a, HEAD, g) for g, a in enumerate(lses)])
    groups = range(ATT_GROUPS)
    return _ew(fn, ins, [_tile_out(ATT_GW, BF16, g) for g in groups] + [_tile_out(HEAD, F32, g) for g in groups],
               rows=dob.shape[0], tm=512, name=name)


HG_ROWS = 256


def _hg_gates(hq, hf, hi, lbv):
    sig = _sig(hf)
    f = lbv + (1.0 - lbv) * sig
    return hq * _sig(hq), 1.0 - f, hi, jnp.log(f), sig, f


def _split3(x):
    hi = _bf(x)
    r1 = x - hi.astype(F32)
    mid = _bf(r1)
    return hi, mid, _bf(r1 - mid.astype(F32))


def _tri_dot(tri, x):
    hi, mid, lo = _split3(x)
    return _dot(tri, hi) + _dot(tri, mid) + _dot(tri, lo)


def _row(x, i):
    rows = lax.broadcasted_iota(jnp.int32, x.shape, 0)
    return jnp.sum(jnp.where(rows == i, x, 0.0), axis=0, keepdims=True)


def _hg_decay(logf, q, k):
    c = HG_CHUNK
    row = lax.broadcasted_iota(jnp.int32, (c, c), 0)
    col = lax.broadcasted_iota(jnp.int32, (c, c), 1)
    g = _tri_dot((row >= col).astype(BF16), logf)
    gm = _row(g, c // 2 - 1)
    gl = _row(g, c - 1)
    decays = jnp.exp(g), jnp.exp(g - gm), jnp.exp(gm - g), jnp.exp(gl - g)
    return gl, decays, q * decays[0], q * decays[1], k * decays[2], k * decays[3]


def _hg_out_fwd(o, hg, gain):
    r = lax.rsqrt(_head_mean(o * o) + EPS)
    return o * r * gain * (hg * _sig(hg))


def _hgrn_fwd(proj, hf, lb, gain, name, rider=None):
    t = proj.shape[0]
    nck = HG_ROWS // HG_CHUNK

    def body(hq_ref, hf_ref, hi_ref, hg_ref, lb_ref, gn_ref, o_ref, oa_ref, sall_ref, st_ref):
        @pl.when(pl.program_id(0) == 0)
        def _():
            st_ref[...] = jnp.zeros_like(st_ref)

        lbv = lb_ref[...]
        gnv = gn_ref[...]
        c = HG_CHUNK
        mask = lax.broadcasted_iota(jnp.int32, (c, c), 0) >= lax.broadcasted_iota(jnp.int32, (c, c), 1)

        def chunk(cc, carry):
            sl = pl.ds(pl.multiple_of(cc * c, c), c)
            q, k, v, logf, _, _ = _hg_gates(hq_ref[sl, :].astype(F32), hf_ref[sl, :], hi_ref[sl, :].astype(F32), lbv)
            gl, _, qg, qt, kt, kd = _hg_decay(logf, q, k)
            egl = jnp.exp(gl)
            os = []
            for h in range(HG_HEADS):
                hs = slice(h * HEAD, (h + 1) * HEAD)
                st = st_ref[h]
                sall_ref[cc, h] = st
                a = jnp.where(mask, _dot_nt(_bf(qt[:, hs]), _bf(kt[:, hs])), 0.0)
                os.append(_dot(_bf(a), _bf(v[:, hs])) + _dot_nt(_bf(qg[:, hs]), _bf(st)))
                st_ref[h] = egl[:, hs] * st + _dot_tn(_bf(v[:, hs]), _bf(kd[:, hs]))
            o = _cat(os)
            o_ref[sl, :] = o
            oa_ref[sl, :] = _hg_out_fwd(o, hg_ref[sl, :].astype(F32), gnv).astype(oa_ref.dtype)
            return carry

        lax.fori_loop(0, nck, chunk, 0)

    col = lambda j: pl.BlockSpec((HG_ROWS, D_MODEL), lambda i, j=j: (i, j))
    small = pl.BlockSpec((1, D_MODEL), lambda i: (0, 0))
    return _pcall(
        body, grid=(t // HG_ROWS,),
        in_specs=[col(0), col(0), col(2), col(3), small, small],
        out_specs=[col(0), col(0), pl.BlockSpec((nck, HG_HEADS, HEAD, HEAD), lambda i: (i, 0, 0, 0))],
        out_shape=[jax.ShapeDtypeStruct((t, D_MODEL), F32), jax.ShapeDtypeStruct((t, D_MODEL), BF16),
                   jax.ShapeDtypeStruct((t // HG_CHUNK, HG_HEADS, HEAD, HEAD), F32)],
        scratch_shapes=[pltpu.VMEM((HG_HEADS, HEAD, HEAD), F32)],
        name=name, sem=("arbitrary",), args=(proj, hf, proj, proj, lb, gain), rider=rider)


def _terms(x, precise):
    hi = _bf(x)
    return (hi, _bf(x - hi.astype(F32))) if precise else (hi,)


def _mm(dot, a, b):
    out = dot(a[0], b[0])
    if len(a) > 1:
        out = out + dot(a[1], b[0])
    if len(b) > 1:
        out = out + dot(a[0], b[1])
    return out


def _hgrn_bwd(doa, oscan, proj, hf, sall, lb, gain, dqk, dvs, dgab, name, precise, rider=None):
    t = proj.shape[0]
    nck = HG_ROWS // HG_CHUNK
    nsteps = t // HG_ROWS
    terms = functools.partial(_terms, precise=precise)
    n_view = sum(d > 1 for d in DILATIONS)

    def body(doa_ref, os_ref, hq_ref, hf_ref, hi_ref, hg_ref, sall_ref, lb_ref, gn_ref, dqk_ref, dv0_ref, dv1_ref,
             dv2_ref, dgab_ref, dproj_ref, dgn_ref, dlb_ref, dst_ref, *bufs):
        @pl.when(pl.program_id(0) == 0)
        def _():
            dst_ref[...] = jnp.zeros_like(dst_ref)
            dgn_ref[...] = jnp.zeros_like(dgn_ref)
            dlb_ref[...] = jnp.zeros_like(dlb_ref)

        at = 4 * D_MODEL
        dproj_ref[:, at:at + 6 * ATT_GW] = dqk_ref[...]
        at += 6 * ATT_GW
        spare = list(bufs)
        for d, dv_ref in zip(DILATIONS, (dv0_ref, dv1_ref, dv2_ref)):
            dv = dv_ref[...] if d == 1 else _rows_from_view(dv_ref, spare.pop(0), ATT_GW, d, HG_ROWS)
            dproj_ref[:, at:at + ATT_GW] = dv.astype(dproj_ref.dtype)
            at += ATT_GW
        dproj_ref[:, at:] = dgab_ref[...]

        lbv = lb_ref[...]
        gnv = gn_ref[...]
        c = HG_CHUNK
        row = lax.broadcasted_iota(jnp.int32, (c, c), 0)
        colm = lax.broadcasted_iota(jnp.int32, (c, c), 1)
        mask = row >= colm
        triu = (row <= colm).astype(BF16)
        last = lax.broadcasted_iota(jnp.int32, (c, HEAD), 0) == c - 1

        def chunk(ci, carry):
            cc = nck - 1 - ci
            sl = pl.ds(pl.multiple_of(cc * c, c), c)
            hq, hg = hq_ref[sl, :].astype(F32), hg_ref[sl, :].astype(F32)
            q, k, v, logf, sig, f = _hg_gates(hq, hf_ref[sl, :], hi_ref[sl, :].astype(F32), lbv)
            gl, (e_qg, e_qt, e_kt, e_kd), qg, qt, kt, kd = _hg_decay(logf, q, k)
            egl = jnp.exp(gl)
            o = os_ref[sl, :]
            dy = doa_ref[sl, :]
            r = lax.rsqrt(_head_mean(o * o) + EPS)
            oh = o * r
            sg = _sig(hg)
            silu_g = hg * sg
            dgn_ref[...] += jnp.sum(dy * oh * silu_g, axis=0, keepdims=True)
            dhg = dy * oh * gnv * (sg * (1.0 + hg * (1.0 - sg)))
            doh = dy * gnv * silu_g
            do = r * (doh - oh * _head_mean(doh * oh))
            dqs, dks, dvs, dgs = [], [], [], []
            for h in range(HG_HEADS):
                hs = slice(h * HEAD, (h + 1) * HEAD)
                st = sall_ref[cc, h]
                dst = dst_ref[h]
                qt_h, kt_h, qg_h, kd_h = qt[:, hs], kt[:, hs], qg[:, hs], kd[:, hs]
                do_p, v_p, qt_p, kt_p, qg_p = terms(do[:, hs]), terms(v[:, hs]), terms(qt_h), terms(kt_h), terms(qg_h)
                st_p, dst_p = terms(st), terms(dst)
                a = jnp.where(mask, _dot_nt(qt_p[0], kt_p[0]), 0.0)
                da = terms(jnp.where(mask, _mm(_dot_nt, do_p, v_p), 0.0))
                dqt = _mm(_dot, da, kt_p)
                dkt = _mm(_dot_tn, da, qt_p)
                dqg = _mm(_dot, do_p, st_p)
                dv = _dot_tn(_bf(a), do_p[0]) + _dot_nt(_bf(kd_h), dst_p[0])
                dkd = _mm(_dot, v_p, dst_p)
                dgl = egl[:, hs] * jnp.sum(st * dst, axis=0, keepdims=True) + jnp.sum(dkd * kd_h, axis=0, keepdims=True)
                dst_ref[h] = egl[:, hs] * dst + _mm(_dot_tn, do_p, qg_p)
                dqs.append(dqt * e_qt[:, hs] + dqg * e_qg[:, hs])
                dks.append(dkt * e_kt[:, hs] + dkd * e_kd[:, hs])
                dvs.append(dv)
                dgs.append(dqt * qt_h - dkt * kt_h + dqg * qg_h - dkd * kd_h + jnp.where(last, dgl, 0.0))
            dq, dk, dv, dg = _cat(dqs), _cat(dks), _cat(dvs), _cat(dgs)
            dlogf = _tri_dot(triu, dg)
            df = dlogf / f - dk
            dlb_ref[...] += jnp.sum(df * (1.0 - sig), axis=0, keepdims=True)
            dhf = df * (1.0 - lbv) * sig * (1.0 - sig)
            sq = _sig(hq)
            dhq = dq * (sq * (1.0 + hq * (1.0 - sq)))
            dproj_ref[sl, :4 * D_MODEL] = _cat([dhq, dhf, dv, dhg]).astype(dproj_ref.dtype)
            return carry

        lax.fori_loop(0, nck, chunk, 0)

    rev = lambda j: pl.BlockSpec((HG_ROWS, D_MODEL), lambda i, j=j: (nsteps - 1 - i, j))
    rows = lambda a, d=1: pl.BlockSpec((HG_ROWS // d, a.shape[1]), lambda i: (nsteps - 1 - i, 0))
    small = pl.BlockSpec((1, D_MODEL), lambda i: (0, 0))
    return _pcall(
        body, grid=(nsteps,),
        in_specs=[rev(0), rev(0), rev(0), rev(0), rev(2), rev(3),
                  pl.BlockSpec((nck, HG_HEADS, HEAD, HEAD), lambda i: (nsteps - 1 - i, 0, 0, 0)), small, small,
                  rows(dqk)] + [rows(a, d) for a, d in zip(dvs, DILATIONS)] + [rows(dgab)],
        out_specs=[pl.BlockSpec((HG_ROWS, P_IN), lambda i: (nsteps - 1 - i, 0)), small, small],
        out_shape=[jax.ShapeDtypeStruct((t, P_IN), BF16), jax.ShapeDtypeStruct((1, D_MODEL), F32),
                   jax.ShapeDtypeStruct((1, D_MODEL), F32)],
        scratch_shapes=[pltpu.VMEM((HG_HEADS, HEAD, HEAD), F32)] + [pltpu.VMEM((ATT_HEADS, HG_ROWS, HEAD), F32)] * n_view,
        name=name, sem=("arbitrary",), args=(doa, oscan, proj, hf, proj, proj, sall, lb, gain, dqk, *dvs, dgab),
        rider=rider)


def _window_masks(has_previous):
    qi = lax.broadcasted_iota(jnp.int32, (ATT_BLK, 2 * ATT_BLK), 0)
    ki = lax.broadcasted_iota(jnp.int32, (ATT_BLK, 2 * ATT_BLK), 1)
    band = jnp.logical_and(ki >= qi, ki <= qi + ATT_BLK)
    return band, jnp.logical_and(band, jnp.logical_or(ki >= ATT_BLK, has_previous))


def _two_blocks(ref, prev_ref, j, hs):
    if j == 0:
        return jnp.concatenate([prev_ref[:, hs], ref[0:ATT_BLK, hs]], axis=0)
    return ref[(j - 1) * ATT_BLK:(j + 1) * ATT_BLK, hs]


def _attn_cfg(qg, g):
    d = DILATIONS[g]
    length = qg.shape[0]
    assert qg.shape[1] == d * ATT_GW
    nb = length // ATT_BLK
    return d, length, nb, min(ATT_STEP_BLOCKS, nb)


def _attn_fwd(qg, kg, vg, g, name):
    d, length, nb, rb = _attn_cfg(qg, g)
    scale = HEAD ** -0.5

    def body(q_ref, k_ref, v_ref, kp_ref, vp_ref, o_ref, l_ref):
        n = pl.program_id(1)
        band, first_band = _window_masks(n > 0)
        lanes = lax.broadcasted_iota(jnp.int32, (ATT_BLK, HEAD), 1)
        for j in range(rb):
            rows = slice(j * ATT_BLK, (j + 1) * ATT_BLK)
            lse = jnp.zeros((ATT_BLK, HEAD), F32)
            for h in range(ATT_HEADS):
                hs = slice(h * HEAD, (h + 1) * HEAD)
                k2, v2 = _two_blocks(k_ref, kp_ref, j, hs), _two_blocks(v_ref, vp_ref, j, hs)
                s = jnp.where(first_band if j == 0 else band, _dot_nt(q_ref[rows, hs], k2) * scale, NEG)
                m = jnp.max(s, axis=1, keepdims=True)
                p = jnp.exp(s - m)
                l = jnp.sum(p, axis=1, keepdims=True)
                o_ref[rows, hs] = (_dot(_bf(p), v2) / l).astype(o_ref.dtype)
                lse = jnp.where(lanes == h, m + jnp.log(l), lse)
            l_ref[rows, :] = lse

    own = pl.BlockSpec((rb * ATT_BLK, ATT_GW), lambda r, n: (n, r))
    own_head = pl.BlockSpec((rb * ATT_BLK, HEAD), lambda r, n: (n, r))
    prev = pl.BlockSpec((ATT_BLK, ATT_GW), lambda r, n: (jnp.maximum(n * rb - 1, 0), r))
    return pl.pallas_call(
        body, grid=(d, nb // rb), in_specs=[own, own, own, prev, prev], out_specs=[own, own_head],
        out_shape=[jax.ShapeDtypeStruct((length, d * ATT_GW), BF16), jax.ShapeDtypeStruct((length, d * HEAD), F32)],
        name=name, compiler_params=_params(("parallel", "arbitrary")))(qg, kg, vg, kg, vg)


def _attn_bwd(qg, kg, vg, dog, lse, delta, g, name):
    d, length, nb, rb = _attn_cfg(qg, g)
    nsteps = nb // rb
    scale = HEAD ** -0.5

    def body(q_ref, k_ref, v_ref, do_ref, l_ref, dl_ref, kp_ref, vp_ref, qn_ref, don_ref, ln_ref, dln_ref,
             dq_ref, dk_ref, dv_ref):
        n = pl.program_id(1)
        band, first_band = _window_masks(n > 0)
        qi = lax.broadcasted_iota(jnp.int32, (ATT_BLK, ATT_BLK), 0)
        ki = lax.broadcasted_iota(jnp.int32, (ATT_BLK, ATT_BLK), 1)
        next_m = jnp.logical_and(ki >= qi, n < nsteps - 1)
        last = slice((rb - 1) * ATT_BLK, rb * ATT_BLK)
        for h in range(ATT_HEADS):
            hs = slice(h * HEAD, (h + 1) * HEAD)
            dk, dv = [None] * rb, [None] * rb
            for j in range(rb):
                rows = slice(j * ATT_BLK, (j + 1) * ATT_BLK)
                q, do = q_ref[rows, hs], do_ref[rows, hs]
                k2, v2 = _two_blocks(k_ref, kp_ref, j, hs), _two_blocks(v_ref, vp_ref, j, hs)
                p = jnp.where(first_band if j == 0 else band,
                              jnp.exp(_dot_nt(q, k2) * scale - _pick(l_ref[rows, :], h)), 0.0)
                ds = _bf(p * (_dot_nt(do, v2) - _pick(dl_ref[rows, :], h)) * scale)
                dq_ref[rows, hs] = _dot(ds, k2).astype(dq_ref.dtype)
                dk2, dv2 = _dot_tn(ds, q), _dot_tn(_bf(p), do)
                if j >= 1:
                    dk[j - 1] = dk[j - 1] + dk2[:ATT_BLK]
                    dv[j - 1] = dv[j - 1] + dv2[:ATT_BLK]
                dk[j], dv[j] = dk2[ATT_BLK:], dv2[ATT_BLK:]
            q, do = qn_ref[:, hs], don_ref[:, hs]
            p = jnp.where(next_m, jnp.exp(_dot_nt(q, k_ref[last, hs]) * scale - _pick(ln_ref[...], h)), 0.0)
            ds = _bf(p * (_dot_nt(do, v_ref[last, hs]) - _pick(dln_ref[...], h)) * scale)
            dk[rb - 1] = dk[rb - 1] + _dot_tn(ds, q)
            dv[rb - 1] = dv[rb - 1] + _dot_tn(_bf(p), do)
            for j in range(rb):
                rows = slice(j * ATT_BLK, (j + 1) * ATT_BLK)
                dk_ref[rows, hs] = dk[j].astype(dk_ref.dtype)
                dv_ref[rows, hs] = dv[j].astype(dv_ref.dtype)

    own = pl.BlockSpec((rb * ATT_BLK, ATT_GW), lambda r, n: (n, r))
    prev = pl.BlockSpec((ATT_BLK, ATT_GW), lambda r, n: (jnp.maximum(n * rb - 1, 0), r))
    nxt = pl.BlockSpec((ATT_BLK, ATT_GW), lambda r, n: (jnp.minimum((n + 1) * rb, nb - 1), r))
    own_head = pl.BlockSpec((rb * ATT_BLK, HEAD), lambda r, n: (n, r))
    nxt_head = pl.BlockSpec((ATT_BLK, HEAD), lambda r, n: (jnp.minimum((n + 1) * rb, nb - 1), r))
    return pl.pallas_call(
        body, grid=(d, nsteps), in_specs=[own] * 4 + [own_head] * 2 + [prev, prev, nxt, nxt, nxt_head, nxt_head],
        out_specs=[own, own, own], out_shape=[jax.ShapeDtypeStruct((length, d * ATT_GW), BF16)] * 3,
        name=name, compiler_params=_params(("parallel", "arbitrary")))(
            qg, kg, vg, dog, lse, delta, kg, vg, qg, dog, lse, delta)


def _rope_tables(t):
    pos = jnp.arange(t, dtype=F32)
    inv = ROPE_THETA ** (-jnp.arange(0, HEAD, 2, dtype=F32) / HEAD)
    ang = pos[:, None] * inv[None, :]
    ang = jnp.concatenate([ang, ang], axis=-1)
    return jnp.cos(ang), jnp.sin(ang)


def _lower_bounds(logits):
    lb = jnp.cumsum(jax.nn.softmax(logits.astype(F32), axis=0), axis=0)
    return lb - lb[0:1]


FFN_ROWS = 256
FF_SHARD = 2 * D_FF // N_CHIPS


def _ffn_in_act(x, g, w_in, name, rider=None):
    t = x.shape[0]

    def body(x_ref, g_ref, w_ref, h_ref, ab_ref, u_ref):
        xv = x_ref[...]
        h = _bf(xv * _rms_rows(xv) * g_ref[...])
        h_ref[...] = h
        for s in range(N_CHIPS // 2):
            cols = slice(s * FF_SHARD, (s + 1) * FF_SHARD)
            a = _dot(h, w_ref[s])
            b = _dot(h, w_ref[s + N_CHIPS // 2])
            ab_ref[:, cols] = a.astype(ab_ref.dtype)
            ab_ref[:, D_FF + s * FF_SHARD:D_FF + (s + 1) * FF_SHARD] = b.astype(ab_ref.dtype)
            u_ref[:, cols] = (a * _sig(a) * b).astype(u_ref.dtype)

    row = lambda w: pl.BlockSpec((FFN_ROWS, w), lambda i: (i, 0))
    return _pcall(
        body, grid=(t // FFN_ROWS,),
        in_specs=[row(D_MODEL), pl.BlockSpec((1, D_MODEL), lambda i: (0, 0)),
                  pl.BlockSpec(w_in.shape, lambda i: (0, 0, 0))],
        out_specs=[row(D_MODEL), row(2 * D_FF), row(D_FF)],
        out_shape=[jax.ShapeDtypeStruct((t, D_MODEL), BF16), jax.ShapeDtypeStruct((t, 2 * D_FF), BF16),
                   jax.ShapeDtypeStruct((t, D_FF), BF16)],
        name=name, sem=("parallel",), args=(x, g, w_in), rider=rider)


def _ffn_bwd_du_act(dx, w_out, ab, name, rider=None):
    t = dx.shape[0]

    def body(dx_ref, w_ref, ab_ref, o_ref):
        du = 0.5 * _dot_nt(_bf(dx_ref[...]), w_ref[0])
        a = ab_ref[:, :D_FF].astype(F32)
        b = ab_ref[:, D_FF:].astype(F32)
        s = _sig(a)
        o_ref[:, :D_FF] = (du * b * (s * (1.0 + a * (1.0 - s)))).astype(o_ref.dtype)
        o_ref[:, D_FF:] = (du * a * s).astype(o_ref.dtype)

    row = lambda w: pl.BlockSpec((FFN_ROWS, w), lambda i: (i, 0))
    return _pcall(
        body, grid=(t // FFN_ROWS,),
        in_specs=[row(D_MODEL), pl.BlockSpec(w_out.shape, lambda i: (0, 0, 0)), row(2 * D_FF)],
        out_specs=row(2 * D_FF), out_shape=jax.ShapeDtypeStruct((t, 2 * D_FF), BF16),
        name=name, sem=("parallel",), args=(dx, w_out, ab), rider=rider)


MIX_ROWS = 512


def _gate_specs():
    return [pl.BlockSpec((MIX_ROWS, 512), lambda i, cb=cb: (i, cb)) for cb in (CB_GA, CB_GA + 1, CB_GB, CB_GB + 1)]


def _gate(lo_ref, hi_ref):
    return _sig(_cat([lo_ref[...], hi_ref[...]]).astype(F32))


def _whole(a):
    return pl.BlockSpec(a.shape, lambda i: (0,) * a.ndim)


def _mix_tail_fwd(oa, ob, proj, x, w_a, w_b, w_o, name):
    t = x.shape[0]

    def body(oa_ref, ob_ref, ga0, ga1, gb0, gb1, x_ref, wa_ref, wb_ref, wo_ref, y_ref, m_ref, ya_ref, yb_ref):
        ya = _dot(oa_ref[...], wa_ref[0])
        yb = _cat([_dot(ob_ref[...], wb_ref[s]) for s in range(N_CHIPS)])
        merged = _bf(_gate(ga0, ga1) * ya + _gate(gb0, gb1) * yb)
        m_ref[...] = merged
        ya_ref[...] = ya.astype(ya_ref.dtype)
        yb_ref[...] = yb.astype(yb_ref.dtype)
        y_ref[...] = x_ref[...] + _dot(merged, wo_ref[0])

    row = lambda w: pl.BlockSpec((MIX_ROWS, w), lambda i: (i, 0))
    return pl.pallas_call(
        body, grid=(t // MIX_ROWS,),
        in_specs=[row(D_MODEL), row(ATT_GW)] + _gate_specs() + [row(D_MODEL), _whole(w_a), _whole(w_b), _whole(w_o)],
        out_specs=[row(D_MODEL)] * 4,
        out_shape=[jax.ShapeDtypeStruct((t, D_MODEL), F32)] + [jax.ShapeDtypeStruct((t, D_MODEL), BF16)] * 3,
        name=name, compiler_params=_params(("parallel",)))(oa, ob, proj, proj, proj, proj, x, w_a, w_b, w_o)


def _mix_tail_bwd(dx, proj, ya, yb, w_a, w_b, w_o, name):
    t = dx.shape[0]
    shard = D_MODEL // N_CHIPS

    def body(dx_ref, ga0, ga1, gb0, gb1, ya_ref, yb_ref, wa_ref, wb_ref, wo_ref, dya_ref, dyb_ref, dg_ref, doa_ref, dob_ref):
        dm = _dot_nt(_bf(dx_ref[...]), wo_ref[0])
        sa, sb = _gate(ga0, ga1), _gate(gb0, gb1)
        dya, dyb = _bf(dm * sa), _bf(dm * sb)
        dya_ref[...] = dya
        dyb_ref[...] = dyb
        dg_ref[:, :D_MODEL] = (dm * ya_ref[...].astype(F32) * sa * (1.0 - sa)).astype(dg_ref.dtype)
        dg_ref[:, D_MODEL:] = (dm * yb_ref[...].astype(F32) * sb * (1.0 - sb)).astype(dg_ref.dtype)
        doa_ref[...] = _dot_nt(dya, wa_ref[0])
        dob = _dot_nt(dyb[:, :shard], wb_ref[0])
        for s in range(1, N_CHIPS):
            dob = dob + _dot_nt(dyb[:, s * shard:(s + 1) * shard], wb_ref[s])
        dob_ref[...] = dob

    row = lambda w: pl.BlockSpec((MIX_ROWS, w), lambda i: (i, 0))
    return pl.pallas_call(
        body, grid=(t // MIX_ROWS,),
        in_specs=[row(D_MODEL)] + _gate_specs() + [row(D_MODEL), row(D_MODEL), _whole(w_a), _whole(w_b), _whole(w_o)],
        out_specs=[row(D_MODEL), row(D_MODEL), row(2 * D_MODEL), row(D_MODEL), row(ATT_GW)],
        out_shape=[jax.ShapeDtypeStruct((t, D_MODEL), BF16), jax.ShapeDtypeStruct((t, D_MODEL), BF16),
                   jax.ShapeDtypeStruct((t, 2 * D_MODEL), BF16), jax.ShapeDtypeStruct((t, D_MODEL), F32),
                   jax.ShapeDtypeStruct((t, ATT_GW), F32)],
        name=name, compiler_params=_params(("parallel",)))(dx, proj, proj, proj, proj, ya, yb, w_a, w_b, w_o)


def _ffn_fwd(x, g, src, l, pre):
    tag = f"l{l}_{pre}"
    w_in = src.weight(l, pre + "_w_in")
    h, ab, u = _ffn_in_act(x, g, w_in, name=tag + "_in_act", rider=src.ride(tag + "_in_act"))
    w_out = src.weight(l, pre + "_w_out")
    y = _mm_nn(u, w_out, name=tag + "_out", tm=512, tn=D_MODEL, out_dtype=F32, res=x, alpha=0.5, rider=src.ride(tag + "_out"))
    return y, (x, h, ab, u, w_in, w_out)


def _ffn_bwd(dx, saved, g, src, l, pre):
    tag = f"l{l}_{pre}"
    x, h, ab, u, w_in, w_out = saved
    g_out = _mm_tn(u, dx, nb=1, name=tag + "_bwd_wout", tm=1024, tk=1408, tn=D_MODEL, alpha=0.5, rider=src.ride(tag + "_bwd_wout"))
    src.grads(l, {pre + "_w_out": g_out.reshape(N_CHIPS, D_FF // N_CHIPS, D_MODEL)})
    dab = _ffn_bwd_du_act(dx, w_out, ab, name=tag + "_bwd_du_act", rider=src.ride(tag + "_bwd_du_act"))
    g_in = _mm_tn(h, dab, nb=N_CHIPS, name=tag + "_bwd_win", tm=2048, tk=D_MODEL, tn=FF_SHARD, rider=src.ride(tag + "_bwd_win"))
    src.grads(l, {pre + "_w_in": g_in})
    return _mm_nt(dab, w_in, name=tag + "_bwd_dh", tm=1024, tp=D_MODEL, tn=FF_SHARD, out_dtype=F32, rider=src.ride(tag + "_bwd_dh"),
                  norm=(x, g, dx))


def _mix_fwd(x, small, lb, cos, sin, src, l):
    tag = f"l{l}_mix"
    w = {}
    h = _norm_fwd(x, small["mix_norm"], name=tag + "_norm")
    w["w_in"] = src.weight(l, "w_in")
    proj = _mm_nn(h, w["w_in"], name=tag + "_in", tm=2048, tn=896, out_dtype=BF16, rider=src.ride(tag + "_in"))
    hf = _mm_nn(h, w["w_in"][0:1, :, D_MODEL:2 * D_MODEL], name=tag + "_hf", tm=1024, tn=D_MODEL, out_dtype=F32)
    oscan, oa, sall = _hgrn_fwd(proj, hf, lb, small["hgrn_out_norm"], name=tag + "_hgrn", rider=src.ride(tag + "_hgrn"))
    qk = _qk_fwd(proj, cos, sin, small["attn_q_norm"], small["attn_k_norm"], name=tag + "_qk")
    outs, lses = [], []
    for g in range(ATT_GROUPS):
        o, lse = _attn_fwd(qk[g], qk[3 + g], qk[6 + g], g, name=f"{tag}_attn{g}")
        outs.append(o)
        lses.append(lse)
    ob = _merge_fwd(outs, lses, name=tag + "_merge")
    w.update({n: src.weight(l, n) for n in ("w_branch_a", "w_branch_b", "w_out")})
    y, merged, ya, yb = _mix_tail_fwd(oa, ob, proj, x, w["w_branch_a"], w["w_branch_b"], w["w_out"], name=tag + "_tail")
    return y, (x, h, proj, hf, oscan, oa, sall, qk, outs, lses, ob, ya, yb, merged, w)


def _mix_bwd(dx, saved, small, lb, cos, sin, src, l, lb_live):
    tag = f"l{l}_mix"
    x, h, proj, hf, oscan, oa, sall, qk, outs, lses, ob, ya, yb, merged, w = saved
    g_wout = _mm_tn(merged, dx, nb=1, name=tag + "_bwd_wout", tm=1024, tk=D_MODEL, tn=D_MODEL)
    dya, dyb, dgab, doa, dob = _mix_tail_bwd(dx, proj, ya, yb, w["w_branch_a"], w["w_branch_b"], w["w_out"], name=tag + "_bwd_tail")
    g_wa = _mm_tn(oa, dya, nb=1, name=tag + "_bwd_wa", tm=1024, tk=D_MODEL, tn=D_MODEL)
    g_wb = _mm_tn(ob, dyb, nb=N_CHIPS, name=tag + "_bwd_wb", tm=2048, tk=ATT_GW, tn=256)
    mb = _merge_bwd(dob, outs, lses, name=tag + "_bwd_merge")
    dqk, dvs = [None] * 6, []
    for g in range(ATT_GROUPS):
        dq, dk, dv = _attn_bwd(qk[g], qk[3 + g], qk[6 + g], mb[g], lses[g], mb[3 + g], g, name=f"{tag}_bwd_attn{g}")
        dqk[g], dqk[3 + g] = dq, dk
        dvs.append(dv)
    dqk_cols, dqn, dkn = _qk_bwd(dqk, proj, cos, sin, small["attn_q_norm"], small["attn_k_norm"], name=tag + "_bwd_qk")
    dproj, dgn, dlb = _hgrn_bwd(doa, oscan, proj, hf, sall, lb, small["hgrn_out_norm"], dqk_cols, dvs, dgab,
                                name=tag + "_bwd_hgrn", precise=lb_live, rider=src.ride(tag + "_bwd_hgrn"))
    src.grads(l, dict(w_branch_a=g_wa.reshape(N_CHIPS, D_MODEL // N_CHIPS, D_MODEL), w_branch_b=g_wb,
                      w_out=g_wout.reshape(N_CHIPS, D_MODEL // N_CHIPS, D_MODEL)))
    g_win = _mm_tn(h, dproj, nb=N_CHIPS, name=tag + "_bwd_win", tm=1024, tk=D_MODEL, tn=2688, rider=src.ride(tag + "_bwd_win"))
    src.grads(l, dict(w_in=g_win))
    dx, dg = _mm_nt(dproj, w["w_in"], name=tag + "_bwd_dh", tm=1024, tp=D_MODEL, tn=2688, out_dtype=F32,
                    rider=src.ride(tag + "_bwd_dh"), norm=(x, small["mix_norm"], dx))
    return dx, dict(mix_norm=dg, hgrn_out_norm=dgn, lb=dlb, attn_q_norm=dqn, attn_k_norm=dkn)


BIG = ("ffn1_w_in", "ffn1_w_out", "w_in", "w_branch_a", "w_branch_b", "w_out", "ffn2_w_in", "ffn2_w_out")
ROW_SHARDED = ("ffn1_w_out", "w_branch_a", "w_out", "ffn2_w_out")
SMALL = ("ffn1_norm", "mix_norm", "hgrn_lb_logits", "hgrn_out_norm", "attn_q_norm", "attn_k_norm", "ffn2_norm")
WEIGHTS = ("ffn1_norm", "ffn1_w_in", "ffn1_w_out", "mix_norm", "w_in", "hgrn_lb_logits", "hgrn_out_norm", "attn_q_norm",
           "attn_k_norm", "w_branch_a", "w_branch_b", "w_out", "ffn2_norm", "ffn2_w_in", "ffn2_w_out")
SMALL_ROWS = 8


def _matmul_ready(name, a):
    return a.reshape(1, a.shape[0] * a.shape[1], a.shape[2]) if name in ROW_SHARDED else a


def _layer_small(small, l):
    s = {n: small[n][l].reshape(1, D_MODEL) for n in ("ffn1_norm", "mix_norm", "hgrn_out_norm", "ffn2_norm")}
    s.update({n: small[n][l] for n in ("attn_q_norm", "attn_k_norm")})
    return s


def _local_step(x, target, small, src):
    t = x.shape[0]
    cos, sin = _rope_tables(t)
    lbs = _lower_bounds(small["hgrn_lb_logits"])
    saved = []
    for l in range(2):
        sm = _layer_small(small, l)
        lb = lbs[l].reshape(1, D_MODEL)
        x, s1 = _ffn_fwd(x, sm["ffn1_norm"], src, l, "ffn1")
        x, s2 = _mix_fwd(x, sm, lb, cos, sin, src, l)
        x, s3 = _ffn_fwd(x, sm["ffn2_norm"], src, l, "ffn2")
        saved.append((sm, lb, s1, s2, s3))
    dx, sq = _loss_fwd_bwd(x, target, name="loss")
    small_rows = [None, None]
    for l in (1, 0):
        sm, lb, s1, s2, s3 = saved[l]
        dx, dg2 = _ffn_bwd(dx, s3, sm["ffn2_norm"], src, l, "ffn2")
        dx, g = _mix_bwd(dx, s2, sm, lb, cos, sin, src, l, lb_live=l > 0)
        dx, dg1 = _ffn_bwd(dx, s1, sm["ffn1_norm"], src, l, "ffn1")
        pad = lambda a: jnp.pad(a[:ATT_GROUPS].reshape(1, ATT_GROUPS * HEAD), ((0, 0), (0, D_MODEL - ATT_GROUPS * HEAD)))
        small_rows[l] = jnp.concatenate(
            [dg1, g["mix_norm"], g["lb"], g["hgrn_out_norm"], pad(g["attn_q_norm"]), pad(g["attn_k_norm"]), dg2,
             jnp.zeros((SMALL_ROWS - 7, D_MODEL), F32)], axis=0)
    return jnp.sum(sq), dx, jnp.concatenate(small_rows, axis=0)


def _coords():
    return lax.axis_index("x"), lax.axis_index("y"), lax.axis_index("c")


def _other_chips(x, y):
    return [(1 - x, y), (x, 1 - y), (1 - x, 1 - y)]


def _half_rows(rows, which):
    return pl.ds(which * (rows // 2), rows // 2)


def _gather_rider(shards):
    n = len(shards)

    def copies(w, full, sems):
        send, recv, fsend, frecv, osend, orecv = sems
        x, y, c = _coords()
        slot = 2 * x + y
        chips = _other_chips(x, y)

        def copy(i, j, blk, src, pair, to):
            return pltpu.make_async_remote_copy(src_ref=src, dst_ref=blk, send_sem=pair[0].at[i * 3 + j],
                                                recv_sem=pair[1].at[i * 3 + j], device_id=to, device_id_type=MESH)

        def block(i, chip_slot, core):
            return full[i].at[chip_slot, _half_rows(shards[i].shape[0], core)]

        pairs = [(i, j, chip) for i in range(n) for j, chip in enumerate(chips)]

        def first():
            return [copy(i, j, block(i, slot, c), w[i].at[_half_rows(shards[i].shape[0], c)], (send, recv), (*chip, c))
                    for i, j, chip in pairs]

        def landed(core, pair):
            return [copy(i, j, block(i, 2 * chip[0] + chip[1], core), block(i, 2 * chip[0] + chip[1], core), pair, (x, y, 1 - c))
                    for i, j, chip in pairs]

        def own():
            return [pltpu.make_async_remote_copy(src_ref=w[i], dst_ref=full[i].at[slot], send_sem=osend.at[i],
                                                 recv_sem=orecv.at[i], device_id=(x, y, 1 - c), device_id_type=MESH)
                    for i in range(n)]

        return first, landed, own

    def begin(w, full, sems):
        first, _, own = copies(w, full, sems)
        for cp in first() + own():
            cp.start()

    def end(w, full, sems):
        first, landed, own = copies(w, full, sems)
        forwards = landed(lax.axis_index("c"), sems[2:4])
        for arrival, forward in zip(landed(lax.axis_index("c"), sems[:2]), forwards):
            arrival.wait_recv()
            forward.start()
        for cp in landed(1 - lax.axis_index("c"), sems[2:4]) + own():
            cp.wait_recv()
        for cp in first() + forwards + own():
            cp.wait_send()

    out_shape = [jax.ShapeDtypeStruct((N_CHIPS,) + s.shape, s.dtype) for s in shards]
    sems = [pltpu.SemaphoreType.DMA((3 * n,))] * 4 + [pltpu.SemaphoreType.DMA((n,))] * 2
    return _Rider(shards, out_shape, sems, begin, end)


N_RECV = 7


def _scatter_rider(parts):
    n = len(parts)

    def copies(p, out, sems):
        send, recv = sems
        x, y, c = _coords()
        slot = 2 * x + y
        chips = _other_chips(x, y)

        def arrivals():
            return [pltpu.make_async_remote_copy(
                src_ref=out[i].at[k], dst_ref=out[i].at[k], send_sem=send.at[0], recv_sem=recv.at[i * N_RECV + k],
                device_id=(x, y, c), device_id_type=MESH) for i in range(n) for k in range(N_RECV)]

        sends = []
        for i in range(n):
            rows = parts[i].shape[1]
            for j, chip in enumerate(chips):
                for core in (0, 1):
                    sends.append(pltpu.make_async_remote_copy(
                        src_ref=p[i].at[2 * chip[0] + chip[1], _half_rows(rows, core)], dst_ref=out[i].at[2 * j + c],
                        send_sem=send.at[i * N_RECV + 2 * j + core], recv_sem=recv.at[i * N_RECV + 2 * j + c],
                        device_id=(*chip, core), device_id_type=MESH))
            sends.append(pltpu.make_async_remote_copy(
                src_ref=p[i].at[slot, _half_rows(rows, 1 - c)], dst_ref=out[i].at[6], send_sem=send.at[i * N_RECV + 6],
                recv_sem=recv.at[i * N_RECV + 6], device_id=(x, y, 1 - c), device_id_type=MESH))
        return sends, arrivals

    def begin(p, out, sems):
        for cp in copies(p, out, sems)[0]:
            cp.start()

    def end(p, out, sems):
        sends, arrivals = copies(p, out, sems)
        for cp in arrivals():
            cp.wait_recv()
        for cp in sends:
            cp.wait_send()

    out_shape = [jax.ShapeDtypeStruct((N_RECV, a.shape[1] // 2, a.shape[2]), a.dtype) for a in parts]
    return _Rider(parts, out_shape, [pltpu.SemaphoreType.DMA((N_RECV * n,))] * 2, begin, end)


def _run_alone(rider, name):
    _pcall(lambda: None, grid=(), in_specs=[], out_specs=[], out_shape=[], name=name, sem=(), args=(), rider=rider)
    return rider.result


def _sum_partials(own, parts, name):
    r, wd = own.shape
    tm = next(t for t in (256, 128, 64, 32, 16) if r % t == 0)

    def body(own_ref, p_ref, o_ref):
        acc = own_ref[...].astype(F32)
        for k in range(N_RECV):
            acc = acc + p_ref[k].astype(F32)
        o_ref[...] = acc

    return pl.pallas_call(
        body, grid=(r // tm,),
        in_specs=[pl.BlockSpec((tm, wd), lambda i: (i, 0)), pl.BlockSpec((N_RECV, tm, wd), lambda i: (0, i, 0))],
        out_specs=pl.BlockSpec((tm, wd), lambda i: (i, 0)), out_shape=jax.ShapeDtypeStruct((r, wd), F32),
        name=name, compiler_params=_params(("parallel",)))(own, parts)


def _exchange_halves(reduced, name):
    n = len(reduced)

    def body(*refs):
        r, out = refs[:n], refs[n:2 * n]
        send, recv = refs[2 * n:]
        x, y, c = _coords()
        sib = [pltpu.make_async_remote_copy(src_ref=r[i], dst_ref=out[i], send_sem=send.at[i], recv_sem=recv.at[i],
                                            device_id=(x, y, 1 - c), device_id_type=MESH) for i in range(n)]
        for cp in sib:
            cp.start()
        for cp in sib:
            cp.wait_recv()
        for cp in sib:
            cp.wait_send()

    out_shape = [jax.ShapeDtypeStruct(a.shape, a.dtype) for a in reduced]
    return pl.pallas_call(body, in_specs=[ANY] * n, out_specs=[ANY] * n, out_shape=out_shape,
                          scratch_shapes=[pltpu.SemaphoreType.DMA((n,))] * 2, name=name)(*reduced)


def _reduce_finish(parts, recv, tag):
    x, y, c = _coords()
    slot = 2 * x + y
    halves = []
    for i, (p, r) in enumerate(zip(parts, recv)):
        half = p.shape[1] // 2
        own = lax.dynamic_slice(p, (slot, c * half, 0), (1, half, p.shape[2]))[0]
        halves.append(_sum_partials(own, r, name=f"{tag}_sum{i}"))
    theirs = _exchange_halves(halves, name=tag + "_exchange")
    return [jnp.where(c == 0, jnp.concatenate([h, t], axis=0), jnp.concatenate([t, h], axis=0)) for h, t in zip(halves, theirs)]


GATHER_RIDES = {
    "l0_ffn1_in_act": ((0, "w_in"), (0, "w_branch_a"), (0, "w_branch_b"), (0, "w_out")),
    "l0_mix_in": ((0, "ffn2_w_in"), (0, "ffn2_w_out"), (1, "ffn1_w_in"), (1, "ffn1_w_out")),
    "l0_mix_hgrn": ((1, "w_in"), (1, "w_branch_a"), (1, "w_branch_b"), (1, "w_out")),
    "l0_ffn2_in_act": ((1, "ffn2_w_in"), (1, "ffn2_w_out")),
}
ALONE_FIRST = ((0, "ffn1_w_in"), (0, "ffn1_w_out"))
SCATTER_RIDES = {
    "l1_mix_bwd_hgrn": ((1, "ffn2_w_in"), (1, "ffn2_w_out")),
    "l0_ffn2_bwd_win": ((1, "ffn1_w_in"),),
    "l0_ffn2_bwd_dh": ((1, "ffn1_w_out"), (1, "w_branch_a"), (1, "w_branch_b"), (1, "w_out")),
    "l0_mix_bwd_hgrn": ((1, "w_in"), (0, "ffn2_w_out")),
    "l0_mix_bwd_win": ((0, "ffn2_w_in"),),
    "l0_mix_bwd_dh": ((0, "w_in"),),
    "l0_ffn1_bwd_wout": ((0, "w_branch_a"), (0, "w_branch_b"), (0, "w_out")),
    "l0_ffn1_bwd_du_act": ((0, "ffn1_w_out"),),
    "l0_ffn1_bwd_dh": ((0, "ffn1_w_in"),),
}


class _Exchange:
    def __init__(self, shards):
        self.shards = shards
        self.pending = []
        self.full = {}
        self.parts = {}
        self.recv = {}

    def _gather(self, keys):
        return _gather_rider([self.shards[n][l] for l, n in keys]), "gather", list(keys)

    def _scatter(self, keys):
        return _scatter_rider([self.parts[k] for k in keys]), "scatter", list(keys)

    def _unpack(self):
        waiting = []
        for rider, kind, keys in self.pending:
            if rider.result is None:
                waiting.append((rider, kind, keys))
            elif kind == "gather":
                self.full.update(zip(keys, rider.result))
            else:
                self.recv.update(zip(keys, rider.result))
        self.pending = waiting

    def ride(self, host):
        if host in GATHER_RIDES:
            self.pending.append(self._gather(GATHER_RIDES[host]))
        elif host in SCATTER_RIDES:
            self.pending.append(self._scatter(SCATTER_RIDES[host]))
        else:
            return None
        return self.pending[-1][0]

    def weight(self, l, name):
        self._unpack()
        if (l, name) not in self.full:
            assert (l, name) in ALONE_FIRST, (l, name)
            job = self._gather(ALONE_FIRST)
            _run_alone(job[0], name="gather_first")
            self.pending.append(job)
            self._unpack()
        return _matmul_ready(name, self.full[(l, name)])

    def grads(self, l, partials):
        self.parts.update({(l, n): a for n, a in partials.items()})

    def reduce(self):
        self._unpack()
        assert not self.pending and set(self.recv) == set(self.parts)
        out = {}
        for l in range(2):
            done = _reduce_finish([self.parts[(l, n)] for n in BIG], [self.recv[(l, n)] for n in BIG], f"reduce_l{l}")
            out[l] = dict(zip(BIG, done))
        return {n: jnp.stack([out[0][n], out[1][n]], axis=0) for n in BIG}


def _all_reduce_small(rows):
    r = rows.shape[0]

    def body(x_ref, o_ref, buf, send, recv):
        x, y, c = _coords()
        me = 4 * x + 2 * y + c
        buf[me] = x_ref[...]
        copies = []
        for k in range(1, 8):
            peer = (x ^ (k >> 2), y ^ ((k >> 1) & 1), c ^ (k & 1))
            cp = pltpu.make_async_remote_copy(src_ref=x_ref, dst_ref=buf.at[me], send_sem=send.at[k - 1], recv_sem=recv.at[me],
                                              device_id=peer, device_id_type=MESH)
            cp.start()
            copies.append(cp)
        for k in range(1, 8):
            src = 4 * (x ^ (k >> 2)) + 2 * (y ^ ((k >> 1) & 1)) + (c ^ (k & 1))
            pltpu.make_async_remote_copy(src_ref=x_ref, dst_ref=buf.at[src], send_sem=send.at[0], recv_sem=recv.at[src],
                                         device_id=(x, y, c), device_id_type=MESH).wait_recv()
        for cp in copies:
            cp.wait_send()
        acc = buf[0]
        for k in range(1, 8):
            acc = acc + buf[k]
        o_ref[...] = acc

    vm = pl.BlockSpec(memory_space=pltpu.VMEM)
    return pl.pallas_call(
        body, in_specs=[vm], out_specs=vm, out_shape=jax.ShapeDtypeStruct(rows.shape, F32),
        scratch_shapes=[pltpu.VMEM((8, r, D_MODEL), F32), pltpu.SemaphoreType.DMA((7,)), pltpu.SemaphoreType.DMA((8,))],
        name="all_reduce_small")(rows)


def _adamw_math(w, g, m, v):
    m = ADAM_B1 * m + (1.0 - ADAM_B1) * g
    v = ADAM_B2 * v + (1.0 - ADAM_B2) * (g * g)
    m_hat = m / (1.0 - ADAM_B1 ** ADAM_STEP)
    v_hat = v / (1.0 - ADAM_B2 ** ADAM_STEP)
    return -ADAM_LR * (m_hat / (jnp.sqrt(v_hat) + ADAM_EPS) + ADAM_WD * w), m, v


def _adamw(w, g, m, v, name):
    shape = w.shape
    cols = shape[-1]
    flat = lambda a: a.reshape(-1, cols)
    rows = flat(w).shape[0]
    tm = 128 if rows % 128 == 0 else rows
    ins = [('t', flat(a), cols, 0) for a in (w, g, m, v)]
    res = _ew(_adamw_math, ins, [('t', cols, F32)] * 3, rows=rows, tm=tm, name=name)
    return [a.reshape(shape) for a in res]


def _small_update(sums, logits, w, m, v):
    def body(s_ref, lg_ref, w_ref, m_ref, v_ref, g_ref, d_ref, nm_ref, nv_ref):
        s = s_ref[...]
        l0, l1 = lg_ref[0:1, :], lg_ref[1:2, :]
        mx = jnp.maximum(l0, l1)
        e0, e1 = jnp.exp(l0 - mx), jnp.exp(l1 - mx)
        sm0, sm1 = e0 / (e0 + e1), e1 / (e0 + e1)
        dl1 = s_ref[SMALL_ROWS + 2:SMALL_ROWS + 3, :] * sm0 * sm1
        row = lax.broadcasted_iota(jnp.int32, s.shape, 0)
        g = jnp.where(row == 2, -dl1, jnp.where(row == SMALL_ROWS + 2, dl1, s))
        d, nm, nv = _adamw_math(w_ref[...], g, m_ref[...], v_ref[...])
        g_ref[...] = g
        d_ref[...] = d
        nm_ref[...] = nm
        nv_ref[...] = nv

    vm = pl.BlockSpec(memory_space=pltpu.VMEM)
    return pl.pallas_call(body, in_specs=[vm] * 5, out_specs=[vm] * 4,
                          out_shape=[jax.ShapeDtypeStruct(sums.shape, F32)] * 4, name="small_update")(sums, logits, w, m, v)


def _pack_small(vals):
    rows = []
    for l in range(2):
        for n in ("ffn1_norm", "mix_norm", "hgrn_lb_logits", "hgrn_out_norm", "attn_q_norm", "attn_k_norm", "ffn2_norm"):
            a = vals[n][l].reshape(1, -1)
            rows.append(jnp.pad(a, ((0, 0), (0, D_MODEL - a.shape[1]))))
        rows.append(jnp.zeros((SMALL_ROWS - 7, D_MODEL), F32))
    return jnp.concatenate(rows, axis=0)


def _unpack_small(packed):
    out = {}
    for k, n in enumerate(("ffn1_norm", "mix_norm", "hgrn_lb_logits", "hgrn_out_norm", "attn_q_norm", "attn_k_norm", "ffn2_norm")):
        a = jnp.stack([packed[k], packed[SMALL_ROWS + k]], axis=0)
        out[n] = a[:, :ATT_GROUPS * HEAD].reshape(2, ATT_GROUPS, HEAD) if n.startswith("attn") else a
    return out


def kernel(x, ffn1_norm, ffn1_w_in, ffn1_w_out, mix_norm, w_in, hgrn_lb_logits, hgrn_out_norm, attn_q_norm, attn_k_norm, w_branch_a, w_branch_b, w_out, ffn2_norm, ffn2_w_in, ffn2_w_out, loss_target, m_ffn1_norm, m_ffn1_w_in, m_ffn1_w_out, m_mix_norm, m_w_in, m_hgrn_lb_logits, m_hgrn_out_norm, m_attn_q_norm, m_attn_k_norm, m_w_branch_a, m_w_branch_b, m_w_out, m_ffn2_norm, m_ffn2_w_in, m_ffn2_w_out, v_ffn1_norm, v_ffn1_w_in, v_ffn1_w_out, v_mix_norm, v_w_in, v_hgrn_lb_logits, v_hgrn_out_norm, v_attn_q_norm, v_attn_k_norm, v_w_branch_a, v_w_branch_b, v_w_out, v_ffn2_norm, v_ffn2_w_in, v_ffn2_w_out):
    a = locals()
    w = {n: a[n] for n in WEIGHTS}
    m = {n: a["m_" + n] for n in WEIGHTS}
    v = {n: a["v_" + n] for n in WEIGHTS}

    exchange = _Exchange({n: w[n].astype(BF16) for n in BIG})
    small = {n: w[n] for n in SMALL}
    sq, grad_x, small_rows = _local_step(x[0], loss_target[0], small, exchange)
    loss = lax.psum(sq, ("x", "y", "c")) * (0.5 / D_MODEL)
    grads = exchange.reduce()

    sums = _all_reduce_small(small_rows)
    g_s, d_s, m_s, v_s = _small_update(sums, w["hgrn_lb_logits"], _pack_small(small), _pack_small({n: m[n] for n in SMALL}),
                                       _pack_small({n: v[n] for n in SMALL}))
    grads.update(_unpack_small(g_s))
    delta, new_m, new_v = _unpack_small(d_s), _unpack_small(m_s), _unpack_small(v_s)
    for n in BIG:
        delta[n], new_m[n], new_v[n] = _adamw(w[n], grads[n], m[n], v[n], name="adamw_" + n)

    return (loss, grad_x[None], *[grads[n] for n in WEIGHTS], *[delta[n] for n in WEIGHTS],
            *[new_m[n] for n in WEIGHTS], *[new_v[n] for n in WEIGHTS])
```

```python
import functools

import jax
import jax.numpy as jnp
from jax import lax
from jax.experimental import pallas as pl
from jax.experimental.pallas import tpu as pltpu

F32 = jnp.float32
BF16 = jnp.bfloat16
MESH = pl.DeviceIdType.MESH

D_MODEL = 1024
D_FF = 2816
N_CHIPS = 4
HEAD = 128
HG_HEADS = 8
HG_CHUNK = 64
ATT_GROUPS = 3
ATT_HEADS = 4
ATT_GW = ATT_HEADS * HEAD
DILATIONS = (1, 4, 16)
ATT_BLK = 128
ATT_STEP_BLOCKS = 8
P_IN = 10752
CB_AQ, CB_AK, CB_AV, CB_GA, CB_GB = 8, 11, 14, 17, 19
EPS = 1e-6
ROPE_THETA = 10000.0
ADAM_LR, ADAM_B1, ADAM_B2, ADAM_EPS, ADAM_WD, ADAM_STEP = 0.001, 0.9, 0.999, 1e-08, 0.01, 10
VMEM_LIMIT_V7X = 56 * 1024 * 1024
NEG = -1e30


def _params(sem):
    return pltpu.CompilerParams(dimension_semantics=sem, vmem_limit_bytes=VMEM_LIMIT_V7X)


def _sig(x):
    return 1.0 / (1.0 + jnp.exp(-x))


def _dot(a, b):
    return jnp.dot(a, b, preferred_element_type=F32)


def _dot_nt(a, b):
    return lax.dot_general(a, b, (((1,), (1,)), ((), ())), preferred_element_type=F32)


def _dot_tn(a, b):
    return lax.dot_general(a, b, (((0,), (0,)), ((), ())), preferred_element_type=F32)


def _bf(x):
    return x.astype(BF16)


ANY = pl.BlockSpec(memory_space=pl.ANY)


class _Rider:
    def __init__(self, args, out_shape, sems, begin, end):
        self.args, self.out_shape, self.sems, self.begin, self.end = list(args), list(out_shape), list(sems), begin, end
        self.result = None


def _pcall(body, *, grid, in_specs, out_specs, out_shape, name, sem, args, scratch_shapes=(), rider=None):
    multi = isinstance(out_shape, (list, tuple))
    o_specs = list(out_specs) if multi else [out_specs]
    o_shape = list(out_shape) if multi else [out_shape]
    if rider is None:
        res = pl.pallas_call(body, grid=grid, in_specs=list(in_specs), out_specs=o_specs, out_shape=o_shape,
                             scratch_shapes=list(scratch_shapes), name=name, compiler_params=_params(sem))(*args)
        return list(res) if multi else res[0]
    counts = [len(in_specs), len(rider.args), len(o_specs), len(rider.out_shape), len(scratch_shapes)]

    def wrapped(*refs):
        groups, at = [], 0
        for c in counts:
            groups.append(refs[at:at + c])
            at += c
        h_in, r_in, h_out, r_out, h_scratch = groups
        r_sems = refs[at:]
        if grid:
            ids = [pl.program_id(a) for a in range(len(grid))]
            first = functools.reduce(jnp.logical_and, [i == 0 for i in ids])
            last = functools.reduce(jnp.logical_and, [i == g - 1 for i, g in zip(ids, grid)])
            pl.when(first)(lambda: rider.begin(r_in, r_out, r_sems))
            body(*h_in, *h_out, *h_scratch)
            pl.when(last)(lambda: rider.end(r_in, r_out, r_sems))
        else:
            rider.begin(r_in, r_out, r_sems)
            body(*h_in, *h_out, *h_scratch)
            rider.end(r_in, r_out, r_sems)

    res = pl.pallas_call(
        wrapped, grid=grid, in_specs=list(in_specs) + [ANY] * counts[1], out_specs=o_specs + [ANY] * counts[3],
        out_shape=o_shape + rider.out_shape, scratch_shapes=list(scratch_shapes) + rider.sems, name=name,
        compiler_params=_params(("arbitrary",) * len(grid)))(*args, *rider.args)
    rider.result = list(res[counts[2]:])
    return list(res[:counts[2]]) if multi else res[0]


def _mm_nn(a, b3, *, name, tm, tn, out_dtype, res=None, alpha=1.0, rider=None):
    m, k = a.shape
    nb, _, nw = b3.shape
    per = nw // tn
    assert nw % tn == 0 and m % tm == 0
    has_res = res is not None

    def body(*refs):
        if has_res:
            a_ref, b_ref, r_ref, o_ref = refs
        else:
            a_ref, b_ref, o_ref = refs
        acc = _dot(_bf(a_ref[...]), b_ref[...])
        if alpha != 1.0:
            acc = alpha * acc
        if has_res:
            acc = r_ref[...] + acc
        o_ref[...] = acc.astype(o_ref.dtype)

    in_specs = [pl.BlockSpec((tm, k), lambda i, j: (i, 0)),
                pl.BlockSpec((None, k, tn), lambda i, j: (j // per, 0, j % per))]
    args = [a, b3]
    if has_res:
        in_specs.append(pl.BlockSpec((tm, tn), lambda i, j: (i, j)))
        args.append(res)
    return _pcall(body, grid=(m // tm, nb * per), in_specs=in_specs, out_specs=pl.BlockSpec((tm, tn), lambda i, j: (i, j)),
                  out_shape=jax.ShapeDtypeStruct((m, nb * nw), out_dtype), name=name, sem=("parallel", "arbitrary"),
                  args=args, rider=rider)


def _mm_nt(d, b3, *, name, tm, tp, tn, out_dtype, alpha=1.0, rider=None, norm=None):
    m, n = d.shape
    nb, p, nw = b3.shape
    per = nw // tn
    nk = n // tn
    assert nb * nw == n and nw % tn == 0 and p % tp == 0 and m % tm == 0 and (norm is None or tp == p)

    def body(d_ref, b_ref, *refs):
        kk = pl.program_id(2)
        acc_ref = refs[-1]

        @pl.when(kk == 0)
        def _():
            acc_ref[...] = jnp.zeros_like(acc_ref)

        acc_ref[...] += _dot_nt(_bf(d_ref[...]), b_ref[...])

        if norm is None:
            @pl.when(kk == nk - 1)
            def _():
                refs[0][...] = (alpha * acc_ref[...]).astype(refs[0].dtype)
        else:
            x_ref, g_ref, dx_ref, o_ref, dg_ref = refs[:5]

            @pl.when(jnp.logical_and(pl.program_id(0) == 0, kk == 0))
            def _():
                dg_ref[...] = jnp.zeros_like(dg_ref)

            @pl.when(kk == nk - 1)
            def _():
                dh = alpha * acc_ref[...]
                xv = x_ref[...]
                r = _rms_rows(xv)
                xh = xv * r
                dxh = dh * g_ref[...]
                o_ref[...] = dx_ref[...] + r * (dxh - xh * jnp.mean(dxh * xh, axis=1, keepdims=True))
                dg_ref[...] += jnp.sum(dh * xh, axis=0, keepdims=True)

    in_specs = [pl.BlockSpec((tm, tn), lambda i, j, kk: (i, kk)),
                pl.BlockSpec((None, tp, tn), lambda i, j, kk: (kk // per, j, kk % per))]
    tile = pl.BlockSpec((tm, tp), lambda i, j, kk: (i, j))
    if norm is None:
        return _pcall(body, grid=(m // tm, p // tp, nk), in_specs=in_specs, out_specs=tile,
                      out_shape=jax.ShapeDtypeStruct((m, p), out_dtype), scratch_shapes=[pltpu.VMEM((tm, tp), F32)],
                      name=name, sem=("parallel", "parallel", "arbitrary"), args=(d, b3), rider=rider)
    x, g, dx = norm
    row = pl.BlockSpec((1, p), lambda i, j, kk: (0, 0))
    return _pcall(body, grid=(m // tm, 1, nk), in_specs=in_specs + [tile, row, tile], out_specs=[tile, row],
                  out_shape=[jax.ShapeDtypeStruct((m, p), F32), jax.ShapeDtypeStruct((1, p), F32)],
                  scratch_shapes=[pltpu.VMEM((tm, tp), F32)], name=name, sem=("arbitrary", "arbitrary", "arbitrary"),
                  args=(d, b3, x, g, dx), rider=rider)


def _mm_tn(a, d, *, nb, name, tm, tk, tn, alpha=1.0, rider=None):
    m, k = a.shape
    _, n = d.shape
    nw = n // nb
    per = nw // tn
    nm = m // tm
    assert nw % tn == 0 and k % tk == 0 and m % tm == 0

    def body(a_ref, d_ref, o_ref, acc_ref):
        mm = pl.program_id(2)

        @pl.when(mm == 0)
        def _():
            acc_ref[...] = jnp.zeros_like(acc_ref)

        acc_ref[...] += _dot_tn(_bf(a_ref[...]), _bf(d_ref[...]))

        @pl.when(mm == nm - 1)
        def _():
            o_ref[...] = (alpha * acc_ref[...]).astype(o_ref.dtype)

    return _pcall(
        body, grid=(k // tk, nb * per, nm),
        in_specs=[pl.BlockSpec((tm, tk), lambda i, j, mm: (mm, i)),
                  pl.BlockSpec((tm, tn), lambda i, j, mm: (mm, j))],
        out_specs=pl.BlockSpec((None, tk, tn), lambda i, j, mm: (j // per, i, j % per)),
        out_shape=jax.ShapeDtypeStruct((nb, k, nw), BF16),
        scratch_shapes=[pltpu.VMEM((tk, tn), F32)],
        name=name, sem=("parallel", "parallel", "arbitrary"), args=(a, d), rider=rider)


def _rows_from_view(ref, buf, w, d, tm):
    for k in range(d):
        for c in range(w // HEAD):
            lanes = slice(k * w + c * HEAD, k * w + (c + 1) * HEAD)
            buf.at[c][pl.ds(k, tm // d, stride=d), :] = ref[:, lanes].astype(F32)
    return _cat([buf[c] for c in range(w // HEAD)])


def _ew(fn, ins, outs, *, rows, tm, name):
    in_specs, args, scratch = [], [], []
    for s in ins:
        if s[0] == 't':
            _, arr, w, cb = s
            in_specs.append(pl.BlockSpec((tm, w), lambda i, cb=cb: (i, cb)))
        elif s[0] == 'v':
            _, arr, w, d = s
            in_specs.append(pl.BlockSpec((tm // d, d * w), lambda i: (i, 0)))
            scratch.append(pltpu.VMEM((w // HEAD, tm, HEAD), F32))
        else:
            arr = s[1]
            in_specs.append(pl.BlockSpec(arr.shape, lambda i, nd=arr.ndim: (0,) * nd))
        args.append(arr)
    out_specs, out_shape = [], []
    for s in outs:
        if s[0] == 't':
            _, w, dt = s
            out_specs.append(pl.BlockSpec((tm, w), lambda i: (i, 0)))
            out_shape.append(jax.ShapeDtypeStruct((rows, w), dt))
        elif s[0] == 'v':
            _, w, dt, d = s
            out_specs.append(pl.BlockSpec((tm // d, d * w), lambda i: (i, 0)))
            out_shape.append(jax.ShapeDtypeStruct((rows // d, d * w), dt))
            scratch.append(pltpu.VMEM((w // HEAD, tm, HEAD), F32))
        else:
            out_specs.append(pl.BlockSpec(s[1], lambda i: (0, 0)))
            out_shape.append(jax.ShapeDtypeStruct(s[1], F32))
    n_in, n_out = len(ins), len(outs)

    def body(*refs):
        bufs = list(refs[n_in + n_out:])
        vals = []
        for r, s in zip(refs[:n_in], ins):
            if s[0] == 'v':
                vals.append(_rows_from_view(r, bufs.pop(0), s[2], s[3], tm))
            else:
                vals.append(r[...])
        res = fn(*vals)
        if not isinstance(res, (tuple, list)):
            res = (res,)
        for r, s, v in zip(refs[n_in:n_in + n_out], outs, res):
            if s[0] == 't':
                r[...] = v.astype(r.dtype)
            elif s[0] == 'v':
                w, d, buf = s[1], s[3], bufs.pop(0)
                for c in range(w // HEAD):
                    buf[c] = v[:, c * HEAD:(c + 1) * HEAD].astype(F32)
                for k in range(d):
                    for c in range(w // HEAD):
                        lanes = slice(k * w + c * HEAD, k * w + (c + 1) * HEAD)
                        r[:, lanes] = buf.at[c][pl.ds(k, tm // d, stride=d), :].astype(r.dtype)
            else:
                @pl.when(pl.program_id(0) == 0)
                def _(r=r):
                    r[...] = jnp.zeros_like(r)

                r[...] += v

    res = pl.pallas_call(
        body, grid=(rows // tm,), in_specs=in_specs, out_specs=out_specs, out_shape=out_shape, scratch_shapes=scratch,
        name=name, compiler_params=_params(("arbitrary",)))(*args)
    return res


def _tile(arr, w, g):
    return ('t', arr, w, 0) if DILATIONS[g] == 1 else ('v', arr, w, DILATIONS[g])


def _tile_out(w, dtype, g):
    return ('t', w, dtype) if DILATIONS[g] == 1 else ('v', w, dtype, DILATIONS[g])


def _heads(x):
    return [x[:, h * HEAD:(h + 1) * HEAD] for h in range(x.shape[1] // HEAD)]


def _cat(xs):
    return jnp.concatenate(xs, axis=1)


def _head_mean(x):
    return _cat([jnp.broadcast_to(jnp.mean(h, axis=1, keepdims=True), h.shape) for h in _heads(x)])


def _rms_rows(x):
    return lax.rsqrt(jnp.mean(x * x, axis=1, keepdims=True) + EPS)


def _norm_fwd(x, g, name):
    return _ew(lambda xv, gv: xv * _rms_rows(xv) * gv,
               [('t', x, D_MODEL, 0), ('f', g)], [('t', D_MODEL, BF16)], rows=x.shape[0], tm=512, name=name)[0]


def _loss_fwd_bwd(y, target, name):
    def fn(yv, tv):
        e = yv - tv
        return e * (1.0 / D_MODEL), jnp.sum(e * e, axis=0, keepdims=True)

    return _ew(fn, [('t', y, D_MODEL, 0), ('t', target, D_MODEL, 0)], [('t', D_MODEL, F32), ('acc', (1, D_MODEL))],
               rows=y.shape[0], tm=512, name=name)


def _rot(x):
    sgn = jnp.where(lax.broadcasted_iota(jnp.int32, x.shape, 1) < HEAD // 2, -1.0, 1.0)
    return pltpu.roll(x, HEAD // 2, 1) * sgn


def _gain_rows(qn, kn):
    return [a[g:g + 1] for a in (qn, kn) for g in range(ATT_GROUPS)]


def _qk_fwd(proj, cos, sin, qn, kn, name):
    def fn(*v):
        xs, cosv, sinv, gains, vs = v[:6], v[6], v[7], v[8:14], v[14:17]
        outs = []
        for j, x in enumerate(xs):
            gain = gains[j]
            ys = []
            for xh in _heads(x.astype(F32)):
                xn = xh * _rms_rows(xh) * gain
                ys.append(xn * cosv + _rot(xn) * sinv)
            outs.append(_cat(ys))
        return outs + list(vs)

    ins = ([('t', proj, 512, CB_AQ + j) for j in range(6)] + [('t', cos, HEAD, 0), ('t', sin, HEAD, 0)]
           + [('f', a) for a in _gain_rows(qn, kn)] + [('t', proj, 512, CB_AV + g) for g in range(ATT_GROUPS)])
    return _ew(fn, ins, [_tile_out(ATT_GW, BF16, j % ATT_GROUPS) for j in range(9)], rows=proj.shape[0], tm=512, name=name)


def _qk_bwd(dqk, proj, cos, sin, qn, kn, name):
    def fn(*v):
        ds, xs, cosv, sinv, gains = v[:6], v[6:12], v[12], v[13], v[14:20]
        rows8 = lax.broadcasted_iota(jnp.int32, (8, HEAD), 0)
        outs, dgs = [], [jnp.zeros((8, HEAD), F32)] * 2
        for j in range(6):
            gain = gains[j]
            dx, dg = [], jnp.zeros((1, HEAD), F32)
            for dyh, xh in zip(_heads(ds[j]), _heads(xs[j].astype(F32))):
                r = _rms_rows(xh)
                xhat = xh * r
                dxn = dyh * cosv - _rot(dyh * sinv)
                dg = dg + jnp.sum(dxn * xhat, axis=0, keepdims=True)
                dxh = dxn * gain
                dx.append(r * (dxh - xhat * jnp.mean(dxh * xhat, axis=1, keepdims=True)))
            outs.append(_cat(dx))
            dgs[j // 3] = dgs[j // 3] + jnp.where(rows8 == j % 3, dg, 0.0)
        return _cat(outs), dgs[0], dgs[1]

    ins = ([_tile(a, ATT_GW, j % ATT_GROUPS) for j, a in enumerate(dqk)] + [('t', proj, 512, CB_AQ + j) for j in range(6)]
           + [('t', cos, HEAD, 0), ('t', sin, HEAD, 0)] + [('f', a) for a in _gain_rows(qn, kn)])
    return _ew(fn, ins, [('t', 6 * ATT_GW, BF16), ('acc', (8, HEAD)), ('acc', (8, HEAD))],
               rows=proj.shape[0], tm=512, name=name)


def _pick(x, h):
    lanes = lax.broadcasted_iota(jnp.int32, x.shape, 1)
    return jnp.sum(jnp.where(lanes == h, x, 0.0), axis=1, keepdims=True)


def _spread(x):
    return _cat([jnp.broadcast_to(_pick(x, h), (x.shape[0], HEAD)) for h in range(ATT_HEADS)])


def _compact(x):
    lanes = lax.broadcasted_iota(jnp.int32, (x.shape[0], HEAD), 1)
    out = jnp.zeros((x.shape[0], HEAD), F32)
    for h, xh in enumerate(_heads(x)):
        out = jnp.where(lanes == h, xh, out)
    return out


def _group_weights(l0, l1, l2):
    l0, l1, l2 = _spread(l0), _spread(l1), _spread(l2)
    m = jnp.maximum(jnp.maximum(l0, l1), l2)
    e0, e1, e2 = jnp.exp(l0 - m), jnp.exp(l1 - m), jnp.exp(l2 - m)
    inv = 1.0 / (e0 + e1 + e2)
    return e0 * inv, e1 * inv, e2 * inv


def _merge_fwd(outs, lses, name):
    def fn(o0, o1, o2, l0, l1, l2):
        a0, a1, a2 = _group_weights(l0, l1, l2)
        return a0 * o0 + a1 * o1 + a2 * o2

    ins = [_tile(a, ATT_GW, g) for g, a in enumerate(outs)] + [_tile(a, HEAD, g) for g, a in enumerate(lses)]
    return _ew(fn, ins, [('t', ATT_GW, BF16)], rows=outs[0].shape[0], tm=1024, name=name)[0]


def _merge_bwd(dob, outs, lses, name):
    def fn(dov, o0, o1, o2, l0, l1, l2):
        a0, a1, a2 = _group_weights(l0, l1, l2)
        ob = a0 * o0 + a1 * o1 + a2 * o2
        s = _head_mean(dov * ob) * float(HEAD)
        return a0 * dov, a1 * dov, a2 * dov, _compact(a0 * s), _compact(a1 * s), _compact(a2 * s)

    ins = ([('t', dob, ATT_GW, 0)] + [_tile(a, ATT_GW, g) for g, a in enumerate(outs)]
           + [_tile(a, HEAD, g) for g, a in enumerate(lses)])
    groups = range(ATT_GROUPS)
    return _ew(fn, ins, [_tile_out(ATT_GW, BF16, g) for g in groups] + [_tile_out(HEAD, F32, g) for g in groups],
               rows=dob.shape[0], tm=1024, name=name)


HG_ROWS = 256


def _hg_gates(hq, hf, hi, lbv):
    sig = _sig(hf)
    f = lbv + (1.0 - lbv) * sig
    return hq * _sig(hq), 1.0 - f, hi, jnp.log(f), sig, f


def _split3(x):
    hi = _bf(x)
    r1 = x - hi.astype(F32)
    mid = _bf(r1)
    return hi, mid, _bf(r1 - mid.astype(F32))


def _tri_dot(tri, x):
    hi, mid, lo = _split3(x)
    return _dot(tri, hi) + _dot(tri, mid) + _dot(tri, lo)


def _row(x, i):
    rows = lax.broadcasted_iota(jnp.int32, x.shape, 0)
    return jnp.sum(jnp.where(rows == i, x, 0.0), axis=0, keepdims=True)


def _hg_decay(logf, q, k):
    c = HG_CHUNK
    row = lax.broadcasted_iota(jnp.int32, (c, c), 0)
    col = lax.broadcasted_iota(jnp.int32, (c, c), 1)
    g = _tri_dot((row >= col).astype(BF16), logf)
    gm = _row(g, c // 2 - 1)
    gl = _row(g, c - 1)
    decays = jnp.exp(g), jnp.exp(g - gm), jnp.exp(gm - g), jnp.exp(gl - g)
    return gl, decays, q * decays[0], q * decays[1], k * decays[2], k * decays[3]


def _hg_out_fwd(o, hg, gain):
    r = lax.rsqrt(_head_mean(o * o) + EPS)
    return o * r * gain * (hg * _sig(hg))


def _hgrn_fwd(proj, hf, lb, gain, name, rider=None):
    t = proj.shape[0]
    nck = HG_ROWS // HG_CHUNK

    def body(hq_ref, hf_ref, hi_ref, hg_ref, lb_ref, gn_ref, o_ref, oa_ref, sall_ref, st_ref):
        @pl.when(pl.program_id(0) == 0)
        def _():
            st_ref[...] = jnp.zeros_like(st_ref)

        lbv = lb_ref[...]
        gnv = gn_ref[...]
        c = HG_CHUNK
        mask = lax.broadcasted_iota(jnp.int32, (c, c), 0) >= lax.broadcasted_iota(jnp.int32, (c, c), 1)

        def chunk(cc, carry):
            sl = pl.ds(pl.multiple_of(cc * c, c), c)
            q, k, v, logf, _, _ = _hg_gates(hq_ref[sl, :].astype(F32), hf_ref[sl, :], hi_ref[sl, :].astype(F32), lbv)
            gl, _, qg, qt, kt, kd = _hg_decay(logf, q, k)
            egl = jnp.exp(gl)
            os = []
            for h in range(HG_HEADS):
                hs = slice(h * HEAD, (h + 1) * HEAD)
                st = st_ref[h]
                sall_ref[cc, h] = st
                a = jnp.where(mask, _dot_nt(_bf(qt[:, hs]), _bf(kt[:, hs])), 0.0)
                os.append(_dot(_bf(a), _bf(v[:, hs])) + _dot_nt(_bf(qg[:, hs]), _bf(st)))
                st_ref[h] = egl[:, hs] * st + _dot_tn(_bf(v[:, hs]), _bf(kd[:, hs]))
            o = _cat(os)
            o_ref[sl, :] = o
            oa_ref[sl, :] = _hg_out_fwd(o, hg_ref[sl, :].astype(F32), gnv).astype(oa_ref.dtype)
            return carry

        lax.fori_loop(0, nck, chunk, 0)

    col = lambda j: pl.BlockSpec((HG_ROWS, D_MODEL), lambda i, j=j: (i, j))
    small = pl.BlockSpec((1, D_MODEL), lambda i: (0, 0))
    return _pcall(
        body, grid=(t // HG_ROWS,),
        in_specs=[col(0), col(0), col(2), col(3), small, small],
        out_specs=[col(0), col(0), pl.BlockSpec((nck, HG_HEADS, HEAD, HEAD), lambda i: (i, 0, 0, 0))],
        out_shape=[jax.ShapeDtypeStruct((t, D_MODEL), F32), jax.ShapeDtypeStruct((t, D_MODEL), BF16),
                   jax.ShapeDtypeStruct((t // HG_CHUNK, HG_HEADS, HEAD, HEAD), F32)],
        scratch_shapes=[pltpu.VMEM((HG_HEADS, HEAD, HEAD), F32)],
        name=name, sem=("arbitrary",), args=(proj, hf, proj, proj, lb, gain), rider=rider)


def _terms(x, precise):
    hi = _bf(x)
    return (hi, _bf(x - hi.astype(F32))) if precise else (hi,)


def _mm(dot, a, b):
    out = dot(a[0], b[0])
    if len(a) > 1:
        out = out + dot(a[1], b[0])
    if len(b) > 1:
        out = out + dot(a[0], b[1])
    return out


def _hgrn_bwd(doa, oscan, proj, hf, sall, lb, gain, dqk, dvs, dgab, name, precise, rider=None):
    t = proj.shape[0]
    nck = HG_ROWS // HG_CHUNK
    nsteps = t // HG_ROWS
    terms = functools.partial(_terms, precise=precise)
    n_view = sum(d > 1 for d in DILATIONS)

    def body(doa_ref, os_ref, hq_ref, hf_ref, hi_ref, hg_ref, sall_ref, lb_ref, gn_ref, dqk_ref, dv0_ref, dv1_ref,
             dv2_ref, dgab_ref, dproj_ref, dgn_ref, dlb_ref, dst_ref, *bufs):
        @pl.when(pl.program_id(0) == 0)
        def _():
            dst_ref[...] = jnp.zeros_like(dst_ref)
            dgn_ref[...] = jnp.zeros_like(dgn_ref)
            dlb_ref[...] = jnp.zeros_like(dlb_ref)

        at = 4 * D_MODEL
        dproj_ref[:, at:at + 6 * ATT_GW] = dqk_ref[...]
        at += 6 * ATT_GW
        spare = list(bufs)
        for d, dv_ref in zip(DILATIONS, (dv0_ref, dv1_ref, dv2_ref)):
            dv = dv_ref[...] if d == 1 else _rows_from_view(dv_ref, spare.pop(0), ATT_GW, d, HG_ROWS)
            dproj_ref[:, at:at + ATT_GW] = dv.astype(dproj_ref.dtype)
            at += ATT_GW
        dproj_ref[:, at:] = dgab_ref[...]

        lbv = lb_ref[...]
        gnv = gn_ref[...]
        c = HG_CHUNK
        row = lax.broadcasted_iota(jnp.int32, (c, c), 0)
        colm = lax.broadcasted_iota(jnp.int32, (c, c), 1)
        mask = row >= colm
        triu = (row <= colm).astype(BF16)
        last = lax.broadcasted_iota(jnp.int32, (c, HEAD), 0) == c - 1

        def chunk(ci, carry):
            cc = nck - 1 - ci
            sl = pl.ds(pl.multiple_of(cc * c, c), c)
            hq, hg = hq_ref[sl, :].astype(F32), hg_ref[sl, :].astype(F32)
            q, k, v, logf, sig, f = _hg_gates(hq, hf_ref[sl, :], hi_ref[sl, :].astype(F32), lbv)
            gl, (e_qg, e_qt, e_kt, e_kd), qg, qt, kt, kd = _hg_decay(logf, q, k)
            egl = jnp.exp(gl)
            o = os_ref[sl, :]
            dy = doa_ref[sl, :]
            r = lax.rsqrt(_head_mean(o * o) + EPS)
            oh = o * r
            sg = _sig(hg)
            silu_g = hg * sg
            dgn_ref[...] += jnp.sum(dy * oh * silu_g, axis=0, keepdims=True)
            dhg = dy * oh * gnv * (sg * (1.0 + hg * (1.0 - sg)))
            doh = dy * gnv * silu_g
            do = r * (doh - oh * _head_mean(doh * oh))
            dqs, dks, dvs, dgs = [], [], [], []
            for h in range(HG_HEADS):
                hs = slice(h * HEAD, (h + 1) * HEAD)
                st = sall_ref[cc, h]
                dst = dst_ref[h]
                qt_h, kt_h, qg_h, kd_h = qt[:, hs], kt[:, hs], qg[:, hs], kd[:, hs]
                do_p, v_p, qt_p, kt_p, qg_p = terms(do[:, hs]), terms(v[:, hs]), terms(qt_h), terms(kt_h), terms(qg_h)
                st_p, dst_p = terms(st), terms(dst)
                a = jnp.where(mask, _dot_nt(qt_p[0], kt_p[0]), 0.0)
                da = terms(jnp.where(mask, _mm(_dot_nt, do_p, v_p), 0.0))
                dqt = _mm(_dot, da, kt_p)
                dkt = _mm(_dot_tn, da, qt_p)
                dqg = _mm(_dot, do_p, st_p)
                dv = _dot_tn(_bf(a), do_p[0]) + _dot_nt(_bf(kd_h), dst_p[0])
                dkd = _mm(_dot, v_p, dst_p)
                dgl = egl[:, hs] * jnp.sum(st * dst, axis=0, keepdims=True) + jnp.sum(dkd * kd_h, axis=0, keepdims=True)
                dst_ref[h] = egl[:, hs] * dst + _mm(_dot_tn, do_p, qg_p)
                dqs.append(dqt * e_qt[:, hs] + dqg * e_qg[:, hs])
                dks.append(dkt * e_kt[:, hs] + dkd * e_kd[:, hs])
                dvs.append(dv)
                dgs.append(dqt * qt_h - dkt * kt_h + dqg * qg_h - dkd * kd_h + jnp.where(last, dgl, 0.0))
            dq, dk, dv, dg = _cat(dqs), _cat(dks), _cat(dvs), _cat(dgs)
            dlogf = _tri_dot(triu, dg)
            df = dlogf / f - dk
            dlb_ref[...] += jnp.sum(df * (1.0 - sig), axis=0, keepdims=True)
            dhf = df * (1.0 - lbv) * sig * (1.0 - sig)
            sq = _sig(hq)
            dhq = dq * (sq * (1.0 + hq * (1.0 - sq)))
            dproj_ref[sl, :4 * D_MODEL] = _cat([dhq, dhf, dv, dhg]).astype(dproj_ref.dtype)
            return carry

        lax.fori_loop(0, nck, chunk, 0)

    rev = lambda j: pl.BlockSpec((HG_ROWS, D_MODEL), lambda i, j=j: (nsteps - 1 - i, j))
    rows = lambda a, d=1: pl.BlockSpec((HG_ROWS // d, a.shape[1]), lambda i: (nsteps - 1 - i, 0))
    small = pl.BlockSpec((1, D_MODEL), lambda i: (0, 0))
    return _pcall(
        body, grid=(nsteps,),
        in_specs=[rev(0), rev(0), rev(0), rev(0), rev(2), rev(3),
                  pl.BlockSpec((nck, HG_HEADS, HEAD, HEAD), lambda i: (nsteps - 1 - i, 0, 0, 0)), small, small,
                  rows(dqk)] + [rows(a, d) for a, d in zip(dvs, DILATIONS)] + [rows(dgab)],
        out_specs=[pl.BlockSpec((HG_ROWS, P_IN), lambda i: (nsteps - 1 - i, 0)), small, small],
        out_shape=[jax.ShapeDtypeStruct((t, P_IN), BF16), jax.ShapeDtypeStruct((1, D_MODEL), F32),
                   jax.ShapeDtypeStruct((1, D_MODEL), F32)],
        scratch_shapes=[pltpu.VMEM((HG_HEADS, HEAD, HEAD), F32)] + [pltpu.VMEM((ATT_HEADS, HG_ROWS, HEAD), F32)] * n_view,
        name=name, sem=("arbitrary",), args=(doa, oscan, proj, hf, proj, proj, sall, lb, gain, dqk, *dvs, dgab),
        rider=rider)


def _window_masks(has_previous):
    qi = lax.broadcasted_iota(jnp.int32, (ATT_BLK, 2 * ATT_BLK), 0)
    ki = lax.broadcasted_iota(jnp.int32, (ATT_BLK, 2 * ATT_BLK), 1)
    band = jnp.logical_and(ki >= qi, ki <= qi + ATT_BLK)
    return band, jnp.logical_and(band, jnp.logical_or(ki >= ATT_BLK, has_previous))


def _two_blocks(ref, prev_ref, j, hs):
    if j == 0:
        return jnp.concatenate([prev_ref[:, hs], ref[0:ATT_BLK, hs]], axis=0)
    return ref[(j - 1) * ATT_BLK:(j + 1) * ATT_BLK, hs]


def _attn_cfg(qg, g):
    d = DILATIONS[g]
    length = qg.shape[0]
    assert qg.shape[1] == d * ATT_GW
    nb = length // ATT_BLK
    return d, length, nb, min(ATT_STEP_BLOCKS, nb)


def _attn_fwd(qg, kg, vg, g, name):
    d, length, nb, rb = _attn_cfg(qg, g)
    scale = HEAD ** -0.5

    def body(q_ref, k_ref, v_ref, kp_ref, vp_ref, o_ref, l_ref):
        n = pl.program_id(1)
        band, first_band = _window_masks(n > 0)
        lanes = lax.broadcasted_iota(jnp.int32, (ATT_BLK, HEAD), 1)
        for j in range(rb):
            rows = slice(j * ATT_BLK, (j + 1) * ATT_BLK)
            lse = jnp.zeros((ATT_BLK, HEAD), F32)
            for h in range(ATT_HEADS):
                hs = slice(h * HEAD, (h + 1) * HEAD)
                k2, v2 = _two_blocks(k_ref, kp_ref, j, hs), _two_blocks(v_ref, vp_ref, j, hs)
                s = jnp.where(first_band if j == 0 else band, _dot_nt(q_ref[rows, hs], k2) * scale, NEG)
                m = jnp.max(s, axis=1, keepdims=True)
                p = jnp.exp(s - m)
                l = jnp.sum(p, axis=1, keepdims=True)
                o_ref[rows, hs] = (_dot(_bf(p), v2) / l).astype(o_ref.dtype)
                lse = jnp.where(lanes == h, m + jnp.log(l), lse)
            l_ref[rows, :] = lse

    own = pl.BlockSpec((rb * ATT_BLK, ATT_GW), lambda r, n: (n, r))
    own_head = pl.BlockSpec((rb * ATT_BLK, HEAD), lambda r, n: (n, r))
    prev = pl.BlockSpec((ATT_BLK, ATT_GW), lambda r, n: (jnp.maximum(n * rb - 1, 0), r))
    return pl.pallas_call(
        body, grid=(d, nb // rb), in_specs=[own, own, own, prev, prev], out_specs=[own, own_head],
        out_shape=[jax.ShapeDtypeStruct((length, d * ATT_GW), BF16), jax.ShapeDtypeStruct((length, d * HEAD), F32)],
        name=name, compiler_params=_params(("parallel", "arbitrary")))(qg, kg, vg, kg, vg)


def _attn_bwd(qg, kg, vg, dog, lse, delta, g, name):
    d, length, nb, rb = _attn_cfg(qg, g)
    nsteps = nb // rb
    scale = HEAD ** -0.5

    def body(q_ref, k_ref, v_ref, do_ref, l_ref, dl_ref, kp_ref, vp_ref, qn_ref, don_ref, ln_ref, dln_ref,
             dq_ref, dk_ref, dv_ref):
        n = pl.program_id(1)
        band, first_band = _window_masks(n > 0)
        qi = lax.broadcasted_iota(jnp.int32, (ATT_BLK, ATT_BLK), 0)
        ki = lax.broadcasted_iota(jnp.int32, (ATT_BLK, ATT_BLK), 1)
        next_m = jnp.logical_and(ki >= qi, n < nsteps - 1)
        last = slice((rb - 1) * ATT_BLK, rb * ATT_BLK)
        for h in range(ATT_HEADS):
            hs = slice(h * HEAD, (h + 1) * HEAD)
            dk, dv = [None] * rb, [None] * rb
            for j in range(rb):
                rows = slice(j * ATT_BLK, (j + 1) * ATT_BLK)
                q, do = q_ref[rows, hs], do_ref[rows, hs]
                k2, v2 = _two_blocks(k_ref, kp_ref, j, hs), _two_blocks(v_ref, vp_ref, j, hs)
                p = jnp.where(first_band if j == 0 else band,
                              jnp.exp(_dot_nt(q, k2) * scale - _pick(l_ref[rows, :], h)), 0.0)
                ds = _bf(p * (_dot_nt(do, v2) - _pick(dl_ref[rows, :], h)) * scale)
                dq_ref[rows, hs] = _dot(ds, k2).astype(dq_ref.dtype)
                dk2, dv2 = _dot_tn(ds, q), _dot_tn(_bf(p), do)
                if j >= 1:
                    dk[j - 1] = dk[j - 1] + dk2[:ATT_BLK]
                    dv[j - 1] = dv[j - 1] + dv2[:ATT_BLK]
                dk[j], dv[j] = dk2[ATT_BLK:], dv2[ATT_BLK:]
            q, do = qn_ref[:, hs], don_ref[:, hs]
            p = jnp.where(next_m, jnp.exp(_dot_nt(q, k_ref[last, hs]) * scale - _pick(ln_ref[...], h)), 0.0)
            ds = _bf(p * (_dot_nt(do, v_ref[last, hs]) - _pick(dln_ref[...], h)) * scale)
            dk[rb - 1] = dk[rb - 1] + _dot_tn(ds, q)
            dv[rb - 1] = dv[rb - 1] + _dot_tn(_bf(p), do)
            for j in range(rb):
                rows = slice(j * ATT_BLK, (j + 1) * ATT_BLK)
                dk_ref[rows, hs] = dk[j].astype(dk_ref.dtype)
                dv_ref[rows, hs] = dv[j].astype(dv_ref.dtype)

    own = pl.BlockSpec((rb * ATT_BLK, ATT_GW), lambda r, n: (n, r))
    prev = pl.BlockSpec((ATT_BLK, ATT_GW), lambda r, n: (jnp.maximum(n * rb - 1, 0), r))
    nxt = pl.BlockSpec((ATT_BLK, ATT_GW), lambda r, n: (jnp.minimum((n + 1) * rb, nb - 1), r))
    own_head = pl.BlockSpec((rb * ATT_BLK, HEAD), lambda r, n: (n, r))
    nxt_head = pl.BlockSpec((ATT_BLK, HEAD), lambda r, n: (jnp.minimum((n + 1) * rb, nb - 1), r))
    return pl.pallas_call(
        body, grid=(d, nsteps), in_specs=[own] * 4 + [own_head] * 2 + [prev, prev, nxt, nxt, nxt_head, nxt_head],
        out_specs=[own, own, own], out_shape=[jax.ShapeDtypeStruct((length, d * ATT_GW), BF16)] * 3,
        name=name, compiler_params=_params(("parallel", "arbitrary")))(
            qg, kg, vg, dog, lse, delta, kg, vg, qg, dog, lse, delta)


def _rope_tables(t):
    pos = jnp.arange(t, dtype=F32)
    inv = ROPE_THETA ** (-jnp.arange(0, HEAD, 2, dtype=F32) / HEAD)
    ang = pos[:, None] * inv[None, :]
    ang = jnp.concatenate([ang, ang], axis=-1)
    return jnp.cos(ang), jnp.sin(ang)


def _lower_bounds(logits):
    lb = jnp.cumsum(jax.nn.softmax(logits.astype(F32), axis=0), axis=0)
    return lb - lb[0:1]


FFN_ROWS = 256
FF_SHARD = 2 * D_FF // N_CHIPS


def _ffn_in_act(x, g, w_in, name, rider=None):
    t = x.shape[0]

    def body(x_ref, g_ref, w_ref, h_ref, ab_ref, u_ref):
        xv = x_ref[...]
        h = _bf(xv * _rms_rows(xv) * g_ref[...])
        h_ref[...] = h
        for s in range(N_CHIPS // 2):
            cols = slice(s * FF_SHARD, (s + 1) * FF_SHARD)
            a = _dot(h, w_ref[s])
            b = _dot(h, w_ref[s + N_CHIPS // 2])
            ab_ref[:, cols] = a.astype(ab_ref.dtype)
            ab_ref[:, D_FF + s * FF_SHARD:D_FF + (s + 1) * FF_SHARD] = b.astype(ab_ref.dtype)
            u_ref[:, cols] = (a * _sig(a) * b).astype(u_ref.dtype)

    row = lambda w: pl.BlockSpec((FFN_ROWS, w), lambda i: (i, 0))
    return _pcall(
        body, grid=(t // FFN_ROWS,),
        in_specs=[row(D_MODEL), pl.BlockSpec((1, D_MODEL), lambda i: (0, 0)),
                  pl.BlockSpec(w_in.shape, lambda i: (0, 0, 0))],
        out_specs=[row(D_MODEL), row(2 * D_FF), row(D_FF)],
        out_shape=[jax.ShapeDtypeStruct((t, D_MODEL), BF16), jax.ShapeDtypeStruct((t, 2 * D_FF), BF16),
                   jax.ShapeDtypeStruct((t, D_FF), BF16)],
        name=name, sem=("parallel",), args=(x, g, w_in), rider=rider)


def _ffn_bwd_du_act(dx, w_out, ab, name, rider=None):
    t = dx.shape[0]

    def body(dx_ref, w_ref, ab_ref, o_ref):
        du = 0.5 * _dot_nt(_bf(dx_ref[...]), w_ref[0])
        a = ab_ref[:, :D_FF].astype(F32)
        b = ab_ref[:, D_FF:].astype(F32)
        s = _sig(a)
        o_ref[:, :D_FF] = (du * b * (s * (1.0 + a * (1.0 - s)))).astype(o_ref.dtype)
        o_ref[:, D_FF:] = (du * a * s).astype(o_ref.dtype)

    row = lambda w: pl.BlockSpec((FFN_ROWS, w), lambda i: (i, 0))
    return _pcall(
        body, grid=(t // FFN_ROWS,),
        in_specs=[row(D_MODEL), pl.BlockSpec(w_out.shape, lambda i: (0, 0, 0)), row(2 * D_FF)],
        out_specs=row(2 * D_FF), out_shape=jax.ShapeDtypeStruct((t, 2 * D_FF), BF16),
        name=name, sem=("parallel",), args=(dx, w_out, ab), rider=rider)


MIX_ROWS = 512


def _gate_specs():
    return [pl.BlockSpec((MIX_ROWS, 512), lambda i, cb=cb: (i, cb)) for cb in (CB_GA, CB_GA + 1, CB_GB, CB_GB + 1)]


def _gate(lo_ref, hi_ref):
    return _sig(_cat([lo_ref[...], hi_ref[...]]).astype(F32))


def _whole(a):
    return pl.BlockSpec(a.shape, lambda i: (0,) * a.ndim)


def _mix_tail_fwd(oa, ob, proj, x, w_a, w_b, w_o, name):
    t = x.shape[0]

    def body(oa_ref, ob_ref, ga0, ga1, gb0, gb1, x_ref, wa_ref, wb_ref, wo_ref, y_ref, m_ref, ya_ref, yb_ref):
        ya = _dot(oa_ref[...], wa_ref[0])
        yb = _cat([_dot(ob_ref[...], wb_ref[s]) for s in range(N_CHIPS)])
        merged = _bf(_gate(ga0, ga1) * ya + _gate(gb0, gb1) * yb)
        m_ref[...] = merged
        ya_ref[...] = ya.astype(ya_ref.dtype)
        yb_ref[...] = yb.astype(yb_ref.dtype)
        y_ref[...] = x_ref[...] + _dot(merged, wo_ref[0])

    row = lambda w: pl.BlockSpec((MIX_ROWS, w), lambda i: (i, 0))
    return pl.pallas_call(
        body, grid=(t // MIX_ROWS,),
        in_specs=[row(D_MODEL), row(ATT_GW)] + _gate_specs() + [row(D_MODEL), _whole(w_a), _whole(w_b), _whole(w_o)],
        out_specs=[row(D_MODEL)] * 4,
        out_shape=[jax.ShapeDtypeStruct((t, D_MODEL), F32)] + [jax.ShapeDtypeStruct((t, D_MODEL), BF16)] * 3,
        name=name, compiler_params=_params(("parallel",)))(oa, ob, proj, proj, proj, proj, x, w_a, w_b, w_o)


def _mix_tail_bwd(dx, proj, ya, yb, w_a, w_b, w_o, name):
    t = dx.shape[0]
    shard = D_MODEL // N_CHIPS

    def body(dx_ref, ga0, ga1, gb0, gb1, ya_ref, yb_ref, wa_ref, wb_ref, wo_ref, dya_ref, dyb_ref, dg_ref, doa_ref, dob_ref):
        dm = _dot_nt(_bf(dx_ref[...]), wo_ref[0])
        sa, sb = _gate(ga0, ga1), _gate(gb0, gb1)
        dya, dyb = _bf(dm * sa), _bf(dm * sb)
        dya_ref[...] = dya
        dyb_ref[...] = dyb
        dg_ref[:, :D_MODEL] = (dm * ya_ref[...].astype(F32) * sa * (1.0 - sa)).astype(dg_ref.dtype)
        dg_ref[:, D_MODEL:] = (dm * yb_ref[...].astype(F32) * sb * (1.0 - sb)).astype(dg_ref.dtype)
        doa_ref[...] = _dot_nt(dya, wa_ref[0])
        dob = _dot_nt(dyb[:, :shard], wb_ref[0])
        for s in range(1, N_CHIPS):
            dob = dob + _dot_nt(dyb[:, s * shard:(s + 1) * shard], wb_ref[s])
        dob_ref[...] = dob

    row = lambda w: pl.BlockSpec((MIX_ROWS, w), lambda i: (i, 0))
    return pl.pallas_call(
        body, grid=(t // MIX_ROWS,),
        in_specs=[row(D_MODEL)] + _gate_specs() + [row(D_MODEL), row(D_MODEL), _whole(w_a), _whole(w_b), _whole(w_o)],
        out_specs=[row(D_MODEL), row(D_MODEL), row(2 * D_MODEL), row(D_MODEL), row(ATT_GW)],
        out_shape=[jax.ShapeDtypeStruct((t, D_MODEL), BF16), jax.ShapeDtypeStruct((t, D_MODEL), BF16),
                   jax.ShapeDtypeStruct((t, 2 * D_MODEL), BF16), jax.ShapeDtypeStruct((t, D_MODEL), F32),
                   jax.ShapeDtypeStruct((t, ATT_GW), F32)],
        name=name, compiler_params=_params(("parallel",)))(dx, proj, proj, proj, proj, ya, yb, w_a, w_b, w_o)


def _ffn_fwd(x, g, src, l, pre):
    tag = f"l{l}_{pre}"
    w_in = src.weight(l, pre + "_w_in")
    h, ab, u = _ffn_in_act(x, g, w_in, name=tag + "_in_act", rider=src.ride(tag + "_in_act"))
    w_out = src.weight(l, pre + "_w_out")
    y = _mm_nn(u, w_out, name=tag + "_out", tm=512, tn=D_MODEL, out_dtype=F32, res=x, alpha=0.5, rider=src.ride(tag + "_out"))
    return y, (x, h, ab, u, w_in, w_out)


def _ffn_bwd(dx, saved, g, src, l, pre):
    tag = f"l{l}_{pre}"
    x, h, ab, u, w_in, w_out = saved
    g_out = _mm_tn(u, dx, nb=1, name=tag + "_bwd_wout", tm=1024, tk=1408, tn=D_MODEL, alpha=0.5, rider=src.ride(tag + "_bwd_wout"))
    src.grads(l, {pre + "_w_out": g_out.reshape(N_CHIPS, D_FF // N_CHIPS, D_MODEL)})
    dab = _ffn_bwd_du_act(dx, w_out, ab, name=tag + "_bwd_du_act", rider=src.ride(tag + "_bwd_du_act"))
    g_in = _mm_tn(h, dab, nb=N_CHIPS, name=tag + "_bwd_win", tm=2048, tk=D_MODEL, tn=FF_SHARD, rider=src.ride(tag + "_bwd_win"))
    src.grads(l, {pre + "_w_in": g_in})
    return _mm_nt(dab, w_in, name=tag + "_bwd_dh", tm=1024, tp=D_MODEL, tn=FF_SHARD, out_dtype=F32, rider=src.ride(tag + "_bwd_dh"),
                  norm=(x, g, dx))


def _mix_fwd(x, small, lb, cos, sin, src, l):
    tag = f"l{l}_mix"
    w = {}
    h = _norm_fwd(x, small["mix_norm"], name=tag + "_norm")
    w["w_in"] = src.weight(l, "w_in")
    proj = _mm_nn(h, w["w_in"], name=tag + "_in", tm=2048, tn=896, out_dtype=BF16, rider=src.ride(tag + "_in"))
    hf = _mm_nn(h, w["w_in"][0:1, :, D_MODEL:2 * D_MODEL], name=tag + "_hf", tm=1024, tn=D_MODEL, out_dtype=F32)
    oscan, oa, sall = _hgrn_fwd(proj, hf, lb, small["hgrn_out_norm"], name=tag + "_hgrn", rider=src.ride(tag + "_hgrn"))
    qk = _qk_fwd(proj, cos, sin, small["attn_q_norm"], small["attn_k_norm"], name=tag + "_qk")
    outs, lses = [], []
    for g in range(ATT_GROUPS):
        o, lse = _attn_fwd(qk[g], qk[3 + g], qk[6 + g], g, name=f"{tag}_attn{g}")
        outs.append(o)
        lses.append(lse)
    ob = _merge_fwd(outs, lses, name=tag + "_merge")
    w.update({n: src.weight(l, n) for n in ("w_branch_a", "w_branch_b", "w_out")})
    y, merged, ya, yb = _mix_tail_fwd(oa, ob, proj, x, w["w_branch_a"], w["w_branch_b"], w["w_out"], name=tag + "_tail")
    return y, (x, h, proj, hf, oscan, oa, sall, qk, outs, lses, ob, ya, yb, merged, w)


def _mix_bwd(dx, saved, small, lb, cos, sin, src, l, lb_live):
    tag = f"l{l}_mix"
    x, h, proj, hf, oscan, oa, sall, qk, outs, lses, ob, ya, yb, merged, w = saved
    g_wout = _mm_tn(merged, dx, nb=1, name=tag + "_bwd_wout", tm=1024, tk=D_MODEL, tn=D_MODEL)
    dya, dyb, dgab, doa, dob = _mix_tail_bwd(dx, proj, ya, yb, w["w_branch_a"], w["w_branch_b"], w["w_out"], name=tag + "_bwd_tail")
    g_wa = _mm_tn(oa, dya, nb=1, name=tag + "_bwd_wa", tm=1024, tk=D_MODEL, tn=D_MODEL)
    g_wb = _mm_tn(ob, dyb, nb=N_CHIPS, name=tag + "_bwd_wb", tm=2048, tk=ATT_GW, tn=256)
    mb = _merge_bwd(dob, outs, lses, name=tag + "_bwd_merge")
    dqk, dvs = [None] * 6, []
    for g in range(ATT_GROUPS):
        dq, dk, dv = _attn_bwd(qk[g], qk[3 + g], qk[6 + g], mb[g], lses[g], mb[3 + g], g, name=f"{tag}_bwd_attn{g}")
        dqk[g], dqk[3 + g] = dq, dk
        dvs.append(dv)
    dqk_cols, dqn, dkn = _qk_bwd(dqk, proj, cos, sin, small["attn_q_norm"], small["attn_k_norm"], name=tag + "_bwd_qk")
    dproj, dgn, dlb = _hgrn_bwd(doa, oscan, proj, hf, sall, lb, small["hgrn_out_norm"], dqk_cols, dvs, dgab,
                                name=tag + "_bwd_hgrn", precise=lb_live, rider=src.ride(tag + "_bwd_hgrn"))
    src.grads(l, dict(w_branch_a=g_wa.reshape(N_CHIPS, D_MODEL // N_CHIPS, D_MODEL), w_branch_b=g_wb,
                      w_out=g_wout.reshape(N_CHIPS, D_MODEL // N_CHIPS, D_MODEL)))
    g_win = _mm_tn(h, dproj, nb=N_CHIPS, name=tag + "_bwd_win", tm=1024, tk=D_MODEL, tn=2688, rider=src.ride(tag + "_bwd_win"))
    src.grads(l, dict(w_in=g_win))
    dx, dg = _mm_nt(dproj, w["w_in"], name=tag + "_bwd_dh", tm=1024, tp=D_MODEL, tn=2688, out_dtype=F32,
                    rider=src.ride(tag + "_bwd_dh"), norm=(x, small["mix_norm"], dx))
    return dx, dict(mix_norm=dg, hgrn_out_norm=dgn, lb=dlb, attn_q_norm=dqn, attn_k_norm=dkn)


BIG = ("ffn1_w_in", "ffn1_w_out", "w_in", "w_branch_a", "w_branch_b", "w_out", "ffn2_w_in", "ffn2_w_out")
ROW_SHARDED = ("ffn1_w_out", "w_branch_a", "w_out", "ffn2_w_out")
SMALL = ("ffn1_norm", "mix_norm", "hgrn_lb_logits", "hgrn_out_norm", "attn_q_norm", "attn_k_norm", "ffn2_norm")
WEIGHTS = ("ffn1_norm", "ffn1_w_in", "ffn1_w_out", "mix_norm", "w_in", "hgrn_lb_logits", "hgrn_out_norm", "attn_q_norm",
           "attn_k_norm", "w_branch_a", "w_branch_b", "w_out", "ffn2_norm", "ffn2_w_in", "ffn2_w_out")
SMALL_ROWS = 8


def _matmul_ready(name, a):
    return a.reshape(1, a.shape[0] * a.shape[1], a.shape[2]) if name in ROW_SHARDED else a


def _layer_small(small, l):
    s = {n: small[n][l].reshape(1, D_MODEL) for n in ("ffn1_norm", "mix_norm", "hgrn_out_norm", "ffn2_norm")}
    s.update({n: small[n][l] for n in ("attn_q_norm", "attn_k_norm")})
    return s


def _local_step(x, target, small, src):
    t = x.shape[0]
    cos, sin = _rope_tables(t)
    lbs = _lower_bounds(small["hgrn_lb_logits"])
    saved = []
    for l in range(2):
        sm = _layer_small(small, l)
        lb = lbs[l].reshape(1, D_MODEL)
        x, s1 = _ffn_fwd(x, sm["ffn1_norm"], src, l, "ffn1")
        x, s2 = _mix_fwd(x, sm, lb, cos, sin, src, l)
        x, s3 = _ffn_fwd(x, sm["ffn2_norm"], src, l, "ffn2")
        saved.append((sm, lb, s1, s2, s3))
    dx, sq = _loss_fwd_bwd(x, target, name="loss")
    small_rows = [None, None]
    for l in (1, 0):
        sm, lb, s1, s2, s3 = saved[l]
        dx, dg2 = _ffn_bwd(dx, s3, sm["ffn2_norm"], src, l, "ffn2")
        dx, g = _mix_bwd(dx, s2, sm, lb, cos, sin, src, l, lb_live=l > 0)
        dx, dg1 = _ffn_bwd(dx, s1, sm["ffn1_norm"], src, l, "ffn1")
        pad = lambda a: jnp.pad(a[:ATT_GROUPS].reshape(1, ATT_GROUPS * HEAD), ((0, 0), (0, D_MODEL - ATT_GROUPS * HEAD)))
        small_rows[l] = jnp.concatenate(
            [dg1, g["mix_norm"], g["lb"], g["hgrn_out_norm"], pad(g["attn_q_norm"]), pad(g["attn_k_norm"]), dg2,
             jnp.zeros((SMALL_ROWS - 7, D_MODEL), F32)], axis=0)
    return jnp.sum(sq), dx, jnp.concatenate(small_rows, axis=0)


def _coords():
    return lax.axis_index("x"), lax.axis_index("y"), lax.axis_index("c")


def _other_chips(x, y):
    return [(1 - x, y), (x, 1 - y), (1 - x, 1 - y)]


def _half_rows(rows, which):
    return pl.ds(which * (rows // 2), rows // 2)


def _gather_rider(shards):
    n = len(shards)

    def copies(w, full, sems):
        send, recv, fsend, frecv, osend, orecv = sems
        x, y, c = _coords()
        slot = 2 * x + y
        chips = _other_chips(x, y)

        def copy(i, j, blk, src, pair, to):
            return pltpu.make_async_remote_copy(src_ref=src, dst_ref=blk, send_sem=pair[0].at[i * 3 + j],
                                                recv_sem=pair[1].at[i * 3 + j], device_id=to, device_id_type=MESH)

        def block(i, chip_slot, core):
            return full[i].at[chip_slot, _half_rows(shards[i].shape[0], core)]

        pairs = [(i, j, chip) for i in range(n) for j, chip in enumerate(chips)]

        def first():
            return [copy(i, j, block(i, slot, c), w[i].at[_half_rows(shards[i].shape[0], c)], (send, recv), (*chip, c))
                    for i, j, chip in pairs]

        def landed(core, pair):
            return [copy(i, j, block(i, 2 * chip[0] + chip[1], core), block(i, 2 * chip[0] + chip[1], core), pair, (x, y, 1 - c))
                    for i, j, chip in pairs]

        def own():
            return [pltpu.make_async_remote_copy(src_ref=w[i], dst_ref=full[i].at[slot], send_sem=osend.at[i],
                                                 recv_sem=orecv.at[i], device_id=(x, y, 1 - c), device_id_type=MESH)
                    for i in range(n)]

        return first, landed, own

    def begin(w, full, sems):
        first, _, own = copies(w, full, sems)
        for cp in first() + own():
            cp.start()

    def end(w, full, sems):
        first, landed, own = copies(w, full, sems)
        forwards = landed(lax.axis_index("c"), sems[2:4])
        for arrival, forward in zip(landed(lax.axis_index("c"), sems[:2]), forwards):
            arrival.wait_recv()
            forward.start()
        for cp in landed(1 - lax.axis_index("c"), sems[2:4]) + own():
            cp.wait_recv()
        for cp in first() + forwards + own():
            cp.wait_send()

    out_shape = [jax.ShapeDtypeStruct((N_CHIPS,) + s.shape, s.dtype) for s in shards]
    sems = [pltpu.SemaphoreType.DMA((3 * n,))] * 4 + [pltpu.SemaphoreType.DMA((n,))] * 2
    return _Rider(shards, out_shape, sems, begin, end)


N_RECV = 7


def _scatter_rider(parts):
    n = len(parts)

    def copies(p, out, sems):
        send, recv = sems
        x, y, c = _coords()
        slot = 2 * x + y
        chips = _other_chips(x, y)

        def arrivals():
            return [pltpu.make_async_remote_copy(
                src_ref=out[i].at[k], dst_ref=out[i].at[k], send_sem=send.at[0], recv_sem=recv.at[i * N_RECV + k],
                device_id=(x, y, c), device_id_type=MESH) for i in range(n) for k in range(N_RECV)]

        sends = []
        for i in range(n):
            rows = parts[i].shape[1]
            for j, chip in enumerate(chips):
                for core in (0, 1):
                    sends.append(pltpu.make_async_remote_copy(
                        src_ref=p[i].at[2 * chip[0] + chip[1], _half_rows(rows, core)], dst_ref=out[i].at[2 * j + c],
                        send_sem=send.at[i * N_RECV + 2 * j + core], recv_sem=recv.at[i * N_RECV + 2 * j + c],
                        device_id=(*chip, core), device_id_type=MESH))
            sends.append(pltpu.make_async_remote_copy(
                src_ref=p[i].at[slot, _half_rows(rows, 1 - c)], dst_ref=out[i].at[6], send_sem=send.at[i * N_RECV + 6],
                recv_sem=recv.at[i * N_RECV + 6], device_id=(x, y, 1 - c), device_id_type=MESH))
        return sends, arrivals

    def begin(p, out, sems):
        for cp in copies(p, out, sems)[0]:
            cp.start()

    def end(p, out, sems):
        sends, arrivals = copies(p, out, sems)
        for cp in arrivals():
            cp.wait_recv()
        for cp in sends:
            cp.wait_send()

    out_shape = [jax.ShapeDtypeStruct((N_RECV, a.shape[1] // 2, a.shape[2]), a.dtype) for a in parts]
    return _Rider(parts, out_shape, [pltpu.SemaphoreType.DMA((N_RECV * n,))] * 2, begin, end)


def _run_alone(rider, name):
    _pcall(lambda: None, grid=(), in_specs=[], out_specs=[], out_shape=[], name=name, sem=(), args=(), rider=rider)
    return rider.result


def _sum_partials(own, parts, name):
    r, wd = own.shape
    tm = next(t for t in (256, 128, 64, 32, 16) if r % t == 0)

    def body(own_ref, p_ref, o_ref):
        acc = own_ref[...].astype(F32)
        for k in range(N_RECV):
            acc = acc + p_ref[k].astype(F32)
        o_ref[...] = acc

    return pl.pallas_call(
        body, grid=(r // tm,),
        in_specs=[pl.BlockSpec((tm, wd), lambda i: (i, 0)), pl.BlockSpec((N_RECV, tm, wd), lambda i: (0, i, 0))],
        out_specs=pl.BlockSpec((tm, wd), lambda i: (i, 0)), out_shape=jax.ShapeDtypeStruct((r, wd), F32),
        name=name, compiler_params=_params(("parallel",)))(own, parts)


def _exchange_halves(reduced, name):
    n = len(reduced)

    def body(*refs):
        r, out = refs[:n], refs[n:2 * n]
        send, recv = refs[2 * n:]
        x, y, c = _coords()
        sib = [pltpu.make_async_remote_copy(src_ref=r[i], dst_ref=out[i], send_sem=send.at[i], recv_sem=recv.at[i],
                                            device_id=(x, y, 1 - c), device_id_type=MESH) for i in range(n)]
        for cp in sib:
            cp.start()
        for cp in sib:
            cp.wait_recv()
        for cp in sib:
            cp.wait_send()

    out_shape = [jax.ShapeDtypeStruct(a.shape, a.dtype) for a in reduced]
    return pl.pallas_call(body, in_specs=[ANY] * n, out_specs=[ANY] * n, out_shape=out_shape,
                          scratch_shapes=[pltpu.SemaphoreType.DMA((n,))] * 2, name=name)(*reduced)


def _reduce_finish(parts, recv, tag):
    x, y, c = _coords()
    slot = 2 * x + y
    halves = []
    for i, (p, r) in enumerate(zip(parts, recv)):
        half = p.shape[1] // 2
        own = lax.dynamic_slice(p, (slot, c * half, 0), (1, half, p.shape[2]))[0]
        halves.append(_sum_partials(own, r, name=f"{tag}_sum{i}"))
    theirs = _exchange_halves(halves, name=tag + "_exchange")
    return [jnp.where(c == 0, jnp.concatenate([h, t], axis=0), jnp.concatenate([t, h], axis=0)) for h, t in zip(halves, theirs)]


GATHER_RIDES = {
    "l0_ffn1_in_act": ((0, "w_in"), (0, "w_branch_a"), (0, "w_branch_b"), (0, "w_out")),
    "l0_mix_in": ((0, "ffn2_w_in"), (0, "ffn2_w_out"), (1, "ffn1_w_in"), (1, "ffn1_w_out")),
    "l0_mix_hgrn": ((1, "w_in"), (1, "w_branch_a"), (1, "w_branch_b"), (1, "w_out")),
    "l0_ffn2_in_act": ((1, "ffn2_w_in"), (1, "ffn2_w_out")),
}
ALONE_FIRST = ((0, "ffn1_w_in"), (0, "ffn1_w_out"))
SCATTER_RIDES = {
    "l1_mix_bwd_hgrn": ((1, "ffn2_w_in"), (1, "ffn2_w_out")),
    "l0_ffn2_bwd_win": ((1, "ffn1_w_in"),),
    "l0_ffn2_bwd_dh": ((1, "ffn1_w_out"), (1, "w_branch_a"), (1, "w_branch_b"), (1, "w_out")),
    "l0_mix_bwd_hgrn": ((1, "w_in"), (0, "ffn2_w_out")),
    "l0_mix_bwd_win": ((0, "ffn2_w_in"),),
    "l0_mix_bwd_dh": ((0, "w_in"),),
    "l0_ffn1_bwd_wout": ((0, "w_branch_a"), (0, "w_branch_b"), (0, "w_out")),
    "l0_ffn1_bwd_du_act": ((0, "ffn1_w_out"),),
    "l0_ffn1_bwd_dh": ((0, "ffn1_w_in"),),
}


class _Exchange:
    def __init__(self, shards):
        self.shards = shards
        self.pending = []
        self.full = {}
        self.parts = {}
        self.recv = {}

    def _gather(self, keys):
        return _gather_rider([self.shards[n][l] for l, n in keys]), "gather", list(keys)

    def _scatter(self, keys):
        return _scatter_rider([self.parts[k] for k in keys]), "scatter", list(keys)

    def _unpack(self):
        waiting = []
        for rider, kind, keys in self.pending:
            if rider.result is None:
                waiting.append((rider, kind, keys))
            elif kind == "gather":
                self.full.update(zip(keys, rider.result))
            else:
                self.recv.update(zip(keys, rider.result))
        self.pending = waiting

    def ride(self, host):
        if host in GATHER_RIDES:
            self.pending.append(self._gather(GATHER_RIDES[host]))
        elif host in SCATTER_RIDES:
            self.pending.append(self._scatter(SCATTER_RIDES[host]))
        else:
            return None
        return self.pending[-1][0]

    def weight(self, l, name):
        self._unpack()
        if (l, name) not in self.full:
            assert (l, name) in ALONE_FIRST, (l, name)
            job = self._gather(ALONE_FIRST)
            _run_alone(job[0], name="gather_first")
            self.pending.append(job)
            self._unpack()
        return _matmul_ready(name, self.full[(l, name)])

    def grads(self, l, partials):
        self.parts.update({(l, n): a for n, a in partials.items()})

    def reduce(self):
        self._unpack()
        assert not self.pending and set(self.recv) == set(self.parts)
        out = {}
        for l in range(2):
            done = _reduce_finish([self.parts[(l, n)] for n in BIG], [self.recv[(l, n)] for n in BIG], f"reduce_l{l}")
            out[l] = dict(zip(BIG, done))
        return {n: jnp.stack([out[0][n], out[1][n]], axis=0) for n in BIG}


def _all_reduce_small(rows):
    r = rows.shape[0]

    def body(x_ref, o_ref, buf, send, recv):
        x, y, c = _coords()
        me = 4 * x + 2 * y + c
        buf[me] = x_ref[...]
        copies = []
        for k in range(1, 8):
            peer = (x ^ (k >> 2), y ^ ((k >> 1) & 1), c ^ (k & 1))
            cp = pltpu.make_async_remote_copy(src_ref=x_ref, dst_ref=buf.at[me], send_sem=send.at[k - 1], recv_sem=recv.at[me],
                                              device_id=peer, device_id_type=MESH)
            cp.start()
            copies.append(cp)
        for k in range(1, 8):
            src = 4 * (x ^ (k >> 2)) + 2 * (y ^ ((k >> 1) & 1)) + (c ^ (k & 1))
            pltpu.make_async_remote_copy(src_ref=x_ref, dst_ref=buf.at[src], send_sem=send.at[0], recv_sem=recv.at[src],
                                         device_id=(x, y, c), device_id_type=MESH).wait_recv()
        for cp in copies:
            cp.wait_send()
        acc = buf[0]
        for k in range(1, 8):
            acc = acc + buf[k]
        o_ref[...] = acc

    vm = pl.BlockSpec(memory_space=pltpu.VMEM)
    return pl.pallas_call(
        body, in_specs=[vm], out_specs=vm, out_shape=jax.ShapeDtypeStruct(rows.shape, F32),
        scratch_shapes=[pltpu.VMEM((8, r, D_MODEL), F32), pltpu.SemaphoreType.DMA((7,)), pltpu.SemaphoreType.DMA((8,))],
        name="all_reduce_small")(rows)


def _adamw_math(w, g, m, v):
    m = ADAM_B1 * m + (1.0 - ADAM_B1) * g
    v = ADAM_B2 * v + (1.0 - ADAM_B2) * (g * g)
    m_hat = m / (1.0 - ADAM_B1 ** ADAM_STEP)
    v_hat = v / (1.0 - ADAM_B2 ** ADAM_STEP)
    return -ADAM_LR * (m_hat / (jnp.sqrt(v_hat) + ADAM_EPS) + ADAM_WD * w), m, v


def _adamw(w, g, m, v, name):
    shape = w.shape
    cols = shape[-1]
    flat = lambda a: a.reshape(-1, cols)
    rows = flat(w).shape[0]
    tm = 128 if rows % 128 == 0 else rows
    ins = [('t', flat(a), cols, 0) for a in (w, g, m, v)]
    res = _ew(_adamw_math, ins, [('t', cols, F32)] * 3, rows=rows, tm=tm, name=name)
    return [a.reshape(shape) for a in res]


def _small_update(sums, logits, w, m, v):
    def body(s_ref, lg_ref, w_ref, m_ref, v_ref, g_ref, d_ref, nm_ref, nv_ref):
        s = s_ref[...]
        l0, l1 = lg_ref[0:1, :], lg_ref[1:2, :]
        mx = jnp.maximum(l0, l1)
        e0, e1 = jnp.exp(l0 - mx), jnp.exp(l1 - mx)
        sm0, sm1 = e0 / (e0 + e1), e1 / (e0 + e1)
        dl1 = s_ref[SMALL_ROWS + 2:SMALL_ROWS + 3, :] * sm0 * sm1
        row = lax.broadcasted_iota(jnp.int32, s.shape, 0)
        g = jnp.where(row == 2, -dl1, jnp.where(row == SMALL_ROWS + 2, dl1, s))
        d, nm, nv = _adamw_math(w_ref[...], g, m_ref[...], v_ref[...])
        g_ref[...] = g
        d_ref[...] = d
        nm_ref[...] = nm
        nv_ref[...] = nv

    vm = pl.BlockSpec(memory_space=pltpu.VMEM)
    return pl.pallas_call(body, in_specs=[vm] * 5, out_specs=[vm] * 4,
                          out_shape=[jax.ShapeDtypeStruct(sums.shape, F32)] * 4, name="small_update")(sums, logits, w, m, v)


def _pack_small(vals):
    rows = []
    for l in range(2):
        for n in ("ffn1_norm", "mix_norm", "hgrn_lb_logits", "hgrn_out_norm", "attn_q_norm", "attn_k_norm", "ffn2_norm"):
            a = vals[n][l].reshape(1, -1)
            rows.append(jnp.pad(a, ((0, 0), (0, D_MODEL - a.shape[1]))))
        rows.append(jnp.zeros((SMALL_ROWS - 7, D_MODEL), F32))
    return jnp.concatenate(rows, axis=0)


def _unpack_small(packed):
    out = {}
    for k, n in enumerate(("ffn1_norm", "mix_norm", "hgrn_lb_logits", "hgrn_out_norm", "attn_q_norm", "attn_k_norm", "ffn2_norm")):
        a = jnp.stack([packed[k], packed[SMALL_ROWS + k]], axis=0)
        out[n] = a[:, :ATT_GROUPS * HEAD].reshape(2, ATT_GROUPS, HEAD) if n.startswith("attn") else a
    return out


def kernel(x, ffn1_norm, ffn1_w_in, ffn1_w_out, mix_norm, w_in, hgrn_lb_logits, hgrn_out_norm, attn_q_norm, attn_k_norm, w_branch_a, w_branch_b, w_out, ffn2_norm, ffn2_w_in, ffn2_w_out, loss_target, m_ffn1_norm, m_ffn1_w_in, m_ffn1_w_out, m_mix_norm, m_w_in, m_hgrn_lb_logits, m_hgrn_out_norm, m_attn_q_norm, m_attn_k_norm, m_w_branch_a, m_w_branch_b, m_w_out, m_ffn2_norm, m_ffn2_w_in, m_ffn2_w_out, v_ffn1_norm, v_ffn1_w_in, v_ffn1_w_out, v_mix_norm, v_w_in, v_hgrn_lb_logits, v_hgrn_out_norm, v_attn_q_norm, v_attn_k_norm, v_w_branch_a, v_w_branch_b, v_w_out, v_ffn2_norm, v_ffn2_w_in, v_ffn2_w_out):
    a = locals()
    w = {n: a[n] for n in WEIGHTS}
    m = {n: a["m_" + n] for n in WEIGHTS}
    v = {n: a["v_" + n] for n in WEIGHTS}

    exchange = _Exchange({n: w[n].astype(BF16) for n in BIG})
    small = {n: w[n] for n in SMALL}
    sq, grad_x, small_rows = _local_step(x[0], loss_target[0], small, exchange)
    loss = lax.psum(sq, ("x", "y", "c")) * (0.5 / D_MODEL)
    grads = exchange.reduce()

    sums = _all_reduce_small(small_rows)
    g_s, d_s, m_s, v_s = _small_update(sums, w["hgrn_lb_logits"], _pack_small(small), _pack_small({n: m[n] for n in SMALL}),
                                       _pack_small({n: v[n] for n in SMALL}))
    grads.update(_unpack_small(g_s))
    delta, new_m, new_v = _unpack_small(d_s), _unpack_small(m_s), _unpack_small(v_s)
    for n in BIG:
        delta[n], new_m[n], new_v[n] = _adamw(w[n], grads[n], m[n], v[n], name="adamw_" + n)

    return (loss, grad_x[None], *[grads[n] for n in WEIGHTS], *[delta[n] for n in WEIGHTS],
            *[new_m[n] for n in WEIGHTS], *[new_v[n] for n in WEIGHTS])
```

```python
import functools

import jax
import jax.numpy as jnp
from jax import lax
from jax.experimental import pallas as pl
from jax.experimental.pallas import tpu as pltpu

F32 = jnp.float32
BF16 = jnp.bfloat16
MESH = pl.DeviceIdType.MESH

D_MODEL = 1024
D_FF = 2816
N_CHIPS = 4
HEAD = 128
HG_HEADS = 8
HG_CHUNK = 64
ATT_GROUPS = 3
ATT_HEADS = 4
ATT_GW = ATT_HEADS * HEAD
DILATIONS = (1, 4, 16)
ATT_BLK = 128
ATT_STEP_BLOCKS = 8
P_IN = 10752
CB_AQ, CB_AK, CB_AV, CB_GA, CB_GB = 8, 11, 14, 17, 19
EPS = 1e-6
ROPE_THETA = 10000.0
ADAM_LR, ADAM_B1, ADAM_B2, ADAM_EPS, ADAM_WD, ADAM_STEP = 0.001, 0.9, 0.999, 1e-08, 0.01, 10
VMEM_LIMIT_V7X = 56 * 1024 * 1024
NEG = -1e30


def _params(sem):
    return pltpu.CompilerParams(dimension_semantics=sem, vmem_limit_bytes=VMEM_LIMIT_V7X)


def _sig(x):
    return 1.0 / (1.0 + jnp.exp(-x))


def _dot(a, b):
    return jnp.dot(a, b, preferred_element_type=F32)


def _dot_nt(a, b):
    return lax.dot_general(a, b, (((1,), (1,)), ((), ())), preferred_element_type=F32)


def _dot_tn(a, b):
    return lax.dot_general(a, b, (((0,), (0,)), ((), ())), preferred_element_type=F32)


def _bf(x):
    return x.astype(BF16)


ANY = pl.BlockSpec(memory_space=pl.ANY)


class _Rider:
    def __init__(self, args, out_shape, sems, begin, end):
        self.args, self.out_shape, self.sems, self.begin, self.end = list(args), list(out_shape), list(sems), begin, end
        self.result = None


def _pcall(body, *, grid, in_specs, out_specs, out_shape, name, sem, args, scratch_shapes=(), rider=None):
    multi = isinstance(out_shape, (list, tuple))
    o_specs = list(out_specs) if multi else [out_specs]
    o_shape = list(out_shape) if multi else [out_shape]
    if rider is None:
        res = pl.pallas_call(body, grid=grid, in_specs=list(in_specs), out_specs=o_specs, out_shape=o_shape,
                             scratch_shapes=list(scratch_shapes), name=name, compiler_params=_params(sem))(*args)
        return list(res) if multi else res[0]
    counts = [len(in_specs), len(rider.args), len(o_specs), len(rider.out_shape), len(scratch_shapes)]

    def wrapped(*refs):
        groups, at = [], 0
        for c in counts:
            groups.append(refs[at:at + c])
            at += c
        h_in, r_in, h_out, r_out, h_scratch = groups
        r_sems = refs[at:]
        if grid:
            ids = [pl.program_id(a) for a in range(len(grid))]
            first = functools.reduce(jnp.logical_and, [i == 0 for i in ids])
            last = functools.reduce(jnp.logical_and, [i == g - 1 for i, g in zip(ids, grid)])
            pl.when(first)(lambda: rider.begin(r_in, r_out, r_sems))
            body(*h_in, *h_out, *h_scratch)
            pl.when(last)(lambda: rider.end(r_in, r_out, r_sems))
        else:
            rider.begin(r_in, r_out, r_sems)
            body(*h_in, *h_out, *h_scratch)
            rider.end(r_in, r_out, r_sems)

    res = pl.pallas_call(
        wrapped, grid=grid, in_specs=list(in_specs) + [ANY] * counts[1], out_specs=o_specs + [ANY] * counts[3],
        out_shape=o_shape + rider.out_shape, scratch_shapes=list(scratch_shapes) + rider.sems, name=name,
        compiler_params=_params(("arbitrary",) * len(grid)))(*args, *rider.args)
    rider.result = list(res[counts[2]:])
    return list(res[:counts[2]]) if multi else res[0]


def _mm_nn(a, b3, *, name, tm, tn, out_dtype, res=None, alpha=1.0, rider=None):
    m, k = a.shape
    nb, _, nw = b3.shape
    per = nw // tn
    assert nw % tn == 0 and m % tm == 0
    has_res = res is not None

    def body(*refs):
        if has_res:
            a_ref, b_ref, r_ref, o_ref = refs
        else:
            a_ref, b_ref, o_ref = refs
        acc = _dot(_bf(a_ref[...]), b_ref[...])
        if alpha != 1.0:
            acc = alpha * acc
        if has_res:
            acc = r_ref[...] + acc
        o_ref[...] = acc.astype(o_ref.dtype)

    in_specs = [pl.BlockSpec((tm, k), lambda i, j: (i, 0)),
                pl.BlockSpec((None, k, tn), lambda i, j: (j // per, 0, j % per))]
    args = [a, b3]
    if has_res:
        in_specs.append(pl.BlockSpec((tm, tn), lambda i, j: (i, j)))
        args.append(res)
    return _pcall(body, grid=(m // tm, nb * per), in_specs=in_specs, out_specs=pl.BlockSpec((tm, tn), lambda i, j: (i, j)),
                  out_shape=jax.ShapeDtypeStruct((m, nb * nw), out_dtype), name=name, sem=("parallel", "arbitrary"),
                  args=args, rider=rider)


def _mm_nt(d, b3, *, name, tm, tp, tn, out_dtype, alpha=1.0, rider=None, norm=None):
    m, n = d.shape
    nb, p, nw = b3.shape
    per = nw // tn
    nk = n // tn
    assert nb * nw == n and nw % tn == 0 and p % tp == 0 and m % tm == 0 and (norm is None or tp == p)

    def body(d_ref, b_ref, *refs):
        kk = pl.program_id(2)
        acc_ref = refs[-1]

        @pl.when(kk == 0)
        def _():
            acc_ref[...] = jnp.zeros_like(acc_ref)

        acc_ref[...] += _dot_nt(_bf(d_ref[...]), b_ref[...])

        if norm is None:
            @pl.when(kk == nk - 1)
            def _():
                refs[0][...] = (alpha * acc_ref[...]).astype(refs[0].dtype)
        else:
            x_ref, g_ref, dx_ref, o_ref, dg_ref = refs[:5]

            @pl.when(jnp.logical_and(pl.program_id(0) == 0, kk == 0))
            def _():
                dg_ref[...] = jnp.zeros_like(dg_ref)

            @pl.when(kk == nk - 1)
            def _():
                dh = alpha * acc_ref[...]
                xv = x_ref[...]
                r = _rms_rows(xv)
                xh = xv * r
                dxh = dh * g_ref[...]
                o_ref[...] = dx_ref[...] + r * (dxh - xh * jnp.mean(dxh * xh, axis=1, keepdims=True))
                dg_ref[...] += jnp.sum(dh * xh, axis=0, keepdims=True)

    in_specs = [pl.BlockSpec((tm, tn), lambda i, j, kk: (i, kk)),
                pl.BlockSpec((None, tp, tn), lambda i, j, kk: (kk // per, j, kk % per))]
    tile = pl.BlockSpec((tm, tp), lambda i, j, kk: (i, j))
    if norm is None:
        return _pcall(body, grid=(m // tm, p // tp, nk), in_specs=in_specs, out_specs=tile,
                      out_shape=jax.ShapeDtypeStruct((m, p), out_dtype), scratch_shapes=[pltpu.VMEM((tm, tp), F32)],
                      name=name, sem=("parallel", "parallel", "arbitrary"), args=(d, b3), rider=rider)
    x, g, dx = norm
    row = pl.BlockSpec((1, p), lambda i, j, kk: (0, 0))
    return _pcall(body, grid=(m // tm, 1, nk), in_specs=in_specs + [tile, row, tile], out_specs=[tile, row],
                  out_shape=[jax.ShapeDtypeStruct((m, p), F32), jax.ShapeDtypeStruct((1, p), F32)],
                  scratch_shapes=[pltpu.VMEM((tm, tp), F32)], name=name, sem=("arbitrary", "arbitrary", "arbitrary"),
                  args=(d, b3, x, g, dx), rider=rider)


def _mm_tn(a, d, *, nb, name, tm, tk, tn, alpha=1.0, rider=None):
    m, k = a.shape
    _, n = d.shape
    nw = n // nb
    per = nw // tn
    nm = m // tm
    assert nw % tn == 0 and k % tk == 0 and m % tm == 0

    def body(a_ref, d_ref, o_ref, acc_ref):
        mm = pl.program_id(2)

        @pl.when(mm == 0)
        def _():
            acc_ref[...] = jnp.zeros_like(acc_ref)

        acc_ref[...] += _dot_tn(_bf(a_ref[...]), _bf(d_ref[...]))

        @pl.when(mm == nm - 1)
        def _():
            o_ref[...] = (alpha * acc_ref[...]).astype(o_ref.dtype)

    return _pcall(
        body, grid=(k // tk, nb * per, nm),
        in_specs=[pl.BlockSpec((tm, tk), lambda i, j, mm: (mm, i)),
                  pl.BlockSpec((tm, tn), lambda i, j, mm: (mm, j))],
        out_specs=pl.BlockSpec((None, tk, tn), lambda i, j, mm: (j // per, i, j % per)),
        out_shape=jax.ShapeDtypeStruct((nb, k, nw), BF16),
        scratch_shapes=[pltpu.VMEM((tk, tn), F32)],
        name=name, sem=("parallel", "parallel", "arbitrary"), args=(a, d), rider=rider)


def _rows_from_view(ref, buf, w, d, tm):
    for k in range(d):
        for c in range(w // HEAD):
            lanes = slice(k * w + c * HEAD, k * w + (c + 1) * HEAD)
            buf.at[c][pl.ds(k, tm // d, stride=d), :] = ref[:, lanes].astype(F32)
    return _cat([buf[c] for c in range(w // HEAD)])


def _ew(fn, ins, outs, *, rows, tm, name):
    in_specs, args, scratch = [], [], []
    for s in ins:
        if s[0] == 't':
            _, arr, w, cb = s
            in_specs.append(pl.BlockSpec((tm, w), lambda i, cb=cb: (i, cb)))
        elif s[0] == 'v':
            _, arr, w, d = s
            in_specs.append(pl.BlockSpec((tm // d, d * w), lambda i: (i, 0)))
            scratch.append(pltpu.VMEM((w // HEAD, tm, HEAD), F32))
        else:
            arr = s[1]
            in_specs.append(pl.BlockSpec(arr.shape, lambda i, nd=arr.ndim: (0,) * nd))
        args.append(arr)
    out_specs, out_shape = [], []
    for s in outs:
        if s[0] == 't':
            _, w, dt = s
            out_specs.append(pl.BlockSpec((tm, w), lambda i: (i, 0)))
            out_shape.append(jax.ShapeDtypeStruct((rows, w), dt))
        elif s[0] == 'v':
            _, w, dt, d = s
            out_specs.append(pl.BlockSpec((tm // d, d * w), lambda i: (i, 0)))
            out_shape.append(jax.ShapeDtypeStruct((rows // d, d * w), dt))
            scratch.append(pltpu.VMEM((w // HEAD, tm, HEAD), F32))
        else:
            out_specs.append(pl.BlockSpec(s[1], lambda i: (0, 0)))
            out_shape.append(jax.ShapeDtypeStruct(s[1], F32))
    n_in, n_out = len(ins), len(outs)

    def body(*refs):
        bufs = list(refs[n_in + n_out:])
        vals = []
        for r, s in zip(refs[:n_in], ins):
            if s[0] == 'v':
                vals.append(_rows_from_view(r, bufs.pop(0), s[2], s[3], tm))
            else:
                vals.append(r[...])
        res = fn(*vals)
        if not isinstance(res, (tuple, list)):
            res = (res,)
        for r, s, v in zip(refs[n_in:n_in + n_out], outs, res):
            if s[0] == 't':
                r[...] = v.astype(r.dtype)
            elif s[0] == 'v':
                w, d, buf = s[1], s[3], bufs.pop(0)
                for c in range(w // HEAD):
                    buf[c] = v[:, c * HEAD:(c + 1) * HEAD].astype(F32)
                for k in range(d):
                    for c in range(w // HEAD):
                        lanes = slice(k * w + c * HEAD, k * w + (c + 1) * HEAD)
                        r[:, lanes] = buf.at[c][pl.ds(k, tm // d, stride=d), :].astype(r.dtype)
            else:
                @pl.when(pl.program_id(0) == 0)
                def _(r=r):
                    r[...] = jnp.zeros_like(r)

                r[...] += v

    res = pl.pallas_call(
        body, grid=(rows // tm,), in_specs=in_specs, out_specs=out_specs, out_shape=out_shape, scratch_shapes=scratch,
        name=name, compiler_params=_params(("arbitrary",)))(*args)
    return res


def _tile(arr, w, g):
    return ('t', arr, w, 0) if DILATIONS[g] == 1 else ('v', arr, w, DILATIONS[g])


def _tile_out(w, dtype, g):
    return ('t', w, dtype) if DILATIONS[g] == 1 else ('v', w, dtype, DILATIONS[g])


def _heads(x):
    return [x[:, h * HEAD:(h + 1) * HEAD] for h in range(x.shape[1] // HEAD)]


def _cat(xs):
    return jnp.concatenate(xs, axis=1)


def _head_mean(x):
    return _cat([jnp.broadcast_to(jnp.mean(h, axis=1, keepdims=True), h.shape) for h in _heads(x)])


def _rms_rows(x):
    return lax.rsqrt(jnp.mean(x * x, axis=1, keepdims=True) + EPS)


def _norm_fwd(x, g, name):
    return _ew(lambda xv, gv: xv * _rms_rows(xv) * gv,
               [('t', x, D_MODEL, 0), ('f', g)], [('t', D_MODEL, BF16)], rows=x.shape[0], tm=512, name=name)[0]


def _loss_fwd_bwd(y, target, name):
    def fn(yv, tv):
        e = yv - tv
        return e * (1.0 / D_MODEL), jnp.sum(e * e, axis=0, keepdims=True)

    return _ew(fn, [('t', y, D_MODEL, 0), ('t', target, D_MODEL, 0)], [('t', D_MODEL, F32), ('acc', (1, D_MODEL))],
               rows=y.shape[0], tm=512, name=name)


def _rot(x):
    sgn = jnp.where(lax.broadcasted_iota(jnp.int32, x.shape, 1) < HEAD // 2, -1.0, 1.0)
    return pltpu.roll(x, HEAD // 2, 1) * sgn


def _gain_rows(qn, kn):
    return [a[g:g + 1] for a in (qn, kn) for g in range(ATT_GROUPS)]


def _qk_fwd(proj, cos, sin, qn, kn, name):
    def fn(*v):
        xs, cosv, sinv, gains, vs = v[:6], v[6], v[7], v[8:14], v[14:17]
        outs = []
        for j, x in enumerate(xs):
            gain = gains[j]
            ys = []
            for xh in _heads(x.astype(F32)):
                xn = xh * _rms_rows(xh) * gain
                ys.append(xn * cosv + _rot(xn) * sinv)
            outs.append(_cat(ys))
        return outs + list(vs)

    ins = ([('t', proj, 512, CB_AQ + j) for j in range(6)] + [('t', cos, HEAD, 0), ('t', sin, HEAD, 0)]
           + [('f', a) for a in _gain_rows(qn, kn)] + [('t', proj, 512, CB_AV + g) for g in range(ATT_GROUPS)])
    return _ew(fn, ins, [_tile_out(ATT_GW, BF16, j % ATT_GROUPS) for j in range(9)], rows=proj.shape[0], tm=512, name=name)


def _qk_bwd(dqk, proj, cos, sin, qn, kn, name):
    def fn(*v):
        ds, xs, cosv, sinv, gains = v[:6], v[6:12], v[12], v[13], v[14:20]
        rows8 = lax.broadcasted_iota(jnp.int32, (8, HEAD), 0)
        outs, dgs = [], [jnp.zeros((8, HEAD), F32)] * 2
        for j in range(6):
            gain = gains[j]
            dx, dg = [], jnp.zeros((1, HEAD), F32)
            for dyh, xh in zip(_heads(ds[j]), _heads(xs[j].astype(F32))):
                r = _rms_rows(xh)
                xhat = xh * r
                dxn = dyh * cosv - _rot(dyh * sinv)
                dg = dg + jnp.sum(dxn * xhat, axis=0, keepdims=True)
                dxh = dxn * gain
                dx.append(r * (dxh - xhat * jnp.mean(dxh * xhat, axis=1, keepdims=True)))
            outs.append(_cat(dx))
            dgs[j // 3] = dgs[j // 3] + jnp.where(rows8 == j % 3, dg, 0.0)
        return _cat(outs), dgs[0], dgs[1]

    ins = ([_tile(a, ATT_GW, j % ATT_GROUPS) for j, a in enumerate(dqk)] + [('t', proj, 512, CB_AQ + j) for j in range(6)]
           + [('t', cos, HEAD, 0), ('t', sin, HEAD, 0)] + [('f', a) for a in _gain_rows(qn, kn)])
    return _ew(fn, ins, [('t', 6 * ATT_GW, BF16), ('acc', (8, HEAD)), ('acc', (8, HEAD))],
               rows=proj.shape[0], tm=512, name=name)


def _pick(x, h):
    lanes = lax.broadcasted_iota(jnp.int32, x.shape, 1)
    return jnp.sum(jnp.where(lanes == h, x, 0.0), axis=1, keepdims=True)


def _spread(x):
    return _cat([jnp.broadcast_to(_pick(x, h), (x.shape[0], HEAD)) for h in range(ATT_HEADS)])


def _compact(x):
    lanes = lax.broadcasted_iota(jnp.int32, (x.shape[0], HEAD), 1)
    out = jnp.zeros((x.shape[0], HEAD), F32)
    for h, xh in enumerate(_heads(x)):
        out = jnp.where(lanes == h, xh, out)
    return out


def _group_weights(l0, l1, l2):
    l0, l1, l2 = _spread(l0), _spread(l1), _spread(l2)
    m = jnp.maximum(jnp.maximum(l0, l1), l2)
    e0, e1, e2 = jnp.exp(l0 - m), jnp.exp(l1 - m), jnp.exp(l2 - m)
    inv = 1.0 / (e0 + e1 + e2)
    return e0 * inv, e1 * inv, e2 * inv


def _merge_fwd(outs, lses, name):
    def fn(o0, o1, o2, l0, l1, l2):
        a0, a1, a2 = _group_weights(l0, l1, l2)
        return a0 * o0 + a1 * o1 + a2 * o2

    ins = [_tile(a, ATT_GW, g) for g, a in enumerate(outs)] + [_tile(a, HEAD, g) for g, a in enumerate(lses)]
    return _ew(fn, ins, [('t', ATT_GW, BF16)], rows=outs[0].shape[0], tm=1024, name=name)[0]


def _merge_bwd(dob, outs, lses, name):
    def fn(dov, o0, o1, o2, l0, l1, l2):
        a0, a1, a2 = _group_weights(l0, l1, l2)
        ob = a0 * o0 + a1 * o1 + a2 * o2
        s = _head_mean(dov * ob) * float(HEAD)
        return a0 * dov, a1 * dov, a2 * dov, _compact(a0 * s), _compact(a1 * s), _compact(a2 * s)

    ins = ([('t', dob, ATT_GW, 0)] + [_tile(a, ATT_GW, g) for g, a in enumerate(outs)]
           + [_tile(a, HEAD, g) for g, a in enumerate(lses)])
    groups = range(ATT_GROUPS)
    return _ew(fn, ins, [_tile_out(ATT_GW, BF16, g) for g in groups] + [_tile_out(HEAD, F32, g) for g in groups],
               rows=dob.shape[0], tm=1024, name=name)


HG_ROWS = 256


def _hg_gates(hq, hf, hi, lbv):
    sig = _sig(hf)
    f = lbv + (1.0 - lbv) * sig
    return hq * _sig(hq), 1.0 - f, hi, jnp.log(f), sig, f


def _split3(x):
    hi = _bf(x)
    r1 = x - hi.astype(F32)
    mid = _bf(r1)
    return hi, mid, _bf(r1 - mid.astype(F32))


def _tri_dot(tri, x):
    hi, mid, lo = _split3(x)
    return _dot(tri, hi) + _dot(tri, mid) + _dot(tri, lo)


def _row(x, i):
    rows = lax.broadcasted_iota(jnp.int32, x.shape, 0)
    return jnp.sum(jnp.where(rows == i, x, 0.0), axis=0, keepdims=True)


def _hg_decay(logf, q, k):
    c = HG_CHUNK
    row = lax.broadcasted_iota(jnp.int32, (c, c), 0)
    col = lax.broadcasted_iota(jnp.int32, (c, c), 1)
    g = _tri_dot((row >= col).astype(BF16), logf)
    gm = _row(g, c // 2 - 1)
    gl = _row(g, c - 1)
    decays = jnp.exp(g), jnp.exp(g - gm), jnp.exp(gm - g), jnp.exp(gl - g)
    return gl, decays, q * decays[0], q * decays[1], k * decays[2], k * decays[3]


def _hg_out_fwd(o, hg, gain):
    r = lax.rsqrt(_head_mean(o * o) + EPS)
    return o * r * gain * (hg * _sig(hg))


def _hgrn_fwd(proj, hf, lb, gain, name, rider=None):
    t = proj.shape[0]
    nck = HG_ROWS // HG_CHUNK

    def body(hq_ref, hf_ref, hi_ref, hg_ref, lb_ref, gn_ref, o_ref, oa_ref, sall_ref, st_ref):
        @pl.when(pl.program_id(0) == 0)
        def _():
            st_ref[...] = jnp.zeros_like(st_ref)

        lbv = lb_ref[...]
        gnv = gn_ref[...]
        c = HG_CHUNK
        mask = lax.broadcasted_iota(jnp.int32, (c, c), 0) >= lax.broadcasted_iota(jnp.int32, (c, c), 1)

        def chunk(cc, carry):
            sl = pl.ds(pl.multiple_of(cc * c, c), c)
            q, k, v, logf, _, _ = _hg_gates(hq_ref[sl, :].astype(F32), hf_ref[sl, :], hi_ref[sl, :].astype(F32), lbv)
            gl, _, qg, qt, kt, kd = _hg_decay(logf, q, k)
            egl = jnp.exp(gl)
            os = []
            for h in range(HG_HEADS):
                hs = slice(h * HEAD, (h + 1) * HEAD)
                st = st_ref[h]
                sall_ref[cc, h] = st
                a = jnp.where(mask, _dot_nt(_bf(qt[:, hs]), _bf(kt[:, hs])), 0.0)
                os.append(_dot(_bf(a), _bf(v[:, hs])) + _dot_nt(_bf(qg[:, hs]), _bf(st)))
                st_ref[h] = egl[:, hs] * st + _dot_tn(_bf(v[:, hs]), _bf(kd[:, hs]))
            o = _cat(os)
            o_ref[sl, :] = o
            oa_ref[sl, :] = _hg_out_fwd(o, hg_ref[sl, :].astype(F32), gnv).astype(oa_ref.dtype)
            return carry

        lax.fori_loop(0, nck, chunk, 0)

    col = lambda j: pl.BlockSpec((HG_ROWS, D_MODEL), lambda i, j=j: (i, j))
    small = pl.BlockSpec((1, D_MODEL), lambda i: (0, 0))
    return _pcall(
        body, grid=(t // HG_ROWS,),
        in_specs=[col(0), col(0), col(2), col(3), small, small],
        out_specs=[col(0), col(0), pl.BlockSpec((nck, HG_HEADS, HEAD, HEAD), lambda i: (i, 0, 0, 0))],
        out_shape=[jax.ShapeDtypeStruct((t, D_MODEL), F32), jax.ShapeDtypeStruct((t, D_MODEL), BF16),
                   jax.ShapeDtypeStruct((t // HG_CHUNK, HG_HEADS, HEAD, HEAD), F32)],
        scratch_shapes=[pltpu.VMEM((HG_HEADS, HEAD, HEAD), F32)],
        name=name, sem=("arbitrary",), args=(proj, hf, proj, proj, lb, gain), rider=rider)


def _terms(x, precise):
    hi = _bf(x)
    return (hi, _bf(x - hi.astype(F32))) if precise else (hi,)


def _mm(dot, a, b):
    out = dot(a[0], b[0])
    if len(a) > 1:
        out = out + dot(a[1], b[0])
    if len(b) > 1:
        out = out + dot(a[0], b[1])
    return out


def _hgrn_bwd(doa, oscan, proj, hf, sall, lb, gain, dqk, dvs, dgab, name, precise, rider=None):
    t = proj.shape[0]
    nck = HG_ROWS // HG_CHUNK
    nsteps = t // HG_ROWS
    terms = functools.partial(_terms, precise=precise)
    n_view = sum(d > 1 for d in DILATIONS)

    def body(doa_ref, os_ref, hq_ref, hf_ref, hi_ref, hg_ref, sall_ref, lb_ref, gn_ref, dqk_ref, dv0_ref, dv1_ref,
             dv2_ref, dgab_ref, dproj_ref, dgn_ref, dlb_ref, dst_ref, *bufs):
        @pl.when(pl.program_id(0) == 0)
        def _():
            dst_ref[...] = jnp.zeros_like(dst_ref)
            dgn_ref[...] = jnp.zeros_like(dgn_ref)
            dlb_ref[...] = jnp.zeros_like(dlb_ref)

        at = 4 * D_MODEL
        dproj_ref[:, at:at + 6 * ATT_GW] = dqk_ref[...]
        at += 6 * ATT_GW
        spare = list(bufs)
        for d, dv_ref in zip(DILATIONS, (dv0_ref, dv1_ref, dv2_ref)):
            dv = dv_ref[...] if d == 1 else _rows_from_view(dv_ref, spare.pop(0), ATT_GW, d, HG_ROWS)
            dproj_ref[:, at:at + ATT_GW] = dv.astype(dproj_ref.dtype)
            at += ATT_GW
        dproj_ref[:, at:] = dgab_ref[...]

        lbv = lb_ref[...]
        gnv = gn_ref[...]
        c = HG_CHUNK
        row = lax.broadcasted_iota(jnp.int32, (c, c), 0)
        colm = lax.broadcasted_iota(jnp.int32, (c, c), 1)
        mask = row >= colm
        triu = (row <= colm).astype(BF16)
        last = lax.broadcasted_iota(jnp.int32, (c, HEAD), 0) == c - 1

        def chunk(ci, carry):
            cc = nck - 1 - ci
            sl = pl.ds(pl.multiple_of(cc * c, c), c)
            hq, hg = hq_ref[sl, :].astype(F32), hg_ref[sl, :].astype(F32)
            q, k, v, logf, sig, f = _hg_gates(hq, hf_ref[sl, :], hi_ref[sl, :].astype(F32), lbv)
            gl, (e_qg, e_qt, e_kt, e_kd), qg, qt, kt, kd = _hg_decay(logf, q, k)
            egl = jnp.exp(gl)
            o = os_ref[sl, :]
            dy = doa_ref[sl, :]
            r = lax.rsqrt(_head_mean(o * o) + EPS)
            oh = o * r
            sg = _sig(hg)
            silu_g = hg * sg
            dgn_ref[...] += jnp.sum(dy * oh * silu_g, axis=0, keepdims=True)
            dhg = dy * oh * gnv * (sg * (1.0 + hg * (1.0 - sg)))
            doh = dy * gnv * silu_g
            do = r * (doh - oh * _head_mean(doh * oh))
            dqs, dks, dvs, dgs = [], [], [], []
            for h in range(HG_HEADS):
                hs = slice(h * HEAD, (h + 1) * HEAD)
                st = sall_ref[cc, h]
                dst = dst_ref[h]
                qt_h, kt_h, qg_h, kd_h = qt[:, hs], kt[:, hs], qg[:, hs], kd[:, hs]
                do_p, v_p, qt_p, kt_p, qg_p = terms(do[:, hs]), terms(v[:, hs]), terms(qt_h), terms(kt_h), terms(qg_h)
                st_p, dst_p = terms(st), terms(dst)
                a = jnp.where(mask, _dot_nt(qt_p[0], kt_p[0]), 0.0)
                da = terms(jnp.where(mask, _mm(_dot_nt, do_p, v_p), 0.0))
                dqt = _mm(_dot, da, kt_p)
                dkt = _mm(_dot_tn, da, qt_p)
                dqg = _mm(_dot, do_p, st_p)
                dv = _dot_tn(_bf(a), do_p[0]) + _dot_nt(_bf(kd_h), dst_p[0])
                dkd = _mm(_dot, v_p, dst_p)
                dgl = egl[:, hs] * jnp.sum(st * dst, axis=0, keepdims=True) + jnp.sum(dkd * kd_h, axis=0, keepdims=True)
                dst_ref[h] = egl[:, hs] * dst + _mm(_dot_tn, do_p, qg_p)
                dqs.append(dqt * e_qt[:, hs] + dqg * e_qg[:, hs])
                dks.append(dkt * e_kt[:, hs] + dkd * e_kd[:, hs])
                dvs.append(dv)
                dgs.append(dqt * qt_h - dkt * kt_h + dqg * qg_h - dkd * kd_h + jnp.where(last, dgl, 0.0))
            dq, dk, dv, dg = _cat(dqs), _cat(dks), _cat(dvs), _cat(dgs)
            dlogf = _tri_dot(triu, dg)
            df = dlogf / f - dk
            dlb_ref[...] += jnp.sum(df * (1.0 - sig), axis=0, keepdims=True)
            dhf = df * (1.0 - lbv) * sig * (1.0 - sig)
            sq = _sig(hq)
            dhq = dq * (sq * (1.0 + hq * (1.0 - sq)))
            dproj_ref[sl, :4 * D_MODEL] = _cat([dhq, dhf, dv, dhg]).astype(dproj_ref.dtype)
            return carry

        lax.fori_loop(0, nck, chunk, 0)

    rev = lambda j: pl.BlockSpec((HG_ROWS, D_MODEL), lambda i, j=j: (nsteps - 1 - i, j))
    rows = lambda a, d=1: pl.BlockSpec((HG_ROWS // d, a.shape[1]), lambda i: (nsteps - 1 - i, 0))
    small = pl.BlockSpec((1, D_MODEL), lambda i: (0, 0))
    return _pcall(
        body, grid=(nsteps,),
        in_specs=[rev(0), rev(0), rev(0), rev(0), rev(2), rev(3),
                  pl.BlockSpec((nck, HG_HEADS, HEAD, HEAD), lambda i: (nsteps - 1 - i, 0, 0, 0)), small, small,
                  rows(dqk)] + [rows(a, d) for a, d in zip(dvs, DILATIONS)] + [rows(dgab)],
        out_specs=[pl.BlockSpec((HG_ROWS, P_IN), lambda i: (nsteps - 1 - i, 0)), small, small],
        out_shape=[jax.ShapeDtypeStruct((t, P_IN), BF16), jax.ShapeDtypeStruct((1, D_MODEL), F32),
                   jax.ShapeDtypeStruct((1, D_MODEL), F32)],
        scratch_shapes=[pltpu.VMEM((HG_HEADS, HEAD, HEAD), F32)] + [pltpu.VMEM((ATT_HEADS, HG_ROWS, HEAD), F32)] * n_view,
        name=name, sem=("arbitrary",), args=(doa, oscan, proj, hf, proj, proj, sall, lb, gain, dqk, *dvs, dgab),
        rider=rider)


def _window_masks(has_previous):
    qi = lax.broadcasted_iota(jnp.int32, (ATT_BLK, 2 * ATT_BLK), 0)
    ki = lax.broadcasted_iota(jnp.int32, (ATT_BLK, 2 * ATT_BLK), 1)
    band = jnp.logical_and(ki >= qi, ki <= qi + ATT_BLK)
    return band, jnp.logical_and(band, jnp.logical_or(ki >= ATT_BLK, has_previous))


def _two_blocks(ref, prev_ref, j, hs):
    if j == 0:
        return jnp.concatenate([prev_ref[:, hs], ref[0:ATT_BLK, hs]], axis=0)
    return ref[(j - 1) * ATT_BLK:(j + 1) * ATT_BLK, hs]


def _attn_cfg(qg, g):
    d = DILATIONS[g]
    length = qg.shape[0]
    assert qg.shape[1] == d * ATT_GW
    nb = length // ATT_BLK
    return d, length, nb, min(ATT_STEP_BLOCKS, nb)


def _attn_fwd(qg, kg, vg, g, name):
    d, length, nb, rb = _attn_cfg(qg, g)
    scale = HEAD ** -0.5

    def body(q_ref, k_ref, v_ref, kp_ref, vp_ref, o_ref, l_ref):
        n = pl.program_id(1)
        band, first_band = _window_masks(n > 0)
        lanes = lax.broadcasted_iota(jnp.int32, (ATT_BLK, HEAD), 1)
        for j in range(rb):
            rows = slice(j * ATT_BLK, (j + 1) * ATT_BLK)
            lse = jnp.zeros((ATT_BLK, HEAD), F32)
            for h in range(ATT_HEADS):
                hs = slice(h * HEAD, (h + 1) * HEAD)
                k2, v2 = _two_blocks(k_ref, kp_ref, j, hs), _two_blocks(v_ref, vp_ref, j, hs)
                s = jnp.where(first_band if j == 0 else band, _dot_nt(q_ref[rows, hs], k2) * scale, NEG)
                m = jnp.max(s, axis=1, keepdims=True)
                p = jnp.exp(s - m)
                l = jnp.sum(p, axis=1, keepdims=True)
                o_ref[rows, hs] = (_dot(_bf(p), v2) / l).astype(o_ref.dtype)
                lse = jnp.where(lanes == h, m + jnp.log(l), lse)
            l_ref[rows, :] = lse

    own = pl.BlockSpec((rb * ATT_BLK, ATT_GW), lambda r, n: (n, r))
    own_head = pl.BlockSpec((rb * ATT_BLK, HEAD), lambda r, n: (n, r))
    prev = pl.BlockSpec((ATT_BLK, ATT_GW), lambda r, n: (jnp.maximum(n * rb - 1, 0), r))
    return pl.pallas_call(
        body, grid=(d, nb // rb), in_specs=[own, own, own, prev, prev], out_specs=[own, own_head],
        out_shape=[jax.ShapeDtypeStruct((length, d * ATT_GW), BF16), jax.ShapeDtypeStruct((length, d * HEAD), F32)],
        name=name, compiler_params=_params(("parallel", "arbitrary")))(qg, kg, vg, kg, vg)


def _attn_bwd(qg, kg, vg, dog, lse, delta, g, name):
    d, length, nb, rb = _attn_cfg(qg, g)
    nsteps = nb // rb
    scale = HEAD ** -0.5

    def body(q_ref, k_ref, v_ref, do_ref, l_ref, dl_ref, kp_ref, vp_ref, qn_ref, don_ref, ln_ref, dln_ref,
             dq_ref, dk_ref, dv_ref):
        n = pl.program_id(1)
        band, first_band = _window_masks(n > 0)
        qi = lax.broadcasted_iota(jnp.int32, (ATT_BLK, ATT_BLK), 0)
        ki = lax.broadcasted_iota(jnp.int32, (ATT_BLK, ATT_BLK), 1)
        next_m = jnp.logical_and(ki >= qi, n < nsteps - 1)
        last = slice((rb - 1) * ATT_BLK, rb * ATT_BLK)
        for h in range(ATT_HEADS):
            hs = slice(h * HEAD, (h + 1) * HEAD)
            dk, dv = [None] * rb, [None] * rb
            for j in range(rb):
                rows = slice(j * ATT_BLK, (j + 1) * ATT_BLK)
                q, do = q_ref[rows, hs], do_ref[rows, hs]
                k2, v2 = _two_blocks(k_ref, kp_ref, j, hs), _two_blocks(v_ref, vp_ref, j, hs)
                p = jnp.where(first_band if j == 0 else band,
                              jnp.exp(_dot_nt(q, k2) * scale - _pick(l_ref[rows, :], h)), 0.0)
                ds = _bf(p * (_dot_nt(do, v2) - _pick(dl_ref[rows, :], h)) * scale)
                dq_ref[rows, hs] = _dot(ds, k2).astype(dq_ref.dtype)
                dk2, dv2 = _dot_tn(ds, q), _dot_tn(_bf(p), do)
                if j >= 1:
                    dk[j - 1] = dk[j - 1] + dk2[:ATT_BLK]
                    dv[j - 1] = dv[j - 1] + dv2[:ATT_BLK]
                dk[j], dv[j] = dk2[ATT_BLK:], dv2[ATT_BLK:]
            q, do = qn_ref[:, hs], don_ref[:, hs]
            p = jnp.where(next_m, jnp.exp(_dot_nt(q, k_ref[last, hs]) * scale - _pick(ln_ref[...], h)), 0.0)
            ds = _bf(p * (_dot_nt(do, v_ref[last, hs]) - _pick(dln_ref[...], h)) * scale)
            dk[rb - 1] = dk[rb - 1] + _dot_tn(ds, q)
            dv[rb - 1] = dv[rb - 1] + _dot_tn(_bf(p), do)
            for j in range(rb):
                rows = slice(j * ATT_BLK, (j + 1) * ATT_BLK)
                dk_ref[rows, hs] = dk[j].astype(dk_ref.dtype)
                dv_ref[rows, hs] = dv[j].astype(dv_ref.dtype)

    own = pl.BlockSpec((rb * ATT_BLK, ATT_GW), lambda r, n: (n, r))
    prev = pl.BlockSpec((ATT_BLK, ATT_GW), lambda r, n: (jnp.maximum(n * rb - 1, 0), r))
    nxt = pl.BlockSpec((ATT_BLK, ATT_GW), lambda r, n: (jnp.minimum((n + 1) * rb, nb - 1), r))
    own_head = pl.BlockSpec((rb * ATT_BLK, HEAD), lambda r, n: (n, r))
    nxt_head = pl.BlockSpec((ATT_BLK, HEAD), lambda r, n: (jnp.minimum((n + 1) * rb, nb - 1), r))
    return pl.pallas_call(
        body, grid=(d, nsteps), in_specs=[own] * 4 + [own_head] * 2 + [prev, prev, nxt, nxt, nxt_head, nxt_head],
        out_specs=[own, own, own], out_shape=[jax.ShapeDtypeStruct((length, d * ATT_GW), BF16)] * 3,
        name=name, compiler_params=_params(("parallel", "arbitrary")))(
            qg, kg, vg, dog, lse, delta, kg, vg, qg, dog, lse, delta)


def _rope_tables(t):
    pos = jnp.arange(t, dtype=F32)
    inv = ROPE_THETA ** (-jnp.arange(0, HEAD, 2, dtype=F32) / HEAD)
    ang = pos[:, None] * inv[None, :]
    ang = jnp.concatenate([ang, ang], axis=-1)
    return jnp.cos(ang), jnp.sin(ang)


def _lower_bounds(logits):
    lb = jnp.cumsum(jax.nn.softmax(logits.astype(F32), axis=0), axis=0)
    return lb - lb[0:1]


FFN_ROWS = 256
FF_SHARD = 2 * D_FF // N_CHIPS


def _ffn_in_act(x, g, w_in, name, rider=None):
    t = x.shape[0]

    def body(x_ref, g_ref, w_ref, h_ref, ab_ref, u_ref):
        xv = x_ref[...]
        h = _bf(xv * _rms_rows(xv) * g_ref[...])
        h_ref[...] = h
        for s in range(N_CHIPS // 2):
            cols = slice(s * FF_SHARD, (s + 1) * FF_SHARD)
            a = _dot(h, w_ref[s])
            b = _dot(h, w_ref[s + N_CHIPS // 2])
            ab_ref[:, cols] = a.astype(ab_ref.dtype)
            ab_ref[:, D_FF + s * FF_SHARD:D_FF + (s + 1) * FF_SHARD] = b.astype(ab_ref.dtype)
            u_ref[:, cols] = (a * _sig(a) * b).astype(u_ref.dtype)

    row = lambda w: pl.BlockSpec((FFN_ROWS, w), lambda i: (i, 0))
    return _pcall(
        body, grid=(t // FFN_ROWS,),
        in_specs=[row(D_MODEL), pl.BlockSpec((1, D_MODEL), lambda i: (0, 0)),
                  pl.BlockSpec(w_in.shape, lambda i: (0, 0, 0))],
        out_specs=[row(D_MODEL), row(2 * D_FF), row(D_FF)],
        out_shape=[jax.ShapeDtypeStruct((t, D_MODEL), BF16), jax.ShapeDtypeStruct((t, 2 * D_FF), BF16),
                   jax.ShapeDtypeStruct((t, D_FF), BF16)],
        name=name, sem=("parallel",), args=(x, g, w_in), rider=rider)


def _ffn_bwd_du_act(dx, w_out, ab, name, rider=None):
    t = dx.shape[0]

    def body(dx_ref, w_ref, ab_ref, o_ref):
        du = 0.5 * _dot_nt(_bf(dx_ref[...]), w_ref[0])
        a = ab_ref[:, :D_FF].astype(F32)
        b = ab_ref[:, D_FF:].astype(F32)
        s = _sig(a)
        o_ref[:, :D_FF] = (du * b * (s * (1.0 + a * (1.0 - s)))).astype(o_ref.dtype)
        o_ref[:, D_FF:] = (du * a * s).astype(o_ref.dtype)

    row = lambda w: pl.BlockSpec((FFN_ROWS, w), lambda i: (i, 0))
    return _pcall(
        body, grid=(t // FFN_ROWS,),
        in_specs=[row(D_MODEL), pl.BlockSpec(w_out.shape, lambda i: (0, 0, 0)), row(2 * D_FF)],
        out_specs=row(2 * D_FF), out_shape=jax.ShapeDtypeStruct((t, 2 * D_FF), BF16),
        name=name, sem=("parallel",), args=(dx, w_out, ab), rider=rider)


MIX_ROWS = 512


def _gate_specs():
    return [pl.BlockSpec((MIX_ROWS, 512), lambda i, cb=cb: (i, cb)) for cb in (CB_GA, CB_GA + 1, CB_GB, CB_GB + 1)]


def _gate(lo_ref, hi_ref):
    return _sig(_cat([lo_ref[...], hi_ref[...]]).astype(F32))


def _whole(a):
    return pl.BlockSpec(a.shape, lambda i: (0,) * a.ndim)


def _mix_tail_fwd(oa, ob, proj, x, w_a, w_b, w_o, name):
    t = x.shape[0]

    def body(oa_ref, ob_ref, ga0, ga1, gb0, gb1, x_ref, wa_ref, wb_ref, wo_ref, y_ref, m_ref, ya_ref, yb_ref):
        ya = _dot(oa_ref[...], wa_ref[0])
        yb = _cat([_dot(ob_ref[...], wb_ref[s]) for s in range(N_CHIPS)])
        merged = _bf(_gate(ga0, ga1) * ya + _gate(gb0, gb1) * yb)
        m_ref[...] = merged
        ya_ref[...] = ya.astype(ya_ref.dtype)
        yb_ref[...] = yb.astype(yb_ref.dtype)
        y_ref[...] = x_ref[...] + _dot(merged, wo_ref[0])

    row = lambda w: pl.BlockSpec((MIX_ROWS, w), lambda i: (i, 0))
    return pl.pallas_call(
        body, grid=(t // MIX_ROWS,),
        in_specs=[row(D_MODEL), row(ATT_GW)] + _gate_specs() + [row(D_MODEL), _whole(w_a), _whole(w_b), _whole(w_o)],
        out_specs=[row(D_MODEL)] * 4,
        out_shape=[jax.ShapeDtypeStruct((t, D_MODEL), F32)] + [jax.ShapeDtypeStruct((t, D_MODEL), BF16)] * 3,
        name=name, compiler_params=_params(("parallel",)))(oa, ob, proj, proj, proj, proj, x, w_a, w_b, w_o)


def _mix_tail_bwd(dx, proj, ya, yb, w_a, w_b, w_o, name):
    t = dx.shape[0]
    shard = D_MODEL // N_CHIPS

    def body(dx_ref, ga0, ga1, gb0, gb1, ya_ref, yb_ref, wa_ref, wb_ref, wo_ref, dya_ref, dyb_ref, dg_ref, doa_ref, dob_ref):
        dm = _dot_nt(_bf(dx_ref[...]), wo_ref[0])
        sa, sb = _gate(ga0, ga1), _gate(gb0, gb1)
        dya, dyb = _bf(dm * sa), _bf(dm * sb)
        dya_ref[...] = dya
        dyb_ref[...] = dyb
        dg_ref[:, :D_MODEL] = (dm * ya_ref[...].astype(F32) * sa * (1.0 - sa)).astype(dg_ref.dtype)
        dg_ref[:, D_MODEL:] = (dm * yb_ref[...].astype(F32) * sb * (1.0 - sb)).astype(dg_ref.dtype)
        doa_ref[...] = _dot_nt(dya, wa_ref[0])
        dob = _dot_nt(dyb[:, :shard], wb_ref[0])
        for s in range(1, N_CHIPS):
            dob = dob + _dot_nt(dyb[:, s * shard:(s + 1) * shard], wb_ref[s])
        dob_ref[...] = dob

    row = lambda w: pl.BlockSpec((MIX_ROWS, w), lambda i: (i, 0))
    return pl.pallas_call(
        body, grid=(t // MIX_ROWS,),
        in_specs=[row(D_MODEL)] + _gate_specs() + [row(D_MODEL), row(D_MODEL), _whole(w_a), _whole(w_b), _whole(w_o)],
        out_specs=[row(D_MODEL), row(D_MODEL), row(2 * D_MODEL), row(D_MODEL), row(ATT_GW)],
        out_shape=[jax.ShapeDtypeStruct((t, D_MODEL), BF16), jax.ShapeDtypeStruct((t, D_MODEL), BF16),
                   jax.ShapeDtypeStruct((t, 2 * D_MODEL), BF16), jax.ShapeDtypeStruct((t, D_MODEL), F32),
                   jax.ShapeDtypeStruct((t, ATT_GW), F32)],
        name=name, compiler_params=_params(("parallel",)))(dx, proj, proj, proj, proj, ya, yb, w_a, w_b, w_o)


def _ffn_fwd(x, g, src, l, pre):
    tag = f"l{l}_{pre}"
    w_in = src.weight(l, pre + "_w_in")
    h, ab, u = _ffn_in_act(x, g, w_in, name=tag + "_in_act", rider=src.ride(tag + "_in_act"))
    w_out = src.weight(l, pre + "_w_out")
    y = _mm_nn(u, w_out, name=tag + "_out", tm=512, tn=D_MODEL, out_dtype=F32, res=x, alpha=0.5, rider=src.ride(tag + "_out"))
    return y, (x, h, ab, u, w_in, w_out)


def _ffn_bwd(dx, saved, g, src, l, pre):
    tag = f"l{l}_{pre}"
    x, h, ab, u, w_in, w_out = saved
    g_out = _mm_tn(u, dx, nb=1, name=tag + "_bwd_wout", tm=1024, tk=1408, tn=D_MODEL, alpha=0.5, rider=src.ride(tag + "_bwd_wout"))
    src.grads(l, {pre + "_w_out": g_out.reshape(N_CHIPS, D_FF // N_CHIPS, D_MODEL)})
    dab = _ffn_bwd_du_act(dx, w_out, ab, name=tag + "_bwd_du_act", rider=src.ride(tag + "_bwd_du_act"))
    g_in = _mm_tn(h, dab, nb=N_CHIPS, name=tag + "_bwd_win", tm=2048, tk=D_MODEL, tn=FF_SHARD, rider=src.ride(tag + "_bwd_win"))
    src.grads(l, {pre + "_w_in": g_in})
    return _mm_nt(dab, w_in, name=tag + "_bwd_dh", tm=1024, tp=D_MODEL, tn=FF_SHARD, out_dtype=F32, rider=src.ride(tag + "_bwd_dh"),
                  norm=(x, g, dx))


def _mix_fwd(x, small, lb, cos, sin, src, l):
    tag = f"l{l}_mix"
    w = {}
    h = _norm_fwd(x, small["mix_norm"], name=tag + "_norm")
    w["w_in"] = src.weight(l, "w_in")
    proj = _mm_nn(h, w["w_in"], name=tag + "_in", tm=2048, tn=896, out_dtype=BF16, rider=src.ride(tag + "_in"))
    hf = _mm_nn(h, w["w_in"][0:1, :, D_MODEL:2 * D_MODEL], name=tag + "_hf", tm=1024, tn=D_MODEL, out_dtype=F32)
    oscan, oa, sall = _hgrn_fwd(proj, hf, lb, small["hgrn_out_norm"], name=tag + "_hgrn", rider=src.ride(tag + "_hgrn"))
    qk = _qk_fwd(proj, cos, sin, small["attn_q_norm"], small["attn_k_norm"], name=tag + "_qk")
    outs, lses = [], []
    for g in range(ATT_GROUPS):
        o, lse = _attn_fwd(qk[g], qk[3 + g], qk[6 + g], g, name=f"{tag}_attn{g}")
        outs.append(o)
        lses.append(lse)
    ob = _merge_fwd(outs, lses, name=tag + "_merge")
    w.update({n: src.weight(l, n) for n in ("w_branch_a", "w_branch_b", "w_out")})
    y, merged, ya, yb = _mix_tail_fwd(oa, ob, proj, x, w["w_branch_a"], w["w_branch_b"], w["w_out"], name=tag + "_tail")
    return y, (x, h, proj, hf, oscan, oa, sall, qk, outs, lses, ob, ya, yb, merged, w)


def _mix_bwd(dx, saved, small, lb, cos, sin, src, l, lb_live):
    tag = f"l{l}_mix"
    x, h, proj, hf, oscan, oa, sall, qk, outs, lses, ob, ya, yb, merged, w = saved
    g_wout = _mm_tn(merged, dx, nb=1, name=tag + "_bwd_wout", tm=1024, tk=D_MODEL, tn=D_MODEL)
    dya, dyb, dgab, doa, dob = _mix_tail_bwd(dx, proj, ya, yb, w["w_branch_a"], w["w_branch_b"], w["w_out"], name=tag + "_bwd_tail")
    g_wa = _mm_tn(oa, dya, nb=1, name=tag + "_bwd_wa", tm=1024, tk=D_MODEL, tn=D_MODEL)
    g_wb = _mm_tn(ob, dyb, nb=N_CHIPS, name=tag + "_bwd_wb", tm=2048, tk=ATT_GW, tn=256)
    mb = _merge_bwd(dob, outs, lses, name=tag + "_bwd_merge")
    dqk, dvs = [None] * 6, []
    for g in range(ATT_GROUPS):
        dq, dk, dv = _attn_bwd(qk[g], qk[3 + g], qk[6 + g], mb[g], lses[g], mb[3 + g], g, name=f"{tag}_bwd_attn{g}")
        dqk[g], dqk[3 + g] = dq, dk
        dvs.append(dv)
    dqk_cols, dqn, dkn = _qk_bwd(dqk, proj, cos, sin, small["attn_q_norm"], small["attn_k_norm"], name=tag + "_bwd_qk")
    dproj, dgn, dlb = _hgrn_bwd(doa, oscan, proj, hf, sall, lb, small["hgrn_out_norm"], dqk_cols, dvs, dgab,
                                name=tag + "_bwd_hgrn", precise=lb_live, rider=src.ride(tag + "_bwd_hgrn"))
    src.grads(l, dict(w_branch_a=g_wa.reshape(N_CHIPS, D_MODEL // N_CHIPS, D_MODEL), w_branch_b=g_wb,
                      w_out=g_wout.reshape(N_CHIPS, D_MODEL // N_CHIPS, D_MODEL)))
    g_win = _mm_tn(h, dproj, nb=N_CHIPS, name=tag + "_bwd_win", tm=1024, tk=D_MODEL, tn=2688, rider=src.ride(tag + "_bwd_win"))
    src.grads(l, dict(w_in=g_win))
    dx, dg = _mm_nt(dproj, w["w_in"], name=tag + "_bwd_dh", tm=1024, tp=D_MODEL, tn=2688, out_dtype=F32,
                    rider=src.ride(tag + "_bwd_dh"), norm=(x, small["mix_norm"], dx))
    return dx, dict(mix_norm=dg, hgrn_out_norm=dgn, lb=dlb, attn_q_norm=dqn, attn_k_norm=dkn)


BIG = ("ffn1_w_in", "ffn1_w_out", "w_in", "w_branch_a", "w_branch_b", "w_out", "ffn2_w_in", "ffn2_w_out")
ROW_SHARDED = ("ffn1_w_out", "w_branch_a", "w_out", "ffn2_w_out")
SMALL = ("ffn1_norm", "mix_norm", "hgrn_lb_logits", "hgrn_out_norm", "attn_q_norm", "attn_k_norm", "ffn2_norm")
WEIGHTS = ("ffn1_norm", "ffn1_w_in", "ffn1_w_out", "mix_norm", "w_in", "hgrn_lb_logits", "hgrn_out_norm", "attn_q_norm",
           "attn_k_norm", "w_branch_a", "w_branch_b", "w_out", "ffn2_norm", "ffn2_w_in", "ffn2_w_out")
SMALL_ROWS = 8


def _matmul_ready(name, a):
    return a.reshape(1, a.shape[0] * a.shape[1], a.shape[2]) if name in ROW_SHARDED else a


def _layer_small(small, l):
    s = {n: small[n][l].reshape(1, D_MODEL) for n in ("ffn1_norm", "mix_norm", "hgrn_out_norm", "ffn2_norm")}
    s.update({n: small[n][l] for n in ("attn_q_norm", "attn_k_norm")})
    return s


def _local_step(x, target, small, src):
    t = x.shape[0]
    cos, sin = _rope_tables(t)
    lbs = _lower_bounds(small["hgrn_lb_logits"])
    saved = []
    for l in range(2):
        sm = _layer_small(small, l)
        lb = lbs[l].reshape(1, D_MODEL)
        x, s1 = _ffn_fwd(x, sm["ffn1_norm"], src, l, "ffn1")
        x, s2 = _mix_fwd(x, sm, lb, cos, sin, src, l)
        x, s3 = _ffn_fwd(x, sm["ffn2_norm"], src, l, "ffn2")
        saved.append((sm, lb, s1, s2, s3))
    dx, sq = _loss_fwd_bwd(x, target, name="loss")
    small_rows = [None, None]
    for l in (1, 0):
        sm, lb, s1, s2, s3 = saved[l]
        dx, dg2 = _ffn_bwd(dx, s3, sm["ffn2_norm"], src, l, "ffn2")
        dx, g = _mix_bwd(dx, s2, sm, lb, cos, sin, src, l, lb_live=l > 0)
        dx, dg1 = _ffn_bwd(dx, s1, sm["ffn1_norm"], src, l, "ffn1")
        pad = lambda a: jnp.pad(a[:ATT_GROUPS].reshape(1, ATT_GROUPS * HEAD), ((0, 0), (0, D_MODEL - ATT_GROUPS * HEAD)))
        small_rows[l] = jnp.concatenate(
            [dg1, g["mix_norm"], g["lb"], g["hgrn_out_norm"], pad(g["attn_q_norm"]), pad(g["attn_k_norm"]), dg2,
             jnp.zeros((SMALL_ROWS - 7, D_MODEL), F32)], axis=0)
    return jnp.sum(sq), dx, jnp.concatenate(small_rows, axis=0)


def _coords():
    return lax.axis_index("x"), lax.axis_index("y"), lax.axis_index("c")


def _other_chips(x, y):
    return [(1 - x, y), (x, 1 - y), (1 - x, 1 - y)]


def _half_rows(rows, which):
    return pl.ds(which * (rows // 2), rows // 2)


def _gather_rider(shards):
    n = len(shards)

    def copies(w, full, sems):
        send, recv, fsend, frecv, osend, orecv = sems
        x, y, c = _coords()
        slot = 2 * x + y
        chips = _other_chips(x, y)

        def copy(i, j, blk, src, pair, to):
            return pltpu.make_async_remote_copy(src_ref=src, dst_ref=blk, send_sem=pair[0].at[i * 3 + j],
                                                recv_sem=pair[1].at[i * 3 + j], device_id=to, device_id_type=MESH)

        def block(i, chip_slot, core):
            return full[i].at[chip_slot, _half_rows(shards[i].shape[0], core)]

        pairs = [(i, j, chip) for i in range(n) for j, chip in enumerate(chips)]

        def first():
            return [copy(i, j, block(i, slot, c), w[i].at[_half_rows(shards[i].shape[0], c)], (send, recv), (*chip, c))
                    for i, j, chip in pairs]

        def landed(core, pair):
            return [copy(i, j, block(i, 2 * chip[0] + chip[1], core), block(i, 2 * chip[0] + chip[1], core), pair, (x, y, 1 - c))
                    for i, j, chip in pairs]

        def own():
            return [pltpu.make_async_remote_copy(src_ref=w[i], dst_ref=full[i].at[slot], send_sem=osend.at[i],
                                                 recv_sem=orecv.at[i], device_id=(x, y, 1 - c), device_id_type=MESH)
                    for i in range(n)]

        return first, landed, own

    def begin(w, full, sems):
        first, _, own = copies(w, full, sems)
        for cp in first() + own():
            cp.start()

    def end(w, full, sems):
        first, landed, own = copies(w, full, sems)
        forwards = landed(lax.axis_index("c"), sems[2:4])
        for arrival, forward in zip(landed(lax.axis_index("c"), sems[:2]), forwards):
            arrival.wait_recv()
            forward.start()
        for cp in landed(1 - lax.axis_index("c"), sems[2:4]) + own():
            cp.wait_recv()
        for cp in first() + forwards + own():
            cp.wait_send()

    out_shape = [jax.ShapeDtypeStruct((N_CHIPS,) + s.shape, s.dtype) for s in shards]
    sems = [pltpu.SemaphoreType.DMA((3 * n,))] * 4 + [pltpu.SemaphoreType.DMA((n,))] * 2
    return _Rider(shards, out_shape, sems, begin, end)


N_RECV = 7


def _scatter_rider(parts):
    n = len(parts)

    def copies(p, out, sems):
        send, recv = sems
        x, y, c = _coords()
        slot = 2 * x + y
        chips = _other_chips(x, y)

        def arrivals():
            return [pltpu.make_async_remote_copy(
                src_ref=out[i].at[k], dst_ref=out[i].at[k], send_sem=send.at[0], recv_sem=recv.at[i * N_RECV + k],
                device_id=(x, y, c), device_id_type=MESH) for i in range(n) for k in range(N_RECV)]

        sends = []
        for i in range(n):
            rows = parts[i].shape[1]
            for j, chip in enumerate(chips):
                for core in (0, 1):
                    sends.append(pltpu.make_async_remote_copy(
                        src_ref=p[i].at[2 * chip[0] + chip[1], _half_rows(rows, core)], dst_ref=out[i].at[2 * j + c],
                        send_sem=send.at[i * N_RECV + 2 * j + core], recv_sem=recv.at[i * N_RECV + 2 * j + c],
                        device_id=(*chip, core), device_id_type=MESH))
            sends.append(pltpu.make_async_remote_copy(
                src_ref=p[i].at[slot, _half_rows(rows, 1 - c)], dst_ref=out[i].at[6], send_sem=send.at[i * N_RECV + 6],
                recv_sem=recv.at[i * N_RECV + 6], device_id=(x, y, 1 - c), device_id_type=MESH))
        return sends, arrivals

    def begin(p, out, sems):
        for cp in copies(p, out, sems)[0]:
            cp.start()

    def end(p, out, sems):
        sends, arrivals = copies(p, out, sems)
        for cp in arrivals():
            cp.wait_recv()
        for cp in sends:
            cp.wait_send()

    out_shape = [jax.ShapeDtypeStruct((N_RECV, a.shape[1] // 2, a.shape[2]), a.dtype) for a in parts]
    return _Rider(parts, out_shape, [pltpu.SemaphoreType.DMA((N_RECV * n,))] * 2, begin, end)


def _run_alone(rider, name):
    _pcall(lambda: None, grid=(), in_specs=[], out_specs=[], out_shape=[], name=name, sem=(), args=(), rider=rider)
    return rider.result


def _sum_partials(own, parts, name):
    r, wd = own.shape
    tm = next(t for t in (256, 128, 64, 32, 16) if r % t == 0)

    def body(own_ref, p_ref, o_ref):
        acc = own_ref[...].astype(F32)
        for k in range(N_RECV):
            acc = acc + p_ref[k].astype(F32)
        o_ref[...] = acc

    return pl.pallas_call(
        body, grid=(r // tm,),
        in_specs=[pl.BlockSpec((tm, wd), lambda i: (i, 0)), pl.BlockSpec((N_RECV, tm, wd), lambda i: (0, i, 0))],
        out_specs=pl.BlockSpec((tm, wd), lambda i: (i, 0)), out_shape=jax.ShapeDtypeStruct((r, wd), F32),
        name=name, compiler_params=_params(("parallel",)))(own, parts)


def _exchange_halves(reduced, name):
    n = len(reduced)

    def body(*refs):
        r, out = refs[:n], refs[n:2 * n]
        send, recv = refs[2 * n:]
        x, y, c = _coords()
        sib = [pltpu.make_async_remote_copy(src_ref=r[i], dst_ref=out[i], send_sem=send.at[i], recv_sem=recv.at[i],
                                            device_id=(x, y, 1 - c), device_id_type=MESH) for i in range(n)]
        for cp in sib:
            cp.start()
        for cp in sib:
            cp.wait_recv()
        for cp in sib:
            cp.wait_send()

    out_shape = [jax.ShapeDtypeStruct(a.shape, a.dtype) for a in reduced]
    return pl.pallas_call(body, in_specs=[ANY] * n, out_specs=[ANY] * n, out_shape=out_shape,
                          scratch_shapes=[pltpu.SemaphoreType.DMA((n,))] * 2, name=name)(*reduced)


def _reduce_finish(parts, recv, tag):
    x, y, c = _coords()
    slot = 2 * x + y
    halves = []
    for i, (p, r) in enumerate(zip(parts, recv)):
        half = p.shape[1] // 2
        own = lax.dynamic_slice(p, (slot, c * half, 0), (1, half, p.shape[2]))[0]
        halves.append(_sum_partials(own, r, name=f"{tag}_sum{i}"))
    theirs = _exchange_halves(halves, name=tag + "_exchange")
    return [jnp.where(c == 0, jnp.concatenate([h, t], axis=0), jnp.concatenate([t, h], axis=0)) for h, t in zip(halves, theirs)]


GATHER_RIDES = {
    "l0_ffn1_in_act": ((0, "ffn1_w_out"), (0, "w_in")),
    "l0_ffn1_out": ((0, "w_branch_a"), (0, "w_branch_b"), (0, "w_out")),
    "l0_mix_in": ((0, "ffn2_w_in"), (0, "ffn2_w_out"), (1, "ffn1_w_in"), (1, "ffn1_w_out")),
    "l0_mix_hgrn": ((1, "w_in"), (1, "w_branch_a"), (1, "w_branch_b"), (1, "w_out")),
    "l0_ffn2_in_act": ((1, "ffn2_w_in"), (1, "ffn2_w_out")),
}
ALONE_FIRST = ((0, "ffn1_w_in"),)
SCATTER_RIDES = {
    "l1_mix_bwd_hgrn": ((1, "ffn2_w_in"), (1, "ffn2_w_out")),
    "l0_ffn2_bwd_win": ((1, "ffn1_w_in"),),
    "l0_ffn2_bwd_dh": ((1, "ffn1_w_out"), (1, "w_branch_a"), (1, "w_branch_b"), (1, "w_out")),
    "l0_mix_bwd_hgrn": ((1, "w_in"), (0, "ffn2_w_out")),
    "l0_mix_bwd_win": ((0, "ffn2_w_in"),),
    "l0_mix_bwd_dh": ((0, "w_in"),),
    "l0_ffn1_bwd_wout": ((0, "w_branch_a"), (0, "w_branch_b"), (0, "w_out")),
    "l0_ffn1_bwd_du_act": ((0, "ffn1_w_out"),),
    "l0_ffn1_bwd_dh": ((0, "ffn1_w_in"),),
}


class _Exchange:
    def __init__(self, shards):
        self.shards = shards
        self.pending = []
        self.full = {}
        self.parts = {}
        self.recv = {}

    def _gather(self, keys):
        return _gather_rider([self.shards[n][l] for l, n in keys]), "gather", list(keys)

    def _scatter(self, keys):
        return _scatter_rider([self.parts[k] for k in keys]), "scatter", list(keys)

    def _unpack(self):
        waiting = []
        for rider, kind, keys in self.pending:
            if rider.result is None:
                waiting.append((rider, kind, keys))
            elif kind == "gather":
                self.full.update(zip(keys, rider.result))
            else:
                self.recv.update(zip(keys, rider.result))
        self.pending = waiting

    def ride(self, host):
        if host in GATHER_RIDES:
            self.pending.append(self._gather(GATHER_RIDES[host]))
        elif host in SCATTER_RIDES:
            self.pending.append(self._scatter(SCATTER_RIDES[host]))
        else:
            return None
        return self.pending[-1][0]

    def weight(self, l, name):
        self._unpack()
        if (l, name) not in self.full:
            assert (l, name) in ALONE_FIRST, (l, name)
            job = self._gather(ALONE_FIRST)
            _run_alone(job[0], name="gather_first")
            self.pending.append(job)
            self._unpack()
        return _matmul_ready(name, self.full[(l, name)])

    def grads(self, l, partials):
        self.parts.update({(l, n): a for n, a in partials.items()})

    def reduce(self):
        self._unpack()
        assert not self.pending and set(self.recv) == set(self.parts)
        out = {}
        for l in range(2):
            done = _reduce_finish([self.parts[(l, n)] for n in BIG], [self.recv[(l, n)] for n in BIG], f"reduce_l{l}")
            out[l] = dict(zip(BIG, done))
        return {n: jnp.stack([out[0][n], out[1][n]], axis=0) for n in BIG}


def _all_reduce_small(rows):
    r = rows.shape[0]

    def body(x_ref, o_ref, buf, send, recv):
        x, y, c = _coords()
        me = 4 * x + 2 * y + c
        buf[me] = x_ref[...]
        copies = []
        for k in range(1, 8):
            peer = (x ^ (k >> 2), y ^ ((k >> 1) & 1), c ^ (k & 1))
            cp = pltpu.make_async_remote_copy(src_ref=x_ref, dst_ref=buf.at[me], send_sem=send.at[k - 1], recv_sem=recv.at[me],
                                              device_id=peer, device_id_type=MESH)
            cp.start()
            copies.append(cp)
        for k in range(1, 8):
            src = 4 * (x ^ (k >> 2)) + 2 * (y ^ ((k >> 1) & 1)) + (c ^ (k & 1))
            pltpu.make_async_remote_copy(src_ref=x_ref, dst_ref=buf.at[src], send_sem=send.at[0], recv_sem=recv.at[src],
                                         device_id=(x, y, c), device_id_type=MESH).wait_recv()
        for cp in copies:
            cp.wait_send()
        acc = buf[0]
        for k in range(1, 8):
            acc = acc + buf[k]
        o_ref[...] = acc

    vm = pl.BlockSpec(memory_space=pltpu.VMEM)
    return pl.pallas_call(
        body, in_specs=[vm], out_specs=vm, out_shape=jax.ShapeDtypeStruct(rows.shape, F32),
        scratch_shapes=[pltpu.VMEM((8, r, D_MODEL), F32), pltpu.SemaphoreType.DMA((7,)), pltpu.SemaphoreType.DMA((8,))],
        name="all_reduce_small")(rows)


def _adamw_math(w, g, m, v):
    m = ADAM_B1 * m + (1.0 - ADAM_B1) * g
    v = ADAM_B2 * v + (1.0 - ADAM_B2) * (g * g)
    m_hat = m / (1.0 - ADAM_B1 ** ADAM_STEP)
    v_hat = v / (1.0 - ADAM_B2 ** ADAM_STEP)
    return -ADAM_LR * (m_hat / (jnp.sqrt(v_hat) + ADAM_EPS) + ADAM_WD * w), m, v


def _adamw(w, g, m, v, name):
    shape = w.shape
    cols = shape[-1]
    flat = lambda a: a.reshape(-1, cols)
    rows = flat(w).shape[0]
    tm = 128 if rows % 128 == 0 else rows
    ins = [('t', flat(a), cols, 0) for a in (w, g, m, v)]
    res = _ew(_adamw_math, ins, [('t', cols, F32)] * 3, rows=rows, tm=tm, name=name)
    return [a.reshape(shape) for a in res]


def _small_update(sums, logits, w, m, v):
    def body(s_ref, lg_ref, w_ref, m_ref, v_ref, g_ref, d_ref, nm_ref, nv_ref):
        s = s_ref[...]
        l0, l1 = lg_ref[0:1, :], lg_ref[1:2, :]
        mx = jnp.maximum(l0, l1)
        e0, e1 = jnp.exp(l0 - mx), jnp.exp(l1 - mx)
        sm0, sm1 = e0 / (e0 + e1), e1 / (e0 + e1)
        dl1 = s_ref[SMALL_ROWS + 2:SMALL_ROWS + 3, :] * sm0 * sm1
        row = lax.broadcasted_iota(jnp.int32, s.shape, 0)
        g = jnp.where(row == 2, -dl1, jnp.where(row == SMALL_ROWS + 2, dl1, s))
        d, nm, nv = _adamw_math(w_ref[...], g, m_ref[...], v_ref[...])
        g_ref[...] = g
        d_ref[...] = d
        nm_ref[...] = nm
        nv_ref[...] = nv

    vm = pl.BlockSpec(memory_space=pltpu.VMEM)
    return pl.pallas_call(body, in_specs=[vm] * 5, out_specs=[vm] * 4,
                          out_shape=[jax.ShapeDtypeStruct(sums.shape, F32)] * 4, name="small_update")(sums, logits, w, m, v)


def _pack_small(vals):
    rows = []
    for l in range(2):
        for n in ("ffn1_norm", "mix_norm", "hgrn_lb_logits", "hgrn_out_norm", "attn_q_norm", "attn_k_norm", "ffn2_norm"):
            a = vals[n][l].reshape(1, -1)
            rows.append(jnp.pad(a, ((0, 0), (0, D_MODEL - a.shape[1]))))
        rows.append(jnp.zeros((SMALL_ROWS - 7, D_MODEL), F32))
    return jnp.concatenate(rows, axis=0)


def _unpack_small(packed):
    out = {}
    for k, n in enumerate(("ffn1_norm", "mix_norm", "hgrn_lb_logits", "hgrn_out_norm", "attn_q_norm", "attn_k_norm", "ffn2_norm")):
        a = jnp.stack([packed[k], packed[SMALL_ROWS + k]], axis=0)
        out[n] = a[:, :ATT_GROUPS * HEAD].reshape(2, ATT_GROUPS, HEAD) if n.startswith("attn") else a
    return out


def kernel(x, ffn1_norm, ffn1_w_in, ffn1_w_out, mix_norm, w_in, hgrn_lb_logits, hgrn_out_norm, attn_q_norm, attn_k_norm, w_branch_a, w_branch_b, w_out, ffn2_norm, ffn2_w_in, ffn2_w_out, loss_target, m_ffn1_norm, m_ffn1_w_in, m_ffn1_w_out, m_mix_norm, m_w_in, m_hgrn_lb_logits, m_hgrn_out_norm, m_attn_q_norm, m_attn_k_norm, m_w_branch_a, m_w_branch_b, m_w_out, m_ffn2_norm, m_ffn2_w_in, m_ffn2_w_out, v_ffn1_norm, v_ffn1_w_in, v_ffn1_w_out, v_mix_norm, v_w_in, v_hgrn_lb_logits, v_hgrn_out_norm, v_attn_q_norm, v_attn_k_norm, v_w_branch_a, v_w_branch_b, v_w_out, v_ffn2_norm, v_ffn2_w_in, v_ffn2_w_out):
    a = locals()
    w = {n: a[n] for n in WEIGHTS}
    m = {n: a["m_" + n] for n in WEIGHTS}
    v = {n: a["v_" + n] for n in WEIGHTS}

    exchange = _Exchange({n: w[n].astype(BF16) for n in BIG})
    small = {n: w[n] for n in SMALL}
    sq, grad_x, small_rows = _local_step(x[0], loss_target[0], small, exchange)
    loss = lax.psum(sq, ("x", "y", "c")) * (0.5 / D_MODEL)
    grads = exchange.reduce()

    sums = _all_reduce_small(small_rows)
    g_s, d_s, m_s, v_s = _small_update(sums, w["hgrn_lb_logits"], _pack_small(small), _pack_small({n: m[n] for n in SMALL}),
                                       _pack_small({n: v[n] for n in SMALL}))
    grads.update(_unpack_small(g_s))
    delta, new_m, new_v = _unpack_small(d_s), _unpack_small(m_s), _unpack_small(v_s)
    for n in BIG:
        delta[n], new_m[n], new_v[n] = _adamw(w[n], grads[n], m[n], v[n], name="adamw_" + n)

    return (loss, grad_x[None], *[grads[n] for n in WEIGHTS], *[delta[n] for n in WEIGHTS],
            *[new_m[n] for n in WEIGHTS], *[new_v[n] for n in WEIGHTS])
```
